```python
import math
import jax
import jax.numpy as jnp
from jax import lax
import numpy as np

D_MODEL = 1024
BATCH = 4
SEQ = 4096
DEPTH = 2
DEC_BATCH = 128
DEC_SEQ = 1
PAST_LEN = 16384
PAGE_SIZE = 128

N_BRANCH = 4
BRANCH_W = D_MODEL // N_BRANCH
GLA_HEADS = 4
GLA_DK = BRANCH_W // (2 * GLA_HEADS)
GLA_DV = BRANCH_W // GLA_HEADS
GLA_RANK = 16
GLA_TAU = 16.0
SWA_HEADS = 4
SWA_KV_HEADS = 2
SWA_HD = BRANCH_W // SWA_HEADS
WINDOW = 128
ML_HEADS = 4
ML_HD = BRANCH_W // ML_HEADS
RET_HEADS = 4
RET_HD = BRANCH_W // RET_HEADS
ROPE_BASE = 10000.0
CHUNK = 64
N_GROUPS = 4
EXPERTS_PER_GROUP = 8
N_EXPERTS = N_GROUPS * EXPERTS_PER_GROUP
TOP_K = 2
D_EXPERT = D_MODEL // 4
MOE_BLOCK = 64
ALPHA = (2 * DEPTH) ** 0.25
BETA = (8 * DEPTH) ** -0.25
LN_EPS = 1e-5
IN_SIZES = (GLA_HEADS * GLA_DK, GLA_HEADS * GLA_DK, BRANCH_W, BRANCH_W, GLA_RANK,
            BRANCH_W, SWA_KV_HEADS * SWA_HD, SWA_KV_HEADS * SWA_HD,
            BRANCH_W, BRANCH_W, BRANCH_W, ML_HEADS, ML_HEADS, BRANCH_W,
            BRANCH_W, BRANCH_W, BRANCH_W, BRANCH_W,
            N_BRANCH * D_MODEL)
IN_WIDTH = sum(IN_SIZES)
ML_F_COL = sum(IN_SIZES[:12])

kernel_name = 'hybrid_parallel_gated_decoder_step'


def _layer_norm(x, g, b):
    xf = x.astype(jnp.float32)
    mu = jnp.mean(xf, axis=-1, keepdims=True)
    var = jnp.mean(jnp.square(xf - mu), axis=-1, keepdims=True)
    y = (xf - mu) * lax.rsqrt(var + LN_EPS) * g.astype(jnp.float32) + b.astype(jnp.float32)
    return y.astype(x.dtype)


def _head_norm(o, g):
    mu = jnp.mean(o, axis=-1, keepdims=True)
    var = jnp.mean(jnp.square(o - mu), axis=-1, keepdims=True)
    y = (o - mu) * lax.rsqrt(var + LN_EPS)
    return y.reshape(o.shape[0], o.shape[1], -1) * g.astype(jnp.float32)


def _rotary(x, pos):
    half = x.shape[-1] // 2
    freq = ROPE_BASE ** (-jnp.arange(half, dtype=jnp.float32) / half)
    ang = pos.astype(jnp.float32)[:, None] * freq[None, :]
    cos = jnp.cos(ang)[None, :, None, :]
    sin = jnp.sin(ang)[None, :, None, :]
    x1 = x[..., :half].astype(jnp.float32)
    x2 = x[..., half:].astype(jnp.float32)
    return jnp.concatenate([x1 * cos - x2 * sin, x1 * sin + x2 * cos], axis=-1)


def _to_chunks(t, c):
    b, l = t.shape[:2]
    return jnp.swapaxes(t.reshape(b, l // c, c, *t.shape[2:]), 0, 1)


def _from_chunks(t):
    n, b, c = t.shape[:3]
    return jnp.swapaxes(t, 0, 1).reshape(b, n * c, *t.shape[3:])


def _gla_chunked(q, k, v, log_a, s0):
    c = math.gcd(q.shape[1], CHUNK)
    causal = jnp.tril(jnp.ones((c, c), dtype=bool))[None, :, :, None, None]

    def step(s, inp):
        qc, kc, vc, gc = inp
        b = jnp.cumsum(gc, axis=1)
        o = jnp.einsum('bthk,bhkv->bthv', qc * jnp.exp(b), s)
        rel = jnp.exp(jnp.where(causal, b[:, :, None] - b[:, None, :], -jnp.inf))
        att = jnp.sum(qc[:, :, None] * kc[:, None, :] * rel, axis=-1)
        o = o + jnp.einsum('btsh,bshv->bthv', att, vc)
        b_last = b[:, -1]
        s_new = jnp.exp(b_last)[..., None] * s + jnp.einsum('bshk,bshv->bhkv', kc * jnp.exp(b_last[:, None] - b), vc)
        return s_new, o

    s_fin, o = lax.scan(step, s0, (_to_chunks(q, c), _to_chunks(k, c), _to_chunks(v, c), _to_chunks(log_a, c)))
    return _from_chunks(o), s_fin


def _retention_chunked(q, k, v, s0):
    c = math.gcd(q.shape[1], CHUNK)
    log_g = jnp.log1p(-jnp.exp2(-5.0 - jnp.arange(RET_HEADS, dtype=jnp.float32)))
    idx = jnp.arange(c, dtype=jnp.float32)
    rel = idx[:, None] - idx[None, :]
    dmat = jnp.exp(jnp.where(rel[None] >= 0, rel[None] * log_g[:, None, None], -jnp.inf))
    q_decay = jnp.exp((idx + 1.0)[:, None] * log_g[None, :])
    k_decay = jnp.exp((c - 1.0 - idx)[:, None] * log_g[None, :])
    chunk_decay = jnp.exp(c * log_g)

    def step(s, inp):
        qc, kc, vc = inp
        o = jnp.einsum('bthk,bhkv->bthv', qc * q_decay[None, :, :, None], s)
        att = jnp.einsum('bthk,bshk->bhts', qc, kc) * dmat[None]
        o = o + jnp.einsum('bhts,bshv->bthv', att, vc)
        s_new = chunk_decay[None, :, None, None] * s + jnp.einsum('bshk,bshv->bhkv', kc * k_decay[None, :, :, None], vc)
        return s_new, o

    s_fin, o = lax.scan(step, s0, (_to_chunks(q, c), _to_chunks(k, c), _to_chunks(v, c)))
    return _from_chunks(o), s_fin


def _mlstm_chunked(q, k, v, i_pre, log_f, c0, n0, m0):
    c = math.gcd(q.shape[1], CHUNK)
    causal = jnp.tril(jnp.ones((c, c), dtype=bool))[None, :, :, None]

    def step(carry, inp):
        cm, nm, mm = carry
        qc, kc, vc, ic, fc = inp
        b = jnp.cumsum(fc, axis=1)
        d_log = jnp.where(causal, b[:, :, None] - b[:, None, :] + ic[:, None, :], -jnp.inf)
        inter_log = b + mm[:, None]
        m_t = jnp.maximum(inter_log, jnp.max(d_log, axis=2))
        w_intra = jnp.exp(d_log - m_t[:, :, None])
        w_inter = jnp.exp(inter_log - m_t)
        qk = jnp.einsum('bthd,bshd->btsh', qc, kc) * w_intra
        num = w_inter[..., None] * jnp.einsum('bthd,bhde->bthe', qc, cm) + jnp.einsum('btsh,bshe->bthe', qk, vc)
        den = w_inter * jnp.einsum('bthd,bhd->bth', qc, nm) + jnp.sum(qk, axis=2)
        h = num / jnp.maximum(jnp.abs(den), jnp.exp(-m_t))[..., None]
        m_new = m_t[:, -1]
        carry_w = jnp.exp(b[:, -1] + mm - m_new)
        w_s = jnp.exp(b[:, -1:] - b + ic - m_new[:, None])
        c_new = carry_w[..., None, None] * cm + jnp.einsum('bshd,bshe->bhde', kc * w_s[..., None], vc)
        n_new = carry_w[..., None] * nm + jnp.einsum('bsh,bshd->bhd', w_s, kc)
        return (c_new, n_new, m_new), h

    (c_fin, n_fin, m_fin), h = lax.scan(
        step, (c0, n0, m0),
        (_to_chunks(q, c), _to_chunks(k, c), _to_chunks(v, c), _to_chunks(i_pre, c), _to_chunks(log_f, c)))
    return _from_chunks(h), c_fin, n_fin, m_fin


def _swa_sink_attention(q, k, v, k_prev, v_prev, start, sinks):
    bsz, l, n_h, d = q.shape
    n_kv = k.shape[2]
    g = n_h // n_kv
    qb_len = math.gcd(l, WINDOW)
    nb = l // qb_len
    k_ext = jnp.concatenate([k_prev.astype(k.dtype), k], axis=1)
    v_ext = jnp.concatenate([v_prev.astype(v.dtype), v], axis=1)
    idx = jnp.arange(nb)[:, None] * qb_len + jnp.arange(WINDOW + qb_len)[None, :]
    kb = k_ext[:, idx].astype(jnp.float32)
    vb = v_ext[:, idx].astype(jnp.float32)
    qb = q.reshape(bsz, nb, qb_len, n_kv, g, d).astype(jnp.float32)
    a = jnp.arange(qb_len)[:, None]
    e = jnp.arange(WINDOW + qb_len)[None, :]
    in_band = (e > a) & (e <= a + WINDOW)
    mask = in_band[None] & (start - WINDOW + idx >= 0)[:, None, :]
    s = jnp.einsum('bnqkgd,bnskd->bnkgqs', qb, kb) * (d ** -0.5)
    s = jnp.where(mask[None, :, None, None], s, -jnp.inf)
    sink = sinks.astype(jnp.float32).reshape(1, 1, n_kv, g, 1, 1)
    mx = jnp.maximum(jnp.max(s, axis=-1, keepdims=True), sink)
    p = jnp.exp(s - mx)
    p = p / (jnp.sum(p, axis=-1, keepdims=True) + jnp.exp(sink - mx))
    o = jnp.einsum('bnkgqs,bnskd->bnqkgd', p, vb)
    return o.reshape(bsz, l, n_h * d), k_ext[:, -WINDOW:], v_ext[:, -WINDOW:]


def _token_mixers(x, pos, start, st, w_in, b_in, gla_w_a2, gla_b_a2, gla_norm_g, swa_sinks,
                  ml_norm_g, ret_norm_g, w_branch, w_out):
    s_gla, k_buf, v_buf, c_ml, n_ml, m_ml, s_ret = st
    bsz, l, _ = x.shape
    dt = x.dtype
    f32 = jnp.float32
    h = x @ w_in + b_in
    offs = np.cumsum(IN_SIZES)[:-1].tolist()
    (gq, gk, gv, gr, ga, sq, sk, sv, mq, mk, mv, mi, mf, mo,
     rq, rk, rv, rg, gates) = jnp.split(h, offs, axis=-1)

    q = gq.reshape(bsz, l, GLA_HEADS, GLA_DK).astype(f32) * (GLA_DK ** -0.5)
    k = gk.reshape(bsz, l, GLA_HEADS, GLA_DK).astype(f32)
    v = gv.reshape(bsz, l, GLA_HEADS, GLA_DV).astype(f32)
    log_a = jax.nn.log_sigmoid((ga @ gla_w_a2 + gla_b_a2).astype(f32)).reshape(bsz, l, GLA_HEADS, GLA_DK) / GLA_TAU
    o, s_gla_new = _gla_chunked(q, k, v, log_a, s_gla.astype(f32))
    y_a = jax.nn.silu(gr.astype(f32)) * _head_norm(o, gla_norm_g)

    y_b, k_buf_new, v_buf_new = _swa_sink_attention(
        sq.reshape(bsz, l, SWA_HEADS, SWA_HD), sk.reshape(bsz, l, SWA_KV_HEADS, SWA_HD),
        sv.reshape(bsz, l, SWA_KV_HEADS, SWA_HD), k_buf, v_buf, start, swa_sinks)

    q = mq.reshape(bsz, l, ML_HEADS, ML_HD).astype(f32)
    k = mk.reshape(bsz, l, ML_HEADS, ML_HD).astype(f32) * (ML_HD ** -0.5)
    v = mv.reshape(bsz, l, ML_HEADS, ML_HD).astype(f32)
    o, c_new, n_new, m_new = _mlstm_chunked(q, k, v, mi.astype(f32), jax.nn.log_sigmoid(mf.astype(f32)),
                                            c_ml.astype(f32), n_ml.astype(f32), m_ml.astype(f32))
    o_gate = jax.nn.sigmoid(mo.astype(f32)).reshape(bsz, l, ML_HEADS, ML_HD)
    y_c = _head_norm(o_gate * o, ml_norm_g)

    q = _rotary(rq.reshape(bsz, l, RET_HEADS, RET_HD), pos)
    k = _rotary(rk.reshape(bsz, l, RET_HEADS, RET_HD), pos) * (RET_HD ** -0.5)
    v = rv.reshape(bsz, l, RET_HEADS, RET_HD).astype(f32)
    o, s_ret_new = _retention_chunked(q, k, v, s_ret.astype(f32))
    y_d = jax.nn.silu(rg.astype(f32)) * _head_norm(o, ret_norm_g)

    gate = jax.nn.sigmoid(gates.astype(f32)).reshape(bsz, l, N_BRANCH, D_MODEL)
    merged = jnp.zeros((bsz, l, D_MODEL), f32)
    for n, y_n in enumerate((y_a, y_b, y_c, y_d)):
        merged = merged + gate[:, :, n] * (y_n.astype(dt) @ w_branch[n]).astype(f32)
    out = merged.astype(dt) @ w_out
    new_st = (s_gla_new.astype(dt), k_buf_new.astype(dt), v_buf_new.astype(dt), c_new.astype(dt),
              n_new.astype(dt), m_new.astype(dt), s_ret_new.astype(dt))
    return out, new_st


def _hier_moe(x2, w_router_grp, b_router_grp, w_router_exp, b_router_exp, w_exp_gate, w_exp_up, w_exp_down):
    t, d = x2.shape
    rows = jnp.arange(t)
    g_logits = (x2 @ w_router_grp + b_router_grp).astype(jnp.float32)
    group = jnp.argmax(g_logits, axis=-1)
    p_group = jax.nn.softmax(g_logits, axis=-1)[rows, group]
    e_logits = (x2 @ w_router_exp + b_router_exp).astype(jnp.float32)
    e_logits = e_logits.reshape(t, N_GROUPS, EXPERTS_PER_GROUP)[rows, group]
    top_val, top_idx = lax.top_k(e_logits, TOP_K)
    gate = p_group[:, None] * jax.nn.softmax(top_val, axis=-1)
    expert = group[:, None] * EXPERTS_PER_GROUP + top_idx
    n_assign = t * TOP_K
    e_flat = expert.reshape(n_assign)
    order = jnp.argsort(e_flat)
    e_sorted = e_flat[order]
    tok_sorted = (jnp.arange(n_assign) // TOP_K)[order]
    gate_sorted = gate.reshape(n_assign)[order]
    counts = jnp.bincount(e_flat, length=N_EXPERTS)
    padded = (counts + MOE_BLOCK - 1) // MOE_BLOCK * MOE_BLOCK
    start = jnp.cumsum(counts) - counts
    pad_end = jnp.cumsum(padded)
    pad_start = pad_end - padded
    dest = pad_start[e_sorted] + jnp.arange(n_assign) - start[e_sorted]
    n_blocks = -(-n_assign // MOE_BLOCK) + N_EXPERTS
    xs = jnp.zeros((n_blocks * MOE_BLOCK, d), x2.dtype).at[dest].set(x2[tok_sorted])
    block_expert = jnp.minimum(
        jnp.searchsorted(pad_end, jnp.arange(n_blocks) * MOE_BLOCK, side='right'), N_EXPERTS - 1)

    def expert_block(args):
        xb, e = args
        hb = jax.nn.silu(xb @ w_exp_gate[e]) * (xb @ w_exp_up[e])
        return hb @ w_exp_down[e]

    ys = lax.map(expert_block, (xs.reshape(n_blocks, MOE_BLOCK, d), block_expert))
    ys = ys.reshape(n_blocks * MOE_BLOCK, d)[dest] * gate_sorted[:, None].astype(x2.dtype)
    return jax.ops.segment_sum(ys, tok_sorted, num_segments=t)


def _decoder_layer(x, pos, start, st, w_in, b_in, gla_w_a2, gla_b_a2, gla_norm_g, swa_sinks, ml_norm_g,
                   ret_norm_g, w_branch, w_out, ln1_g, ln1_b, w_router_grp, b_router_grp, w_router_exp,
                   b_router_exp, w_exp_gate, w_exp_up, w_exp_down, ln2_g, ln2_b):
    bsz, l, d = x.shape
    mix, new_st = _token_mixers(x, pos, start, st, w_in, b_in, gla_w_a2, gla_b_a2, gla_norm_g, swa_sinks,
                                ml_norm_g, ret_norm_g, w_branch, w_out)
    x = _layer_norm(ALPHA * x + mix, ln1_g, ln1_b)
    ffn = _hier_moe(x.reshape(bsz * l, d), w_router_grp, b_router_grp, w_router_exp, b_router_exp,
                    w_exp_gate, w_exp_up, w_exp_down).reshape(bsz, l, d)
    x = _layer_norm(ALPHA * x + ffn, ln2_g, ln2_b)
    return x, new_st


def setup_inputs(seed: int = 0) -> dict:
    key = jax.random.key(seed)
    ks = iter(jax.random.split(key, 40))

    def nrm(shape, scale=1.0):
        return scale * jax.random.normal(next(ks), shape, jnp.float32)

    b_in = nrm((DEPTH, IN_WIDTH), 0.02)
    b_in = b_in.at[:, ML_F_COL:ML_F_COL + ML_HEADS].add(jnp.linspace(3.0, 6.0, ML_HEADS))
    return {
        'x_prompt': nrm((BATCH, SEQ, D_MODEL)),
        'x_sample': nrm((DEC_BATCH, DEC_SEQ, D_MODEL)),
        'state_gla': nrm((DEPTH, DEC_BATCH, GLA_HEADS, GLA_DK, GLA_DV)),
        'cache_swa_k': nrm((DEPTH, DEC_BATCH, WINDOW, SWA_KV_HEADS, SWA_HD)),
        'cache_swa_v': nrm((DEPTH, DEC_BATCH, WINDOW, SWA_KV_HEADS, SWA_HD)),
        'state_mlstm_c': nrm((DEPTH, DEC_BATCH, ML_HEADS, ML_HD, ML_HD), 0.5),
        'state_mlstm_n': nrm((DEPTH, DEC_BATCH, ML_HEADS, ML_HD), 0.5),
        'state_mlstm_m': nrm((DEPTH, DEC_BATCH, ML_HEADS)),
        'state_ret': nrm((DEPTH, DEC_BATCH, RET_HEADS, RET_HD, RET_HD)),
        'w_in': nrm((DEPTH, D_MODEL, IN_WIDTH), D_MODEL ** -0.5),
        'b_in': b_in,
        'gla_w_a2': nrm((DEPTH, GLA_RANK, GLA_HEADS * GLA_DK), GLA_RANK ** -0.5),
        'gla_b_a2': nrm((DEPTH, GLA_HEADS * GLA_DK), 0.02),
        'gla_norm_g': 1.0 + nrm((DEPTH, BRANCH_W), 0.02),
        'swa_sinks': nrm((DEPTH, SWA_HEADS), 0.5),
        'ml_norm_g': 1.0 + nrm((DEPTH, BRANCH_W), 0.02),
        'ret_norm_g': 1.0 + nrm((DEPTH, BRANCH_W), 0.02),
        'w_branch': nrm((DEPTH, N_BRANCH, BRANCH_W, D_MODEL), BRANCH_W ** -0.5),
        'w_out': nrm((DEPTH, D_MODEL, D_MODEL), BETA * D_MODEL ** -0.5),
        'ln1_g': 1.0 + nrm((DEPTH, D_MODEL), 0.02),
        'ln1_b': nrm((DEPTH, D_MODEL), 0.02),
        'w_router_grp': nrm((DEPTH, D_MODEL, N_GROUPS), D_MODEL ** -0.5),
        'b_router_grp': nrm((DEPTH, N_GROUPS), 0.01),
        'w_router_exp': nrm((DEPTH, D_MODEL, N_EXPERTS), D_MODEL ** -0.5),
        'b_router_exp': nrm((DEPTH, N_EXPERTS), 0.01),
        'w_exp_gate': nrm((DEPTH, N_EXPERTS, D_MODEL, D_EXPERT), D_MODEL ** -0.5),
        'w_exp_up': nrm((DEPTH, N_EXPERTS, D_MODEL, D_EXPERT), D_MODEL ** -0.5),
        'w_exp_down': nrm((DEPTH, N_EXPERTS, D_EXPERT, D_MODEL), BETA * D_EXPERT ** -0.5),
        'ln2_g': 1.0 + nrm((DEPTH, D_MODEL), 0.02),
        'ln2_b': nrm((DEPTH, D_MODEL), 0.02),
    }


def _stack(states, i):
    return jnp.stack([s[i] for s in states], axis=0)


def reference(x_prompt, x_sample, state_gla, cache_swa_k, cache_swa_v, state_mlstm_c, state_mlstm_n,
              state_mlstm_m, state_ret, w_in, b_in, gla_w_a2, gla_b_a2, gla_norm_g, swa_sinks, ml_norm_g,
              ret_norm_g, w_branch, w_out, ln1_g, ln1_b, w_router_grp, b_router_grp, w_router_exp,
              b_router_exp, w_exp_gate, w_exp_up, w_exp_down, ln2_g, ln2_b):
    bsz, l, _ = x_prompt.shape
    dt = x_prompt.dtype
    pos_p = jnp.arange(l)
    pos_s = PAST_LEN + jnp.arange(x_sample.shape[1])
    fresh = (jnp.zeros((bsz, GLA_HEADS, GLA_DK, GLA_DV), dt),
             jnp.zeros((bsz, WINDOW, SWA_KV_HEADS, SWA_HD), dt),
             jnp.zeros((bsz, WINDOW, SWA_KV_HEADS, SWA_HD), dt),
             jnp.zeros((bsz, ML_HEADS, ML_HD, ML_HD), dt),
             jnp.zeros((bsz, ML_HEADS, ML_HD), dt),
             jnp.zeros((bsz, ML_HEADS), dt),
             jnp.zeros((bsz, RET_HEADS, RET_HD, RET_HD), dt))
    y_p, y_s = x_prompt, x_sample
    new_p, new_s = [], []
    for li in range(DEPTH):
        lw = (w_in[li], b_in[li], gla_w_a2[li], gla_b_a2[li], gla_norm_g[li], swa_sinks[li], ml_norm_g[li],
              ret_norm_g[li], w_branch[li], w_out[li], ln1_g[li], ln1_b[li], w_router_grp[li], b_router_grp[li],
              w_router_exp[li], b_router_exp[li], w_exp_gate[li], w_exp_up[li], w_exp_down[li], ln2_g[li], ln2_b[li])
        y_p, st_p = _decoder_layer(y_p, pos_p, 0, fresh, *lw)
        st_in = (state_gla[li], cache_swa_k[li], cache_swa_v[li], state_mlstm_c[li], state_mlstm_n[li],
                 state_mlstm_m[li], state_ret[li])
        y_s, st_s = _decoder_layer(y_s, pos_s, PAST_LEN, st_in, *lw)
        new_p.append(st_p)
        new_s.append(st_s)
    return (y_p, y_s,
            _stack(new_p, 0), _stack(new_s, 0),
            _stack(new_p, 1), _stack(new_s, 1),
            _stack(new_p, 2), _stack(new_s, 2),
            _stack(new_p, 3), _stack(new_s, 3),
            _stack(new_p, 4), _stack(new_s, 4),
            _stack(new_p, 5), _stack(new_s, 5),
            _stack(new_p, 6), _stack(new_s, 6))
```

```python
import functools
import math

import jax
import jax.numpy as jnp
from jax import lax
from jax.experimental import pallas as pl
from jax.experimental.pallas import tpu as pltpu

F32 = jnp.float32
BF16 = jnp.bfloat16

D_MODEL = 1024
N_BRANCH = 4
BRANCH_W = D_MODEL // N_BRANCH
N_HEADS = 4
GLA_DK = 32
GLA_RANK = 16
GLA_TAU = 16.0
HD = 64
SWA_KV = 2
WINDOW = 128
ROPE_BASE = 10000.0
PAST_LEN = 16384
N_GROUPS = 4
EPG = 8
N_EXPERTS = N_GROUPS * EPG
D_EXPERT = D_MODEL // 4
DEPTH = 2
ALPHA = (2 * DEPTH) ** 0.25
LN_EPS = 1e-5
GATE_COL = 3352
VMEM_LIMIT = 56 * 1024 * 1024

W_GLA, W_SWA, W_ML, W_RET, W_SMALL = 768, 512, 1024, 1024, 128
W_MIX = W_GLA + W_SWA + W_ML + W_RET + W_SMALL


def _cparams(sem):
    return pltpu.CompilerParams(dimension_semantics=sem, vmem_limit_bytes=VMEM_LIMIT)


def _dot(a, b):
    return jnp.dot(a, b, preferred_element_type=F32)


def _einsum(spec, a, b):
    return jnp.einsum(spec, a, b, preferred_element_type=F32)


def _split2(x):
    hi = x.astype(BF16)
    lo = (x - hi.astype(F32)).astype(BF16)
    return hi, lo


def _split3(x):
    hi = x.astype(BF16)
    r = x - hi.astype(F32)
    mid = r.astype(BF16)
    lo = (r - mid.astype(F32)).astype(BF16)
    return hi, mid, lo


def _dot_exact_lhs(a_exact, b):
    return sum(_dot(a_exact, p) for p in _split3(b))


def _dot_exact_rhs(a, b_exact):
    return sum(_dot(p, b_exact) for p in _split3(a))


def _layer_norm(x, g, b):
    mu = jnp.mean(x, axis=-1, keepdims=True)
    xc = x - mu
    var = jnp.mean(xc * xc, axis=-1, keepdims=True)
    return xc * lax.rsqrt(var + LN_EPS) * g + b


def _inproj_kernel(x_ref, w_ref, b_ref, gla_ref, swa_ref, ml_ref, ret_ref, sm_ref):
    xb = x_ref[...].astype(BF16)
    off = 0
    for ref, width in ((gla_ref, W_GLA), (swa_ref, W_SWA), (ml_ref, W_ML), (ret_ref, W_RET), (sm_ref, W_SMALL)):
        ref[...] = _dot(xb, w_ref[:, off:off + width]) + b_ref[:, off:off + width]
        off += width


def _inproj(x2, wm, bm, tm):
    t = x2.shape[0]
    widths = (W_GLA, W_SWA, W_ML, W_RET, W_SMALL)
    return pl.pallas_call(
        _inproj_kernel,
        grid=(t // tm,),
        in_specs=[pl.BlockSpec((tm, D_MODEL), lambda i: (i, 0)),
                  pl.BlockSpec((D_MODEL, W_MIX), lambda i: (0, 0)),
                  pl.BlockSpec((1, W_MIX), lambda i: (0, 0))],
        out_specs=[pl.BlockSpec((tm, w), lambda i: (i, 0)) for w in widths],
        out_shape=[jax.ShapeDtypeStruct((t, w), F32) for w in widths],
        compiler_params=_cparams(("parallel",)),
        name="inproj",
    )(x2, wm, bm)


def _merge_kernel(x_ref, ya_ref, yb_ref, yc_ref, yd_ref, wg_ref, bg_ref, wbr_ref, wout_ref, g1_ref, b1_ref,
                  wrh_ref, wrl_ref, br_ref, x1_ref, route_ref, cnt_ref, carry_ref, *, tm):
    @pl.when(pl.program_id(0) == 0)
    def _():
        carry_ref[...] = jnp.zeros_like(carry_ref)

    x = x_ref[...]
    xb = x.astype(BF16)
    merged = jnp.zeros((tm, D_MODEL), F32)
    for n, y_ref in enumerate((ya_ref, yb_ref, yc_ref, yd_ref)):
        gate = jax.nn.sigmoid(_dot(xb, wg_ref[:, n * D_MODEL:(n + 1) * D_MODEL]) + bg_ref[:, n * D_MODEL:(n + 1) * D_MODEL])
        merged = merged + gate * _dot(y_ref[...].astype(BF16), wbr_ref[n])
    mix = _dot(merged.astype(BF16), wout_ref[...])
    x1 = _layer_norm(ALPHA * x + mix, g1_ref[...], b1_ref[...])
    x1_ref[...] = x1

    xh, xl = _split2(x1)
    logits = _dot(xh, wrh_ref[...]) + _dot(xh, wrl_ref[...]) + _dot(xl, wrh_ref[...]) + br_ref[...]
    lane = lax.broadcasted_iota(jnp.int32, (tm, 128), 1)
    lanef = lane.astype(F32)
    neg = jnp.float32(-jnp.inf)
    big = jnp.float32(1e9)
    isg = lane < N_GROUPS
    gmax = jnp.max(jnp.where(isg, logits, neg), axis=-1, keepdims=True)
    gidx = jnp.min(jnp.where(isg & (logits == gmax), lanef, big), axis=-1, keepdims=True)
    p_group = 1.0 / jnp.sum(jnp.where(isg, jnp.exp(logits - gmax), 0.0), axis=-1, keepdims=True)
    lo = N_GROUPS + EPG * gidx
    em = (lanef >= lo) & (lanef < lo + EPG)
    v1 = jnp.max(jnp.where(em, logits, neg), axis=-1, keepdims=True)
    i1 = jnp.min(jnp.where(em & (logits == v1), lanef, big), axis=-1, keepdims=True)
    em2 = em & (lanef != i1)
    v2 = jnp.max(jnp.where(em2, logits, neg), axis=-1, keepdims=True)
    i2 = jnp.min(jnp.where(em2 & (logits == v2), lanef, big), axis=-1, keepdims=True)
    e21 = jnp.exp(v2 - v1)
    gt1 = 1.0 / (1.0 + e21)
    gt2 = e21 * gt1

    sel1 = lanef == i1
    sel2 = lanef == i2
    onehot = jnp.where(sel1 | sel2, 1.0, 0.0)
    row = lax.broadcasted_iota(jnp.int32, (tm, tm), 0)
    col = lax.broadcasted_iota(jnp.int32, (tm, tm), 1)
    tri = jnp.where(col < row, 1.0, 0.0).astype(BF16)
    before = _dot(tri, onehot.astype(BF16)) + carry_ref[...]
    rank1 = jnp.sum(jnp.where(sel1, before, 0.0), axis=-1, keepdims=True)
    rank2 = jnp.sum(jnp.where(sel2, before, 0.0), axis=-1, keepdims=True)
    carry_ref[...] = carry_ref[...] + jnp.sum(onehot, axis=0, keepdims=True)
    cnt_ref[...] = carry_ref[...]

    vals = (i1 - N_GROUPS, i2 - N_GROUPS, p_group * gt1, p_group * gt2, rank1, rank2)
    route = jnp.zeros((tm, 128), F32)
    for k, v in enumerate(vals):
        route = jnp.where(lane == k, v, route)
    route_ref[...] = route


def _merge(x2, ys, lw, tm):
    t = x2.shape[0]
    tok = lambda w: pl.BlockSpec((tm, w), lambda i: (i, 0))
    full = lambda shape: pl.BlockSpec(shape, lambda i: (0,) * len(shape))
    return pl.pallas_call(
        functools.partial(_merge_kernel, tm=tm),
        grid=(t // tm,),
        in_specs=[tok(D_MODEL), tok(BRANCH_W), tok(BRANCH_W), tok(BRANCH_W), tok(BRANCH_W),
                  full((D_MODEL, N_BRANCH * D_MODEL)), full((1, N_BRANCH * D_MODEL)),
                  full((N_BRANCH, BRANCH_W, D_MODEL)), full((D_MODEL, D_MODEL)),
                  full((1, D_MODEL)), full((1, D_MODEL)),
                  full((D_MODEL, 128)), full((D_MODEL, 128)), full((1, 128))],
        out_specs=[tok(D_MODEL), tok(128), full((1, 128))],
        out_shape=[jax.ShapeDtypeStruct((t, D_MODEL), F32), jax.ShapeDtypeStruct((t, 128), F32),
                   jax.ShapeDtypeStruct((1, 128), F32)],
        scratch_shapes=[pltpu.VMEM((1, 128), F32)],
        compiler_params=_cparams(("arbitrary",)),
        name="merge",
    )(x2, *ys, lw["wg"], lw["bg"], lw["wbr"], lw["wout"], lw["ln1_g"], lw["ln1_b"], lw["wr_hi"], lw["wr_lo"], lw["br"])


def _row_copy(src_hbm, src_row, dst_vmem, dst_row, sem):
    return pltpu.make_async_copy(src_hbm.at[pl.ds(src_row, 1), :], dst_vmem.at[pl.ds(dst_row, 1), :], sem)


def _expert_kernel(be_ref, nu_ref, src_ref, x_hbm, wg_ref, wu_ref, wd_ref, ys_ref, xbuf, sem, *, bm):
    blk = pl.program_id(0)

    @pl.when(blk < nu_ref[0])
    def _():
        base = blk * bm

        def issue(r, c):
            _row_copy(x_hbm, src_ref[base + r], xbuf, r, sem).start()
            return c

        lax.fori_loop(0, bm, issue, 0)

        def drain(r, c):
            _row_copy(x_hbm, 0, xbuf, r, sem).wait()
            return c

        lax.fori_loop(0, bm, drain, 0)
        xb = xbuf[...].astype(BF16)
        hid = jax.nn.silu(_dot(xb, wg_ref[...])) * _dot(xb, wu_ref[...])
        ys_ref[...] = _dot(hid.astype(BF16), wd_ref[...])

    @pl.when(blk >= nu_ref[0])
    def _():
        ys_ref[...] = jnp.zeros_like(ys_ref)


def _experts(x1, block_expert, n_used, src_tok, lw, bm):
    n_blocks = block_expert.shape[0]
    grid_spec = pltpu.PrefetchScalarGridSpec(
        num_scalar_prefetch=3,
        grid=(n_blocks,),
        in_specs=[pl.BlockSpec(memory_space=pl.ANY),
                  pl.BlockSpec((None, D_MODEL, D_EXPERT), lambda b, be, nu, src: (be[b], 0, 0)),
                  pl.BlockSpec((None, D_MODEL, D_EXPERT), lambda b, be, nu, src: (be[b], 0, 0)),
                  pl.BlockSpec((None, D_EXPERT, D_MODEL), lambda b, be, nu, src: (be[b], 0, 0))],
        out_specs=pl.BlockSpec((bm, D_MODEL), lambda b, be, nu, src: (b, 0)),
        scratch_shapes=[pltpu.VMEM((bm, D_MODEL), F32), pltpu.SemaphoreType.DMA(())],
    )
    return pl.pallas_call(
        functools.partial(_expert_kernel, bm=bm),
        grid_spec=grid_spec,
        out_shape=jax.ShapeDtypeStruct((n_blocks * bm, D_MODEL), F32),
        compiler_params=_cparams(("arbitrary",)),
        name="experts",
    )(block_expert, n_used, src_tok, x1, lw["we_gate"], lw["we_up"], lw["we_down"])


def _combine_kernel(dest_ref, ys_hbm, x1_ref, route_ref, g2_ref, b2_ref, out_ref, buf, sem, *, tm):
    t0 = pl.program_id(0) * tm

    def issue(r, c):
        for j in range(2):
            _row_copy(ys_hbm, dest_ref[2 * (t0 + r) + j], buf.at[j], r, sem).start()
        return c

    lax.fori_loop(0, tm, issue, 0)

    def drain(r, c):
        for j in range(2):
            _row_copy(ys_hbm, 0, buf.at[j], r, sem).wait()
        return c

    lax.fori_loop(0, tm, drain, 0)
    route = route_ref[...]
    ffn = buf[0] * route[:, 2:3] + buf[1] * route[:, 3:4]
    out_ref[...] = _layer_norm(ALPHA * x1_ref[...] + ffn, g2_ref[...], b2_ref[...])


def _combine(ys, x1, route, dest_flat, lw, tm):
    t = x1.shape[0]
    grid_spec = pltpu.PrefetchScalarGridSpec(
        num_scalar_prefetch=1,
        grid=(t // tm,),
        in_specs=[pl.BlockSpec(memory_space=pl.ANY),
                  pl.BlockSpec((tm, D_MODEL), lambda i, d: (i, 0)),
                  pl.BlockSpec((tm, 128), lambda i, d: (i, 0)),
                  pl.BlockSpec((1, D_MODEL), lambda i, d: (0, 0)),
                  pl.BlockSpec((1, D_MODEL), lambda i, d: (0, 0))],
        out_specs=pl.BlockSpec((tm, D_MODEL), lambda i, d: (i, 0)),
        scratch_shapes=[pltpu.VMEM((2, tm, D_MODEL), F32), pltpu.SemaphoreType.DMA(())],
    )
    return pl.pallas_call(
        functools.partial(_combine_kernel, tm=tm),
        grid_spec=grid_spec,
        out_shape=jax.ShapeDtypeStruct((t, D_MODEL), F32),
        compiler_params=_cparams(("arbitrary",)),
        name="combine",
    )(dest_flat, ys, x1, route, lw["ln2_g"], lw["ln2_b"])


def _moe(x1, route, cnt, lw, bm, tm):
    t = x1.shape[0]
    expert = route[:, 0:2].astype(jnp.int32)
    rank = route[:, 4:6].astype(jnp.int32)
    counts = cnt[0, N_GROUPS:N_GROUPS + N_EXPERTS].astype(jnp.int32)
    padded = (counts + bm - 1) // bm * bm
    pad_end = jnp.cumsum(padded)
    pad_start = pad_end - padded
    dest = (pad_start[expert] + rank).reshape(-1)
    n_blocks = -(-(2 * t) // bm) + N_EXPERTS
    tok = jnp.arange(2 * t, dtype=jnp.int32) // 2
    src_tok = jnp.zeros((n_blocks * bm,), jnp.int32).at[dest].set(tok)
    block_expert = jnp.minimum(
        jnp.searchsorted(pad_end, jnp.arange(n_blocks, dtype=jnp.int32) * bm, side="right"), N_EXPERTS - 1
    ).astype(jnp.int32)
    n_used = (pad_end[-1] // bm).astype(jnp.int32).reshape(1)
    ys = _experts(x1, block_expert, n_used, src_tok, lw, bm)
    return _combine(ys, x1, route, dest, lw, tm)


CHUNK = 128


def _dot_nt(a, b):
    return lax.dot_general(a, b, (((1,), (1,)), ((), ())), preferred_element_type=F32)


def _dot_tn(a, b):
    return lax.dot_general(a, b, (((0,), (0,)), ((), ())), preferred_element_type=F32)


def _lane_head(width, group):
    return lax.broadcasted_iota(jnp.int32, (1, width), 1) // group


def _block_diag_mask(rows_head, cols_head):
    return rows_head == cols_head


def _head_norm(o, gain):
    r = lax.broadcasted_iota(jnp.int32, (BRANCH_W, BRANCH_W), 0) // HD
    c = lax.broadcasted_iota(jnp.int32, (BRANCH_W, BRANCH_W), 1) // HD
    avg = jnp.where(r == c, 1.0 / HD, 0.0).astype(BF16)
    mu = _dot_exact_rhs(o, avg)
    xc = o - mu
    var = _dot_exact_rhs(xc * xc, avg)
    return xc * lax.rsqrt(var + LN_EPS) * gain


def _stack_heads(x, head_of_lane):
    return jnp.concatenate([jnp.where(head_of_lane == h, x, 0.0) for h in range(N_HEADS)], axis=0)


def _pick_heads(stacked, rows, head_of_lane):
    out = jnp.where(head_of_lane == 0, stacked[0:rows], 0.0)
    for h in range(1, N_HEADS):
        out = jnp.where(head_of_lane == h, stacked[h * rows:(h + 1) * rows], out)
    return out


def _spread_heads(cols, head_of_lane):
    out = jnp.where(head_of_lane == 0, cols[0], 0.0)
    for h in range(1, N_HEADS):
        out = jnp.where(head_of_lane == h, cols[h], out)
    return out


def _swa_prompt_kernel(q_ref, kv_ref, kvp_ref, sink_ref, y_ref):
    n = pl.program_id(1)
    q = q_ref[...]
    qa, qb = q[:, 0:128], q[:, 128:256]
    kv, kvp = kv_ref[...], kvp_ref[...]
    kext = jnp.concatenate([kvp[:, 0:128], kv[:, 0:128]], axis=0).astype(BF16)
    vext = jnp.concatenate([kvp[:, 128:256], kv[:, 128:256]], axis=0).astype(BF16)
    lo = lax.broadcasted_iota(jnp.int32, (1, 128), 1) < HD
    qs = jnp.concatenate([jnp.where(lo, qa, 0.0), jnp.where(lo, qb, 0.0),
                          jnp.where(lo, 0.0, qa), jnp.where(lo, 0.0, qb)], axis=0).astype(BF16)
    s = _dot_nt(qs, kext) * (HD ** -0.5)
    a = lax.broadcasted_iota(jnp.int32, (WINDOW, 2 * WINDOW), 0)
    e = lax.broadcasted_iota(jnp.int32, (WINDOW, 2 * WINDOW), 1)
    valid = (e > a) & (e <= a + WINDOW) & ((n > 0) | (e >= WINDOW))
    outs = []
    for h in range(N_HEADS):
        sh = jnp.where(valid, s[h * WINDOW:(h + 1) * WINDOW], -jnp.inf)
        sink = sink_ref[:, h:h + 1]
        mx = jnp.maximum(jnp.max(sh, axis=-1, keepdims=True), sink)
        p = jnp.exp(sh - mx)
        p = p / (jnp.sum(p, axis=-1, keepdims=True) + jnp.exp(sink - mx))
        outs.append(_dot(p.astype(BF16), vext))
    y_ref[...] = jnp.concatenate([jnp.where(lo, outs[0], outs[2]), jnp.where(lo, outs[1], outs[3])], axis=1)


def _swa_prompt(swa, lw, bsz, seq):
    nb = seq // WINDOW
    return pl.pallas_call(
        _swa_prompt_kernel,
        grid=(bsz, nb),
        in_specs=[pl.BlockSpec((WINDOW, 256), lambda b, n: (b * nb + n, 0)),
                  pl.BlockSpec((WINDOW, 256), lambda b, n: (b * nb + n, 1)),
                  pl.BlockSpec((WINDOW, 256), lambda b, n: (b * nb + jnp.maximum(n - 1, 0), 1)),
                  pl.BlockSpec((1, 128), lambda b, n: (0, 0))],
        out_specs=pl.BlockSpec((WINDOW, BRANCH_W), lambda b, n: (b * nb + n, 0)),
        out_shape=jax.ShapeDtypeStruct((bsz * seq, BRANCH_W), F32),
        compiler_params=_cparams(("parallel", "parallel")),
        name="swa_prompt",
    )(swa, swa, swa, lw["sinks"])


def _rotate(x, cos, sin):
    x1, x2 = x[:, 0:128], x[:, 128:256]
    return jnp.concatenate([x1 * cos - x2 * sin, x1 * sin + x2 * cos], axis=1)


def _ret_prompt_kernel(r_ref, cos_ref, sin_ref, dmat_ref, qdec_ref, kdec_ref, cdec_ref, g_ref, y_ref, s_out_ref, s_ref):
    n = pl.program_id(1)

    @pl.when(n == 0)
    def _():
        s_ref[...] = jnp.zeros_like(s_ref)

    c = CHUNK
    r = r_ref[...]
    cos, sin = cos_ref[...], sin_ref[...]
    qr = _rotate(r[:, 0:256], cos, sin)
    kr = _rotate(r[:, 256:512], cos, sin) * (HD ** -0.5)
    v = r[:, 512:768].astype(BF16)
    hq = _lane_head(BRANCH_W, 32) % N_HEADS
    hv = _lane_head(BRANCH_W, HD)
    att = _dot_nt(_stack_heads(qr, hq).astype(BF16), kr.astype(BF16)) * dmat_ref[...]
    o = _pick_heads(_dot(att.astype(BF16), v), c, hv)
    s = s_ref[...]
    o = o + _dot((qr * qdec_ref[...]).astype(BF16), s.astype(BF16))
    upd = _dot_tn((kr * kdec_ref[...]).astype(BF16), v)
    rows_head = (lax.broadcasted_iota(jnp.int32, (BRANCH_W, 1), 0) // 32) % N_HEADS
    s_new = cdec_ref[...] * s + jnp.where(rows_head == hv, upd, 0.0)
    s_ref[...] = s_new
    s_out_ref[...] = s_new
    y_ref[...] = jax.nn.silu(r[:, 768:1024]) * _head_norm(o, g_ref[...])


def _ret_tables(seq):
    c = CHUNK
    half = HD // 2
    freq = ROPE_BASE ** (-jnp.arange(half, dtype=F32) / half)
    ang = jnp.arange(seq).astype(F32)[:, None] * freq[None, :]
    cos = jnp.tile(jnp.cos(ang), (1, N_HEADS))
    sin = jnp.tile(jnp.sin(ang), (1, N_HEADS))
    log_g = jnp.log1p(-jnp.exp2(-5.0 - jnp.arange(N_HEADS, dtype=F32)))
    idx = jnp.arange(c, dtype=F32)
    rel = idx[:, None] - idx[None, :]
    dmat = jnp.exp(jnp.where(rel[None] >= 0, rel[None] * log_g[:, None, None], -jnp.inf)).reshape(N_HEADS * c, c)
    lane_head_q = (jnp.arange(BRANCH_W) // 32) % N_HEADS
    qdec = jnp.exp((idx + 1.0)[:, None] * log_g[lane_head_q][None, :])
    kdec = jnp.exp((c - 1.0 - idx)[:, None] * log_g[lane_head_q][None, :])
    cdec = jnp.exp(c * log_g)[jnp.arange(BRANCH_W) // HD][None, :]
    return cos, sin, dmat, qdec, kdec, cdec


def _ret_prompt(ret, lw, bsz, seq):
    c = CHUNK
    nc = seq // c
    cos, sin, dmat, qdec, kdec, cdec = _ret_tables(seq)
    full = lambda shape: pl.BlockSpec(shape, lambda b, n: (0,) * len(shape))
    return pl.pallas_call(
        _ret_prompt_kernel,
        grid=(bsz, nc),
        in_specs=[pl.BlockSpec((c, W_RET), lambda b, n: (b * nc + n, 0)),
                  pl.BlockSpec((c, 128), lambda b, n: (n, 0)), pl.BlockSpec((c, 128), lambda b, n: (n, 0)),
                  full((N_HEADS * c, c)), full((c, BRANCH_W)), full((c, BRANCH_W)), full((1, BRANCH_W)),
                  full((1, BRANCH_W))],
        out_specs=[pl.BlockSpec((c, BRANCH_W), lambda b, n: (b * nc + n, 0)),
                   pl.BlockSpec((None, BRANCH_W, BRANCH_W), lambda b, n: (b, 0, 0))],
        out_shape=[jax.ShapeDtypeStruct((bsz * seq, BRANCH_W), F32),
                   jax.ShapeDtypeStruct((bsz, BRANCH_W, BRANCH_W), F32)],
        scratch_shapes=[pltpu.VMEM((BRANCH_W, BRANCH_W), F32)],
        compiler_params=_cparams(("parallel", "arbitrary")),
        name="ret_prompt",
    )(ret, cos, sin, dmat, qdec, kdec, cdec, lw["ret_g"])


def _ret_state_from_blockdiag(s_bd):
    b = s_bd.shape[0]
    s = s_bd.reshape(b, 2, N_HEADS, 32, N_HEADS, HD)
    s = jnp.stack([s[:, :, h, :, h, :] for h in range(N_HEADS)], axis=1)
    return s.reshape(b, N_HEADS, HD, HD)


ML_I_LANE = GLA_RANK
ML_F_LANE = GLA_RANK + N_HEADS


def _ml_prompt_kernel(m_ref, sm_ref, g_ref, y_ref, c_out_ref, n_out_ref, m_out_ref, c_ref, n_ref, mm_ref):
    nchunk = pl.program_id(1)

    @pl.when(nchunk == 0)
    def _():
        c_ref[...] = jnp.zeros_like(c_ref)
        n_ref[...] = jnp.zeros_like(n_ref)
        mm_ref[...] = jnp.zeros_like(mm_ref)

    c = CHUNK
    x = m_ref[...]
    q = x[:, 0:256]
    ks = x[:, 256:512] * (HD ** -0.5)
    v = x[:, 512:768].astype(BF16)
    sm = sm_ref[...]
    smt = sm.T
    row = lax.broadcasted_iota(jnp.int32, (c, c), 0)
    col = lax.broadcasted_iota(jnp.int32, (c, c), 1)
    causal = col <= row
    tri_lo = jnp.where(causal, 1.0, 0.0).astype(BF16)
    tri_up = jnp.where(row <= col, 1.0, 0.0).astype(BF16)
    bcols = _dot_exact_lhs(tri_lo, jax.nn.log_sigmoid(sm))
    brows = _dot_exact_rhs(jax.nn.log_sigmoid(smt), tri_up)
    hv = _lane_head(BRANCH_W, HD)
    sc = _dot_nt(_stack_heads(q, hv).astype(BF16), ks.astype(BF16))
    mm = mm_ref[...]
    qk, w_inter, m_tok, rowsum, w_s, carry, m_new = [], [], [], [], [], [], []
    for h in range(N_HEADS):
        bcol = bcols[:, ML_F_LANE + h:ML_F_LANE + h + 1]
        brow = brows[ML_F_LANE + h:ML_F_LANE + h + 1, :]
        icol = sm[:, ML_I_LANE + h:ML_I_LANE + h + 1]
        irow = smt[ML_I_LANE + h:ML_I_LANE + h + 1, :]
        m_prev = mm[:, h:h + 1]
        d_log = jnp.where(causal, bcol - brow + irow, -jnp.inf)
        inter_log = bcol + m_prev
        m_t = jnp.maximum(inter_log, jnp.max(d_log, axis=-1, keepdims=True))
        qk_h = sc[h * c:(h + 1) * c] * jnp.exp(d_log - m_t)
        qk.append(qk_h)
        rowsum.append(jnp.sum(qk_h, axis=-1, keepdims=True))
        w_inter.append(jnp.exp(inter_log - m_t))
        m_tok.append(m_t)
        mn = m_t[c - 1:c, :]
        b_last = bcol[c - 1:c, :]
        m_new.append(mn)
        carry.append(jnp.exp(b_last + m_prev - mn))
        w_s.append(jnp.exp(b_last - bcol + icol - mn))
    cm = c_ref[...]
    nrow = n_ref[...]
    w_inter_b = _spread_heads(w_inter, hv)
    num = w_inter_b * _dot(q.astype(BF16), cm.astype(BF16)) + _pick_heads(
        _dot(jnp.concatenate(qk, axis=0).astype(BF16), v), c, hv)
    r = lax.broadcasted_iota(jnp.int32, (BRANCH_W, BRANCH_W), 0) // HD
    cc = lax.broadcasted_iota(jnp.int32, (BRANCH_W, BRANCH_W), 1) // HD
    same_head = r == cc
    qn = _dot_exact_rhs(q * nrow, jnp.where(same_head, 1.0, 0.0).astype(BF16))
    den = w_inter_b * qn + _spread_heads(rowsum, hv)
    hout = num / jnp.maximum(jnp.abs(den), _spread_heads([jnp.exp(-m) for m in m_tok], hv))
    y_ref[...] = _head_norm(jax.nn.sigmoid(x[:, 768:1024]) * hout, g_ref[...])

    khat = ks * _spread_heads(w_s, hv)
    carry_b = _spread_heads(carry, hv)
    c_new = carry_b * cm + jnp.where(same_head, _dot_tn(khat.astype(BF16), v), 0.0)
    n_new = carry_b * nrow + jnp.sum(khat, axis=0, keepdims=True)
    lane = lax.broadcasted_iota(jnp.int32, (1, 128), 1)
    m_vec = jnp.zeros((1, 128), F32)
    for h in range(N_HEADS):
        m_vec = jnp.where(lane == h, m_new[h], m_vec)
    c_ref[...] = c_new
    n_ref[...] = n_new
    mm_ref[...] = m_vec
    c_out_ref[...] = c_new
    n_out_ref[...] = n_new
    m_out_ref[...] = m_vec


def _ml_prompt(ml, small, lw, bsz, seq):
    c = CHUNK
    nc = seq // c
    return pl.pallas_call(
        _ml_prompt_kernel,
        grid=(bsz, nc),
        in_specs=[pl.BlockSpec((c, W_ML), lambda b, n: (b * nc + n, 0)),
                  pl.BlockSpec((c, W_SMALL), lambda b, n: (b * nc + n, 0)),
                  pl.BlockSpec((1, BRANCH_W), lambda b, n: (0, 0))],
        out_specs=[pl.BlockSpec((c, BRANCH_W), lambda b, n: (b * nc + n, 0)),
                   pl.BlockSpec((None, BRANCH_W, BRANCH_W), lambda b, n: (b, 0, 0)),
                   pl.BlockSpec((None, 1, BRANCH_W), lambda b, n: (b, 0, 0)),
                   pl.BlockSpec((None, 1, 128), lambda b, n: (b, 0, 0))],
        out_shape=[jax.ShapeDtypeStruct((bsz * seq, BRANCH_W), F32),
                   jax.ShapeDtypeStruct((bsz, BRANCH_W, BRANCH_W), F32),
                   jax.ShapeDtypeStruct((bsz, 1, BRANCH_W), F32),
                   jax.ShapeDtypeStruct((bsz, 1, 128), F32)],
        scratch_shapes=[pltpu.VMEM((BRANCH_W, BRANCH_W), F32), pltpu.VMEM((1, BRANCH_W), F32),
                        pltpu.VMEM((1, 128), F32)],
        compiler_params=_cparams(("parallel", "arbitrary")),
        name="ml_prompt",
    )(ml, small, lw["ml_g"])


def _diag_blocks(s_bd):
    b = s_bd.shape[0]
    s = s_bd.reshape(b, N_HEADS, HD, N_HEADS, HD)
    return jnp.stack([s[:, h, :, h, :] for h in range(N_HEADS)], axis=1)


GLA_TILE = 256
GLA_SUB = 16


def _gla_prompt_kernel(x_ref, sm_ref, wa2_ref, ba2_ref, g_ref, y_ref, s_out_ref, s_ref):
    n = pl.program_id(1)

    @pl.when(n == 0)
    def _():
        s_ref[...] = jnp.zeros_like(s_ref)

    tq, sc = GLA_TILE, GLA_SUB
    x = x_ref[...]
    q = x[:, 0:128] * (GLA_DK ** -0.5)
    k = x[:, 128:256]
    v = x[:, 256:512]
    log_a = jax.nn.log_sigmoid(_dot(sm_ref[...].astype(BF16), wa2_ref[...]) + ba2_ref[...]) / GLA_TAU
    r = lax.broadcasted_iota(jnp.int32, (tq, tq), 0)
    c = lax.broadcasted_iota(jnp.int32, (tq, tq), 1)
    same = (r // sc) == (c // sc)
    b = _dot_exact_lhs(jnp.where(same & (c <= r), 1.0, 0.0).astype(BF16), log_a)
    btot = _dot_exact_lhs(jnp.where(same, 1.0, 0.0).astype(BF16), log_a)
    qd = (q * jnp.exp(b)).astype(BF16)
    khat_t = (k * jnp.exp(btot - b)).T
    dec_t = jnp.exp(btot).T
    vb = v.astype(BF16)
    kh = lax.broadcasted_iota(jnp.int32, (128, 1), 0) // GLA_DK
    hv = _lane_head(BRANCH_W, HD)
    expand = jnp.where(kh == hv, 1.0, 0.0).astype(BF16)
    lane_chunk = lax.broadcasted_iota(jnp.int32, (1, tq), 1) // sc
    trow = lax.broadcasted_iota(jnp.int32, (sc, 1), 0)
    s = s_ref[...]
    outs = []
    for ci in range(tq // sc):
        lo = ci * sc
        qc, kc, bc, vc = q[lo:lo + sc], k[lo:lo + sc], b[lo:lo + sc], v[lo:lo + sc]
        o = _dot(qd[lo:lo + sc], s.astype(BF16))
        pair = [qc * kc[j:j + 1] * jnp.exp(jnp.where(trow >= j, bc - bc[j:j + 1], -jnp.inf)) for j in range(sc)]
        att = _dot(jnp.concatenate(pair, axis=0).astype(BF16), expand)
        for j in range(sc):
            o = o + att[j * sc:(j + 1) * sc] * vc[j:j + 1]
        outs.append(o)
        upd = _dot(jnp.where(lane_chunk == ci, khat_t, 0.0).astype(BF16), vb)
        s = dec_t[:, lo:lo + 1] * s + jnp.where(kh == hv, upd, 0.0)
    s_ref[...] = s
    s_out_ref[...] = s
    y_ref[...] = jax.nn.silu(x[:, 512:768]) * _head_norm(jnp.concatenate(outs, axis=0), g_ref[...])


def _gla_prompt(gla, small, lw, bsz, seq):
    tq = GLA_TILE
    nt = seq // tq
    full = lambda shape: pl.BlockSpec(shape, lambda b, n: (0,) * len(shape))
    return pl.pallas_call(
        _gla_prompt_kernel,
        grid=(bsz, nt),
        in_specs=[pl.BlockSpec((tq, W_GLA), lambda b, n: (b * nt + n, 0)),
                  pl.BlockSpec((tq, W_SMALL), lambda b, n: (b * nt + n, 0)),
                  full((128, 128)), full((1, 128)), full((1, BRANCH_W))],
        out_specs=[pl.BlockSpec((tq, BRANCH_W), lambda b, n: (b * nt + n, 0)),
                   pl.BlockSpec((None, 128, BRANCH_W), lambda b, n: (b, 0, 0))],
        out_shape=[jax.ShapeDtypeStruct((bsz * seq, BRANCH_W), F32),
                   jax.ShapeDtypeStruct((bsz, 128, BRANCH_W), F32)],
        scratch_shapes=[pltpu.VMEM((128, BRANCH_W), F32)],
        compiler_params=_cparams(("parallel", "arbitrary")),
        name="gla_prompt",
    )(gla, small, lw["wa2"], lw["ba2"], lw["gla_g"])


def _gla_state_from_blockdiag(s_bd):
    b = s_bd.shape[0]
    s = s_bd.reshape(b, N_HEADS, GLA_DK, N_HEADS, HD)
    return jnp.stack([s[:, h, :, h, :] for h in range(N_HEADS)], axis=1)


LANES = 128


def _rank1_update(st_ref, snt_ref, q_ref, k_ref, decay_row, v, dk):
    def body(i, o):
        rows = pl.ds(pl.multiple_of(i * HD, HD), HD)
        new = decay_row(i) * st_ref[rows, :] + k_ref[pl.ds(i, 1), :] * v
        snt_ref[rows, :] = new
        return o + q_ref[pl.ds(i, 1), :] * new
    return lax.fori_loop(0, dk, body, jnp.zeros((HD, LANES), F32), unroll=4)


def _head_rows(h, n):
    return pl.ds(pl.multiple_of(h * n, n), n)


def _gla_sample_kernel(x_ref, sm_ref, wa2_ref, ba2_ref, g_ref, s_ref, y_ref, s_out_ref,
                       qt_ref, kt_ref, at_ref, vt_ref, ot_ref, st_ref, snt_ref):
    h = pl.program_id(0)

    @pl.when(h == 0)
    def _():
        x = x_ref[...]
        qt_ref[...] = (x[:, 0:128] * (GLA_DK ** -0.5)).T
        kt_ref[...] = x[:, 128:256].T
        vt_ref[...] = x[:, 256:512].T
        log_a = jax.nn.log_sigmoid(_dot(sm_ref[...].astype(BF16), wa2_ref[...]) + ba2_ref[...]) / GLA_TAU
        at_ref[...] = jnp.exp(log_a).T

    st_ref[...] = s_ref[...].T
    base = h * GLA_DK
    o = _rank1_update(st_ref, snt_ref, qt_ref.at[_head_rows(h, GLA_DK)], kt_ref.at[_head_rows(h, GLA_DK)],
                      lambda i: at_ref[pl.ds(base + i, 1), :], vt_ref[_head_rows(h, HD), :], GLA_DK)
    ot_ref[_head_rows(h, HD), :] = o
    s_out_ref[...] = snt_ref[...].T

    @pl.when(h == N_HEADS - 1)
    def _():
        y_ref[...] = jax.nn.silu(x_ref[:, 512:768]) * _head_norm(ot_ref[...].T, g_ref[...])


def _gla_sample(gla, small, state, lw):
    per_head = GLA_DK * HD
    full = lambda shape: pl.BlockSpec(shape, lambda h: (0,) * len(shape))
    vm = lambda r: pltpu.VMEM((r, LANES), F32)
    return pl.pallas_call(
        _gla_sample_kernel,
        grid=(N_HEADS,),
        in_specs=[full((LANES, W_GLA)), full((LANES, W_SMALL)), full((128, 128)), full((1, 128)), full((1, BRANCH_W)),
                  pl.BlockSpec((LANES, per_head), lambda h: (0, h))],
        out_specs=[full((LANES, BRANCH_W)), pl.BlockSpec((LANES, per_head), lambda h: (0, h))],
        out_shape=[jax.ShapeDtypeStruct((LANES, BRANCH_W), F32), jax.ShapeDtypeStruct(state.shape, F32)],
        scratch_shapes=[vm(128), vm(128), vm(128), vm(256), vm(256), vm(per_head), vm(per_head)],
        compiler_params=_cparams(("arbitrary",)),
        name="gla_sample",
    )(gla, small, lw["wa2"], lw["ba2"], lw["gla_g"], state)


def _ml_sample_kernel(x_ref, sc_ref, n_ref, g_ref, c_ref, y_ref, c_out_ref, n_out_ref, m_out_ref,
                      qt_ref, kt_ref, vt_ref, nt_ref, nnt_ref, ot_ref, w_ref, kp_ref, st_ref, snt_ref):
    h = pl.program_id(0)

    @pl.when(h == 0)
    def _():
        x = x_ref[...]
        qt_ref[...] = x[:, 0:256].T
        kt_ref[...] = (x[:, 256:512] * (HD ** -0.5)).T
        vt_ref[...] = x[:, 512:768].T
        nt_ref[...] = n_ref[...].T
        sct = sc_ref[...].T
        log_f = jax.nn.log_sigmoid(sct[8:16])
        m_old = sct[16:24]
        m_new = jnp.maximum(log_f + m_old, sct[0:8])
        w_ref[0:8, :] = jnp.exp(log_f + m_old - m_new)
        w_ref[8:16, :] = jnp.exp(sct[0:8] - m_new)
        w_ref[16:24, :] = m_new

    st_ref[...] = c_ref[...].T
    w_carry = w_ref[pl.ds(h, 1), :]
    w_tok = w_ref[pl.ds(8 + h, 1), :]
    m_new = w_ref[pl.ds(16 + h, 1), :]
    kp_ref[...] = kt_ref[_head_rows(h, HD), :] * w_tok
    q_h = qt_ref.at[_head_rows(h, HD)]
    n_new = w_carry * nt_ref[_head_rows(h, HD), :] + kp_ref[...]
    num = _rank1_update(st_ref, snt_ref, q_h, kp_ref, lambda i: w_carry, vt_ref[_head_rows(h, HD), :], HD)
    den = jnp.sum(q_h[...] * n_new, axis=0, keepdims=True)
    ot_ref[_head_rows(h, HD), :] = num / jnp.maximum(jnp.abs(den), jnp.exp(-m_new))
    nnt_ref[_head_rows(h, HD), :] = n_new
    c_out_ref[...] = snt_ref[...].T

    @pl.when(h == N_HEADS - 1)
    def _():
        y_ref[...] = _head_norm(jax.nn.sigmoid(x_ref[:, 768:1024]) * ot_ref[...].T, g_ref[...])
        n_out_ref[...] = nnt_ref[...].T
        m_out_ref[...] = jnp.concatenate([w_ref[16:24, :], jnp.zeros((LANES - 8, LANES), F32)], axis=0).T


def _ml_sample(ml, scal, n_state, c_state, lw):
    per_head = HD * HD
    full = lambda shape: pl.BlockSpec(shape, lambda h: (0,) * len(shape))
    vm = lambda r: pltpu.VMEM((r, LANES), F32)
    return pl.pallas_call(
        _ml_sample_kernel,
        grid=(N_HEADS,),
        in_specs=[full((LANES, W_ML)), full((LANES, 128)), full((LANES, BRANCH_W)), full((1, BRANCH_W)),
                  pl.BlockSpec((LANES, per_head), lambda h: (0, h))],
        out_specs=[full((LANES, BRANCH_W)), pl.BlockSpec((LANES, per_head), lambda h: (0, h)),
                   full((LANES, BRANCH_W)), full((LANES, 128))],
        out_shape=[jax.ShapeDtypeStruct((LANES, BRANCH_W), F32), jax.ShapeDtypeStruct(c_state.shape, F32),
                   jax.ShapeDtypeStruct((LANES, BRANCH_W), F32), jax.ShapeDtypeStruct((LANES, 128), F32)],
        scratch_shapes=[vm(256), vm(256), vm(256), vm(256), vm(256), vm(256), vm(24), vm(HD), vm(per_head), vm(per_head)],
        compiler_params=_cparams(("arbitrary",)),
        name="ml_sample",
    )(ml, scal, n_state, lw["ml_g"], c_state)


def _ret_sample_kernel(x_ref, cos_ref, sin_ref, gam_ref, g_ref, s_ref, y_ref, s_out_ref,
                       qt_ref, kt_ref, vt_ref, ot_ref, qr_ref, kr_ref, st_ref, snt_ref):
    h = pl.program_id(0)

    @pl.when(h == 0)
    def _():
        x = x_ref[...]
        qt_ref[...] = x[:, 0:256].T
        kt_ref[...] = (x[:, 256:512] * (HD ** -0.5)).T
        vt_ref[...] = x[:, 512:768].T

    cos, sin = cos_ref[...], sin_ref[...]
    for src, dst in ((qt_ref, qr_ref), (kt_ref, kr_ref)):
        x1 = src[_head_rows(h, 32), :]
        x2 = src[pl.ds(pl.multiple_of(128 + h * 32, 32), 32), :]
        dst[0:32, :] = x1 * cos - x2 * sin
        dst[32:64, :] = x1 * sin + x2 * cos
    st_ref[...] = s_ref[...].T
    gamma = gam_ref[pl.ds(h, 1), :]
    ot_ref[_head_rows(h, HD), :] = _rank1_update(st_ref, snt_ref, qr_ref, kr_ref, lambda i: gamma,
                                                 vt_ref[_head_rows(h, HD), :], HD)
    s_out_ref[...] = snt_ref[...].T

    @pl.when(h == N_HEADS - 1)
    def _():
        y_ref[...] = jax.nn.silu(x_ref[:, 768:1024]) * _head_norm(ot_ref[...].T, g_ref[...])


def _ret_sample(ret, state, lw):
    per_head = HD * HD
    half = HD // 2
    freq = ROPE_BASE ** (-jnp.arange(half, dtype=F32) / half)
    ang = jnp.full((1,), PAST_LEN, F32)[:, None] * freq[None, :]
    cos = jnp.broadcast_to(jnp.cos(ang).reshape(half, 1), (half, LANES))
    sin = jnp.broadcast_to(jnp.sin(ang).reshape(half, 1), (half, LANES))
    log_g = jnp.log1p(-jnp.exp2(-5.0 - jnp.arange(N_HEADS, dtype=F32)))
    gam = jnp.broadcast_to(jnp.concatenate([jnp.exp(log_g), jnp.ones((4,), F32)]).reshape(8, 1), (8, LANES))
    full = lambda shape: pl.BlockSpec(shape, lambda h: (0,) * len(shape))
    vm = lambda r: pltpu.VMEM((r, LANES), F32)
    return pl.pallas_call(
        _ret_sample_kernel,
        grid=(N_HEADS,),
        in_specs=[full((LANES, W_RET)), full((half, LANES)), full((half, LANES)), full((8, LANES)), full((1, BRANCH_W)),
                  pl.BlockSpec((LANES, per_head), lambda h: (0, h))],
        out_specs=[full((LANES, BRANCH_W)), pl.BlockSpec((LANES, per_head), lambda h: (0, h))],
        out_shape=[jax.ShapeDtypeStruct((LANES, BRANCH_W), F32), jax.ShapeDtypeStruct(state.shape, F32)],
        scratch_shapes=[vm(256), vm(256), vm(256), vm(256), vm(HD), vm(HD), vm(per_head), vm(per_head)],
        compiler_params=_cparams(("arbitrary",)),
        name="ret_sample",
    )(ret, cos, sin, gam, lw["ret_g"], state)


def _swa_sample_kernel(q_ref, kn_ref, vn_ref, sink_ref, kc_ref, vc_ref, o_ref, ko_ref, vo_ref):
    q = q_ref[...]
    kc, vc = kc_ref[...], vc_ref[...]
    kn, vn = kn_ref[...], vn_ref[...]
    scale = HD ** -0.5
    s = _einsum("bhd,bwd->bhw", q.astype(BF16), kc.astype(BF16)) * scale
    s_new = jnp.sum(q * kn, axis=-1, keepdims=True) * scale
    w = lax.broadcasted_iota(jnp.int32, (1, 1, WINDOW), 2)
    s = jnp.where(w > 0, s, -jnp.inf)
    sink = sink_ref[...][:, 0:1]
    mx = jnp.maximum(jnp.maximum(jnp.max(s, axis=-1, keepdims=True), s_new), sink)
    p = jnp.exp(s - mx)
    p_new = jnp.exp(s_new - mx)
    den = jnp.sum(p, axis=-1, keepdims=True) + p_new + jnp.exp(sink - mx)
    o = _einsum("bhw,bwd->bhd", p.astype(BF16), vc.astype(BF16)) + p_new * vn
    o_ref[...] = o / den
    ko_ref[:, 0:WINDOW - 1, :] = kc[:, 1:WINDOW, :]
    ko_ref[:, WINDOW - 1:WINDOW, :] = kn
    vo_ref[:, 0:WINDOW - 1, :] = vc[:, 1:WINDOW, :]
    vo_ref[:, WINDOW - 1:WINDOW, :] = vn


def _swa_sample(swa, cache_k, cache_v, lw, bt=16):
    bsz = swa.shape[0]
    lo = (jnp.arange(128) < HD)[None, :]
    qa, qb = swa[:, 0:128], swa[:, 128:256]
    q = jnp.stack([jnp.where(lo, qa, 0.0), jnp.where(lo, qb, 0.0), jnp.where(lo, 0.0, qa), jnp.where(lo, 0.0, qb)], axis=1)
    kn = swa[:, None, 256:384]
    vn = swa[:, None, 384:512]
    sink = jnp.broadcast_to(lw["sinks"][0, 0:N_HEADS].reshape(N_HEADS, 1), (N_HEADS, 128))
    blk = lambda shape: pl.BlockSpec(shape, lambda i: (i,) + (0,) * (len(shape) - 1))
    o, ko, vo = pl.pallas_call(
        _swa_sample_kernel,
        grid=(bsz // bt,),
        in_specs=[blk((bt, N_HEADS, 128)), blk((bt, 1, 128)), blk((bt, 1, 128)),
                  pl.BlockSpec((N_HEADS, 128), lambda i: (0, 0)),
                  blk((bt, WINDOW, 128)), blk((bt, WINDOW, 128))],
        out_specs=[blk((bt, N_HEADS, 128)), blk((bt, WINDOW, 128)), blk((bt, WINDOW, 128))],
        out_shape=[jax.ShapeDtypeStruct((bsz, N_HEADS, 128), F32), jax.ShapeDtypeStruct(cache_k.shape, F32),
                   jax.ShapeDtypeStruct(cache_v.shape, F32)],
        compiler_params=_cparams(("parallel",)),
        name="swa_sample",
    )(q, kn, vn, sink, cache_k, cache_v)
    y = jnp.concatenate([o[:, 0, 0:64], o[:, 2, 64:128], o[:, 1, 0:64], o[:, 3, 64:128]], axis=1)
    return y, ko, vo


def _ml_scalars(small, m_state):
    z4 = jnp.zeros((small.shape[0], 4), F32)
    return jnp.concatenate([small[:, ML_I_LANE:ML_I_LANE + 4], z4, small[:, ML_F_LANE:ML_F_LANE + 4], z4, m_state, z4,
                            jnp.zeros((small.shape[0], 128 - 24), F32)], axis=1)


def _take_cols(a, pieces, pad):
    parts = [a[..., s:s + n] for s, n in pieces]
    if pad:
        parts.append(jnp.zeros(a.shape[:-1] + (pad,), a.dtype))
    return jnp.concatenate(parts, axis=-1)


def _mix_pieces():
    p = [(0, 768)]
    sq = 784
    p += [(sq, 64), (sq + 128, 64), (sq + 64, 64), (sq + 192, 64)]
    p += [(1040, 256)]
    p += [(1296, 768), (2072, 256)]
    for base in (2328, 2584):
        for half in range(2):
            p += [(base + h * HD + half * 32, 32) for h in range(N_HEADS)]
    p += [(2840, 512)]
    p += [(768, 16), (2064, 8)]
    return p


def _prep_layer(li, w_in, b_in, gla_w_a2, gla_b_a2, gla_norm_g, swa_sinks, ml_norm_g, ret_norm_g, w_branch, w_out,
                ln1_g, ln1_b, w_router_grp, b_router_grp, w_router_exp, b_router_exp, w_exp_gate, w_exp_up,
                w_exp_down, ln2_g, ln2_b):
    pieces = _mix_pieces()
    pad = W_MIX - sum(n for _, n in pieces)
    row = lambda v: v.reshape(1, -1).astype(F32)
    wr = jnp.concatenate([w_router_grp[li], w_router_exp[li],
                          jnp.zeros((D_MODEL, 128 - N_GROUPS - N_EXPERTS), F32)], axis=1)
    wr_hi = wr.astype(BF16)
    wb = w_branch[li]
    wb1 = jnp.concatenate([wb[1, 0:64], wb[1, 128:192], wb[1, 64:128], wb[1, 192:256]], axis=0)
    return dict(
        wm=_take_cols(w_in[li], pieces, pad).astype(BF16),
        bm=row(_take_cols(b_in[li], pieces, pad)),
        wg=w_in[li][:, GATE_COL:].astype(BF16),
        bg=row(b_in[li][GATE_COL:]),
        wa2=jnp.concatenate([gla_w_a2[li], jnp.zeros((128 - GLA_RANK, 128), F32)], axis=0).astype(BF16),
        ba2=row(gla_b_a2[li]),
        gla_g=row(gla_norm_g[li]), ml_g=row(ml_norm_g[li]), ret_g=row(ret_norm_g[li]),
        sinks=row(jnp.concatenate([swa_sinks[li], jnp.zeros((128 - N_HEADS,), F32)])),
        wbr=jnp.stack([wb[0], wb1, wb[2], wb[3]]).astype(BF16),
        wout=w_out[li].astype(BF16),
        ln1_g=row(ln1_g[li]), ln1_b=row(ln1_b[li]), ln2_g=row(ln2_g[li]), ln2_b=row(ln2_b[li]),
        wr_hi=wr_hi, wr_lo=(wr - wr_hi.astype(F32)).astype(BF16),
        br=row(jnp.concatenate([b_router_grp[li], b_router_exp[li], jnp.zeros((128 - N_GROUPS - N_EXPERTS,), F32)])),
        we_gate=w_exp_gate[li].astype(BF16), we_up=w_exp_up[li].astype(BF16), we_down=w_exp_down[li].astype(BF16),
    )


TOK_TILE = 256
MOE_ROWS_PROMPT = 256
MOE_ROWS_SAMPLE = 32


def _ffn_half(x2, ys, lw, tm, bm):
    x1, route, cnt = _merge(x2, ys, lw, tm)
    return _moe(x1, route, cnt, lw, bm, tm)


def _layer_prompt(x2, lw, bsz, seq):
    gla, swa, ml, ret, small = _inproj(x2, lw["wm"], lw["bm"], TOK_TILE)
    ya, s_gla = _gla_prompt(gla, small, lw, bsz, seq)
    yb = _swa_prompt(swa, lw, bsz, seq)
    yc, c_bd, n_row, m_row = _ml_prompt(ml, small, lw, bsz, seq)
    yd, s_ret = _ret_prompt(ret, lw, bsz, seq)
    kv_tail = swa.reshape(bsz, seq, W_SWA)[:, seq - WINDOW:, 256:512]
    states = (_gla_state_from_blockdiag(s_gla),
              kv_tail[:, :, 0:128].reshape(bsz, WINDOW, SWA_KV, HD),
              kv_tail[:, :, 128:256].reshape(bsz, WINDOW, SWA_KV, HD),
              _diag_blocks(c_bd), n_row.reshape(bsz, N_HEADS, HD), m_row[:, 0, 0:N_HEADS],
              _ret_state_from_blockdiag(s_ret))
    return _ffn_half(x2, (ya, yb, yc, yd), lw, TOK_TILE, MOE_ROWS_PROMPT), states


def _layer_sample(x2, st, lw):
    s_gla, k_buf, v_buf, c_ml, n_ml, m_ml, s_ret = st
    bsz = x2.shape[0]
    gla, swa, ml, ret, small = _inproj(x2, lw["wm"], lw["bm"], bsz)
    ya, s_gla_new = _gla_sample(gla, small, s_gla.reshape(bsz, -1), lw)
    yb, k_new, v_new = _swa_sample(swa, k_buf.reshape(bsz, WINDOW, SWA_KV * HD), v_buf.reshape(bsz, WINDOW, SWA_KV * HD), lw)
    yc, c_new, n_new, m_new = _ml_sample(ml, _ml_scalars(small, m_ml), n_ml.reshape(bsz, -1), c_ml.reshape(bsz, -1), lw)
    yd, s_ret_new = _ret_sample(ret, s_ret.reshape(bsz, -1), lw)
    states = (s_gla_new.reshape(s_gla.shape), k_new.reshape(k_buf.shape), v_new.reshape(v_buf.shape),
              c_new.reshape(c_ml.shape), n_new.reshape(n_ml.shape), m_new[:, 0:N_HEADS], s_ret_new.reshape(s_ret.shape))
    return _ffn_half(x2, (ya, yb, yc, yd), lw, bsz, MOE_ROWS_SAMPLE), states


def kernel(x_prompt, x_sample, state_gla, cache_swa_k, cache_swa_v, state_mlstm_c, state_mlstm_n, state_mlstm_m,
           state_ret, w_in, b_in, gla_w_a2, gla_b_a2, gla_norm_g, swa_sinks, ml_norm_g, ret_norm_g, w_branch, w_out,
           ln1_g, ln1_b, w_router_grp, b_router_grp, w_router_exp, b_router_exp, w_exp_gate, w_exp_up, w_exp_down,
           ln2_g, ln2_b):
    bsz, seq, _ = x_prompt.shape
    dec_b, dec_seq, _ = x_sample.shape
    assert dec_seq == 1 and dec_b == LANES and seq % GLA_TILE == 0 and (bsz * seq) % TOK_TILE == 0
    weights = (w_in, b_in, gla_w_a2, gla_b_a2, gla_norm_g, swa_sinks, ml_norm_g, ret_norm_g, w_branch, w_out,
               ln1_g, ln1_b, w_router_grp, b_router_grp, w_router_exp, b_router_exp, w_exp_gate, w_exp_up,
               w_exp_down, ln2_g, ln2_b)
    y_p = x_prompt.reshape(bsz * seq, D_MODEL)
    y_s = x_sample.reshape(dec_b, D_MODEL)
    new_p, new_s = [], []
    for li in range(w_in.shape[0]):
        lw = _prep_layer(li, *weights)
        y_p, st_p = _layer_prompt(y_p, lw, bsz, seq)
        st_in = (state_gla[li], cache_swa_k[li], cache_swa_v[li], state_mlstm_c[li], state_mlstm_n[li],
                 state_mlstm_m[li], state_ret[li])
        y_s, st_s = _layer_sample(y_s, st_in, lw)
        new_p.append(st_p)
        new_s.append(st_s)
    outs = [y_p.reshape(bsz, seq, D_MODEL), y_s.reshape(dec_b, dec_seq, D_MODEL)]
    for i in range(7):
        outs.append(jnp.stack([s[i] for s in new_p], axis=0))
        outs.append(jnp.stack([s[i] for s in new_s], axis=0))
    return tuple(outs)
```

```python
import functools
import math

import jax
import jax.numpy as jnp
from jax import lax
from jax.experimental import pallas as pl
from jax.experimental.pallas import tpu as pltpu

F32 = jnp.float32
BF16 = jnp.bfloat16

D_MODEL = 1024
N_BRANCH = 4
BRANCH_W = D_MODEL // N_BRANCH
N_HEADS = 4
GLA_DK = 32
GLA_RANK = 16
GLA_TAU = 16.0
HD = 64
SWA_KV = 2
WINDOW = 128
ROPE_BASE = 10000.0
PAST_LEN = 16384
N_GROUPS = 4
EPG = 8
N_EXPERTS = N_GROUPS * EPG
D_EXPERT = D_MODEL // 4
DEPTH = 2
ALPHA = (2 * DEPTH) ** 0.25
LN_EPS = 1e-5
GATE_COL = 3352
VMEM_LIMIT = 56 * 1024 * 1024

W_GLA, W_SWA, W_ML, W_RET, W_SMALL = 768, 512, 1024, 1024, 128
W_MIX = W_GLA + W_SWA + W_ML + W_RET + W_SMALL


def _cparams(sem):
    return pltpu.CompilerParams(dimension_semantics=sem, vmem_limit_bytes=VMEM_LIMIT)


def _dot(a, b):
    return jnp.dot(a, b, preferred_element_type=F32)


def _einsum(spec, a, b):
    return jnp.einsum(spec, a, b, preferred_element_type=F32)


def _split2(x):
    hi = x.astype(BF16)
    lo = (x - hi.astype(F32)).astype(BF16)
    return hi, lo


def _split3(x):
    hi = x.astype(BF16)
    r = x - hi.astype(F32)
    mid = r.astype(BF16)
    lo = (r - mid.astype(F32)).astype(BF16)
    return hi, mid, lo


def _dot_exact_lhs(a_exact, b):
    return sum(_dot(a_exact, p) for p in _split3(b))


def _dot_exact_rhs(a, b_exact):
    return sum(_dot(p, b_exact) for p in _split3(a))


def _layer_norm(x, g, b):
    mu = jnp.mean(x, axis=-1, keepdims=True)
    xc = x - mu
    var = jnp.mean(xc * xc, axis=-1, keepdims=True)
    return xc * lax.rsqrt(var + LN_EPS) * g + b


def _inproj_kernel(x_ref, w_ref, b_ref, gla_ref, swa_ref, ml_ref, ret_ref, sm_ref):
    xb = x_ref[...].astype(BF16)
    off = 0
    for ref, width in ((gla_ref, W_GLA), (swa_ref, W_SWA), (ml_ref, W_ML), (ret_ref, W_RET), (sm_ref, W_SMALL)):
        ref[...] = _dot(xb, w_ref[:, off:off + width]) + b_ref[:, off:off + width]
        off += width


def _inproj(x2, wm, bm, tm):
    t = x2.shape[0]
    widths = (W_GLA, W_SWA, W_ML, W_RET, W_SMALL)
    return pl.pallas_call(
        _inproj_kernel,
        grid=(t // tm,),
        in_specs=[pl.BlockSpec((tm, D_MODEL), lambda i: (i, 0)),
                  pl.BlockSpec((D_MODEL, W_MIX), lambda i: (0, 0)),
                  pl.BlockSpec((1, W_MIX), lambda i: (0, 0))],
        out_specs=[pl.BlockSpec((tm, w), lambda i: (i, 0)) for w in widths],
        out_shape=[jax.ShapeDtypeStruct((t, w), F32) for w in widths],
        compiler_params=_cparams(("parallel",)),
        name="inproj",
    )(x2, wm, bm)


def _merge_kernel(x_ref, ya_ref, yb_ref, yc_ref, yd_ref, wg_ref, bg_ref, wbr_ref, wout_ref, g1_ref, b1_ref,
                  wrh_ref, wrl_ref, br_ref, x1_ref, route_ref, cnt_ref, carry_ref, *, tm):
    @pl.when(pl.program_id(0) == 0)
    def _():
        carry_ref[...] = jnp.zeros_like(carry_ref)

    x = x_ref[...]
    xb = x.astype(BF16)
    merged = jnp.zeros((tm, D_MODEL), F32)
    for n, y_ref in enumerate((ya_ref, yb_ref, yc_ref, yd_ref)):
        gate = jax.nn.sigmoid(_dot(xb, wg_ref[:, n * D_MODEL:(n + 1) * D_MODEL]) + bg_ref[:, n * D_MODEL:(n + 1) * D_MODEL])
        merged = merged + gate * _dot(y_ref[...].astype(BF16), wbr_ref[n])
    mix = _dot(merged.astype(BF16), wout_ref[...])
    x1 = _layer_norm(ALPHA * x + mix, g1_ref[...], b1_ref[...])
    x1_ref[...] = x1

    xh, xl = _split2(x1)
    logits = _dot(xh, wrh_ref[...]) + _dot(xh, wrl_ref[...]) + _dot(xl, wrh_ref[...]) + br_ref[...]
    lane = lax.broadcasted_iota(jnp.int32, (tm, 128), 1)
    lanef = lane.astype(F32)
    neg = jnp.float32(-jnp.inf)
    big = jnp.float32(1e9)
    isg = lane < N_GROUPS
    gmax = jnp.max(jnp.where(isg, logits, neg), axis=-1, keepdims=True)
    gidx = jnp.min(jnp.where(isg & (logits == gmax), lanef, big), axis=-1, keepdims=True)
    p_group = 1.0 / jnp.sum(jnp.where(isg, jnp.exp(logits - gmax), 0.0), axis=-1, keepdims=True)
    lo = N_GROUPS + EPG * gidx
    em = (lanef >= lo) & (lanef < lo + EPG)
    v1 = jnp.max(jnp.where(em, logits, neg), axis=-1, keepdims=True)
    i1 = jnp.min(jnp.where(em & (logits == v1), lanef, big), axis=-1, keepdims=True)
    em2 = em & (lanef != i1)
    v2 = jnp.max(jnp.where(em2, logits, neg), axis=-1, keepdims=True)
    i2 = jnp.min(jnp.where(em2 & (logits == v2), lanef, big), axis=-1, keepdims=True)
    e21 = jnp.exp(v2 - v1)
    gt1 = 1.0 / (1.0 + e21)
    gt2 = e21 * gt1

    sel1 = lanef == i1
    sel2 = lanef == i2
    onehot = jnp.where(sel1 | sel2, 1.0, 0.0)
    row = lax.broadcasted_iota(jnp.int32, (tm, tm), 0)
    col = lax.broadcasted_iota(jnp.int32, (tm, tm), 1)
    tri = jnp.where(col < row, 1.0, 0.0).astype(BF16)
    before = _dot(tri, onehot.astype(BF16)) + carry_ref[...]
    rank1 = jnp.sum(jnp.where(sel1, before, 0.0), axis=-1, keepdims=True)
    rank2 = jnp.sum(jnp.where(sel2, before, 0.0), axis=-1, keepdims=True)
    carry_ref[...] = carry_ref[...] + jnp.sum(onehot, axis=0, keepdims=True)
    cnt_ref[...] = carry_ref[...]

    vals = (i1 - N_GROUPS, i2 - N_GROUPS, p_group * gt1, p_group * gt2, rank1, rank2)
    route = jnp.zeros((tm, 128), F32)
    for k, v in enumerate(vals):
        route = jnp.where(lane == k, v, route)
    route_ref[...] = route


def _merge(x2, ys, lw, tm):
    t = x2.shape[0]
    tok = lambda w: pl.BlockSpec((tm, w), lambda i: (i, 0))
    full = lambda shape: pl.BlockSpec(shape, lambda i: (0,) * len(shape))
    return pl.pallas_call(
        functools.partial(_merge_kernel, tm=tm),
        grid=(t // tm,),
        in_specs=[tok(D_MODEL), tok(BRANCH_W), tok(BRANCH_W), tok(BRANCH_W), tok(BRANCH_W),
                  full((D_MODEL, N_BRANCH * D_MODEL)), full((1, N_BRANCH * D_MODEL)),
                  full((N_BRANCH, BRANCH_W, D_MODEL)), full((D_MODEL, D_MODEL)),
                  full((1, D_MODEL)), full((1, D_MODEL)),
                  full((D_MODEL, 128)), full((D_MODEL, 128)), full((1, 128))],
        out_specs=[tok(D_MODEL), tok(128), full((1, 128))],
        out_shape=[jax.ShapeDtypeStruct((t, D_MODEL), F32), jax.ShapeDtypeStruct((t, 128), F32),
                   jax.ShapeDtypeStruct((1, 128), F32)],
        scratch_shapes=[pltpu.VMEM((1, 128), F32)],
        compiler_params=_cparams(("arbitrary",)),
        name="merge",
    )(x2, *ys, lw["wg"], lw["bg"], lw["wbr"], lw["wout"], lw["ln1_g"], lw["ln1_b"], lw["wr_hi"], lw["wr_lo"], lw["br"])


def _row_copy(src_hbm, src_row, dst_vmem, dst_row, sem):
    return pltpu.make_async_copy(src_hbm.at[pl.ds(src_row, 1), :], dst_vmem.at[pl.ds(dst_row, 1), :], sem)


ROW_UNROLL = 8


def _gather_rows(src_hbm, idx_ref, idx_base, idx_stride, dst, sem, rows):
    def issue(i, c):
        r0 = pl.multiple_of(i * ROW_UNROLL, ROW_UNROLL)
        for u in range(ROW_UNROLL):
            _row_copy(src_hbm, idx_ref[idx_base + idx_stride * (r0 + u)], dst, r0 + u, sem).start()
        return c

    lax.fori_loop(0, rows // ROW_UNROLL, issue, 0)


def _wait_rows(src_hbm, dst, sem, rows):
    pltpu.make_async_copy(src_hbm.at[pl.ds(0, rows), :], dst, sem).wait()


def _expert_kernel(be_ref, nu_ref, src_ref, x_hbm, wg_ref, wu_ref, wd_ref, ys_ref, xbuf, sem, *, bm):
    blk = pl.program_id(0)
    n_used = nu_ref[0]
    slot = blk % 2

    @pl.when((blk == 0) & (n_used > 0))
    def _():
        _gather_rows(x_hbm, src_ref, 0, 1, xbuf.at[0], sem.at[0], bm)

    @pl.when(blk + 1 < n_used)
    def _():
        _gather_rows(x_hbm, src_ref, (blk + 1) * bm, 1, xbuf.at[1 - slot], sem.at[1 - slot], bm)

    @pl.when(blk < n_used)
    def _():
        _wait_rows(x_hbm, xbuf.at[slot], sem.at[slot], bm)
        xb = xbuf[slot].astype(BF16)
        hid = jax.nn.silu(_dot(xb, wg_ref[...])) * _dot(xb, wu_ref[...])
        ys_ref[...] = _dot(hid.astype(BF16), wd_ref[...])

    @pl.when(blk >= n_used)
    def _():
        ys_ref[...] = jnp.zeros_like(ys_ref)


def _experts(x1, block_expert, n_used, src_tok, lw, bm):
    n_blocks = block_expert.shape[0]
    grid_spec = pltpu.PrefetchScalarGridSpec(
        num_scalar_prefetch=3,
        grid=(n_blocks,),
        in_specs=[pl.BlockSpec(memory_space=pl.ANY),
                  pl.BlockSpec((None, D_MODEL, D_EXPERT), lambda b, be, nu, src: (be[b], 0, 0)),
                  pl.BlockSpec((None, D_MODEL, D_EXPERT), lambda b, be, nu, src: (be[b], 0, 0)),
                  pl.BlockSpec((None, D_EXPERT, D_MODEL), lambda b, be, nu, src: (be[b], 0, 0))],
        out_specs=pl.BlockSpec((bm, D_MODEL), lambda b, be, nu, src: (b, 0)),
        scratch_shapes=[pltpu.VMEM((2, bm, D_MODEL), F32), pltpu.SemaphoreType.DMA((2,))],
    )
    return pl.pallas_call(
        functools.partial(_expert_kernel, bm=bm),
        grid_spec=grid_spec,
        out_shape=jax.ShapeDtypeStruct((n_blocks * bm, D_MODEL), F32),
        compiler_params=_cparams(("arbitrary",)),
        name="experts",
    )(block_expert, n_used, src_tok, x1, lw["we_gate"], lw["we_up"], lw["we_down"])


def _combine_kernel(dest_ref, ys_hbm, x1_ref, route_ref, g2_ref, b2_ref, out_ref, buf, sem, *, tm):
    i = pl.program_id(0)
    slot = i % 2

    def fetch(tile, s):
        for j in range(2):
            _gather_rows(ys_hbm, dest_ref, 2 * tile * tm + j, 2, buf.at[s, j], sem.at[s], tm)

    @pl.when(i == 0)
    def _():
        fetch(0, 0)

    @pl.when(i + 1 < pl.num_programs(0))
    def _():
        fetch(i + 1, 1 - slot)

    for j in range(2):
        _wait_rows(ys_hbm, buf.at[slot, j], sem.at[slot], tm)
    route = route_ref[...]
    ffn = buf[slot, 0] * route[:, 2:3] + buf[slot, 1] * route[:, 3:4]
    out_ref[...] = _layer_norm(ALPHA * x1_ref[...] + ffn, g2_ref[...], b2_ref[...])


def _combine(ys, x1, route, dest_flat, lw, tm):
    t = x1.shape[0]
    grid_spec = pltpu.PrefetchScalarGridSpec(
        num_scalar_prefetch=1,
        grid=(t // tm,),
        in_specs=[pl.BlockSpec(memory_space=pl.ANY),
                  pl.BlockSpec((tm, D_MODEL), lambda i, d: (i, 0)),
                  pl.BlockSpec((tm, 128), lambda i, d: (i, 0)),
                  pl.BlockSpec((1, D_MODEL), lambda i, d: (0, 0)),
                  pl.BlockSpec((1, D_MODEL), lambda i, d: (0, 0))],
        out_specs=pl.BlockSpec((tm, D_MODEL), lambda i, d: (i, 0)),
        scratch_shapes=[pltpu.VMEM((2, 2, tm, D_MODEL), F32), pltpu.SemaphoreType.DMA((2,))],
    )
    return pl.pallas_call(
        functools.partial(_combine_kernel, tm=tm),
        grid_spec=grid_spec,
        out_shape=jax.ShapeDtypeStruct((t, D_MODEL), F32),
        compiler_params=_cparams(("arbitrary",)),
        name="combine",
    )(dest_flat, ys, x1, route, lw["ln2_g"], lw["ln2_b"])


def _moe(x1, route, cnt, lw, bm, tm):
    t = x1.shape[0]
    expert = route[:, 0:2].astype(jnp.int32)
    rank = route[:, 4:6].astype(jnp.int32)
    counts = cnt[0, N_GROUPS:N_GROUPS + N_EXPERTS].astype(jnp.int32)
    padded = (counts + bm - 1) // bm * bm
    pad_end = jnp.cumsum(padded)
    pad_start = pad_end - padded
    dest = (pad_start[expert] + rank).reshape(-1)
    n_blocks = -(-(2 * t) // bm) + N_EXPERTS
    tok = jnp.arange(2 * t, dtype=jnp.int32) // 2
    src_tok = jnp.zeros((n_blocks * bm,), jnp.int32).at[dest].set(tok)
    first_row = jnp.arange(n_blocks, dtype=jnp.int32) * bm
    block_expert = jnp.minimum(jnp.sum(pad_end[None, :] <= first_row[:, None], axis=1), N_EXPERTS - 1).astype(jnp.int32)
    n_used = (pad_end[-1] // bm).astype(jnp.int32).reshape(1)
    ys = _experts(x1, block_expert, n_used, src_tok, lw, bm)
    return _combine(ys, x1, route, dest, lw, tm)


CHUNK = 128


def _dot_nt(a, b):
    return lax.dot_general(a, b, (((1,), (1,)), ((), ())), preferred_element_type=F32)


def _dot_tn(a, b):
    return lax.dot_general(a, b, (((0,), (0,)), ((), ())), preferred_element_type=F32)


def _lane_head(width, group):
    return lax.broadcasted_iota(jnp.int32, (1, width), 1) // group


def _block_diag_mask(rows_head, cols_head):
    return rows_head == cols_head


def _head_norm(o, gain):
    r = lax.broadcasted_iota(jnp.int32, (BRANCH_W, BRANCH_W), 0) // HD
    c = lax.broadcasted_iota(jnp.int32, (BRANCH_W, BRANCH_W), 1) // HD
    avg = jnp.where(r == c, 1.0 / HD, 0.0).astype(BF16)
    mu = _dot_exact_rhs(o, avg)
    xc = o - mu
    var = _dot_exact_rhs(xc * xc, avg)
    return xc * lax.rsqrt(var + LN_EPS) * gain


def _stack_heads(x, head_of_lane):
    return jnp.concatenate([jnp.where(head_of_lane == h, x, 0.0) for h in range(N_HEADS)], axis=0)


def _pick_heads(stacked, rows, head_of_lane):
    out = jnp.where(head_of_lane == 0, stacked[0:rows], 0.0)
    for h in range(1, N_HEADS):
        out = jnp.where(head_of_lane == h, stacked[h * rows:(h + 1) * rows], out)
    return out


def _spread_heads(cols, head_of_lane):
    out = jnp.where(head_of_lane == 0, cols[0], 0.0)
    for h in range(1, N_HEADS):
        out = jnp.where(head_of_lane == h, cols[h], out)
    return out


def _swa_prompt_kernel(q_ref, kv_ref, kvp_ref, sink_ref, y_ref):
    n = pl.program_id(1)
    q = q_ref[...]
    qa, qb = q[:, 0:128], q[:, 128:256]
    kv, kvp = kv_ref[...], kvp_ref[...]
    kext = jnp.concatenate([kvp[:, 0:128], kv[:, 0:128]], axis=0).astype(BF16)
    vext = jnp.concatenate([kvp[:, 128:256], kv[:, 128:256]], axis=0).astype(BF16)
    lo = lax.broadcasted_iota(jnp.int32, (1, 128), 1) < HD
    qs = jnp.concatenate([jnp.where(lo, qa, 0.0), jnp.where(lo, qb, 0.0),
                          jnp.where(lo, 0.0, qa), jnp.where(lo, 0.0, qb)], axis=0).astype(BF16)
    s = _dot_nt(qs, kext) * (HD ** -0.5)
    a = lax.broadcasted_iota(jnp.int32, (WINDOW, 2 * WINDOW), 0)
    e = lax.broadcasted_iota(jnp.int32, (WINDOW, 2 * WINDOW), 1)
    valid = (e > a) & (e <= a + WINDOW) & ((n > 0) | (e >= WINDOW))
    outs = []
    for h in range(N_HEADS):
        sh = jnp.where(valid, s[h * WINDOW:(h + 1) * WINDOW], -jnp.inf)
        sink = sink_ref[:, h:h + 1]
        mx = jnp.maximum(jnp.max(sh, axis=-1, keepdims=True), sink)
        p = jnp.exp(sh - mx)
        p = p / (jnp.sum(p, axis=-1, keepdims=True) + jnp.exp(sink - mx))
        outs.append(_dot(p.astype(BF16), vext))
    y_ref[...] = jnp.concatenate([jnp.where(lo, outs[0], outs[2]), jnp.where(lo, outs[1], outs[3])], axis=1)


def _swa_prompt(swa, lw, bsz, seq):
    nb = seq // WINDOW
    return pl.pallas_call(
        _swa_prompt_kernel,
        grid=(bsz, nb),
        in_specs=[pl.BlockSpec((WINDOW, 256), lambda b, n: (b * nb + n, 0)),
                  pl.BlockSpec((WINDOW, 256), lambda b, n: (b * nb + n, 1)),
                  pl.BlockSpec((WINDOW, 256), lambda b, n: (b * nb + jnp.maximum(n - 1, 0), 1)),
                  pl.BlockSpec((1, 128), lambda b, n: (0, 0))],
        out_specs=pl.BlockSpec((WINDOW, BRANCH_W), lambda b, n: (b * nb + n, 0)),
        out_shape=jax.ShapeDtypeStruct((bsz * seq, BRANCH_W), F32),
        compiler_params=_cparams(("parallel", "parallel")),
        name="swa_prompt",
    )(swa, swa, swa, lw["sinks"])


def _rotate(x, cos, sin):
    x1, x2 = x[:, 0:128], x[:, 128:256]
    return jnp.concatenate([x1 * cos - x2 * sin, x1 * sin + x2 * cos], axis=1)


def _ret_prompt_kernel(r_ref, cos_ref, sin_ref, dmat_ref, qdec_ref, kdec_ref, cdec_ref, g_ref, y_ref, s_out_ref, s_ref):
    n = pl.program_id(1)

    @pl.when(n == 0)
    def _():
        s_ref[...] = jnp.zeros_like(s_ref)

    c = CHUNK
    r = r_ref[...]
    cos, sin = cos_ref[...], sin_ref[...]
    qr = _rotate(r[:, 0:256], cos, sin)
    kr = _rotate(r[:, 256:512], cos, sin) * (HD ** -0.5)
    v = r[:, 512:768].astype(BF16)
    hq = _lane_head(BRANCH_W, 32) % N_HEADS
    hv = _lane_head(BRANCH_W, HD)
    att = _dot_nt(_stack_heads(qr, hq).astype(BF16), kr.astype(BF16)) * dmat_ref[...]
    o = _pick_heads(_dot(att.astype(BF16), v), c, hv)
    s = s_ref[...]
    o = o + _dot((qr * qdec_ref[...]).astype(BF16), s.astype(BF16))
    upd = _dot_tn((kr * kdec_ref[...]).astype(BF16), v)
    rows_head = (lax.broadcasted_iota(jnp.int32, (BRANCH_W, 1), 0) // 32) % N_HEADS
    s_new = cdec_ref[...] * s + jnp.where(rows_head == hv, upd, 0.0)
    s_ref[...] = s_new
    s_out_ref[...] = s_new
    y_ref[...] = jax.nn.silu(r[:, 768:1024]) * _head_norm(o, g_ref[...])


def _ret_tables(seq):
    c = CHUNK
    half = HD // 2
    freq = ROPE_BASE ** (-jnp.arange(half, dtype=F32) / half)
    ang = jnp.arange(seq).astype(F32)[:, None] * freq[None, :]
    cos = jnp.tile(jnp.cos(ang), (1, N_HEADS))
    sin = jnp.tile(jnp.sin(ang), (1, N_HEADS))
    log_g = jnp.log1p(-jnp.exp2(-5.0 - jnp.arange(N_HEADS, dtype=F32)))
    idx = jnp.arange(c, dtype=F32)
    rel = idx[:, None] - idx[None, :]
    dmat = jnp.exp(jnp.where(rel[None] >= 0, rel[None] * log_g[:, None, None], -jnp.inf)).reshape(N_HEADS * c, c)
    lane_head_q = (jnp.arange(BRANCH_W) // 32) % N_HEADS
    qdec = jnp.exp((idx + 1.0)[:, None] * log_g[lane_head_q][None, :])
    kdec = jnp.exp((c - 1.0 - idx)[:, None] * log_g[lane_head_q][None, :])
    cdec = jnp.exp(c * log_g)[jnp.arange(BRANCH_W) // HD][None, :]
    return cos, sin, dmat, qdec, kdec, cdec


def _ret_prompt(ret, lw, bsz, seq):
    c = CHUNK
    nc = seq // c
    cos, sin, dmat, qdec, kdec, cdec = _ret_tables(seq)
    full = lambda shape: pl.BlockSpec(shape, lambda b, n: (0,) * len(shape))
    return pl.pallas_call(
        _ret_prompt_kernel,
        grid=(bsz, nc),
        in_specs=[pl.BlockSpec((c, W_RET), lambda b, n: (b * nc + n, 0)),
                  pl.BlockSpec((c, 128), lambda b, n: (n, 0)), pl.BlockSpec((c, 128), lambda b, n: (n, 0)),
                  full((N_HEADS * c, c)), full((c, BRANCH_W)), full((c, BRANCH_W)), full((1, BRANCH_W)),
                  full((1, BRANCH_W))],
        out_specs=[pl.BlockSpec((c, BRANCH_W), lambda b, n: (b * nc + n, 0)),
                   pl.BlockSpec((None, BRANCH_W, BRANCH_W), lambda b, n: (b, 0, 0))],
        out_shape=[jax.ShapeDtypeStruct((bsz * seq, BRANCH_W), F32),
                   jax.ShapeDtypeStruct((bsz, BRANCH_W, BRANCH_W), F32)],
        scratch_shapes=[pltpu.VMEM((BRANCH_W, BRANCH_W), F32)],
        compiler_params=_cparams(("parallel", "arbitrary")),
        name="ret_prompt",
    )(ret, cos, sin, dmat, qdec, kdec, cdec, lw["ret_g"])


def _ret_state_from_blockdiag(s_bd):
    b = s_bd.shape[0]
    s = s_bd.reshape(b, 2, N_HEADS, 32, N_HEADS, HD)
    s = jnp.stack([s[:, :, h, :, h, :] for h in range(N_HEADS)], axis=1)
    return s.reshape(b, N_HEADS, HD, HD)


ML_I_LANE = GLA_RANK
ML_F_LANE = GLA_RANK + N_HEADS


def _ml_prompt_kernel(m_ref, sm_ref, g_ref, y_ref, c_out_ref, n_out_ref, m_out_ref, c_ref, n_ref, mm_ref):
    nchunk = pl.program_id(1)

    @pl.when(nchunk == 0)
    def _():
        c_ref[...] = jnp.zeros_like(c_ref)
        n_ref[...] = jnp.zeros_like(n_ref)
        mm_ref[...] = jnp.zeros_like(mm_ref)

    c = CHUNK
    x = m_ref[...]
    q = x[:, 0:256]
    ks = x[:, 256:512] * (HD ** -0.5)
    v = x[:, 512:768].astype(BF16)
    sm = sm_ref[...]
    smt = sm.T
    row = lax.broadcasted_iota(jnp.int32, (c, c), 0)
    col = lax.broadcasted_iota(jnp.int32, (c, c), 1)
    causal = col <= row
    tri_lo = jnp.where(causal, 1.0, 0.0).astype(BF16)
    tri_up = jnp.where(row <= col, 1.0, 0.0).astype(BF16)
    bcols = _dot_exact_lhs(tri_lo, jax.nn.log_sigmoid(sm))
    brows = _dot_exact_rhs(jax.nn.log_sigmoid(smt), tri_up)
    hv = _lane_head(BRANCH_W, HD)
    sc = _dot_nt(_stack_heads(q, hv).astype(BF16), ks.astype(BF16))
    mm = mm_ref[...]
    qk, w_inter, m_tok, rowsum, w_s, carry, m_new = [], [], [], [], [], [], []
    for h in range(N_HEADS):
        bcol = bcols[:, ML_F_LANE + h:ML_F_LANE + h + 1]
        brow = brows[ML_F_LANE + h:ML_F_LANE + h + 1, :]
        icol = sm[:, ML_I_LANE + h:ML_I_LANE + h + 1]
        irow = smt[ML_I_LANE + h:ML_I_LANE + h + 1, :]
        m_prev = mm[:, h:h + 1]
        d_log = jnp.where(causal, bcol - brow + irow, -jnp.inf)
        inter_log = bcol + m_prev
        m_t = jnp.maximum(inter_log, jnp.max(d_log, axis=-1, keepdims=True))
        qk_h = sc[h * c:(h + 1) * c] * jnp.exp(d_log - m_t)
        qk.append(qk_h)
        rowsum.append(jnp.sum(qk_h, axis=-1, keepdims=True))
        w_inter.append(jnp.exp(inter_log - m_t))
        m_tok.append(m_t)
        mn = m_t[c - 1:c, :]
        b_last = bcol[c - 1:c, :]
        m_new.append(mn)
        carry.append(jnp.exp(b_last + m_prev - mn))
        w_s.append(jnp.exp(b_last - bcol + icol - mn))
    cm = c_ref[...]
    nrow = n_ref[...]
    w_inter_b = _spread_heads(w_inter, hv)
    num = w_inter_b * _dot(q.astype(BF16), cm.astype(BF16)) + _pick_heads(
        _dot(jnp.concatenate(qk, axis=0).astype(BF16), v), c, hv)
    r = lax.broadcasted_iota(jnp.int32, (BRANCH_W, BRANCH_W), 0) // HD
    cc = lax.broadcasted_iota(jnp.int32, (BRANCH_W, BRANCH_W), 1) // HD
    same_head = r == cc
    qn = _dot_exact_rhs(q * nrow, jnp.where(same_head, 1.0, 0.0).astype(BF16))
    den = w_inter_b * qn + _spread_heads(rowsum, hv)
    hout = num / jnp.maximum(jnp.abs(den), _spread_heads([jnp.exp(-m) for m in m_tok], hv))
    y_ref[...] = _head_norm(jax.nn.sigmoid(x[:, 768:1024]) * hout, g_ref[...])

    khat = ks * _spread_heads(w_s, hv)
    carry_b = _spread_heads(carry, hv)
    c_new = carry_b * cm + jnp.where(same_head, _dot_tn(khat.astype(BF16), v), 0.0)
    n_new = carry_b * nrow + jnp.sum(khat, axis=0, keepdims=True)
    lane = lax.broadcasted_iota(jnp.int32, (1, 128), 1)
    m_vec = jnp.zeros((1, 128), F32)
    for h in range(N_HEADS):
        m_vec = jnp.where(lane == h, m_new[h], m_vec)
    c_ref[...] = c_new
    n_ref[...] = n_new
    mm_ref[...] = m_vec
    c_out_ref[...] = c_new
    n_out_ref[...] = n_new
    m_out_ref[...] = m_vec


def _ml_prompt(ml, small, lw, bsz, seq):
    c = CHUNK
    nc = seq // c
    return pl.pallas_call(
        _ml_prompt_kernel,
        grid=(bsz, nc),
        in_specs=[pl.BlockSpec((c, W_ML), lambda b, n: (b * nc + n, 0)),
                  pl.BlockSpec((c, W_SMALL), lambda b, n: (b * nc + n, 0)),
                  pl.BlockSpec((1, BRANCH_W), lambda b, n: (0, 0))],
        out_specs=[pl.BlockSpec((c, BRANCH_W), lambda b, n: (b * nc + n, 0)),
                   pl.BlockSpec((None, BRANCH_W, BRANCH_W), lambda b, n: (b, 0, 0)),
                   pl.BlockSpec((None, 1, BRANCH_W), lambda b, n: (b, 0, 0)),
                   pl.BlockSpec((None, 1, 128), lambda b, n: (b, 0, 0))],
        out_shape=[jax.ShapeDtypeStruct((bsz * seq, BRANCH_W), F32),
                   jax.ShapeDtypeStruct((bsz, BRANCH_W, BRANCH_W), F32),
                   jax.ShapeDtypeStruct((bsz, 1, BRANCH_W), F32),
                   jax.ShapeDtypeStruct((bsz, 1, 128), F32)],
        scratch_shapes=[pltpu.VMEM((BRANCH_W, BRANCH_W), F32), pltpu.VMEM((1, BRANCH_W), F32),
                        pltpu.VMEM((1, 128), F32)],
        compiler_params=_cparams(("parallel", "arbitrary")),
        name="ml_prompt",
    )(ml, small, lw["ml_g"])


def _diag_blocks(s_bd):
    b = s_bd.shape[0]
    s = s_bd.reshape(b, N_HEADS, HD, N_HEADS, HD)
    return jnp.stack([s[:, h, :, h, :] for h in range(N_HEADS)], axis=1)


GLA_TILE = 256
GLA_SUB = 16


def _gla_prompt_kernel(x_ref, sm_ref, wa2_ref, ba2_ref, g_ref, y_ref, s_out_ref, s_ref):
    n = pl.program_id(1)

    @pl.when(n == 0)
    def _():
        s_ref[...] = jnp.zeros_like(s_ref)

    tq, sc = GLA_TILE, GLA_SUB
    x = x_ref[...]
    q = x[:, 0:128] * (GLA_DK ** -0.5)
    k = x[:, 128:256]
    v = x[:, 256:512]
    log_a = jax.nn.log_sigmoid(_dot(sm_ref[...].astype(BF16), wa2_ref[...]) + ba2_ref[...]) / GLA_TAU
    r = lax.broadcasted_iota(jnp.int32, (tq, tq), 0)
    c = lax.broadcasted_iota(jnp.int32, (tq, tq), 1)
    same = (r // sc) == (c // sc)
    b = _dot_exact_lhs(jnp.where(same & (c <= r), 1.0, 0.0).astype(BF16), log_a)
    btot = _dot_exact_lhs(jnp.where(same, 1.0, 0.0).astype(BF16), log_a)
    qd = (q * jnp.exp(b)).astype(BF16)
    khat_t = (k * jnp.exp(btot - b)).T
    dec_t = jnp.exp(btot).T
    vb = v.astype(BF16)
    kh = lax.broadcasted_iota(jnp.int32, (128, 1), 0) // GLA_DK
    hv = _lane_head(BRANCH_W, HD)
    expand = jnp.where(kh == hv, 1.0, 0.0).astype(BF16)
    lane_chunk = lax.broadcasted_iota(jnp.int32, (1, tq), 1) // sc
    trow = lax.broadcasted_iota(jnp.int32, (sc, 1), 0)
    s = s_ref[...]
    outs = []
    for ci in range(tq // sc):
        lo = ci * sc
        qc, kc, bc, vc = q[lo:lo + sc], k[lo:lo + sc], b[lo:lo + sc], v[lo:lo + sc]
        o = _dot(qd[lo:lo + sc], s.astype(BF16))
        pair = [qc * kc[j:j + 1] * jnp.exp(jnp.where(trow >= j, bc - bc[j:j + 1], -jnp.inf)) for j in range(sc)]
        att = _dot(jnp.concatenate(pair, axis=0).astype(BF16), expand)
        for j in range(sc):
            o = o + att[j * sc:(j + 1) * sc] * vc[j:j + 1]
        outs.append(o)
        upd = _dot(jnp.where(lane_chunk == ci, khat_t, 0.0).astype(BF16), vb)
        s = dec_t[:, lo:lo + 1] * s + jnp.where(kh == hv, upd, 0.0)
    s_ref[...] = s
    s_out_ref[...] = s
    y_ref[...] = jax.nn.silu(x[:, 512:768]) * _head_norm(jnp.concatenate(outs, axis=0), g_ref[...])


def _gla_prompt(gla, small, lw, bsz, seq):
    tq = GLA_TILE
    nt = seq // tq
    full = lambda shape: pl.BlockSpec(shape, lambda b, n: (0,) * len(shape))
    return pl.pallas_call(
        _gla_prompt_kernel,
        grid=(bsz, nt),
        in_specs=[pl.BlockSpec((tq, W_GLA), lambda b, n: (b * nt + n, 0)),
                  pl.BlockSpec((tq, W_SMALL), lambda b, n: (b * nt + n, 0)),
                  full((128, 128)), full((1, 128)), full((1, BRANCH_W))],
        out_specs=[pl.BlockSpec((tq, BRANCH_W), lambda b, n: (b * nt + n, 0)),
                   pl.BlockSpec((None, 128, BRANCH_W), lambda b, n: (b, 0, 0))],
        out_shape=[jax.ShapeDtypeStruct((bsz * seq, BRANCH_W), F32),
                   jax.ShapeDtypeStruct((bsz, 128, BRANCH_W), F32)],
        scratch_shapes=[pltpu.VMEM((128, BRANCH_W), F32)],
        compiler_params=_cparams(("parallel", "arbitrary")),
        name="gla_prompt",
    )(gla, small, lw["wa2"], lw["ba2"], lw["gla_g"])


def _gla_state_from_blockdiag(s_bd):
    b = s_bd.shape[0]
    s = s_bd.reshape(b, N_HEADS, GLA_DK, N_HEADS, HD)
    return jnp.stack([s[:, h, :, h, :] for h in range(N_HEADS)], axis=1)


LANES = 128


def _rank1_update(st_ref, snt_ref, q_ref, k_ref, decay_row, v, dk):
    def body(i, o):
        rows = pl.ds(pl.multiple_of(i * HD, HD), HD)
        new = decay_row(i) * st_ref[rows, :] + k_ref[pl.ds(i, 1), :] * v
        snt_ref[rows, :] = new
        return o + q_ref[pl.ds(i, 1), :] * new
    return lax.fori_loop(0, dk, body, jnp.zeros((HD, LANES), F32), unroll=4)


def _head_rows(h, n):
    return pl.ds(pl.multiple_of(h * n, n), n)


def _gla_sample_kernel(x_ref, sm_ref, wa2_ref, ba2_ref, g_ref, s_ref, y_ref, s_out_ref,
                       qt_ref, kt_ref, at_ref, vt_ref, ot_ref, st_ref, snt_ref):
    h = pl.program_id(0)

    @pl.when(h == 0)
    def _():
        x = x_ref[...]
        qt_ref[...] = (x[:, 0:128] * (GLA_DK ** -0.5)).T
        kt_ref[...] = x[:, 128:256].T
        vt_ref[...] = x[:, 256:512].T
        log_a = jax.nn.log_sigmoid(_dot(sm_ref[...].astype(BF16), wa2_ref[...]) + ba2_ref[...]) / GLA_TAU
        at_ref[...] = jnp.exp(log_a).T

    st_ref[...] = s_ref[...].T
    base = h * GLA_DK
    o = _rank1_update(st_ref, snt_ref, qt_ref.at[_head_rows(h, GLA_DK)], kt_ref.at[_head_rows(h, GLA_DK)],
                      lambda i: at_ref[pl.ds(base + i, 1), :], vt_ref[_head_rows(h, HD), :], GLA_DK)
    ot_ref[_head_rows(h, HD), :] = o
    s_out_ref[...] = snt_ref[...].T

    @pl.when(h == N_HEADS - 1)
    def _():
        y_ref[...] = jax.nn.silu(x_ref[:, 512:768]) * _head_norm(ot_ref[...].T, g_ref[...])


def _gla_sample(gla, small, state, lw):
    per_head = GLA_DK * HD
    full = lambda shape: pl.BlockSpec(shape, lambda h: (0,) * len(shape))
    vm = lambda r: pltpu.VMEM((r, LANES), F32)
    return pl.pallas_call(
        _gla_sample_kernel,
        grid=(N_HEADS,),
        in_specs=[full((LANES, W_GLA)), full((LANES, W_SMALL)), full((128, 128)), full((1, 128)), full((1, BRANCH_W)),
                  pl.BlockSpec((LANES, per_head), lambda h: (0, h))],
        out_specs=[full((LANES, BRANCH_W)), pl.BlockSpec((LANES, per_head), lambda h: (0, h))],
        out_shape=[jax.ShapeDtypeStruct((LANES, BRANCH_W), F32), jax.ShapeDtypeStruct(state.shape, F32)],
        scratch_shapes=[vm(128), vm(128), vm(128), vm(256), vm(256), vm(per_head), vm(per_head)],
        compiler_params=_cparams(("arbitrary",)),
        name="gla_sample",
    )(gla, small, lw["wa2"], lw["ba2"], lw["gla_g"], state)


def _ml_sample_kernel(x_ref, sc_ref, n_ref, g_ref, c_ref, y_ref, c_out_ref, n_out_ref, m_out_ref,
                      qt_ref, kt_ref, vt_ref, nt_ref, nnt_ref, ot_ref, w_ref, kp_ref, st_ref, snt_ref):
    h = pl.program_id(0)

    @pl.when(h == 0)
    def _():
        x = x_ref[...]
        qt_ref[...] = x[:, 0:256].T
        kt_ref[...] = (x[:, 256:512] * (HD ** -0.5)).T
        vt_ref[...] = x[:, 512:768].T
        nt_ref[...] = n_ref[...].T
        sct = sc_ref[...].T
        log_f = jax.nn.log_sigmoid(sct[8:16])
        m_old = sct[16:24]
        m_new = jnp.maximum(log_f + m_old, sct[0:8])
        w_ref[0:8, :] = jnp.exp(log_f + m_old - m_new)
        w_ref[8:16, :] = jnp.exp(sct[0:8] - m_new)
        w_ref[16:24, :] = m_new

    st_ref[...] = c_ref[...].T
    w_carry = w_ref[pl.ds(h, 1), :]
    w_tok = w_ref[pl.ds(8 + h, 1), :]
    m_new = w_ref[pl.ds(16 + h, 1), :]
    kp_ref[...] = kt_ref[_head_rows(h, HD), :] * w_tok
    q_h = qt_ref.at[_head_rows(h, HD)]
    n_new = w_carry * nt_ref[_head_rows(h, HD), :] + kp_ref[...]
    num = _rank1_update(st_ref, snt_ref, q_h, kp_ref, lambda i: w_carry, vt_ref[_head_rows(h, HD), :], HD)
    den = jnp.sum(q_h[...] * n_new, axis=0, keepdims=True)
    ot_ref[_head_rows(h, HD), :] = num / jnp.maximum(jnp.abs(den), jnp.exp(-m_new))
    nnt_ref[_head_rows(h, HD), :] = n_new
    c_out_ref[...] = snt_ref[...].T

    @pl.when(h == N_HEADS - 1)
    def _():
        y_ref[...] = _head_norm(jax.nn.sigmoid(x_ref[:, 768:1024]) * ot_ref[...].T, g_ref[...])
        n_out_ref[...] = nnt_ref[...].T
        m_out_ref[...] = jnp.concatenate([w_ref[16:24, :], jnp.zeros((LANES - 8, LANES), F32)], axis=0).T


def _ml_sample(ml, scal, n_state, c_state, lw):
    per_head = HD * HD
    full = lambda shape: pl.BlockSpec(shape, lambda h: (0,) * len(shape))
    vm = lambda r: pltpu.VMEM((r, LANES), F32)
    return pl.pallas_call(
        _ml_sample_kernel,
        grid=(N_HEADS,),
        in_specs=[full((LANES, W_ML)), full((LANES, 128)), full((LANES, BRANCH_W)), full((1, BRANCH_W)),
                  pl.BlockSpec((LANES, per_head), lambda h: (0, h))],
        out_specs=[full((LANES, BRANCH_W)), pl.BlockSpec((LANES, per_head), lambda h: (0, h)),
                   full((LANES, BRANCH_W)), full((LANES, 128))],
        out_shape=[jax.ShapeDtypeStruct((LANES, BRANCH_W), F32), jax.ShapeDtypeStruct(c_state.shape, F32),
                   jax.ShapeDtypeStruct((LANES, BRANCH_W), F32), jax.ShapeDtypeStruct((LANES, 128), F32)],
        scratch_shapes=[vm(256), vm(256), vm(256), vm(256), vm(256), vm(256), vm(24), vm(HD), vm(per_head), vm(per_head)],
        compiler_params=_cparams(("arbitrary",)),
        name="ml_sample",
    )(ml, scal, n_state, lw["ml_g"], c_state)


def _ret_sample_kernel(x_ref, cos_ref, sin_ref, gam_ref, g_ref, s_ref, y_ref, s_out_ref,
                       qt_ref, kt_ref, vt_ref, ot_ref, qr_ref, kr_ref, st_ref, snt_ref):
    h = pl.program_id(0)

    @pl.when(h == 0)
    def _():
        x = x_ref[...]
        qt_ref[...] = x[:, 0:256].T
        kt_ref[...] = (x[:, 256:512] * (HD ** -0.5)).T
        vt_ref[...] = x[:, 512:768].T

    cos, sin = cos_ref[...], sin_ref[...]
    for src, dst in ((qt_ref, qr_ref), (kt_ref, kr_ref)):
        x1 = src[_head_rows(h, 32), :]
        x2 = src[pl.ds(pl.multiple_of(128 + h * 32, 32), 32), :]
        dst[0:32, :] = x1 * cos - x2 * sin
        dst[32:64, :] = x1 * sin + x2 * cos
    st_ref[...] = s_ref[...].T
    gamma = gam_ref[pl.ds(h, 1), :]
    ot_ref[_head_rows(h, HD), :] = _rank1_update(st_ref, snt_ref, qr_ref, kr_ref, lambda i: gamma,
                                                 vt_ref[_head_rows(h, HD), :], HD)
    s_out_ref[...] = snt_ref[...].T

    @pl.when(h == N_HEADS - 1)
    def _():
        y_ref[...] = jax.nn.silu(x_ref[:, 768:1024]) * _head_norm(ot_ref[...].T, g_ref[...])


def _ret_sample(ret, state, lw):
    per_head = HD * HD
    half = HD // 2
    freq = ROPE_BASE ** (-jnp.arange(half, dtype=F32) / half)
    ang = jnp.full((1,), PAST_LEN, F32)[:, None] * freq[None, :]
    cos = jnp.broadcast_to(jnp.cos(ang).reshape(half, 1), (half, LANES))
    sin = jnp.broadcast_to(jnp.sin(ang).reshape(half, 1), (half, LANES))
    log_g = jnp.log1p(-jnp.exp2(-5.0 - jnp.arange(N_HEADS, dtype=F32)))
    gam = jnp.broadcast_to(jnp.concatenate([jnp.exp(log_g), jnp.ones((4,), F32)]).reshape(8, 1), (8, LANES))
    full = lambda shape: pl.BlockSpec(shape, lambda h: (0,) * len(shape))
    vm = lambda r: pltpu.VMEM((r, LANES), F32)
    return pl.pallas_call(
        _ret_sample_kernel,
        grid=(N_HEADS,),
        in_specs=[full((LANES, W_RET)), full((half, LANES)), full((half, LANES)), full((8, LANES)), full((1, BRANCH_W)),
                  pl.BlockSpec((LANES, per_head), lambda h: (0, h))],
        out_specs=[full((LANES, BRANCH_W)), pl.BlockSpec((LANES, per_head), lambda h: (0, h))],
        out_shape=[jax.ShapeDtypeStruct((LANES, BRANCH_W), F32), jax.ShapeDtypeStruct(state.shape, F32)],
        scratch_shapes=[vm(256), vm(256), vm(256), vm(256), vm(HD), vm(HD), vm(per_head), vm(per_head)],
        compiler_params=_cparams(("arbitrary",)),
        name="ret_sample",
    )(ret, cos, sin, gam, lw["ret_g"], state)


def _swa_sample_kernel(q_ref, kn_ref, vn_ref, sink_ref, kc_ref, vc_ref, o_ref, ko_ref, vo_ref):
    q = q_ref[...]
    kc, vc = kc_ref[...], vc_ref[...]
    kn, vn = kn_ref[...], vn_ref[...]
    scale = HD ** -0.5
    s = _einsum("bhd,bwd->bhw", q.astype(BF16), kc.astype(BF16)) * scale
    s_new = jnp.sum(q * kn, axis=-1, keepdims=True) * scale
    w = lax.broadcasted_iota(jnp.int32, (1, 1, WINDOW), 2)
    s = jnp.where(w > 0, s, -jnp.inf)
    sink = sink_ref[...][:, 0:1]
    mx = jnp.maximum(jnp.maximum(jnp.max(s, axis=-1, keepdims=True), s_new), sink)
    p = jnp.exp(s - mx)
    p_new = jnp.exp(s_new - mx)
    den = jnp.sum(p, axis=-1, keepdims=True) + p_new + jnp.exp(sink - mx)
    o = _einsum("bhw,bwd->bhd", p.astype(BF16), vc.astype(BF16)) + p_new * vn
    o_ref[...] = o / den
    ko_ref[:, 0:WINDOW - 1, :] = kc[:, 1:WINDOW, :]
    ko_ref[:, WINDOW - 1:WINDOW, :] = kn
    vo_ref[:, 0:WINDOW - 1, :] = vc[:, 1:WINDOW, :]
    vo_ref[:, WINDOW - 1:WINDOW, :] = vn


def _swa_sample(swa, cache_k, cache_v, lw, bt=16):
    bsz = swa.shape[0]
    lo = (jnp.arange(128) < HD)[None, :]
    qa, qb = swa[:, 0:128], swa[:, 128:256]
    q = jnp.stack([jnp.where(lo, qa, 0.0), jnp.where(lo, qb, 0.0), jnp.where(lo, 0.0, qa), jnp.where(lo, 0.0, qb)], axis=1)
    kn = swa[:, None, 256:384]
    vn = swa[:, None, 384:512]
    sink = jnp.broadcast_to(lw["sinks"][0, 0:N_HEADS].reshape(N_HEADS, 1), (N_HEADS, 128))
    blk = lambda shape: pl.BlockSpec(shape, lambda i: (i,) + (0,) * (len(shape) - 1))
    o, ko, vo = pl.pallas_call(
        _swa_sample_kernel,
        grid=(bsz // bt,),
        in_specs=[blk((bt, N_HEADS, 128)), blk((bt, 1, 128)), blk((bt, 1, 128)),
                  pl.BlockSpec((N_HEADS, 128), lambda i: (0, 0)),
                  blk((bt, WINDOW, 128)), blk((bt, WINDOW, 128))],
        out_specs=[blk((bt, N_HEADS, 128)), blk((bt, WINDOW, 128)), blk((bt, WINDOW, 128))],
        out_shape=[jax.ShapeDtypeStruct((bsz, N_HEADS, 128), F32), jax.ShapeDtypeStruct(cache_k.shape, F32),
                   jax.ShapeDtypeStruct(cache_v.shape, F32)],
        compiler_params=_cparams(("parallel",)),
        name="swa_sample",
    )(q, kn, vn, sink, cache_k, cache_v)
    y = jnp.concatenate([o[:, 0, 0:64], o[:, 2, 64:128], o[:, 1, 0:64], o[:, 3, 64:128]], axis=1)
    return y, ko, vo


def _ml_scalars(small, m_state):
    z4 = jnp.zeros((small.shape[0], 4), F32)
    return jnp.concatenate([small[:, ML_I_LANE:ML_I_LANE + 4], z4, small[:, ML_F_LANE:ML_F_LANE + 4], z4, m_state, z4,
                            jnp.zeros((small.shape[0], 128 - 24), F32)], axis=1)


def _take_cols(a, pieces, pad):
    parts = [a[..., s:s + n] for s, n in pieces]
    if pad:
        parts.append(jnp.zeros(a.shape[:-1] + (pad,), a.dtype))
    return jnp.concatenate(parts, axis=-1)


def _mix_pieces():
    p = [(0, 768)]
    sq = 784
    p += [(sq, 64), (sq + 128, 64), (sq + 64, 64), (sq + 192, 64)]
    p += [(1040, 256)]
    p += [(1296, 768), (2072, 256)]
    for base in (2328, 2584):
        for half in range(2):
            p += [(base + h * HD + half * 32, 32) for h in range(N_HEADS)]
    p += [(2840, 512)]
    p += [(768, 16), (2064, 8)]
    return p


def _prep_layer(li, w_in, b_in, gla_w_a2, gla_b_a2, gla_norm_g, swa_sinks, ml_norm_g, ret_norm_g, w_branch, w_out,
                ln1_g, ln1_b, w_router_grp, b_router_grp, w_router_exp, b_router_exp, w_exp_gate, w_exp_up,
                w_exp_down, ln2_g, ln2_b):
    pieces = _mix_pieces()
    pad = W_MIX - sum(n for _, n in pieces)
    row = lambda v: v.reshape(1, -1).astype(F32)
    wr = jnp.concatenate([w_router_grp[li], w_router_exp[li],
                          jnp.zeros((D_MODEL, 128 - N_GROUPS - N_EXPERTS), F32)], axis=1)
    wr_hi = wr.astype(BF16)
    wb = w_branch[li]
    wb1 = jnp.concatenate([wb[1, 0:64], wb[1, 128:192], wb[1, 64:128], wb[1, 192:256]], axis=0)
    return dict(
        wm=_take_cols(w_in[li], pieces, pad).astype(BF16),
        bm=row(_take_cols(b_in[li], pieces, pad)),
        wg=w_in[li][:, GATE_COL:].astype(BF16),
        bg=row(b_in[li][GATE_COL:]),
        wa2=jnp.concatenate([gla_w_a2[li], jnp.zeros((128 - GLA_RANK, 128), F32)], axis=0).astype(BF16),
        ba2=row(gla_b_a2[li]),
        gla_g=row(gla_norm_g[li]), ml_g=row(ml_norm_g[li]), ret_g=row(ret_norm_g[li]),
        sinks=row(jnp.concatenate([swa_sinks[li], jnp.zeros((128 - N_HEADS,), F32)])),
        wbr=jnp.stack([wb[0], wb1, wb[2], wb[3]]).astype(BF16),
        wout=w_out[li].astype(BF16),
        ln1_g=row(ln1_g[li]), ln1_b=row(ln1_b[li]), ln2_g=row(ln2_g[li]), ln2_b=row(ln2_b[li]),
        wr_hi=wr_hi, wr_lo=(wr - wr_hi.astype(F32)).astype(BF16),
        br=row(jnp.concatenate([b_router_grp[li], b_router_exp[li], jnp.zeros((128 - N_GROUPS - N_EXPERTS,), F32)])),
        we_gate=w_exp_gate[li].astype(BF16), we_up=w_exp_up[li].astype(BF16), we_down=w_exp_down[li].astype(BF16),
    )


TOK_TILE = 256
MOE_ROWS_PROMPT = 256
MOE_ROWS_SAMPLE = 32


def _ffn_half(x2, ys, lw, tm, bm):
    x1, route, cnt = _merge(x2, ys, lw, tm)
    return _moe(x1, route, cnt, lw, bm, tm)


def _layer_prompt(x2, lw, bsz, seq):
    gla, swa, ml, ret, small = _inproj(x2, lw["wm"], lw["bm"], TOK_TILE)
    ya, s_gla = _gla_prompt(gla, small, lw, bsz, seq)
    yb = _swa_prompt(swa, lw, bsz, seq)
    yc, c_bd, n_row, m_row = _ml_prompt(ml, small, lw, bsz, seq)
    yd, s_ret = _ret_prompt(ret, lw, bsz, seq)
    kv_tail = swa.reshape(bsz, seq, W_SWA)[:, seq - WINDOW:, 256:512]
    states = (_gla_state_from_blockdiag(s_gla),
              kv_tail[:, :, 0:128].reshape(bsz, WINDOW, SWA_KV, HD),
              kv_tail[:, :, 128:256].reshape(bsz, WINDOW, SWA_KV, HD),
              _diag_blocks(c_bd), n_row.reshape(bsz, N_HEADS, HD), m_row[:, 0, 0:N_HEADS],
              _ret_state_from_blockdiag(s_ret))
    return _ffn_half(x2, (ya, yb, yc, yd), lw, TOK_TILE, MOE_ROWS_PROMPT), states


def _layer_sample(x2, st, lw):
    s_gla, k_buf, v_buf, c_ml, n_ml, m_ml, s_ret = st
    bsz = x2.shape[0]
    gla, swa, ml, ret, small = _inproj(x2, lw["wm"], lw["bm"], bsz)
    ya, s_gla_new = _gla_sample(gla, small, s_gla.reshape(bsz, -1), lw)
    yb, k_new, v_new = _swa_sample(swa, k_buf.reshape(bsz, WINDOW, SWA_KV * HD), v_buf.reshape(bsz, WINDOW, SWA_KV * HD), lw)
    yc, c_new, n_new, m_new = _ml_sample(ml, _ml_scalars(small, m_ml), n_ml.reshape(bsz, -1), c_ml.reshape(bsz, -1), lw)
    yd, s_ret_new = _ret_sample(ret, s_ret.reshape(bsz, -1), lw)
    states = (s_gla_new.reshape(s_gla.shape), k_new.reshape(k_buf.shape), v_new.reshape(v_buf.shape),
              c_new.reshape(c_ml.shape), n_new.reshape(n_ml.shape), m_new[:, 0:N_HEADS], s_ret_new.reshape(s_ret.shape))
    return _ffn_half(x2, (ya, yb, yc, yd), lw, bsz, MOE_ROWS_SAMPLE), states


def kernel(x_prompt, x_sample, state_gla, cache_swa_k, cache_swa_v, state_mlstm_c, state_mlstm_n, state_mlstm_m,
           state_ret, w_in, b_in, gla_w_a2, gla_b_a2, gla_norm_g, swa_sinks, ml_norm_g, ret_norm_g, w_branch, w_out,
           ln1_g, ln1_b, w_router_grp, b_router_grp, w_router_exp, b_router_exp, w_exp_gate, w_exp_up, w_exp_down,
           ln2_g, ln2_b):
    bsz, seq, _ = x_prompt.shape
    dec_b, dec_seq, _ = x_sample.shape
    assert dec_seq == 1 and dec_b == LANES and seq % GLA_TILE == 0 and (bsz * seq) % TOK_TILE == 0
    weights = (w_in, b_in, gla_w_a2, gla_b_a2, gla_norm_g, swa_sinks, ml_norm_g, ret_norm_g, w_branch, w_out,
               ln1_g, ln1_b, w_router_grp, b_router_grp, w_router_exp, b_router_exp, w_exp_gate, w_exp_up,
               w_exp_down, ln2_g, ln2_b)
    y_p = x_prompt.reshape(bsz * seq, D_MODEL)
    y_s = x_sample.reshape(dec_b, D_MODEL)
    new_p, new_s = [], []
    for li in range(w_in.shape[0]):
        lw = _prep_layer(li, *weights)
        y_p, st_p = _layer_prompt(y_p, lw, bsz, seq)
        st_in = (state_gla[li], cache_swa_k[li], cache_swa_v[li], state_mlstm_c[li], state_mlstm_n[li],
                 state_mlstm_m[li], state_ret[li])
        y_s, st_s = _layer_sample(y_s, st_in, lw)
        new_p.append(st_p)
        new_s.append(st_s)
    outs = [y_p.reshape(bsz, seq, D_MODEL), y_s.reshape(dec_b, dec_seq, D_MODEL)]
    for i in range(7):
        outs.append(jnp.stack([s[i] for s in new_p], axis=0))
        outs.append(jnp.stack([s[i] for s in new_s], axis=0))
    return tuple(outs)
```

```python
import functools
import math

import jax
import jax.numpy as jnp
from jax import lax
from jax.experimental import pallas as pl
from jax.experimental.pallas import tpu as pltpu

F32 = jnp.float32
BF16 = jnp.bfloat16

D_MODEL = 1024
N_BRANCH = 4
BRANCH_W = D_MODEL // N_BRANCH
N_HEADS = 4
GLA_DK = 32
GLA_RANK = 16
GLA_TAU = 16.0
HD = 64
SWA_KV = 2
WINDOW = 128
ROPE_BASE = 10000.0
PAST_LEN = 16384
N_GROUPS = 4
EPG = 8
N_EXPERTS = N_GROUPS * EPG
D_EXPERT = D_MODEL // 4
DEPTH = 2
ALPHA = (2 * DEPTH) ** 0.25
LN_EPS = 1e-5
GATE_COL = 3352
VMEM_LIMIT = 56 * 1024 * 1024

W_GLA, W_SWA, W_ML, W_RET, W_SMALL = 768, 512, 1024, 1024, 128
W_MIX = W_GLA + W_SWA + W_ML + W_RET + W_SMALL


def _cparams(sem):
    return pltpu.CompilerParams(dimension_semantics=sem, vmem_limit_bytes=VMEM_LIMIT)


def _dot(a, b):
    return jnp.dot(a, b, preferred_element_type=F32)


def _einsum(spec, a, b):
    return jnp.einsum(spec, a, b, preferred_element_type=F32)


def _split2(x):
    hi = x.astype(BF16)
    lo = (x - hi.astype(F32)).astype(BF16)
    return hi, lo


def _split3(x):
    hi = x.astype(BF16)
    r = x - hi.astype(F32)
    mid = r.astype(BF16)
    lo = (r - mid.astype(F32)).astype(BF16)
    return hi, mid, lo


def _dot_exact_lhs(a_exact, b):
    return sum(_dot(a_exact, p) for p in _split3(b))


def _dot_exact_rhs(a, b_exact):
    return sum(_dot(p, b_exact) for p in _split3(a))


def _layer_norm(x, g, b):
    mu = jnp.mean(x, axis=-1, keepdims=True)
    xc = x - mu
    var = jnp.mean(xc * xc, axis=-1, keepdims=True)
    return xc * lax.rsqrt(var + LN_EPS) * g + b


def _inproj_kernel(x_ref, w_ref, b_ref, gla_ref, swa_ref, ml_ref, ret_ref, sm_ref):
    xb = x_ref[...].astype(BF16)
    off = 0
    for ref, width in ((gla_ref, W_GLA), (swa_ref, W_SWA), (ml_ref, W_ML), (ret_ref, W_RET), (sm_ref, W_SMALL)):
        ref[...] = _dot(xb, w_ref[:, off:off + width]) + b_ref[:, off:off + width]
        off += width


def _inproj(x2, wm, bm, tm):
    t = x2.shape[0]
    widths = (W_GLA, W_SWA, W_ML, W_RET, W_SMALL)
    return pl.pallas_call(
        _inproj_kernel,
        grid=(t // tm,),
        in_specs=[pl.BlockSpec((tm, D_MODEL), lambda i: (i, 0)),
                  pl.BlockSpec((D_MODEL, W_MIX), lambda i: (0, 0)),
                  pl.BlockSpec((1, W_MIX), lambda i: (0, 0))],
        out_specs=[pl.BlockSpec((tm, w), lambda i: (i, 0)) for w in widths],
        out_shape=[jax.ShapeDtypeStruct((t, w), F32) for w in widths],
        compiler_params=_cparams(("parallel",)),
        name="inproj",
    )(x2, wm, bm)


def _merge_kernel(x_ref, ya_ref, yb_ref, yc_ref, yd_ref, wg_ref, bg_ref, wbr_ref, wout_ref, g1_ref, b1_ref,
                  wrh_ref, wrl_ref, br_ref, x1_ref, route_ref, cnt_ref, xs_ref, *, tm):
    x = x_ref[...]
    xb = x.astype(BF16)
    merged = jnp.zeros((tm, D_MODEL), F32)
    for n, y_ref in enumerate((ya_ref, yb_ref, yc_ref, yd_ref)):
        gate = jax.nn.sigmoid(_dot(xb, wg_ref[:, n * D_MODEL:(n + 1) * D_MODEL]) + bg_ref[:, n * D_MODEL:(n + 1) * D_MODEL])
        merged = merged + gate * _dot(y_ref[...].astype(BF16), wbr_ref[n])
    mix = _dot(merged.astype(BF16), wout_ref[...])
    x1 = _layer_norm(ALPHA * x + mix, g1_ref[...], b1_ref[...])
    x1_ref[...] = x1

    xh, xl = _split2(x1)
    logits = _dot(xh, wrh_ref[...]) + _dot(xh, wrl_ref[...]) + _dot(xl, wrh_ref[...]) + br_ref[...]
    lane = lax.broadcasted_iota(jnp.int32, (tm, 128), 1)
    lanef = lane.astype(F32)
    neg = jnp.float32(-jnp.inf)
    big = jnp.float32(1e9)
    isg = lane < N_GROUPS
    gmax = jnp.max(jnp.where(isg, logits, neg), axis=-1, keepdims=True)
    gidx = jnp.min(jnp.where(isg & (logits == gmax), lanef, big), axis=-1, keepdims=True)
    p_group = 1.0 / jnp.sum(jnp.where(isg, jnp.exp(logits - gmax), 0.0), axis=-1, keepdims=True)
    lo = N_GROUPS + EPG * gidx
    em = (lanef >= lo) & (lanef < lo + EPG)
    v1 = jnp.max(jnp.where(em, logits, neg), axis=-1, keepdims=True)
    i1 = jnp.min(jnp.where(em & (logits == v1), lanef, big), axis=-1, keepdims=True)
    em2 = em & (lanef != i1)
    v2 = jnp.max(jnp.where(em2, logits, neg), axis=-1, keepdims=True)
    i2 = jnp.min(jnp.where(em2 & (logits == v2), lanef, big), axis=-1, keepdims=True)
    e21 = jnp.exp(v2 - v1)
    gt1 = 1.0 / (1.0 + e21)
    gt2 = e21 * gt1

    sel1 = lanef == i1
    sel2 = lanef == i2
    onehot = jnp.where(sel1 | sel2, 1.0, 0.0)
    cnt = jnp.sum(onehot, axis=0, keepdims=True)
    seg = jnp.floor((cnt + (PIECE - 1)) * (1.0 / PIECE)) * PIECE
    er = lax.broadcasted_iota(jnp.int32, (128, 128), 0)
    ec = lax.broadcasted_iota(jnp.int32, (128, 128), 1)
    seg_off = _dot(jnp.broadcast_to(seg, (8, 128)).astype(BF16), jnp.where(er < ec, 1.0, 0.0).astype(BF16))[0:1]
    row = lax.broadcasted_iota(jnp.int32, (tm, tm), 0)
    col = lax.broadcasted_iota(jnp.int32, (tm, tm), 1)
    earlier = _dot(jnp.where(col < row, 1.0, 0.0).astype(BF16), onehot.astype(BF16))
    where_to = seg_off + earlier
    pos1 = jnp.sum(jnp.where(sel1, where_to, 0.0), axis=-1, keepdims=True)
    pos2 = jnp.sum(jnp.where(sel2, where_to, 0.0), axis=-1, keepdims=True)
    rows_t = xs_ref.shape[0]
    r = lax.broadcasted_iota(jnp.int32, (tm, rows_t), 1).astype(F32)
    place = jnp.where((r == pos1) | (r == pos2), 1.0, 0.0).astype(BF16)
    xs_ref[...] = _dot_tn(place, x1.astype(BF16)).astype(BF16)
    cnt_ref[...] = cnt

    vals = (i1 - N_GROUPS, i2 - N_GROUPS, p_group * gt1, p_group * gt2, pos1, pos2)
    route = jnp.zeros((tm, 128), F32)
    for k, v in enumerate(vals):
        route = jnp.where(lane == k, v, route)
    route_ref[...] = route


PIECE = 16


def _tile_rows(tm):
    return -(-(2 * tm + N_EXPERTS * (PIECE - 1)) // 128) * 128


def _merge(x2, ys, lw, tm):
    t = x2.shape[0]
    nt = t // tm
    rows_t = _tile_rows(tm)
    tok = lambda w: pl.BlockSpec((tm, w), lambda i: (i, 0))
    full = lambda shape: pl.BlockSpec(shape, lambda i: (0,) * len(shape))
    return pl.pallas_call(
        functools.partial(_merge_kernel, tm=tm),
        grid=(t // tm,),
        in_specs=[tok(D_MODEL), tok(BRANCH_W), tok(BRANCH_W), tok(BRANCH_W), tok(BRANCH_W),
                  full((D_MODEL, N_BRANCH * D_MODEL)), full((1, N_BRANCH * D_MODEL)),
                  full((N_BRANCH, BRANCH_W, D_MODEL)), full((D_MODEL, D_MODEL)),
                  full((1, D_MODEL)), full((1, D_MODEL)),
                  full((D_MODEL, 128)), full((D_MODEL, 128)), full((1, 128))],
        out_specs=[tok(D_MODEL), tok(128), pl.BlockSpec((None, 1, 128), lambda i: (i, 0, 0)),
                   pl.BlockSpec((rows_t, D_MODEL), lambda i: (i, 0))],
        out_shape=[jax.ShapeDtypeStruct((t, D_MODEL), F32), jax.ShapeDtypeStruct((t, 128), F32),
                   jax.ShapeDtypeStruct((nt, 1, 128), F32), jax.ShapeDtypeStruct((nt * rows_t, D_MODEL), BF16)],
        compiler_params=_cparams(("parallel",)),
        name="merge",
    )(x2, *ys, lw["wg"], lw["bg"], lw["wbr"], lw["wout"], lw["ln1_g"], lw["ln1_b"], lw["wr_hi"], lw["wr_lo"], lw["br"])


def _piece_copy(hbm, hbm_row, vmem, piece, sem, to_vmem):
    h = hbm.at[pl.ds(pl.multiple_of(hbm_row, PIECE), PIECE), :]
    v = vmem.at[pl.ds(pl.multiple_of(piece * PIECE, PIECE), PIECE), :]
    return pltpu.make_async_copy(h, v, sem) if to_vmem else pltpu.make_async_copy(v, h, sem)


def _expert_kernel(be_ref, nu_ref, nv_ref, off_ref, xs_hbm, wg_ref, wu_ref, wd_ref, ys_hbm, xin, yout, sem_in, sem_out,
                   *, bp):
    del xs_hbm
    blk = pl.program_id(0)
    n_used = nu_ref[0]
    slot = blk % 2

    def pieces(b, s, buf, sem, to_vmem, start):
        def body(p, c):
            cp = _piece_copy(ys_hbm, off_ref[b * bp + p], buf.at[s], p, sem.at[s], to_vmem)
            cp.start() if start else cp.wait()
            return c
        lax.fori_loop(0, nv_ref[b], body, 0)

    @pl.when(blk == 0)
    def _():
        xin[...] = jnp.zeros_like(xin)

        @pl.when(n_used > 0)
        def _():
            pieces(0, 0, xin, sem_in, True, True)

    @pl.when(blk + 1 < n_used)
    def _():
        pieces(blk + 1, 1 - slot, xin, sem_in, True, True)

    @pl.when(blk < n_used)
    def _():
        pieces(blk, slot, xin, sem_in, True, False)

        @pl.when(blk >= 2)
        def _():
            pieces(blk - 2, slot, yout, sem_out, False, False)

        xb = xin[slot]
        hid = jax.nn.silu(_dot(xb, wg_ref[...])) * _dot(xb, wu_ref[...])
        yout[slot] = _dot(hid.astype(BF16), wd_ref[...]).astype(BF16)
        pieces(blk, slot, yout, sem_out, False, True)

    @pl.when(blk == n_used - 1)
    def _():
        @pl.when(blk >= 1)
        def _():
            pieces(blk - 1, 1 - slot, yout, sem_out, False, False)
        pieces(blk, slot, yout, sem_out, False, False)


def _experts(xs, block_expert, n_used, n_valid, piece_off, lw, bp):
    n_blocks = block_expert.shape[0]
    rows = bp * PIECE
    wspec = lambda shape: pl.BlockSpec((None,) + shape, lambda b, be, nu, nv, off: (be[b], 0, 0))
    grid_spec = pltpu.PrefetchScalarGridSpec(
        num_scalar_prefetch=4,
        grid=(n_blocks,),
        in_specs=[pl.BlockSpec(memory_space=pl.ANY), wspec((D_MODEL, D_EXPERT)), wspec((D_MODEL, D_EXPERT)),
                  wspec((D_EXPERT, D_MODEL))],
        out_specs=pl.BlockSpec(memory_space=pl.ANY),
        scratch_shapes=[pltpu.VMEM((2, rows, D_MODEL), BF16), pltpu.VMEM((2, rows, D_MODEL), BF16),
                        pltpu.SemaphoreType.DMA((2,)), pltpu.SemaphoreType.DMA((2,))],
    )
    return pl.pallas_call(
        functools.partial(_expert_kernel, bp=bp),
        grid_spec=grid_spec,
        out_shape=jax.ShapeDtypeStruct(xs.shape, BF16),
        input_output_aliases={4: 0},
        compiler_params=_cparams(("arbitrary",)),
        name="experts",
    )(block_expert, n_used, n_valid, piece_off, xs, lw["we_gate"], lw["we_up"], lw["we_down"])


def _combine_kernel(ys_ref, x1_ref, route_ref, g2_ref, b2_ref, out_ref, *, tm):
    route = route_ref[...]
    ys = ys_ref[...]
    r = lax.broadcasted_iota(jnp.int32, (tm, ys.shape[0]), 1).astype(F32)
    ffn = jnp.zeros((tm, D_MODEL), F32)
    for j in range(2):
        pick = jnp.where(r == route[:, 4 + j:5 + j], 1.0, 0.0).astype(BF16)
        ffn = ffn + route[:, 2 + j:3 + j] * _dot(pick, ys)
    out_ref[...] = _layer_norm(ALPHA * x1_ref[...] + ffn, g2_ref[...], b2_ref[...])


def _combine(ys, x1, route, lw, tm):
    t = x1.shape[0]
    rows_t = _tile_rows(tm)
    full = lambda shape: pl.BlockSpec(shape, lambda i: (0,) * len(shape))
    return pl.pallas_call(
        functools.partial(_combine_kernel, tm=tm),
        grid=(t // tm,),
        in_specs=[pl.BlockSpec((rows_t, D_MODEL), lambda i: (i, 0)), pl.BlockSpec((tm, D_MODEL), lambda i: (i, 0)),
                  pl.BlockSpec((tm, 128), lambda i: (i, 0)), full((1, D_MODEL)), full((1, D_MODEL))],
        out_specs=pl.BlockSpec((tm, D_MODEL), lambda i: (i, 0)),
        out_shape=jax.ShapeDtypeStruct((t, D_MODEL), F32),
        compiler_params=_cparams(("parallel",)),
        name="combine",
    )(ys, x1, route, lw["ln2_g"], lw["ln2_b"])


def _piece_tables(cnt, tm, bp):
    nt = cnt.shape[0]
    rows_t = _tile_rows(tm)
    counts = cnt[:, 0, N_GROUPS:N_GROUPS + N_EXPERTS].astype(jnp.int32)
    pcs = (counts + PIECE - 1) // PIECE
    seg_off = (jnp.cumsum(pcs, axis=1) - pcs) * PIECE
    tile_end = jnp.cumsum(pcs, axis=0).T
    total = tile_end[:, -1]
    padded = (total + bp - 1) // bp * bp
    e_end = jnp.cumsum(padded)
    e_start = e_end - padded
    n_blocks = -(-(nt * rows_t // PIECE + N_EXPERTS * (bp - 1)) // bp)
    g = jnp.arange(n_blocks * bp, dtype=jnp.int32)
    e = jnp.minimum(jnp.sum(e_end[None, :] <= g[:, None], axis=1), N_EXPERTS - 1)
    local = g - e_start[e]
    ends = tile_end[e]
    tile = jnp.minimum(jnp.sum(ends <= local[:, None], axis=1), nt - 1)
    first = jnp.take_along_axis(ends, tile[:, None], axis=1)[:, 0] - pcs[tile, e]
    off = tile * rows_t + seg_off[tile, e] + (local - first) * PIECE
    valid = local < total[e]
    off = jnp.where(valid, off, 0).astype(jnp.int32)
    n_valid = jnp.sum(valid.reshape(n_blocks, bp), axis=1).astype(jnp.int32)
    block_expert = e.reshape(n_blocks, bp)[:, 0].astype(jnp.int32)
    n_used = (e_end[-1] // bp).astype(jnp.int32).reshape(1)
    return block_expert, n_used, n_valid, off


def _moe(x1, route, cnt, xs, lw, bp, tm):
    ys = _experts(xs, *_piece_tables(cnt, tm, bp), lw, bp)
    return _combine(ys, x1, route, lw, tm)


CHUNK = 128


def _dot_nt(a, b):
    return lax.dot_general(a, b, (((1,), (1,)), ((), ())), preferred_element_type=F32)


def _dot_tn(a, b):
    return lax.dot_general(a, b, (((0,), (0,)), ((), ())), preferred_element_type=F32)


def _lane_head(width, group):
    return lax.broadcasted_iota(jnp.int32, (1, width), 1) // group


def _block_diag_mask(rows_head, cols_head):
    return rows_head == cols_head


def _head_norm(o, gain):
    r = lax.broadcasted_iota(jnp.int32, (BRANCH_W, BRANCH_W), 0) // HD
    c = lax.broadcasted_iota(jnp.int32, (BRANCH_W, BRANCH_W), 1) // HD
    avg = jnp.where(r == c, 1.0 / HD, 0.0).astype(BF16)
    mu = _dot_exact_rhs(o, avg)
    xc = o - mu
    var = _dot_exact_rhs(xc * xc, avg)
    return xc * lax.rsqrt(var + LN_EPS) * gain


def _stack_heads(x, head_of_lane):
    return jnp.concatenate([jnp.where(head_of_lane == h, x, 0.0) for h in range(N_HEADS)], axis=0)


def _pick_heads(stacked, rows, head_of_lane):
    out = jnp.where(head_of_lane == 0, stacked[0:rows], 0.0)
    for h in range(1, N_HEADS):
        out = jnp.where(head_of_lane == h, stacked[h * rows:(h + 1) * rows], out)
    return out


def _spread_heads(cols, head_of_lane):
    out = jnp.where(head_of_lane == 0, cols[0], 0.0)
    for h in range(1, N_HEADS):
        out = jnp.where(head_of_lane == h, cols[h], out)
    return out


SWA_BLOCKS = 4


def _swa_prompt_kernel(q_ref, kv_ref, kvp_ref, sink_ref, y_ref):
    n = pl.program_id(1)
    lo = lax.broadcasted_iota(jnp.int32, (1, 128), 1) < HD
    e = lax.broadcasted_iota(jnp.int32, (2 * WINDOW, N_HEADS * WINDOW), 0)
    col = lax.broadcasted_iota(jnp.int32, (2 * WINDOW, N_HEADS * WINDOW), 1)
    a = col % WINDOW
    band = (e > a) & (e <= a + WINDOW)
    head = lax.broadcasted_iota(jnp.int32, (1, N_HEADS * WINDOW), 1) // WINDOW
    sink = _spread_heads([sink_ref[:, h:h + 1] for h in range(N_HEADS)], head)
    for i in range(SWA_BLOCKS):
        rows = slice(i * WINDOW, (i + 1) * WINDOW)
        q = q_ref[rows, :]
        qa, qb = q[:, 0:128], q[:, 128:256]
        kv = kv_ref[rows, :]
        kvp = kvp_ref[...] if i == 0 else kv_ref[(i - 1) * WINDOW:i * WINDOW, :]
        kext = jnp.concatenate([kvp[:, 0:128], kv[:, 0:128]], axis=0).astype(BF16)
        vext = jnp.concatenate([kvp[:, 128:256], kv[:, 128:256]], axis=0).astype(BF16)
        qs = jnp.concatenate([jnp.where(lo, qa, 0.0), jnp.where(lo, qb, 0.0),
                              jnp.where(lo, 0.0, qa), jnp.where(lo, 0.0, qb)], axis=0).astype(BF16)
        valid = band & ((n > 0) | (e >= WINDOW)) if i == 0 else band
        s = jnp.where(valid, _dot_nt(kext, qs) * (HD ** -0.5), -jnp.inf)
        mx = jnp.maximum(jnp.max(s, axis=0, keepdims=True), sink)
        p = jnp.exp(s - mx)
        den = jnp.sum(p, axis=0, keepdims=True) + jnp.exp(sink - mx)
        o = _dot_tn(vext, p.astype(BF16)) * (1.0 / den)
        ya = jnp.concatenate([o[0:HD, 0:128], o[HD:128, 256:384]], axis=0).T
        yb = jnp.concatenate([o[0:HD, 128:256], o[HD:128, 384:512]], axis=0).T
        y_ref[rows, :] = jnp.concatenate([ya, yb], axis=1)


def _swa_prompt(swa, lw, bsz, seq):
    qt = SWA_BLOCKS * WINDOW
    nt = seq // qt
    return pl.pallas_call(
        _swa_prompt_kernel,
        grid=(bsz, nt),
        in_specs=[pl.BlockSpec((qt, 256), lambda b, n: (b * nt + n, 0)),
                  pl.BlockSpec((qt, 256), lambda b, n: (b * nt + n, 1)),
                  pl.BlockSpec((WINDOW, 256), lambda b, n: (jnp.maximum((b * nt + n) * SWA_BLOCKS - 1, 0), 1)),
                  pl.BlockSpec((1, 128), lambda b, n: (0, 0))],
        out_specs=pl.BlockSpec((qt, BRANCH_W), lambda b, n: (b * nt + n, 0)),
        out_shape=jax.ShapeDtypeStruct((bsz * seq, BRANCH_W), F32),
        compiler_params=_cparams(("parallel", "parallel")),
        name="swa_prompt",
    )(swa, swa, swa, lw["sinks"])


def _rotate(x, cos, sin):
    x1, x2 = x[:, 0:128], x[:, 128:256]
    return jnp.concatenate([x1 * cos - x2 * sin, x1 * sin + x2 * cos], axis=1)


def _ret_prompt_kernel(r_ref, cos_ref, sin_ref, dmat_ref, qdec_ref, kdec_ref, cdec_ref, g_ref, y_ref, s_out_ref, s_ref):
    n = pl.program_id(1)

    @pl.when(n == 0)
    def _():
        s_ref[...] = jnp.zeros_like(s_ref)

    c = CHUNK
    r = r_ref[...]
    cos, sin = cos_ref[...], sin_ref[...]
    qr = _rotate(r[:, 0:256], cos, sin)
    kr = _rotate(r[:, 256:512], cos, sin) * (HD ** -0.5)
    v = r[:, 512:768].astype(BF16)
    hq = _lane_head(BRANCH_W, 32) % N_HEADS
    hv = _lane_head(BRANCH_W, HD)
    att = _dot_nt(_stack_heads(qr, hq).astype(BF16), kr.astype(BF16)) * dmat_ref[...]
    o = _pick_heads(_dot(att.astype(BF16), v), c, hv)
    s = s_ref[...]
    o = o + _dot((qr * qdec_ref[...]).astype(BF16), s.astype(BF16))
    upd = _dot_tn((kr * kdec_ref[...]).astype(BF16), v)
    rows_head = (lax.broadcasted_iota(jnp.int32, (BRANCH_W, 1), 0) // 32) % N_HEADS
    s_new = cdec_ref[...] * s + jnp.where(rows_head == hv, upd, 0.0)
    s_ref[...] = s_new
    s_out_ref[...] = s_new
    y_ref[...] = jax.nn.silu(r[:, 768:1024]) * _head_norm(o, g_ref[...])


def _ret_tables(seq):
    c = CHUNK
    half = HD // 2
    freq = ROPE_BASE ** (-jnp.arange(half, dtype=F32) / half)
    ang = jnp.arange(seq).astype(F32)[:, None] * freq[None, :]
    cos = jnp.tile(jnp.cos(ang), (1, N_HEADS))
    sin = jnp.tile(jnp.sin(ang), (1, N_HEADS))
    log_g = jnp.log1p(-jnp.exp2(-5.0 - jnp.arange(N_HEADS, dtype=F32)))
    idx = jnp.arange(c, dtype=F32)
    rel = idx[:, None] - idx[None, :]
    dmat = jnp.exp(jnp.where(rel[None] >= 0, rel[None] * log_g[:, None, None], -jnp.inf)).reshape(N_HEADS * c, c)
    lane_head_q = (jnp.arange(BRANCH_W) // 32) % N_HEADS
    qdec = jnp.exp((idx + 1.0)[:, None] * log_g[lane_head_q][None, :])
    kdec = jnp.exp((c - 1.0 - idx)[:, None] * log_g[lane_head_q][None, :])
    cdec = jnp.exp(c * log_g)[jnp.arange(BRANCH_W) // HD][None, :]
    return cos, sin, dmat, qdec, kdec, cdec


def _ret_prompt(ret, lw, bsz, seq):
    c = CHUNK
    nc = seq // c
    cos, sin, dmat, qdec, kdec, cdec = _ret_tables(seq)
    full = lambda shape: pl.BlockSpec(shape, lambda b, n: (0,) * len(shape))
    return pl.pallas_call(
        _ret_prompt_kernel,
        grid=(bsz, nc),
        in_specs=[pl.BlockSpec((c, W_RET), lambda b, n: (b * nc + n, 0)),
                  pl.BlockSpec((c, 128), lambda b, n: (n, 0)), pl.BlockSpec((c, 128), lambda b, n: (n, 0)),
                  full((N_HEADS * c, c)), full((c, BRANCH_W)), full((c, BRANCH_W)), full((1, BRANCH_W)),
                  full((1, BRANCH_W))],
        out_specs=[pl.BlockSpec((c, BRANCH_W), lambda b, n: (b * nc + n, 0)),
                   pl.BlockSpec((None, BRANCH_W, BRANCH_W), lambda b, n: (b, 0, 0))],
        out_shape=[jax.ShapeDtypeStruct((bsz * seq, BRANCH_W), F32),
                   jax.ShapeDtypeStruct((bsz, BRANCH_W, BRANCH_W), F32)],
        scratch_shapes=[pltpu.VMEM((BRANCH_W, BRANCH_W), F32)],
        compiler_params=_cparams(("parallel", "arbitrary")),
        name="ret_prompt",
    )(ret, cos, sin, dmat, qdec, kdec, cdec, lw["ret_g"])


def _ret_state_from_blockdiag(s_bd):
    b = s_bd.shape[0]
    s = s_bd.reshape(b, 2, N_HEADS, 32, N_HEADS, HD)
    s = jnp.stack([s[:, :, h, :, h, :] for h in range(N_HEADS)], axis=1)
    return s.reshape(b, N_HEADS, HD, HD)


ML_I_LANE = GLA_RANK
ML_F_LANE = GLA_RANK + N_HEADS


def _ml_prompt_kernel(m_ref, sm_ref, g_ref, y_ref, c_out_ref, n_out_ref, m_out_ref, c_ref, n_ref, mm_ref):
    nchunk = pl.program_id(1)

    @pl.when(nchunk == 0)
    def _():
        c_ref[...] = jnp.zeros_like(c_ref)
        n_ref[...] = jnp.zeros_like(n_ref)
        mm_ref[...] = jnp.zeros_like(mm_ref)

    c = CHUNK
    x = m_ref[...]
    q = x[:, 0:256]
    ks = x[:, 256:512] * (HD ** -0.5)
    v = x[:, 512:768].astype(BF16)
    sm = sm_ref[...]
    smt = sm.T
    row = lax.broadcasted_iota(jnp.int32, (c, c), 0)
    col = lax.broadcasted_iota(jnp.int32, (c, c), 1)
    causal = col <= row
    tri_lo = jnp.where(causal, 1.0, 0.0).astype(BF16)
    tri_up = jnp.where(row <= col, 1.0, 0.0).astype(BF16)
    bcols = _dot_exact_lhs(tri_lo, jax.nn.log_sigmoid(sm))
    brows = _dot_exact_rhs(jax.nn.log_sigmoid(smt), tri_up)
    hv = _lane_head(BRANCH_W, HD)
    sc = _dot_nt(_stack_heads(q, hv).astype(BF16), ks.astype(BF16))
    mm = mm_ref[...]
    qk, w_inter, m_tok, rowsum, w_s, carry, m_new = [], [], [], [], [], [], []
    for h in range(N_HEADS):
        bcol = bcols[:, ML_F_LANE + h:ML_F_LANE + h + 1]
        brow = brows[ML_F_LANE + h:ML_F_LANE + h + 1, :]
        icol = sm[:, ML_I_LANE + h:ML_I_LANE + h + 1]
        irow = smt[ML_I_LANE + h:ML_I_LANE + h + 1, :]
        m_prev = mm[:, h:h + 1]
        d_log = jnp.where(causal, bcol - brow + irow, -jnp.inf)
        inter_log = bcol + m_prev
        m_t = jnp.maximum(inter_log, jnp.max(d_log, axis=-1, keepdims=True))
        qk_h = sc[h * c:(h + 1) * c] * jnp.exp(d_log - m_t)
        qk.append(qk_h)
        rowsum.append(jnp.sum(qk_h, axis=-1, keepdims=True))
        w_inter.append(jnp.exp(inter_log - m_t))
        m_tok.append(m_t)
        mn = m_t[c - 1:c, :]
        b_last = bcol[c - 1:c, :]
        m_new.append(mn)
        carry.append(jnp.exp(b_last + m_prev - mn))
        w_s.append(jnp.exp(b_last - bcol + icol - mn))
    cm = c_ref[...]
    nrow = n_ref[...]
    w_inter_b = _spread_heads(w_inter, hv)
    num = w_inter_b * _dot(q.astype(BF16), cm.astype(BF16)) + _pick_heads(
        _dot(jnp.concatenate(qk, axis=0).astype(BF16), v), c, hv)
    r = lax.broadcasted_iota(jnp.int32, (BRANCH_W, BRANCH_W), 0) // HD
    cc = lax.broadcasted_iota(jnp.int32, (BRANCH_W, BRANCH_W), 1) // HD
    same_head = r == cc
    qn = _dot_exact_rhs(q * nrow, jnp.where(same_head, 1.0, 0.0).astype(BF16))
    den = w_inter_b * qn + _spread_heads(rowsum, hv)
    hout = num / jnp.maximum(jnp.abs(den), _spread_heads([jnp.exp(-m) for m in m_tok], hv))
    y_ref[...] = _head_norm(jax.nn.sigmoid(x[:, 768:1024]) * hout, g_ref[...])

    khat = ks * _spread_heads(w_s, hv)
    carry_b = _spread_heads(carry, hv)
    c_new = carry_b * cm + jnp.where(same_head, _dot_tn(khat.astype(BF16), v), 0.0)
    n_new = carry_b * nrow + jnp.sum(khat, axis=0, keepdims=True)
    lane = lax.broadcasted_iota(jnp.int32, (1, 128), 1)
    m_vec = jnp.zeros((1, 128), F32)
    for h in range(N_HEADS):
        m_vec = jnp.where(lane == h, m_new[h], m_vec)
    c_ref[...] = c_new
    n_ref[...] = n_new
    mm_ref[...] = m_vec
    c_out_ref[...] = c_new
    n_out_ref[...] = n_new
    m_out_ref[...] = m_vec


def _ml_prompt(ml, small, lw, bsz, seq):
    c = CHUNK
    nc = seq // c
    return pl.pallas_call(
        _ml_prompt_kernel,
        grid=(bsz, nc),
        in_specs=[pl.BlockSpec((c, W_ML), lambda b, n: (b * nc + n, 0)),
                  pl.BlockSpec((c, W_SMALL), lambda b, n: (b * nc + n, 0)),
                  pl.BlockSpec((1, BRANCH_W), lambda b, n: (0, 0))],
        out_specs=[pl.BlockSpec((c, BRANCH_W), lambda b, n: (b * nc + n, 0)),
                   pl.BlockSpec((None, BRANCH_W, BRANCH_W), lambda b, n: (b, 0, 0)),
                   pl.BlockSpec((None, 1, BRANCH_W), lambda b, n: (b, 0, 0)),
                   pl.BlockSpec((None, 1, 128), lambda b, n: (b, 0, 0))],
        out_shape=[jax.ShapeDtypeStruct((bsz * seq, BRANCH_W), F32),
                   jax.ShapeDtypeStruct((bsz, BRANCH_W, BRANCH_W), F32),
                   jax.ShapeDtypeStruct((bsz, 1, BRANCH_W), F32),
                   jax.ShapeDtypeStruct((bsz, 1, 128), F32)],
        scratch_shapes=[pltpu.VMEM((BRANCH_W, BRANCH_W), F32), pltpu.VMEM((1, BRANCH_W), F32),
                        pltpu.VMEM((1, 128), F32)],
        compiler_params=_cparams(("parallel", "arbitrary")),
        name="ml_prompt",
    )(ml, small, lw["ml_g"])


def _diag_blocks(s_bd):
    b = s_bd.shape[0]
    s = s_bd.reshape(b, N_HEADS, HD, N_HEADS, HD)
    return jnp.stack([s[:, h, :, h, :] for h in range(N_HEADS)], axis=1)


GLA_TILE = 256
GLA_SUB = 16


def _gla_prompt_kernel(x_ref, sm_ref, wa2_ref, ba2_ref, g_ref, y_ref, s_out_ref, s_ref):
    n = pl.program_id(1)

    @pl.when(n == 0)
    def _():
        s_ref[...] = jnp.zeros_like(s_ref)

    tq, sc = GLA_TILE, GLA_SUB
    x = x_ref[...]
    q = x[:, 0:128] * (GLA_DK ** -0.5)
    k = x[:, 128:256]
    v = x[:, 256:512]
    log_a = jax.nn.log_sigmoid(_dot(sm_ref[...].astype(BF16), wa2_ref[...]) + ba2_ref[...]) / GLA_TAU
    r = lax.broadcasted_iota(jnp.int32, (tq, tq), 0)
    c = lax.broadcasted_iota(jnp.int32, (tq, tq), 1)
    same = (r // sc) == (c // sc)
    b = _dot_exact_lhs(jnp.where(same & (c <= r), 1.0, 0.0).astype(BF16), log_a)
    btot = _dot_exact_lhs(jnp.where(same, 1.0, 0.0).astype(BF16), log_a)
    qd = (q * jnp.exp(b)).astype(BF16)
    khat_t = (k * jnp.exp(btot - b)).T
    dec_t = jnp.exp(btot).T
    vb = v.astype(BF16)
    kh = lax.broadcasted_iota(jnp.int32, (128, 1), 0) // GLA_DK
    hv = _lane_head(BRANCH_W, HD)
    expand = jnp.where(kh == hv, 1.0, 0.0).astype(BF16)
    lane_chunk = lax.broadcasted_iota(jnp.int32, (1, tq), 1) // sc
    trow = lax.broadcasted_iota(jnp.int32, (sc, 1), 0)
    s = s_ref[...]
    outs = []
    for ci in range(tq // sc):
        lo = ci * sc
        qc, kc, bc, vc = q[lo:lo + sc], k[lo:lo + sc], b[lo:lo + sc], v[lo:lo + sc]
        o = _dot(qd[lo:lo + sc], s.astype(BF16))
        pair = [qc * kc[j:j + 1] * jnp.exp(jnp.where(trow >= j, bc - bc[j:j + 1], -jnp.inf)) for j in range(sc)]
        att = _dot(jnp.concatenate(pair, axis=0).astype(BF16), expand)
        for j in range(sc):
            o = o + att[j * sc:(j + 1) * sc] * vc[j:j + 1]
        outs.append(o)
        upd = _dot(jnp.where(lane_chunk == ci, khat_t, 0.0).astype(BF16), vb)
        s = dec_t[:, lo:lo + 1] * s + jnp.where(kh == hv, upd, 0.0)
    s_ref[...] = s
    s_out_ref[...] = s
    y_ref[...] = jax.nn.silu(x[:, 512:768]) * _head_norm(jnp.concatenate(outs, axis=0), g_ref[...])


def _gla_prompt(gla, small, lw, bsz, seq):
    tq = GLA_TILE
    nt = seq // tq
    full = lambda shape: pl.BlockSpec(shape, lambda b, n: (0,) * len(shape))
    return pl.pallas_call(
        _gla_prompt_kernel,
        grid=(bsz, nt),
        in_specs=[pl.BlockSpec((tq, W_GLA), lambda b, n: (b * nt + n, 0)),
                  pl.BlockSpec((tq, W_SMALL), lambda b, n: (b * nt + n, 0)),
                  full((128, 128)), full((1, 128)), full((1, BRANCH_W))],
        out_specs=[pl.BlockSpec((tq, BRANCH_W), lambda b, n: (b * nt + n, 0)),
                   pl.BlockSpec((None, 128, BRANCH_W), lambda b, n: (b, 0, 0))],
        out_shape=[jax.ShapeDtypeStruct((bsz * seq, BRANCH_W), F32),
                   jax.ShapeDtypeStruct((bsz, 128, BRANCH_W), F32)],
        scratch_shapes=[pltpu.VMEM((128, BRANCH_W), F32)],
        compiler_params=_cparams(("parallel", "arbitrary")),
        name="gla_prompt",
    )(gla, small, lw["wa2"], lw["ba2"], lw["gla_g"])


def _gla_state_from_blockdiag(s_bd):
    b = s_bd.shape[0]
    s = s_bd.reshape(b, N_HEADS, GLA_DK, N_HEADS, HD)
    return jnp.stack([s[:, h, :, h, :] for h in range(N_HEADS)], axis=1)


LANES = 128


def _rank1_update(st_ref, snt_ref, q_ref, k_ref, decay_row, v, dk):
    def body(i, o):
        rows = pl.ds(pl.multiple_of(i * HD, HD), HD)
        new = decay_row(i) * st_ref[rows, :] + k_ref[pl.ds(i, 1), :] * v
        snt_ref[rows, :] = new
        return o + q_ref[pl.ds(i, 1), :] * new
    return lax.fori_loop(0, dk, body, jnp.zeros((HD, LANES), F32), unroll=4)


def _head_rows(h, n):
    return pl.ds(pl.multiple_of(h * n, n), n)


def _gla_sample_kernel(x_ref, sm_ref, wa2_ref, ba2_ref, g_ref, s_ref, y_ref, s_out_ref,
                       qt_ref, kt_ref, at_ref, vt_ref, ot_ref, st_ref, snt_ref):
    h = pl.program_id(0)

    @pl.when(h == 0)
    def _():
        x = x_ref[...]
        qt_ref[...] = (x[:, 0:128] * (GLA_DK ** -0.5)).T
        kt_ref[...] = x[:, 128:256].T
        vt_ref[...] = x[:, 256:512].T
        log_a = jax.nn.log_sigmoid(_dot(sm_ref[...].astype(BF16), wa2_ref[...]) + ba2_ref[...]) / GLA_TAU
        at_ref[...] = jnp.exp(log_a).T

    st_ref[...] = s_ref[...].T
    base = h * GLA_DK
    o = _rank1_update(st_ref, snt_ref, qt_ref.at[_head_rows(h, GLA_DK)], kt_ref.at[_head_rows(h, GLA_DK)],
                      lambda i: at_ref[pl.ds(base + i, 1), :], vt_ref[_head_rows(h, HD), :], GLA_DK)
    ot_ref[_head_rows(h, HD), :] = o
    s_out_ref[...] = snt_ref[...].T

    @pl.when(h == N_HEADS - 1)
    def _():
        y_ref[...] = jax.nn.silu(x_ref[:, 512:768]) * _head_norm(ot_ref[...].T, g_ref[...])


def _gla_sample(gla, small, state, lw):
    per_head = GLA_DK * HD
    full = lambda shape: pl.BlockSpec(shape, lambda h: (0,) * len(shape))
    vm = lambda r: pltpu.VMEM((r, LANES), F32)
    return pl.pallas_call(
        _gla_sample_kernel,
        grid=(N_HEADS,),
        in_specs=[full((LANES, W_GLA)), full((LANES, W_SMALL)), full((128, 128)), full((1, 128)), full((1, BRANCH_W)),
                  pl.BlockSpec((LANES, per_head), lambda h: (0, h))],
        out_specs=[full((LANES, BRANCH_W)), pl.BlockSpec((LANES, per_head), lambda h: (0, h))],
        out_shape=[jax.ShapeDtypeStruct((LANES, BRANCH_W), F32), jax.ShapeDtypeStruct(state.shape, F32)],
        scratch_shapes=[vm(128), vm(128), vm(128), vm(256), vm(256), vm(per_head), vm(per_head)],
        compiler_params=_cparams(("arbitrary",)),
        name="gla_sample",
    )(gla, small, lw["wa2"], lw["ba2"], lw["gla_g"], state)


def _ml_sample_kernel(x_ref, sc_ref, n_ref, g_ref, c_ref, y_ref, c_out_ref, n_out_ref, m_out_ref,
                      qt_ref, kt_ref, vt_ref, nt_ref, nnt_ref, ot_ref, w_ref, kp_ref, st_ref, snt_ref):
    h = pl.program_id(0)

    @pl.when(h == 0)
    def _():
        x = x_ref[...]
        qt_ref[...] = x[:, 0:256].T
        kt_ref[...] = (x[:, 256:512] * (HD ** -0.5)).T
        vt_ref[...] = x[:, 512:768].T
        nt_ref[...] = n_ref[...].T
        sct = sc_ref[...].T
        log_f = jax.nn.log_sigmoid(sct[8:16])
        m_old = sct[16:24]
        m_new = jnp.maximum(log_f + m_old, sct[0:8])
        w_ref[0:8, :] = jnp.exp(log_f + m_old - m_new)
        w_ref[8:16, :] = jnp.exp(sct[0:8] - m_new)
        w_ref[16:24, :] = m_new

    st_ref[...] = c_ref[...].T
    w_carry = w_ref[pl.ds(h, 1), :]
    w_tok = w_ref[pl.ds(8 + h, 1), :]
    m_new = w_ref[pl.ds(16 + h, 1), :]
    kp_ref[...] = kt_ref[_head_rows(h, HD), :] * w_tok
    q_h = qt_ref.at[_head_rows(h, HD)]
    n_new = w_carry * nt_ref[_head_rows(h, HD), :] + kp_ref[...]
    num = _rank1_update(st_ref, snt_ref, q_h, kp_ref, lambda i: w_carry, vt_ref[_head_rows(h, HD), :], HD)
    den = jnp.sum(q_h[...] * n_new, axis=0, keepdims=True)
    ot_ref[_head_rows(h, HD), :] = num / jnp.maximum(jnp.abs(den), jnp.exp(-m_new))
    nnt_ref[_head_rows(h, HD), :] = n_new
    c_out_ref[...] = snt_ref[...].T

    @pl.when(h == N_HEADS - 1)
    def _():
        y_ref[...] = _head_norm(jax.nn.sigmoid(x_ref[:, 768:1024]) * ot_ref[...].T, g_ref[...])
        n_out_ref[...] = nnt_ref[...].T
        m_out_ref[...] = jnp.concatenate([w_ref[16:24, :], jnp.zeros((LANES - 8, LANES), F32)], axis=0).T


def _ml_sample(ml, scal, n_state, c_state, lw):
    per_head = HD * HD
    full = lambda shape: pl.BlockSpec(shape, lambda h: (0,) * len(shape))
    vm = lambda r: pltpu.VMEM((r, LANES), F32)
    return pl.pallas_call(
        _ml_sample_kernel,
        grid=(N_HEADS,),
        in_specs=[full((LANES, W_ML)), full((LANES, 128)), full((LANES, BRANCH_W)), full((1, BRANCH_W)),
                  pl.BlockSpec((LANES, per_head), lambda h: (0, h))],
        out_specs=[full((LANES, BRANCH_W)), pl.BlockSpec((LANES, per_head), lambda h: (0, h)),
                   full((LANES, BRANCH_W)), full((LANES, 128))],
        out_shape=[jax.ShapeDtypeStruct((LANES, BRANCH_W), F32), jax.ShapeDtypeStruct(c_state.shape, F32),
                   jax.ShapeDtypeStruct((LANES, BRANCH_W), F32), jax.ShapeDtypeStruct((LANES, 128), F32)],
        scratch_shapes=[vm(256), vm(256), vm(256), vm(256), vm(256), vm(256), vm(24), vm(HD), vm(per_head), vm(per_head)],
        compiler_params=_cparams(("arbitrary",)),
        name="ml_sample",
    )(ml, scal, n_state, lw["ml_g"], c_state)


def _ret_sample_kernel(x_ref, cos_ref, sin_ref, gam_ref, g_ref, s_ref, y_ref, s_out_ref,
                       qt_ref, kt_ref, vt_ref, ot_ref, qr_ref, kr_ref, st_ref, snt_ref):
    h = pl.program_id(0)

    @pl.when(h == 0)
    def _():
        x = x_ref[...]
        qt_ref[...] = x[:, 0:256].T
        kt_ref[...] = (x[:, 256:512] * (HD ** -0.5)).T
        vt_ref[...] = x[:, 512:768].T

    cos, sin = cos_ref[...], sin_ref[...]
    for src, dst in ((qt_ref, qr_ref), (kt_ref, kr_ref)):
        x1 = src[_head_rows(h, 32), :]
        x2 = src[pl.ds(pl.multiple_of(128 + h * 32, 32), 32), :]
        dst[0:32, :] = x1 * cos - x2 * sin
        dst[32:64, :] = x1 * sin + x2 * cos
    st_ref[...] = s_ref[...].T
    gamma = gam_ref[pl.ds(h, 1), :]
    ot_ref[_head_rows(h, HD), :] = _rank1_update(st_ref, snt_ref, qr_ref, kr_ref, lambda i: gamma,
                                                 vt_ref[_head_rows(h, HD), :], HD)
    s_out_ref[...] = snt_ref[...].T

    @pl.when(h == N_HEADS - 1)
    def _():
        y_ref[...] = jax.nn.silu(x_ref[:, 768:1024]) * _head_norm(ot_ref[...].T, g_ref[...])


def _ret_sample(ret, state, lw):
    per_head = HD * HD
    half = HD // 2
    freq = ROPE_BASE ** (-jnp.arange(half, dtype=F32) / half)
    ang = jnp.full((1,), PAST_LEN, F32)[:, None] * freq[None, :]
    cos = jnp.broadcast_to(jnp.cos(ang).reshape(half, 1), (half, LANES))
    sin = jnp.broadcast_to(jnp.sin(ang).reshape(half, 1), (half, LANES))
    log_g = jnp.log1p(-jnp.exp2(-5.0 - jnp.arange(N_HEADS, dtype=F32)))
    gam = jnp.broadcast_to(jnp.concatenate([jnp.exp(log_g), jnp.ones((4,), F32)]).reshape(8, 1), (8, LANES))
    full = lambda shape: pl.BlockSpec(shape, lambda h: (0,) * len(shape))
    vm = lambda r: pltpu.VMEM((r, LANES), F32)
    return pl.pallas_call(
        _ret_sample_kernel,
        grid=(N_HEADS,),
        in_specs=[full((LANES, W_RET)), full((half, LANES)), full((half, LANES)), full((8, LANES)), full((1, BRANCH_W)),
                  pl.BlockSpec((LANES, per_head), lambda h: (0, h))],
        out_specs=[full((LANES, BRANCH_W)), pl.BlockSpec((LANES, per_head), lambda h: (0, h))],
        out_shape=[jax.ShapeDtypeStruct((LANES, BRANCH_W), F32), jax.ShapeDtypeStruct(state.shape, F32)],
        scratch_shapes=[vm(256), vm(256), vm(256), vm(256), vm(HD), vm(HD), vm(per_head), vm(per_head)],
        compiler_params=_cparams(("arbitrary",)),
        name="ret_sample",
    )(ret, cos, sin, gam, lw["ret_g"], state)


def _swa_sample_kernel(q_ref, kn_ref, vn_ref, sink_ref, kc_ref, vc_ref, o_ref, ko_ref, vo_ref):
    q = q_ref[...]
    kc, vc = kc_ref[...], vc_ref[...]
    kn, vn = kn_ref[...], vn_ref[...]
    scale = HD ** -0.5
    s = _einsum("bhd,bwd->bhw", q.astype(BF16), kc.astype(BF16)) * scale
    s_new = jnp.sum(q * kn, axis=-1, keepdims=True) * scale
    w = lax.broadcasted_iota(jnp.int32, (1, 1, WINDOW), 2)
    s = jnp.where(w > 0, s, -jnp.inf)
    sink = sink_ref[...][:, 0:1]
    mx = jnp.maximum(jnp.maximum(jnp.max(s, axis=-1, keepdims=True), s_new), sink)
    p = jnp.exp(s - mx)
    p_new = jnp.exp(s_new - mx)
    den = jnp.sum(p, axis=-1, keepdims=True) + p_new + jnp.exp(sink - mx)
    o = _einsum("bhw,bwd->bhd", p.astype(BF16), vc.astype(BF16)) + p_new * vn
    o_ref[...] = o / den
    ko_ref[:, 0:WINDOW - 1, :] = kc[:, 1:WINDOW, :]
    ko_ref[:, WINDOW - 1:WINDOW, :] = kn
    vo_ref[:, 0:WINDOW - 1, :] = vc[:, 1:WINDOW, :]
    vo_ref[:, WINDOW - 1:WINDOW, :] = vn


def _swa_sample(swa, cache_k, cache_v, lw, bt=16):
    bsz = swa.shape[0]
    lo = (jnp.arange(128) < HD)[None, :]
    qa, qb = swa[:, 0:128], swa[:, 128:256]
    q = jnp.stack([jnp.where(lo, qa, 0.0), jnp.where(lo, qb, 0.0), jnp.where(lo, 0.0, qa), jnp.where(lo, 0.0, qb)], axis=1)
    kn = swa[:, None, 256:384]
    vn = swa[:, None, 384:512]
    sink = jnp.broadcast_to(lw["sinks"][0, 0:N_HEADS].reshape(N_HEADS, 1), (N_HEADS, 128))
    blk = lambda shape: pl.BlockSpec(shape, lambda i: (i,) + (0,) * (len(shape) - 1))
    o, ko, vo = pl.pallas_call(
        _swa_sample_kernel,
        grid=(bsz // bt,),
        in_specs=[blk((bt, N_HEADS, 128)), blk((bt, 1, 128)), blk((bt, 1, 128)),
                  pl.BlockSpec((N_HEADS, 128), lambda i: (0, 0)),
                  blk((bt, WINDOW, 128)), blk((bt, WINDOW, 128))],
        out_specs=[blk((bt, N_HEADS, 128)), blk((bt, WINDOW, 128)), blk((bt, WINDOW, 128))],
        out_shape=[jax.ShapeDtypeStruct((bsz, N_HEADS, 128), F32), jax.ShapeDtypeStruct(cache_k.shape, F32),
                   jax.ShapeDtypeStruct(cache_v.shape, F32)],
        compiler_params=_cparams(("parallel",)),
        name="swa_sample",
    )(q, kn, vn, sink, cache_k, cache_v)
    y = jnp.concatenate([o[:, 0, 0:64], o[:, 2, 64:128], o[:, 1, 0:64], o[:, 3, 64:128]], axis=1)
    return y, ko, vo


def _ml_scalars(small, m_state):
    z4 = jnp.zeros((small.shape[0], 4), F32)
    return jnp.concatenate([small[:, ML_I_LANE:ML_I_LANE + 4], z4, small[:, ML_F_LANE:ML_F_LANE + 4], z4, m_state, z4,
                            jnp.zeros((small.shape[0], 128 - 24), F32)], axis=1)


def _take_cols(a, pieces, pad):
    parts = [a[..., s:s + n] for s, n in pieces]
    if pad:
        parts.append(jnp.zeros(a.shape[:-1] + (pad,), a.dtype))
    return jnp.concatenate(parts, axis=-1)


def _mix_pieces():
    p = [(0, 768)]
    sq = 784
    p += [(sq, 64), (sq + 128, 64), (sq + 64, 64), (sq + 192, 64)]
    p += [(1040, 256)]
    p += [(1296, 768), (2072, 256)]
    for base in (2328, 2584):
        for half in range(2):
            p += [(base + h * HD + half * 32, 32) for h in range(N_HEADS)]
    p += [(2840, 512)]
    p += [(768, 16), (2064, 8)]
    return p


def _prep_layer(li, w_in, b_in, gla_w_a2, gla_b_a2, gla_norm_g, swa_sinks, ml_norm_g, ret_norm_g, w_branch, w_out,
                ln1_g, ln1_b, w_router_grp, b_router_grp, w_router_exp, b_router_exp, w_exp_gate, w_exp_up,
                w_exp_down, ln2_g, ln2_b):
    pieces = _mix_pieces()
    pad = W_MIX - sum(n for _, n in pieces)
    row = lambda v: v.reshape(1, -1).astype(F32)
    wr = jnp.concatenate([w_router_grp[li], w_router_exp[li],
                          jnp.zeros((D_MODEL, 128 - N_GROUPS - N_EXPERTS), F32)], axis=1)
    wr_hi = wr.astype(BF16)
    wb = w_branch[li]
    wb1 = jnp.concatenate([wb[1, 0:64], wb[1, 128:192], wb[1, 64:128], wb[1, 192:256]], axis=0)
    return dict(
        wm=_take_cols(w_in[li], pieces, pad).astype(BF16),
        bm=row(_take_cols(b_in[li], pieces, pad)),
        wg=w_in[li][:, GATE_COL:].astype(BF16),
        bg=row(b_in[li][GATE_COL:]),
        wa2=jnp.concatenate([gla_w_a2[li], jnp.zeros((128 - GLA_RANK, 128), F32)], axis=0).astype(BF16),
        ba2=row(gla_b_a2[li]),
        gla_g=row(gla_norm_g[li]), ml_g=row(ml_norm_g[li]), ret_g=row(ret_norm_g[li]),
        sinks=row(jnp.concatenate([swa_sinks[li], jnp.zeros((128 - N_HEADS,), F32)])),
        wbr=jnp.stack([wb[0], wb1, wb[2], wb[3]]).astype(BF16),
        wout=w_out[li].astype(BF16),
        ln1_g=row(ln1_g[li]), ln1_b=row(ln1_b[li]), ln2_g=row(ln2_g[li]), ln2_b=row(ln2_b[li]),
        wr_hi=wr_hi, wr_lo=(wr - wr_hi.astype(F32)).astype(BF16),
        br=row(jnp.concatenate([b_router_grp[li], b_router_exp[li], jnp.zeros((128 - N_GROUPS - N_EXPERTS,), F32)])),
        we_gate=w_exp_gate[li].astype(BF16), we_up=w_exp_up[li].astype(BF16), we_down=w_exp_down[li].astype(BF16),
    )


TOK_TILE = 256
MOE_PIECES_PROMPT = 16
MOE_PIECES_SAMPLE = 1


def _ffn_half(x2, ys, lw, tm, bp):
    x1, route, cnt, xs = _merge(x2, ys, lw, tm)
    return _moe(x1, route, cnt, xs, lw, bp, tm)


def _layer_prompt(x2, lw, bsz, seq):
    gla, swa, ml, ret, small = _inproj(x2, lw["wm"], lw["bm"], TOK_TILE)
    ya, s_gla = _gla_prompt(gla, small, lw, bsz, seq)
    yb = _swa_prompt(swa, lw, bsz, seq)
    yc, c_bd, n_row, m_row = _ml_prompt(ml, small, lw, bsz, seq)
    yd, s_ret = _ret_prompt(ret, lw, bsz, seq)
    kv_tail = swa.reshape(bsz, seq, W_SWA)[:, seq - WINDOW:, 256:512]
    states = (_gla_state_from_blockdiag(s_gla),
              kv_tail[:, :, 0:128].reshape(bsz, WINDOW, SWA_KV, HD),
              kv_tail[:, :, 128:256].reshape(bsz, WINDOW, SWA_KV, HD),
              _diag_blocks(c_bd), n_row.reshape(bsz, N_HEADS, HD), m_row[:, 0, 0:N_HEADS],
              _ret_state_from_blockdiag(s_ret))
    return _ffn_half(x2, (ya, yb, yc, yd), lw, TOK_TILE, MOE_PIECES_PROMPT), states


def _layer_sample(x2, st, lw):
    s_gla, k_buf, v_buf, c_ml, n_ml, m_ml, s_ret = st
    bsz = x2.shape[0]
    gla, swa, ml, ret, small = _inproj(x2, lw["wm"], lw["bm"], bsz)
    ya, s_gla_new = _gla_sample(gla, small, s_gla.reshape(bsz, -1), lw)
    yb, k_new, v_new = _swa_sample(swa, k_buf.reshape(bsz, WINDOW, SWA_KV * HD), v_buf.reshape(bsz, WINDOW, SWA_KV * HD), lw)
    yc, c_new, n_new, m_new = _ml_sample(ml, _ml_scalars(small, m_ml), n_ml.reshape(bsz, -1), c_ml.reshape(bsz, -1), lw)
    yd, s_ret_new = _ret_sample(ret, s_ret.reshape(bsz, -1), lw)
    states = (s_gla_new.reshape(s_gla.shape), k_new.reshape(k_buf.shape), v_new.reshape(v_buf.shape),
              c_new.reshape(c_ml.shape), n_new.reshape(n_ml.shape), m_new[:, 0:N_HEADS], s_ret_new.reshape(s_ret.shape))
    return _ffn_half(x2, (ya, yb, yc, yd), lw, bsz, MOE_PIECES_SAMPLE), states


def kernel(x_prompt, x_sample, state_gla, cache_swa_k, cache_swa_v, state_mlstm_c, state_mlstm_n, state_mlstm_m,
           state_ret, w_in, b_in, gla_w_a2, gla_b_a2, gla_norm_g, swa_sinks, ml_norm_g, ret_norm_g, w_branch, w_out,
           ln1_g, ln1_b, w_router_grp, b_router_grp, w_router_exp, b_router_exp, w_exp_gate, w_exp_up, w_exp_down,
           ln2_g, ln2_b):
    bsz, seq, _ = x_prompt.shape
    dec_b, dec_seq, _ = x_sample.shape
    assert dec_seq == 1 and dec_b == LANES and seq % GLA_TILE == 0 and (bsz * seq) % TOK_TILE == 0
    weights = (w_in, b_in, gla_w_a2, gla_b_a2, gla_norm_g, swa_sinks, ml_norm_g, ret_norm_g, w_branch, w_out,
               ln1_g, ln1_b, w_router_grp, b_router_grp, w_router_exp, b_router_exp, w_exp_gate, w_exp_up,
               w_exp_down, ln2_g, ln2_b)
    y_p = x_prompt.reshape(bsz * seq, D_MODEL)
    y_s = x_sample.reshape(dec_b, D_MODEL)
    new_p, new_s = [], []
    for li in range(w_in.shape[0]):
        lw = _prep_layer(li, *weights)
        y_p, st_p = _layer_prompt(y_p, lw, bsz, seq)
        st_in = (state_gla[li], cache_swa_k[li], cache_swa_v[li], state_mlstm_c[li], state_mlstm_n[li],
                 state_mlstm_m[li], state_ret[li])
        y_s, st_s = _layer_sample(y_s, st_in, lw)
        new_p.append(st_p)
        new_s.append(st_s)
    outs = [y_p.reshape(bsz, seq, D_MODEL), y_s.reshape(dec_b, dec_seq, D_MODEL)]
    for i in range(7):
        outs.append(jnp.stack([s[i] for s in new_p], axis=0))
        outs.append(jnp.stack([s[i] for s in new_s], axis=0))
    return tuple(outs)
```

```python
import functools
import math

import jax
import jax.numpy as jnp
from jax import lax
from jax.experimental import pallas as pl
from jax.experimental.pallas import tpu as pltpu

F32 = jnp.float32
BF16 = jnp.bfloat16

D_MODEL = 1024
N_BRANCH = 4
BRANCH_W = D_MODEL // N_BRANCH
N_HEADS = 4
GLA_DK = 32
GLA_RANK = 16
GLA_TAU = 16.0
HD = 64
SWA_KV = 2
WINDOW = 128
ROPE_BASE = 10000.0
PAST_LEN = 16384
N_GROUPS = 4
EPG = 8
N_EXPERTS = N_GROUPS * EPG
D_EXPERT = D_MODEL // 4
DEPTH = 2
ALPHA = (2 * DEPTH) ** 0.25
LN_EPS = 1e-5
GATE_COL = 3352
VMEM_LIMIT = 56 * 1024 * 1024

W_GLA, W_SWA, W_ML, W_RET, W_SMALL = 768, 512, 1024, 1024, 128
W_MIX = W_GLA + W_SWA + W_ML + W_RET + W_SMALL


def _cparams(sem):
    return pltpu.CompilerParams(dimension_semantics=sem, vmem_limit_bytes=VMEM_LIMIT)


def _dot(a, b):
    return jnp.dot(a, b, preferred_element_type=F32)


def _einsum(spec, a, b):
    return jnp.einsum(spec, a, b, preferred_element_type=F32)


def _split2(x):
    hi = x.astype(BF16)
    lo = (x - hi.astype(F32)).astype(BF16)
    return hi, lo


def _split3(x):
    hi = x.astype(BF16)
    r = x - hi.astype(F32)
    mid = r.astype(BF16)
    lo = (r - mid.astype(F32)).astype(BF16)
    return hi, mid, lo


def _dot_exact_lhs(a_exact, b):
    return sum(_dot(a_exact, p) for p in _split3(b))


def _dot_exact_rhs(a, b_exact):
    return sum(_dot(p, b_exact) for p in _split3(a))


def _layer_norm(x, g, b):
    mu = jnp.mean(x, axis=-1, keepdims=True)
    xc = x - mu
    var = jnp.mean(xc * xc, axis=-1, keepdims=True)
    return xc * lax.rsqrt(var + LN_EPS) * g + b


def _inproj_kernel(x_ref, w_ref, b_ref, gla_ref, swa_ref, ml_ref, ret_ref, sm_ref):
    xb = x_ref[...].astype(BF16)
    off = 0
    for ref, width in ((gla_ref, W_GLA), (swa_ref, W_SWA), (ml_ref, W_ML), (ret_ref, W_RET), (sm_ref, W_SMALL)):
        ref[...] = _dot(xb, w_ref[:, off:off + width]) + b_ref[:, off:off + width]
        off += width


def _inproj(x2, wm, bm, tm):
    t = x2.shape[0]
    widths = (W_GLA, W_SWA, W_ML, W_RET, W_SMALL)
    return pl.pallas_call(
        _inproj_kernel,
        grid=(t // tm,),
        in_specs=[pl.BlockSpec((tm, D_MODEL), lambda i: (i, 0)),
                  pl.BlockSpec((D_MODEL, W_MIX), lambda i: (0, 0)),
                  pl.BlockSpec((1, W_MIX), lambda i: (0, 0))],
        out_specs=[pl.BlockSpec((tm, w), lambda i: (i, 0)) for w in widths],
        out_shape=[jax.ShapeDtypeStruct((t, w), F32) for w in widths],
        compiler_params=_cparams(("parallel",)),
        name="inproj",
    )(x2, wm, bm)


def _merge_kernel(x_ref, ya_ref, yb_ref, yc_ref, yd_ref, wg_ref, bg_ref, wbr_ref, wout_ref, g1_ref, b1_ref,
                  wrh_ref, wrl_ref, br_ref, x1_ref, route_ref, cnt_ref, xs_ref, *, tm):
    x = x_ref[...]
    xb = x.astype(BF16)
    merged = jnp.zeros((tm, D_MODEL), F32)
    for n, y_ref in enumerate((ya_ref, yb_ref, yc_ref, yd_ref)):
        gate = jax.nn.sigmoid(_dot(xb, wg_ref[:, n * D_MODEL:(n + 1) * D_MODEL]) + bg_ref[:, n * D_MODEL:(n + 1) * D_MODEL])
        merged = merged + gate * _dot(y_ref[...].astype(BF16), wbr_ref[n])
    mix = _dot(merged.astype(BF16), wout_ref[...])
    x1 = _layer_norm(ALPHA * x + mix, g1_ref[...], b1_ref[...])
    x1_ref[...] = x1

    xh, xl = _split2(x1)
    logits = _dot(xh, wrh_ref[...]) + _dot(xh, wrl_ref[...]) + _dot(xl, wrh_ref[...]) + br_ref[...]
    lane = lax.broadcasted_iota(jnp.int32, (tm, 128), 1)
    lanef = lane.astype(F32)
    neg = jnp.float32(-jnp.inf)
    big = jnp.float32(1e9)
    isg = lane < N_GROUPS
    gmax = jnp.max(jnp.where(isg, logits, neg), axis=-1, keepdims=True)
    gidx = jnp.min(jnp.where(isg & (logits == gmax), lanef, big), axis=-1, keepdims=True)
    p_group = 1.0 / jnp.sum(jnp.where(isg, jnp.exp(logits - gmax), 0.0), axis=-1, keepdims=True)
    lo = N_GROUPS + EPG * gidx
    em = (lanef >= lo) & (lanef < lo + EPG)
    v1 = jnp.max(jnp.where(em, logits, neg), axis=-1, keepdims=True)
    i1 = jnp.min(jnp.where(em & (logits == v1), lanef, big), axis=-1, keepdims=True)
    em2 = em & (lanef != i1)
    v2 = jnp.max(jnp.where(em2, logits, neg), axis=-1, keepdims=True)
    i2 = jnp.min(jnp.where(em2 & (logits == v2), lanef, big), axis=-1, keepdims=True)
    e21 = jnp.exp(v2 - v1)
    gt1 = 1.0 / (1.0 + e21)
    gt2 = e21 * gt1

    sel1 = lanef == i1
    sel2 = lanef == i2
    onehot = jnp.where(sel1 | sel2, 1.0, 0.0)
    cnt = jnp.sum(onehot, axis=0, keepdims=True)
    seg = jnp.floor((cnt + (PIECE - 1)) * (1.0 / PIECE)) * PIECE
    er = lax.broadcasted_iota(jnp.int32, (128, 128), 0)
    ec = lax.broadcasted_iota(jnp.int32, (128, 128), 1)
    seg_off = _dot(jnp.broadcast_to(seg, (8, 128)).astype(BF16), jnp.where(er < ec, 1.0, 0.0).astype(BF16))[0:1]
    row = lax.broadcasted_iota(jnp.int32, (tm, tm), 0)
    col = lax.broadcasted_iota(jnp.int32, (tm, tm), 1)
    earlier = _dot(jnp.where(col < row, 1.0, 0.0).astype(BF16), onehot.astype(BF16))
    where_to = seg_off + earlier
    pos1 = jnp.sum(jnp.where(sel1, where_to, 0.0), axis=-1, keepdims=True)
    pos2 = jnp.sum(jnp.where(sel2, where_to, 0.0), axis=-1, keepdims=True)
    rows_t = xs_ref.shape[0]
    r = lax.broadcasted_iota(jnp.int32, (tm, rows_t), 1).astype(F32)
    place = jnp.where((r == pos1) | (r == pos2), 1.0, 0.0).astype(BF16)
    xs_ref[...] = _dot_tn(place, x1.astype(BF16)).astype(BF16)
    cnt_ref[...] = cnt

    vals = (i1 - N_GROUPS, i2 - N_GROUPS, p_group * gt1, p_group * gt2, pos1, pos2)
    route = jnp.zeros((tm, 128), F32)
    for k, v in enumerate(vals):
        route = jnp.where(lane == k, v, route)
    route_ref[...] = route


PIECE = 16


def _tile_rows(tm):
    return -(-(2 * tm + N_EXPERTS * (PIECE - 1)) // 128) * 128


def _merge(x2, ys, lw, tm):
    t = x2.shape[0]
    nt = t // tm
    rows_t = _tile_rows(tm)
    tok = lambda w: pl.BlockSpec((tm, w), lambda i: (i, 0))
    full = lambda shape: pl.BlockSpec(shape, lambda i: (0,) * len(shape))
    return pl.pallas_call(
        functools.partial(_merge_kernel, tm=tm),
        grid=(t // tm,),
        in_specs=[tok(D_MODEL), tok(BRANCH_W), tok(BRANCH_W), tok(BRANCH_W), tok(BRANCH_W),
                  full((D_MODEL, N_BRANCH * D_MODEL)), full((1, N_BRANCH * D_MODEL)),
                  full((N_BRANCH, BRANCH_W, D_MODEL)), full((D_MODEL, D_MODEL)),
                  full((1, D_MODEL)), full((1, D_MODEL)),
                  full((D_MODEL, 128)), full((D_MODEL, 128)), full((1, 128))],
        out_specs=[tok(D_MODEL), tok(128), pl.BlockSpec((None, 1, 128), lambda i: (i, 0, 0)),
                   pl.BlockSpec((rows_t, D_MODEL), lambda i: (i, 0))],
        out_shape=[jax.ShapeDtypeStruct((t, D_MODEL), F32), jax.ShapeDtypeStruct((t, 128), F32),
                   jax.ShapeDtypeStruct((nt, 1, 128), F32), jax.ShapeDtypeStruct((nt * rows_t, D_MODEL), BF16)],
        compiler_params=_cparams(("parallel",)),
        name="merge",
    )(x2, *ys, lw["wg"], lw["bg"], lw["wbr"], lw["wout"], lw["ln1_g"], lw["ln1_b"], lw["wr_hi"], lw["wr_lo"], lw["br"])


def _piece_copy(hbm, hbm_row, vmem, piece, sem, to_vmem):
    h = hbm.at[pl.ds(pl.multiple_of(hbm_row, PIECE), PIECE), :]
    v = vmem.at[pl.ds(pl.multiple_of(piece * PIECE, PIECE), PIECE), :]
    return pltpu.make_async_copy(h, v, sem) if to_vmem else pltpu.make_async_copy(v, h, sem)


def _expert_kernel(be_ref, nu_ref, nv_ref, off_ref, xs_hbm, wg_ref, wu_ref, wd_ref, ys_hbm, xin, yout, sem_in, sem_out,
                   *, bp):
    del xs_hbm
    blk = pl.program_id(0)
    n_used = nu_ref[0]
    slot = blk % 2

    def pieces(b, s, buf, sem, to_vmem, start):
        def body(p, c):
            cp = _piece_copy(ys_hbm, off_ref[b * bp + p], buf.at[s], p, sem.at[s], to_vmem)
            cp.start() if start else cp.wait()
            return c
        lax.fori_loop(0, nv_ref[b], body, 0)

    @pl.when(blk == 0)
    def _():
        xin[...] = jnp.zeros_like(xin)

        @pl.when(n_used > 0)
        def _():
            pieces(0, 0, xin, sem_in, True, True)

    @pl.when(blk + 1 < n_used)
    def _():
        pieces(blk + 1, 1 - slot, xin, sem_in, True, True)

    @pl.when(blk < n_used)
    def _():
        pieces(blk, slot, xin, sem_in, True, False)

        @pl.when(blk >= 2)
        def _():
            pieces(blk - 2, slot, yout, sem_out, False, False)

        xb = xin[slot]
        hid = jax.nn.silu(_dot(xb, wg_ref[...].astype(BF16))) * _dot(xb, wu_ref[...].astype(BF16))
        yout[slot] = _dot(hid.astype(BF16), wd_ref[...].astype(BF16)).astype(BF16)
        pieces(blk, slot, yout, sem_out, False, True)

    @pl.when(blk == n_used - 1)
    def _():
        @pl.when(blk >= 1)
        def _():
            pieces(blk - 1, 1 - slot, yout, sem_out, False, False)
        pieces(blk, slot, yout, sem_out, False, False)


def _experts(xs, block_expert, n_used, n_valid, piece_off, lw, bp):
    n_blocks = block_expert.shape[0]
    rows = bp * PIECE
    li = lw["layer"]
    wspec = lambda shape: pl.BlockSpec((None, None) + shape, lambda b, be, nu, nv, off: (li, be[b], 0, 0))
    grid_spec = pltpu.PrefetchScalarGridSpec(
        num_scalar_prefetch=4,
        grid=(n_blocks,),
        in_specs=[pl.BlockSpec(memory_space=pl.ANY), wspec((D_MODEL, D_EXPERT)), wspec((D_MODEL, D_EXPERT)),
                  wspec((D_EXPERT, D_MODEL))],
        out_specs=pl.BlockSpec(memory_space=pl.ANY),
        scratch_shapes=[pltpu.VMEM((2, rows, D_MODEL), BF16), pltpu.VMEM((2, rows, D_MODEL), BF16),
                        pltpu.SemaphoreType.DMA((2,)), pltpu.SemaphoreType.DMA((2,))],
    )
    return pl.pallas_call(
        functools.partial(_expert_kernel, bp=bp),
        grid_spec=grid_spec,
        out_shape=jax.ShapeDtypeStruct(xs.shape, BF16),
        input_output_aliases={4: 0},
        compiler_params=_cparams(("arbitrary",)),
        name="experts",
    )(block_expert, n_used, n_valid, piece_off, xs, lw["we_gate"], lw["we_up"], lw["we_down"])


def _combine_kernel(ys_ref, x1_ref, route_ref, g2_ref, b2_ref, out_ref, *, tm):
    route = route_ref[...]
    ys = ys_ref[...]
    r = lax.broadcasted_iota(jnp.int32, (tm, ys.shape[0]), 1).astype(F32)
    ffn = jnp.zeros((tm, D_MODEL), F32)
    for j in range(2):
        pick = jnp.where(r == route[:, 4 + j:5 + j], 1.0, 0.0).astype(BF16)
        ffn = ffn + route[:, 2 + j:3 + j] * _dot(pick, ys)
    out_ref[...] = _layer_norm(ALPHA * x1_ref[...] + ffn, g2_ref[...], b2_ref[...])


def _combine(ys, x1, route, lw, tm):
    t = x1.shape[0]
    rows_t = _tile_rows(tm)
    full = lambda shape: pl.BlockSpec(shape, lambda i: (0,) * len(shape))
    return pl.pallas_call(
        functools.partial(_combine_kernel, tm=tm),
        grid=(t // tm,),
        in_specs=[pl.BlockSpec((rows_t, D_MODEL), lambda i: (i, 0)), pl.BlockSpec((tm, D_MODEL), lambda i: (i, 0)),
                  pl.BlockSpec((tm, 128), lambda i: (i, 0)), full((1, D_MODEL)), full((1, D_MODEL))],
        out_specs=pl.BlockSpec((tm, D_MODEL), lambda i: (i, 0)),
        out_shape=jax.ShapeDtypeStruct((t, D_MODEL), F32),
        compiler_params=_cparams(("parallel",)),
        name="combine",
    )(ys, x1, route, lw["ln2_g"], lw["ln2_b"])


def _piece_tables(cnt, tm, bp):
    nt = cnt.shape[0]
    rows_t = _tile_rows(tm)
    counts = cnt[:, 0, N_GROUPS:N_GROUPS + N_EXPERTS].astype(jnp.int32)
    pcs = (counts + PIECE - 1) // PIECE
    seg_row = (jnp.cumsum(pcs, axis=1) - pcs) * PIECE + jnp.arange(nt, dtype=jnp.int32)[:, None] * rows_t
    total = jnp.sum(pcs, axis=0)
    pad = (total + bp - 1) // bp * bp - total
    run_len = jnp.concatenate([pcs.T, pad[:, None]], axis=1).reshape(-1)
    run_row = jnp.concatenate([seg_row.T, jnp.zeros((N_EXPERTS, 1), jnp.int32)], axis=1).reshape(-1)
    run_real = jnp.concatenate([jnp.ones((N_EXPERTS, nt), jnp.int32), jnp.zeros((N_EXPERTS, 1), jnp.int32)],
                               axis=1).reshape(-1)
    run_expert = jnp.repeat(jnp.arange(N_EXPERTS, dtype=jnp.int32), nt + 1)
    run_end = jnp.cumsum(run_len)
    run_start = run_end - run_len
    n_blocks = -(-(nt * rows_t // PIECE + N_EXPERTS * (bp - 1)) // bp)
    g = jnp.arange(n_blocks * bp, dtype=jnp.int32)[:, None]
    inside = ((run_start[None, :] <= g) & (g < run_end[None, :])).astype(jnp.int32)
    real = inside * run_real[None, :]
    off = jnp.sum(real * (run_row[None, :] + (g - run_start[None, :]) * PIECE), axis=1).astype(jnp.int32)
    n_valid = jnp.sum(jnp.sum(real, axis=1).reshape(n_blocks, bp), axis=1).astype(jnp.int32)
    block_expert = jnp.sum(inside * run_expert[None, :], axis=1).reshape(n_blocks, bp)[:, 0].astype(jnp.int32)
    n_used = (run_end[-1] // bp).astype(jnp.int32).reshape(1)
    return block_expert, n_used, n_valid, off


def _moe(x1, route, cnt, xs, lw, bp, tm):
    ys = _experts(xs, *_piece_tables(cnt, tm, bp), lw, bp)
    return _combine(ys, x1, route, lw, tm)


CHUNK = 128
STEP_CHUNKS = 4


def _dot_nt(a, b):
    return lax.dot_general(a, b, (((1,), (1,)), ((), ())), preferred_element_type=F32)


def _dot_tn(a, b):
    return lax.dot_general(a, b, (((0,), (0,)), ((), ())), preferred_element_type=F32)


def _lane_head(width, group):
    return lax.broadcasted_iota(jnp.int32, (1, width), 1) // group


def _block_diag_mask(rows_head, cols_head):
    return rows_head == cols_head


def _head_norm(o, gain):
    r = lax.broadcasted_iota(jnp.int32, (BRANCH_W, BRANCH_W), 0) // HD
    c = lax.broadcasted_iota(jnp.int32, (BRANCH_W, BRANCH_W), 1) // HD
    avg = jnp.where(r == c, 1.0 / HD, 0.0).astype(BF16)
    mu = _dot_exact_rhs(o, avg)
    xc = o - mu
    var = _dot_exact_rhs(xc * xc, avg)
    return xc * lax.rsqrt(var + LN_EPS) * gain


def _stack_heads(x, head_of_lane):
    return jnp.concatenate([jnp.where(head_of_lane == h, x, 0.0) for h in range(N_HEADS)], axis=0)


def _pick_heads(stacked, rows, head_of_lane):
    out = jnp.where(head_of_lane == 0, stacked[0:rows], 0.0)
    for h in range(1, N_HEADS):
        out = jnp.where(head_of_lane == h, stacked[h * rows:(h + 1) * rows], out)
    return out


def _spread_heads(cols, head_of_lane):
    out = jnp.where(head_of_lane == 0, cols[0], 0.0)
    for h in range(1, N_HEADS):
        out = jnp.where(head_of_lane == h, cols[h], out)
    return out


SWA_BLOCKS = 4


def _swa_prompt_kernel(q_ref, kv_ref, kvp_ref, sink_ref, y_ref):
    n = pl.program_id(1)
    lo = lax.broadcasted_iota(jnp.int32, (1, 128), 1) < HD
    e = lax.broadcasted_iota(jnp.int32, (2 * WINDOW, N_HEADS * WINDOW), 0)
    col = lax.broadcasted_iota(jnp.int32, (2 * WINDOW, N_HEADS * WINDOW), 1)
    a = col % WINDOW
    band = (e > a) & (e <= a + WINDOW)
    head = lax.broadcasted_iota(jnp.int32, (1, N_HEADS * WINDOW), 1) // WINDOW
    sink = _spread_heads([sink_ref[:, h:h + 1] for h in range(N_HEADS)], head)
    blocks = range(SWA_BLOCKS)
    scores, vexts = [], []
    for i in blocks:
        rows = slice(i * WINDOW, (i + 1) * WINDOW)
        q = q_ref[rows, :]
        qa, qb = q[:, 0:128], q[:, 128:256]
        kv = kv_ref[rows, :]
        kvp = kvp_ref[...] if i == 0 else kv_ref[(i - 1) * WINDOW:i * WINDOW, :]
        kext = jnp.concatenate([kvp[:, 0:128], kv[:, 0:128]], axis=0).astype(BF16)
        vexts.append(jnp.concatenate([kvp[:, 128:256], kv[:, 128:256]], axis=0).astype(BF16))
        qs = jnp.concatenate([jnp.where(lo, qa, 0.0), jnp.where(lo, qb, 0.0),
                              jnp.where(lo, 0.0, qa), jnp.where(lo, 0.0, qb)], axis=0).astype(BF16)
        scores.append(_dot_nt(kext, qs))
    probs, inv_den = [], []
    for i in blocks:
        valid = band & ((n > 0) | (e >= WINDOW)) if i == 0 else band
        s = jnp.where(valid, scores[i] * (HD ** -0.5), -jnp.inf)
        mx = jnp.maximum(jnp.max(s, axis=0, keepdims=True), sink)
        p = jnp.exp(s - mx)
        inv_den.append(1.0 / (jnp.sum(p, axis=0, keepdims=True) + jnp.exp(sink - mx)))
        probs.append(p.astype(BF16))
    outs = [_dot_tn(vexts[i], probs[i]) for i in blocks]
    for i in blocks:
        o = outs[i] * inv_den[i]
        ya = jnp.concatenate([o[0:HD, 0:128], o[HD:128, 256:384]], axis=0).T
        yb = jnp.concatenate([o[0:HD, 128:256], o[HD:128, 384:512]], axis=0).T
        y_ref[i * WINDOW:(i + 1) * WINDOW, :] = jnp.concatenate([ya, yb], axis=1)


def _swa_prompt(swa, lw, bsz, seq):
    qt = SWA_BLOCKS * WINDOW
    nt = seq // qt
    return pl.pallas_call(
        _swa_prompt_kernel,
        grid=(bsz, nt),
        in_specs=[pl.BlockSpec((qt, 256), lambda b, n: (b * nt + n, 0)),
                  pl.BlockSpec((qt, 256), lambda b, n: (b * nt + n, 1)),
                  pl.BlockSpec((WINDOW, 256), lambda b, n: (jnp.maximum((b * nt + n) * SWA_BLOCKS - 1, 0), 1)),
                  pl.BlockSpec((1, 128), lambda b, n: (0, 0))],
        out_specs=pl.BlockSpec((qt, BRANCH_W), lambda b, n: (b * nt + n, 0)),
        out_shape=jax.ShapeDtypeStruct((bsz * seq, BRANCH_W), F32),
        compiler_params=_cparams(("parallel", "parallel")),
        name="swa_prompt",
    )(swa, swa, swa, lw["sinks"])


def _rotate(x, cos, sin):
    x1, x2 = x[:, 0:128], x[:, 128:256]
    return jnp.concatenate([x1 * cos - x2 * sin, x1 * sin + x2 * cos], axis=1)


def _ret_prompt_kernel(r_ref, cos_ref, sin_ref, dmat_ref, qdec_ref, kdec_ref, cdec_ref, g_ref, y_ref, s_out_ref, s_ref):
    n = pl.program_id(1)

    @pl.when(n == 0)
    def _():
        s_ref[...] = jnp.zeros_like(s_ref)

    c = CHUNK
    chunks = range(STEP_CHUNKS)
    hq = _lane_head(BRANCH_W, 32) % N_HEADS
    hv = _lane_head(BRANCH_W, HD)
    rows_head = (lax.broadcasted_iota(jnp.int32, (BRANCH_W, 1), 0) // 32) % N_HEADS
    qr, kr, v = [], [], []
    for i in chunks:
        rows = slice(i * c, (i + 1) * c)
        cos, sin = cos_ref[rows, :], sin_ref[rows, :]
        qr.append(_rotate(r_ref[rows, 0:256], cos, sin))
        kr.append(_rotate(r_ref[rows, 256:512], cos, sin) * (HD ** -0.5))
        v.append(r_ref[rows, 512:768].astype(BF16))
    att = [_dot_nt(_stack_heads(qr[i], hq).astype(BF16), kr[i].astype(BF16)) * dmat_ref[...] for i in chunks]
    upd = [_dot_tn((kr[i] * kdec_ref[...]).astype(BF16), v[i]) for i in chunks]
    states = [s_ref[...]]
    for i in chunks:
        states.append(cdec_ref[...] * states[i] + jnp.where(rows_head == hv, upd[i], 0.0))
    o = [_pick_heads(_dot(att[i].astype(BF16), v[i]), c, hv) for i in chunks]
    o = [o[i] + _dot((qr[i] * qdec_ref[...]).astype(BF16), states[i].astype(BF16)) for i in chunks]
    s_ref[...] = states[-1]
    s_out_ref[...] = states[-1]
    y_ref[...] = jax.nn.silu(r_ref[:, 768:1024]) * _head_norm(jnp.concatenate(o, axis=0), g_ref[...])


def _ret_tables(seq):
    c = CHUNK
    half = HD // 2
    freq = ROPE_BASE ** (-jnp.arange(half, dtype=F32) / half)
    ang = jnp.arange(seq).astype(F32)[:, None] * freq[None, :]
    cos = jnp.tile(jnp.cos(ang), (1, N_HEADS))
    sin = jnp.tile(jnp.sin(ang), (1, N_HEADS))
    log_g = jnp.log1p(-jnp.exp2(-5.0 - jnp.arange(N_HEADS, dtype=F32)))
    idx = jnp.arange(c, dtype=F32)
    rel = idx[:, None] - idx[None, :]
    dmat = jnp.exp(jnp.where(rel[None] >= 0, rel[None] * log_g[:, None, None], -jnp.inf)).reshape(N_HEADS * c, c)
    lane_head_q = (jnp.arange(BRANCH_W) // 32) % N_HEADS
    qdec = jnp.exp((idx + 1.0)[:, None] * log_g[lane_head_q][None, :])
    kdec = jnp.exp((c - 1.0 - idx)[:, None] * log_g[lane_head_q][None, :])
    cdec = jnp.exp(c * log_g)[jnp.arange(BRANCH_W) // HD][None, :]
    return cos, sin, dmat, qdec, kdec, cdec


def _ret_prompt(ret, lw, bsz, seq):
    c = CHUNK
    rows = STEP_CHUNKS * c
    nc = seq // rows
    cos, sin, dmat, qdec, kdec, cdec = _ret_tables(seq)
    full = lambda shape: pl.BlockSpec(shape, lambda b, n: (0,) * len(shape))
    return pl.pallas_call(
        _ret_prompt_kernel,
        grid=(bsz, nc),
        in_specs=[pl.BlockSpec((rows, W_RET), lambda b, n: (b * nc + n, 0)),
                  pl.BlockSpec((rows, 128), lambda b, n: (n, 0)), pl.BlockSpec((rows, 128), lambda b, n: (n, 0)),
                  full((N_HEADS * c, c)), full((c, BRANCH_W)), full((c, BRANCH_W)), full((1, BRANCH_W)),
                  full((1, BRANCH_W))],
        out_specs=[pl.BlockSpec((rows, BRANCH_W), lambda b, n: (b * nc + n, 0)),
                   pl.BlockSpec((None, BRANCH_W, BRANCH_W), lambda b, n: (b, 0, 0))],
        out_shape=[jax.ShapeDtypeStruct((bsz * seq, BRANCH_W), F32),
                   jax.ShapeDtypeStruct((bsz, BRANCH_W, BRANCH_W), F32)],
        scratch_shapes=[pltpu.VMEM((BRANCH_W, BRANCH_W), F32)],
        compiler_params=_cparams(("parallel", "arbitrary")),
        name="ret_prompt",
    )(ret, cos, sin, dmat, qdec, kdec, cdec, lw["ret_g"])


def _ret_state_from_blockdiag(s_bd):
    b = s_bd.shape[0]
    s = s_bd.reshape(b, 2, N_HEADS, 32, N_HEADS, HD)
    s = jnp.stack([s[:, :, h, :, h, :] for h in range(N_HEADS)], axis=1)
    return s.reshape(b, N_HEADS, HD, HD)


ML_I_LANE = GLA_RANK
ML_F_LANE = GLA_RANK + N_HEADS


def _ml_prompt_kernel(m_ref, sm_ref, g_ref, y_ref, c_out_ref, n_out_ref, m_out_ref, c_ref, n_ref, mm_ref):
    nchunk = pl.program_id(1)

    @pl.when(nchunk == 0)
    def _():
        c_ref[...] = jnp.zeros_like(c_ref)
        n_ref[...] = jnp.zeros_like(n_ref)
        mm_ref[...] = jnp.zeros_like(mm_ref)

    c = CHUNK
    chunks = range(STEP_CHUNKS)
    row = lax.broadcasted_iota(jnp.int32, (c, c), 0)
    col = lax.broadcasted_iota(jnp.int32, (c, c), 1)
    causal = col <= row
    tri_lo = jnp.where(causal, 1.0, 0.0).astype(BF16)
    tri_up = jnp.where(row <= col, 1.0, 0.0).astype(BF16)
    hv = _lane_head(BRANCH_W, HD)
    r = lax.broadcasted_iota(jnp.int32, (BRANCH_W, BRANCH_W), 0) // HD
    cc = lax.broadcasted_iota(jnp.int32, (BRANCH_W, BRANCH_W), 1) // HD
    same_head = r == cc
    lane = lax.broadcasted_iota(jnp.int32, (1, 128), 1)

    q, ks, v, sm, smt, bcols, brows, sc = [], [], [], [], [], [], [], []
    for i in chunks:
        rows = slice(i * c, (i + 1) * c)
        q.append(m_ref[rows, 0:256])
        ks.append(m_ref[rows, 256:512] * (HD ** -0.5))
        v.append(m_ref[rows, 512:768].astype(BF16))
        sm.append(sm_ref[rows, :])
        smt.append(sm[i].T)
        bcols.append(_dot_exact_lhs(tri_lo, jax.nn.log_sigmoid(sm[i])))
        brows.append(_dot_exact_rhs(jax.nn.log_sigmoid(smt[i]), tri_up))
        sc.append(_dot_nt(_stack_heads(q[i], hv).astype(BF16), ks[i].astype(BF16)))

    mm = mm_ref[...]
    qk, w_inter_b, rowsum_b, floor_b, khat, carry_b = [], [], [], [], [], []
    for i in chunks:
        qk_i, w_inter, m_tok, rowsum, w_s, carry, m_new = [], [], [], [], [], [], []
        for h in range(N_HEADS):
            bcol = bcols[i][:, ML_F_LANE + h:ML_F_LANE + h + 1]
            brow = brows[i][ML_F_LANE + h:ML_F_LANE + h + 1, :]
            icol = sm[i][:, ML_I_LANE + h:ML_I_LANE + h + 1]
            irow = smt[i][ML_I_LANE + h:ML_I_LANE + h + 1, :]
            m_prev = mm[:, h:h + 1]
            d_log = jnp.where(causal, bcol - brow + irow, -jnp.inf)
            inter_log = bcol + m_prev
            m_t = jnp.maximum(inter_log, jnp.max(d_log, axis=-1, keepdims=True))
            qk_h = sc[i][h * c:(h + 1) * c] * jnp.exp(d_log - m_t)
            qk_i.append(qk_h)
            rowsum.append(jnp.sum(qk_h, axis=-1, keepdims=True))
            w_inter.append(jnp.exp(inter_log - m_t))
            m_tok.append(m_t)
            mn = m_t[c - 1:c, :]
            b_last = bcol[c - 1:c, :]
            m_new.append(mn)
            carry.append(jnp.exp(b_last + m_prev - mn))
            w_s.append(jnp.exp(b_last - bcol + icol - mn))
        qk.append(jnp.concatenate(qk_i, axis=0).astype(BF16))
        w_inter_b.append(_spread_heads(w_inter, hv))
        rowsum_b.append(_spread_heads(rowsum, hv))
        floor_b.append(_spread_heads([jnp.exp(-m) for m in m_tok], hv))
        khat.append(ks[i] * _spread_heads(w_s, hv))
        carry_b.append(_spread_heads(carry, hv))
        mm = jnp.zeros((1, 128), F32)
        for h in range(N_HEADS):
            mm = jnp.where(lane == h, m_new[h], mm)

    intra = [_pick_heads(_dot(qk[i], v[i]), c, hv) for i in chunks]
    upd = [_dot_tn(khat[i].astype(BF16), v[i]) for i in chunks]
    cms, nrows = [c_ref[...]], [n_ref[...]]
    for i in chunks:
        cms.append(carry_b[i] * cms[i] + jnp.where(same_head, upd[i], 0.0))
        nrows.append(carry_b[i] * nrows[i] + jnp.sum(khat[i], axis=0, keepdims=True))
    ones_bd = jnp.where(same_head, 1.0, 0.0).astype(BF16)
    hout = []
    for i in chunks:
        num = w_inter_b[i] * _dot(q[i].astype(BF16), cms[i].astype(BF16)) + intra[i]
        den = w_inter_b[i] * _dot_exact_rhs(q[i] * nrows[i], ones_bd) + rowsum_b[i]
        hout.append(num / jnp.maximum(jnp.abs(den), floor_b[i]))
    y_ref[...] = _head_norm(jax.nn.sigmoid(m_ref[:, 768:1024]) * jnp.concatenate(hout, axis=0), g_ref[...])

    c_ref[...] = cms[-1]
    n_ref[...] = nrows[-1]
    mm_ref[...] = mm
    c_out_ref[...] = cms[-1]
    n_out_ref[...] = nrows[-1]
    m_out_ref[...] = mm


def _ml_prompt(ml, small, lw, bsz, seq):
    rows = STEP_CHUNKS * CHUNK
    nc = seq // rows
    return pl.pallas_call(
        _ml_prompt_kernel,
        grid=(bsz, nc),
        in_specs=[pl.BlockSpec((rows, W_ML), lambda b, n: (b * nc + n, 0)),
                  pl.BlockSpec((rows, W_SMALL), lambda b, n: (b * nc + n, 0)),
                  pl.BlockSpec((1, BRANCH_W), lambda b, n: (0, 0))],
        out_specs=[pl.BlockSpec((rows, BRANCH_W), lambda b, n: (b * nc + n, 0)),
                   pl.BlockSpec((None, BRANCH_W, BRANCH_W), lambda b, n: (b, 0, 0)),
                   pl.BlockSpec((None, 1, BRANCH_W), lambda b, n: (b, 0, 0)),
                   pl.BlockSpec((None, 1, 128), lambda b, n: (b, 0, 0))],
        out_shape=[jax.ShapeDtypeStruct((bsz * seq, BRANCH_W), F32),
                   jax.ShapeDtypeStruct((bsz, BRANCH_W, BRANCH_W), F32),
                   jax.ShapeDtypeStruct((bsz, 1, BRANCH_W), F32),
                   jax.ShapeDtypeStruct((bsz, 1, 128), F32)],
        scratch_shapes=[pltpu.VMEM((BRANCH_W, BRANCH_W), F32), pltpu.VMEM((1, BRANCH_W), F32),
                        pltpu.VMEM((1, 128), F32)],
        compiler_params=_cparams(("parallel", "arbitrary")),
        name="ml_prompt",
    )(ml, small, lw["ml_g"])


def _diag_blocks(s_bd):
    b = s_bd.shape[0]
    s = s_bd.reshape(b, N_HEADS, HD, N_HEADS, HD)
    return jnp.stack([s[:, h, :, h, :] for h in range(N_HEADS)], axis=1)


GLA_TILE = 256
GLA_SUB = 16


def _gla_prompt_kernel(x_ref, sm_ref, wa2_ref, ba2_ref, g_ref, y_ref, s_out_ref, s_ref):
    n = pl.program_id(1)

    @pl.when(n == 0)
    def _():
        s_ref[...] = jnp.zeros_like(s_ref)

    tq, sc = GLA_TILE, GLA_SUB
    x = x_ref[...]
    q = x[:, 0:128] * (GLA_DK ** -0.5)
    k = x[:, 128:256]
    v = x[:, 256:512]
    log_a = jax.nn.log_sigmoid(_dot(sm_ref[...].astype(BF16), wa2_ref[...]) + ba2_ref[...]) / GLA_TAU
    r = lax.broadcasted_iota(jnp.int32, (tq, tq), 0)
    c = lax.broadcasted_iota(jnp.int32, (tq, tq), 1)
    same = (r // sc) == (c // sc)
    b = _dot_exact_lhs(jnp.where(same & (c <= r), 1.0, 0.0).astype(BF16), log_a)
    btot = _dot_exact_lhs(jnp.where(same, 1.0, 0.0).astype(BF16), log_a)
    qd = (q * jnp.exp(b)).astype(BF16)
    khat_t = (k * jnp.exp(btot - b)).T
    dec_t = jnp.exp(btot).T
    vb = v.astype(BF16)
    kh = lax.broadcasted_iota(jnp.int32, (128, 1), 0) // GLA_DK
    hv = _lane_head(BRANCH_W, HD)
    expand = jnp.where(kh == hv, 1.0, 0.0).astype(BF16)
    lane_chunk = lax.broadcasted_iota(jnp.int32, (1, tq), 1) // sc
    trow = lax.broadcasted_iota(jnp.int32, (sc, 1), 0)
    chunks = range(tq // sc)
    upds = [_dot(jnp.where(lane_chunk == ci, khat_t, 0.0).astype(BF16), vb) for ci in chunks]
    states = [s_ref[...]]
    for ci in chunks:
        states.append(dec_t[:, ci * sc:ci * sc + 1] * states[ci] + jnp.where(kh == hv, upds[ci], 0.0))
    outs = []
    for ci in chunks:
        lo = ci * sc
        qc, kc, bc, vc = q[lo:lo + sc], k[lo:lo + sc], b[lo:lo + sc], v[lo:lo + sc]
        pair = [qc * kc[j:j + 1] * jnp.exp(jnp.where(trow >= j, bc - bc[j:j + 1], -jnp.inf)) for j in range(sc)]
        att = _dot(jnp.concatenate(pair, axis=0).astype(BF16), expand)
        o = att[0:sc] * vc[0:1]
        for j in range(1, sc):
            o = o + att[j * sc:(j + 1) * sc] * vc[j:j + 1]
        outs.append(o)
    outs = [outs[ci] + _dot(qd[ci * sc:(ci + 1) * sc], states[ci].astype(BF16)) for ci in chunks]
    s_ref[...] = states[-1]
    s_out_ref[...] = states[-1]
    y_ref[...] = jax.nn.silu(x[:, 512:768]) * _head_norm(jnp.concatenate(outs, axis=0), g_ref[...])


def _gla_prompt(gla, small, lw, bsz, seq):
    tq = GLA_TILE
    nt = seq // tq
    full = lambda shape: pl.BlockSpec(shape, lambda b, n: (0,) * len(shape))
    return pl.pallas_call(
        _gla_prompt_kernel,
        grid=(bsz, nt),
        in_specs=[pl.BlockSpec((tq, W_GLA), lambda b, n: (b * nt + n, 0)),
                  pl.BlockSpec((tq, W_SMALL), lambda b, n: (b * nt + n, 0)),
                  full((128, 128)), full((1, 128)), full((1, BRANCH_W))],
        out_specs=[pl.BlockSpec((tq, BRANCH_W), lambda b, n: (b * nt + n, 0)),
                   pl.BlockSpec((None, 128, BRANCH_W), lambda b, n: (b, 0, 0))],
        out_shape=[jax.ShapeDtypeStruct((bsz * seq, BRANCH_W), F32),
                   jax.ShapeDtypeStruct((bsz, 128, BRANCH_W), F32)],
        scratch_shapes=[pltpu.VMEM((128, BRANCH_W), F32)],
        compiler_params=_cparams(("parallel", "arbitrary")),
        name="gla_prompt",
    )(gla, small, lw["wa2"], lw["ba2"], lw["gla_g"])


def _gla_state_from_blockdiag(s_bd):
    b = s_bd.shape[0]
    s = s_bd.reshape(b, N_HEADS, GLA_DK, N_HEADS, HD)
    return jnp.stack([s[:, h, :, h, :] for h in range(N_HEADS)], axis=1)


LANES = 128


def _rank1_update(st_ref, snt_ref, q_ref, k_ref, decay_row, v, dk):
    def body(i, o):
        rows = pl.ds(pl.multiple_of(i * HD, HD), HD)
        new = decay_row(i) * st_ref[rows, :] + k_ref[pl.ds(i, 1), :] * v
        snt_ref[rows, :] = new
        return o + q_ref[pl.ds(i, 1), :] * new
    return lax.fori_loop(0, dk, body, jnp.zeros((HD, LANES), F32), unroll=4)


def _head_rows(h, n):
    return pl.ds(pl.multiple_of(h * n, n), n)


def _gla_sample_kernel(x_ref, sm_ref, wa2_ref, ba2_ref, g_ref, s_ref, y_ref, s_out_ref,
                       qt_ref, kt_ref, at_ref, vt_ref, ot_ref, st_ref, snt_ref):
    h = pl.program_id(0)

    @pl.when(h == 0)
    def _():
        x = x_ref[...]
        qt_ref[...] = (x[:, 0:128] * (GLA_DK ** -0.5)).T
        kt_ref[...] = x[:, 128:256].T
        vt_ref[...] = x[:, 256:512].T
        log_a = jax.nn.log_sigmoid(_dot(sm_ref[...].astype(BF16), wa2_ref[...]) + ba2_ref[...]) / GLA_TAU
        at_ref[...] = jnp.exp(log_a).T

    st_ref[...] = s_ref[...].T
    base = h * GLA_DK
    o = _rank1_update(st_ref, snt_ref, qt_ref.at[_head_rows(h, GLA_DK)], kt_ref.at[_head_rows(h, GLA_DK)],
                      lambda i: at_ref[pl.ds(base + i, 1), :], vt_ref[_head_rows(h, HD), :], GLA_DK)
    ot_ref[_head_rows(h, HD), :] = o
    s_out_ref[...] = snt_ref[...].T

    @pl.when(h == N_HEADS - 1)
    def _():
        y_ref[...] = jax.nn.silu(x_ref[:, 512:768]) * _head_norm(ot_ref[...].T, g_ref[...])


def _gla_sample(gla, small, state, lw):
    per_head = GLA_DK * HD
    full = lambda shape: pl.BlockSpec(shape, lambda h: (0,) * len(shape))
    vm = lambda r: pltpu.VMEM((r, LANES), F32)
    return pl.pallas_call(
        _gla_sample_kernel,
        grid=(N_HEADS,),
        in_specs=[full((LANES, W_GLA)), full((LANES, W_SMALL)), full((128, 128)), full((1, 128)), full((1, BRANCH_W)),
                  pl.BlockSpec((LANES, per_head), lambda h: (0, h))],
        out_specs=[full((LANES, BRANCH_W)), pl.BlockSpec((LANES, per_head), lambda h: (0, h))],
        out_shape=[jax.ShapeDtypeStruct((LANES, BRANCH_W), F32), jax.ShapeDtypeStruct(state.shape, F32)],
        scratch_shapes=[vm(128), vm(128), vm(128), vm(256), vm(256), vm(per_head), vm(per_head)],
        compiler_params=_cparams(("arbitrary",)),
        name="gla_sample",
    )(gla, small, lw["wa2"], lw["ba2"], lw["gla_g"], state)


def _ml_sample_kernel(x_ref, sc_ref, n_ref, g_ref, c_ref, y_ref, c_out_ref, n_out_ref, m_out_ref,
                      qt_ref, kt_ref, vt_ref, nt_ref, nnt_ref, ot_ref, w_ref, kp_ref, st_ref, snt_ref):
    h = pl.program_id(0)

    @pl.when(h == 0)
    def _():
        x = x_ref[...]
        qt_ref[...] = x[:, 0:256].T
        kt_ref[...] = (x[:, 256:512] * (HD ** -0.5)).T
        vt_ref[...] = x[:, 512:768].T
        nt_ref[...] = n_ref[...].T
        sct = sc_ref[...].T
        log_f = jax.nn.log_sigmoid(sct[8:16])
        m_old = sct[16:24]
        m_new = jnp.maximum(log_f + m_old, sct[0:8])
        w_ref[0:8, :] = jnp.exp(log_f + m_old - m_new)
        w_ref[8:16, :] = jnp.exp(sct[0:8] - m_new)
        w_ref[16:24, :] = m_new

    st_ref[...] = c_ref[...].T
    w_carry = w_ref[pl.ds(h, 1), :]
    w_tok = w_ref[pl.ds(8 + h, 1), :]
    m_new = w_ref[pl.ds(16 + h, 1), :]
    kp_ref[...] = kt_ref[_head_rows(h, HD), :] * w_tok
    q_h = qt_ref.at[_head_rows(h, HD)]
    n_new = w_carry * nt_ref[_head_rows(h, HD), :] + kp_ref[...]
    num = _rank1_update(st_ref, snt_ref, q_h, kp_ref, lambda i: w_carry, vt_ref[_head_rows(h, HD), :], HD)
    den = jnp.sum(q_h[...] * n_new, axis=0, keepdims=True)
    ot_ref[_head_rows(h, HD), :] = num / jnp.maximum(jnp.abs(den), jnp.exp(-m_new))
    nnt_ref[_head_rows(h, HD), :] = n_new
    c_out_ref[...] = snt_ref[...].T

    @pl.when(h == N_HEADS - 1)
    def _():
        y_ref[...] = _head_norm(jax.nn.sigmoid(x_ref[:, 768:1024]) * ot_ref[...].T, g_ref[...])
        n_out_ref[...] = nnt_ref[...].T
        m_out_ref[...] = jnp.concatenate([w_ref[16:24, :], jnp.zeros((LANES - 8, LANES), F32)], axis=0).T


def _ml_sample(ml, scal, n_state, c_state, lw):
    per_head = HD * HD
    full = lambda shape: pl.BlockSpec(shape, lambda h: (0,) * len(shape))
    vm = lambda r: pltpu.VMEM((r, LANES), F32)
    return pl.pallas_call(
        _ml_sample_kernel,
        grid=(N_HEADS,),
        in_specs=[full((LANES, W_ML)), full((LANES, 128)), full((LANES, BRANCH_W)), full((1, BRANCH_W)),
                  pl.BlockSpec((LANES, per_head), lambda h: (0, h))],
        out_specs=[full((LANES, BRANCH_W)), pl.BlockSpec((LANES, per_head), lambda h: (0, h)),
                   full((LANES, BRANCH_W)), full((LANES, 128))],
        out_shape=[jax.ShapeDtypeStruct((LANES, BRANCH_W), F32), jax.ShapeDtypeStruct(c_state.shape, F32),
                   jax.ShapeDtypeStruct((LANES, BRANCH_W), F32), jax.ShapeDtypeStruct((LANES, 128), F32)],
        scratch_shapes=[vm(256), vm(256), vm(256), vm(256), vm(256), vm(256), vm(24), vm(HD), vm(per_head), vm(per_head)],
        compiler_params=_cparams(("arbitrary",)),
        name="ml_sample",
    )(ml, scal, n_state, lw["ml_g"], c_state)


def _ret_sample_kernel(x_ref, cos_ref, sin_ref, gam_ref, g_ref, s_ref, y_ref, s_out_ref,
                       qt_ref, kt_ref, vt_ref, ot_ref, qr_ref, kr_ref, st_ref, snt_ref):
    h = pl.program_id(0)

    @pl.when(h == 0)
    def _():
        x = x_ref[...]
        qt_ref[...] = x[:, 0:256].T
        kt_ref[...] = (x[:, 256:512] * (HD ** -0.5)).T
        vt_ref[...] = x[:, 512:768].T

    cos, sin = cos_ref[...], sin_ref[...]
    for src, dst in ((qt_ref, qr_ref), (kt_ref, kr_ref)):
        x1 = src[_head_rows(h, 32), :]
        x2 = src[pl.ds(pl.multiple_of(128 + h * 32, 32), 32), :]
        dst[0:32, :] = x1 * cos - x2 * sin
        dst[32:64, :] = x1 * sin + x2 * cos
    st_ref[...] = s_ref[...].T
    gamma = gam_ref[pl.ds(h, 1), :]
    ot_ref[_head_rows(h, HD), :] = _rank1_update(st_ref, snt_ref, qr_ref, kr_ref, lambda i: gamma,
                                                 vt_ref[_head_rows(h, HD), :], HD)
    s_out_ref[...] = snt_ref[...].T

    @pl.when(h == N_HEADS - 1)
    def _():
        y_ref[...] = jax.nn.silu(x_ref[:, 768:1024]) * _head_norm(ot_ref[...].T, g_ref[...])


def _ret_sample(ret, state, lw):
    per_head = HD * HD
    half = HD // 2
    freq = ROPE_BASE ** (-jnp.arange(half, dtype=F32) / half)
    ang = jnp.full((1,), PAST_LEN, F32)[:, None] * freq[None, :]
    cos = jnp.broadcast_to(jnp.cos(ang).reshape(half, 1), (half, LANES))
    sin = jnp.broadcast_to(jnp.sin(ang).reshape(half, 1), (half, LANES))
    log_g = jnp.log1p(-jnp.exp2(-5.0 - jnp.arange(N_HEADS, dtype=F32)))
    gam = jnp.broadcast_to(jnp.concatenate([jnp.exp(log_g), jnp.ones((4,), F32)]).reshape(8, 1), (8, LANES))
    full = lambda shape: pl.BlockSpec(shape, lambda h: (0,) * len(shape))
    vm = lambda r: pltpu.VMEM((r, LANES), F32)
    return pl.pallas_call(
        _ret_sample_kernel,
        grid=(N_HEADS,),
        in_specs=[full((LANES, W_RET)), full((half, LANES)), full((half, LANES)), full((8, LANES)), full((1, BRANCH_W)),
                  pl.BlockSpec((LANES, per_head), lambda h: (0, h))],
        out_specs=[full((LANES, BRANCH_W)), pl.BlockSpec((LANES, per_head), lambda h: (0, h))],
        out_shape=[jax.ShapeDtypeStruct((LANES, BRANCH_W), F32), jax.ShapeDtypeStruct(state.shape, F32)],
        scratch_shapes=[vm(256), vm(256), vm(256), vm(256), vm(HD), vm(HD), vm(per_head), vm(per_head)],
        compiler_params=_cparams(("arbitrary",)),
        name="ret_sample",
    )(ret, cos, sin, gam, lw["ret_g"], state)


def _swa_sample_kernel(q_ref, kn_ref, vn_ref, sink_ref, kc_ref, vc_ref, o_ref, ko_ref, vo_ref):
    q = q_ref[...]
    kc, vc = kc_ref[...], vc_ref[...]
    kn, vn = kn_ref[...], vn_ref[...]
    scale = HD ** -0.5
    s = _einsum("bhd,bwd->bhw", q.astype(BF16), kc.astype(BF16)) * scale
    s_new = jnp.sum(q * kn, axis=-1, keepdims=True) * scale
    w = lax.broadcasted_iota(jnp.int32, (1, 1, WINDOW), 2)
    s = jnp.where(w > 0, s, -jnp.inf)
    sink = sink_ref[...][:, 0:1]
    mx = jnp.maximum(jnp.maximum(jnp.max(s, axis=-1, keepdims=True), s_new), sink)
    p = jnp.exp(s - mx)
    p_new = jnp.exp(s_new - mx)
    den = jnp.sum(p, axis=-1, keepdims=True) + p_new + jnp.exp(sink - mx)
    o = _einsum("bhw,bwd->bhd", p.astype(BF16), vc.astype(BF16)) + p_new * vn
    o_ref[...] = o / den
    ko_ref[:, 0:WINDOW - 1, :] = kc[:, 1:WINDOW, :]
    ko_ref[:, WINDOW - 1:WINDOW, :] = kn
    vo_ref[:, 0:WINDOW - 1, :] = vc[:, 1:WINDOW, :]
    vo_ref[:, WINDOW - 1:WINDOW, :] = vn


def _swa_sample(swa, cache_k, cache_v, lw, bt=16):
    bsz = swa.shape[0]
    lo = (jnp.arange(128) < HD)[None, :]
    qa, qb = swa[:, 0:128], swa[:, 128:256]
    q = jnp.stack([jnp.where(lo, qa, 0.0), jnp.where(lo, qb, 0.0), jnp.where(lo, 0.0, qa), jnp.where(lo, 0.0, qb)], axis=1)
    kn = swa[:, None, 256:384]
    vn = swa[:, None, 384:512]
    sink = jnp.broadcast_to(lw["sinks"][0, 0:N_HEADS].reshape(N_HEADS, 1), (N_HEADS, 128))
    blk = lambda shape: pl.BlockSpec(shape, lambda i: (i,) + (0,) * (len(shape) - 1))
    o, ko, vo = pl.pallas_call(
        _swa_sample_kernel,
        grid=(bsz // bt,),
        in_specs=[blk((bt, N_HEADS, 128)), blk((bt, 1, 128)), blk((bt, 1, 128)),
                  pl.BlockSpec((N_HEADS, 128), lambda i: (0, 0)),
                  blk((bt, WINDOW, 128)), blk((bt, WINDOW, 128))],
        out_specs=[blk((bt, N_HEADS, 128)), blk((bt, WINDOW, 128)), blk((bt, WINDOW, 128))],
        out_shape=[jax.ShapeDtypeStruct((bsz, N_HEADS, 128), F32), jax.ShapeDtypeStruct(cache_k.shape, F32),
                   jax.ShapeDtypeStruct(cache_v.shape, F32)],
        compiler_params=_cparams(("parallel",)),
        name="swa_sample",
    )(q, kn, vn, sink, cache_k, cache_v)
    y = jnp.concatenate([o[:, 0, 0:64], o[:, 2, 64:128], o[:, 1, 0:64], o[:, 3, 64:128]], axis=1)
    return y, ko, vo


def _ml_scalars(small, m_state):
    z4 = jnp.zeros((small.shape[0], 4), F32)
    return jnp.concatenate([small[:, ML_I_LANE:ML_I_LANE + 4], z4, small[:, ML_F_LANE:ML_F_LANE + 4], z4, m_state, z4,
                            jnp.zeros((small.shape[0], 128 - 24), F32)], axis=1)


def _take_cols(a, pieces, pad):
    parts = [a[..., s:s + n] for s, n in pieces]
    if pad:
        parts.append(jnp.zeros(a.shape[:-1] + (pad,), a.dtype))
    return jnp.concatenate(parts, axis=-1)


def _mix_pieces():
    p = [(0, 768)]
    sq = 784
    p += [(sq, 64), (sq + 128, 64), (sq + 64, 64), (sq + 192, 64)]
    p += [(1040, 256)]
    p += [(1296, 768), (2072, 256)]
    for base in (2328, 2584):
        for half in range(2):
            p += [(base + h * HD + half * 32, 32) for h in range(N_HEADS)]
    p += [(2840, 512)]
    p += [(768, 16), (2064, 8)]
    return p


def _prep_layer(li, w_in, b_in, gla_w_a2, gla_b_a2, gla_norm_g, swa_sinks, ml_norm_g, ret_norm_g, w_branch, w_out,
                ln1_g, ln1_b, w_router_grp, b_router_grp, w_router_exp, b_router_exp, w_exp_gate, w_exp_up,
                w_exp_down, ln2_g, ln2_b):
    pieces = _mix_pieces()
    pad = W_MIX - sum(n for _, n in pieces)
    row = lambda v: v.reshape(1, -1).astype(F32)
    wr = jnp.concatenate([w_router_grp[li], w_router_exp[li],
                          jnp.zeros((D_MODEL, 128 - N_GROUPS - N_EXPERTS), F32)], axis=1)
    wr_hi = wr.astype(BF16)
    wb = w_branch[li]
    wb1 = jnp.concatenate([wb[1, 0:64], wb[1, 128:192], wb[1, 64:128], wb[1, 192:256]], axis=0)
    return dict(
        wm=_take_cols(w_in[li], pieces, pad).astype(BF16),
        bm=row(_take_cols(b_in[li], pieces, pad)),
        wg=w_in[li][:, GATE_COL:].astype(BF16),
        bg=row(b_in[li][GATE_COL:]),
        wa2=jnp.concatenate([gla_w_a2[li], jnp.zeros((128 - GLA_RANK, 128), F32)], axis=0).astype(BF16),
        ba2=row(gla_b_a2[li]),
        gla_g=row(gla_norm_g[li]), ml_g=row(ml_norm_g[li]), ret_g=row(ret_norm_g[li]),
        sinks=row(jnp.concatenate([swa_sinks[li], jnp.zeros((128 - N_HEADS,), F32)])),
        wbr=jnp.stack([wb[0], wb1, wb[2], wb[3]]).astype(BF16),
        wout=w_out[li].astype(BF16),
        ln1_g=row(ln1_g[li]), ln1_b=row(ln1_b[li]), ln2_g=row(ln2_g[li]), ln2_b=row(ln2_b[li]),
        wr_hi=wr_hi, wr_lo=(wr - wr_hi.astype(F32)).astype(BF16),
        br=row(jnp.concatenate([b_router_grp[li], b_router_exp[li], jnp.zeros((128 - N_GROUPS - N_EXPERTS,), F32)])),
        layer=li, we_gate=w_exp_gate, we_up=w_exp_up, we_down=w_exp_down,
    )


TOK_TILE = 256
MOE_PIECES_PROMPT = 16
MOE_PIECES_SAMPLE = 1


def _ffn_half(x2, ys, lw, tm, bp):
    x1, route, cnt, xs = _merge(x2, ys, lw, tm)
    return _moe(x1, route, cnt, xs, lw, bp, tm)


def _layer_prompt(x2, lw, bsz, seq):
    gla, swa, ml, ret, small = _inproj(x2, lw["wm"], lw["bm"], TOK_TILE)
    ya, s_gla = _gla_prompt(gla, small, lw, bsz, seq)
    yb = _swa_prompt(swa, lw, bsz, seq)
    yc, c_bd, n_row, m_row = _ml_prompt(ml, small, lw, bsz, seq)
    yd, s_ret = _ret_prompt(ret, lw, bsz, seq)
    kv_tail = swa.reshape(bsz, seq, W_SWA)[:, seq - WINDOW:, 256:512]
    states = (_gla_state_from_blockdiag(s_gla),
              kv_tail[:, :, 0:128].reshape(bsz, WINDOW, SWA_KV, HD),
              kv_tail[:, :, 128:256].reshape(bsz, WINDOW, SWA_KV, HD),
              _diag_blocks(c_bd), n_row.reshape(bsz, N_HEADS, HD), m_row[:, 0, 0:N_HEADS],
              _ret_state_from_blockdiag(s_ret))
    return _ffn_half(x2, (ya, yb, yc, yd), lw, TOK_TILE, MOE_PIECES_PROMPT), states


def _layer_sample(x2, st, lw):
    s_gla, k_buf, v_buf, c_ml, n_ml, m_ml, s_ret = st
    bsz = x2.shape[0]
    gla, swa, ml, ret, small = _inproj(x2, lw["wm"], lw["bm"], bsz)
    ya, s_gla_new = _gla_sample(gla, small, s_gla.reshape(bsz, -1), lw)
    yb, k_new, v_new = _swa_sample(swa, k_buf.reshape(bsz, WINDOW, SWA_KV * HD), v_buf.reshape(bsz, WINDOW, SWA_KV * HD), lw)
    yc, c_new, n_new, m_new = _ml_sample(ml, _ml_scalars(small, m_ml), n_ml.reshape(bsz, -1), c_ml.reshape(bsz, -1), lw)
    yd, s_ret_new = _ret_sample(ret, s_ret.reshape(bsz, -1), lw)
    states = (s_gla_new.reshape(s_gla.shape), k_new.reshape(k_buf.shape), v_new.reshape(v_buf.shape),
              c_new.reshape(c_ml.shape), n_new.reshape(n_ml.shape), m_new[:, 0:N_HEADS], s_ret_new.reshape(s_ret.shape))
    return _ffn_half(x2, (ya, yb, yc, yd), lw, bsz, MOE_PIECES_SAMPLE), states


def kernel(x_prompt, x_sample, state_gla, cache_swa_k, cache_swa_v, state_mlstm_c, state_mlstm_n, state_mlstm_m,
           state_ret, w_in, b_in, gla_w_a2, gla_b_a2, gla_norm_g, swa_sinks, ml_norm_g, ret_norm_g, w_branch, w_out,
           ln1_g, ln1_b, w_router_grp, b_router_grp, w_router_exp, b_router_exp, w_exp_gate, w_exp_up, w_exp_down,
           ln2_g, ln2_b):
    bsz, seq, _ = x_prompt.shape
    dec_b, dec_seq, _ = x_sample.shape
    assert dec_seq == 1 and dec_b == LANES and seq % GLA_TILE == 0 and (bsz * seq) % TOK_TILE == 0
    weights = (w_in, b_in, gla_w_a2, gla_b_a2, gla_norm_g, swa_sinks, ml_norm_g, ret_norm_g, w_branch, w_out,
               ln1_g, ln1_b, w_router_grp, b_router_grp, w_router_exp, b_router_exp, w_exp_gate, w_exp_up,
               w_exp_down, ln2_g, ln2_b)
    y_p = x_prompt.reshape(bsz * seq, D_MODEL)
    y_s = x_sample.reshape(dec_b, D_MODEL)
    new_p, new_s = [], []
    for li in range(w_in.shape[0]):
        lw = _prep_layer(li, *weights)
        y_p, st_p = _layer_prompt(y_p, lw, bsz, seq)
        st_in = (state_gla[li], cache_swa_k[li], cache_swa_v[li], state_mlstm_c[li], state_mlstm_n[li],
                 state_mlstm_m[li], state_ret[li])
        y_s, st_s = _layer_sample(y_s, st_in, lw)
        new_p.append(st_p)
        new_s.append(st_s)
    outs = [y_p.reshape(bsz, seq, D_MODEL), y_s.reshape(dec_b, dec_seq, D_MODEL)]
    for i in range(7):
        outs.append(jnp.stack([s[i] for s in new_p], axis=0))
        outs.append(jnp.stack([s[i] for s in new_s], axis=0))
    return tuple(outs)
```

```python
import functools
import math

import jax
import jax.numpy as jnp
from jax import lax
from jax.experimental import pallas as pl
from jax.experimental.pallas import tpu as pltpu

F32 = jnp.float32
BF16 = jnp.bfloat16

D_MODEL = 1024
N_BRANCH = 4
BRANCH_W = D_MODEL // N_BRANCH
N_HEADS = 4
GLA_DK = 32
GLA_RANK = 16
GLA_TAU = 16.0
HD = 64
SWA_KV = 2
WINDOW = 128
ROPE_BASE = 10000.0
PAST_LEN = 16384
N_GROUPS = 4
EPG = 8
N_EXPERTS = N_GROUPS * EPG
D_EXPERT = D_MODEL // 4
DEPTH = 2
ALPHA = (2 * DEPTH) ** 0.25
LN_EPS = 1e-5
GATE_COL = 3352
VMEM_LIMIT = 56 * 1024 * 1024

W_GLA, W_SWA, W_ML, W_RET, W_SMALL = 768, 512, 1024, 1024, 128
W_MIX = W_GLA + W_SWA + W_ML + W_RET + W_SMALL


def _cparams(sem):
    return pltpu.CompilerParams(dimension_semantics=sem, vmem_limit_bytes=VMEM_LIMIT)


def _dot(a, b):
    return jnp.dot(a, b, preferred_element_type=F32)


def _einsum(spec, a, b):
    return jnp.einsum(spec, a, b, preferred_element_type=F32)


def _split2(x):
    hi = x.astype(BF16)
    lo = (x - hi.astype(F32)).astype(BF16)
    return hi, lo


def _split3(x):
    hi = x.astype(BF16)
    r = x - hi.astype(F32)
    mid = r.astype(BF16)
    lo = (r - mid.astype(F32)).astype(BF16)
    return hi, mid, lo


def _dot_exact_lhs(a_exact, b):
    return sum(_dot(a_exact, p) for p in _split3(b))


def _dot_exact_rhs(a, b_exact):
    return sum(_dot(p, b_exact) for p in _split3(a))


def _layer_norm(x, g, b):
    mu = jnp.mean(x, axis=-1, keepdims=True)
    xc = x - mu
    var = jnp.mean(xc * xc, axis=-1, keepdims=True)
    return xc * lax.rsqrt(var + LN_EPS) * g + b


def _inproj_kernel(x_ref, w_ref, b_ref, gla_ref, swa_ref, ml_ref, ret_ref, sm_ref):
    xb = x_ref[...].astype(BF16)
    off = 0
    for ref, width in ((gla_ref, W_GLA), (swa_ref, W_SWA), (ml_ref, W_ML), (ret_ref, W_RET), (sm_ref, W_SMALL)):
        ref[...] = _dot(xb, w_ref[:, off:off + width]) + b_ref[:, off:off + width]
        off += width


def _inproj(x2, wm, bm, tm):
    t = x2.shape[0]
    widths = (W_GLA, W_SWA, W_ML, W_RET, W_SMALL)
    return pl.pallas_call(
        _inproj_kernel,
        grid=(t // tm,),
        in_specs=[pl.BlockSpec((tm, D_MODEL), lambda i: (i, 0)),
                  pl.BlockSpec((D_MODEL, W_MIX), lambda i: (0, 0)),
                  pl.BlockSpec((1, W_MIX), lambda i: (0, 0))],
        out_specs=[pl.BlockSpec((tm, w), lambda i: (i, 0)) for w in widths],
        out_shape=[jax.ShapeDtypeStruct((t, w), F32) for w in widths],
        compiler_params=_cparams(("parallel",)),
        name="inproj",
    )(x2, wm, bm)


def _merge_kernel(x_ref, ya_ref, yb_ref, yc_ref, yd_ref, wg_ref, bg_ref, wbr_ref, wout_ref, g1_ref, b1_ref,
                  wrh_ref, wrl_ref, br_ref, x1_ref, route_ref, cnt_ref, xs_ref, *, tm, sub):
    rows_t = xs_ref.shape[0] // sub
    x1 = []
    for t in range(sub):
        rows = slice(t * tm, (t + 1) * tm)
        x = x_ref[rows, :]
        xb = x.astype(BF16)
        merged = jnp.zeros((tm, D_MODEL), F32)
        for n, y_ref in enumerate((ya_ref, yb_ref, yc_ref, yd_ref)):
            cols = slice(n * D_MODEL, (n + 1) * D_MODEL)
            gate = jax.nn.sigmoid(_dot(xb, wg_ref[:, cols]) + bg_ref[:, cols])
            merged = merged + gate * _dot(y_ref[rows, :].astype(BF16), wbr_ref[n])
        mix = _dot(merged.astype(BF16), wout_ref[...])
        x1.append(_layer_norm(ALPHA * x + mix, g1_ref[...], b1_ref[...]))
        x1_ref[rows, :] = x1[t]
    for t in range(sub):
        route, cnt, xs = _route_and_sort(x1[t], wrh_ref[...], wrl_ref[...], br_ref[...], tm, rows_t)
        route_ref[t * tm:(t + 1) * tm, :] = route
        cnt_ref[t] = cnt
        xs_ref[t * rows_t:(t + 1) * rows_t, :] = xs


def _route_and_sort(x1, wr_hi, wr_lo, br, tm, rows_t):
    xh, xl = _split2(x1)
    logits = _dot(xh, wr_hi) + _dot(xh, wr_lo) + _dot(xl, wr_hi) + br
    lane = lax.broadcasted_iota(jnp.int32, (tm, 128), 1)
    lanef = lane.astype(F32)
    neg = jnp.float32(-jnp.inf)
    big = jnp.float32(1e9)
    isg = lane < N_GROUPS
    gmax = jnp.max(jnp.where(isg, logits, neg), axis=-1, keepdims=True)
    gidx = jnp.min(jnp.where(isg & (logits == gmax), lanef, big), axis=-1, keepdims=True)
    p_group = 1.0 / jnp.sum(jnp.where(isg, jnp.exp(logits - gmax), 0.0), axis=-1, keepdims=True)
    lo = N_GROUPS + EPG * gidx
    em = (lanef >= lo) & (lanef < lo + EPG)
    v1 = jnp.max(jnp.where(em, logits, neg), axis=-1, keepdims=True)
    i1 = jnp.min(jnp.where(em & (logits == v1), lanef, big), axis=-1, keepdims=True)
    em2 = em & (lanef != i1)
    v2 = jnp.max(jnp.where(em2, logits, neg), axis=-1, keepdims=True)
    i2 = jnp.min(jnp.where(em2 & (logits == v2), lanef, big), axis=-1, keepdims=True)
    e21 = jnp.exp(v2 - v1)
    gt1 = 1.0 / (1.0 + e21)
    gt2 = e21 * gt1

    sel1 = lanef == i1
    sel2 = lanef == i2
    onehot = jnp.where(sel1 | sel2, 1.0, 0.0)
    cnt = jnp.sum(onehot, axis=0, keepdims=True)
    seg = jnp.floor((cnt + (PIECE - 1)) * (1.0 / PIECE)) * PIECE
    er = lax.broadcasted_iota(jnp.int32, (128, 128), 0)
    ec = lax.broadcasted_iota(jnp.int32, (128, 128), 1)
    seg_off = _dot(jnp.broadcast_to(seg, (8, 128)).astype(BF16), jnp.where(er < ec, 1.0, 0.0).astype(BF16))[0:1]
    row = lax.broadcasted_iota(jnp.int32, (tm, tm), 0)
    col = lax.broadcasted_iota(jnp.int32, (tm, tm), 1)
    earlier = _dot(jnp.where(col < row, 1.0, 0.0).astype(BF16), onehot.astype(BF16))
    where_to = seg_off + earlier
    pos1 = jnp.sum(jnp.where(sel1, where_to, 0.0), axis=-1, keepdims=True)
    pos2 = jnp.sum(jnp.where(sel2, where_to, 0.0), axis=-1, keepdims=True)
    r = lax.broadcasted_iota(jnp.int32, (tm, rows_t), 1).astype(F32)
    place = jnp.where((r == pos1) | (r == pos2), 1.0, 0.0).astype(BF16)
    xs = _dot_tn(place, x1.astype(BF16)).astype(BF16)

    vals = (i1 - N_GROUPS, i2 - N_GROUPS, p_group * gt1, p_group * gt2, pos1, pos2)
    route = jnp.zeros((tm, 128), F32)
    for k, v in enumerate(vals):
        route = jnp.where(lane == k, v, route)
    return route, cnt, xs


PIECE = 16


def _tile_rows(tm):
    return -(-(2 * tm + N_EXPERTS * (PIECE - 1)) // 128) * 128


def _merge(x2, ys, lw, tm, sub):
    t = x2.shape[0]
    nt = t // tm
    rows_t = _tile_rows(tm)
    tok = lambda w: pl.BlockSpec((sub * tm, w), lambda i: (i, 0))
    full = lambda shape: pl.BlockSpec(shape, lambda i: (0,) * len(shape), pipeline_mode=pl.Buffered(1))
    return pl.pallas_call(
        functools.partial(_merge_kernel, tm=tm, sub=sub),
        grid=(nt // sub,),
        in_specs=[tok(D_MODEL), tok(BRANCH_W), tok(BRANCH_W), tok(BRANCH_W), tok(BRANCH_W),
                  full((D_MODEL, N_BRANCH * D_MODEL)), full((1, N_BRANCH * D_MODEL)),
                  full((N_BRANCH, BRANCH_W, D_MODEL)), full((D_MODEL, D_MODEL)),
                  full((1, D_MODEL)), full((1, D_MODEL)),
                  full((D_MODEL, 128)), full((D_MODEL, 128)), full((1, 128))],
        out_specs=[tok(D_MODEL), tok(128), pl.BlockSpec((sub, 1, 128), lambda i: (i, 0, 0)),
                   pl.BlockSpec((sub * rows_t, D_MODEL), lambda i: (i, 0))],
        out_shape=[jax.ShapeDtypeStruct((t, D_MODEL), F32), jax.ShapeDtypeStruct((t, 128), F32),
                   jax.ShapeDtypeStruct((nt, 1, 128), F32), jax.ShapeDtypeStruct((nt * rows_t, D_MODEL), BF16)],
        compiler_params=_cparams(("parallel",)),
        name="merge",
    )(x2, *ys, lw["wg"], lw["bg"], lw["wbr"], lw["wout"], lw["ln1_g"], lw["ln1_b"], lw["wr_hi"], lw["wr_lo"], lw["br"])


def _piece_copy(hbm, hbm_row, vmem, piece, sem, to_vmem):
    h = hbm.at[pl.ds(pl.multiple_of(hbm_row, PIECE), PIECE), :]
    v = vmem.at[pl.ds(pl.multiple_of(piece * PIECE, PIECE), PIECE), :]
    return pltpu.make_async_copy(h, v, sem) if to_vmem else pltpu.make_async_copy(v, h, sem)


def _expert_kernel(be_ref, nu_ref, nv_ref, off_ref, xs_hbm, wg_ref, wu_ref, wd_ref, ys_hbm, xin, yout, sem_in, sem_out,
                   *, bp):
    del xs_hbm
    blk = pl.program_id(0)
    n_used = nu_ref[0]
    slot = blk % 2

    def pieces(b, s, buf, sem, to_vmem, start):
        n = nv_ref[b]

        def one(p, priority):
            cp = _piece_copy(ys_hbm, off_ref[b * bp + p], buf.at[s], p, sem.at[s], to_vmem)
            cp.start(priority=priority) if start else cp.wait()

        def body(i, c):
            one(2 * i, 0)

            @pl.when(2 * i + 1 < n)
            def _():
                one(2 * i + 1, 1)
            return c
        lax.fori_loop(0, (n + 1) // 2, body, 0)

    @pl.when(blk == 0)
    def _():
        xin[...] = jnp.zeros_like(xin)

        @pl.when(n_used > 0)
        def _():
            pieces(0, 0, xin, sem_in, True, True)

    @pl.when(blk + 1 < n_used)
    def _():
        pieces(blk + 1, 1 - slot, xin, sem_in, True, True)

    @pl.when(blk < n_used)
    def _():
        pieces(blk, slot, xin, sem_in, True, False)

        @pl.when(blk >= 2)
        def _():
            pieces(blk - 2, slot, yout, sem_out, False, False)

        xb = xin[slot]
        hid = jax.nn.silu(_dot(xb, wg_ref[...].astype(BF16))) * _dot(xb, wu_ref[...].astype(BF16))
        yout[slot] = _dot(hid.astype(BF16), wd_ref[...].astype(BF16)).astype(BF16)
        pieces(blk, slot, yout, sem_out, False, True)

    @pl.when(blk == n_used - 1)
    def _():
        @pl.when(blk >= 1)
        def _():
            pieces(blk - 1, 1 - slot, yout, sem_out, False, False)
        pieces(blk, slot, yout, sem_out, False, False)


def _experts(xs, block_expert, n_used, n_valid, piece_off, lw, bp):
    n_blocks = block_expert.shape[0]
    rows = bp * PIECE
    li = lw["layer"]
    wspec = lambda shape: pl.BlockSpec((None, None) + shape, lambda b, be, nu, nv, off: (li, be[b], 0, 0))
    grid_spec = pltpu.PrefetchScalarGridSpec(
        num_scalar_prefetch=4,
        grid=(n_blocks,),
        in_specs=[pl.BlockSpec(memory_space=pl.ANY), wspec((D_MODEL, D_EXPERT)), wspec((D_MODEL, D_EXPERT)),
                  wspec((D_EXPERT, D_MODEL))],
        out_specs=pl.BlockSpec(memory_space=pl.ANY),
        scratch_shapes=[pltpu.VMEM((2, rows, D_MODEL), BF16), pltpu.VMEM((2, rows, D_MODEL), BF16),
                        pltpu.SemaphoreType.DMA((2,)), pltpu.SemaphoreType.DMA((2,))],
    )
    return pl.pallas_call(
        functools.partial(_expert_kernel, bp=bp),
        grid_spec=grid_spec,
        out_shape=jax.ShapeDtypeStruct(xs.shape, BF16),
        input_output_aliases={4: 0},
        compiler_params=_cparams(("arbitrary",)),
        name="experts",
    )(block_expert, n_used, n_valid, piece_off, xs, lw["we_gate"], lw["we_up"], lw["we_down"])


def _combine_kernel(ys_ref, x1_ref, route_ref, g2_ref, b2_ref, out_ref, *, tm):
    route = route_ref[...]
    ys = ys_ref[...]
    r = lax.broadcasted_iota(jnp.int32, (tm, ys.shape[0]), 1).astype(F32)
    ffn = jnp.zeros((tm, D_MODEL), F32)
    for j in range(2):
        pick = jnp.where(r == route[:, 4 + j:5 + j], 1.0, 0.0).astype(BF16)
        ffn = ffn + route[:, 2 + j:3 + j] * _dot(pick, ys)
    out_ref[...] = _layer_norm(ALPHA * x1_ref[...] + ffn, g2_ref[...], b2_ref[...])


def _combine(ys, x1, route, lw, tm):
    t = x1.shape[0]
    rows_t = _tile_rows(tm)
    full = lambda shape: pl.BlockSpec(shape, lambda i: (0,) * len(shape))
    return pl.pallas_call(
        functools.partial(_combine_kernel, tm=tm),
        grid=(t // tm,),
        in_specs=[pl.BlockSpec((rows_t, D_MODEL), lambda i: (i, 0)), pl.BlockSpec((tm, D_MODEL), lambda i: (i, 0)),
                  pl.BlockSpec((tm, 128), lambda i: (i, 0)), full((1, D_MODEL)), full((1, D_MODEL))],
        out_specs=pl.BlockSpec((tm, D_MODEL), lambda i: (i, 0)),
        out_shape=jax.ShapeDtypeStruct((t, D_MODEL), F32),
        compiler_params=_cparams(("parallel",)),
        name="combine",
    )(ys, x1, route, lw["ln2_g"], lw["ln2_b"])


def _piece_tables(cnt, tm, bp):
    nt = cnt.shape[0]
    rows_t = _tile_rows(tm)
    counts = cnt[:, 0, N_GROUPS:N_GROUPS + N_EXPERTS].astype(jnp.int32)
    pcs = (counts + PIECE - 1) // PIECE
    seg_row = (jnp.cumsum(pcs, axis=1) - pcs) * PIECE + jnp.arange(nt, dtype=jnp.int32)[:, None] * rows_t
    total = jnp.sum(pcs, axis=0)
    pad = (total + bp - 1) // bp * bp - total
    run_len = jnp.concatenate([pcs.T, pad[:, None]], axis=1).reshape(-1)
    run_row = jnp.concatenate([seg_row.T, jnp.zeros((N_EXPERTS, 1), jnp.int32)], axis=1).reshape(-1)
    run_real = jnp.concatenate([jnp.ones((N_EXPERTS, nt), jnp.int32), jnp.zeros((N_EXPERTS, 1), jnp.int32)],
                               axis=1).reshape(-1)
    run_expert = jnp.repeat(jnp.arange(N_EXPERTS, dtype=jnp.int32), nt + 1)
    run_end = jnp.cumsum(run_len)
    run_start = run_end - run_len
    n_blocks = -(-(nt * rows_t // PIECE + N_EXPERTS * (bp - 1)) // bp)
    g = jnp.arange(n_blocks * bp, dtype=jnp.int32)[:, None]
    inside = ((run_start[None, :] <= g) & (g < run_end[None, :])).astype(jnp.int32)
    real = inside * run_real[None, :]
    off = jnp.sum(real * (run_row[None, :] + (g - run_start[None, :]) * PIECE), axis=1).astype(jnp.int32)
    n_valid = jnp.sum(jnp.sum(real, axis=1).reshape(n_blocks, bp), axis=1).astype(jnp.int32)
    block_expert = jnp.sum(inside * run_expert[None, :], axis=1).reshape(n_blocks, bp)[:, 0].astype(jnp.int32)
    n_used = (run_end[-1] // bp).astype(jnp.int32).reshape(1)
    return block_expert, n_used, n_valid, off


def _moe(x1, route, cnt, xs, lw, bp, tm):
    ys = _experts(xs, *_piece_tables(cnt, tm, bp), lw, bp)
    return _combine(ys, x1, route, lw, tm)


CHUNK = 128
STEP_CHUNKS = 4


def _dot_nt(a, b):
    return lax.dot_general(a, b, (((1,), (1,)), ((), ())), preferred_element_type=F32)


def _dot_tn(a, b):
    return lax.dot_general(a, b, (((0,), (0,)), ((), ())), preferred_element_type=F32)


def _lane_head(width, group):
    return lax.broadcasted_iota(jnp.int32, (1, width), 1) // group


def _block_diag_mask(rows_head, cols_head):
    return rows_head == cols_head


def _head_norm(o, gain):
    r = lax.broadcasted_iota(jnp.int32, (BRANCH_W, BRANCH_W), 0) // HD
    c = lax.broadcasted_iota(jnp.int32, (BRANCH_W, BRANCH_W), 1) // HD
    avg = jnp.where(r == c, 1.0 / HD, 0.0).astype(BF16)
    mu = _dot_exact_rhs(o, avg)
    xc = o - mu
    var = _dot_exact_rhs(xc * xc, avg)
    return xc * lax.rsqrt(var + LN_EPS) * gain


def _stack_heads(x, head_of_lane):
    return jnp.concatenate([jnp.where(head_of_lane == h, x, 0.0) for h in range(N_HEADS)], axis=0)


def _pick_heads(stacked, rows, head_of_lane):
    out = jnp.where(head_of_lane == 0, stacked[0:rows], 0.0)
    for h in range(1, N_HEADS):
        out = jnp.where(head_of_lane == h, stacked[h * rows:(h + 1) * rows], out)
    return out


def _spread_heads(cols, head_of_lane):
    out = jnp.where(head_of_lane == 0, cols[0], 0.0)
    for h in range(1, N_HEADS):
        out = jnp.where(head_of_lane == h, cols[h], out)
    return out


SWA_BLOCKS = 4


def _swa_prompt_kernel(q_ref, kv_ref, kvp_ref, sink_ref, y_ref):
    n = pl.program_id(1)
    lo = lax.broadcasted_iota(jnp.int32, (1, 128), 1) < HD
    e = lax.broadcasted_iota(jnp.int32, (2 * WINDOW, N_HEADS * WINDOW), 0)
    col = lax.broadcasted_iota(jnp.int32, (2 * WINDOW, N_HEADS * WINDOW), 1)
    a = col % WINDOW
    band = (e > a) & (e <= a + WINDOW)
    head = lax.broadcasted_iota(jnp.int32, (1, N_HEADS * WINDOW), 1) // WINDOW
    sink = _spread_heads([sink_ref[:, h:h + 1] for h in range(N_HEADS)], head)
    blocks = range(SWA_BLOCKS)
    scores, vexts = [], []
    for i in blocks:
        rows = slice(i * WINDOW, (i + 1) * WINDOW)
        q = q_ref[rows, :]
        qa, qb = q[:, 0:128], q[:, 128:256]
        kv = kv_ref[rows, :]
        kvp = kvp_ref[...] if i == 0 else kv_ref[(i - 1) * WINDOW:i * WINDOW, :]
        kext = jnp.concatenate([kvp[:, 0:128], kv[:, 0:128]], axis=0).astype(BF16)
        vexts.append(jnp.concatenate([kvp[:, 128:256], kv[:, 128:256]], axis=0).astype(BF16))
        qs = jnp.concatenate([jnp.where(lo, qa, 0.0), jnp.where(lo, qb, 0.0),
                              jnp.where(lo, 0.0, qa), jnp.where(lo, 0.0, qb)], axis=0).astype(BF16)
        scores.append(_dot_nt(kext, qs))
    probs, inv_den = [], []
    for i in blocks:
        valid = band & ((n > 0) | (e >= WINDOW)) if i == 0 else band
        s = jnp.where(valid, scores[i] * (HD ** -0.5), -jnp.inf)
        mx = jnp.maximum(jnp.max(s, axis=0, keepdims=True), sink)
        p = jnp.exp(s - mx)
        inv_den.append(1.0 / (jnp.sum(p, axis=0, keepdims=True) + jnp.exp(sink - mx)))
        probs.append(p.astype(BF16))
    outs = [_dot_tn(vexts[i], probs[i]) for i in blocks]
    for i in blocks:
        o = outs[i] * inv_den[i]
        ya = jnp.concatenate([o[0:HD, 0:128], o[HD:128, 256:384]], axis=0).T
        yb = jnp.concatenate([o[0:HD, 128:256], o[HD:128, 384:512]], axis=0).T
        y_ref[i * WINDOW:(i + 1) * WINDOW, :] = jnp.concatenate([ya, yb], axis=1)


def _swa_prompt(swa, lw, bsz, seq):
    qt = SWA_BLOCKS * WINDOW
    nt = seq // qt
    return pl.pallas_call(
        _swa_prompt_kernel,
        grid=(bsz, nt),
        in_specs=[pl.BlockSpec((qt, 256), lambda b, n: (b * nt + n, 0)),
                  pl.BlockSpec((qt, 256), lambda b, n: (b * nt + n, 1)),
                  pl.BlockSpec((WINDOW, 256), lambda b, n: (jnp.maximum((b * nt + n) * SWA_BLOCKS - 1, 0), 1)),
                  pl.BlockSpec((1, 128), lambda b, n: (0, 0))],
        out_specs=pl.BlockSpec((qt, BRANCH_W), lambda b, n: (b * nt + n, 0)),
        out_shape=jax.ShapeDtypeStruct((bsz * seq, BRANCH_W), F32),
        compiler_params=_cparams(("parallel", "parallel")),
        name="swa_prompt",
    )(swa, swa, swa, lw["sinks"])


def _rotate(x, cos, sin):
    x1, x2 = x[:, 0:128], x[:, 128:256]
    return jnp.concatenate([x1 * cos - x2 * sin, x1 * sin + x2 * cos], axis=1)


def _ret_prompt_kernel(r_ref, cos_ref, sin_ref, dmat_ref, qdec_ref, kdec_ref, cdec_ref, g_ref, y_ref, s_out_ref, s_ref):
    n = pl.program_id(1)

    @pl.when(n == 0)
    def _():
        s_ref[...] = jnp.zeros_like(s_ref)

    c = CHUNK
    chunks = range(STEP_CHUNKS)
    hq = _lane_head(BRANCH_W, 32) % N_HEADS
    hv = _lane_head(BRANCH_W, HD)
    rows_head = (lax.broadcasted_iota(jnp.int32, (BRANCH_W, 1), 0) // 32) % N_HEADS
    qr, kr, v = [], [], []
    for i in chunks:
        rows = slice(i * c, (i + 1) * c)
        cos, sin = cos_ref[rows, :], sin_ref[rows, :]
        qr.append(_rotate(r_ref[rows, 0:256], cos, sin))
        kr.append(_rotate(r_ref[rows, 256:512], cos, sin) * (HD ** -0.5))
        v.append(r_ref[rows, 512:768].astype(BF16))
    att = [_dot_nt(_stack_heads(qr[i], hq).astype(BF16), kr[i].astype(BF16)) * dmat_ref[...] for i in chunks]
    upd = [_dot_tn((kr[i] * kdec_ref[...]).astype(BF16), v[i]) for i in chunks]
    states = [s_ref[...]]
    for i in chunks:
        states.append(cdec_ref[...] * states[i] + jnp.where(rows_head == hv, upd[i], 0.0))
    o = [_pick_heads(_dot(att[i].astype(BF16), v[i]), c, hv) for i in chunks]
    o = [o[i] + _dot((qr[i] * qdec_ref[...]).astype(BF16), states[i].astype(BF16)) for i in chunks]
    s_ref[...] = states[-1]
    s_out_ref[...] = states[-1]
    y_ref[...] = jax.nn.silu(r_ref[:, 768:1024]) * _head_norm(jnp.concatenate(o, axis=0), g_ref[...])


def _ret_tables(seq):
    c = CHUNK
    half = HD // 2
    freq = ROPE_BASE ** (-jnp.arange(half, dtype=F32) / half)
    ang = jnp.arange(seq).astype(F32)[:, None] * freq[None, :]
    cos = jnp.tile(jnp.cos(ang), (1, N_HEADS))
    sin = jnp.tile(jnp.sin(ang), (1, N_HEADS))
    log_g = jnp.log1p(-jnp.exp2(-5.0 - jnp.arange(N_HEADS, dtype=F32)))
    idx = jnp.arange(c, dtype=F32)
    rel = idx[:, None] - idx[None, :]
    dmat = jnp.exp(jnp.where(rel[None] >= 0, rel[None] * log_g[:, None, None], -jnp.inf)).reshape(N_HEADS * c, c)
    lane_head_q = (jnp.arange(BRANCH_W) // 32) % N_HEADS
    qdec = jnp.exp((idx + 1.0)[:, None] * log_g[lane_head_q][None, :])
    kdec = jnp.exp((c - 1.0 - idx)[:, None] * log_g[lane_head_q][None, :])
    cdec = jnp.exp(c * log_g)[jnp.arange(BRANCH_W) // HD][None, :]
    return cos, sin, dmat, qdec, kdec, cdec


def _ret_prompt(ret, lw, bsz, seq):
    c = CHUNK
    rows = STEP_CHUNKS * c
    nc = seq // rows
    cos, sin, dmat, qdec, kdec, cdec = _ret_tables(seq)
    full = lambda shape: pl.BlockSpec(shape, lambda b, n: (0,) * len(shape))
    return pl.pallas_call(
        _ret_prompt_kernel,
        grid=(bsz, nc),
        in_specs=[pl.BlockSpec((rows, W_RET), lambda b, n: (b * nc + n, 0)),
                  pl.BlockSpec((rows, 128), lambda b, n: (n, 0)), pl.BlockSpec((rows, 128), lambda b, n: (n, 0)),
                  full((N_HEADS * c, c)), full((c, BRANCH_W)), full((c, BRANCH_W)), full((1, BRANCH_W)),
                  full((1, BRANCH_W))],
        out_specs=[pl.BlockSpec((rows, BRANCH_W), lambda b, n: (b * nc + n, 0)),
                   pl.BlockSpec((None, BRANCH_W, BRANCH_W), lambda b, n: (b, 0, 0))],
        out_shape=[jax.ShapeDtypeStruct((bsz * seq, BRANCH_W), F32),
                   jax.ShapeDtypeStruct((bsz, BRANCH_W, BRANCH_W), F32)],
        scratch_shapes=[pltpu.VMEM((BRANCH_W, BRANCH_W), F32)],
        compiler_params=_cparams(("parallel", "arbitrary")),
        name="ret_prompt",
    )(ret, cos, sin, dmat, qdec, kdec, cdec, lw["ret_g"])


def _ret_state_from_blockdiag(s_bd):
    b = s_bd.shape[0]
    s = s_bd.reshape(b, 2, N_HEADS, 32, N_HEADS, HD)
    s = jnp.stack([s[:, :, h, :, h, :] for h in range(N_HEADS)], axis=1)
    return s.reshape(b, N_HEADS, HD, HD)


ML_I_LANE = GLA_RANK
ML_F_LANE = GLA_RANK + N_HEADS


def _ml_prompt_kernel(m_ref, sm_ref, g_ref, y_ref, c_out_ref, n_out_ref, m_out_ref, c_ref, n_ref, mm_ref):
    nchunk = pl.program_id(1)

    @pl.when(nchunk == 0)
    def _():
        c_ref[...] = jnp.zeros_like(c_ref)
        n_ref[...] = jnp.zeros_like(n_ref)
        mm_ref[...] = jnp.zeros_like(mm_ref)

    c = CHUNK
    chunks = range(STEP_CHUNKS)
    row = lax.broadcasted_iota(jnp.int32, (c, c), 0)
    col = lax.broadcasted_iota(jnp.int32, (c, c), 1)
    tri_lo = jnp.where(col <= row, 1.0, 0.0).astype(BF16)
    tri_up = jnp.where(row <= col, 1.0, 0.0).astype(BF16)
    hv = _lane_head(BRANCH_W, HD)
    hq = _lane_head(N_HEADS * c, c)
    key = lax.broadcasted_iota(jnp.int32, (c, N_HEADS * c), 0)
    qry = lax.broadcasted_iota(jnp.int32, (c, N_HEADS * c), 1) % c
    causal_t = key <= qry
    r = lax.broadcasted_iota(jnp.int32, (BRANCH_W, BRANCH_W), 0) // HD
    cc = lax.broadcasted_iota(jnp.int32, (BRANCH_W, BRANCH_W), 1) // HD
    same_head = r == cc
    lane = lax.broadcasted_iota(jnp.int32, (1, 128), 1)
    gate_lane = lax.broadcasted_iota(jnp.int32, (128, 1), 0)
    is_i = (gate_lane >= ML_I_LANE) & (gate_lane < ML_I_LANE + N_HEADS)
    is_f = (gate_lane >= ML_F_LANE) & (gate_lane < ML_F_LANE + N_HEADS)
    gate_head = jnp.where(is_i, gate_lane - ML_I_LANE, gate_lane - ML_F_LANE)
    sel_q = jnp.where((is_i | is_f) & (gate_head == hq), 1.0, 0.0).astype(BF16)
    sel_v = jnp.where((is_i | is_f) & (gate_head == hv), 1.0, 0.0).astype(BF16)
    lane128 = lax.broadcasted_iota(jnp.int32, (1, 128), 1)
    in_i = (lane128 >= ML_I_LANE) & (lane128 < ML_I_LANE + N_HEADS)
    in_f = (lane128 >= ML_F_LANE) & (lane128 < ML_F_LANE + N_HEADS)
    head_rows = lax.broadcasted_iota(jnp.int32, (8, 1), 0)

    q, ks, vt, brow, u_q, u_v, sc = [], [], [], [], [], [], []
    for i in chunks:
        rows = slice(i * c, (i + 1) * c)
        q.append(m_ref[rows, 0:256])
        ks.append(m_ref[rows, 256:512] * (HD ** -0.5))
        vt.append(m_ref[rows, 512:768].T.astype(BF16))
        sm = sm_ref[rows, :]
        bcols = _dot_exact_lhs(tri_lo, jax.nn.log_sigmoid(sm))
        brows = _dot_exact_rhs(jax.nn.log_sigmoid(sm.T), tri_up)
        brow.append(jnp.concatenate([brows[ML_F_LANE + h:ML_F_LANE + h + 1, :] for h in range(N_HEADS)], axis=1))
        z = jnp.where(in_i, sm, 0.0) - jnp.where(in_f, bcols, 0.0)
        u_q.append(_dot_exact_rhs(z, sel_q))
        u_v.append(_dot_exact_rhs(z, sel_v))
        sc.append(_dot_nt(ks[i].astype(BF16), _stack_heads(q[i], hv).astype(BF16)))

    mm = mm_ref[...]
    qk, w_inter, rowsum, floor, khat, carry_b = [], [], [], [], [], []
    for i in chunks:
        m_prev = _spread_heads([mm[:, h:h + 1] for h in range(N_HEADS)], hq)
        d_log = jnp.where(causal_t, brow[i] + u_q[i], -jnp.inf)
        inter_log = brow[i] + m_prev
        m_t = jnp.maximum(inter_log, jnp.max(d_log, axis=0, keepdims=True))
        qk_i = sc[i] * jnp.exp(d_log - m_t)
        qk.append(qk_i.astype(BF16))
        rowsum.append(jnp.sum(qk_i, axis=0, keepdims=True))
        w_inter.append(jnp.exp(inter_log - m_t))
        floor.append(jnp.exp(-m_t))
        m_new = [m_t[:, (h + 1) * c - 1:(h + 1) * c] for h in range(N_HEADS)]
        b_last = [brow[i][:, (h + 1) * c - 1:(h + 1) * c] for h in range(N_HEADS)]
        carry_b.append(_spread_heads([jnp.exp(b_last[h] + mm[:, h:h + 1] - m_new[h]) for h in range(N_HEADS)], hv))
        khat.append(ks[i] * jnp.exp(u_v[i] + _spread_heads([b_last[h] - m_new[h] for h in range(N_HEADS)], hv)))
        mm = jnp.zeros((1, 128), F32)
        for h in range(N_HEADS):
            mm = jnp.where(lane == h, m_new[h], mm)

    intra = [_dot(vt[i], qk[i]) for i in chunks]
    upd = [_dot(vt[i], khat[i].astype(BF16)) for i in chunks]
    cms, nrows = [c_ref[...]], [n_ref[...]]
    for i in chunks:
        cms.append(carry_b[i] * cms[i] + jnp.where(same_head, upd[i], 0.0))
        nrows.append(carry_b[i] * nrows[i] + jnp.sum(khat[i], axis=0, keepdims=True))
    hout = []
    for i in chunks:
        qb = q[i].astype(BF16)
        inter = _dot_nt(cms[i].astype(BF16), qb)
        n_sel = jnp.where(head_rows == hv, nrows[i], 0.0)
        qn = sum(_dot_nt(p, qb) for p in _split3(n_sel))
        blocks = []
        for h in range(N_HEADS):
            cols = slice(h * c, (h + 1) * c)
            den = w_inter[i][:, cols] * qn[h:h + 1, :] + rowsum[i][:, cols]
            num = w_inter[i][:, cols] * inter[h * HD:(h + 1) * HD, :] + intra[i][h * HD:(h + 1) * HD, cols]
            blocks.append(num / jnp.maximum(jnp.abs(den), floor[i][:, cols]))
        hout.append(jnp.concatenate(blocks, axis=0).T)
    y_ref[...] = _head_norm(jax.nn.sigmoid(m_ref[:, 768:1024]) * jnp.concatenate(hout, axis=0), g_ref[...])

    c_ref[...] = cms[-1]
    n_ref[...] = nrows[-1]
    mm_ref[...] = mm
    c_out_ref[...] = cms[-1]
    n_out_ref[...] = nrows[-1]
    m_out_ref[...] = mm


def _ml_prompt(ml, small, lw, bsz, seq):
    rows = STEP_CHUNKS * CHUNK
    nc = seq // rows
    return pl.pallas_call(
        _ml_prompt_kernel,
        grid=(bsz, nc),
        in_specs=[pl.BlockSpec((rows, W_ML), lambda b, n: (b * nc + n, 0)),
                  pl.BlockSpec((rows, W_SMALL), lambda b, n: (b * nc + n, 0)),
                  pl.BlockSpec((1, BRANCH_W), lambda b, n: (0, 0))],
        out_specs=[pl.BlockSpec((rows, BRANCH_W), lambda b, n: (b * nc + n, 0)),
                   pl.BlockSpec((None, BRANCH_W, BRANCH_W), lambda b, n: (b, 0, 0)),
                   pl.BlockSpec((None, 1, BRANCH_W), lambda b, n: (b, 0, 0)),
                   pl.BlockSpec((None, 1, 128), lambda b, n: (b, 0, 0))],
        out_shape=[jax.ShapeDtypeStruct((bsz * seq, BRANCH_W), F32),
                   jax.ShapeDtypeStruct((bsz, BRANCH_W, BRANCH_W), F32),
                   jax.ShapeDtypeStruct((bsz, 1, BRANCH_W), F32),
                   jax.ShapeDtypeStruct((bsz, 1, 128), F32)],
        scratch_shapes=[pltpu.VMEM((BRANCH_W, BRANCH_W), F32), pltpu.VMEM((1, BRANCH_W), F32),
                        pltpu.VMEM((1, 128), F32)],
        compiler_params=_cparams(("parallel", "arbitrary")),
        name="ml_prompt",
    )(ml, small, lw["ml_g"])


def _diag_blocks(s_bd):
    b = s_bd.shape[0]
    s = s_bd.reshape(b, N_HEADS, HD, N_HEADS, HD)
    return jnp.stack([s[:, h, :, h, :] for h in range(N_HEADS)], axis=1)


GLA_TILE = 256
GLA_SUB = 16


def _gla_prompt_kernel(x_ref, sm_ref, wa2_ref, ba2_ref, g_ref, y_ref, s_out_ref, s_ref):
    n = pl.program_id(1)

    @pl.when(n == 0)
    def _():
        s_ref[...] = jnp.zeros_like(s_ref)

    tq, sc = GLA_TILE, GLA_SUB
    x = x_ref[...]
    q = x[:, 0:128] * (GLA_DK ** -0.5)
    k = x[:, 128:256]
    v = x[:, 256:512]
    log_a = jax.nn.log_sigmoid(_dot(sm_ref[...].astype(BF16), wa2_ref[...]) + ba2_ref[...]) / GLA_TAU
    r = lax.broadcasted_iota(jnp.int32, (tq, tq), 0)
    c = lax.broadcasted_iota(jnp.int32, (tq, tq), 1)
    same = (r // sc) == (c // sc)
    b = _dot_exact_lhs(jnp.where(same & (c <= r), 1.0, 0.0).astype(BF16), log_a)
    btot = _dot_exact_lhs(jnp.where(same, 1.0, 0.0).astype(BF16), log_a)
    qd = (q * jnp.exp(b)).astype(BF16)
    khat_t = (k * jnp.exp(btot - b)).T
    dec_t = jnp.exp(btot).T
    vb = v.astype(BF16)
    kh = lax.broadcasted_iota(jnp.int32, (128, 1), 0) // GLA_DK
    hv = _lane_head(BRANCH_W, HD)
    expand = jnp.where(kh == hv, 1.0, 0.0).astype(BF16)
    lane_chunk = lax.broadcasted_iota(jnp.int32, (1, tq), 1) // sc
    trow = lax.broadcasted_iota(jnp.int32, (sc, 1), 0)
    chunks = range(tq // sc)
    upds = [_dot(jnp.where(lane_chunk == ci, khat_t, 0.0).astype(BF16), vb) for ci in chunks]
    states = [s_ref[...]]
    for ci in chunks:
        states.append(dec_t[:, ci * sc:ci * sc + 1] * states[ci] + jnp.where(kh == hv, upds[ci], 0.0))
    outs = []
    for ci in chunks:
        lo = ci * sc
        qc, kc, bc, vc = q[lo:lo + sc], k[lo:lo + sc], b[lo:lo + sc], v[lo:lo + sc]
        pair = [qc * kc[j:j + 1] * jnp.exp(jnp.where(trow >= j, bc - bc[j:j + 1], -jnp.inf)) for j in range(sc)]
        att = _dot(jnp.concatenate(pair, axis=0).astype(BF16), expand)
        o = att[0:sc] * vc[0:1]
        for j in range(1, sc):
            o = o + att[j * sc:(j + 1) * sc] * vc[j:j + 1]
        outs.append(o)
    outs = [outs[ci] + _dot(qd[ci * sc:(ci + 1) * sc], states[ci].astype(BF16)) for ci in chunks]
    s_ref[...] = states[-1]
    s_out_ref[...] = states[-1]
    y_ref[...] = jax.nn.silu(x[:, 512:768]) * _head_norm(jnp.concatenate(outs, axis=0), g_ref[...])


def _gla_prompt(gla, small, lw, bsz, seq):
    tq = GLA_TILE
    nt = seq // tq
    full = lambda shape: pl.BlockSpec(shape, lambda b, n: (0,) * len(shape))
    return pl.pallas_call(
        _gla_prompt_kernel,
        grid=(bsz, nt),
        in_specs=[pl.BlockSpec((tq, W_GLA), lambda b, n: (b * nt + n, 0)),
                  pl.BlockSpec((tq, W_SMALL), lambda b, n: (b * nt + n, 0)),
                  full((128, 128)), full((1, 128)), full((1, BRANCH_W))],
        out_specs=[pl.BlockSpec((tq, BRANCH_W), lambda b, n: (b * nt + n, 0)),
                   pl.BlockSpec((None, 128, BRANCH_W), lambda b, n: (b, 0, 0))],
        out_shape=[jax.ShapeDtypeStruct((bsz * seq, BRANCH_W), F32),
                   jax.ShapeDtypeStruct((bsz, 128, BRANCH_W), F32)],
        scratch_shapes=[pltpu.VMEM((128, BRANCH_W), F32)],
        compiler_params=_cparams(("parallel", "arbitrary")),
        name="gla_prompt",
    )(gla, small, lw["wa2"], lw["ba2"], lw["gla_g"])


def _gla_state_from_blockdiag(s_bd):
    b = s_bd.shape[0]
    s = s_bd.reshape(b, N_HEADS, GLA_DK, N_HEADS, HD)
    return jnp.stack([s[:, h, :, h, :] for h in range(N_HEADS)], axis=1)


LANES = 128


def _rank1_update(st_ref, snt_ref, q_ref, k_ref, decay_row, v, dk):
    def body(i, o):
        rows = pl.ds(pl.multiple_of(i * HD, HD), HD)
        new = decay_row(i) * st_ref[rows, :] + k_ref[pl.ds(i, 1), :] * v
        snt_ref[rows, :] = new
        return o + q_ref[pl.ds(i, 1), :] * new
    return lax.fori_loop(0, dk, body, jnp.zeros((HD, LANES), F32), unroll=4)


def _head_rows(h, n):
    return pl.ds(pl.multiple_of(h * n, n), n)


def _gla_sample_kernel(x_ref, sm_ref, wa2_ref, ba2_ref, g_ref, s_ref, y_ref, s_out_ref,
                       qt_ref, kt_ref, at_ref, vt_ref, ot_ref):
    h = pl.program_id(0)

    @pl.when(h == 0)
    def _():
        x = x_ref[...]
        qt_ref[...] = (x[:, 0:128] * (GLA_DK ** -0.5)).T
        kt_ref[...] = x[:, 128:256].T
        vt_ref[...] = x[:, 256:512].T
        log_a = jax.nn.log_sigmoid(_dot(sm_ref[...].astype(BF16), wa2_ref[...]) + ba2_ref[...]) / GLA_TAU
        at_ref[...] = jnp.exp(log_a).T

    base = h * GLA_DK
    o = _rank1_update(s_ref, s_out_ref, qt_ref.at[_head_rows(h, GLA_DK)], kt_ref.at[_head_rows(h, GLA_DK)],
                      lambda i: at_ref[pl.ds(base + i, 1), :], vt_ref[_head_rows(h, HD), :], GLA_DK)
    ot_ref[_head_rows(h, HD), :] = o

    @pl.when(h == N_HEADS - 1)
    def _():
        y_ref[...] = jax.nn.silu(x_ref[:, 512:768]) * _head_norm(ot_ref[...].T, g_ref[...])


def _gla_sample(gla, small, state, lw):
    per_head = GLA_DK * HD
    full = lambda shape: pl.BlockSpec(shape, lambda h: (0,) * len(shape))
    vm = lambda r: pltpu.VMEM((r, LANES), F32)
    return pl.pallas_call(
        _gla_sample_kernel,
        grid=(N_HEADS,),
        in_specs=[full((LANES, W_GLA)), full((LANES, W_SMALL)), full((128, 128)), full((1, 128)), full((1, BRANCH_W)),
                  pl.BlockSpec((per_head, LANES), lambda h: (h, 0))],
        out_specs=[full((LANES, BRANCH_W)), pl.BlockSpec((per_head, LANES), lambda h: (h, 0))],
        out_shape=[jax.ShapeDtypeStruct((LANES, BRANCH_W), F32), jax.ShapeDtypeStruct(state.shape, F32)],
        scratch_shapes=[vm(128), vm(128), vm(128), vm(256), vm(256)],
        compiler_params=_cparams(("arbitrary",)),
        name="gla_sample",
    )(gla, small, lw["wa2"], lw["ba2"], lw["gla_g"], state)


def _ml_sample_kernel(x_ref, sc_ref, n_ref, g_ref, c_ref, y_ref, c_out_ref, n_out_ref, m_out_ref,
                      qt_ref, kt_ref, vt_ref, ot_ref, w_ref, kp_ref):
    h = pl.program_id(0)

    @pl.when(h == 0)
    def _():
        x = x_ref[...]
        qt_ref[...] = x[:, 0:256].T
        kt_ref[...] = (x[:, 256:512] * (HD ** -0.5)).T
        vt_ref[...] = x[:, 512:768].T
        sct = sc_ref[...].T
        log_f = jax.nn.log_sigmoid(sct[8:16])
        m_old = sct[16:24]
        m_new = jnp.maximum(log_f + m_old, sct[0:8])
        w_ref[0:8, :] = jnp.exp(log_f + m_old - m_new)
        w_ref[8:16, :] = jnp.exp(sct[0:8] - m_new)
        w_ref[16:24, :] = m_new

    w_carry = w_ref[pl.ds(h, 1), :]
    w_tok = w_ref[pl.ds(8 + h, 1), :]
    m_new = w_ref[pl.ds(16 + h, 1), :]
    kp_ref[...] = kt_ref[_head_rows(h, HD), :] * w_tok
    q_h = qt_ref.at[_head_rows(h, HD)]
    n_new = w_carry * n_ref[_head_rows(h, HD), :] + kp_ref[...]
    num = _rank1_update(c_ref, c_out_ref, q_h, kp_ref, lambda i: w_carry, vt_ref[_head_rows(h, HD), :], HD)
    den = jnp.sum(q_h[...] * n_new, axis=0, keepdims=True)
    ot_ref[_head_rows(h, HD), :] = num / jnp.maximum(jnp.abs(den), jnp.exp(-m_new))
    n_out_ref[_head_rows(h, HD), :] = n_new

    @pl.when(h == N_HEADS - 1)
    def _():
        y_ref[...] = _head_norm(jax.nn.sigmoid(x_ref[:, 768:1024]) * ot_ref[...].T, g_ref[...])
        m_out_ref[...] = jnp.concatenate([w_ref[16:24, :], jnp.zeros((LANES - 8, LANES), F32)], axis=0).T


def _ml_sample(ml, scal, n_state, c_state, lw):
    per_head = HD * HD
    full = lambda shape: pl.BlockSpec(shape, lambda h: (0,) * len(shape))
    vm = lambda r: pltpu.VMEM((r, LANES), F32)
    return pl.pallas_call(
        _ml_sample_kernel,
        grid=(N_HEADS,),
        in_specs=[full((LANES, W_ML)), full((LANES, 128)), full((BRANCH_W, LANES)), full((1, BRANCH_W)),
                  pl.BlockSpec((per_head, LANES), lambda h: (h, 0))],
        out_specs=[full((LANES, BRANCH_W)), pl.BlockSpec((per_head, LANES), lambda h: (h, 0)),
                   full((BRANCH_W, LANES)), full((LANES, 128))],
        out_shape=[jax.ShapeDtypeStruct((LANES, BRANCH_W), F32), jax.ShapeDtypeStruct(c_state.shape, F32),
                   jax.ShapeDtypeStruct((BRANCH_W, LANES), F32), jax.ShapeDtypeStruct((LANES, 128), F32)],
        scratch_shapes=[vm(256), vm(256), vm(256), vm(256), vm(24), vm(HD)],
        compiler_params=_cparams(("arbitrary",)),
        name="ml_sample",
    )(ml, scal, n_state, lw["ml_g"], c_state)


def _ret_sample_kernel(x_ref, cos_ref, sin_ref, gam_ref, g_ref, s_ref, y_ref, s_out_ref,
                       qt_ref, kt_ref, vt_ref, ot_ref, qr_ref, kr_ref):
    h = pl.program_id(0)

    @pl.when(h == 0)
    def _():
        x = x_ref[...]
        qt_ref[...] = x[:, 0:256].T
        kt_ref[...] = (x[:, 256:512] * (HD ** -0.5)).T
        vt_ref[...] = x[:, 512:768].T

    cos, sin = cos_ref[...], sin_ref[...]
    for src, dst in ((qt_ref, qr_ref), (kt_ref, kr_ref)):
        x1 = src[_head_rows(h, 32), :]
        x2 = src[pl.ds(pl.multiple_of(128 + h * 32, 32), 32), :]
        dst[0:32, :] = x1 * cos - x2 * sin
        dst[32:64, :] = x1 * sin + x2 * cos
    gamma = gam_ref[pl.ds(h, 1), :]
    ot_ref[_head_rows(h, HD), :] = _rank1_update(s_ref, s_out_ref, qr_ref, kr_ref, lambda i: gamma,
                                                 vt_ref[_head_rows(h, HD), :], HD)

    @pl.when(h == N_HEADS - 1)
    def _():
        y_ref[...] = jax.nn.silu(x_ref[:, 768:1024]) * _head_norm(ot_ref[...].T, g_ref[...])


def _ret_sample(ret, state, lw):
    per_head = HD * HD
    half = HD // 2
    freq = ROPE_BASE ** (-jnp.arange(half, dtype=F32) / half)
    ang = jnp.full((1,), PAST_LEN, F32)[:, None] * freq[None, :]
    cos = jnp.broadcast_to(jnp.cos(ang).reshape(half, 1), (half, LANES))
    sin = jnp.broadcast_to(jnp.sin(ang).reshape(half, 1), (half, LANES))
    log_g = jnp.log1p(-jnp.exp2(-5.0 - jnp.arange(N_HEADS, dtype=F32)))
    gam = jnp.broadcast_to(jnp.concatenate([jnp.exp(log_g), jnp.ones((4,), F32)]).reshape(8, 1), (8, LANES))
    full = lambda shape: pl.BlockSpec(shape, lambda h: (0,) * len(shape))
    vm = lambda r: pltpu.VMEM((r, LANES), F32)
    return pl.pallas_call(
        _ret_sample_kernel,
        grid=(N_HEADS,),
        in_specs=[full((LANES, W_RET)), full((half, LANES)), full((half, LANES)), full((8, LANES)), full((1, BRANCH_W)),
                  pl.BlockSpec((per_head, LANES), lambda h: (h, 0))],
        out_specs=[full((LANES, BRANCH_W)), pl.BlockSpec((per_head, LANES), lambda h: (h, 0))],
        out_shape=[jax.ShapeDtypeStruct((LANES, BRANCH_W), F32), jax.ShapeDtypeStruct(state.shape, F32)],
        scratch_shapes=[vm(256), vm(256), vm(256), vm(256), vm(HD), vm(HD)],
        compiler_params=_cparams(("arbitrary",)),
        name="ret_sample",
    )(ret, cos, sin, gam, lw["ret_g"], state)


def _swa_sample_kernel(q_ref, kn_ref, vn_ref, sink_ref, kc_ref, vc_ref, o_ref, ko_ref, vo_ref):
    q = q_ref[...]
    kc, vc = kc_ref[...], vc_ref[...]
    kn, vn = kn_ref[...], vn_ref[...]
    scale = HD ** -0.5
    s = _einsum("bhd,bwd->bhw", q.astype(BF16), kc.astype(BF16)) * scale
    s_new = jnp.sum(q * kn, axis=-1, keepdims=True) * scale
    w = lax.broadcasted_iota(jnp.int32, (1, 1, WINDOW), 2)
    s = jnp.where(w > 0, s, -jnp.inf)
    sink = sink_ref[...][:, 0:1]
    mx = jnp.maximum(jnp.maximum(jnp.max(s, axis=-1, keepdims=True), s_new), sink)
    p = jnp.exp(s - mx)
    p_new = jnp.exp(s_new - mx)
    den = jnp.sum(p, axis=-1, keepdims=True) + p_new + jnp.exp(sink - mx)
    o = _einsum("bhw,bwd->bhd", p.astype(BF16), vc.astype(BF16)) + p_new * vn
    o_ref[...] = o / den
    ko_ref[:, 0:WINDOW - 1, :] = kc[:, 1:WINDOW, :]
    ko_ref[:, WINDOW - 1:WINDOW, :] = kn
    vo_ref[:, 0:WINDOW - 1, :] = vc[:, 1:WINDOW, :]
    vo_ref[:, WINDOW - 1:WINDOW, :] = vn


def _swa_sample(swa, cache_k, cache_v, lw, bt=16):
    bsz = swa.shape[0]
    lo = (jnp.arange(128) < HD)[None, :]
    qa, qb = swa[:, 0:128], swa[:, 128:256]
    q = jnp.stack([jnp.where(lo, qa, 0.0), jnp.where(lo, qb, 0.0), jnp.where(lo, 0.0, qa), jnp.where(lo, 0.0, qb)], axis=1)
    kn = swa[:, None, 256:384]
    vn = swa[:, None, 384:512]
    sink = jnp.broadcast_to(lw["sinks"][0, 0:N_HEADS].reshape(N_HEADS, 1), (N_HEADS, 128))
    blk = lambda shape: pl.BlockSpec(shape, lambda i: (i,) + (0,) * (len(shape) - 1))
    o, ko, vo = pl.pallas_call(
        _swa_sample_kernel,
        grid=(bsz // bt,),
        in_specs=[blk((bt, N_HEADS, 128)), blk((bt, 1, 128)), blk((bt, 1, 128)),
                  pl.BlockSpec((N_HEADS, 128), lambda i: (0, 0)),
                  blk((bt, WINDOW, 128)), blk((bt, WINDOW, 128))],
        out_specs=[blk((bt, N_HEADS, 128)), blk((bt, WINDOW, 128)), blk((bt, WINDOW, 128))],
        out_shape=[jax.ShapeDtypeStruct((bsz, N_HEADS, 128), F32), jax.ShapeDtypeStruct(cache_k.shape, F32),
                   jax.ShapeDtypeStruct(cache_v.shape, F32)],
        compiler_params=_cparams(("parallel",)),
        name="swa_sample",
    )(q, kn, vn, sink, cache_k, cache_v)
    y = jnp.concatenate([o[:, 0, 0:64], o[:, 2, 64:128], o[:, 1, 0:64], o[:, 3, 64:128]], axis=1)
    return y, ko, vo


def _ml_scalars(small, m_state):
    z4 = jnp.zeros((small.shape[0], 4), F32)
    return jnp.concatenate([small[:, ML_I_LANE:ML_I_LANE + 4], z4, small[:, ML_F_LANE:ML_F_LANE + 4], z4, m_state, z4,
                            jnp.zeros((small.shape[0], 128 - 24), F32)], axis=1)


def _take_cols(a, pieces, pad):
    parts = [a[..., s:s + n] for s, n in pieces]
    if pad:
        parts.append(jnp.zeros(a.shape[:-1] + (pad,), a.dtype))
    return jnp.concatenate(parts, axis=-1)


def _mix_pieces():
    p = [(0, 768)]
    sq = 784
    p += [(sq, 64), (sq + 128, 64), (sq + 64, 64), (sq + 192, 64)]
    p += [(1040, 256)]
    p += [(1296, 768), (2072, 256)]
    for base in (2328, 2584):
        for half in range(2):
            p += [(base + h * HD + half * 32, 32) for h in range(N_HEADS)]
    p += [(2840, 512)]
    p += [(768, 16), (2064, 8)]
    return p


def _prep_layer(li, w_in, b_in, gla_w_a2, gla_b_a2, gla_norm_g, swa_sinks, ml_norm_g, ret_norm_g, w_branch, w_out,
                ln1_g, ln1_b, w_router_grp, b_router_grp, w_router_exp, b_router_exp, w_exp_gate, w_exp_up,
                w_exp_down, ln2_g, ln2_b):
    pieces = _mix_pieces()
    pad = W_MIX - sum(n for _, n in pieces)
    row = lambda v: v.reshape(1, -1).astype(F32)
    wr = jnp.concatenate([w_router_grp[li], w_router_exp[li],
                          jnp.zeros((D_MODEL, 128 - N_GROUPS - N_EXPERTS), F32)], axis=1)
    wr_hi = wr.astype(BF16)
    wb = w_branch[li]
    wb1 = jnp.concatenate([wb[1, 0:64], wb[1, 128:192], wb[1, 64:128], wb[1, 192:256]], axis=0)
    return dict(
        wm=_take_cols(w_in[li], pieces, pad).astype(BF16),
        bm=row(_take_cols(b_in[li], pieces, pad)),
        wg=w_in[li][:, GATE_COL:].astype(BF16),
        bg=row(b_in[li][GATE_COL:]),
        wa2=jnp.concatenate([gla_w_a2[li], jnp.zeros((128 - GLA_RANK, 128), F32)], axis=0).astype(BF16),
        ba2=row(gla_b_a2[li]),
        gla_g=row(gla_norm_g[li]), ml_g=row(ml_norm_g[li]), ret_g=row(ret_norm_g[li]),
        sinks=row(jnp.concatenate([swa_sinks[li], jnp.zeros((128 - N_HEADS,), F32)])),
        wbr=jnp.stack([wb[0], wb1, wb[2], wb[3]]).astype(BF16),
        wout=w_out[li].astype(BF16),
        ln1_g=row(ln1_g[li]), ln1_b=row(ln1_b[li]), ln2_g=row(ln2_g[li]), ln2_b=row(ln2_b[li]),
        wr_hi=wr_hi, wr_lo=(wr - wr_hi.astype(F32)).astype(BF16),
        br=row(jnp.concatenate([b_router_grp[li], b_router_exp[li], jnp.zeros((128 - N_GROUPS - N_EXPERTS,), F32)])),
        layer=li, we_gate=w_exp_gate, we_up=w_exp_up, we_down=w_exp_down,
    )


TOK_TILE = 256
MOE_PIECES_PROMPT = 16
MOE_PIECES_SAMPLE = 1


MERGE_TILES = 2


def _ffn_half(x2, ys, lw, tm, bp):
    x1, route, cnt, xs = _merge(x2, ys, lw, tm, MERGE_TILES if x2.shape[0] // tm % MERGE_TILES == 0 else 1)
    return _moe(x1, route, cnt, xs, lw, bp, tm)


def _layer_prompt(x2, lw, bsz, seq):
    gla, swa, ml, ret, small = _inproj(x2, lw["wm"], lw["bm"], TOK_TILE)
    ya, s_gla = _gla_prompt(gla, small, lw, bsz, seq)
    yb = _swa_prompt(swa, lw, bsz, seq)
    yc, c_bd, n_row, m_row = _ml_prompt(ml, small, lw, bsz, seq)
    yd, s_ret = _ret_prompt(ret, lw, bsz, seq)
    kv_tail = swa.reshape(bsz, seq, W_SWA)[:, seq - WINDOW:, 256:512]
    states = (_gla_state_from_blockdiag(s_gla),
              kv_tail[:, :, 0:128].reshape(bsz, WINDOW, SWA_KV, HD),
              kv_tail[:, :, 128:256].reshape(bsz, WINDOW, SWA_KV, HD),
              jnp.swapaxes(_diag_blocks(c_bd), -1, -2), n_row.reshape(bsz, N_HEADS, HD), m_row[:, 0, 0:N_HEADS],
              _ret_state_from_blockdiag(s_ret))
    return _ffn_half(x2, (ya, yb, yc, yd), lw, TOK_TILE, MOE_PIECES_PROMPT), states


def _layer_sample(x2, st, lw):
    s_gla, k_buf, v_buf, c_ml, n_ml, m_ml, s_ret = st
    bsz = x2.shape[0]
    gla, swa, ml, ret, small = _inproj(x2, lw["wm"], lw["bm"], bsz)
    to_lanes = lambda s: jnp.moveaxis(s, 0, -1).reshape(-1, bsz)
    from_lanes = lambda s, like: jnp.moveaxis(s.reshape(like.shape[1:] + (bsz,)), -1, 0)
    ya, s_gla_new = _gla_sample(gla, small, to_lanes(s_gla), lw)
    yb, k_new, v_new = _swa_sample(swa, k_buf.reshape(bsz, WINDOW, SWA_KV * HD), v_buf.reshape(bsz, WINDOW, SWA_KV * HD), lw)
    yc, c_new, n_new, m_new = _ml_sample(ml, _ml_scalars(small, m_ml), to_lanes(n_ml), to_lanes(c_ml), lw)
    yd, s_ret_new = _ret_sample(ret, to_lanes(s_ret), lw)
    states = (from_lanes(s_gla_new, s_gla), k_new.reshape(k_buf.shape), v_new.reshape(v_buf.shape),
              from_lanes(c_new, c_ml), from_lanes(n_new, n_ml), m_new[:, 0:N_HEADS], from_lanes(s_ret_new, s_ret))
    return _ffn_half(x2, (ya, yb, yc, yd), lw, bsz, MOE_PIECES_SAMPLE), states


def kernel(x_prompt, x_sample, state_gla, cache_swa_k, cache_swa_v, state_mlstm_c, state_mlstm_n, state_mlstm_m,
           state_ret, w_in, b_in, gla_w_a2, gla_b_a2, gla_norm_g, swa_sinks, ml_norm_g, ret_norm_g, w_branch, w_out,
           ln1_g, ln1_b, w_router_grp, b_router_grp, w_router_exp, b_router_exp, w_exp_gate, w_exp_up, w_exp_down,
           ln2_g, ln2_b):
    bsz, seq, _ = x_prompt.shape
    dec_b, dec_seq, _ = x_sample.shape
    assert dec_seq == 1 and dec_b == LANES and seq % GLA_TILE == 0 and (bsz * seq) % TOK_TILE == 0
    weights = (w_in, b_in, gla_w_a2, gla_b_a2, gla_norm_g, swa_sinks, ml_norm_g, ret_norm_g, w_branch, w_out,
               ln1_g, ln1_b, w_router_grp, b_router_grp, w_router_exp, b_router_exp, w_exp_gate, w_exp_up,
               w_exp_down, ln2_g, ln2_b)
    y_p = x_prompt.reshape(bsz * seq, D_MODEL)
    y_s = x_sample.reshape(dec_b, D_MODEL)
    new_p, new_s = [], []
    for li in range(w_in.shape[0]):
        lw = _prep_layer(li, *weights)
        y_p, st_p = _layer_prompt(y_p, lw, bsz, seq)
        st_in = (state_gla[li], cache_swa_k[li], cache_swa_v[li], state_mlstm_c[li], state_mlstm_n[li],
                 state_mlstm_m[li], state_ret[li])
        y_s, st_s = _layer_sample(y_s, st_in, lw)
        new_p.append(st_p)
        new_s.append(st_s)
    outs = [y_p.reshape(bsz, seq, D_MODEL), y_s.reshape(dec_b, dec_seq, D_MODEL)]
    for i in range(7):
        outs.append(jnp.stack([s[i] for s in new_p], axis=0))
        outs.append(jnp.stack([s[i] for s in new_s], axis=0))
    return tuple(outs)
```

```python
import functools
import math

import jax
import jax.numpy as jnp
from jax import lax
from jax.experimental import pallas as pl
from jax.experimental.pallas import tpu as pltpu

F32 = jnp.float32
BF16 = jnp.bfloat16

D_MODEL = 1024
N_BRANCH = 4
BRANCH_W = D_MODEL // N_BRANCH
N_HEADS = 4
GLA_DK = 32
GLA_RANK = 16
GLA_TAU = 16.0
HD = 64
SWA_KV = 2
WINDOW = 128
ROPE_BASE = 10000.0
PAST_LEN = 16384
N_GROUPS = 4
EPG = 8
N_EXPERTS = N_GROUPS * EPG
D_EXPERT = D_MODEL // 4
DEPTH = 2
ALPHA = (2 * DEPTH) ** 0.25
LN_EPS = 1e-5
GATE_COL = 3352
VMEM_LIMIT = 56 * 1024 * 1024

W_GLA, W_SWA, W_ML, W_RET, W_SMALL = 768, 512, 1024, 1024, 128
W_MIX = W_GLA + W_SWA + W_ML + W_RET + W_SMALL


def _cparams(sem):
    return pltpu.CompilerParams(dimension_semantics=sem, vmem_limit_bytes=VMEM_LIMIT)


def _dot(a, b):
    return jnp.dot(a, b, preferred_element_type=F32)


def _einsum(spec, a, b):
    return jnp.einsum(spec, a, b, preferred_element_type=F32)


def _split2(x):
    hi = x.astype(BF16)
    lo = (x - hi.astype(F32)).astype(BF16)
    return hi, lo


def _split3(x):
    hi = x.astype(BF16)
    r = x - hi.astype(F32)
    mid = r.astype(BF16)
    lo = (r - mid.astype(F32)).astype(BF16)
    return hi, mid, lo


def _dot_exact_lhs(a_exact, b):
    return sum(_dot(a_exact, p) for p in _split3(b))


def _dot_exact_rhs(a, b_exact):
    return sum(_dot(p, b_exact) for p in _split3(a))


def _layer_norm(x, g, b):
    mu = jnp.mean(x, axis=-1, keepdims=True)
    xc = x - mu
    var = jnp.mean(xc * xc, axis=-1, keepdims=True)
    return xc * lax.rsqrt(var + LN_EPS) * g + b


def _inproj_kernel(x_ref, w_ref, b_ref, gla_ref, swa_ref, ml_ref, ret_ref, sm_ref):
    xb = x_ref[...].astype(BF16)
    off = 0
    for ref, width in ((gla_ref, W_GLA), (swa_ref, W_SWA), (ml_ref, W_ML), (ret_ref, W_RET), (sm_ref, W_SMALL)):
        ref[...] = _dot(xb, w_ref[:, off:off + width]) + b_ref[:, off:off + width]
        off += width


def _inproj(x2, wm, bm, tm):
    t = x2.shape[0]
    widths = (W_GLA, W_SWA, W_ML, W_RET, W_SMALL)
    return pl.pallas_call(
        _inproj_kernel,
        grid=(t // tm,),
        in_specs=[pl.BlockSpec((tm, D_MODEL), lambda i: (i, 0)),
                  pl.BlockSpec((D_MODEL, W_MIX), lambda i: (0, 0)),
                  pl.BlockSpec((1, W_MIX), lambda i: (0, 0))],
        out_specs=[pl.BlockSpec((tm, w), lambda i: (i, 0)) for w in widths],
        out_shape=[jax.ShapeDtypeStruct((t, w), F32) for w in widths],
        compiler_params=_cparams(("parallel",)),
        name="inproj",
    )(x2, wm, bm)


def _merge_kernel(x_ref, ya_ref, yb_ref, yc_ref, yd_ref, wg_ref, bg_ref, wbr_ref, wout_ref, g1_ref, b1_ref,
                  wrh_ref, wrl_ref, br_ref, x1_ref, route_ref, cnt_ref, xs_ref, *, tm, sub):
    rows_t = xs_ref.shape[0] // sub
    x1, picks = [], []
    for t in range(sub):
        rows = slice(t * tm, (t + 1) * tm)
        x = x_ref[rows, :]
        xb = x.astype(BF16)
        merged = jnp.zeros((tm, D_MODEL), F32)
        for n, y_ref in enumerate((ya_ref, yb_ref, yc_ref, yd_ref)):
            cols = slice(n * D_MODEL, (n + 1) * D_MODEL)
            gate = 0.5 * jnp.tanh(0.5 * (_dot(xb, wg_ref[:, cols]) + bg_ref[:, cols])) + 0.5
            merged = merged + gate * _dot(y_ref[rows, :].astype(BF16), wbr_ref[n])
        mix = _dot(merged.astype(BF16), wout_ref[...])
        x1.append(_layer_norm(ALPHA * x + mix, g1_ref[...], b1_ref[...]))
        x1_ref[rows, :] = x1[t]
        picks.append(_route(x1[t], wrh_ref[...], wrl_ref[...], br_ref[...], tm))
    for t in range(sub):
        route, cnt, xs = _sort(x1[t], picks[t], tm, rows_t)
        route_ref[t * tm:(t + 1) * tm, :] = route
        cnt_ref[t] = cnt
        xs_ref[t * rows_t:(t + 1) * rows_t, :] = xs


def _route(x1, wr_hi, wr_lo, br, tm):
    xh, xl = _split2(x1)
    logits = _dot(xh, wr_hi) + _dot(xh, wr_lo) + _dot(xl, wr_hi) + br
    lane = lax.broadcasted_iota(jnp.int32, (tm, 128), 1)
    lanef = lane.astype(F32)
    neg = jnp.float32(-jnp.inf)
    big = jnp.float32(1e9)
    isg = lane < N_GROUPS
    gmax = jnp.max(jnp.where(isg, logits, neg), axis=-1, keepdims=True)
    gidx = jnp.min(jnp.where(isg & (logits == gmax), lanef, big), axis=-1, keepdims=True)
    p_group = 1.0 / jnp.sum(jnp.where(isg, jnp.exp(logits - gmax), 0.0), axis=-1, keepdims=True)
    lo = N_GROUPS + EPG * gidx
    em = (lanef >= lo) & (lanef < lo + EPG)
    v1 = jnp.max(jnp.where(em, logits, neg), axis=-1, keepdims=True)
    i1 = jnp.min(jnp.where(em & (logits == v1), lanef, big), axis=-1, keepdims=True)
    em2 = em & (lanef != i1)
    v2 = jnp.max(jnp.where(em2, logits, neg), axis=-1, keepdims=True)
    i2 = jnp.min(jnp.where(em2 & (logits == v2), lanef, big), axis=-1, keepdims=True)
    e21 = jnp.exp(v2 - v1)
    gt1 = 1.0 / (1.0 + e21)
    return i1, i2, p_group * gt1, p_group * (e21 * gt1)


def _sort(x1, picks, tm, rows_t):
    i1, i2, gate1, gate2 = picks
    lane = lax.broadcasted_iota(jnp.int32, (tm, 128), 1)
    lanef = lane.astype(F32)
    sel1 = lanef == i1
    sel2 = lanef == i2
    onehot = jnp.where(sel1 | sel2, 1.0, 0.0)
    cnt = jnp.sum(onehot, axis=0, keepdims=True)
    seg = jnp.floor((cnt + (PIECE - 1)) * (1.0 / PIECE)) * PIECE
    er = lax.broadcasted_iota(jnp.int32, (128, 128), 0)
    ec = lax.broadcasted_iota(jnp.int32, (128, 128), 1)
    seg_off = _dot(jnp.broadcast_to(seg, (8, 128)).astype(BF16), jnp.where(er < ec, 1.0, 0.0).astype(BF16))[0:1]
    row = lax.broadcasted_iota(jnp.int32, (tm, tm), 0)
    col = lax.broadcasted_iota(jnp.int32, (tm, tm), 1)
    earlier = _dot(jnp.where(col < row, 1.0, 0.0).astype(BF16), onehot.astype(BF16))
    where_to = seg_off + earlier
    w1 = jnp.where(sel1, where_to, 0.0)
    w2 = jnp.where(sel2, where_to, 0.0)
    pos1 = jnp.sum(w1, axis=-1, keepdims=True)
    pos2 = jnp.sum(w2, axis=-1, keepdims=True)
    ones = jnp.ones((8, 128), BF16)
    prow1 = sum(_dot_nt(ones, p) for p in _split3(w1))[0:1]
    prow2 = sum(_dot_nt(ones, p) for p in _split3(w2))[0:1]
    r = lax.broadcasted_iota(jnp.int32, (rows_t, tm), 0).astype(F32)
    place = jnp.where((r == prow1) | (r == prow2), 1.0, 0.0).astype(BF16)
    xs = _dot(place, x1.astype(BF16)).astype(BF16)

    vals = (i1 - N_GROUPS, i2 - N_GROUPS, gate1, gate2, pos1, pos2)
    route = jnp.zeros((tm, 128), F32)
    for k, v in enumerate(vals):
        route = jnp.where(lane == k, v, route)
    return route, cnt, xs


PIECE = 16


def _tile_rows(tm):
    return -(-(2 * tm + N_EXPERTS * (PIECE - 1)) // 128) * 128


def _merge(x2, ys, lw, tm, sub):
    t = x2.shape[0]
    nt = t // tm
    rows_t = _tile_rows(tm)
    tok = lambda w: pl.BlockSpec((sub * tm, w), lambda i: (i, 0))
    full = lambda shape: pl.BlockSpec(shape, lambda i: (0,) * len(shape), pipeline_mode=pl.Buffered(1))
    return pl.pallas_call(
        functools.partial(_merge_kernel, tm=tm, sub=sub),
        grid=(nt // sub,),
        in_specs=[tok(D_MODEL), tok(BRANCH_W), tok(BRANCH_W), tok(BRANCH_W), tok(BRANCH_W),
                  full((D_MODEL, N_BRANCH * D_MODEL)), full((1, N_BRANCH * D_MODEL)),
                  full((N_BRANCH, BRANCH_W, D_MODEL)), full((D_MODEL, D_MODEL)),
                  full((1, D_MODEL)), full((1, D_MODEL)),
                  full((D_MODEL, 128)), full((D_MODEL, 128)), full((1, 128))],
        out_specs=[tok(D_MODEL), tok(128), pl.BlockSpec((sub, 1, 128), lambda i: (i, 0, 0)),
                   pl.BlockSpec((sub * rows_t, D_MODEL), lambda i: (i, 0))],
        out_shape=[jax.ShapeDtypeStruct((t, D_MODEL), F32), jax.ShapeDtypeStruct((t, 128), F32),
                   jax.ShapeDtypeStruct((nt, 1, 128), F32), jax.ShapeDtypeStruct((nt * rows_t, D_MODEL), BF16)],
        compiler_params=_cparams(("parallel",)),
        name="merge",
    )(x2, *ys, lw["wg"], lw["bg"], lw["wbr"], lw["wout"], lw["ln1_g"], lw["ln1_b"], lw["wr_hi"], lw["wr_lo"], lw["br"])


def _piece_copy(hbm, hbm_row, vmem, piece, sem, to_vmem):
    h = hbm.at[pl.ds(pl.multiple_of(hbm_row, PIECE), PIECE), :]
    v = vmem.at[pl.ds(pl.multiple_of(piece * PIECE, PIECE), PIECE), :]
    return pltpu.make_async_copy(h, v, sem) if to_vmem else pltpu.make_async_copy(v, h, sem)


def _expert_kernel(be_ref, nu_ref, nv_ref, off_ref, xs_hbm, wg_ref, wu_ref, wd_ref, ys_hbm, xin, yout, sem_in, sem_out,
                   *, bp):
    del xs_hbm
    blk = pl.program_id(0)
    n_used = nu_ref[0]
    slot = blk % 2

    def pieces(b, s, buf, sem, to_vmem, start):
        n = nv_ref[b]

        def one(p, priority):
            cp = _piece_copy(ys_hbm, off_ref[b * bp + p], buf.at[s], p, sem.at[s], to_vmem)
            cp.start(priority=priority) if start else cp.wait()

        def body(i, c):
            one(2 * i, 0)

            @pl.when(2 * i + 1 < n)
            def _():
                one(2 * i + 1, 1)
            return c
        lax.fori_loop(0, (n + 1) // 2, body, 0)

    @pl.when(blk == 0)
    def _():
        xin[...] = jnp.zeros_like(xin)

        @pl.when(n_used > 0)
        def _():
            pieces(0, 0, xin, sem_in, True, True)

    @pl.when(blk + 1 < n_used)
    def _():
        pieces(blk + 1, 1 - slot, xin, sem_in, True, True)

    @pl.when(blk < n_used)
    def _():
        pieces(blk, slot, xin, sem_in, True, False)

        @pl.when(blk >= 2)
        def _():
            pieces(blk - 2, slot, yout, sem_out, False, False)

        xb = xin[slot]
        hid = jax.nn.silu(_dot(xb, wg_ref[...].astype(BF16))) * _dot(xb, wu_ref[...].astype(BF16))
        yout[slot] = _dot(hid.astype(BF16), wd_ref[...].astype(BF16)).astype(BF16)
        pieces(blk, slot, yout, sem_out, False, True)

    @pl.when(blk == n_used - 1)
    def _():
        @pl.when(blk >= 1)
        def _():
            pieces(blk - 1, 1 - slot, yout, sem_out, False, False)
        pieces(blk, slot, yout, sem_out, False, False)


def _experts(xs, block_expert, n_used, n_valid, piece_off, lw, bp):
    n_blocks = block_expert.shape[0]
    rows = bp * PIECE
    li = lw["layer"]
    wspec = lambda shape: pl.BlockSpec((None, None) + shape, lambda b, be, nu, nv, off: (li, be[b], 0, 0))
    grid_spec = pltpu.PrefetchScalarGridSpec(
        num_scalar_prefetch=4,
        grid=(n_blocks,),
        in_specs=[pl.BlockSpec(memory_space=pl.ANY), wspec((D_MODEL, D_EXPERT)), wspec((D_MODEL, D_EXPERT)),
                  wspec((D_EXPERT, D_MODEL))],
        out_specs=pl.BlockSpec(memory_space=pl.ANY),
        scratch_shapes=[pltpu.VMEM((2, rows, D_MODEL), BF16), pltpu.VMEM((2, rows, D_MODEL), BF16),
                        pltpu.SemaphoreType.DMA((2,)), pltpu.SemaphoreType.DMA((2,))],
    )
    return pl.pallas_call(
        functools.partial(_expert_kernel, bp=bp),
        grid_spec=grid_spec,
        out_shape=jax.ShapeDtypeStruct(xs.shape, BF16),
        input_output_aliases={4: 0},
        compiler_params=_cparams(("arbitrary",)),
        name="experts",
    )(block_expert, n_used, n_valid, piece_off, xs, lw["we_gate"], lw["we_up"], lw["we_down"])


def _combine_kernel(ys_ref, x1_ref, route_ref, g2_ref, b2_ref, out_ref, *, tm):
    route = route_ref[...]
    ys = ys_ref[...]
    r = lax.broadcasted_iota(jnp.int32, (tm, ys.shape[0]), 1).astype(F32)
    ffn = jnp.zeros((tm, D_MODEL), F32)
    for j in range(2):
        pick = jnp.where(r == route[:, 4 + j:5 + j], 1.0, 0.0).astype(BF16)
        ffn = ffn + route[:, 2 + j:3 + j] * _dot(pick, ys)
    out_ref[...] = _layer_norm(ALPHA * x1_ref[...] + ffn, g2_ref[...], b2_ref[...])


def _combine(ys, x1, route, lw, tm):
    t = x1.shape[0]
    rows_t = _tile_rows(tm)
    full = lambda shape: pl.BlockSpec(shape, lambda i: (0,) * len(shape))
    return pl.pallas_call(
        functools.partial(_combine_kernel, tm=tm),
        grid=(t // tm,),
        in_specs=[pl.BlockSpec((rows_t, D_MODEL), lambda i: (i, 0)), pl.BlockSpec((tm, D_MODEL), lambda i: (i, 0)),
                  pl.BlockSpec((tm, 128), lambda i: (i, 0)), full((1, D_MODEL)), full((1, D_MODEL))],
        out_specs=pl.BlockSpec((tm, D_MODEL), lambda i: (i, 0)),
        out_shape=jax.ShapeDtypeStruct((t, D_MODEL), F32),
        compiler_params=_cparams(("parallel",)),
        name="combine",
    )(ys, x1, route, lw["ln2_g"], lw["ln2_b"])


def _piece_tables(cnt, tm, bp):
    nt = cnt.shape[0]
    rows_t = _tile_rows(tm)
    counts = cnt[:, 0, N_GROUPS:N_GROUPS + N_EXPERTS].astype(jnp.int32)
    pcs = ((counts + PIECE - 1) // PIECE).T
    seg_row = ((jnp.cumsum(pcs, axis=0) - pcs) * PIECE + jnp.arange(nt, dtype=jnp.int32)[None, :] * rows_t)
    tile_end = jnp.cumsum(pcs, axis=1)
    total = tile_end[:, -1]
    e_end = jnp.cumsum((total + bp - 1) // bp * bp)
    e_start = jnp.concatenate([jnp.zeros((1,), jnp.int32), e_end[:-1]])
    n_blocks = -(-(nt * rows_t // PIECE + N_EXPERTS * (bp - 1)) // bp)
    g = jnp.arange(n_blocks * bp, dtype=jnp.int32)[:, None]
    in_e = ((e_start[None, :] <= g) & (g < e_end[None, :])).astype(F32)
    pick = lambda table: jnp.dot(in_e, table.astype(F32), precision=lax.Precision.HIGHEST)
    local = g.astype(F32) - pick(e_start[:, None])
    ends, starts, rows = pick(tile_end), pick(tile_end - pcs), pick(seg_row)
    in_t = ((starts <= local) & (local < ends)).astype(F32)
    off = jnp.sum(in_t * (rows + (local - starts) * PIECE), axis=1).astype(jnp.int32)
    n_valid = jnp.sum(jnp.sum(in_t, axis=1).reshape(n_blocks, bp), axis=1).astype(jnp.int32)
    expert = pick(jnp.arange(N_EXPERTS, dtype=jnp.int32)[:, None])[:, 0]
    block_expert = expert.reshape(n_blocks, bp)[:, 0].astype(jnp.int32)
    n_used = (e_end[-1] // bp).astype(jnp.int32).reshape(1)
    return block_expert, n_used, n_valid, off


def _moe(x1, route, cnt, xs, lw, bp, tm):
    ys = _experts(xs, *_piece_tables(cnt, tm, bp), lw, bp)
    return _combine(ys, x1, route, lw, tm)


CHUNK = 128
STEP_CHUNKS = 4


def _dot_nt(a, b):
    return lax.dot_general(a, b, (((1,), (1,)), ((), ())), preferred_element_type=F32)


def _dot_tn(a, b):
    return lax.dot_general(a, b, (((0,), (0,)), ((), ())), preferred_element_type=F32)


def _lane_head(width, group):
    return lax.broadcasted_iota(jnp.int32, (1, width), 1) // group


def _block_diag_mask(rows_head, cols_head):
    return rows_head == cols_head


def _head_norm(o, gain):
    r = lax.broadcasted_iota(jnp.int32, (BRANCH_W, BRANCH_W), 0) // HD
    c = lax.broadcasted_iota(jnp.int32, (BRANCH_W, BRANCH_W), 1) // HD
    avg = jnp.where(r == c, 1.0 / HD, 0.0).astype(BF16)
    mu = _dot_exact_rhs(o, avg)
    xc = o - mu
    var = _dot_exact_rhs(xc * xc, avg)
    return xc * lax.rsqrt(var + LN_EPS) * gain


def _stack_heads(x, head_of_lane):
    return jnp.concatenate([jnp.where(head_of_lane == h, x, 0.0) for h in range(N_HEADS)], axis=0)


def _pick_heads(stacked, rows, head_of_lane):
    out = jnp.where(head_of_lane == 0, stacked[0:rows], 0.0)
    for h in range(1, N_HEADS):
        out = jnp.where(head_of_lane == h, stacked[h * rows:(h + 1) * rows], out)
    return out


def _spread_heads(cols, head_of_lane):
    out = jnp.where(head_of_lane == 0, cols[0], 0.0)
    for h in range(1, N_HEADS):
        out = jnp.where(head_of_lane == h, cols[h], out)
    return out


SWA_BLOCKS = 4


def _swa_prompt_kernel(q_ref, kv_ref, kvp_ref, sink_ref, y_ref):
    n = pl.program_id(1)
    lo = lax.broadcasted_iota(jnp.int32, (1, 128), 1) < HD
    e = lax.broadcasted_iota(jnp.int32, (2 * WINDOW, N_HEADS * WINDOW), 0)
    col = lax.broadcasted_iota(jnp.int32, (2 * WINDOW, N_HEADS * WINDOW), 1)
    a = col % WINDOW
    band = (e > a) & (e <= a + WINDOW)
    head = lax.broadcasted_iota(jnp.int32, (1, N_HEADS * WINDOW), 1) // WINDOW
    sink = _spread_heads([sink_ref[:, h:h + 1] for h in range(N_HEADS)], head)
    blocks = range(SWA_BLOCKS)
    scores, vexts = [], []
    for i in blocks:
        rows = slice(i * WINDOW, (i + 1) * WINDOW)
        q = q_ref[rows, :]
        qa, qb = q[:, 0:128], q[:, 128:256]
        kv = kv_ref[rows, :]
        kvp = kvp_ref[...] if i == 0 else kv_ref[(i - 1) * WINDOW:i * WINDOW, :]
        kext = jnp.concatenate([kvp[:, 0:128], kv[:, 0:128]], axis=0).astype(BF16)
        vexts.append(jnp.concatenate([kvp[:, 128:256], kv[:, 128:256]], axis=0).astype(BF16))
        qs = jnp.concatenate([jnp.where(lo, qa, 0.0), jnp.where(lo, qb, 0.0),
                              jnp.where(lo, 0.0, qa), jnp.where(lo, 0.0, qb)], axis=0).astype(BF16)
        scores.append(_dot_nt(kext, qs))
    probs, inv_den = [], []
    for i in blocks:
        valid = band & ((n > 0) | (e >= WINDOW)) if i == 0 else band
        s = jnp.where(valid, scores[i] * (HD ** -0.5), -jnp.inf)
        mx = jnp.maximum(jnp.max(s, axis=0, keepdims=True), sink)
        p = jnp.exp(s - mx)
        inv_den.append(1.0 / (jnp.sum(p, axis=0, keepdims=True) + jnp.exp(sink - mx)))
        probs.append(p.astype(BF16))
    outs = [_dot_tn(vexts[i], probs[i]) for i in blocks]
    for i in blocks:
        o = outs[i] * inv_den[i]
        ya = jnp.concatenate([o[0:HD, 0:128], o[HD:128, 256:384]], axis=0).T
        yb = jnp.concatenate([o[0:HD, 128:256], o[HD:128, 384:512]], axis=0).T
        y_ref[i * WINDOW:(i + 1) * WINDOW, :] = jnp.concatenate([ya, yb], axis=1)


def _swa_prompt(swa, lw, bsz, seq):
    qt = SWA_BLOCKS * WINDOW
    nt = seq // qt
    return pl.pallas_call(
        _swa_prompt_kernel,
        grid=(bsz, nt),
        in_specs=[pl.BlockSpec((qt, 256), lambda b, n: (b * nt + n, 0)),
                  pl.BlockSpec((qt, 256), lambda b, n: (b * nt + n, 1)),
                  pl.BlockSpec((WINDOW, 256), lambda b, n: (jnp.maximum((b * nt + n) * SWA_BLOCKS - 1, 0), 1)),
                  pl.BlockSpec((1, 128), lambda b, n: (0, 0))],
        out_specs=pl.BlockSpec((qt, BRANCH_W), lambda b, n: (b * nt + n, 0)),
        out_shape=jax.ShapeDtypeStruct((bsz * seq, BRANCH_W), F32),
        compiler_params=_cparams(("parallel", "parallel")),
        name="swa_prompt",
    )(swa, swa, swa, lw["sinks"])


def _rotate(x, cos, sin):
    x1, x2 = x[:, 0:128], x[:, 128:256]
    return jnp.concatenate([x1 * cos - x2 * sin, x1 * sin + x2 * cos], axis=1)


def _ret_prompt_kernel(r_ref, cos_ref, sin_ref, dmat_ref, qdec_ref, kdec_ref, cdec_ref, g_ref, y_ref, s_out_ref, s_ref):
    n = pl.program_id(1)

    @pl.when(n == 0)
    def _():
        s_ref[...] = jnp.zeros_like(s_ref)

    c = CHUNK
    chunks = range(STEP_CHUNKS)
    hq = _lane_head(BRANCH_W, 32) % N_HEADS
    hv = _lane_head(BRANCH_W, HD)
    rows_head = (lax.broadcasted_iota(jnp.int32, (BRANCH_W, 1), 0) // 32) % N_HEADS
    qr, kr, v = [], [], []
    for i in chunks:
        rows = slice(i * c, (i + 1) * c)
        cos, sin = cos_ref[rows, :], sin_ref[rows, :]
        qr.append(_rotate(r_ref[rows, 0:256], cos, sin))
        kr.append(_rotate(r_ref[rows, 256:512], cos, sin) * (HD ** -0.5))
        v.append(r_ref[rows, 512:768].astype(BF16))
    att = [_dot_nt(_stack_heads(qr[i], hq).astype(BF16), kr[i].astype(BF16)) * dmat_ref[...] for i in chunks]
    upd = [_dot_tn((kr[i] * kdec_ref[...]).astype(BF16), v[i]) for i in chunks]
    states = [s_ref[...]]
    for i in chunks:
        states.append(cdec_ref[...] * states[i] + jnp.where(rows_head == hv, upd[i], 0.0))
    o = [_pick_heads(_dot(att[i].astype(BF16), v[i]), c, hv) for i in chunks]
    o = [o[i] + _dot((qr[i] * qdec_ref[...]).astype(BF16), states[i].astype(BF16)) for i in chunks]
    s_ref[...] = states[-1]
    s_out_ref[...] = states[-1]
    y_ref[...] = jax.nn.silu(r_ref[:, 768:1024]) * _head_norm(jnp.concatenate(o, axis=0), g_ref[...])


def _ret_tables(seq):
    c = CHUNK
    half = HD // 2
    freq = ROPE_BASE ** (-jnp.arange(half, dtype=F32) / half)
    ang = jnp.arange(seq).astype(F32)[:, None] * freq[None, :]
    cos = jnp.tile(jnp.cos(ang), (1, N_HEADS))
    sin = jnp.tile(jnp.sin(ang), (1, N_HEADS))
    log_g = jnp.log1p(-jnp.exp2(-5.0 - jnp.arange(N_HEADS, dtype=F32)))
    idx = jnp.arange(c, dtype=F32)
    rel = idx[:, None] - idx[None, :]
    dmat = jnp.exp(jnp.where(rel[None] >= 0, rel[None] * log_g[:, None, None], -jnp.inf)).reshape(N_HEADS * c, c)
    lane_head_q = (jnp.arange(BRANCH_W) // 32) % N_HEADS
    qdec = jnp.exp((idx + 1.0)[:, None] * log_g[lane_head_q][None, :])
    kdec = jnp.exp((c - 1.0 - idx)[:, None] * log_g[lane_head_q][None, :])
    cdec = jnp.exp(c * log_g)[jnp.arange(BRANCH_W) // HD][None, :]
    return cos, sin, dmat, qdec, kdec, cdec


def _ret_prompt(ret, lw, bsz, seq):
    c = CHUNK
    rows = STEP_CHUNKS * c
    nc = seq // rows
    cos, sin, dmat, qdec, kdec, cdec = _ret_tables(seq)
    full = lambda shape: pl.BlockSpec(shape, lambda b, n: (0,) * len(shape))
    return pl.pallas_call(
        _ret_prompt_kernel,
        grid=(bsz, nc),
        in_specs=[pl.BlockSpec((rows, W_RET), lambda b, n: (b * nc + n, 0)),
                  pl.BlockSpec((rows, 128), lambda b, n: (n, 0)), pl.BlockSpec((rows, 128), lambda b, n: (n, 0)),
                  full((N_HEADS * c, c)), full((c, BRANCH_W)), full((c, BRANCH_W)), full((1, BRANCH_W)),
                  full((1, BRANCH_W))],
        out_specs=[pl.BlockSpec((rows, BRANCH_W), lambda b, n: (b * nc + n, 0)),
                   pl.BlockSpec((None, BRANCH_W, BRANCH_W), lambda b, n: (b, 0, 0))],
        out_shape=[jax.ShapeDtypeStruct((bsz * seq, BRANCH_W), F32),
                   jax.ShapeDtypeStruct((bsz, BRANCH_W, BRANCH_W), F32)],
        scratch_shapes=[pltpu.VMEM((BRANCH_W, BRANCH_W), F32)],
        compiler_params=_cparams(("parallel", "arbitrary")),
        name="ret_prompt",
    )(ret, cos, sin, dmat, qdec, kdec, cdec, lw["ret_g"])


def _ret_state_from_blockdiag(s_bd):
    b = s_bd.shape[0]
    s = s_bd.reshape(b, 2, N_HEADS, 32, N_HEADS, HD)
    s = jnp.stack([s[:, :, h, :, h, :] for h in range(N_HEADS)], axis=1)
    return s.reshape(b, N_HEADS, HD, HD)


ML_I_LANE = GLA_RANK
ML_F_LANE = GLA_RANK + N_HEADS


def _ml_prompt_kernel(m_ref, sm_ref, g_ref, y_ref, c_out_ref, n_out_ref, m_out_ref, c_ref, n_ref, mm_ref):
    nchunk = pl.program_id(1)

    @pl.when(nchunk == 0)
    def _():
        c_ref[...] = jnp.zeros_like(c_ref)
        n_ref[...] = jnp.zeros_like(n_ref)
        mm_ref[...] = jnp.zeros_like(mm_ref)

    c = CHUNK
    chunks = range(STEP_CHUNKS)
    row = lax.broadcasted_iota(jnp.int32, (c, c), 0)
    col = lax.broadcasted_iota(jnp.int32, (c, c), 1)
    tri_lo = jnp.where(col <= row, 1.0, 0.0).astype(BF16)
    tri_up = jnp.where(row <= col, 1.0, 0.0).astype(BF16)
    hv = _lane_head(BRANCH_W, HD)
    hq = _lane_head(N_HEADS * c, c)
    key = lax.broadcasted_iota(jnp.int32, (c, N_HEADS * c), 0)
    qry = lax.broadcasted_iota(jnp.int32, (c, N_HEADS * c), 1) % c
    causal_t = key <= qry
    r = lax.broadcasted_iota(jnp.int32, (BRANCH_W, BRANCH_W), 0) // HD
    cc = lax.broadcasted_iota(jnp.int32, (BRANCH_W, BRANCH_W), 1) // HD
    same_head = r == cc
    lane = lax.broadcasted_iota(jnp.int32, (1, 128), 1)
    gate_lane = lax.broadcasted_iota(jnp.int32, (128, 1), 0)
    is_i = (gate_lane >= ML_I_LANE) & (gate_lane < ML_I_LANE + N_HEADS)
    is_f = (gate_lane >= ML_F_LANE) & (gate_lane < ML_F_LANE + N_HEADS)
    gate_head = jnp.where(is_i, gate_lane - ML_I_LANE, gate_lane - ML_F_LANE)
    sel_q = jnp.where((is_i | is_f) & (gate_head == hq), 1.0, 0.0).astype(BF16)
    sel_v = jnp.where((is_i | is_f) & (gate_head == hv), 1.0, 0.0).astype(BF16)
    lane128 = lax.broadcasted_iota(jnp.int32, (1, 128), 1)
    in_i = (lane128 >= ML_I_LANE) & (lane128 < ML_I_LANE + N_HEADS)
    in_f = (lane128 >= ML_F_LANE) & (lane128 < ML_F_LANE + N_HEADS)
    head_rows = lax.broadcasted_iota(jnp.int32, (8, 1), 0)

    q, ks, vt, brow, u_q, u_v, sc = [], [], [], [], [], [], []
    for i in chunks:
        rows = slice(i * c, (i + 1) * c)
        q.append(m_ref[rows, 0:256])
        ks.append(m_ref[rows, 256:512] * (HD ** -0.5))
        vt.append(m_ref[rows, 512:768].T.astype(BF16))
        sm = sm_ref[rows, :]
        bcols = _dot_exact_lhs(tri_lo, jax.nn.log_sigmoid(sm))
        brows = _dot_exact_rhs(jax.nn.log_sigmoid(sm.T), tri_up)
        brow.append(jnp.concatenate([brows[ML_F_LANE + h:ML_F_LANE + h + 1, :] for h in range(N_HEADS)], axis=1))
        z = jnp.where(in_i, sm, 0.0) - jnp.where(in_f, bcols, 0.0)
        u_q.append(_dot_exact_rhs(z, sel_q))
        u_v.append(_dot_exact_rhs(z, sel_v))
        sc.append(_dot_nt(ks[i].astype(BF16), _stack_heads(q[i], hv).astype(BF16)))

    mm = mm_ref[...]
    qk, w_inter, rowsum, floor, khat, carry_b = [], [], [], [], [], []
    for i in chunks:
        m_prev = _spread_heads([mm[:, h:h + 1] for h in range(N_HEADS)], hq)
        d_log = jnp.where(causal_t, brow[i] + u_q[i], -jnp.inf)
        inter_log = brow[i] + m_prev
        m_t = jnp.maximum(inter_log, jnp.max(d_log, axis=0, keepdims=True))
        qk_i = sc[i] * jnp.exp(d_log - m_t)
        qk.append(qk_i.astype(BF16))
        rowsum.append(jnp.sum(qk_i, axis=0, keepdims=True))
        w_inter.append(jnp.exp(inter_log - m_t))
        floor.append(jnp.exp(-m_t))
        m_new = [m_t[:, (h + 1) * c - 1:(h + 1) * c] for h in range(N_HEADS)]
        b_last = [brow[i][:, (h + 1) * c - 1:(h + 1) * c] for h in range(N_HEADS)]
        carry_b.append(_spread_heads([jnp.exp(b_last[h] + mm[:, h:h + 1] - m_new[h]) for h in range(N_HEADS)], hv))
        khat.append(ks[i] * jnp.exp(u_v[i] + _spread_heads([b_last[h] - m_new[h] for h in range(N_HEADS)], hv)))
        mm = jnp.zeros((1, 128), F32)
        for h in range(N_HEADS):
            mm = jnp.where(lane == h, m_new[h], mm)

    intra = [_dot(vt[i], qk[i]) for i in chunks]
    upd = [_dot(vt[i], khat[i].astype(BF16)) for i in chunks]
    cms, nrows = [c_ref[...]], [n_ref[...]]
    for i in chunks:
        cms.append(carry_b[i] * cms[i] + jnp.where(same_head, upd[i], 0.0))
        nrows.append(carry_b[i] * nrows[i] + jnp.sum(khat[i], axis=0, keepdims=True))
    hout = []
    for i in chunks:
        qb = q[i].astype(BF16)
        inter = _dot_nt(cms[i].astype(BF16), qb)
        n_sel = jnp.where(head_rows == hv, nrows[i], 0.0)
        qn = sum(_dot_nt(p, qb) for p in _split3(n_sel))
        blocks = []
        for h in range(N_HEADS):
            cols = slice(h * c, (h + 1) * c)
            den = w_inter[i][:, cols] * qn[h:h + 1, :] + rowsum[i][:, cols]
            num = w_inter[i][:, cols] * inter[h * HD:(h + 1) * HD, :] + intra[i][h * HD:(h + 1) * HD, cols]
            blocks.append(num / jnp.maximum(jnp.abs(den), floor[i][:, cols]))
        hout.append(jnp.concatenate(blocks, axis=0).T)
    y_ref[...] = _head_norm(jax.nn.sigmoid(m_ref[:, 768:1024]) * jnp.concatenate(hout, axis=0), g_ref[...])

    c_ref[...] = cms[-1]
    n_ref[...] = nrows[-1]
    mm_ref[...] = mm
    c_out_ref[...] = cms[-1]
    n_out_ref[...] = nrows[-1]
    m_out_ref[...] = mm


def _ml_prompt(ml, small, lw, bsz, seq):
    rows = STEP_CHUNKS * CHUNK
    nc = seq // rows
    return pl.pallas_call(
        _ml_prompt_kernel,
        grid=(bsz, nc),
        in_specs=[pl.BlockSpec((rows, W_ML), lambda b, n: (b * nc + n, 0)),
                  pl.BlockSpec((rows, W_SMALL), lambda b, n: (b * nc + n, 0)),
                  pl.BlockSpec((1, BRANCH_W), lambda b, n: (0, 0))],
        out_specs=[pl.BlockSpec((rows, BRANCH_W), lambda b, n: (b * nc + n, 0)),
                   pl.BlockSpec((None, BRANCH_W, BRANCH_W), lambda b, n: (b, 0, 0)),
                   pl.BlockSpec((None, 1, BRANCH_W), lambda b, n: (b, 0, 0)),
                   pl.BlockSpec((None, 1, 128), lambda b, n: (b, 0, 0))],
        out_shape=[jax.ShapeDtypeStruct((bsz * seq, BRANCH_W), F32),
                   jax.ShapeDtypeStruct((bsz, BRANCH_W, BRANCH_W), F32),
                   jax.ShapeDtypeStruct((bsz, 1, BRANCH_W), F32),
                   jax.ShapeDtypeStruct((bsz, 1, 128), F32)],
        scratch_shapes=[pltpu.VMEM((BRANCH_W, BRANCH_W), F32), pltpu.VMEM((1, BRANCH_W), F32),
                        pltpu.VMEM((1, 128), F32)],
        compiler_params=_cparams(("parallel", "arbitrary")),
        name="ml_prompt",
    )(ml, small, lw["ml_g"])


def _diag_blocks(s_bd):
    b = s_bd.shape[0]
    s = s_bd.reshape(b, N_HEADS, HD, N_HEADS, HD)
    return jnp.stack([s[:, h, :, h, :] for h in range(N_HEADS)], axis=1)


GLA_TILE = 256
GLA_SUB = 16


def _gla_prompt_kernel(x_ref, sm_ref, wa2_ref, ba2_ref, g_ref, y_ref, s_out_ref, s_ref):
    n = pl.program_id(1)

    @pl.when(n == 0)
    def _():
        s_ref[...] = jnp.zeros_like(s_ref)

    tq, sc = GLA_TILE, GLA_SUB
    x = x_ref[...]
    q = x[:, 0:128] * (GLA_DK ** -0.5)
    k = x[:, 128:256]
    v = x[:, 256:512]
    log_a = jax.nn.log_sigmoid(_dot(sm_ref[...].astype(BF16), wa2_ref[...]) + ba2_ref[...]) / GLA_TAU
    r = lax.broadcasted_iota(jnp.int32, (tq, tq), 0)
    c = lax.broadcasted_iota(jnp.int32, (tq, tq), 1)
    same = (r // sc) == (c // sc)
    b = _dot_exact_lhs(jnp.where(same & (c <= r), 1.0, 0.0).astype(BF16), log_a)
    btot = _dot_exact_lhs(jnp.where(same, 1.0, 0.0).astype(BF16), log_a)
    qd = (q * jnp.exp(b)).astype(BF16)
    khat_t = (k * jnp.exp(btot - b)).T
    dec_t = jnp.exp(btot).T
    vb = v.astype(BF16)
    kh = lax.broadcasted_iota(jnp.int32, (128, 1), 0) // GLA_DK
    hv = _lane_head(BRANCH_W, HD)
    expand = jnp.where(kh == hv, 1.0, 0.0).astype(BF16)
    lane_chunk = lax.broadcasted_iota(jnp.int32, (1, tq), 1) // sc
    trow = lax.broadcasted_iota(jnp.int32, (sc, 1), 0)
    chunks = range(tq // sc)
    upds = [_dot(jnp.where(lane_chunk == ci, khat_t, 0.0).astype(BF16), vb) for ci in chunks]
    states = [s_ref[...]]
    for ci in chunks:
        states.append(dec_t[:, ci * sc:ci * sc + 1] * states[ci] + jnp.where(kh == hv, upds[ci], 0.0))
    outs = []
    for ci in chunks:
        lo = ci * sc
        qc, kc, bc, vc = q[lo:lo + sc], k[lo:lo + sc], b[lo:lo + sc], v[lo:lo + sc]
        pair = [qc * kc[j:j + 1] * jnp.exp(jnp.where(trow >= j, bc - bc[j:j + 1], -jnp.inf)) for j in range(sc)]
        att = _dot(jnp.concatenate(pair, axis=0).astype(BF16), expand)
        o = att[0:sc] * vc[0:1]
        for j in range(1, sc):
            o = o + att[j * sc:(j + 1) * sc] * vc[j:j + 1]
        outs.append(o)
    outs = [outs[ci] + _dot(qd[ci * sc:(ci + 1) * sc], states[ci].astype(BF16)) for ci in chunks]
    s_ref[...] = states[-1]
    s_out_ref[...] = states[-1]
    y_ref[...] = jax.nn.silu(x[:, 512:768]) * _head_norm(jnp.concatenate(outs, axis=0), g_ref[...])


def _gla_prompt(gla, small, lw, bsz, seq):
    tq = GLA_TILE
    nt = seq // tq
    full = lambda shape: pl.BlockSpec(shape, lambda b, n: (0,) * len(shape))
    return pl.pallas_call(
        _gla_prompt_kernel,
        grid=(bsz, nt),
        in_specs=[pl.BlockSpec((tq, W_GLA), lambda b, n: (b * nt + n, 0)),
                  pl.BlockSpec((tq, W_SMALL), lambda b, n: (b * nt + n, 0)),
                  full((128, 128)), full((1, 128)), full((1, BRANCH_W))],
        out_specs=[pl.BlockSpec((tq, BRANCH_W), lambda b, n: (b * nt + n, 0)),
                   pl.BlockSpec((None, 128, BRANCH_W), lambda b, n: (b, 0, 0))],
        out_shape=[jax.ShapeDtypeStruct((bsz * seq, BRANCH_W), F32),
                   jax.ShapeDtypeStruct((bsz, 128, BRANCH_W), F32)],
        scratch_shapes=[pltpu.VMEM((128, BRANCH_W), F32)],
        compiler_params=_cparams(("parallel", "arbitrary")),
        name="gla_prompt",
    )(gla, small, lw["wa2"], lw["ba2"], lw["gla_g"])


def _gla_state_from_blockdiag(s_bd):
    b = s_bd.shape[0]
    s = s_bd.reshape(b, N_HEADS, GLA_DK, N_HEADS, HD)
    return jnp.stack([s[:, h, :, h, :] for h in range(N_HEADS)], axis=1)


LANES = 128


def _rank1_update(st_ref, snt_ref, q_ref, k_ref, decay_row, v, dk):
    def body(i, o):
        rows = pl.ds(pl.multiple_of(i * HD, HD), HD)
        new = decay_row(i) * st_ref[rows, :] + k_ref[pl.ds(i, 1), :] * v
        snt_ref[rows, :] = new
        return o + q_ref[pl.ds(i, 1), :] * new
    return lax.fori_loop(0, dk, body, jnp.zeros((HD, LANES), F32), unroll=4)


def _head_rows(h, n):
    return pl.ds(pl.multiple_of(h * n, n), n)


def _gla_sample_kernel(x_ref, sm_ref, wa2_ref, ba2_ref, g_ref, s_ref, y_ref, s_out_ref,
                       qt_ref, kt_ref, at_ref, vt_ref, ot_ref):
    h = pl.program_id(0)

    @pl.when(h == 0)
    def _():
        x = x_ref[...]
        qt_ref[...] = (x[:, 0:128] * (GLA_DK ** -0.5)).T
        kt_ref[...] = x[:, 128:256].T
        vt_ref[...] = x[:, 256:512].T
        log_a = jax.nn.log_sigmoid(_dot(sm_ref[...].astype(BF16), wa2_ref[...]) + ba2_ref[...]) / GLA_TAU
        at_ref[...] = jnp.exp(log_a).T

    base = h * GLA_DK
    o = _rank1_update(s_ref, s_out_ref, qt_ref.at[_head_rows(h, GLA_DK)], kt_ref.at[_head_rows(h, GLA_DK)],
                      lambda i: at_ref[pl.ds(base + i, 1), :], vt_ref[_head_rows(h, HD), :], GLA_DK)
    ot_ref[_head_rows(h, HD), :] = o

    @pl.when(h == N_HEADS - 1)
    def _():
        y_ref[...] = jax.nn.silu(x_ref[:, 512:768]) * _head_norm(ot_ref[...].T, g_ref[...])


def _gla_sample(gla, small, state, lw):
    per_head = GLA_DK * HD
    full = lambda shape: pl.BlockSpec(shape, lambda h: (0,) * len(shape))
    vm = lambda r: pltpu.VMEM((r, LANES), F32)
    return pl.pallas_call(
        _gla_sample_kernel,
        grid=(N_HEADS,),
        in_specs=[full((LANES, W_GLA)), full((LANES, W_SMALL)), full((128, 128)), full((1, 128)), full((1, BRANCH_W)),
                  pl.BlockSpec((per_head, LANES), lambda h: (h, 0))],
        out_specs=[full((LANES, BRANCH_W)), pl.BlockSpec((per_head, LANES), lambda h: (h, 0))],
        out_shape=[jax.ShapeDtypeStruct((LANES, BRANCH_W), F32), jax.ShapeDtypeStruct(state.shape, F32)],
        scratch_shapes=[vm(128), vm(128), vm(128), vm(256), vm(256)],
        compiler_params=_cparams(("arbitrary",)),
        name="gla_sample",
    )(gla, small, lw["wa2"], lw["ba2"], lw["gla_g"], state)


def _ml_sample_kernel(x_ref, sc_ref, n_ref, g_ref, c_ref, y_ref, c_out_ref, n_out_ref, m_out_ref,
                      qt_ref, kt_ref, vt_ref, ot_ref, w_ref, kp_ref):
    h = pl.program_id(0)

    @pl.when(h == 0)
    def _():
        x = x_ref[...]
        qt_ref[...] = x[:, 0:256].T
        kt_ref[...] = (x[:, 256:512] * (HD ** -0.5)).T
        vt_ref[...] = x[:, 512:768].T
        sct = sc_ref[...].T
        log_f = jax.nn.log_sigmoid(sct[8:16])
        m_old = sct[16:24]
        m_new = jnp.maximum(log_f + m_old, sct[0:8])
        w_ref[0:8, :] = jnp.exp(log_f + m_old - m_new)
        w_ref[8:16, :] = jnp.exp(sct[0:8] - m_new)
        w_ref[16:24, :] = m_new

    w_carry = w_ref[pl.ds(h, 1), :]
    w_tok = w_ref[pl.ds(8 + h, 1), :]
    m_new = w_ref[pl.ds(16 + h, 1), :]
    kp_ref[...] = kt_ref[_head_rows(h, HD), :] * w_tok
    q_h = qt_ref.at[_head_rows(h, HD)]
    n_new = w_carry * n_ref[_head_rows(h, HD), :] + kp_ref[...]
    num = _rank1_update(c_ref, c_out_ref, q_h, kp_ref, lambda i: w_carry, vt_ref[_head_rows(h, HD), :], HD)
    den = jnp.sum(q_h[...] * n_new, axis=0, keepdims=True)
    ot_ref[_head_rows(h, HD), :] = num / jnp.maximum(jnp.abs(den), jnp.exp(-m_new))
    n_out_ref[_head_rows(h, HD), :] = n_new

    @pl.when(h == N_HEADS - 1)
    def _():
        y_ref[...] = _head_norm(jax.nn.sigmoid(x_ref[:, 768:1024]) * ot_ref[...].T, g_ref[...])
        m_out_ref[...] = jnp.concatenate([w_ref[16:24, :], jnp.zeros((LANES - 8, LANES), F32)], axis=0).T


def _ml_sample(ml, scal, n_state, c_state, lw):
    per_head = HD * HD
    full = lambda shape: pl.BlockSpec(shape, lambda h: (0,) * len(shape))
    vm = lambda r: pltpu.VMEM((r, LANES), F32)
    return pl.pallas_call(
        _ml_sample_kernel,
        grid=(N_HEADS,),
        in_specs=[full((LANES, W_ML)), full((LANES, 128)), full((BRANCH_W, LANES)), full((1, BRANCH_W)),
                  pl.BlockSpec((per_head, LANES), lambda h: (h, 0))],
        out_specs=[full((LANES, BRANCH_W)), pl.BlockSpec((per_head, LANES), lambda h: (h, 0)),
                   full((BRANCH_W, LANES)), full((LANES, 128))],
        out_shape=[jax.ShapeDtypeStruct((LANES, BRANCH_W), F32), jax.ShapeDtypeStruct(c_state.shape, F32),
                   jax.ShapeDtypeStruct((BRANCH_W, LANES), F32), jax.ShapeDtypeStruct((LANES, 128), F32)],
        scratch_shapes=[vm(256), vm(256), vm(256), vm(256), vm(24), vm(HD)],
        compiler_params=_cparams(("arbitrary",)),
        name="ml_sample",
    )(ml, scal, n_state, lw["ml_g"], c_state)


def _ret_sample_kernel(x_ref, cos_ref, sin_ref, gam_ref, g_ref, s_ref, y_ref, s_out_ref,
                       qt_ref, kt_ref, vt_ref, ot_ref, qr_ref, kr_ref):
    h = pl.program_id(0)

    @pl.when(h == 0)
    def _():
        x = x_ref[...]
        qt_ref[...] = x[:, 0:256].T
        kt_ref[...] = (x[:, 256:512] * (HD ** -0.5)).T
        vt_ref[...] = x[:, 512:768].T

    cos, sin = cos_ref[...], sin_ref[...]
    for src, dst in ((qt_ref, qr_ref), (kt_ref, kr_ref)):
        x1 = src[_head_rows(h, 32), :]
        x2 = src[pl.ds(pl.multiple_of(128 + h * 32, 32), 32), :]
        dst[0:32, :] = x1 * cos - x2 * sin
        dst[32:64, :] = x1 * sin + x2 * cos
    gamma = gam_ref[pl.ds(h, 1), :]
    ot_ref[_head_rows(h, HD), :] = _rank1_update(s_ref, s_out_ref, qr_ref, kr_ref, lambda i: gamma,
                                                 vt_ref[_head_rows(h, HD), :], HD)

    @pl.when(h == N_HEADS - 1)
    def _():
        y_ref[...] = jax.nn.silu(x_ref[:, 768:1024]) * _head_norm(ot_ref[...].T, g_ref[...])


def _ret_sample(ret, state, lw):
    per_head = HD * HD
    half = HD // 2
    freq = ROPE_BASE ** (-jnp.arange(half, dtype=F32) / half)
    ang = jnp.full((1,), PAST_LEN, F32)[:, None] * freq[None, :]
    cos = jnp.broadcast_to(jnp.cos(ang).reshape(half, 1), (half, LANES))
    sin = jnp.broadcast_to(jnp.sin(ang).reshape(half, 1), (half, LANES))
    log_g = jnp.log1p(-jnp.exp2(-5.0 - jnp.arange(N_HEADS, dtype=F32)))
    gam = jnp.broadcast_to(jnp.concatenate([jnp.exp(log_g), jnp.ones((4,), F32)]).reshape(8, 1), (8, LANES))
    full = lambda shape: pl.BlockSpec(shape, lambda h: (0,) * len(shape))
    vm = lambda r: pltpu.VMEM((r, LANES), F32)
    return pl.pallas_call(
        _ret_sample_kernel,
        grid=(N_HEADS,),
        in_specs=[full((LANES, W_RET)), full((half, LANES)), full((half, LANES)), full((8, LANES)), full((1, BRANCH_W)),
                  pl.BlockSpec((per_head, LANES), lambda h: (h, 0))],
        out_specs=[full((LANES, BRANCH_W)), pl.BlockSpec((per_head, LANES), lambda h: (h, 0))],
        out_shape=[jax.ShapeDtypeStruct((LANES, BRANCH_W), F32), jax.ShapeDtypeStruct(state.shape, F32)],
        scratch_shapes=[vm(256), vm(256), vm(256), vm(256), vm(HD), vm(HD)],
        compiler_params=_cparams(("arbitrary",)),
        name="ret_sample",
    )(ret, cos, sin, gam, lw["ret_g"], state)


def _swa_sample_kernel(q_ref, kn_ref, vn_ref, sink_ref, kc_ref, vc_ref, o_ref, ko_ref, vo_ref):
    q = q_ref[...]
    kc, vc = kc_ref[...], vc_ref[...]
    kn, vn = kn_ref[...], vn_ref[...]
    scale = HD ** -0.5
    s = _einsum("bhd,bwd->bhw", q.astype(BF16), kc.astype(BF16)) * scale
    s_new = jnp.sum(q * kn, axis=-1, keepdims=True) * scale
    w = lax.broadcasted_iota(jnp.int32, (1, 1, WINDOW), 2)
    s = jnp.where(w > 0, s, -jnp.inf)
    sink = sink_ref[...][:, 0:1]
    mx = jnp.maximum(jnp.maximum(jnp.max(s, axis=-1, keepdims=True), s_new), sink)
    p = jnp.exp(s - mx)
    p_new = jnp.exp(s_new - mx)
    den = jnp.sum(p, axis=-1, keepdims=True) + p_new + jnp.exp(sink - mx)
    o = _einsum("bhw,bwd->bhd", p.astype(BF16), vc.astype(BF16)) + p_new * vn
    o_ref[...] = o / den
    ko_ref[:, 0:WINDOW - 1, :] = kc[:, 1:WINDOW, :]
    ko_ref[:, WINDOW - 1:WINDOW, :] = kn
    vo_ref[:, 0:WINDOW - 1, :] = vc[:, 1:WINDOW, :]
    vo_ref[:, WINDOW - 1:WINDOW, :] = vn


def _swa_sample(swa, cache_k, cache_v, lw, bt=16):
    bsz = swa.shape[0]
    lo = (jnp.arange(128) < HD)[None, :]
    qa, qb = swa[:, 0:128], swa[:, 128:256]
    q = jnp.stack([jnp.where(lo, qa, 0.0), jnp.where(lo, qb, 0.0), jnp.where(lo, 0.0, qa), jnp.where(lo, 0.0, qb)], axis=1)
    kn = swa[:, None, 256:384]
    vn = swa[:, None, 384:512]
    sink = jnp.broadcast_to(lw["sinks"][0, 0:N_HEADS].reshape(N_HEADS, 1), (N_HEADS, 128))
    blk = lambda shape: pl.BlockSpec(shape, lambda i: (i,) + (0,) * (len(shape) - 1))
    o, ko, vo = pl.pallas_call(
        _swa_sample_kernel,
        grid=(bsz // bt,),
        in_specs=[blk((bt, N_HEADS, 128)), blk((bt, 1, 128)), blk((bt, 1, 128)),
                  pl.BlockSpec((N_HEADS, 128), lambda i: (0, 0)),
                  blk((bt, WINDOW, 128)), blk((bt, WINDOW, 128))],
        out_specs=[blk((bt, N_HEADS, 128)), blk((bt, WINDOW, 128)), blk((bt, WINDOW, 128))],
        out_shape=[jax.ShapeDtypeStruct((bsz, N_HEADS, 128), F32), jax.ShapeDtypeStruct(cache_k.shape, F32),
                   jax.ShapeDtypeStruct(cache_v.shape, F32)],
        compiler_params=_cparams(("parallel",)),
        name="swa_sample",
    )(q, kn, vn, sink, cache_k, cache_v)
    y = jnp.concatenate([o[:, 0, 0:64], o[:, 2, 64:128], o[:, 1, 0:64], o[:, 3, 64:128]], axis=1)
    return y, ko, vo


def _ml_scalars(small, m_state):
    z4 = jnp.zeros((small.shape[0], 4), F32)
    return jnp.concatenate([small[:, ML_I_LANE:ML_I_LANE + 4], z4, small[:, ML_F_LANE:ML_F_LANE + 4], z4, m_state, z4,
                            jnp.zeros((small.shape[0], 128 - 24), F32)], axis=1)


def _take_cols(a, pieces, pad):
    parts = [a[..., s:s + n] for s, n in pieces]
    if pad:
        parts.append(jnp.zeros(a.shape[:-1] + (pad,), a.dtype))
    return jnp.concatenate(parts, axis=-1)


def _mix_pieces():
    p = [(0, 768)]
    sq = 784
    p += [(sq, 64), (sq + 128, 64), (sq + 64, 64), (sq + 192, 64)]
    p += [(1040, 256)]
    p += [(1296, 768), (2072, 256)]
    for base in (2328, 2584):
        for half in range(2):
            p += [(base + h * HD + half * 32, 32) for h in range(N_HEADS)]
    p += [(2840, 512)]
    p += [(768, 16), (2064, 8)]
    return p


def _prep_layer(li, w_in, b_in, gla_w_a2, gla_b_a2, gla_norm_g, swa_sinks, ml_norm_g, ret_norm_g, w_branch, w_out,
                ln1_g, ln1_b, w_router_grp, b_router_grp, w_router_exp, b_router_exp, w_exp_gate, w_exp_up,
                w_exp_down, ln2_g, ln2_b):
    pieces = _mix_pieces()
    pad = W_MIX - sum(n for _, n in pieces)
    row = lambda v: v.reshape(1, -1).astype(F32)
    wr = jnp.concatenate([w_router_grp[li], w_router_exp[li],
                          jnp.zeros((D_MODEL, 128 - N_GROUPS - N_EXPERTS), F32)], axis=1)
    wr_hi = wr.astype(BF16)
    wb = w_branch[li]
    wb1 = jnp.concatenate([wb[1, 0:64], wb[1, 128:192], wb[1, 64:128], wb[1, 192:256]], axis=0)
    return dict(
        wm=_take_cols(w_in[li], pieces, pad).astype(BF16),
        bm=row(_take_cols(b_in[li], pieces, pad)),
        wg=w_in[li][:, GATE_COL:].astype(BF16),
        bg=row(b_in[li][GATE_COL:]),
        wa2=jnp.concatenate([gla_w_a2[li], jnp.zeros((128 - GLA_RANK, 128), F32)], axis=0).astype(BF16),
        ba2=row(gla_b_a2[li]),
        gla_g=row(gla_norm_g[li]), ml_g=row(ml_norm_g[li]), ret_g=row(ret_norm_g[li]),
        sinks=row(jnp.concatenate([swa_sinks[li], jnp.zeros((128 - N_HEADS,), F32)])),
        wbr=jnp.stack([wb[0], wb1, wb[2], wb[3]]).astype(BF16),
        wout=w_out[li].astype(BF16),
        ln1_g=row(ln1_g[li]), ln1_b=row(ln1_b[li]), ln2_g=row(ln2_g[li]), ln2_b=row(ln2_b[li]),
        wr_hi=wr_hi, wr_lo=(wr - wr_hi.astype(F32)).astype(BF16),
        br=row(jnp.concatenate([b_router_grp[li], b_router_exp[li], jnp.zeros((128 - N_GROUPS - N_EXPERTS,), F32)])),
        layer=li, we_gate=w_exp_gate, we_up=w_exp_up, we_down=w_exp_down,
    )


TOK_TILE = 256
MOE_PIECES_PROMPT = 32
MOE_PIECES_SAMPLE = 1


MERGE_TILES = 2


def _ffn_half(x2, ys, lw, tm, bp):
    x1, route, cnt, xs = _merge(x2, ys, lw, tm, MERGE_TILES if x2.shape[0] // tm % MERGE_TILES == 0 else 1)
    return _moe(x1, route, cnt, xs, lw, bp, tm)


def _layer_prompt(x2, lw, bsz, seq):
    gla, swa, ml, ret, small = _inproj(x2, lw["wm"], lw["bm"], TOK_TILE)
    ya, s_gla = _gla_prompt(gla, small, lw, bsz, seq)
    yb = _swa_prompt(swa, lw, bsz, seq)
    yc, c_bd, n_row, m_row = _ml_prompt(ml, small, lw, bsz, seq)
    yd, s_ret = _ret_prompt(ret, lw, bsz, seq)
    kv_tail = swa.reshape(bsz, seq, W_SWA)[:, seq - WINDOW:, 256:512]
    states = (_gla_state_from_blockdiag(s_gla),
              kv_tail[:, :, 0:128].reshape(bsz, WINDOW, SWA_KV, HD),
              kv_tail[:, :, 128:256].reshape(bsz, WINDOW, SWA_KV, HD),
              jnp.swapaxes(_diag_blocks(c_bd), -1, -2), n_row.reshape(bsz, N_HEADS, HD), m_row[:, 0, 0:N_HEADS],
              _ret_state_from_blockdiag(s_ret))
    return _ffn_half(x2, (ya, yb, yc, yd), lw, TOK_TILE, MOE_PIECES_PROMPT), states


def _layer_sample(x2, st, lw):
    s_gla, k_buf, v_buf, c_ml, n_ml, m_ml, s_ret = st
    bsz = x2.shape[0]
    gla, swa, ml, ret, small = _inproj(x2, lw["wm"], lw["bm"], bsz)
    to_lanes = lambda s: jnp.moveaxis(s, 0, -1).reshape(-1, bsz)
    from_lanes = lambda s, like: jnp.moveaxis(s.reshape(like.shape[1:] + (bsz,)), -1, 0)
    ya, s_gla_new = _gla_sample(gla, small, to_lanes(s_gla), lw)
    yb, k_new, v_new = _swa_sample(swa, k_buf.reshape(bsz, WINDOW, SWA_KV * HD), v_buf.reshape(bsz, WINDOW, SWA_KV * HD), lw)
    yc, c_new, n_new, m_new = _ml_sample(ml, _ml_scalars(small, m_ml), to_lanes(n_ml), to_lanes(c_ml), lw)
    yd, s_ret_new = _ret_sample(ret, to_lanes(s_ret), lw)
    states = (from_lanes(s_gla_new, s_gla), k_new.reshape(k_buf.shape), v_new.reshape(v_buf.shape),
              from_lanes(c_new, c_ml), from_lanes(n_new, n_ml), m_new[:, 0:N_HEADS], from_lanes(s_ret_new, s_ret))
    return _ffn_half(x2, (ya, yb, yc, yd), lw, bsz, MOE_PIECES_SAMPLE), states


def kernel(x_prompt, x_sample, state_gla, cache_swa_k, cache_swa_v, state_mlstm_c, state_mlstm_n, state_mlstm_m,
           state_ret, w_in, b_in, gla_w_a2, gla_b_a2, gla_norm_g, swa_sinks, ml_norm_g, ret_norm_g, w_branch, w_out,
           ln1_g, ln1_b, w_router_grp, b_router_grp, w_router_exp, b_router_exp, w_exp_gate, w_exp_up, w_exp_down,
           ln2_g, ln2_b):
    bsz, seq, _ = x_prompt.shape
    dec_b, dec_seq, _ = x_sample.shape
    assert dec_seq == 1 and dec_b == LANES and seq % GLA_TILE == 0 and (bsz * seq) % TOK_TILE == 0
    weights = (w_in, b_in, gla_w_a2, gla_b_a2, gla_norm_g, swa_sinks, ml_norm_g, ret_norm_g, w_branch, w_out,
               ln1_g, ln1_b, w_router_grp, b_router_grp, w_router_exp, b_router_exp, w_exp_gate, w_exp_up,
               w_exp_down, ln2_g, ln2_b)
    y_p = x_prompt.reshape(bsz * seq, D_MODEL)
    y_s = x_sample.reshape(dec_b, D_MODEL)
    new_p, new_s = [], []
    for li in range(w_in.shape[0]):
        lw = _prep_layer(li, *weights)
        y_p, st_p = _layer_prompt(y_p, lw, bsz, seq)
        st_in = (state_gla[li], cache_swa_k[li], cache_swa_v[li], state_mlstm_c[li], state_mlstm_n[li],
                 state_mlstm_m[li], state_ret[li])
        y_s, st_s = _layer_sample(y_s, st_in, lw)
        new_p.append(st_p)
        new_s.append(st_s)
    outs = [y_p.reshape(bsz, seq, D_MODEL), y_s.reshape(dec_b, dec_seq, D_MODEL)]
    for i in range(7):
        outs.append(jnp.stack([s[i] for s in new_p], axis=0))
        outs.append(jnp.stack([s[i] for s in new_s], axis=0))
    return tuple(outs)
```

```python
import functools
import math

import jax
import jax.numpy as jnp
from jax import lax
from jax.experimental import pallas as pl
from jax.experimental.pallas import tpu as pltpu

F32 = jnp.float32
BF16 = jnp.bfloat16

D_MODEL = 1024
N_BRANCH = 4
BRANCH_W = D_MODEL // N_BRANCH
N_HEADS = 4
GLA_DK = 32
GLA_RANK = 16
GLA_TAU = 16.0
HD = 64
SWA_KV = 2
WINDOW = 128
ROPE_BASE = 10000.0
PAST_LEN = 16384
N_GROUPS = 4
EPG = 8
N_EXPERTS = N_GROUPS * EPG
D_EXPERT = D_MODEL // 4
DEPTH = 2
ALPHA = (2 * DEPTH) ** 0.25
LN_EPS = 1e-5
GATE_COL = 3352
VMEM_LIMIT = 56 * 1024 * 1024

W_GLA, W_SWA, W_ML, W_RET, W_SMALL = 768, 512, 1024, 1024, 128
W_MIX = W_GLA + W_SWA + W_ML + W_RET + W_SMALL


def _cparams(sem):
    return pltpu.CompilerParams(dimension_semantics=sem, vmem_limit_bytes=VMEM_LIMIT)


def _dot(a, b):
    return jnp.dot(a, b, preferred_element_type=F32)


def _einsum(spec, a, b):
    return jnp.einsum(spec, a, b, preferred_element_type=F32)


def _split2(x):
    hi = x.astype(BF16)
    lo = (x - hi.astype(F32)).astype(BF16)
    return hi, lo


def _split3(x):
    hi = x.astype(BF16)
    r = x - hi.astype(F32)
    mid = r.astype(BF16)
    lo = (r - mid.astype(F32)).astype(BF16)
    return hi, mid, lo


def _dot_exact_lhs(a_exact, b):
    return sum(_dot(a_exact, p) for p in _split3(b))


def _dot_exact_rhs(a, b_exact):
    return sum(_dot(p, b_exact) for p in _split3(a))


def _layer_norm(x, g, b):
    mu = jnp.mean(x, axis=-1, keepdims=True)
    xc = x - mu
    var = jnp.mean(xc * xc, axis=-1, keepdims=True)
    return xc * lax.rsqrt(var + LN_EPS) * g + b


def _inproj_kernel(x_ref, w_ref, b_ref, gla_ref, swa_ref, ml_ref, ret_ref, sm_ref):
    xb = x_ref[...].astype(BF16)
    off = 0
    for ref, width in ((gla_ref, W_GLA), (swa_ref, W_SWA), (ml_ref, W_ML), (ret_ref, W_RET), (sm_ref, W_SMALL)):
        ref[...] = _dot(xb, w_ref[:, off:off + width]) + b_ref[:, off:off + width]
        off += width


def _inproj(x2, wm, bm, tm):
    t = x2.shape[0]
    widths = (W_GLA, W_SWA, W_ML, W_RET, W_SMALL)
    return pl.pallas_call(
        _inproj_kernel,
        grid=(t // tm,),
        in_specs=[pl.BlockSpec((tm, D_MODEL), lambda i: (i, 0)),
                  pl.BlockSpec((D_MODEL, W_MIX), lambda i: (0, 0)),
                  pl.BlockSpec((1, W_MIX), lambda i: (0, 0))],
        out_specs=[pl.BlockSpec((tm, w), lambda i: (i, 0)) for w in widths],
        out_shape=[jax.ShapeDtypeStruct((t, w), F32) for w in widths],
        compiler_params=_cparams(("parallel",)),
        name="inproj",
    )(x2, wm, bm)


def _merge_kernel(x_ref, ya_ref, yb_ref, yc_ref, yd_ref, wg_ref, bg_ref, wbr_ref, wout_ref, g1_ref, b1_ref,
                  wrh_ref, wrl_ref, br_ref, x1_ref, route_ref, cnt_ref, xs_ref, *, tm, sub):
    rows_t = xs_ref.shape[0] // sub
    x1, picks = [], []
    for t in range(sub):
        rows = slice(t * tm, (t + 1) * tm)
        x = x_ref[rows, :]
        xb = x.astype(BF16)
        merged = jnp.zeros((tm, D_MODEL), F32)
        for n, y_ref in enumerate((ya_ref, yb_ref, yc_ref, yd_ref)):
            cols = slice(n * D_MODEL, (n + 1) * D_MODEL)
            gate = 0.5 * jnp.tanh(0.5 * (_dot(xb, wg_ref[:, cols]) + bg_ref[:, cols])) + 0.5
            merged = merged + gate * _dot(y_ref[rows, :].astype(BF16), wbr_ref[n])
        mix = _dot(merged.astype(BF16), wout_ref[...])
        x1.append(_layer_norm(ALPHA * x + mix, g1_ref[...], b1_ref[...]))
        x1_ref[rows, :] = x1[t]
        picks.append(_route(x1[t], wrh_ref[...], wrl_ref[...], br_ref[...], tm))
    for t in range(sub):
        route, cnt, xs = _sort(x1[t], picks[t], tm, rows_t)
        route_ref[t * tm:(t + 1) * tm, :] = route
        cnt_ref[t] = cnt
        xs_ref[t * rows_t:(t + 1) * rows_t, :] = xs


def _route(x1, wr_hi, wr_lo, br, tm):
    xh, xl = _split2(x1)
    logits = _dot(xh, wr_hi) + _dot(xh, wr_lo) + _dot(xl, wr_hi) + br
    lane = lax.broadcasted_iota(jnp.int32, (tm, 128), 1)
    lanef = lane.astype(F32)
    neg = jnp.float32(-jnp.inf)
    big = jnp.float32(1e9)
    isg = lane < N_GROUPS
    gmax = jnp.max(jnp.where(isg, logits, neg), axis=-1, keepdims=True)
    gidx = jnp.min(jnp.where(isg & (logits == gmax), lanef, big), axis=-1, keepdims=True)
    p_group = 1.0 / jnp.sum(jnp.where(isg, jnp.exp(logits - gmax), 0.0), axis=-1, keepdims=True)
    lo = N_GROUPS + EPG * gidx
    em = (lanef >= lo) & (lanef < lo + EPG)
    v1 = jnp.max(jnp.where(em, logits, neg), axis=-1, keepdims=True)
    i1 = jnp.min(jnp.where(em & (logits == v1), lanef, big), axis=-1, keepdims=True)
    em2 = em & (lanef != i1)
    v2 = jnp.max(jnp.where(em2, logits, neg), axis=-1, keepdims=True)
    i2 = jnp.min(jnp.where(em2 & (logits == v2), lanef, big), axis=-1, keepdims=True)
    e21 = jnp.exp(v2 - v1)
    gt1 = 1.0 / (1.0 + e21)
    return i1, i2, p_group * gt1, p_group * (e21 * gt1)


def _sort(x1, picks, tm, rows_t):
    i1, i2, gate1, gate2 = picks
    lane = lax.broadcasted_iota(jnp.int32, (tm, 128), 1)
    lanef = lane.astype(F32)
    sel1 = lanef == i1
    sel2 = lanef == i2
    onehot = jnp.where(sel1 | sel2, 1.0, 0.0)
    cnt = jnp.sum(onehot, axis=0, keepdims=True)
    seg = jnp.floor((cnt + (PIECE - 1)) * (1.0 / PIECE)) * PIECE
    er = lax.broadcasted_iota(jnp.int32, (128, 128), 0)
    ec = lax.broadcasted_iota(jnp.int32, (128, 128), 1)
    seg_off = _dot(jnp.broadcast_to(seg, (8, 128)).astype(BF16), jnp.where(er < ec, 1.0, 0.0).astype(BF16))[0:1]
    row = lax.broadcasted_iota(jnp.int32, (tm, tm), 0)
    col = lax.broadcasted_iota(jnp.int32, (tm, tm), 1)
    earlier = _dot(jnp.where(col < row, 1.0, 0.0).astype(BF16), onehot.astype(BF16))
    where_to = seg_off + earlier
    w1 = jnp.where(sel1, where_to, 0.0)
    w2 = jnp.where(sel2, where_to, 0.0)
    pos1 = jnp.sum(w1, axis=-1, keepdims=True)
    pos2 = jnp.sum(w2, axis=-1, keepdims=True)
    ones = jnp.ones((8, 128), BF16)
    prow1 = sum(_dot_nt(ones, p) for p in _split2(w1))[0:1]
    prow2 = sum(_dot_nt(ones, p) for p in _split2(w2))[0:1]
    r = lax.broadcasted_iota(jnp.int32, (rows_t, tm), 0).astype(F32)
    place = jnp.where((r == prow1) | (r == prow2), 1.0, 0.0).astype(BF16)
    xs = _dot(place, x1.astype(BF16)).astype(BF16)

    vals = (i1 - N_GROUPS, i2 - N_GROUPS, gate1, gate2, pos1, pos2)
    route = jnp.zeros((tm, 128), F32)
    for k, v in enumerate(vals):
        route = jnp.where(lane == k, v, route)
    return route, cnt, xs


PIECE = 16


def _tile_rows(tm):
    return -(-(2 * tm + N_EXPERTS * (PIECE - 1)) // 128) * 128


def _merge(x2, ys, lw, tm, sub):
    t = x2.shape[0]
    nt = t // tm
    rows_t = _tile_rows(tm)
    tok = lambda w: pl.BlockSpec((sub * tm, w), lambda i: (i, 0))
    full = lambda shape: pl.BlockSpec(shape, lambda i: (0,) * len(shape), pipeline_mode=pl.Buffered(1))
    return pl.pallas_call(
        functools.partial(_merge_kernel, tm=tm, sub=sub),
        grid=(nt // sub,),
        in_specs=[tok(D_MODEL), tok(BRANCH_W), tok(BRANCH_W), tok(BRANCH_W), tok(BRANCH_W),
                  full((D_MODEL, N_BRANCH * D_MODEL)), full((1, N_BRANCH * D_MODEL)),
                  full((N_BRANCH, BRANCH_W, D_MODEL)), full((D_MODEL, D_MODEL)),
                  full((1, D_MODEL)), full((1, D_MODEL)),
                  full((D_MODEL, 128)), full((D_MODEL, 128)), full((1, 128))],
        out_specs=[tok(D_MODEL), tok(128), pl.BlockSpec((sub, 1, 128), lambda i: (i, 0, 0)),
                   pl.BlockSpec((sub * rows_t, D_MODEL), lambda i: (i, 0))],
        out_shape=[jax.ShapeDtypeStruct((t, D_MODEL), F32), jax.ShapeDtypeStruct((t, 128), F32),
                   jax.ShapeDtypeStruct((nt, 1, 128), F32), jax.ShapeDtypeStruct((nt * rows_t, D_MODEL), BF16)],
        compiler_params=_cparams(("parallel",)),
        name="merge",
    )(x2, *ys, lw["wg"], lw["bg"], lw["wbr"], lw["wout"], lw["ln1_g"], lw["ln1_b"], lw["wr_hi"], lw["wr_lo"], lw["br"])


def _piece_copy(hbm, hbm_row, vmem, piece, sem, to_vmem):
    h = hbm.at[pl.ds(pl.multiple_of(hbm_row, PIECE), PIECE), :]
    v = vmem.at[pl.ds(pl.multiple_of(piece * PIECE, PIECE), PIECE), :]
    return pltpu.make_async_copy(h, v, sem) if to_vmem else pltpu.make_async_copy(v, h, sem)


def _expert_kernel(be_ref, nu_ref, nv_ref, off_ref, xs_hbm, wg_ref, wu_ref, wd_ref, ys_hbm, xin, yout, sem_in, sem_out,
                   *, bp):
    del xs_hbm
    blk = pl.program_id(0)
    n_used = nu_ref[0]
    slot = blk % 2

    def pieces(b, s, buf, sem, to_vmem, start):
        n = nv_ref[b]

        def one(p, priority):
            cp = _piece_copy(ys_hbm, off_ref[b * bp + p], buf.at[s], p, sem.at[s], to_vmem)
            cp.start(priority=priority) if start else cp.wait()

        def body(i, c):
            one(2 * i, 0)

            @pl.when(2 * i + 1 < n)
            def _():
                one(2 * i + 1, 1)
            return c
        lax.fori_loop(0, (n + 1) // 2, body, 0)

    @pl.when(blk == 0)
    def _():
        xin[...] = jnp.zeros_like(xin)

        @pl.when(n_used > 0)
        def _():
            pieces(0, 0, xin, sem_in, True, True)

    @pl.when(blk + 1 < n_used)
    def _():
        pieces(blk + 1, 1 - slot, xin, sem_in, True, True)

    @pl.when(blk < n_used)
    def _():
        pieces(blk, slot, xin, sem_in, True, False)

        @pl.when(blk >= 2)
        def _():
            pieces(blk - 2, slot, yout, sem_out, False, False)

        xb = xin[slot]
        hid = jax.nn.silu(_dot(xb, wg_ref[...].astype(BF16))) * _dot(xb, wu_ref[...].astype(BF16))
        yout[slot] = _dot(hid.astype(BF16), wd_ref[...].astype(BF16)).astype(BF16)
        pieces(blk, slot, yout, sem_out, False, True)

    @pl.when(blk == n_used - 1)
    def _():
        @pl.when(blk >= 1)
        def _():
            pieces(blk - 1, 1 - slot, yout, sem_out, False, False)
        pieces(blk, slot, yout, sem_out, False, False)


def _experts(xs, block_expert, n_used, n_valid, piece_off, lw, bp):
    n_blocks = block_expert.shape[0]
    rows = bp * PIECE
    li = lw["layer"]
    wspec = lambda shape: pl.BlockSpec((None, None) + shape, lambda b, be, nu, nv, off: (li, be[b], 0, 0))
    grid_spec = pltpu.PrefetchScalarGridSpec(
        num_scalar_prefetch=4,
        grid=(n_blocks,),
        in_specs=[pl.BlockSpec(memory_space=pl.ANY), wspec((D_MODEL, D_EXPERT)), wspec((D_MODEL, D_EXPERT)),
                  wspec((D_EXPERT, D_MODEL))],
        out_specs=pl.BlockSpec(memory_space=pl.ANY),
        scratch_shapes=[pltpu.VMEM((2, rows, D_MODEL), BF16), pltpu.VMEM((2, rows, D_MODEL), BF16),
                        pltpu.SemaphoreType.DMA((2,)), pltpu.SemaphoreType.DMA((2,))],
    )
    return pl.pallas_call(
        functools.partial(_expert_kernel, bp=bp),
        grid_spec=grid_spec,
        out_shape=jax.ShapeDtypeStruct(xs.shape, BF16),
        input_output_aliases={4: 0},
        compiler_params=_cparams(("arbitrary",)),
        name="experts",
    )(block_expert, n_used, n_valid, piece_off, xs, lw["we_gate"], lw["we_up"], lw["we_down"])


def _combine_kernel(ys_ref, x1_ref, route_ref, g2_ref, b2_ref, out_ref, *, tm):
    route = route_ref[...]
    ys = ys_ref[...]
    r = lax.broadcasted_iota(jnp.int32, (tm, ys.shape[0]), 1).astype(F32)
    ffn = jnp.zeros((tm, D_MODEL), F32)
    for j in range(2):
        pick = jnp.where(r == route[:, 4 + j:5 + j], 1.0, 0.0).astype(BF16)
        ffn = ffn + route[:, 2 + j:3 + j] * _dot(pick, ys)
    out_ref[...] = _layer_norm(ALPHA * x1_ref[...] + ffn, g2_ref[...], b2_ref[...])


def _combine(ys, x1, route, lw, tm):
    t = x1.shape[0]
    rows_t = _tile_rows(tm)
    full = lambda shape: pl.BlockSpec(shape, lambda i: (0,) * len(shape))
    return pl.pallas_call(
        functools.partial(_combine_kernel, tm=tm),
        grid=(t // tm,),
        in_specs=[pl.BlockSpec((rows_t, D_MODEL), lambda i: (i, 0)), pl.BlockSpec((tm, D_MODEL), lambda i: (i, 0)),
                  pl.BlockSpec((tm, 128), lambda i: (i, 0)), full((1, D_MODEL)), full((1, D_MODEL))],
        out_specs=pl.BlockSpec((tm, D_MODEL), lambda i: (i, 0)),
        out_shape=jax.ShapeDtypeStruct((t, D_MODEL), F32),
        compiler_params=_cparams(("parallel",)),
        name="combine",
    )(ys, x1, route, lw["ln2_g"], lw["ln2_b"])


def _piece_tables(cnt, tm, bp):
    nt = cnt.shape[0]
    rows_t = _tile_rows(tm)
    counts = cnt[:, 0, N_GROUPS:N_GROUPS + N_EXPERTS].astype(jnp.int32)
    pcs = ((counts + PIECE - 1) // PIECE).T
    seg_row = ((jnp.cumsum(pcs, axis=0) - pcs) * PIECE + jnp.arange(nt, dtype=jnp.int32)[None, :] * rows_t)
    tile_end = jnp.cumsum(pcs, axis=1)
    total = tile_end[:, -1]
    e_end = jnp.cumsum((total + bp - 1) // bp * bp)
    e_start = jnp.concatenate([jnp.zeros((1,), jnp.int32), e_end[:-1]])
    n_blocks = -(-(nt * rows_t // PIECE + N_EXPERTS * (bp - 1)) // bp)
    g = jnp.arange(n_blocks * bp, dtype=jnp.int32)[:, None]
    in_e = ((e_start[None, :] <= g) & (g < e_end[None, :])).astype(F32)
    pick = lambda table: jnp.dot(in_e, table.astype(F32), precision=lax.Precision.HIGHEST)
    local = g.astype(F32) - pick(e_start[:, None])
    ends, starts, rows = pick(tile_end), pick(tile_end - pcs), pick(seg_row)
    in_t = ((starts <= local) & (local < ends)).astype(F32)
    off = jnp.sum(in_t * (rows + (local - starts) * PIECE), axis=1).astype(jnp.int32)
    n_valid = jnp.sum(jnp.sum(in_t, axis=1).reshape(n_blocks, bp), axis=1).astype(jnp.int32)
    expert = pick(jnp.arange(N_EXPERTS, dtype=jnp.int32)[:, None])[:, 0]
    block_expert = expert.reshape(n_blocks, bp)[:, 0].astype(jnp.int32)
    n_used = (e_end[-1] // bp).astype(jnp.int32).reshape(1)
    return block_expert, n_used, n_valid, off


def _moe(x1, route, cnt, xs, lw, bp, tm):
    ys = _experts(xs, *_piece_tables(cnt, tm, bp), lw, bp)
    return _combine(ys, x1, route, lw, tm)


CHUNK = 128
STEP_CHUNKS = 4


def _dot_nt(a, b):
    return lax.dot_general(a, b, (((1,), (1,)), ((), ())), preferred_element_type=F32)


def _dot_tn(a, b):
    return lax.dot_general(a, b, (((0,), (0,)), ((), ())), preferred_element_type=F32)


def _lane_head(width, group):
    return lax.broadcasted_iota(jnp.int32, (1, width), 1) // group


def _block_diag_mask(rows_head, cols_head):
    return rows_head == cols_head


def _head_norm(o, gain):
    r = lax.broadcasted_iota(jnp.int32, (BRANCH_W, BRANCH_W), 0) // HD
    c = lax.broadcasted_iota(jnp.int32, (BRANCH_W, BRANCH_W), 1) // HD
    avg = jnp.where(r == c, 1.0 / HD, 0.0).astype(BF16)
    mu = _dot_exact_rhs(o, avg)
    xc = o - mu
    var = _dot_exact_rhs(xc * xc, avg)
    return xc * lax.rsqrt(var + LN_EPS) * gain


def _stack_heads(x, head_of_lane):
    return jnp.concatenate([jnp.where(head_of_lane == h, x, 0.0) for h in range(N_HEADS)], axis=0)


def _pick_heads(stacked, rows, head_of_lane):
    out = jnp.where(head_of_lane == 0, stacked[0:rows], 0.0)
    for h in range(1, N_HEADS):
        out = jnp.where(head_of_lane == h, stacked[h * rows:(h + 1) * rows], out)
    return out


def _spread_heads(cols, head_of_lane):
    out = jnp.where(head_of_lane == 0, cols[0], 0.0)
    for h in range(1, N_HEADS):
        out = jnp.where(head_of_lane == h, cols[h], out)
    return out


SWA_BLOCKS = 4


def _swa_prompt_kernel(q_ref, kv_ref, kvp_ref, sink_ref, y_ref):
    n = pl.program_id(1)
    lo = lax.broadcasted_iota(jnp.int32, (1, 128), 1) < HD
    e = lax.broadcasted_iota(jnp.int32, (2 * WINDOW, N_HEADS * WINDOW), 0)
    col = lax.broadcasted_iota(jnp.int32, (2 * WINDOW, N_HEADS * WINDOW), 1)
    a = col % WINDOW
    band = (e > a) & (e <= a + WINDOW)
    head = lax.broadcasted_iota(jnp.int32, (1, N_HEADS * WINDOW), 1) // WINDOW
    sink = _spread_heads([sink_ref[:, h:h + 1] for h in range(N_HEADS)], head)
    blocks = range(SWA_BLOCKS)
    scores, vexts = [], []
    for i in blocks:
        rows = slice(i * WINDOW, (i + 1) * WINDOW)
        q = q_ref[rows, :]
        qa, qb = q[:, 0:128], q[:, 128:256]
        kv = kv_ref[rows, :]
        kvp = kvp_ref[...] if i == 0 else kv_ref[(i - 1) * WINDOW:i * WINDOW, :]
        kext = jnp.concatenate([kvp[:, 0:128], kv[:, 0:128]], axis=0).astype(BF16)
        vexts.append(jnp.concatenate([kvp[:, 128:256], kv[:, 128:256]], axis=0).astype(BF16))
        qs = jnp.concatenate([jnp.where(lo, qa, 0.0), jnp.where(lo, qb, 0.0),
                              jnp.where(lo, 0.0, qa), jnp.where(lo, 0.0, qb)], axis=0).astype(BF16)
        scores.append(_dot_nt(kext, qs))
    probs, inv_den = [], []
    for i in blocks:
        valid = band & ((n > 0) | (e >= WINDOW)) if i == 0 else band
        s = jnp.where(valid, scores[i] * (HD ** -0.5), -jnp.inf)
        mx = jnp.maximum(jnp.max(s, axis=0, keepdims=True), sink)
        p = jnp.exp(s - mx)
        inv_den.append(1.0 / (jnp.sum(p, axis=0, keepdims=True) + jnp.exp(sink - mx)))
        probs.append(p.astype(BF16))
    outs = [_dot_tn(vexts[i], probs[i]) for i in blocks]
    for i in blocks:
        o = outs[i] * inv_den[i]
        ya = jnp.concatenate([o[0:HD, 0:128], o[HD:128, 256:384]], axis=0).T
        yb = jnp.concatenate([o[0:HD, 128:256], o[HD:128, 384:512]], axis=0).T
        y_ref[i * WINDOW:(i + 1) * WINDOW, :] = jnp.concatenate([ya, yb], axis=1)


def _swa_prompt(swa, lw, bsz, seq):
    qt = SWA_BLOCKS * WINDOW
    nt = seq // qt
    return pl.pallas_call(
        _swa_prompt_kernel,
        grid=(bsz, nt),
        in_specs=[pl.BlockSpec((qt, 256), lambda b, n: (b * nt + n, 0)),
                  pl.BlockSpec((qt, 256), lambda b, n: (b * nt + n, 1)),
                  pl.BlockSpec((WINDOW, 256), lambda b, n: (jnp.maximum((b * nt + n) * SWA_BLOCKS - 1, 0), 1)),
                  pl.BlockSpec((1, 128), lambda b, n: (0, 0))],
        out_specs=pl.BlockSpec((qt, BRANCH_W), lambda b, n: (b * nt + n, 0)),
        out_shape=jax.ShapeDtypeStruct((bsz * seq, BRANCH_W), F32),
        compiler_params=_cparams(("parallel", "parallel")),
        name="swa_prompt",
    )(swa, swa, swa, lw["sinks"])


def _rotate(x, cos, sin):
    x1, x2 = x[:, 0:128], x[:, 128:256]
    return jnp.concatenate([x1 * cos - x2 * sin, x1 * sin + x2 * cos], axis=1)


def _ret_prompt_kernel(r_ref, cos_ref, sin_ref, dmat_ref, qdec_ref, kdec_ref, cdec_ref, g_ref, y_ref, s_out_ref, s_ref):
    n = pl.program_id(1)

    @pl.when(n == 0)
    def _():
        s_ref[...] = jnp.zeros_like(s_ref)

    c = CHUNK
    chunks = range(STEP_CHUNKS)
    hq = _lane_head(BRANCH_W, 32) % N_HEADS
    hv = _lane_head(BRANCH_W, HD)
    rows_head = (lax.broadcasted_iota(jnp.int32, (BRANCH_W, 1), 0) // 32) % N_HEADS
    qr, kr, v = [], [], []
    for i in chunks:
        rows = slice(i * c, (i + 1) * c)
        cos, sin = cos_ref[rows, :], sin_ref[rows, :]
        qr.append(_rotate(r_ref[rows, 0:256], cos, sin))
        kr.append(_rotate(r_ref[rows, 256:512], cos, sin) * (HD ** -0.5))
        v.append(r_ref[rows, 512:768].astype(BF16))
    att = [_dot_nt(_stack_heads(qr[i], hq).astype(BF16), kr[i].astype(BF16)) * dmat_ref[...] for i in chunks]
    upd = [_dot_tn((kr[i] * kdec_ref[...]).astype(BF16), v[i]) for i in chunks]
    states = [s_ref[...]]
    for i in chunks:
        states.append(cdec_ref[...] * states[i] + jnp.where(rows_head == hv, upd[i], 0.0))
    o = [_pick_heads(_dot(att[i].astype(BF16), v[i]), c, hv) for i in chunks]
    o = [o[i] + _dot((qr[i] * qdec_ref[...]).astype(BF16), states[i].astype(BF16)) for i in chunks]
    s_ref[...] = states[-1]
    s_out_ref[...] = states[-1]
    y_ref[...] = jax.nn.silu(r_ref[:, 768:1024]) * _head_norm(jnp.concatenate(o, axis=0), g_ref[...])


def _ret_tables(seq):
    c = CHUNK
    half = HD // 2
    freq = ROPE_BASE ** (-jnp.arange(half, dtype=F32) / half)
    ang = jnp.arange(seq).astype(F32)[:, None] * freq[None, :]
    cos = jnp.tile(jnp.cos(ang), (1, N_HEADS))
    sin = jnp.tile(jnp.sin(ang), (1, N_HEADS))
    log_g = jnp.log1p(-jnp.exp2(-5.0 - jnp.arange(N_HEADS, dtype=F32)))
    idx = jnp.arange(c, dtype=F32)
    rel = idx[:, None] - idx[None, :]
    dmat = jnp.exp(jnp.where(rel[None] >= 0, rel[None] * log_g[:, None, None], -jnp.inf)).reshape(N_HEADS * c, c)
    lane_head_q = (jnp.arange(BRANCH_W) // 32) % N_HEADS
    qdec = jnp.exp((idx + 1.0)[:, None] * log_g[lane_head_q][None, :])
    kdec = jnp.exp((c - 1.0 - idx)[:, None] * log_g[lane_head_q][None, :])
    cdec = jnp.exp(c * log_g)[jnp.arange(BRANCH_W) // HD][None, :]
    return cos, sin, dmat, qdec, kdec, cdec


def _ret_prompt(ret, lw, bsz, seq):
    c = CHUNK
    rows = STEP_CHUNKS * c
    nc = seq // rows
    cos, sin, dmat, qdec, kdec, cdec = _ret_tables(seq)
    full = lambda shape: pl.BlockSpec(shape, lambda b, n: (0,) * len(shape))
    return pl.pallas_call(
        _ret_prompt_kernel,
        grid=(bsz, nc),
        in_specs=[pl.BlockSpec((rows, W_RET), lambda b, n: (b * nc + n, 0)),
                  pl.BlockSpec((rows, 128), lambda b, n: (n, 0)), pl.BlockSpec((rows, 128), lambda b, n: (n, 0)),
                  full((N_HEADS * c, c)), full((c, BRANCH_W)), full((c, BRANCH_W)), full((1, BRANCH_W)),
                  full((1, BRANCH_W))],
        out_specs=[pl.BlockSpec((rows, BRANCH_W), lambda b, n: (b * nc + n, 0)),
                   pl.BlockSpec((None, BRANCH_W, BRANCH_W), lambda b, n: (b, 0, 0))],
        out_shape=[jax.ShapeDtypeStruct((bsz * seq, BRANCH_W), F32),
                   jax.ShapeDtypeStruct((bsz, BRANCH_W, BRANCH_W), F32)],
        scratch_shapes=[pltpu.VMEM((BRANCH_W, BRANCH_W), F32)],
        compiler_params=_cparams(("parallel", "arbitrary")),
        name="ret_prompt",
    )(ret, cos, sin, dmat, qdec, kdec, cdec, lw["ret_g"])


def _ret_state_from_blockdiag(s_bd):
    b = s_bd.shape[0]
    s = s_bd.reshape(b, 2, N_HEADS, 32, N_HEADS, HD)
    s = jnp.stack([s[:, :, h, :, h, :] for h in range(N_HEADS)], axis=1)
    return s.reshape(b, N_HEADS, HD, HD)


ML_I_LANE = GLA_RANK
ML_F_LANE = GLA_RANK + N_HEADS


def _ml_prompt_kernel(m_ref, sm_ref, g_ref, y_ref, c_out_ref, n_out_ref, m_out_ref, c_ref, n_ref, mm_ref):
    nchunk = pl.program_id(1)

    @pl.when(nchunk == 0)
    def _():
        c_ref[...] = jnp.zeros_like(c_ref)
        n_ref[...] = jnp.zeros_like(n_ref)
        mm_ref[...] = jnp.zeros_like(mm_ref)

    c = CHUNK
    chunks = range(STEP_CHUNKS)
    row = lax.broadcasted_iota(jnp.int32, (c, c), 0)
    col = lax.broadcasted_iota(jnp.int32, (c, c), 1)
    tri_lo = jnp.where(col <= row, 1.0, 0.0).astype(BF16)
    tri_up = jnp.where(row <= col, 1.0, 0.0).astype(BF16)
    hv = _lane_head(BRANCH_W, HD)
    hq = _lane_head(N_HEADS * c, c)
    key = lax.broadcasted_iota(jnp.int32, (c, N_HEADS * c), 0)
    qry = lax.broadcasted_iota(jnp.int32, (c, N_HEADS * c), 1) % c
    causal_t = key <= qry
    r = lax.broadcasted_iota(jnp.int32, (BRANCH_W, BRANCH_W), 0) // HD
    cc = lax.broadcasted_iota(jnp.int32, (BRANCH_W, BRANCH_W), 1) // HD
    same_head = r == cc
    lane = lax.broadcasted_iota(jnp.int32, (1, 128), 1)
    gate_lane = lax.broadcasted_iota(jnp.int32, (128, 1), 0)
    is_i = (gate_lane >= ML_I_LANE) & (gate_lane < ML_I_LANE + N_HEADS)
    is_f = (gate_lane >= ML_F_LANE) & (gate_lane < ML_F_LANE + N_HEADS)
    gate_head = jnp.where(is_i, gate_lane - ML_I_LANE, gate_lane - ML_F_LANE)
    sel_q = jnp.where((is_i | is_f) & (gate_head == hq), 1.0, 0.0).astype(BF16)
    sel_v = jnp.where((is_i | is_f) & (gate_head == hv), 1.0, 0.0).astype(BF16)
    sel_qv = jnp.concatenate([sel_q, sel_v], axis=1)
    lane128 = lax.broadcasted_iota(jnp.int32, (1, 128), 1)
    in_i = (lane128 >= ML_I_LANE) & (lane128 < ML_I_LANE + N_HEADS)
    in_f = (lane128 >= ML_F_LANE) & (lane128 < ML_F_LANE + N_HEADS)
    head_rows = lax.broadcasted_iota(jnp.int32, (8, 1), 0)

    q, ks, vt, brow, u_q, u_v, sc = [], [], [], [], [], [], []
    for i in chunks:
        rows = slice(i * c, (i + 1) * c)
        q.append(m_ref[rows, 0:256])
        ks.append(m_ref[rows, 256:512] * (HD ** -0.5))
        vt.append(m_ref[rows, 512:768].T.astype(BF16))
        sm = sm_ref[rows, :]
        bcols = _dot_exact_lhs(tri_lo, jax.nn.log_sigmoid(sm))
        brows = _dot_exact_rhs(jax.nn.log_sigmoid(sm.T), tri_up)
        brow.append(jnp.concatenate([brows[ML_F_LANE + h:ML_F_LANE + h + 1, :] for h in range(N_HEADS)], axis=1))
        z = jnp.where(in_i, sm, 0.0) - jnp.where(in_f, bcols, 0.0)
        u = _dot_exact_rhs(z, sel_qv)
        u_q.append(u[:, 0:N_HEADS * c])
        u_v.append(u[:, N_HEADS * c:])
        sc.append(_dot_nt(ks[i].astype(BF16), _stack_heads(q[i], hv).astype(BF16)))

    mm = mm_ref[...]
    qk, w_inter, rowsum, floor, khat, carry_b = [], [], [], [], [], []
    for i in chunks:
        m_prev = _spread_heads([mm[:, h:h + 1] for h in range(N_HEADS)], hq)
        d_log = jnp.where(causal_t, brow[i] + u_q[i], -jnp.inf)
        inter_log = brow[i] + m_prev
        m_t = jnp.maximum(inter_log, jnp.max(d_log, axis=0, keepdims=True))
        qk_i = sc[i] * jnp.exp(d_log - m_t)
        qk.append(qk_i.astype(BF16))
        rowsum.append(jnp.sum(qk_i, axis=0, keepdims=True))
        w_inter.append(jnp.exp(inter_log - m_t))
        floor.append(jnp.exp(-m_t))
        m_new = [m_t[:, (h + 1) * c - 1:(h + 1) * c] for h in range(N_HEADS)]
        b_last = [brow[i][:, (h + 1) * c - 1:(h + 1) * c] for h in range(N_HEADS)]
        carry_b.append(_spread_heads([jnp.exp(b_last[h] + mm[:, h:h + 1] - m_new[h]) for h in range(N_HEADS)], hv))
        khat.append(ks[i] * jnp.exp(u_v[i] + _spread_heads([b_last[h] - m_new[h] for h in range(N_HEADS)], hv)))
        mm = jnp.zeros((1, 128), F32)
        for h in range(N_HEADS):
            mm = jnp.where(lane == h, m_new[h], mm)

    intra = [_dot(vt[i], qk[i]) for i in chunks]
    upd = [_dot(vt[i], khat[i].astype(BF16)) for i in chunks]
    cms, nrows = [c_ref[...]], [n_ref[...]]
    for i in chunks:
        cms.append(carry_b[i] * cms[i] + jnp.where(same_head, upd[i], 0.0))
        nrows.append(carry_b[i] * nrows[i] + jnp.sum(khat[i], axis=0, keepdims=True))
    hout = []
    for i in chunks:
        qb = q[i].astype(BF16)
        inter = _dot_nt(cms[i].astype(BF16), qb)
        n_sel = jnp.where(head_rows == hv, nrows[i], 0.0)
        qn = sum(_dot_nt(p, qb) for p in _split3(n_sel))
        blocks = []
        for h in range(N_HEADS):
            cols = slice(h * c, (h + 1) * c)
            den = w_inter[i][:, cols] * qn[h:h + 1, :] + rowsum[i][:, cols]
            num = w_inter[i][:, cols] * inter[h * HD:(h + 1) * HD, :] + intra[i][h * HD:(h + 1) * HD, cols]
            blocks.append(num / jnp.maximum(jnp.abs(den), floor[i][:, cols]))
        hout.append(jnp.concatenate(blocks, axis=0).T)
    y_ref[...] = _head_norm(jax.nn.sigmoid(m_ref[:, 768:1024]) * jnp.concatenate(hout, axis=0), g_ref[...])

    c_ref[...] = cms[-1]
    n_ref[...] = nrows[-1]
    mm_ref[...] = mm
    c_out_ref[...] = cms[-1]
    n_out_ref[...] = nrows[-1]
    m_out_ref[...] = mm


def _ml_prompt(ml, small, lw, bsz, seq):
    rows = STEP_CHUNKS * CHUNK
    nc = seq // rows
    return pl.pallas_call(
        _ml_prompt_kernel,
        grid=(bsz, nc),
        in_specs=[pl.BlockSpec((rows, W_ML), lambda b, n: (b * nc + n, 0)),
                  pl.BlockSpec((rows, W_SMALL), lambda b, n: (b * nc + n, 0)),
                  pl.BlockSpec((1, BRANCH_W), lambda b, n: (0, 0))],
        out_specs=[pl.BlockSpec((rows, BRANCH_W), lambda b, n: (b * nc + n, 0)),
                   pl.BlockSpec((None, BRANCH_W, BRANCH_W), lambda b, n: (b, 0, 0)),
                   pl.BlockSpec((None, 1, BRANCH_W), lambda b, n: (b, 0, 0)),
                   pl.BlockSpec((None, 1, 128), lambda b, n: (b, 0, 0))],
        out_shape=[jax.ShapeDtypeStruct((bsz * seq, BRANCH_W), F32),
                   jax.ShapeDtypeStruct((bsz, BRANCH_W, BRANCH_W), F32),
                   jax.ShapeDtypeStruct((bsz, 1, BRANCH_W), F32),
                   jax.ShapeDtypeStruct((bsz, 1, 128), F32)],
        scratch_shapes=[pltpu.VMEM((BRANCH_W, BRANCH_W), F32), pltpu.VMEM((1, BRANCH_W), F32),
                        pltpu.VMEM((1, 128), F32)],
        compiler_params=_cparams(("parallel", "arbitrary")),
        name="ml_prompt",
    )(ml, small, lw["ml_g"])


def _diag_blocks(s_bd):
    b = s_bd.shape[0]
    s = s_bd.reshape(b, N_HEADS, HD, N_HEADS, HD)
    return jnp.stack([s[:, h, :, h, :] for h in range(N_HEADS)], axis=1)


GLA_TILE = 256
GLA_SUB = 16


def _gla_prompt_kernel(x_ref, sm_ref, wa2_ref, ba2_ref, g_ref, y_ref, s_out_ref, s_ref):
    n = pl.program_id(1)

    @pl.when(n == 0)
    def _():
        s_ref[...] = jnp.zeros_like(s_ref)

    tq, sc = GLA_TILE, GLA_SUB
    x = x_ref[...]
    q = x[:, 0:128] * (GLA_DK ** -0.5)
    k = x[:, 128:256]
    v = x[:, 256:512]
    log_a = jax.nn.log_sigmoid(_dot(sm_ref[...].astype(BF16), wa2_ref[...]) + ba2_ref[...]) / GLA_TAU
    r = lax.broadcasted_iota(jnp.int32, (tq, tq), 0)
    c = lax.broadcasted_iota(jnp.int32, (tq, tq), 1)
    same = (r // sc) == (c // sc)
    b = _dot_exact_lhs(jnp.where(same & (c <= r), 1.0, 0.0).astype(BF16), log_a)
    btot = _dot_exact_lhs(jnp.where(same, 1.0, 0.0).astype(BF16), log_a)
    qd = (q * jnp.exp(b)).astype(BF16)
    dec_t = jnp.exp(btot).T
    vb = v.astype(BF16)
    kh = lax.broadcasted_iota(jnp.int32, (128, 1), 0) // GLA_DK
    hv = _lane_head(BRANCH_W, HD)
    expand = jnp.where(kh == hv, 1.0, 0.0).astype(BF16)
    trow = lax.broadcasted_iota(jnp.int32, (sc, 1), 0)
    chunks = range(tq // sc)
    khat_b = (k * jnp.exp(btot - b)).astype(BF16)
    upds = [_dot_tn(khat_b[ci * sc:(ci + 1) * sc], vb[ci * sc:(ci + 1) * sc]) for ci in chunks]
    states = [s_ref[...]]
    for ci in chunks:
        states.append(dec_t[:, ci * sc:ci * sc + 1] * states[ci] + jnp.where(kh == hv, upds[ci], 0.0))
    outs = []
    for ci in chunks:
        lo = ci * sc
        qc, kc, bc, vc = q[lo:lo + sc], k[lo:lo + sc], b[lo:lo + sc], v[lo:lo + sc]
        pair = [qc * kc[j:j + 1] * jnp.exp(jnp.where(trow >= j, bc - bc[j:j + 1], -jnp.inf)) for j in range(sc)]
        att = _dot(jnp.concatenate(pair, axis=0).astype(BF16), expand)
        o = att[0:sc] * vc[0:1]
        for j in range(1, sc):
            o = o + att[j * sc:(j + 1) * sc] * vc[j:j + 1]
        outs.append(o)
    outs = [outs[ci] + _dot(qd[ci * sc:(ci + 1) * sc], states[ci].astype(BF16)) for ci in chunks]
    s_ref[...] = states[-1]
    s_out_ref[...] = states[-1]
    y_ref[...] = jax.nn.silu(x[:, 512:768]) * _head_norm(jnp.concatenate(outs, axis=0), g_ref[...])


def _gla_prompt(gla, small, lw, bsz, seq):
    tq = GLA_TILE
    nt = seq // tq
    full = lambda shape: pl.BlockSpec(shape, lambda b, n: (0,) * len(shape))
    return pl.pallas_call(
        _gla_prompt_kernel,
        grid=(bsz, nt),
        in_specs=[pl.BlockSpec((tq, W_GLA), lambda b, n: (b * nt + n, 0)),
                  pl.BlockSpec((tq, W_SMALL), lambda b, n: (b * nt + n, 0)),
                  full((128, 128)), full((1, 128)), full((1, BRANCH_W))],
        out_specs=[pl.BlockSpec((tq, BRANCH_W), lambda b, n: (b * nt + n, 0)),
                   pl.BlockSpec((None, 128, BRANCH_W), lambda b, n: (b, 0, 0))],
        out_shape=[jax.ShapeDtypeStruct((bsz * seq, BRANCH_W), F32),
                   jax.ShapeDtypeStruct((bsz, 128, BRANCH_W), F32)],
        scratch_shapes=[pltpu.VMEM((128, BRANCH_W), F32)],
        compiler_params=_cparams(("parallel", "arbitrary")),
        name="gla_prompt",
    )(gla, small, lw["wa2"], lw["ba2"], lw["gla_g"])


def _gla_state_from_blockdiag(s_bd):
    b = s_bd.shape[0]
    s = s_bd.reshape(b, N_HEADS, GLA_DK, N_HEADS, HD)
    return jnp.stack([s[:, h, :, h, :] for h in range(N_HEADS)], axis=1)


LANES = 128


def _rank1_update(st_ref, snt_ref, q_ref, k_ref, decay_row, v, dk):
    def body(i, o):
        rows = pl.ds(pl.multiple_of(i * HD, HD), HD)
        new = decay_row(i) * st_ref[rows, :] + k_ref[pl.ds(i, 1), :] * v
        snt_ref[rows, :] = new
        return o + q_ref[pl.ds(i, 1), :] * new
    return lax.fori_loop(0, dk, body, jnp.zeros((HD, LANES), F32), unroll=4)


def _head_rows(h, n):
    return pl.ds(pl.multiple_of(h * n, n), n)


def _gla_sample_kernel(x_ref, sm_ref, wa2_ref, ba2_ref, g_ref, s_ref, y_ref, s_out_ref,
                       qt_ref, kt_ref, at_ref, vt_ref, ot_ref):
    h = pl.program_id(0)

    @pl.when(h == 0)
    def _():
        x = x_ref[...]
        qt_ref[...] = (x[:, 0:128] * (GLA_DK ** -0.5)).T
        kt_ref[...] = x[:, 128:256].T
        vt_ref[...] = x[:, 256:512].T
        log_a = jax.nn.log_sigmoid(_dot(sm_ref[...].astype(BF16), wa2_ref[...]) + ba2_ref[...]) / GLA_TAU
        at_ref[...] = jnp.exp(log_a).T

    base = h * GLA_DK
    o = _rank1_update(s_ref, s_out_ref, qt_ref.at[_head_rows(h, GLA_DK)], kt_ref.at[_head_rows(h, GLA_DK)],
                      lambda i: at_ref[pl.ds(base + i, 1), :], vt_ref[_head_rows(h, HD), :], GLA_DK)
    ot_ref[_head_rows(h, HD), :] = o

    @pl.when(h == N_HEADS - 1)
    def _():
        y_ref[...] = jax.nn.silu(x_ref[:, 512:768]) * _head_norm(ot_ref[...].T, g_ref[...])


def _gla_sample(gla, small, state, lw):
    per_head = GLA_DK * HD
    li = lw["layer"]
    full = lambda shape: pl.BlockSpec(shape, lambda h: (0,) * len(shape))
    vm = lambda r: pltpu.VMEM((r, LANES), F32)
    return pl.pallas_call(
        _gla_sample_kernel,
        grid=(N_HEADS,),
        in_specs=[full((LANES, W_GLA)), full((LANES, W_SMALL)), full((128, 128)), full((1, 128)), full((1, BRANCH_W)),
                  pl.BlockSpec((None, per_head, LANES), lambda h: (li, h, 0))],
        out_specs=[full((LANES, BRANCH_W)), pl.BlockSpec((per_head, LANES), lambda h: (h, 0))],
        out_shape=[jax.ShapeDtypeStruct((LANES, BRANCH_W), F32), jax.ShapeDtypeStruct(state.shape[1:], F32)],
        scratch_shapes=[vm(128), vm(128), vm(128), vm(256), vm(256)],
        compiler_params=_cparams(("arbitrary",)),
        name="gla_sample",
    )(gla, small, lw["wa2"], lw["ba2"], lw["gla_g"], state)


def _ml_sample_kernel(x_ref, sc_ref, n_ref, g_ref, c_ref, y_ref, c_out_ref, n_out_ref, m_out_ref,
                      qt_ref, kt_ref, vt_ref, ot_ref, w_ref, kp_ref):
    h = pl.program_id(0)

    @pl.when(h == 0)
    def _():
        x = x_ref[...]
        qt_ref[...] = x[:, 0:256].T
        kt_ref[...] = (x[:, 256:512] * (HD ** -0.5)).T
        vt_ref[...] = x[:, 512:768].T
        sct = sc_ref[...].T
        log_f = jax.nn.log_sigmoid(sct[8:16])
        m_old = sct[16:24]
        m_new = jnp.maximum(log_f + m_old, sct[0:8])
        w_ref[0:8, :] = jnp.exp(log_f + m_old - m_new)
        w_ref[8:16, :] = jnp.exp(sct[0:8] - m_new)
        w_ref[16:24, :] = m_new

    w_carry = w_ref[pl.ds(h, 1), :]
    w_tok = w_ref[pl.ds(8 + h, 1), :]
    m_new = w_ref[pl.ds(16 + h, 1), :]
    kp_ref[...] = kt_ref[_head_rows(h, HD), :] * w_tok
    q_h = qt_ref.at[_head_rows(h, HD)]
    n_new = w_carry * n_ref[_head_rows(h, HD), :] + kp_ref[...]
    num = _rank1_update(c_ref, c_out_ref, q_h, kp_ref, lambda i: w_carry, vt_ref[_head_rows(h, HD), :], HD)
    den = jnp.sum(q_h[...] * n_new, axis=0, keepdims=True)
    ot_ref[_head_rows(h, HD), :] = num / jnp.maximum(jnp.abs(den), jnp.exp(-m_new))
    n_out_ref[_head_rows(h, HD), :] = n_new

    @pl.when(h == N_HEADS - 1)
    def _():
        y_ref[...] = _head_norm(jax.nn.sigmoid(x_ref[:, 768:1024]) * ot_ref[...].T, g_ref[...])
        m_out_ref[...] = jnp.concatenate([w_ref[16:24, :], jnp.zeros((LANES - 8, LANES), F32)], axis=0).T


def _ml_sample(ml, scal, n_state, c_state, lw):
    per_head = HD * HD
    li = lw["layer"]
    full = lambda shape: pl.BlockSpec(shape, lambda h: (0,) * len(shape))
    vm = lambda r: pltpu.VMEM((r, LANES), F32)
    return pl.pallas_call(
        _ml_sample_kernel,
        grid=(N_HEADS,),
        in_specs=[full((LANES, W_ML)), full((LANES, 128)),
                  pl.BlockSpec((None, BRANCH_W, LANES), lambda h: (li, 0, 0)), full((1, BRANCH_W)),
                  pl.BlockSpec((None, per_head, LANES), lambda h: (li, h, 0))],
        out_specs=[full((LANES, BRANCH_W)), pl.BlockSpec((per_head, LANES), lambda h: (h, 0)),
                   full((BRANCH_W, LANES)), full((LANES, 128))],
        out_shape=[jax.ShapeDtypeStruct((LANES, BRANCH_W), F32), jax.ShapeDtypeStruct(c_state.shape[1:], F32),
                   jax.ShapeDtypeStruct((BRANCH_W, LANES), F32), jax.ShapeDtypeStruct((LANES, 128), F32)],
        scratch_shapes=[vm(256), vm(256), vm(256), vm(256), vm(24), vm(HD)],
        compiler_params=_cparams(("arbitrary",)),
        name="ml_sample",
    )(ml, scal, n_state, lw["ml_g"], c_state)


def _ret_sample_kernel(x_ref, cos_ref, sin_ref, gam_ref, g_ref, s_ref, y_ref, s_out_ref,
                       qt_ref, kt_ref, vt_ref, ot_ref, qr_ref, kr_ref):
    h = pl.program_id(0)

    @pl.when(h == 0)
    def _():
        x = x_ref[...]
        qt_ref[...] = x[:, 0:256].T
        kt_ref[...] = (x[:, 256:512] * (HD ** -0.5)).T
        vt_ref[...] = x[:, 512:768].T

    cos, sin = cos_ref[...], sin_ref[...]
    for src, dst in ((qt_ref, qr_ref), (kt_ref, kr_ref)):
        x1 = src[_head_rows(h, 32), :]
        x2 = src[pl.ds(pl.multiple_of(128 + h * 32, 32), 32), :]
        dst[0:32, :] = x1 * cos - x2 * sin
        dst[32:64, :] = x1 * sin + x2 * cos
    gamma = gam_ref[pl.ds(h, 1), :]
    ot_ref[_head_rows(h, HD), :] = _rank1_update(s_ref, s_out_ref, qr_ref, kr_ref, lambda i: gamma,
                                                 vt_ref[_head_rows(h, HD), :], HD)

    @pl.when(h == N_HEADS - 1)
    def _():
        y_ref[...] = jax.nn.silu(x_ref[:, 768:1024]) * _head_norm(ot_ref[...].T, g_ref[...])


def _ret_sample(ret, state, lw):
    per_head = HD * HD
    li = lw["layer"]
    half = HD // 2
    freq = ROPE_BASE ** (-jnp.arange(half, dtype=F32) / half)
    ang = jnp.full((1,), PAST_LEN, F32)[:, None] * freq[None, :]
    cos = jnp.broadcast_to(jnp.cos(ang).reshape(half, 1), (half, LANES))
    sin = jnp.broadcast_to(jnp.sin(ang).reshape(half, 1), (half, LANES))
    log_g = jnp.log1p(-jnp.exp2(-5.0 - jnp.arange(N_HEADS, dtype=F32)))
    gam = jnp.broadcast_to(jnp.concatenate([jnp.exp(log_g), jnp.ones((4,), F32)]).reshape(8, 1), (8, LANES))
    full = lambda shape: pl.BlockSpec(shape, lambda h: (0,) * len(shape))
    vm = lambda r: pltpu.VMEM((r, LANES), F32)
    return pl.pallas_call(
        _ret_sample_kernel,
        grid=(N_HEADS,),
        in_specs=[full((LANES, W_RET)), full((half, LANES)), full((half, LANES)), full((8, LANES)), full((1, BRANCH_W)),
                  pl.BlockSpec((None, per_head, LANES), lambda h: (li, h, 0))],
        out_specs=[full((LANES, BRANCH_W)), pl.BlockSpec((per_head, LANES), lambda h: (h, 0))],
        out_shape=[jax.ShapeDtypeStruct((LANES, BRANCH_W), F32), jax.ShapeDtypeStruct(state.shape[1:], F32)],
        scratch_shapes=[vm(256), vm(256), vm(256), vm(256), vm(HD), vm(HD)],
        compiler_params=_cparams(("arbitrary",)),
        name="ret_sample",
    )(ret, cos, sin, gam, lw["ret_g"], state)


def _swa_sample_kernel(q_ref, kn_ref, vn_ref, sink_ref, kc_ref, vc_ref, o_ref, ko_ref, vo_ref):
    q = q_ref[...]
    kc, vc = kc_ref[...], vc_ref[...]
    kn, vn = kn_ref[...], vn_ref[...]
    scale = HD ** -0.5
    s = _einsum("bhd,bwd->bhw", q.astype(BF16), kc.astype(BF16)) * scale
    s_new = jnp.sum(q * kn, axis=-1, keepdims=True) * scale
    w = lax.broadcasted_iota(jnp.int32, (1, 1, WINDOW), 2)
    s = jnp.where(w > 0, s, -jnp.inf)
    sink = sink_ref[...][:, 0:1]
    mx = jnp.maximum(jnp.maximum(jnp.max(s, axis=-1, keepdims=True), s_new), sink)
    p = jnp.exp(s - mx)
    p_new = jnp.exp(s_new - mx)
    den = jnp.sum(p, axis=-1, keepdims=True) + p_new + jnp.exp(sink - mx)
    o = _einsum("bhw,bwd->bhd", p.astype(BF16), vc.astype(BF16)) + p_new * vn
    o_ref[...] = o / den
    ko_ref[:, 0:WINDOW - 1, :] = kc[:, 1:WINDOW, :]
    ko_ref[:, WINDOW - 1:WINDOW, :] = kn
    vo_ref[:, 0:WINDOW - 1, :] = vc[:, 1:WINDOW, :]
    vo_ref[:, WINDOW - 1:WINDOW, :] = vn


def _swa_sample(swa, cache_k, cache_v, lw, bt=16):
    bsz = swa.shape[0]
    li = lw["layer"]
    cache = pl.BlockSpec((None, bt, WINDOW, 128), lambda i: (li, i, 0, 0))
    lo = (jnp.arange(128) < HD)[None, :]
    qa, qb = swa[:, 0:128], swa[:, 128:256]
    q = jnp.stack([jnp.where(lo, qa, 0.0), jnp.where(lo, qb, 0.0), jnp.where(lo, 0.0, qa), jnp.where(lo, 0.0, qb)], axis=1)
    kn = swa[:, None, 256:384]
    vn = swa[:, None, 384:512]
    sink = jnp.broadcast_to(lw["sinks"][0, 0:N_HEADS].reshape(N_HEADS, 1), (N_HEADS, 128))
    blk = lambda shape: pl.BlockSpec(shape, lambda i: (i,) + (0,) * (len(shape) - 1))
    o, ko, vo = pl.pallas_call(
        _swa_sample_kernel,
        grid=(bsz // bt,),
        in_specs=[blk((bt, N_HEADS, 128)), blk((bt, 1, 128)), blk((bt, 1, 128)),
                  pl.BlockSpec((N_HEADS, 128), lambda i: (0, 0)), cache, cache],
        out_specs=[blk((bt, N_HEADS, 128)), blk((bt, WINDOW, 128)), blk((bt, WINDOW, 128))],
        out_shape=[jax.ShapeDtypeStruct((bsz, N_HEADS, 128), F32), jax.ShapeDtypeStruct(cache_k.shape[1:], F32),
                   jax.ShapeDtypeStruct(cache_v.shape[1:], F32)],
        compiler_params=_cparams(("parallel",)),
        name="swa_sample",
    )(q, kn, vn, sink, cache_k, cache_v)
    y = jnp.concatenate([o[:, 0, 0:64], o[:, 2, 64:128], o[:, 1, 0:64], o[:, 3, 64:128]], axis=1)
    return y, ko, vo


def _ml_scalars(small, m_state):
    z4 = jnp.zeros((small.shape[0], 4), F32)
    return jnp.concatenate([small[:, ML_I_LANE:ML_I_LANE + 4], z4, small[:, ML_F_LANE:ML_F_LANE + 4], z4, m_state, z4,
                            jnp.zeros((small.shape[0], 128 - 24), F32)], axis=1)


def _take_cols(a, pieces, pad):
    parts = [a[..., s:s + n] for s, n in pieces]
    if pad:
        parts.append(jnp.zeros(a.shape[:-1] + (pad,), a.dtype))
    return jnp.concatenate(parts, axis=-1)


def _mix_pieces():
    p = [(0, 768)]
    sq = 784
    p += [(sq, 64), (sq + 128, 64), (sq + 64, 64), (sq + 192, 64)]
    p += [(1040, 256)]
    p += [(1296, 768), (2072, 256)]
    for base in (2328, 2584):
        for half in range(2):
            p += [(base + h * HD + half * 32, 32) for h in range(N_HEADS)]
    p += [(2840, 512)]
    p += [(768, 16), (2064, 8)]
    return p


def _prep_layer(li, w_in, b_in, gla_w_a2, gla_b_a2, gla_norm_g, swa_sinks, ml_norm_g, ret_norm_g, w_branch, w_out,
                ln1_g, ln1_b, w_router_grp, b_router_grp, w_router_exp, b_router_exp, w_exp_gate, w_exp_up,
                w_exp_down, ln2_g, ln2_b):
    pieces = _mix_pieces()
    pad = W_MIX - sum(n for _, n in pieces)
    row = lambda v: v.reshape(1, -1).astype(F32)
    wr = jnp.concatenate([w_router_grp[li], w_router_exp[li],
                          jnp.zeros((D_MODEL, 128 - N_GROUPS - N_EXPERTS), F32)], axis=1)
    wr_hi = wr.astype(BF16)
    wb = w_branch[li]
    wb1 = jnp.concatenate([wb[1, 0:64], wb[1, 128:192], wb[1, 64:128], wb[1, 192:256]], axis=0)
    return dict(
        wm=_take_cols(w_in[li], pieces, pad).astype(BF16),
        bm=row(_take_cols(b_in[li], pieces, pad)),
        wg=w_in[li][:, GATE_COL:].astype(BF16),
        bg=row(b_in[li][GATE_COL:]),
        wa2=jnp.concatenate([gla_w_a2[li], jnp.zeros((128 - GLA_RANK, 128), F32)], axis=0).astype(BF16),
        ba2=row(gla_b_a2[li]),
        gla_g=row(gla_norm_g[li]), ml_g=row(ml_norm_g[li]), ret_g=row(ret_norm_g[li]),
        sinks=row(jnp.concatenate([swa_sinks[li], jnp.zeros((128 - N_HEADS,), F32)])),
        wbr=jnp.stack([wb[0], wb1, wb[2], wb[3]]).astype(BF16),
        wout=w_out[li].astype(BF16),
        ln1_g=row(ln1_g[li]), ln1_b=row(ln1_b[li]), ln2_g=row(ln2_g[li]), ln2_b=row(ln2_b[li]),
        wr_hi=wr_hi, wr_lo=(wr - wr_hi.astype(F32)).astype(BF16),
        br=row(jnp.concatenate([b_router_grp[li], b_router_exp[li], jnp.zeros((128 - N_GROUPS - N_EXPERTS,), F32)])),
        layer=li, we_gate=w_exp_gate, we_up=w_exp_up, we_down=w_exp_down,
    )


TOK_TILE = 256
MOE_PIECES_PROMPT = 32
MOE_PIECES_SAMPLE = 1


MERGE_TILES = 2


def _ffn_half(x2, ys, lw, tm, bp):
    x1, route, cnt, xs = _merge(x2, ys, lw, tm, MERGE_TILES if x2.shape[0] // tm % MERGE_TILES == 0 else 1)
    return _moe(x1, route, cnt, xs, lw, bp, tm)


def _layer_prompt(x2, lw, bsz, seq):
    gla, swa, ml, ret, small = _inproj(x2, lw["wm"], lw["bm"], 2 * TOK_TILE)
    ya, s_gla = _gla_prompt(gla, small, lw, bsz, seq)
    yb = _swa_prompt(swa, lw, bsz, seq)
    yc, c_bd, n_row, m_row = _ml_prompt(ml, small, lw, bsz, seq)
    yd, s_ret = _ret_prompt(ret, lw, bsz, seq)
    kv_tail = swa.reshape(bsz, seq, W_SWA)[:, seq - WINDOW:, 256:512]
    states = (_gla_state_from_blockdiag(s_gla),
              kv_tail[:, :, 0:128].reshape(bsz, WINDOW, SWA_KV, HD),
              kv_tail[:, :, 128:256].reshape(bsz, WINDOW, SWA_KV, HD),
              jnp.swapaxes(_diag_blocks(c_bd), -1, -2), n_row.reshape(bsz, N_HEADS, HD), m_row[:, 0, 0:N_HEADS],
              _ret_state_from_blockdiag(s_ret))
    return _ffn_half(x2, (ya, yb, yc, yd), lw, TOK_TILE, MOE_PIECES_PROMPT), states


def _layer_sample(x2, st, lw):
    s_gla, k_buf, v_buf, c_ml, n_ml, m_ml, s_ret = st
    bsz = x2.shape[0]
    depth = s_gla.shape[0]
    gla, swa, ml, ret, small = _inproj(x2, lw["wm"], lw["bm"], bsz)
    to_lanes = lambda s: jnp.moveaxis(s, 1, -1).reshape(depth, -1, bsz)
    from_lanes = lambda s, like: jnp.moveaxis(s.reshape(like.shape[2:] + (bsz,)), -1, 0)
    as_rows = lambda c: c.reshape(depth, bsz, WINDOW, SWA_KV * HD)
    ya, s_gla_new = _gla_sample(gla, small, to_lanes(s_gla), lw)
    yb, k_new, v_new = _swa_sample(swa, as_rows(k_buf), as_rows(v_buf), lw)
    yc, c_new, n_new, m_new = _ml_sample(ml, _ml_scalars(small, m_ml[lw["layer"]]), to_lanes(n_ml), to_lanes(c_ml), lw)
    yd, s_ret_new = _ret_sample(ret, to_lanes(s_ret), lw)
    states = (from_lanes(s_gla_new, s_gla), k_new.reshape(k_buf.shape[1:]), v_new.reshape(v_buf.shape[1:]),
              from_lanes(c_new, c_ml), from_lanes(n_new, n_ml), m_new[:, 0:N_HEADS], from_lanes(s_ret_new, s_ret))
    return _ffn_half(x2, (ya, yb, yc, yd), lw, bsz, MOE_PIECES_SAMPLE), states


def kernel(x_prompt, x_sample, state_gla, cache_swa_k, cache_swa_v, state_mlstm_c, state_mlstm_n, state_mlstm_m,
           state_ret, w_in, b_in, gla_w_a2, gla_b_a2, gla_norm_g, swa_sinks, ml_norm_g, ret_norm_g, w_branch, w_out,
           ln1_g, ln1_b, w_router_grp, b_router_grp, w_router_exp, b_router_exp, w_exp_gate, w_exp_up, w_exp_down,
           ln2_g, ln2_b):
    bsz, seq, _ = x_prompt.shape
    dec_b, dec_seq, _ = x_sample.shape
    assert dec_seq == 1 and dec_b == LANES and seq % GLA_TILE == 0 and (bsz * seq) % TOK_TILE == 0
    weights = (w_in, b_in, gla_w_a2, gla_b_a2, gla_norm_g, swa_sinks, ml_norm_g, ret_norm_g, w_branch, w_out,
               ln1_g, ln1_b, w_router_grp, b_router_grp, w_router_exp, b_router_exp, w_exp_gate, w_exp_up,
               w_exp_down, ln2_g, ln2_b)
    y_p = x_prompt.reshape(bsz * seq, D_MODEL)
    y_s = x_sample.reshape(dec_b, D_MODEL)
    new_p, new_s = [], []
    for li in range(w_in.shape[0]):
        lw = _prep_layer(li, *weights)
        y_p, st_p = _layer_prompt(y_p, lw, bsz, seq)
        st_in = (state_gla, cache_swa_k, cache_swa_v, state_mlstm_c, state_mlstm_n, state_mlstm_m, state_ret)
        y_s, st_s = _layer_sample(y_s, st_in, lw)
        new_p.append(st_p)
        new_s.append(st_s)
    outs = [y_p.reshape(bsz, seq, D_MODEL), y_s.reshape(dec_b, dec_seq, D_MODEL)]
    for i in range(7):
        outs.append(jnp.stack([s[i] for s in new_p], axis=0))
        outs.append(jnp.stack([s[i] for s in new_s], axis=0))
    return tuple(outs)
```

```python
import functools
import math

import jax
import jax.numpy as jnp
from jax import lax
from jax.experimental import pallas as pl
from jax.experimental.pallas import tpu as pltpu

F32 = jnp.float32
BF16 = jnp.bfloat16

D_MODEL = 1024
N_BRANCH = 4
BRANCH_W = D_MODEL // N_BRANCH
N_HEADS = 4
GLA_DK = 32
GLA_RANK = 16
GLA_TAU = 16.0
HD = 64
SWA_KV = 2
WINDOW = 128
ROPE_BASE = 10000.0
PAST_LEN = 16384
N_GROUPS = 4
EPG = 8
N_EXPERTS = N_GROUPS * EPG
D_EXPERT = D_MODEL // 4
DEPTH = 2
ALPHA = (2 * DEPTH) ** 0.25
LN_EPS = 1e-5
GATE_COL = 3352
VMEM_LIMIT = 56 * 1024 * 1024

W_GLA, W_SWA, W_ML, W_RET, W_SMALL = 768, 512, 1024, 1024, 128
W_MIX = W_GLA + W_SWA + W_ML + W_RET + W_SMALL


def _cparams(sem):
    return pltpu.CompilerParams(dimension_semantics=sem, vmem_limit_bytes=VMEM_LIMIT)


def _dot(a, b):
    return jnp.dot(a, b, preferred_element_type=F32)


def _einsum(spec, a, b):
    return jnp.einsum(spec, a, b, preferred_element_type=F32)


def _split2(x):
    hi = x.astype(BF16)
    lo = (x - hi.astype(F32)).astype(BF16)
    return hi, lo


def _split3(x):
    hi = x.astype(BF16)
    r = x - hi.astype(F32)
    mid = r.astype(BF16)
    lo = (r - mid.astype(F32)).astype(BF16)
    return hi, mid, lo


def _dot_exact_lhs(a_exact, b):
    return sum(_dot(a_exact, p) for p in _split3(b))


def _dot_exact_rhs(a, b_exact):
    return sum(_dot(p, b_exact) for p in _split3(a))


def _layer_norm(x, g, b):
    mu = jnp.mean(x, axis=-1, keepdims=True)
    xc = x - mu
    var = jnp.mean(xc * xc, axis=-1, keepdims=True)
    return xc * lax.rsqrt(var + LN_EPS) * g + b


def _inproj_kernel(x_ref, w_ref, b_ref, gla_ref, swa_ref, ml_ref, ret_ref, sm_ref):
    xb = x_ref[...].astype(BF16)
    off = 0
    for ref, width in ((gla_ref, W_GLA), (swa_ref, W_SWA), (ml_ref, W_ML), (ret_ref, W_RET), (sm_ref, W_SMALL)):
        ref[...] = _dot_nt(xb, w_ref[off:off + width, :]) + b_ref[:, off:off + width]
        off += width


def _inproj(x2, wm, bm, tm):
    t = x2.shape[0]
    widths = (W_GLA, W_SWA, W_ML, W_RET, W_SMALL)
    return pl.pallas_call(
        _inproj_kernel,
        grid=(t // tm,),
        in_specs=[pl.BlockSpec((tm, D_MODEL), lambda i: (i, 0)),
                  pl.BlockSpec((W_MIX, D_MODEL), lambda i: (0, 0)),
                  pl.BlockSpec((1, W_MIX), lambda i: (0, 0))],
        out_specs=[pl.BlockSpec((tm, w), lambda i: (i, 0)) for w in widths],
        out_shape=[jax.ShapeDtypeStruct((t, w), F32) for w in widths],
        compiler_params=_cparams(("parallel",)),
        name="inproj",
    )(x2, wm, bm)


def _merge_kernel(x_ref, ya_ref, yb_ref, yc_ref, yd_ref, wg_ref, bg_ref, wbr_ref, wout_ref, g1_ref, b1_ref,
                  wrh_ref, wrl_ref, br_ref, x1_ref, route_ref, cnt_ref, xs_ref, *, tm, sub):
    rows_t = xs_ref.shape[0] // sub
    x1, picks = [], []
    for t in range(sub):
        rows = slice(t * tm, (t + 1) * tm)
        x = x_ref[rows, :]
        xb = x.astype(BF16)
        merged = jnp.zeros((tm, D_MODEL), F32)
        for n, y_ref in enumerate((ya_ref, yb_ref, yc_ref, yd_ref)):
            cols = slice(n * D_MODEL, (n + 1) * D_MODEL)
            gate = 0.5 * jnp.tanh(0.5 * (_dot_nt(xb, wg_ref[cols, :]) + bg_ref[:, cols])) + 0.5
            merged = merged + gate * _dot(y_ref[rows, :].astype(BF16), wbr_ref[n])
        mix = _dot(merged.astype(BF16), wout_ref[...])
        x1.append(_layer_norm(ALPHA * x + mix, g1_ref[...], b1_ref[...]))
        x1_ref[rows, :] = x1[t]
        picks.append(_route(x1[t], wrh_ref[...], wrl_ref[...], br_ref[...], tm))
    for t in range(sub):
        route, cnt, xs = _sort(x1[t], picks[t], tm, rows_t)
        route_ref[t * tm:(t + 1) * tm, :] = route
        cnt_ref[t] = cnt
        xs_ref[t * rows_t:(t + 1) * rows_t, :] = xs


def _route(x1, wr_hi, wr_lo, br, tm):
    xh, xl = _split2(x1)
    logits = _dot(xh, wr_hi) + _dot(xh, wr_lo) + _dot(xl, wr_hi) + br
    lane = lax.broadcasted_iota(jnp.int32, (tm, 128), 1)
    lanef = lane.astype(F32)
    neg = jnp.float32(-jnp.inf)
    big = jnp.float32(1e9)
    isg = lane < N_GROUPS
    gmax = jnp.max(jnp.where(isg, logits, neg), axis=-1, keepdims=True)
    gidx = jnp.min(jnp.where(isg & (logits == gmax), lanef, big), axis=-1, keepdims=True)
    p_group = 1.0 / jnp.sum(jnp.where(isg, jnp.exp(logits - gmax), 0.0), axis=-1, keepdims=True)
    lo = N_GROUPS + EPG * gidx
    em = (lanef >= lo) & (lanef < lo + EPG)
    v1 = jnp.max(jnp.where(em, logits, neg), axis=-1, keepdims=True)
    i1 = jnp.min(jnp.where(em & (logits == v1), lanef, big), axis=-1, keepdims=True)
    em2 = em & (lanef != i1)
    v2 = jnp.max(jnp.where(em2, logits, neg), axis=-1, keepdims=True)
    i2 = jnp.min(jnp.where(em2 & (logits == v2), lanef, big), axis=-1, keepdims=True)
    e21 = jnp.exp(v2 - v1)
    gt1 = 1.0 / (1.0 + e21)
    return i1, i2, p_group * gt1, p_group * (e21 * gt1)


def _sort(x1, picks, tm, rows_t):
    i1, i2, gate1, gate2 = picks
    lane = lax.broadcasted_iota(jnp.int32, (tm, 128), 1)
    lanef = lane.astype(F32)
    sel1 = lanef == i1
    sel2 = lanef == i2
    onehot = jnp.where(sel1 | sel2, 1.0, 0.0)
    cnt = jnp.sum(onehot, axis=0, keepdims=True)
    seg = jnp.floor((cnt + (PIECE - 1)) * (1.0 / PIECE)) * PIECE
    er = lax.broadcasted_iota(jnp.int32, (128, 128), 0)
    ec = lax.broadcasted_iota(jnp.int32, (128, 128), 1)
    seg_off = _dot(jnp.broadcast_to(seg, (8, 128)).astype(BF16), jnp.where(er < ec, 1.0, 0.0).astype(BF16))[0:1]
    row = lax.broadcasted_iota(jnp.int32, (tm, tm), 0)
    col = lax.broadcasted_iota(jnp.int32, (tm, tm), 1)
    earlier = _dot(jnp.where(col < row, 1.0, 0.0).astype(BF16), onehot.astype(BF16))
    where_to = seg_off + earlier
    w1 = jnp.where(sel1, where_to, 0.0)
    w2 = jnp.where(sel2, where_to, 0.0)
    pos1 = jnp.sum(w1, axis=-1, keepdims=True)
    pos2 = jnp.sum(w2, axis=-1, keepdims=True)
    ones = jnp.ones((8, 128), BF16)
    prow1 = sum(_dot_nt(ones, p) for p in _split2(w1))[0:1]
    prow2 = sum(_dot_nt(ones, p) for p in _split2(w2))[0:1]
    r = lax.broadcasted_iota(jnp.int32, (rows_t, tm), 0).astype(F32)
    place = jnp.where((r == prow1) | (r == prow2), 1.0, 0.0).astype(BF16)
    xs = _dot(place, x1.astype(BF16)).astype(BF16)

    vals = (i1 - N_GROUPS, i2 - N_GROUPS, gate1, gate2, pos1, pos2)
    route = jnp.zeros((tm, 128), F32)
    for k, v in enumerate(vals):
        route = jnp.where(lane == k, v, route)
    return route, cnt, xs


PIECE = 16


def _tile_rows(tm):
    return -(-(2 * tm + N_EXPERTS * (PIECE - 1)) // 128) * 128


def _merge(x2, ys, lw, tm, sub):
    t = x2.shape[0]
    nt = t // tm
    rows_t = _tile_rows(tm)
    tok = lambda w: pl.BlockSpec((sub * tm, w), lambda i: (i, 0))
    full = lambda shape: pl.BlockSpec(shape, lambda i: (0,) * len(shape), pipeline_mode=pl.Buffered(1))
    return pl.pallas_call(
        functools.partial(_merge_kernel, tm=tm, sub=sub),
        grid=(nt // sub,),
        in_specs=[tok(D_MODEL), tok(BRANCH_W), tok(BRANCH_W), tok(BRANCH_W), tok(BRANCH_W),
                  full((N_BRANCH * D_MODEL, D_MODEL)), full((1, N_BRANCH * D_MODEL)),
                  full((N_BRANCH, BRANCH_W, D_MODEL)), full((D_MODEL, D_MODEL)),
                  full((1, D_MODEL)), full((1, D_MODEL)),
                  full((D_MODEL, 128)), full((D_MODEL, 128)), full((1, 128))],
        out_specs=[tok(D_MODEL), tok(128), pl.BlockSpec((sub, 1, 128), lambda i: (i, 0, 0)),
                   pl.BlockSpec((sub * rows_t, D_MODEL), lambda i: (i, 0))],
        out_shape=[jax.ShapeDtypeStruct((t, D_MODEL), F32), jax.ShapeDtypeStruct((t, 128), F32),
                   jax.ShapeDtypeStruct((nt, 1, 128), F32), jax.ShapeDtypeStruct((nt * rows_t, D_MODEL), BF16)],
        compiler_params=_cparams(("parallel",)),
        name="merge",
    )(x2, *ys, lw["wg"], lw["bg"], lw["wbr"], lw["wout"], lw["ln1_g"], lw["ln1_b"], lw["wr_hi"], lw["wr_lo"], lw["br"])


def _piece_copy(hbm, hbm_row, vmem, piece, sem, to_vmem):
    h = hbm.at[pl.ds(pl.multiple_of(hbm_row, PIECE), PIECE), :]
    v = vmem.at[pl.ds(pl.multiple_of(piece * PIECE, PIECE), PIECE), :]
    return pltpu.make_async_copy(h, v, sem) if to_vmem else pltpu.make_async_copy(v, h, sem)


def _expert_kernel(be_ref, nu_ref, nv_ref, off_ref, xs_hbm, wg_ref, wu_ref, wd_ref, ys_hbm, xin, yout, sem_in, sem_out,
                   *, bp):
    del xs_hbm
    blk = pl.program_id(0)
    n_used = nu_ref[0]
    slot = blk % 2

    def pieces(b, s, buf, sem, to_vmem, start):
        n = nv_ref[b]

        def one(p, priority):
            cp = _piece_copy(ys_hbm, off_ref[b * bp + p], buf.at[s], p, sem.at[s], to_vmem)
            cp.start(priority=priority) if start else cp.wait()

        def body(i, c):
            one(2 * i, 0)

            @pl.when(2 * i + 1 < n)
            def _():
                one(2 * i + 1, 1)
            return c
        lax.fori_loop(0, (n + 1) // 2, body, 0)

    @pl.when(blk == 0)
    def _():
        xin[...] = jnp.zeros_like(xin)

        @pl.when(n_used > 0)
        def _():
            pieces(0, 0, xin, sem_in, True, True)

    @pl.when(blk + 1 < n_used)
    def _():
        pieces(blk + 1, 1 - slot, xin, sem_in, True, True)

    @pl.when(blk < n_used)
    def _():
        pieces(blk, slot, xin, sem_in, True, False)

        @pl.when(blk >= 2)
        def _():
            pieces(blk - 2, slot, yout, sem_out, False, False)

        xb = xin[slot]
        hid = jax.nn.silu(_dot(xb, wg_ref[...].astype(BF16))) * _dot(xb, wu_ref[...].astype(BF16))
        yout[slot] = _dot(hid.astype(BF16), wd_ref[...].astype(BF16)).astype(BF16)
        pieces(blk, slot, yout, sem_out, False, True)

    @pl.when(blk == n_used - 1)
    def _():
        @pl.when(blk >= 1)
        def _():
            pieces(blk - 1, 1 - slot, yout, sem_out, False, False)
        pieces(blk, slot, yout, sem_out, False, False)


def _experts(xs, block_expert, n_used, n_valid, piece_off, lw, bp):
    n_blocks = block_expert.shape[0]
    rows = bp * PIECE
    li = lw["layer"]
    wspec = lambda shape: pl.BlockSpec((None, None) + shape, lambda b, be, nu, nv, off: (li, be[b], 0, 0))
    grid_spec = pltpu.PrefetchScalarGridSpec(
        num_scalar_prefetch=4,
        grid=(n_blocks,),
        in_specs=[pl.BlockSpec(memory_space=pl.ANY), wspec((D_MODEL, D_EXPERT)), wspec((D_MODEL, D_EXPERT)),
                  wspec((D_EXPERT, D_MODEL))],
        out_specs=pl.BlockSpec(memory_space=pl.ANY),
        scratch_shapes=[pltpu.VMEM((2, rows, D_MODEL), BF16), pltpu.VMEM((2, rows, D_MODEL), BF16),
                        pltpu.SemaphoreType.DMA((2,)), pltpu.SemaphoreType.DMA((2,))],
    )
    return pl.pallas_call(
        functools.partial(_expert_kernel, bp=bp),
        grid_spec=grid_spec,
        out_shape=jax.ShapeDtypeStruct(xs.shape, BF16),
        input_output_aliases={4: 0},
        compiler_params=_cparams(("arbitrary",)),
        name="experts",
    )(block_expert, n_used, n_valid, piece_off, xs, lw["we_gate"], lw["we_up"], lw["we_down"])


def _combine_kernel(ys_ref, x1_ref, route_ref, g2_ref, b2_ref, out_ref, *, tm):
    route = route_ref[...]
    ys = ys_ref[...]
    r = lax.broadcasted_iota(jnp.int32, (tm, ys.shape[0]), 1).astype(F32)
    ffn = jnp.zeros((tm, D_MODEL), F32)
    for j in range(2):
        pick = jnp.where(r == route[:, 4 + j:5 + j], 1.0, 0.0).astype(BF16)
        ffn = ffn + route[:, 2 + j:3 + j] * _dot(pick, ys)
    out_ref[...] = _layer_norm(ALPHA * x1_ref[...] + ffn, g2_ref[...], b2_ref[...])


def _combine(ys, x1, route, lw, tm):
    t = x1.shape[0]
    rows_t = _tile_rows(tm)
    full = lambda shape: pl.BlockSpec(shape, lambda i: (0,) * len(shape))
    return pl.pallas_call(
        functools.partial(_combine_kernel, tm=tm),
        grid=(t // tm,),
        in_specs=[pl.BlockSpec((rows_t, D_MODEL), lambda i: (i, 0)), pl.BlockSpec((tm, D_MODEL), lambda i: (i, 0)),
                  pl.BlockSpec((tm, 128), lambda i: (i, 0)), full((1, D_MODEL)), full((1, D_MODEL))],
        out_specs=pl.BlockSpec((tm, D_MODEL), lambda i: (i, 0)),
        out_shape=jax.ShapeDtypeStruct((t, D_MODEL), F32),
        compiler_params=_cparams(("parallel",)),
        name="combine",
    )(ys, x1, route, lw["ln2_g"], lw["ln2_b"])


def _piece_tables(cnt, tm, bp):
    nt = cnt.shape[0]
    rows_t = _tile_rows(tm)
    counts = cnt[:, 0, N_GROUPS:N_GROUPS + N_EXPERTS].astype(jnp.int32)
    pcs = ((counts + PIECE - 1) // PIECE).T
    seg_row = ((jnp.cumsum(pcs, axis=0) - pcs) * PIECE + jnp.arange(nt, dtype=jnp.int32)[None, :] * rows_t)
    tile_end = jnp.cumsum(pcs, axis=1)
    total = tile_end[:, -1]
    e_end = jnp.cumsum((total + bp - 1) // bp * bp)
    e_start = jnp.concatenate([jnp.zeros((1,), jnp.int32), e_end[:-1]])
    n_blocks = -(-(nt * rows_t // PIECE + N_EXPERTS * (bp - 1)) // bp)
    g = jnp.arange(n_blocks * bp, dtype=jnp.int32)[:, None]
    in_e = ((e_start[None, :] <= g) & (g < e_end[None, :])).astype(F32)
    pick = lambda table: jnp.dot(in_e, table.astype(F32), precision=lax.Precision.HIGHEST)
    local = g.astype(F32) - pick(e_start[:, None])
    ends, starts, rows = pick(tile_end), pick(tile_end - pcs), pick(seg_row)
    in_t = ((starts <= local) & (local < ends)).astype(F32)
    off = jnp.sum(in_t * (rows + (local - starts) * PIECE), axis=1).astype(jnp.int32)
    n_valid = jnp.sum(jnp.sum(in_t, axis=1).reshape(n_blocks, bp), axis=1).astype(jnp.int32)
    expert = pick(jnp.arange(N_EXPERTS, dtype=jnp.int32)[:, None])[:, 0]
    block_expert = expert.reshape(n_blocks, bp)[:, 0].astype(jnp.int32)
    n_used = (e_end[-1] // bp).astype(jnp.int32).reshape(1)
    return block_expert, n_used, n_valid, off


def _moe(x1, route, cnt, xs, lw, bp, tm):
    ys = _experts(xs, *_piece_tables(cnt, tm, bp), lw, bp)
    return _combine(ys, x1, route, lw, tm)


CHUNK = 128
STEP_CHUNKS = 4


def _dot_nt(a, b):
    return lax.dot_general(a, b, (((1,), (1,)), ((), ())), preferred_element_type=F32)


def _dot_tn(a, b):
    return lax.dot_general(a, b, (((0,), (0,)), ((), ())), preferred_element_type=F32)


def _lane_head(width, group):
    return lax.broadcasted_iota(jnp.int32, (1, width), 1) // group


def _block_diag_mask(rows_head, cols_head):
    return rows_head == cols_head


def _head_norm(o, gain):
    r = lax.broadcasted_iota(jnp.int32, (BRANCH_W, BRANCH_W), 0) // HD
    c = lax.broadcasted_iota(jnp.int32, (BRANCH_W, BRANCH_W), 1) // HD
    avg = jnp.where(r == c, 1.0 / HD, 0.0).astype(BF16)
    mu = _dot_exact_rhs(o, avg)
    xc = o - mu
    var = _dot_exact_rhs(xc * xc, avg)
    return xc * lax.rsqrt(var + LN_EPS) * gain


def _stack_heads(x, head_of_lane):
    return jnp.concatenate([jnp.where(head_of_lane == h, x, 0.0) for h in range(N_HEADS)], axis=0)


def _pick_heads(stacked, rows, head_of_lane):
    out = jnp.where(head_of_lane == 0, stacked[0:rows], 0.0)
    for h in range(1, N_HEADS):
        out = jnp.where(head_of_lane == h, stacked[h * rows:(h + 1) * rows], out)
    return out


def _spread_heads(cols, head_of_lane):
    out = jnp.where(head_of_lane == 0, cols[0], 0.0)
    for h in range(1, N_HEADS):
        out = jnp.where(head_of_lane == h, cols[h], out)
    return out


SWA_BLOCKS = 4


def _swa_prompt_kernel(q_ref, kv_ref, kvp_ref, sink_ref, y_ref):
    n = pl.program_id(1)
    lo = lax.broadcasted_iota(jnp.int32, (1, 128), 1) < HD
    e = lax.broadcasted_iota(jnp.int32, (2 * WINDOW, N_HEADS * WINDOW), 0)
    col = lax.broadcasted_iota(jnp.int32, (2 * WINDOW, N_HEADS * WINDOW), 1)
    a = col % WINDOW
    band = (e > a) & (e <= a + WINDOW)
    head = lax.broadcasted_iota(jnp.int32, (1, N_HEADS * WINDOW), 1) // WINDOW
    sink = _spread_heads([sink_ref[:, h:h + 1] for h in range(N_HEADS)], head)
    blocks = range(SWA_BLOCKS)
    scores, vexts = [], []
    for i in blocks:
        rows = slice(i * WINDOW, (i + 1) * WINDOW)
        q = q_ref[rows, :]
        qa, qb = q[:, 0:128], q[:, 128:256]
        kv = kv_ref[rows, :]
        kvp = kvp_ref[...] if i == 0 else kv_ref[(i - 1) * WINDOW:i * WINDOW, :]
        kext = jnp.concatenate([kvp[:, 0:128], kv[:, 0:128]], axis=0).astype(BF16)
        vexts.append(jnp.concatenate([kvp[:, 128:256], kv[:, 128:256]], axis=0).astype(BF16))
        qs = jnp.concatenate([jnp.where(lo, qa, 0.0), jnp.where(lo, qb, 0.0),
                              jnp.where(lo, 0.0, qa), jnp.where(lo, 0.0, qb)], axis=0).astype(BF16)
        scores.append(_dot_nt(kext, qs))
    probs, inv_den = [], []
    for i in blocks:
        valid = band & ((n > 0) | (e >= WINDOW)) if i == 0 else band
        s = jnp.where(valid, scores[i] * (HD ** -0.5), -jnp.inf)
        mx = jnp.maximum(jnp.max(s, axis=0, keepdims=True), sink)
        p = jnp.exp(s - mx)
        inv_den.append(1.0 / (jnp.sum(p, axis=0, keepdims=True) + jnp.exp(sink - mx)))
        probs.append(p.astype(BF16))
    outs = [_dot_tn(vexts[i], probs[i]) for i in blocks]
    for i in blocks:
        o = outs[i] * inv_den[i]
        ya = jnp.concatenate([o[0:HD, 0:128], o[HD:128, 256:384]], axis=0).T
        yb = jnp.concatenate([o[0:HD, 128:256], o[HD:128, 384:512]], axis=0).T
        y_ref[i * WINDOW:(i + 1) * WINDOW, :] = jnp.concatenate([ya, yb], axis=1)


def _swa_prompt(swa, lw, bsz, seq):
    qt = SWA_BLOCKS * WINDOW
    nt = seq // qt
    return pl.pallas_call(
        _swa_prompt_kernel,
        grid=(bsz, nt),
        in_specs=[pl.BlockSpec((qt, 256), lambda b, n: (b * nt + n, 0)),
                  pl.BlockSpec((qt, 256), lambda b, n: (b * nt + n, 1)),
                  pl.BlockSpec((WINDOW, 256), lambda b, n: (jnp.maximum((b * nt + n) * SWA_BLOCKS - 1, 0), 1)),
                  pl.BlockSpec((1, 128), lambda b, n: (0, 0))],
        out_specs=pl.BlockSpec((qt, BRANCH_W), lambda b, n: (b * nt + n, 0)),
        out_shape=jax.ShapeDtypeStruct((bsz * seq, BRANCH_W), F32),
        compiler_params=_cparams(("parallel", "parallel")),
        name="swa_prompt",
    )(swa, swa, swa, lw["sinks"])


def _rotate(x, cos, sin):
    x1, x2 = x[:, 0:128], x[:, 128:256]
    return jnp.concatenate([x1 * cos - x2 * sin, x1 * sin + x2 * cos], axis=1)


def _ret_prompt_kernel(r_ref, cos_ref, sin_ref, dmat_ref, qdec_ref, kdec_ref, cdec_ref, g_ref, y_ref, s_out_ref, s_ref):
    n = pl.program_id(1)

    @pl.when(n == 0)
    def _():
        s_ref[...] = jnp.zeros_like(s_ref)

    c = CHUNK
    chunks = range(STEP_CHUNKS)
    hq = _lane_head(BRANCH_W, 32) % N_HEADS
    hv = _lane_head(BRANCH_W, HD)
    rows_head = (lax.broadcasted_iota(jnp.int32, (BRANCH_W, 1), 0) // 32) % N_HEADS
    qr, kr, v = [], [], []
    for i in chunks:
        rows = slice(i * c, (i + 1) * c)
        cos, sin = cos_ref[rows, :], sin_ref[rows, :]
        qr.append(_rotate(r_ref[rows, 0:256], cos, sin))
        kr.append(_rotate(r_ref[rows, 256:512], cos, sin) * (HD ** -0.5))
        v.append(r_ref[rows, 512:768].astype(BF16))
    att = [_dot_nt(_stack_heads(qr[i], hq).astype(BF16), kr[i].astype(BF16)) * dmat_ref[...] for i in chunks]
    upd = [_dot_tn((kr[i] * kdec_ref[...]).astype(BF16), v[i]) for i in chunks]
    states = [s_ref[...]]
    for i in chunks:
        states.append(cdec_ref[...] * states[i] + jnp.where(rows_head == hv, upd[i], 0.0))
    o = [_pick_heads(_dot(att[i].astype(BF16), v[i]), c, hv) for i in chunks]
    o = [o[i] + _dot((qr[i] * qdec_ref[...]).astype(BF16), states[i].astype(BF16)) for i in chunks]
    s_ref[...] = states[-1]
    s_out_ref[...] = states[-1]
    y_ref[...] = jax.nn.silu(r_ref[:, 768:1024]) * _head_norm(jnp.concatenate(o, axis=0), g_ref[...])


def _ret_tables(seq):
    c = CHUNK
    half = HD // 2
    freq = ROPE_BASE ** (-jnp.arange(half, dtype=F32) / half)
    ang = jnp.arange(seq).astype(F32)[:, None] * freq[None, :]
    cos = jnp.tile(jnp.cos(ang), (1, N_HEADS))
    sin = jnp.tile(jnp.sin(ang), (1, N_HEADS))
    log_g = jnp.log1p(-jnp.exp2(-5.0 - jnp.arange(N_HEADS, dtype=F32)))
    idx = jnp.arange(c, dtype=F32)
    rel = idx[:, None] - idx[None, :]
    dmat = jnp.exp(jnp.where(rel[None] >= 0, rel[None] * log_g[:, None, None], -jnp.inf)).reshape(N_HEADS * c, c)
    lane_head_q = (jnp.arange(BRANCH_W) // 32) % N_HEADS
    qdec = jnp.exp((idx + 1.0)[:, None] * log_g[lane_head_q][None, :])
    kdec = jnp.exp((c - 1.0 - idx)[:, None] * log_g[lane_head_q][None, :])
    cdec = jnp.exp(c * log_g)[jnp.arange(BRANCH_W) // HD][None, :]
    return cos, sin, dmat, qdec, kdec, cdec


def _ret_prompt(ret, lw, bsz, seq):
    c = CHUNK
    rows = STEP_CHUNKS * c
    nc = seq // rows
    cos, sin, dmat, qdec, kdec, cdec = _ret_tables(seq)
    full = lambda shape: pl.BlockSpec(shape, lambda b, n: (0,) * len(shape))
    return pl.pallas_call(
        _ret_prompt_kernel,
        grid=(bsz, nc),
        in_specs=[pl.BlockSpec((rows, W_RET), lambda b, n: (b * nc + n, 0)),
                  pl.BlockSpec((rows, 128), lambda b, n: (n, 0)), pl.BlockSpec((rows, 128), lambda b, n: (n, 0)),
                  full((N_HEADS * c, c)), full((c, BRANCH_W)), full((c, BRANCH_W)), full((1, BRANCH_W)),
                  full((1, BRANCH_W))],
        out_specs=[pl.BlockSpec((rows, BRANCH_W), lambda b, n: (b * nc + n, 0)),
                   pl.BlockSpec((None, BRANCH_W, BRANCH_W), lambda b, n: (b, 0, 0))],
        out_shape=[jax.ShapeDtypeStruct((bsz * seq, BRANCH_W), F32),
                   jax.ShapeDtypeStruct((bsz, BRANCH_W, BRANCH_W), F32)],
        scratch_shapes=[pltpu.VMEM((BRANCH_W, BRANCH_W), F32)],
        compiler_params=_cparams(("parallel", "arbitrary")),
        name="ret_prompt",
    )(ret, cos, sin, dmat, qdec, kdec, cdec, lw["ret_g"])


def _ret_state_from_blockdiag(s_bd):
    b = s_bd.shape[0]
    s = s_bd.reshape(b, 2, N_HEADS, 32, N_HEADS, HD)
    s = jnp.stack([s[:, :, h, :, h, :] for h in range(N_HEADS)], axis=1)
    return s.reshape(b, N_HEADS, HD, HD)


ML_I_LANE = GLA_RANK
ML_F_LANE = GLA_RANK + N_HEADS


def _ml_prompt_kernel(m_ref, sm_ref, g_ref, y_ref, c_out_ref, n_out_ref, m_out_ref, c_ref, n_ref, mm_ref):
    nchunk = pl.program_id(1)

    @pl.when(nchunk == 0)
    def _():
        c_ref[...] = jnp.zeros_like(c_ref)
        n_ref[...] = jnp.zeros_like(n_ref)
        mm_ref[...] = jnp.zeros_like(mm_ref)

    c = CHUNK
    chunks = range(STEP_CHUNKS)
    row = lax.broadcasted_iota(jnp.int32, (c, c), 0)
    col = lax.broadcasted_iota(jnp.int32, (c, c), 1)
    tri_lo = jnp.where(col <= row, 1.0, 0.0).astype(BF16)
    tri_up = jnp.where(row <= col, 1.0, 0.0).astype(BF16)
    hv = _lane_head(BRANCH_W, HD)
    hq = _lane_head(N_HEADS * c, c)
    key = lax.broadcasted_iota(jnp.int32, (c, N_HEADS * c), 0)
    qry = lax.broadcasted_iota(jnp.int32, (c, N_HEADS * c), 1) % c
    causal_t = key <= qry
    r = lax.broadcasted_iota(jnp.int32, (BRANCH_W, BRANCH_W), 0) // HD
    cc = lax.broadcasted_iota(jnp.int32, (BRANCH_W, BRANCH_W), 1) // HD
    same_head = r == cc
    lane = lax.broadcasted_iota(jnp.int32, (1, 128), 1)
    gate_lane = lax.broadcasted_iota(jnp.int32, (128, 1), 0)
    is_i = (gate_lane >= ML_I_LANE) & (gate_lane < ML_I_LANE + N_HEADS)
    is_f = (gate_lane >= ML_F_LANE) & (gate_lane < ML_F_LANE + N_HEADS)
    gate_head = jnp.where(is_i, gate_lane - ML_I_LANE, gate_lane - ML_F_LANE)
    sel_q = jnp.where((is_i | is_f) & (gate_head == hq), 1.0, 0.0).astype(BF16)
    sel_v = jnp.where((is_i | is_f) & (gate_head == hv), 1.0, 0.0).astype(BF16)
    sel_qv = jnp.concatenate([sel_q, sel_v], axis=1)
    lane128 = lax.broadcasted_iota(jnp.int32, (1, 128), 1)
    in_i = (lane128 >= ML_I_LANE) & (lane128 < ML_I_LANE + N_HEADS)
    in_f = (lane128 >= ML_F_LANE) & (lane128 < ML_F_LANE + N_HEADS)
    head_rows = lax.broadcasted_iota(jnp.int32, (8, 1), 0)

    q, ks, vt, brow, u_q, u_v, sc = [], [], [], [], [], [], []
    for i in chunks:
        rows = slice(i * c, (i + 1) * c)
        q.append(m_ref[rows, 0:256])
        ks.append(m_ref[rows, 256:512] * (HD ** -0.5))
        vt.append(m_ref[rows, 512:768].T.astype(BF16))
        sm = sm_ref[rows, :]
        bcols = _dot_exact_lhs(tri_lo, jax.nn.log_sigmoid(sm))
        brows = _dot_exact_rhs(jax.nn.log_sigmoid(sm.T), tri_up)
        brow.append(jnp.concatenate([brows[ML_F_LANE + h:ML_F_LANE + h + 1, :] for h in range(N_HEADS)], axis=1))
        z = jnp.where(in_i, sm, 0.0) - jnp.where(in_f, bcols, 0.0)
        u = _dot_exact_rhs(z, sel_qv)
        u_q.append(u[:, 0:N_HEADS * c])
        u_v.append(u[:, N_HEADS * c:])
        sc.append(_dot_nt(ks[i].astype(BF16), _stack_heads(q[i], hv).astype(BF16)))

    mm = mm_ref[...]
    qk, w_inter, rowsum, floor, khat, carry_b = [], [], [], [], [], []
    for i in chunks:
        m_prev = _spread_heads([mm[:, h:h + 1] for h in range(N_HEADS)], hq)
        d_log = jnp.where(causal_t, brow[i] + u_q[i], -jnp.inf)
        inter_log = brow[i] + m_prev
        m_t = jnp.maximum(inter_log, jnp.max(d_log, axis=0, keepdims=True))
        qk_i = sc[i] * jnp.exp(d_log - m_t)
        qk.append(qk_i.astype(BF16))
        rowsum.append(jnp.sum(qk_i, axis=0, keepdims=True))
        w_inter.append(jnp.exp(inter_log - m_t))
        floor.append(jnp.exp(-m_t))
        m_new = [m_t[:, (h + 1) * c - 1:(h + 1) * c] for h in range(N_HEADS)]
        b_last = [brow[i][:, (h + 1) * c - 1:(h + 1) * c] for h in range(N_HEADS)]
        carry_b.append(_spread_heads([jnp.exp(b_last[h] + mm[:, h:h + 1] - m_new[h]) for h in range(N_HEADS)], hv))
        khat.append(ks[i] * jnp.exp(u_v[i] + _spread_heads([b_last[h] - m_new[h] for h in range(N_HEADS)], hv)))
        mm = jnp.zeros((1, 128), F32)
        for h in range(N_HEADS):
            mm = jnp.where(lane == h, m_new[h], mm)

    intra = [_dot(vt[i], qk[i]) for i in chunks]
    upd = [_dot(vt[i], khat[i].astype(BF16)) for i in chunks]
    cms, nrows = [c_ref[...]], [n_ref[...]]
    for i in chunks:
        cms.append(carry_b[i] * cms[i] + jnp.where(same_head, upd[i], 0.0))
        nrows.append(carry_b[i] * nrows[i] + jnp.sum(khat[i], axis=0, keepdims=True))
    hout = []
    for i in chunks:
        qb = q[i].astype(BF16)
        inter = _dot_nt(cms[i].astype(BF16), qb)
        n_sel = jnp.where(head_rows == hv, nrows[i], 0.0)
        qn = sum(_dot_nt(p, qb) for p in _split3(n_sel))
        blocks = []
        for h in range(N_HEADS):
            cols = slice(h * c, (h + 1) * c)
            den = w_inter[i][:, cols] * qn[h:h + 1, :] + rowsum[i][:, cols]
            num = w_inter[i][:, cols] * inter[h * HD:(h + 1) * HD, :] + intra[i][h * HD:(h + 1) * HD, cols]
            blocks.append(num / jnp.maximum(jnp.abs(den), floor[i][:, cols]))
        hout.append(jnp.concatenate(blocks, axis=0).T)
    y_ref[...] = _head_norm(jax.nn.sigmoid(m_ref[:, 768:1024]) * jnp.concatenate(hout, axis=0), g_ref[...])

    c_ref[...] = cms[-1]
    n_ref[...] = nrows[-1]
    mm_ref[...] = mm
    c_out_ref[...] = cms[-1]
    n_out_ref[...] = nrows[-1]
    m_out_ref[...] = mm


def _ml_prompt(ml, small, lw, bsz, seq):
    rows = STEP_CHUNKS * CHUNK
    nc = seq // rows
    return pl.pallas_call(
        _ml_prompt_kernel,
        grid=(bsz, nc),
        in_specs=[pl.BlockSpec((rows, W_ML), lambda b, n: (b * nc + n, 0)),
                  pl.BlockSpec((rows, W_SMALL), lambda b, n: (b * nc + n, 0)),
                  pl.BlockSpec((1, BRANCH_W), lambda b, n: (0, 0))],
        out_specs=[pl.BlockSpec((rows, BRANCH_W), lambda b, n: (b * nc + n, 0)),
                   pl.BlockSpec((None, BRANCH_W, BRANCH_W), lambda b, n: (b, 0, 0)),
                   pl.BlockSpec((None, 1, BRANCH_W), lambda b, n: (b, 0, 0)),
                   pl.BlockSpec((None, 1, 128), lambda b, n: (b, 0, 0))],
        out_shape=[jax.ShapeDtypeStruct((bsz * seq, BRANCH_W), F32),
                   jax.ShapeDtypeStruct((bsz, BRANCH_W, BRANCH_W), F32),
                   jax.ShapeDtypeStruct((bsz, 1, BRANCH_W), F32),
                   jax.ShapeDtypeStruct((bsz, 1, 128), F32)],
        scratch_shapes=[pltpu.VMEM((BRANCH_W, BRANCH_W), F32), pltpu.VMEM((1, BRANCH_W), F32),
                        pltpu.VMEM((1, 128), F32)],
        compiler_params=_cparams(("parallel", "arbitrary")),
        name="ml_prompt",
    )(ml, small, lw["ml_g"])


def _diag_blocks(s_bd):
    b = s_bd.shape[0]
    s = s_bd.reshape(b, N_HEADS, HD, N_HEADS, HD)
    return jnp.stack([s[:, h, :, h, :] for h in range(N_HEADS)], axis=1)


GLA_TILE = 256
GLA_SUB = 16


def _gla_prompt_kernel(x_ref, sm_ref, wa2_ref, ba2_ref, g_ref, y_ref, s_out_ref, s_ref):
    n = pl.program_id(1)

    @pl.when(n == 0)
    def _():
        s_ref[...] = jnp.zeros_like(s_ref)

    tq, sc = GLA_TILE, GLA_SUB
    x = x_ref[...]
    q = x[:, 0:128] * (GLA_DK ** -0.5)
    k = x[:, 128:256]
    v = x[:, 256:512]
    log_a = jax.nn.log_sigmoid(_dot(sm_ref[...].astype(BF16), wa2_ref[...]) + ba2_ref[...]) / GLA_TAU
    r = lax.broadcasted_iota(jnp.int32, (tq, tq), 0)
    c = lax.broadcasted_iota(jnp.int32, (tq, tq), 1)
    same = (r // sc) == (c // sc)
    b = _dot_exact_lhs(jnp.where(same & (c <= r), 1.0, 0.0).astype(BF16), log_a)
    btot = _dot_exact_lhs(jnp.where(same, 1.0, 0.0).astype(BF16), log_a)
    qd = (q * jnp.exp(b)).astype(BF16)
    dec_t = jnp.exp(btot).T
    vb = v.astype(BF16)
    kh = lax.broadcasted_iota(jnp.int32, (128, 1), 0) // GLA_DK
    hv = _lane_head(BRANCH_W, HD)
    expand = jnp.where(kh == hv, 1.0, 0.0).astype(BF16)
    trow = lax.broadcasted_iota(jnp.int32, (sc, 1), 0)
    chunks = range(tq // sc)
    khat_b = (k * jnp.exp(btot - b)).astype(BF16)
    upds = [_dot_tn(khat_b[ci * sc:(ci + 1) * sc], vb[ci * sc:(ci + 1) * sc]) for ci in chunks]
    states = [s_ref[...]]
    for ci in chunks:
        states.append(dec_t[:, ci * sc:ci * sc + 1] * states[ci] + jnp.where(kh == hv, upds[ci], 0.0))
    outs = []
    for ci in chunks:
        lo = ci * sc
        qc, kc, bc, vc = q[lo:lo + sc], k[lo:lo + sc], b[lo:lo + sc], v[lo:lo + sc]
        pair = [qc * kc[j:j + 1] * jnp.exp(jnp.where(trow >= j, bc - bc[j:j + 1], -jnp.inf)) for j in range(sc)]
        att = _dot(jnp.concatenate(pair, axis=0).astype(BF16), expand)
        o = att[0:sc] * vc[0:1]
        for j in range(1, sc):
            o = o + att[j * sc:(j + 1) * sc] * vc[j:j + 1]
        outs.append(o)
    outs = [outs[ci] + _dot(qd[ci * sc:(ci + 1) * sc], states[ci].astype(BF16)) for ci in chunks]
    s_ref[...] = states[-1]
    s_out_ref[...] = states[-1]
    y_ref[...] = jax.nn.silu(x[:, 512:768]) * _head_norm(jnp.concatenate(outs, axis=0), g_ref[...])


def _gla_prompt(gla, small, lw, bsz, seq):
    tq = GLA_TILE
    nt = seq // tq
    full = lambda shape: pl.BlockSpec(shape, lambda b, n: (0,) * len(shape))
    return pl.pallas_call(
        _gla_prompt_kernel,
        grid=(bsz, nt),
        in_specs=[pl.BlockSpec((tq, W_GLA), lambda b, n: (b * nt + n, 0)),
                  pl.BlockSpec((tq, W_SMALL), lambda b, n: (b * nt + n, 0)),
                  full((128, 128)), full((1, 128)), full((1, BRANCH_W))],
        out_specs=[pl.BlockSpec((tq, BRANCH_W), lambda b, n: (b * nt + n, 0)),
                   pl.BlockSpec((None, 128, BRANCH_W), lambda b, n: (b, 0, 0))],
        out_shape=[jax.ShapeDtypeStruct((bsz * seq, BRANCH_W), F32),
                   jax.ShapeDtypeStruct((bsz, 128, BRANCH_W), F32)],
        scratch_shapes=[pltpu.VMEM((128, BRANCH_W), F32)],
        compiler_params=_cparams(("parallel", "arbitrary")),
        name="gla_prompt",
    )(gla, small, lw["wa2"], lw["ba2"], lw["gla_g"])


def _gla_state_from_blockdiag(s_bd):
    b = s_bd.shape[0]
    s = s_bd.reshape(b, N_HEADS, GLA_DK, N_HEADS, HD)
    return jnp.stack([s[:, h, :, h, :] for h in range(N_HEADS)], axis=1)


LANES = 128


def _rank1_update(st_ref, snt_ref, q_ref, k_ref, decay_row, v, dk):
    def body(i, o):
        rows = pl.ds(pl.multiple_of(i * HD, HD), HD)
        new = decay_row(i) * st_ref[rows, :] + k_ref[pl.ds(i, 1), :] * v
        snt_ref[rows, :] = new
        return o + q_ref[pl.ds(i, 1), :] * new
    return lax.fori_loop(0, dk, body, jnp.zeros((HD, LANES), F32), unroll=4)


def _head_rows(h, n):
    return pl.ds(pl.multiple_of(h * n, n), n)


def _gla_sample_kernel(x_ref, sm_ref, wa2_ref, ba2_ref, g_ref, s_ref, y_ref, s_out_ref,
                       qt_ref, kt_ref, at_ref, vt_ref, ot_ref):
    h = pl.program_id(0)

    @pl.when(h == 0)
    def _():
        x = x_ref[...]
        qt_ref[...] = (x[:, 0:128] * (GLA_DK ** -0.5)).T
        kt_ref[...] = x[:, 128:256].T
        vt_ref[...] = x[:, 256:512].T
        log_a = jax.nn.log_sigmoid(_dot(sm_ref[...].astype(BF16), wa2_ref[...]) + ba2_ref[...]) / GLA_TAU
        at_ref[...] = jnp.exp(log_a).T

    base = h * GLA_DK
    o = _rank1_update(s_ref, s_out_ref, qt_ref.at[_head_rows(h, GLA_DK)], kt_ref.at[_head_rows(h, GLA_DK)],
                      lambda i: at_ref[pl.ds(base + i, 1), :], vt_ref[_head_rows(h, HD), :], GLA_DK)
    ot_ref[_head_rows(h, HD), :] = o

    @pl.when(h == N_HEADS - 1)
    def _():
        y_ref[...] = jax.nn.silu(x_ref[:, 512:768]) * _head_norm(ot_ref[...].T, g_ref[...])


def _gla_sample(gla, small, state, lw):
    per_head = GLA_DK * HD
    li = lw["layer"]
    full = lambda shape: pl.BlockSpec(shape, lambda h: (0,) * len(shape))
    vm = lambda r: pltpu.VMEM((r, LANES), F32)
    return pl.pallas_call(
        _gla_sample_kernel,
        grid=(N_HEADS,),
        in_specs=[full((LANES, W_GLA)), full((LANES, W_SMALL)), full((128, 128)), full((1, 128)), full((1, BRANCH_W)),
                  pl.BlockSpec((None, per_head, LANES), lambda h: (li, h, 0))],
        out_specs=[full((LANES, BRANCH_W)), pl.BlockSpec((per_head, LANES), lambda h: (h, 0))],
        out_shape=[jax.ShapeDtypeStruct((LANES, BRANCH_W), F32), jax.ShapeDtypeStruct(state.shape[1:], F32)],
        scratch_shapes=[vm(128), vm(128), vm(128), vm(256), vm(256)],
        compiler_params=_cparams(("arbitrary",)),
        name="gla_sample",
    )(gla, small, lw["wa2"], lw["ba2"], lw["gla_g"], state)


def _ml_sample_kernel(x_ref, sc_ref, n_ref, g_ref, c_ref, y_ref, c_out_ref, n_out_ref, m_out_ref,
                      qt_ref, kt_ref, vt_ref, ot_ref, w_ref, kp_ref):
    h = pl.program_id(0)

    @pl.when(h == 0)
    def _():
        x = x_ref[...]
        qt_ref[...] = x[:, 0:256].T
        kt_ref[...] = (x[:, 256:512] * (HD ** -0.5)).T
        vt_ref[...] = x[:, 512:768].T
        sct = sc_ref[...].T
        log_f = jax.nn.log_sigmoid(sct[8:16])
        m_old = sct[16:24]
        m_new = jnp.maximum(log_f + m_old, sct[0:8])
        w_ref[0:8, :] = jnp.exp(log_f + m_old - m_new)
        w_ref[8:16, :] = jnp.exp(sct[0:8] - m_new)
        w_ref[16:24, :] = m_new

    w_carry = w_ref[pl.ds(h, 1), :]
    w_tok = w_ref[pl.ds(8 + h, 1), :]
    m_new = w_ref[pl.ds(16 + h, 1), :]
    kp_ref[...] = kt_ref[_head_rows(h, HD), :] * w_tok
    q_h = qt_ref.at[_head_rows(h, HD)]
    n_new = w_carry * n_ref[_head_rows(h, HD), :] + kp_ref[...]
    num = _rank1_update(c_ref, c_out_ref, q_h, kp_ref, lambda i: w_carry, vt_ref[_head_rows(h, HD), :], HD)
    den = jnp.sum(q_h[...] * n_new, axis=0, keepdims=True)
    ot_ref[_head_rows(h, HD), :] = num / jnp.maximum(jnp.abs(den), jnp.exp(-m_new))
    n_out_ref[_head_rows(h, HD), :] = n_new

    @pl.when(h == N_HEADS - 1)
    def _():
        y_ref[...] = _head_norm(jax.nn.sigmoid(x_ref[:, 768:1024]) * ot_ref[...].T, g_ref[...])
        m_out_ref[...] = jnp.concatenate([w_ref[16:24, :], jnp.zeros((LANES - 8, LANES), F32)], axis=0).T


def _ml_sample(ml, scal, n_state, c_state, lw):
    per_head = HD * HD
    li = lw["layer"]
    full = lambda shape: pl.BlockSpec(shape, lambda h: (0,) * len(shape))
    vm = lambda r: pltpu.VMEM((r, LANES), F32)
    return pl.pallas_call(
        _ml_sample_kernel,
        grid=(N_HEADS,),
        in_specs=[full((LANES, W_ML)), full((LANES, 128)),
                  pl.BlockSpec((None, BRANCH_W, LANES), lambda h: (li, 0, 0)), full((1, BRANCH_W)),
                  pl.BlockSpec((None, per_head, LANES), lambda h: (li, h, 0))],
        out_specs=[full((LANES, BRANCH_W)), pl.BlockSpec((per_head, LANES), lambda h: (h, 0)),
                   full((BRANCH_W, LANES)), full((LANES, 128))],
        out_shape=[jax.ShapeDtypeStruct((LANES, BRANCH_W), F32), jax.ShapeDtypeStruct(c_state.shape[1:], F32),
                   jax.ShapeDtypeStruct((BRANCH_W, LANES), F32), jax.ShapeDtypeStruct((LANES, 128), F32)],
        scratch_shapes=[vm(256), vm(256), vm(256), vm(256), vm(24), vm(HD)],
        compiler_params=_cparams(("arbitrary",)),
        name="ml_sample",
    )(ml, scal, n_state, lw["ml_g"], c_state)


def _ret_sample_kernel(x_ref, cos_ref, sin_ref, gam_ref, g_ref, s_ref, y_ref, s_out_ref,
                       qt_ref, kt_ref, vt_ref, ot_ref, qr_ref, kr_ref):
    h = pl.program_id(0)

    @pl.when(h == 0)
    def _():
        x = x_ref[...]
        qt_ref[...] = x[:, 0:256].T
        kt_ref[...] = (x[:, 256:512] * (HD ** -0.5)).T
        vt_ref[...] = x[:, 512:768].T

    cos, sin = cos_ref[...], sin_ref[...]
    for src, dst in ((qt_ref, qr_ref), (kt_ref, kr_ref)):
        x1 = src[_head_rows(h, 32), :]
        x2 = src[pl.ds(pl.multiple_of(128 + h * 32, 32), 32), :]
        dst[0:32, :] = x1 * cos - x2 * sin
        dst[32:64, :] = x1 * sin + x2 * cos
    gamma = gam_ref[pl.ds(h, 1), :]
    ot_ref[_head_rows(h, HD), :] = _rank1_update(s_ref, s_out_ref, qr_ref, kr_ref, lambda i: gamma,
                                                 vt_ref[_head_rows(h, HD), :], HD)

    @pl.when(h == N_HEADS - 1)
    def _():
        y_ref[...] = jax.nn.silu(x_ref[:, 768:1024]) * _head_norm(ot_ref[...].T, g_ref[...])


def _ret_sample(ret, state, lw):
    per_head = HD * HD
    li = lw["layer"]
    half = HD // 2
    freq = ROPE_BASE ** (-jnp.arange(half, dtype=F32) / half)
    ang = jnp.full((1,), PAST_LEN, F32)[:, None] * freq[None, :]
    cos = jnp.broadcast_to(jnp.cos(ang).reshape(half, 1), (half, LANES))
    sin = jnp.broadcast_to(jnp.sin(ang).reshape(half, 1), (half, LANES))
    log_g = jnp.log1p(-jnp.exp2(-5.0 - jnp.arange(N_HEADS, dtype=F32)))
    gam = jnp.broadcast_to(jnp.concatenate([jnp.exp(log_g), jnp.ones((4,), F32)]).reshape(8, 1), (8, LANES))
    full = lambda shape: pl.BlockSpec(shape, lambda h: (0,) * len(shape))
    vm = lambda r: pltpu.VMEM((r, LANES), F32)
    return pl.pallas_call(
        _ret_sample_kernel,
        grid=(N_HEADS,),
        in_specs=[full((LANES, W_RET)), full((half, LANES)), full((half, LANES)), full((8, LANES)), full((1, BRANCH_W)),
                  pl.BlockSpec((None, per_head, LANES), lambda h: (li, h, 0))],
        out_specs=[full((LANES, BRANCH_W)), pl.BlockSpec((per_head, LANES), lambda h: (h, 0))],
        out_shape=[jax.ShapeDtypeStruct((LANES, BRANCH_W), F32), jax.ShapeDtypeStruct(state.shape[1:], F32)],
        scratch_shapes=[vm(256), vm(256), vm(256), vm(256), vm(HD), vm(HD)],
        compiler_params=_cparams(("arbitrary",)),
        name="ret_sample",
    )(ret, cos, sin, gam, lw["ret_g"], state)


def _swa_sample_kernel(q_ref, kn_ref, vn_ref, knc_ref, vnc_ref, sink_ref, kc_ref, vc_ref, o_ref, ko_ref, vo_ref):
    q = q_ref[...]
    kc, vc = kc_ref[...], vc_ref[...]
    kn, vn = kn_ref[...], vn_ref[...]
    scale = HD ** -0.5
    s = _einsum("bhd,bdw->bhw", q.astype(BF16), kc.astype(BF16)) * scale
    s_new = jnp.sum(q * kn, axis=-1, keepdims=True) * scale
    w = lax.broadcasted_iota(jnp.int32, (1, 1, WINDOW), 2)
    s = jnp.where(w > 0, s, -jnp.inf)
    sink = sink_ref[...][:, 0:1]
    mx = jnp.maximum(jnp.maximum(jnp.max(s, axis=-1, keepdims=True), s_new), sink)
    p = jnp.exp(s - mx)
    p_new = jnp.exp(s_new - mx)
    den = jnp.sum(p, axis=-1, keepdims=True) + p_new + jnp.exp(sink - mx)
    o = _einsum("bhw,bdw->bhd", p.astype(BF16), vc.astype(BF16)) + p_new * vn
    o_ref[...] = o / den
    newest = w == WINDOW - 1
    ko_ref[...] = jnp.where(newest, knc_ref[...], pltpu.roll(kc, WINDOW - 1, 2))
    vo_ref[...] = jnp.where(newest, vnc_ref[...], pltpu.roll(vc, WINDOW - 1, 2))


def _swa_sample(swa, cache_k, cache_v, lw, bt=16):
    bsz = swa.shape[0]
    li = lw["layer"]
    cache = pl.BlockSpec((None, bt, WINDOW, 128), lambda i: (li, i, 0, 0))
    lo = (jnp.arange(128) < HD)[None, :]
    qa, qb = swa[:, 0:128], swa[:, 128:256]
    q = jnp.stack([jnp.where(lo, qa, 0.0), jnp.where(lo, qb, 0.0), jnp.where(lo, 0.0, qa), jnp.where(lo, 0.0, qb)], axis=1)
    kn = swa[:, None, 256:384]
    vn = swa[:, None, 384:512]
    sink = jnp.broadcast_to(lw["sinks"][0, 0:N_HEADS].reshape(N_HEADS, 1), (N_HEADS, 128))
    blk = lambda shape: pl.BlockSpec(shape, lambda i: (i,) + (0,) * (len(shape) - 1))
    o, ko, vo = pl.pallas_call(
        _swa_sample_kernel,
        grid=(bsz // bt,),
        in_specs=[blk((bt, N_HEADS, 128)), blk((bt, 1, 128)), blk((bt, 1, 128)), blk((bt, 128, 1)), blk((bt, 128, 1)),
                  pl.BlockSpec((N_HEADS, 128), lambda i: (0, 0)), cache, cache],
        out_specs=[blk((bt, N_HEADS, 128)), blk((bt, WINDOW, 128)), blk((bt, WINDOW, 128))],
        out_shape=[jax.ShapeDtypeStruct((bsz, N_HEADS, 128), F32), jax.ShapeDtypeStruct(cache_k.shape[1:], F32),
                   jax.ShapeDtypeStruct(cache_v.shape[1:], F32)],
        compiler_params=_cparams(("parallel",)),
        name="swa_sample",
    )(q, kn, vn, jnp.swapaxes(kn, 1, 2), jnp.swapaxes(vn, 1, 2), sink, cache_k, cache_v)
    y = jnp.concatenate([o[:, 0, 0:64], o[:, 2, 64:128], o[:, 1, 0:64], o[:, 3, 64:128]], axis=1)
    return y, ko, vo


def _ml_scalars(small, m_state):
    z4 = jnp.zeros((small.shape[0], 4), F32)
    return jnp.concatenate([small[:, ML_I_LANE:ML_I_LANE + 4], z4, small[:, ML_F_LANE:ML_F_LANE + 4], z4, m_state, z4,
                            jnp.zeros((small.shape[0], 128 - 24), F32)], axis=1)


def _take_cols(a, pieces, pad):
    parts = [a[..., s:s + n] for s, n in pieces]
    if pad:
        parts.append(jnp.zeros(a.shape[:-1] + (pad,), a.dtype))
    return jnp.concatenate(parts, axis=-1)


def _mix_pieces():
    p = [(0, 768)]
    sq = 784
    p += [(sq, 64), (sq + 128, 64), (sq + 64, 64), (sq + 192, 64)]
    p += [(1040, 256)]
    p += [(1296, 768), (2072, 256)]
    for base in (2328, 2584):
        for half in range(2):
            p += [(base + h * HD + half * 32, 32) for h in range(N_HEADS)]
    p += [(2840, 512)]
    p += [(768, 16), (2064, 8)]
    return p


def _mixer_rows(w_t):
    x = w_t.shape[1]
    swa_q = w_t[784:1040].reshape(2, 2, HD, x).transpose(1, 0, 2, 3).reshape(BRANCH_W, x)
    halves = lambda lo: w_t[lo:lo + BRANCH_W].reshape(N_HEADS, 2, 32, x).transpose(1, 0, 2, 3).reshape(BRANCH_W, x)
    parts = [w_t[0:768], swa_q, w_t[1040:1296], w_t[1296:2064], w_t[2072:2328], halves(2328), halves(2584),
             w_t[2840:3352], w_t[768:784], w_t[2064:2072]]
    pad = W_MIX - sum(p.shape[0] for p in parts)
    return jnp.concatenate(parts + [jnp.zeros((pad, x), w_t.dtype)], axis=0)


def _prep_layer(li, w_in, b_in, gla_w_a2, gla_b_a2, gla_norm_g, swa_sinks, ml_norm_g, ret_norm_g, w_branch, w_out,
                ln1_g, ln1_b, w_router_grp, b_router_grp, w_router_exp, b_router_exp, w_exp_gate, w_exp_up,
                w_exp_down, ln2_g, ln2_b):
    pieces = _mix_pieces()
    pad = W_MIX - sum(n for _, n in pieces)
    row = lambda v: v.reshape(1, -1).astype(F32)
    wr = jnp.concatenate([w_router_grp[li], w_router_exp[li],
                          jnp.zeros((D_MODEL, 128 - N_GROUPS - N_EXPERTS), F32)], axis=1)
    wr_hi = wr.astype(BF16)
    wb = w_branch[li]
    wb1 = jnp.concatenate([wb[1, 0:64], wb[1, 128:192], wb[1, 64:128], wb[1, 192:256]], axis=0)
    w_t = w_in[li].T
    return dict(
        wm=_mixer_rows(w_t).astype(BF16),
        bm=row(_mixer_rows(b_in[li][:, None])),
        wg=w_t[GATE_COL:].astype(BF16),
        bg=row(b_in[li][GATE_COL:]),
        wa2=jnp.concatenate([gla_w_a2[li], jnp.zeros((128 - GLA_RANK, 128), F32)], axis=0).astype(BF16),
        ba2=row(gla_b_a2[li]),
        gla_g=row(gla_norm_g[li]), ml_g=row(ml_norm_g[li]), ret_g=row(ret_norm_g[li]),
        sinks=row(jnp.concatenate([swa_sinks[li], jnp.zeros((128 - N_HEADS,), F32)])),
        wbr=jnp.stack([wb[0], wb1, wb[2], wb[3]]).astype(BF16),
        wout=w_out[li].astype(BF16),
        ln1_g=row(ln1_g[li]), ln1_b=row(ln1_b[li]), ln2_g=row(ln2_g[li]), ln2_b=row(ln2_b[li]),
        wr_hi=wr_hi, wr_lo=(wr - wr_hi.astype(F32)).astype(BF16),
        br=row(jnp.concatenate([b_router_grp[li], b_router_exp[li], jnp.zeros((128 - N_GROUPS - N_EXPERTS,), F32)])),
        layer=li, we_gate=w_exp_gate, we_up=w_exp_up, we_down=w_exp_down,
    )


TOK_TILE = 256
MOE_PIECES_PROMPT = 32
MOE_PIECES_SAMPLE = 1


MERGE_TILES = 2


def _ffn_half(x2, ys, lw, tm, bp):
    x1, route, cnt, xs = _merge(x2, ys, lw, tm, MERGE_TILES if x2.shape[0] // tm % MERGE_TILES == 0 else 1)
    return _moe(x1, route, cnt, xs, lw, bp, tm)


def _layer_prompt(x2, lw, bsz, seq):
    gla, swa, ml, ret, small = _inproj(x2, lw["wm"], lw["bm"], 2 * TOK_TILE)
    ya, s_gla = _gla_prompt(gla, small, lw, bsz, seq)
    yb = _swa_prompt(swa, lw, bsz, seq)
    yc, c_bd, n_row, m_row = _ml_prompt(ml, small, lw, bsz, seq)
    yd, s_ret = _ret_prompt(ret, lw, bsz, seq)
    kv_tail = swa.reshape(bsz, seq, W_SWA)[:, seq - WINDOW:, 256:512]
    states = (_gla_state_from_blockdiag(s_gla),
              kv_tail[:, :, 0:128].reshape(bsz, WINDOW, SWA_KV, HD),
              kv_tail[:, :, 128:256].reshape(bsz, WINDOW, SWA_KV, HD),
              jnp.swapaxes(_diag_blocks(c_bd), -1, -2), n_row.reshape(bsz, N_HEADS, HD), m_row[:, 0, 0:N_HEADS],
              _ret_state_from_blockdiag(s_ret))
    return _ffn_half(x2, (ya, yb, yc, yd), lw, TOK_TILE, MOE_PIECES_PROMPT), states


def _layer_sample(x2, st, lw):
    s_gla, k_buf, v_buf, c_ml, n_ml, m_ml, s_ret = st
    bsz = x2.shape[0]
    depth = s_gla.shape[0]
    gla, swa, ml, ret, small = _inproj(x2, lw["wm"], lw["bm"], bsz)
    to_lanes = lambda s: jnp.moveaxis(s, 1, -1).reshape(depth, -1, bsz)
    from_lanes = lambda s, like: jnp.moveaxis(s.reshape(like.shape[2:] + (bsz,)), -1, 0)
    cache_t = lambda c: jnp.swapaxes(c.reshape(depth, bsz, WINDOW, SWA_KV * HD), 2, 3)
    ya, s_gla_new = _gla_sample(gla, small, to_lanes(s_gla), lw)
    yb, k_new, v_new = _swa_sample(swa, cache_t(k_buf), cache_t(v_buf), lw)
    k_new, v_new = jnp.swapaxes(k_new, 1, 2), jnp.swapaxes(v_new, 1, 2)
    yc, c_new, n_new, m_new = _ml_sample(ml, _ml_scalars(small, m_ml[lw["layer"]]), to_lanes(n_ml), to_lanes(c_ml), lw)
    yd, s_ret_new = _ret_sample(ret, to_lanes(s_ret), lw)
    states = (from_lanes(s_gla_new, s_gla), k_new.reshape(k_buf.shape[1:]), v_new.reshape(v_buf.shape[1:]),
              from_lanes(c_new, c_ml), from_lanes(n_new, n_ml), m_new[:, 0:N_HEADS], from_lanes(s_ret_new, s_ret))
    return _ffn_half(x2, (ya, yb, yc, yd), lw, bsz, MOE_PIECES_SAMPLE), states


def kernel(x_prompt, x_sample, state_gla, cache_swa_k, cache_swa_v, state_mlstm_c, state_mlstm_n, state_mlstm_m,
           state_ret, w_in, b_in, gla_w_a2, gla_b_a2, gla_norm_g, swa_sinks, ml_norm_g, ret_norm_g, w_branch, w_out,
           ln1_g, ln1_b, w_router_grp, b_router_grp, w_router_exp, b_router_exp, w_exp_gate, w_exp_up, w_exp_down,
           ln2_g, ln2_b):
    bsz, seq, _ = x_prompt.shape
    dec_b, dec_seq, _ = x_sample.shape
    assert dec_seq == 1 and dec_b == LANES and seq % GLA_TILE == 0 and (bsz * seq) % TOK_TILE == 0
    weights = (w_in, b_in, gla_w_a2, gla_b_a2, gla_norm_g, swa_sinks, ml_norm_g, ret_norm_g, w_branch, w_out,
               ln1_g, ln1_b, w_router_grp, b_router_grp, w_router_exp, b_router_exp, w_exp_gate, w_exp_up,
               w_exp_down, ln2_g, ln2_b)
    y_p = x_prompt.reshape(bsz * seq, D_MODEL)
    y_s = x_sample.reshape(dec_b, D_MODEL)
    new_p, new_s = [], []
    for li in range(w_in.shape[0]):
        lw = _prep_layer(li, *weights)
        y_p, st_p = _layer_prompt(y_p, lw, bsz, seq)
        st_in = (state_gla, cache_swa_k, cache_swa_v, state_mlstm_c, state_mlstm_n, state_mlstm_m, state_ret)
        y_s, st_s = _layer_sample(y_s, st_in, lw)
        new_p.append(st_p)
        new_s.append(st_s)
    outs = [y_p.reshape(bsz, seq, D_MODEL), y_s.reshape(dec_b, dec_seq, D_MODEL)]
    for i in range(7):
        outs.append(jnp.stack([s[i] for s in new_p], axis=0))
        outs.append(jnp.stack([s[i] for s in new_s], axis=0))
    return tuple(outs)
```

```python
import functools

import jax
import jax.numpy as jnp
from jax import lax
from jax.experimental import pallas as pl
from jax.experimental.pallas import tpu as pltpu

F32 = jnp.float32
BF16 = jnp.bfloat16

D_MODEL = 1024
N_BRANCH = 4
BRANCH_W = D_MODEL // N_BRANCH
N_HEADS = 4
GLA_DK = 32
GLA_RANK = 16
GLA_TAU = 16.0
HD = 64
SWA_KV = 2
WINDOW = 128
ROPE_BASE = 10000.0
PAST_LEN = 16384
N_GROUPS = 4
EPG = 8
N_EXPERTS = N_GROUPS * EPG
D_EXPERT = D_MODEL // 4
DEPTH = 2
ALPHA = (2 * DEPTH) ** 0.25
LN_EPS = 1e-5
VMEM_LIMIT = 56 * 1024 * 1024

W_GLA, W_SWA, W_ML, W_RET, W_SMALL = 768, 512, 1024, 1024, 128
W_MIX = W_GLA + W_SWA + W_ML + W_RET + W_SMALL


def _cparams(sem):
    return pltpu.CompilerParams(dimension_semantics=sem, vmem_limit_bytes=VMEM_LIMIT)


def _dot(a, b):
    return jnp.dot(a, b, preferred_element_type=F32)


def _einsum(spec, a, b):
    return jnp.einsum(spec, a, b, preferred_element_type=F32)


def _split2(x):
    hi = x.astype(BF16)
    lo = (x - hi.astype(F32)).astype(BF16)
    return hi, lo


def _split3(x):
    hi = x.astype(BF16)
    r = x - hi.astype(F32)
    mid = r.astype(BF16)
    lo = (r - mid.astype(F32)).astype(BF16)
    return hi, mid, lo


def _dot_exact_lhs(a_exact, b):
    return sum(_dot(a_exact, p) for p in _split3(b))


def _dot_exact_rhs(a, b_exact):
    return sum(_dot(p, b_exact) for p in _split3(a))


def _layer_norm(x, g, b):
    mu = jnp.mean(x, axis=-1, keepdims=True)
    xc = x - mu
    var = jnp.mean(xc * xc, axis=-1, keepdims=True)
    return xc * lax.rsqrt(var + LN_EPS) * g + b


def _inproj_kernel(x_ref, w_ref, b_ref, gla_ref, swa_ref, ml_ref, ret_ref, sm_ref):
    xb = x_ref[...].astype(BF16)
    off = 0
    for ref, width in ((gla_ref, W_GLA), (swa_ref, W_SWA), (ml_ref, W_ML), (ret_ref, W_RET), (sm_ref, W_SMALL)):
        ref[...] = _dot_nt(xb, w_ref[off:off + width, :]) + b_ref[:, off:off + width]
        off += width


def _inproj(x2, wm, bm, tm):
    t = x2.shape[0]
    widths = (W_GLA, W_SWA, W_ML, W_RET, W_SMALL)
    return pl.pallas_call(
        _inproj_kernel,
        grid=(t // tm,),
        in_specs=[pl.BlockSpec((tm, D_MODEL), lambda i: (i, 0)),
                  pl.BlockSpec((W_MIX, D_MODEL), lambda i: (0, 0)),
                  pl.BlockSpec((1, W_MIX), lambda i: (0, 0))],
        out_specs=[pl.BlockSpec((tm, w), lambda i: (i, 0)) for w in widths],
        out_shape=[jax.ShapeDtypeStruct((t, w), F32) for w in widths],
        compiler_params=_cparams(("parallel",)),
        name="inproj",
    )(x2, wm, bm)


def _merge_kernel(x_ref, ya_ref, yb_ref, yc_ref, yd_ref, wg_ref, bg_ref, wbr_ref, wout_ref, g1_ref, b1_ref,
                  wrh_ref, wrl_ref, br_ref, x1_ref, route_ref, cnt_ref, xs_ref, *, tm, sub):
    rows_t = xs_ref.shape[0] // sub
    x1, picks = [], []
    for t in range(sub):
        rows = slice(t * tm, (t + 1) * tm)
        x = x_ref[rows, :]
        xb = x.astype(BF16)
        merged = jnp.zeros((tm, D_MODEL), F32)
        for n, y_ref in enumerate((ya_ref, yb_ref, yc_ref, yd_ref)):
            cols = slice(n * D_MODEL, (n + 1) * D_MODEL)
            gate = 0.5 * jnp.tanh(0.5 * (_dot_nt(xb, wg_ref[cols, :]) + bg_ref[:, cols])) + 0.5
            merged = merged + gate * _dot(y_ref[rows, :].astype(BF16), wbr_ref[n])
        mix = _dot(merged.astype(BF16), wout_ref[...])
        x1.append(_layer_norm(ALPHA * x + mix, g1_ref[...], b1_ref[...]))
        x1_ref[rows, :] = x1[t]
        picks.append(_route(x1[t], wrh_ref[...], wrl_ref[...], br_ref[...], tm))
    for t in range(sub):
        route, cnt, xs = _sort(x1[t], picks[t], tm, rows_t)
        route_ref[t * tm:(t + 1) * tm, :] = route
        cnt_ref[t] = cnt
        xs_ref[t * rows_t:(t + 1) * rows_t, :] = xs


def _route(x1, wr_hi, wr_lo, br, tm):
    xh, xl = _split2(x1)
    both = _dot(xh, jnp.concatenate([wr_hi, wr_lo], axis=1))
    logits = both[:, 0:128] + both[:, 128:256] + _dot(xl, wr_hi) + br
    lane = lax.broadcasted_iota(jnp.int32, (tm, 128), 1)
    lanef = lane.astype(F32)
    neg = jnp.float32(-jnp.inf)
    big = jnp.float32(1e9)
    isg = lane < N_GROUPS
    gmax = jnp.max(jnp.where(isg, logits, neg), axis=-1, keepdims=True)
    gidx = jnp.min(jnp.where(isg & (logits == gmax), lanef, big), axis=-1, keepdims=True)
    p_group = 1.0 / jnp.sum(jnp.where(isg, jnp.exp(logits - gmax), 0.0), axis=-1, keepdims=True)
    lo = N_GROUPS + EPG * gidx
    em = (lanef >= lo) & (lanef < lo + EPG)
    v1 = jnp.max(jnp.where(em, logits, neg), axis=-1, keepdims=True)
    i1 = jnp.min(jnp.where(em & (logits == v1), lanef, big), axis=-1, keepdims=True)
    em2 = em & (lanef != i1)
    v2 = jnp.max(jnp.where(em2, logits, neg), axis=-1, keepdims=True)
    i2 = jnp.min(jnp.where(em2 & (logits == v2), lanef, big), axis=-1, keepdims=True)
    e21 = jnp.exp(v2 - v1)
    gt1 = 1.0 / (1.0 + e21)
    return i1, i2, p_group * gt1, p_group * (e21 * gt1)


def _sort(x1, picks, tm, rows_t):
    i1, i2, gate1, gate2 = picks
    lane = lax.broadcasted_iota(jnp.int32, (tm, 128), 1)
    lanef = lane.astype(F32)
    sel1 = lanef == i1
    sel2 = lanef == i2
    onehot = jnp.where(sel1 | sel2, 1.0, 0.0)
    cnt = jnp.sum(onehot, axis=0, keepdims=True)
    seg = jnp.floor((cnt + (PIECE - 1)) * (1.0 / PIECE)) * PIECE
    er = lax.broadcasted_iota(jnp.int32, (128, 128), 0)
    ec = lax.broadcasted_iota(jnp.int32, (128, 128), 1)
    seg_off = _dot(jnp.broadcast_to(seg, (8, 128)).astype(BF16), jnp.where(er < ec, 1.0, 0.0).astype(BF16))[0:1]
    row = lax.broadcasted_iota(jnp.int32, (tm, tm), 0)
    col = lax.broadcasted_iota(jnp.int32, (tm, tm), 1)
    earlier = _dot(jnp.where(col < row, 1.0, 0.0).astype(BF16), onehot.astype(BF16))
    where_to = seg_off + earlier
    w1 = jnp.where(sel1, where_to, 0.0)
    w2 = jnp.where(sel2, where_to, 0.0)
    pos1 = jnp.sum(w1, axis=-1, keepdims=True)
    pos2 = jnp.sum(w2, axis=-1, keepdims=True)
    ones = jnp.ones((8, 128), BF16)
    prow1 = sum(_dot_nt(ones, p) for p in _split2(w1))[0:1]
    prow2 = sum(_dot_nt(ones, p) for p in _split2(w2))[0:1]
    r = lax.broadcasted_iota(jnp.int32, (rows_t, tm), 0).astype(F32)
    place = jnp.where((r == prow1) | (r == prow2), 1.0, 0.0).astype(BF16)
    xs = _dot(place, x1.astype(BF16)).astype(BF16)

    vals = (i1 - N_GROUPS, i2 - N_GROUPS, gate1, gate2, pos1, pos2)
    route = jnp.zeros((tm, 128), F32)
    for k, v in enumerate(vals):
        route = jnp.where(lane == k, v, route)
    return route, cnt, xs


PIECE = 16


def _tile_rows(tm):
    return -(-(2 * tm + N_EXPERTS * (PIECE - 1)) // 128) * 128


def _merge(x2, ys, lw, tm, sub):
    t = x2.shape[0]
    nt = t // tm
    rows_t = _tile_rows(tm)
    tok = lambda w: pl.BlockSpec((sub * tm, w), lambda i: (i, 0))
    full = lambda shape: pl.BlockSpec(shape, lambda i: (0,) * len(shape), pipeline_mode=pl.Buffered(1))
    return pl.pallas_call(
        functools.partial(_merge_kernel, tm=tm, sub=sub),
        grid=(nt // sub,),
        in_specs=[tok(D_MODEL), tok(BRANCH_W), tok(BRANCH_W), tok(BRANCH_W), tok(BRANCH_W),
                  full((N_BRANCH * D_MODEL, D_MODEL)), full((1, N_BRANCH * D_MODEL)),
                  full((N_BRANCH, BRANCH_W, D_MODEL)), full((D_MODEL, D_MODEL)),
                  full((1, D_MODEL)), full((1, D_MODEL)),
                  full((D_MODEL, 128)), full((D_MODEL, 128)), full((1, 128))],
        out_specs=[tok(D_MODEL), tok(128), pl.BlockSpec((sub, 1, 128), lambda i: (i, 0, 0)),
                   pl.BlockSpec((sub * rows_t, D_MODEL), lambda i: (i, 0))],
        out_shape=[jax.ShapeDtypeStruct((t, D_MODEL), F32), jax.ShapeDtypeStruct((t, 128), F32),
                   jax.ShapeDtypeStruct((nt, 1, 128), F32), jax.ShapeDtypeStruct((nt * rows_t, D_MODEL), BF16)],
        compiler_params=_cparams(("parallel",)),
        name="merge",
    )(x2, *ys, lw["wg"], lw["bg"], lw["wbr"], lw["wout"], lw["ln1_g"], lw["ln1_b"], lw["wr_hi"], lw["wr_lo"], lw["br"])


def _piece_copy(hbm, hbm_row, vmem, piece, sem, to_vmem):
    h = hbm.at[pl.ds(pl.multiple_of(hbm_row, PIECE), PIECE), :]
    v = vmem.at[pl.ds(pl.multiple_of(piece * PIECE, PIECE), PIECE), :]
    return pltpu.make_async_copy(h, v, sem) if to_vmem else pltpu.make_async_copy(v, h, sem)


def _expert_kernel(be_ref, nu_ref, nv_ref, off_ref, xs_hbm, wg_ref, wu_ref, wd_ref, ys_hbm, xin, yout, sem_in, sem_out,
                   *, bp):
    del xs_hbm
    blk = pl.program_id(0)
    n_used = nu_ref[0]
    slot = blk % 2

    def pieces(b, s, buf, sem, to_vmem, start):
        n = nv_ref[b]

        def one(p, priority):
            cp = _piece_copy(ys_hbm, off_ref[b * bp + p], buf.at[s], p, sem.at[s], to_vmem)
            cp.start(priority=priority) if start else cp.wait()

        def body(i, c):
            one(2 * i, 0)

            @pl.when(2 * i + 1 < n)
            def _():
                one(2 * i + 1, 1)
            return c
        lax.fori_loop(0, (n + 1) // 2, body, 0)

    @pl.when(blk == 0)
    def _():
        xin[...] = jnp.zeros_like(xin)

        @pl.when(n_used > 0)
        def _():
            pieces(0, 0, xin, sem_in, True, True)

    @pl.when(blk + 1 < n_used)
    def _():
        pieces(blk + 1, 1 - slot, xin, sem_in, True, True)

    @pl.when(blk < n_used)
    def _():
        pieces(blk, slot, xin, sem_in, True, False)

        @pl.when(blk >= 2)
        def _():
            pieces(blk - 2, slot, yout, sem_out, False, False)

        xb = xin[slot]
        hid = jax.nn.silu(_dot(xb, wg_ref[...].astype(BF16))) * _dot(xb, wu_ref[...].astype(BF16))
        yout[slot] = _dot(hid.astype(BF16), wd_ref[...].astype(BF16)).astype(BF16)
        pieces(blk, slot, yout, sem_out, False, True)

    @pl.when(blk == n_used - 1)
    def _():
        @pl.when(blk >= 1)
        def _():
            pieces(blk - 1, 1 - slot, yout, sem_out, False, False)
        pieces(blk, slot, yout, sem_out, False, False)


def _experts(xs, block_expert, n_used, n_valid, piece_off, lw, bp):
    n_blocks = block_expert.shape[0]
    rows = bp * PIECE
    li = lw["layer"]
    wspec = lambda shape: pl.BlockSpec((None, None) + shape, lambda b, be, nu, nv, off: (li, be[b], 0, 0))
    grid_spec = pltpu.PrefetchScalarGridSpec(
        num_scalar_prefetch=4,
        grid=(n_blocks,),
        in_specs=[pl.BlockSpec(memory_space=pl.ANY), wspec((D_MODEL, D_EXPERT)), wspec((D_MODEL, D_EXPERT)),
                  wspec((D_EXPERT, D_MODEL))],
        out_specs=pl.BlockSpec(memory_space=pl.ANY),
        scratch_shapes=[pltpu.VMEM((2, rows, D_MODEL), BF16), pltpu.VMEM((2, rows, D_MODEL), BF16),
                        pltpu.SemaphoreType.DMA((2,)), pltpu.SemaphoreType.DMA((2,))],
    )
    return pl.pallas_call(
        functools.partial(_expert_kernel, bp=bp),
        grid_spec=grid_spec,
        out_shape=jax.ShapeDtypeStruct(xs.shape, BF16),
        input_output_aliases={4: 0},
        compiler_params=_cparams(("arbitrary",)),
        name="experts",
    )(block_expert, n_used, n_valid, piece_off, xs, lw["we_gate"], lw["we_up"], lw["we_down"])


def _combine_kernel(ys_ref, x1_ref, route_ref, g2_ref, b2_ref, out_ref, *, tm):
    route = route_ref[...]
    ys = ys_ref[...]
    r = lax.broadcasted_iota(jnp.int32, (tm, ys.shape[0]), 1).astype(F32)
    ffn = jnp.zeros((tm, D_MODEL), F32)
    for j in range(2):
        pick = jnp.where(r == route[:, 4 + j:5 + j], 1.0, 0.0).astype(BF16)
        ffn = ffn + route[:, 2 + j:3 + j] * _dot(pick, ys)
    out_ref[...] = _layer_norm(ALPHA * x1_ref[...] + ffn, g2_ref[...], b2_ref[...])


def _combine(ys, x1, route, lw, tm):
    t = x1.shape[0]
    rows_t = _tile_rows(tm)
    full = lambda shape: pl.BlockSpec(shape, lambda i: (0,) * len(shape))
    return pl.pallas_call(
        functools.partial(_combine_kernel, tm=tm),
        grid=(t // tm,),
        in_specs=[pl.BlockSpec((rows_t, D_MODEL), lambda i: (i, 0)), pl.BlockSpec((tm, D_MODEL), lambda i: (i, 0)),
                  pl.BlockSpec((tm, 128), lambda i: (i, 0)), full((1, D_MODEL)), full((1, D_MODEL))],
        out_specs=pl.BlockSpec((tm, D_MODEL), lambda i: (i, 0)),
        out_shape=jax.ShapeDtypeStruct((t, D_MODEL), F32),
        compiler_params=_cparams(("parallel",)),
        name="combine",
    )(ys, x1, route, lw["ln2_g"], lw["ln2_b"])


def _piece_tables(cnt, tm, bp):
    nt = cnt.shape[0]
    rows_t = _tile_rows(tm)
    counts = cnt[:, 0, N_GROUPS:N_GROUPS + N_EXPERTS].astype(jnp.int32)
    pcs = ((counts + PIECE - 1) // PIECE).T
    seg_row = ((jnp.cumsum(pcs, axis=0) - pcs) * PIECE + jnp.arange(nt, dtype=jnp.int32)[None, :] * rows_t)
    tile_end = jnp.cumsum(pcs, axis=1)
    total = tile_end[:, -1]
    e_end = jnp.cumsum((total + bp - 1) // bp * bp)
    e_start = jnp.concatenate([jnp.zeros((1,), jnp.int32), e_end[:-1]])
    n_blocks = -(-(nt * rows_t // PIECE + N_EXPERTS * (bp - 1)) // bp)
    g = jnp.arange(n_blocks * bp, dtype=jnp.int32)[:, None]
    in_e = ((e_start[None, :] <= g) & (g < e_end[None, :])).astype(F32)
    pick = lambda table: jnp.dot(in_e, table.astype(F32), precision=lax.Precision.HIGHEST)
    local = g.astype(F32) - pick(e_start[:, None])
    ends, starts, rows = pick(tile_end), pick(tile_end - pcs), pick(seg_row)
    in_t = ((starts <= local) & (local < ends)).astype(F32)
    off = jnp.sum(in_t * (rows + (local - starts) * PIECE), axis=1).astype(jnp.int32)
    n_valid = jnp.sum(jnp.sum(in_t, axis=1).reshape(n_blocks, bp), axis=1).astype(jnp.int32)
    expert = pick(jnp.arange(N_EXPERTS, dtype=jnp.int32)[:, None])[:, 0]
    block_expert = expert.reshape(n_blocks, bp)[:, 0].astype(jnp.int32)
    n_used = (e_end[-1] // bp).astype(jnp.int32).reshape(1)
    return block_expert, n_used, n_valid, off


def _moe(x1, route, cnt, xs, lw, bp, tm):
    ys = _experts(xs, *_piece_tables(cnt, tm, bp), lw, bp)
    return _combine(ys, x1, route, lw, tm)


CHUNK = 128
STEP_CHUNKS = 8


def _dot_nt(a, b):
    return lax.dot_general(a, b, (((1,), (1,)), ((), ())), preferred_element_type=F32)


def _dot_tn(a, b):
    return lax.dot_general(a, b, (((0,), (0,)), ((), ())), preferred_element_type=F32)


def _lane_head(width, group):
    return lax.broadcasted_iota(jnp.int32, (1, width), 1) // group


def _head_norm(o, gain):
    r = lax.broadcasted_iota(jnp.int32, (BRANCH_W, BRANCH_W), 0) // HD
    c = lax.broadcasted_iota(jnp.int32, (BRANCH_W, BRANCH_W), 1) // HD
    avg = jnp.where(r == c, 1.0 / HD, 0.0).astype(BF16)
    mu = _dot_exact_rhs(o, avg)
    xc = o - mu
    var = _dot_exact_rhs(xc * xc, avg)
    return xc * lax.rsqrt(var + LN_EPS) * gain


def _stack_heads(x, head_of_lane):
    return jnp.concatenate([jnp.where(head_of_lane == h, x, 0.0) for h in range(N_HEADS)], axis=0)


def _pick_heads(stacked, rows, head_of_lane):
    out = jnp.where(head_of_lane == 0, stacked[0:rows], 0.0)
    for h in range(1, N_HEADS):
        out = jnp.where(head_of_lane == h, stacked[h * rows:(h + 1) * rows], out)
    return out


def _spread_heads(cols, head_of_lane):
    out = jnp.where(head_of_lane == 0, cols[0], 0.0)
    for h in range(1, N_HEADS):
        out = jnp.where(head_of_lane == h, cols[h], out)
    return out


SWA_BLOCKS = 4


def _swa_prompt_kernel(q_ref, kv_ref, kvp_ref, sink_ref, y_ref):
    n = pl.program_id(1)
    lo = lax.broadcasted_iota(jnp.int32, (1, 128), 1) < HD
    e = lax.broadcasted_iota(jnp.int32, (2 * WINDOW, N_HEADS * WINDOW), 0)
    col = lax.broadcasted_iota(jnp.int32, (2 * WINDOW, N_HEADS * WINDOW), 1)
    a = col % WINDOW
    band = (e > a) & (e <= a + WINDOW)
    head = lax.broadcasted_iota(jnp.int32, (1, N_HEADS * WINDOW), 1) // WINDOW
    sink = _spread_heads([sink_ref[:, h:h + 1] for h in range(N_HEADS)], head)
    blocks = range(SWA_BLOCKS)
    scores, vexts = [], []
    for i in blocks:
        rows = slice(i * WINDOW, (i + 1) * WINDOW)
        q = q_ref[rows, :]
        qa, qb = q[:, 0:128], q[:, 128:256]
        kv = kv_ref[rows, :]
        kvp = kvp_ref[...] if i == 0 else kv_ref[(i - 1) * WINDOW:i * WINDOW, :]
        kext = jnp.concatenate([kvp[:, 0:128], kv[:, 0:128]], axis=0).astype(BF16)
        vexts.append(jnp.concatenate([kvp[:, 128:256], kv[:, 128:256]], axis=0).astype(BF16))
        qs = jnp.concatenate([jnp.where(lo, qa, 0.0), jnp.where(lo, qb, 0.0),
                              jnp.where(lo, 0.0, qa), jnp.where(lo, 0.0, qb)], axis=0).astype(BF16)
        scores.append(_dot_nt(kext, qs))
    probs, inv_den = [], []
    for i in blocks:
        valid = band & ((n > 0) | (e >= WINDOW)) if i == 0 else band
        s = jnp.where(valid, scores[i] * (HD ** -0.5), -jnp.inf)
        mx = jnp.maximum(jnp.max(s, axis=0, keepdims=True), sink)
        p = jnp.exp(s - mx)
        inv_den.append(1.0 / (jnp.sum(p, axis=0, keepdims=True) + jnp.exp(sink - mx)))
        probs.append(p.astype(BF16))
    outs = [_dot_tn(vexts[i], probs[i]) for i in blocks]
    for i in blocks:
        o = outs[i] * inv_den[i]
        ya = jnp.concatenate([o[0:HD, 0:128], o[HD:128, 256:384]], axis=0).T
        yb = jnp.concatenate([o[0:HD, 128:256], o[HD:128, 384:512]], axis=0).T
        y_ref[i * WINDOW:(i + 1) * WINDOW, :] = jnp.concatenate([ya, yb], axis=1)


def _swa_prompt(swa, lw, bsz, seq):
    qt = SWA_BLOCKS * WINDOW
    nt = seq // qt
    return pl.pallas_call(
        _swa_prompt_kernel,
        grid=(bsz, nt),
        in_specs=[pl.BlockSpec((qt, 256), lambda b, n: (b * nt + n, 0)),
                  pl.BlockSpec((qt, 256), lambda b, n: (b * nt + n, 1)),
                  pl.BlockSpec((WINDOW, 256), lambda b, n: (jnp.maximum((b * nt + n) * SWA_BLOCKS - 1, 0), 1)),
                  pl.BlockSpec((1, 128), lambda b, n: (0, 0))],
        out_specs=pl.BlockSpec((qt, BRANCH_W), lambda b, n: (b * nt + n, 0)),
        out_shape=jax.ShapeDtypeStruct((bsz * seq, BRANCH_W), F32),
        compiler_params=_cparams(("parallel", "parallel")),
        name="swa_prompt",
    )(swa, swa, swa, lw["sinks"])


def _rotate(x, cos, sin):
    x1, x2 = x[:, 0:128], x[:, 128:256]
    return jnp.concatenate([x1 * cos - x2 * sin, x1 * sin + x2 * cos], axis=1)


def _ret_prompt_kernel(r_ref, cos_ref, sin_ref, dmat_ref, qdec_ref, kdec_ref, cdec_ref, g_ref, y_ref, s_out_ref, s_ref):
    n = pl.program_id(1)

    @pl.when(n == 0)
    def _():
        s_ref[...] = jnp.zeros_like(s_ref)

    c = CHUNK
    chunks = range(STEP_CHUNKS)
    hq = _lane_head(BRANCH_W, 32) % N_HEADS
    hv = _lane_head(BRANCH_W, HD)
    rows_head = (lax.broadcasted_iota(jnp.int32, (BRANCH_W, 1), 0) // 32) % N_HEADS
    qr, kr, v = [], [], []
    for i in chunks:
        rows = slice(i * c, (i + 1) * c)
        cos, sin = cos_ref[rows, :], sin_ref[rows, :]
        qr.append(_rotate(r_ref[rows, 0:256], cos, sin))
        kr.append(_rotate(r_ref[rows, 256:512], cos, sin) * (HD ** -0.5))
        v.append(r_ref[rows, 512:768].astype(BF16))
    att = [_dot_nt(_stack_heads(qr[i], hq).astype(BF16), kr[i].astype(BF16)) * dmat_ref[...] for i in chunks]
    upd = [_dot_tn((kr[i] * kdec_ref[...]).astype(BF16), v[i]) for i in chunks]
    states = [s_ref[...]]
    for i in chunks:
        states.append(cdec_ref[...] * states[i] + jnp.where(rows_head == hv, upd[i], 0.0))
    o = [_pick_heads(_dot(att[i].astype(BF16), v[i]), c, hv) for i in chunks]
    o = [o[i] + _dot((qr[i] * qdec_ref[...]).astype(BF16), states[i].astype(BF16)) for i in chunks]
    s_ref[...] = states[-1]
    s_out_ref[...] = states[-1]
    y_ref[...] = jax.nn.silu(r_ref[:, 768:1024]) * _head_norm(jnp.concatenate(o, axis=0), g_ref[...])


def _ret_tables(seq):
    c = CHUNK
    half = HD // 2
    freq = ROPE_BASE ** (-jnp.arange(half, dtype=F32) / half)
    ang = jnp.arange(seq).astype(F32)[:, None] * freq[None, :]
    cos = jnp.tile(jnp.cos(ang), (1, N_HEADS))
    sin = jnp.tile(jnp.sin(ang), (1, N_HEADS))
    log_g = jnp.log1p(-jnp.exp2(-5.0 - jnp.arange(N_HEADS, dtype=F32)))
    idx = jnp.arange(c, dtype=F32)
    rel = idx[:, None] - idx[None, :]
    dmat = jnp.exp(jnp.where(rel[None] >= 0, rel[None] * log_g[:, None, None], -jnp.inf)).reshape(N_HEADS * c, c)
    lane_head_q = (jnp.arange(BRANCH_W) // 32) % N_HEADS
    qdec = jnp.exp((idx + 1.0)[:, None] * log_g[lane_head_q][None, :])
    kdec = jnp.exp((c - 1.0 - idx)[:, None] * log_g[lane_head_q][None, :])
    cdec = jnp.exp(c * log_g)[jnp.arange(BRANCH_W) // HD][None, :]
    return cos, sin, dmat, qdec, kdec, cdec


def _ret_prompt(ret, lw, bsz, seq):
    c = CHUNK
    rows = STEP_CHUNKS * c
    nc = seq // rows
    cos, sin, dmat, qdec, kdec, cdec = _ret_tables(seq)
    full = lambda shape: pl.BlockSpec(shape, lambda b, n: (0,) * len(shape))
    return pl.pallas_call(
        _ret_prompt_kernel,
        grid=(bsz, nc),
        in_specs=[pl.BlockSpec((rows, W_RET), lambda b, n: (b * nc + n, 0)),
                  pl.BlockSpec((rows, 128), lambda b, n: (n, 0)), pl.BlockSpec((rows, 128), lambda b, n: (n, 0)),
                  full((N_HEADS * c, c)), full((c, BRANCH_W)), full((c, BRANCH_W)), full((1, BRANCH_W)),
                  full((1, BRANCH_W))],
        out_specs=[pl.BlockSpec((rows, BRANCH_W), lambda b, n: (b * nc + n, 0)),
                   pl.BlockSpec((None, BRANCH_W, BRANCH_W), lambda b, n: (b, 0, 0))],
        out_shape=[jax.ShapeDtypeStruct((bsz * seq, BRANCH_W), F32),
                   jax.ShapeDtypeStruct((bsz, BRANCH_W, BRANCH_W), F32)],
        scratch_shapes=[pltpu.VMEM((BRANCH_W, BRANCH_W), F32)],
        compiler_params=_cparams(("parallel", "arbitrary")),
        name="ret_prompt",
    )(ret, cos, sin, dmat, qdec, kdec, cdec, lw["ret_g"])


def _ret_state_from_blockdiag(s_bd):
    b = s_bd.shape[0]
    s = s_bd.reshape(b, 2, N_HEADS, 32, N_HEADS, HD)
    s = jnp.stack([s[:, :, h, :, h, :] for h in range(N_HEADS)], axis=1)
    return s.reshape(b, N_HEADS, HD, HD)


ML_I_LANE = GLA_RANK
ML_F_LANE = GLA_RANK + N_HEADS


def _ml_prompt_kernel(m_ref, sm_ref, g_ref, y_ref, c_out_ref, n_out_ref, m_out_ref, c_ref, n_ref, mm_ref):
    nchunk = pl.program_id(1)

    @pl.when(nchunk == 0)
    def _():
        c_ref[...] = jnp.zeros_like(c_ref)
        n_ref[...] = jnp.zeros_like(n_ref)
        mm_ref[...] = jnp.zeros_like(mm_ref)

    c = CHUNK
    chunks = range(STEP_CHUNKS)
    row = lax.broadcasted_iota(jnp.int32, (c, c), 0)
    col = lax.broadcasted_iota(jnp.int32, (c, c), 1)
    tri_lo = jnp.where(col <= row, 1.0, 0.0).astype(BF16)
    tri_up = jnp.where(row <= col, 1.0, 0.0).astype(BF16)
    hv = _lane_head(BRANCH_W, HD)
    hq = _lane_head(N_HEADS * c, c)
    key = lax.broadcasted_iota(jnp.int32, (c, N_HEADS * c), 0)
    qry = lax.broadcasted_iota(jnp.int32, (c, N_HEADS * c), 1) % c
    causal_t = key <= qry
    r = lax.broadcasted_iota(jnp.int32, (BRANCH_W, BRANCH_W), 0) // HD
    cc = lax.broadcasted_iota(jnp.int32, (BRANCH_W, BRANCH_W), 1) // HD
    same_head = r == cc
    lane = lax.broadcasted_iota(jnp.int32, (1, 128), 1)
    gate_lane = lax.broadcasted_iota(jnp.int32, (128, 1), 0)
    is_i = (gate_lane >= ML_I_LANE) & (gate_lane < ML_I_LANE + N_HEADS)
    is_f = (gate_lane >= ML_F_LANE) & (gate_lane < ML_F_LANE + N_HEADS)
    gate_head = jnp.where(is_i, gate_lane - ML_I_LANE, gate_lane - ML_F_LANE)
    sel_q = jnp.where((is_i | is_f) & (gate_head == hq), 1.0, 0.0).astype(BF16)
    sel_v = jnp.where((is_i | is_f) & (gate_head == hv), 1.0, 0.0).astype(BF16)
    sel_qv = jnp.concatenate([sel_q, sel_v], axis=1)
    lane128 = lax.broadcasted_iota(jnp.int32, (1, 128), 1)
    in_i = (lane128 >= ML_I_LANE) & (lane128 < ML_I_LANE + N_HEADS)
    in_f = (lane128 >= ML_F_LANE) & (lane128 < ML_F_LANE + N_HEADS)
    head_rows = lax.broadcasted_iota(jnp.int32, (8, 1), 0)

    q, ks, vt, brow, u_q, u_v, sc = [], [], [], [], [], [], []
    for i in chunks:
        rows = slice(i * c, (i + 1) * c)
        q.append(m_ref[rows, 0:256])
        ks.append(m_ref[rows, 256:512] * (HD ** -0.5))
        vt.append(m_ref[rows, 512:768].T.astype(BF16))
        sm = sm_ref[rows, :]
        bcols = _dot_exact_lhs(tri_lo, jax.nn.log_sigmoid(sm))
        brows = _dot_exact_rhs(jax.nn.log_sigmoid(sm.T), tri_up)
        brow.append(jnp.concatenate([brows[ML_F_LANE + h:ML_F_LANE + h + 1, :] for h in range(N_HEADS)], axis=1))
        z = jnp.where(in_i, sm, 0.0) - jnp.where(in_f, bcols, 0.0)
        u = _dot_exact_rhs(z, sel_qv)
        u_q.append(u[:, 0:N_HEADS * c])
        u_v.append(u[:, N_HEADS * c:])
        sc.append(_dot_nt(ks[i].astype(BF16), _stack_heads(q[i], hv).astype(BF16)))

    mm = mm_ref[...]
    qk, w_inter, rowsum, floor, khat, carry_b = [], [], [], [], [], []
    for i in chunks:
        m_prev = _spread_heads([mm[:, h:h + 1] for h in range(N_HEADS)], hq)
        d_log = jnp.where(causal_t, brow[i] + u_q[i], -jnp.inf)
        inter_log = brow[i] + m_prev
        m_t = jnp.maximum(inter_log, jnp.max(d_log, axis=0, keepdims=True))
        qk_i = sc[i] * jnp.exp(d_log - m_t)
        qk.append(qk_i.astype(BF16))
        rowsum.append(jnp.sum(qk_i, axis=0, keepdims=True))
        w_inter.append(jnp.exp(inter_log - m_t))
        floor.append(jnp.exp(-m_t))
        m_new = [m_t[:, (h + 1) * c - 1:(h + 1) * c] for h in range(N_HEADS)]
        b_last = [brow[i][:, (h + 1) * c - 1:(h + 1) * c] for h in range(N_HEADS)]
        carry_b.append(_spread_heads([jnp.exp(b_last[h] + mm[:, h:h + 1] - m_new[h]) for h in range(N_HEADS)], hv))
        khat.append(ks[i] * jnp.exp(u_v[i] + _spread_heads([b_last[h] - m_new[h] for h in range(N_HEADS)], hv)))
        mm = jnp.zeros((1, 128), F32)
        for h in range(N_HEADS):
            mm = jnp.where(lane == h, m_new[h], mm)

    intra = [_dot(vt[i], qk[i]) for i in chunks]
    upd = [_dot(vt[i], khat[i].astype(BF16)) for i in chunks]
    cms, nrows = [c_ref[...]], [n_ref[...]]
    for i in chunks:
        cms.append(carry_b[i] * cms[i] + jnp.where(same_head, upd[i], 0.0))
        nrows.append(carry_b[i] * nrows[i] + jnp.sum(khat[i], axis=0, keepdims=True))
    hout = []
    for i in chunks:
        qb = q[i].astype(BF16)
        inter = _dot_nt(cms[i].astype(BF16), qb)
        n_sel = jnp.where(head_rows == hv, nrows[i], 0.0)
        qn = sum(_dot_nt(p, qb) for p in _split3(n_sel))
        blocks = []
        for h in range(N_HEADS):
            cols = slice(h * c, (h + 1) * c)
            den = w_inter[i][:, cols] * qn[h:h + 1, :] + rowsum[i][:, cols]
            num = w_inter[i][:, cols] * inter[h * HD:(h + 1) * HD, :] + intra[i][h * HD:(h + 1) * HD, cols]
            blocks.append(num / jnp.maximum(jnp.abs(den), floor[i][:, cols]))
        hout.append(jnp.concatenate(blocks, axis=0).T)
    y_ref[...] = _head_norm(jax.nn.sigmoid(m_ref[:, 768:1024]) * jnp.concatenate(hout, axis=0), g_ref[...])

    c_ref[...] = cms[-1]
    n_ref[...] = nrows[-1]
    mm_ref[...] = mm
    c_out_ref[...] = cms[-1]
    n_out_ref[...] = nrows[-1]
    m_out_ref[...] = mm


def _ml_prompt(ml, small, lw, bsz, seq):
    rows = STEP_CHUNKS * CHUNK
    nc = seq // rows
    return pl.pallas_call(
        _ml_prompt_kernel,
        grid=(bsz, nc),
        in_specs=[pl.BlockSpec((rows, W_ML), lambda b, n: (b * nc + n, 0)),
                  pl.BlockSpec((rows, W_SMALL), lambda b, n: (b * nc + n, 0)),
                  pl.BlockSpec((1, BRANCH_W), lambda b, n: (0, 0))],
        out_specs=[pl.BlockSpec((rows, BRANCH_W), lambda b, n: (b * nc + n, 0)),
                   pl.BlockSpec((None, BRANCH_W, BRANCH_W), lambda b, n: (b, 0, 0)),
                   pl.BlockSpec((None, 1, BRANCH_W), lambda b, n: (b, 0, 0)),
                   pl.BlockSpec((None, 1, 128), lambda b, n: (b, 0, 0))],
        out_shape=[jax.ShapeDtypeStruct((bsz * seq, BRANCH_W), F32),
                   jax.ShapeDtypeStruct((bsz, BRANCH_W, BRANCH_W), F32),
                   jax.ShapeDtypeStruct((bsz, 1, BRANCH_W), F32),
                   jax.ShapeDtypeStruct((bsz, 1, 128), F32)],
        scratch_shapes=[pltpu.VMEM((BRANCH_W, BRANCH_W), F32), pltpu.VMEM((1, BRANCH_W), F32),
                        pltpu.VMEM((1, 128), F32)],
        compiler_params=_cparams(("parallel", "arbitrary")),
        name="ml_prompt",
    )(ml, small, lw["ml_g"])


def _diag_blocks(s_bd):
    b = s_bd.shape[0]
    s = s_bd.reshape(b, N_HEADS, HD, N_HEADS, HD)
    return jnp.stack([s[:, h, :, h, :] for h in range(N_HEADS)], axis=1)


GLA_TILE = 256
GLA_SUB = 16


def _gla_prompt_kernel(x_ref, sm_ref, wa2_ref, ba2_ref, g_ref, y_ref, s_out_ref, s_ref):
    n = pl.program_id(1)

    @pl.when(n == 0)
    def _():
        s_ref[...] = jnp.zeros_like(s_ref)

    tq, sc = GLA_TILE, GLA_SUB
    x = x_ref[...]
    q = x[:, 0:128] * (GLA_DK ** -0.5)
    k = x[:, 128:256]
    v = x[:, 256:512]
    log_a = jax.nn.log_sigmoid(_dot(sm_ref[...].astype(BF16), wa2_ref[...]) + ba2_ref[...]) / GLA_TAU
    r = lax.broadcasted_iota(jnp.int32, (tq, tq), 0)
    c = lax.broadcasted_iota(jnp.int32, (tq, tq), 1)
    same = (r // sc) == (c // sc)
    b = _dot_exact_lhs(jnp.where(same & (c <= r), 1.0, 0.0).astype(BF16), log_a)
    btot = _dot_exact_lhs(jnp.where(same, 1.0, 0.0).astype(BF16), log_a)
    qd = (q * jnp.exp(b)).astype(BF16)
    dec_t = jnp.exp(btot).T
    vb = v.astype(BF16)
    kh = lax.broadcasted_iota(jnp.int32, (128, 1), 0) // GLA_DK
    hv = _lane_head(BRANCH_W, HD)
    expand = jnp.where(kh == hv, 1.0, 0.0).astype(BF16)
    trow = lax.broadcasted_iota(jnp.int32, (sc, 1), 0)
    chunks = range(tq // sc)
    khat_b = (k * jnp.exp(btot - b)).astype(BF16)
    upds = [_dot_tn(khat_b[ci * sc:(ci + 1) * sc], vb[ci * sc:(ci + 1) * sc]) for ci in chunks]
    states = [s_ref[...]]
    for ci in chunks:
        states.append(dec_t[:, ci * sc:ci * sc + 1] * states[ci] + jnp.where(kh == hv, upds[ci], 0.0))
    outs = []
    for ci in chunks:
        lo = ci * sc
        qc, kc, bc, vc = q[lo:lo + sc], k[lo:lo + sc], b[lo:lo + sc], v[lo:lo + sc]
        pair = [qc * kc[j:j + 1] * jnp.exp(jnp.where(trow >= j, bc - bc[j:j + 1], -jnp.inf)) for j in range(sc)]
        att = _dot(jnp.concatenate(pair, axis=0).astype(BF16), expand)
        o = att[0:sc] * vc[0:1]
        for j in range(1, sc):
            o = o + att[j * sc:(j + 1) * sc] * vc[j:j + 1]
        outs.append(o)
    outs = [outs[ci] + _dot(qd[ci * sc:(ci + 1) * sc], states[ci].astype(BF16)) for ci in chunks]
    s_ref[...] = states[-1]
    s_out_ref[...] = states[-1]
    y_ref[...] = jax.nn.silu(x[:, 512:768]) * _head_norm(jnp.concatenate(outs, axis=0), g_ref[...])


def _gla_prompt(gla, small, lw, bsz, seq):
    tq = GLA_TILE
    nt = seq // tq
    full = lambda shape: pl.BlockSpec(shape, lambda b, n: (0,) * len(shape))
    return pl.pallas_call(
        _gla_prompt_kernel,
        grid=(bsz, nt),
        in_specs=[pl.BlockSpec((tq, W_GLA), lambda b, n: (b * nt + n, 0)),
                  pl.BlockSpec((tq, W_SMALL), lambda b, n: (b * nt + n, 0)),
                  full((128, 128)), full((1, 128)), full((1, BRANCH_W))],
        out_specs=[pl.BlockSpec((tq, BRANCH_W), lambda b, n: (b * nt + n, 0)),
                   pl.BlockSpec((None, 128, BRANCH_W), lambda b, n: (b, 0, 0))],
        out_shape=[jax.ShapeDtypeStruct((bsz * seq, BRANCH_W), F32),
                   jax.ShapeDtypeStruct((bsz, 128, BRANCH_W), F32)],
        scratch_shapes=[pltpu.VMEM((128, BRANCH_W), F32)],
        compiler_params=_cparams(("parallel", "arbitrary")),
        name="gla_prompt",
    )(gla, small, lw["wa2"], lw["ba2"], lw["gla_g"])


def _gla_state_from_blockdiag(s_bd):
    b = s_bd.shape[0]
    s = s_bd.reshape(b, N_HEADS, GLA_DK, N_HEADS, HD)
    return jnp.stack([s[:, h, :, h, :] for h in range(N_HEADS)], axis=1)


LANES = 128


def _rank1_update(st_ref, snt_ref, q_ref, k_ref, decay_row, v, dk):
    def body(i, o):
        rows = pl.ds(pl.multiple_of(i * HD, HD), HD)
        new = decay_row(i) * st_ref[rows, :] + k_ref[pl.ds(i, 1), :] * v
        snt_ref[rows, :] = new
        return o + q_ref[pl.ds(i, 1), :] * new
    return lax.fori_loop(0, dk, body, jnp.zeros((HD, LANES), F32), unroll=4)


def _head_rows(h, n):
    return pl.ds(pl.multiple_of(h * n, n), n)


def _gla_sample_kernel(x_ref, sm_ref, wa2_ref, ba2_ref, g_ref, s_ref, y_ref, s_out_ref,
                       qt_ref, kt_ref, at_ref, vt_ref, ot_ref):
    h = pl.program_id(0)

    @pl.when(h == 0)
    def _():
        x = x_ref[...]
        qt_ref[...] = (x[:, 0:128] * (GLA_DK ** -0.5)).T
        kt_ref[...] = x[:, 128:256].T
        vt_ref[...] = x[:, 256:512].T
        log_a = jax.nn.log_sigmoid(_dot(sm_ref[...].astype(BF16), wa2_ref[...]) + ba2_ref[...]) / GLA_TAU
        at_ref[...] = jnp.exp(log_a).T

    base = h * GLA_DK
    o = _rank1_update(s_ref, s_out_ref, qt_ref.at[_head_rows(h, GLA_DK)], kt_ref.at[_head_rows(h, GLA_DK)],
                      lambda i: at_ref[pl.ds(base + i, 1), :], vt_ref[_head_rows(h, HD), :], GLA_DK)
    ot_ref[_head_rows(h, HD), :] = o

    @pl.when(h == N_HEADS - 1)
    def _():
        y_ref[...] = jax.nn.silu(x_ref[:, 512:768]) * _head_norm(ot_ref[...].T, g_ref[...])


def _gla_sample(gla, small, state, lw):
    per_head = GLA_DK * HD
    li = lw["layer"]
    full = lambda shape: pl.BlockSpec(shape, lambda h: (0,) * len(shape))
    vm = lambda r: pltpu.VMEM((r, LANES), F32)
    return pl.pallas_call(
        _gla_sample_kernel,
        grid=(N_HEADS,),
        in_specs=[full((LANES, W_GLA)), full((LANES, W_SMALL)), full((128, 128)), full((1, 128)), full((1, BRANCH_W)),
                  pl.BlockSpec((None, per_head, LANES), lambda h: (li, h, 0))],
        out_specs=[full((LANES, BRANCH_W)), pl.BlockSpec((per_head, LANES), lambda h: (h, 0))],
        out_shape=[jax.ShapeDtypeStruct((LANES, BRANCH_W), F32), jax.ShapeDtypeStruct(state.shape[1:], F32)],
        scratch_shapes=[vm(128), vm(128), vm(128), vm(256), vm(256)],
        compiler_params=_cparams(("arbitrary",)),
        name="gla_sample",
    )(gla, small, lw["wa2"], lw["ba2"], lw["gla_g"], state)


def _ml_sample_kernel(x_ref, sc_ref, n_ref, g_ref, c_ref, y_ref, c_out_ref, n_out_ref, m_out_ref,
                      qt_ref, kt_ref, vt_ref, ot_ref, w_ref, kp_ref):
    h = pl.program_id(0)

    @pl.when(h == 0)
    def _():
        x = x_ref[...]
        qt_ref[...] = x[:, 0:256].T
        kt_ref[...] = (x[:, 256:512] * (HD ** -0.5)).T
        vt_ref[...] = x[:, 512:768].T
        sct = sc_ref[...].T
        log_f = jax.nn.log_sigmoid(sct[8:16])
        m_old = sct[16:24]
        m_new = jnp.maximum(log_f + m_old, sct[0:8])
        w_ref[0:8, :] = jnp.exp(log_f + m_old - m_new)
        w_ref[8:16, :] = jnp.exp(sct[0:8] - m_new)
        w_ref[16:24, :] = m_new

    w_carry = w_ref[pl.ds(h, 1), :]
    w_tok = w_ref[pl.ds(8 + h, 1), :]
    m_new = w_ref[pl.ds(16 + h, 1), :]
    kp_ref[...] = kt_ref[_head_rows(h, HD), :] * w_tok
    q_h = qt_ref.at[_head_rows(h, HD)]
    n_new = w_carry * n_ref[_head_rows(h, HD), :] + kp_ref[...]
    num = _rank1_update(c_ref, c_out_ref, q_h, kp_ref, lambda i: w_carry, vt_ref[_head_rows(h, HD), :], HD)
    den = jnp.sum(q_h[...] * n_new, axis=0, keepdims=True)
    ot_ref[_head_rows(h, HD), :] = num / jnp.maximum(jnp.abs(den), jnp.exp(-m_new))
    n_out_ref[_head_rows(h, HD), :] = n_new

    @pl.when(h == N_HEADS - 1)
    def _():
        y_ref[...] = _head_norm(jax.nn.sigmoid(x_ref[:, 768:1024]) * ot_ref[...].T, g_ref[...])
        m_out_ref[...] = jnp.concatenate([w_ref[16:24, :], jnp.zeros((LANES - 8, LANES), F32)], axis=0).T


def _ml_sample(ml, scal, n_state, c_state, lw):
    per_head = HD * HD
    li = lw["layer"]
    full = lambda shape: pl.BlockSpec(shape, lambda h: (0,) * len(shape))
    vm = lambda r: pltpu.VMEM((r, LANES), F32)
    return pl.pallas_call(
        _ml_sample_kernel,
        grid=(N_HEADS,),
        in_specs=[full((LANES, W_ML)), full((LANES, 128)),
                  pl.BlockSpec((None, BRANCH_W, LANES), lambda h: (li, 0, 0)), full((1, BRANCH_W)),
                  pl.BlockSpec((None, per_head, LANES), lambda h: (li, h, 0))],
        out_specs=[full((LANES, BRANCH_W)), pl.BlockSpec((per_head, LANES), lambda h: (h, 0)),
                   full((BRANCH_W, LANES)), full((LANES, 128))],
        out_shape=[jax.ShapeDtypeStruct((LANES, BRANCH_W), F32), jax.ShapeDtypeStruct(c_state.shape[1:], F32),
                   jax.ShapeDtypeStruct((BRANCH_W, LANES), F32), jax.ShapeDtypeStruct((LANES, 128), F32)],
        scratch_shapes=[vm(256), vm(256), vm(256), vm(256), vm(24), vm(HD)],
        compiler_params=_cparams(("arbitrary",)),
        name="ml_sample",
    )(ml, scal, n_state, lw["ml_g"], c_state)


def _ret_sample_kernel(x_ref, cos_ref, sin_ref, gam_ref, g_ref, s_ref, y_ref, s_out_ref,
                       qt_ref, kt_ref, vt_ref, ot_ref, qr_ref, kr_ref):
    h = pl.program_id(0)

    @pl.when(h == 0)
    def _():
        x = x_ref[...]
        qt_ref[...] = x[:, 0:256].T
        kt_ref[...] = (x[:, 256:512] * (HD ** -0.5)).T
        vt_ref[...] = x[:, 512:768].T

    cos, sin = cos_ref[...], sin_ref[...]
    for src, dst in ((qt_ref, qr_ref), (kt_ref, kr_ref)):
        x1 = src[_head_rows(h, 32), :]
        x2 = src[pl.ds(pl.multiple_of(128 + h * 32, 32), 32), :]
        dst[0:32, :] = x1 * cos - x2 * sin
        dst[32:64, :] = x1 * sin + x2 * cos
    gamma = gam_ref[pl.ds(h, 1), :]
    ot_ref[_head_rows(h, HD), :] = _rank1_update(s_ref, s_out_ref, qr_ref, kr_ref, lambda i: gamma,
                                                 vt_ref[_head_rows(h, HD), :], HD)

    @pl.when(h == N_HEADS - 1)
    def _():
        y_ref[...] = jax.nn.silu(x_ref[:, 768:1024]) * _head_norm(ot_ref[...].T, g_ref[...])


def _ret_sample(ret, state, lw):
    per_head = HD * HD
    li = lw["layer"]
    half = HD // 2
    freq = ROPE_BASE ** (-jnp.arange(half, dtype=F32) / half)
    ang = jnp.full((1,), PAST_LEN, F32)[:, None] * freq[None, :]
    cos = jnp.broadcast_to(jnp.cos(ang).reshape(half, 1), (half, LANES))
    sin = jnp.broadcast_to(jnp.sin(ang).reshape(half, 1), (half, LANES))
    log_g = jnp.log1p(-jnp.exp2(-5.0 - jnp.arange(N_HEADS, dtype=F32)))
    gam = jnp.broadcast_to(jnp.concatenate([jnp.exp(log_g), jnp.ones((4,), F32)]).reshape(8, 1), (8, LANES))
    full = lambda shape: pl.BlockSpec(shape, lambda h: (0,) * len(shape))
    vm = lambda r: pltpu.VMEM((r, LANES), F32)
    return pl.pallas_call(
        _ret_sample_kernel,
        grid=(N_HEADS,),
        in_specs=[full((LANES, W_RET)), full((half, LANES)), full((half, LANES)), full((8, LANES)), full((1, BRANCH_W)),
                  pl.BlockSpec((None, per_head, LANES), lambda h: (li, h, 0))],
        out_specs=[full((LANES, BRANCH_W)), pl.BlockSpec((per_head, LANES), lambda h: (h, 0))],
        out_shape=[jax.ShapeDtypeStruct((LANES, BRANCH_W), F32), jax.ShapeDtypeStruct(state.shape[1:], F32)],
        scratch_shapes=[vm(256), vm(256), vm(256), vm(256), vm(HD), vm(HD)],
        compiler_params=_cparams(("arbitrary",)),
        name="ret_sample",
    )(ret, cos, sin, gam, lw["ret_g"], state)


def _swa_sample_kernel(q_ref, kn_ref, vn_ref, knc_ref, vnc_ref, sink_ref, kc_ref, vc_ref, o_ref, ko_ref, vo_ref):
    q = q_ref[...]
    kc, vc = kc_ref[...], vc_ref[...]
    kn, vn = kn_ref[...], vn_ref[...]
    scale = HD ** -0.5
    s = _einsum("bhd,bdw->bhw", q.astype(BF16), kc.astype(BF16)) * scale
    s_new = jnp.sum(q * kn, axis=-1, keepdims=True) * scale
    w = lax.broadcasted_iota(jnp.int32, (1, 1, WINDOW), 2)
    s = jnp.where(w > 0, s, -jnp.inf)
    sink = sink_ref[...][:, 0:1]
    mx = jnp.maximum(jnp.maximum(jnp.max(s, axis=-1, keepdims=True), s_new), sink)
    p = jnp.exp(s - mx)
    p_new = jnp.exp(s_new - mx)
    den = jnp.sum(p, axis=-1, keepdims=True) + p_new + jnp.exp(sink - mx)
    o = _einsum("bhw,bdw->bhd", p.astype(BF16), vc.astype(BF16)) + p_new * vn
    o_ref[...] = o / den
    newest = w == WINDOW - 1
    ko_ref[...] = jnp.where(newest, knc_ref[...], pltpu.roll(kc, WINDOW - 1, 2))
    vo_ref[...] = jnp.where(newest, vnc_ref[...], pltpu.roll(vc, WINDOW - 1, 2))


def _swa_sample(swa, cache_k, cache_v, lw, bt=16):
    bsz = swa.shape[0]
    li = lw["layer"]
    cache = pl.BlockSpec((None, bt, WINDOW, 128), lambda i: (li, i, 0, 0))
    lo = (jnp.arange(128) < HD)[None, :]
    qa, qb = swa[:, 0:128], swa[:, 128:256]
    q = jnp.stack([jnp.where(lo, qa, 0.0), jnp.where(lo, qb, 0.0), jnp.where(lo, 0.0, qa), jnp.where(lo, 0.0, qb)], axis=1)
    kn = swa[:, None, 256:384]
    vn = swa[:, None, 384:512]
    sink = jnp.broadcast_to(lw["sinks"][0, 0:N_HEADS].reshape(N_HEADS, 1), (N_HEADS, 128))
    blk = lambda shape: pl.BlockSpec(shape, lambda i: (i,) + (0,) * (len(shape) - 1))
    o, ko, vo = pl.pallas_call(
        _swa_sample_kernel,
        grid=(bsz // bt,),
        in_specs=[blk((bt, N_HEADS, 128)), blk((bt, 1, 128)), blk((bt, 1, 128)), blk((bt, 128, 1)), blk((bt, 128, 1)),
                  pl.BlockSpec((N_HEADS, 128), lambda i: (0, 0)), cache, cache],
        out_specs=[blk((bt, N_HEADS, 128)), blk((bt, WINDOW, 128)), blk((bt, WINDOW, 128))],
        out_shape=[jax.ShapeDtypeStruct((bsz, N_HEADS, 128), F32), jax.ShapeDtypeStruct(cache_k.shape[1:], F32),
                   jax.ShapeDtypeStruct(cache_v.shape[1:], F32)],
        compiler_params=_cparams(("parallel",)),
        name="swa_sample",
    )(q, kn, vn, jnp.swapaxes(kn, 1, 2), jnp.swapaxes(vn, 1, 2), sink, cache_k, cache_v)
    y = jnp.concatenate([o[:, 0, 0:64], o[:, 2, 64:128], o[:, 1, 0:64], o[:, 3, 64:128]], axis=1)
    return y, ko, vo


def _ml_scalars(small, m_state):
    z4 = jnp.zeros((small.shape[0], 4), F32)
    return jnp.concatenate([small[:, ML_I_LANE:ML_I_LANE + 4], z4, small[:, ML_F_LANE:ML_F_LANE + 4], z4, m_state, z4,
                            jnp.zeros((small.shape[0], 128 - 24), F32)], axis=1)


IN_NAMES = ("gq", "gk", "gv", "gr", "ga", "sq", "sk", "sv", "mq", "mk", "mv", "mi", "mf", "mo", "rq", "rk", "rv", "rg",
            "gates")
IN_SIZES = (N_HEADS * GLA_DK, N_HEADS * GLA_DK, BRANCH_W, BRANCH_W, GLA_RANK, BRANCH_W, SWA_KV * HD, SWA_KV * HD,
            BRANCH_W, BRANCH_W, BRANCH_W, N_HEADS, N_HEADS, BRANCH_W, BRANCH_W, BRANCH_W, BRANCH_W, BRANCH_W,
            N_BRANCH * D_MODEL)
IN_START = {name: sum(IN_SIZES[:i]) for i, name in enumerate(IN_NAMES)}
GATE_COL = IN_START["gates"]


def _mixer_rows(w_t):
    x = w_t.shape[1]
    at = lambda name, n: w_t[IN_START[name]:IN_START[name] + n]
    swa_q = at("sq", BRANCH_W).reshape(2, 2, HD, x).transpose(1, 0, 2, 3).reshape(BRANCH_W, x)
    halves = lambda name: at(name, BRANCH_W).reshape(N_HEADS, 2, HD // 2, x).transpose(1, 0, 2, 3).reshape(BRANCH_W, x)
    parts = [at("gq", W_GLA), swa_q, at("sk", 2 * SWA_KV * HD), at("mq", 3 * BRANCH_W), at("mo", BRANCH_W),
             halves("rq"), halves("rk"), at("rv", 2 * BRANCH_W), at("ga", GLA_RANK), at("mi", 2 * N_HEADS)]
    pad = W_MIX - sum(p.shape[0] for p in parts)
    return jnp.concatenate(parts + [jnp.zeros((pad, x), w_t.dtype)], axis=0)


def _prep_layer(li, w_in, b_in, gla_w_a2, gla_b_a2, gla_norm_g, swa_sinks, ml_norm_g, ret_norm_g, w_branch, w_out,
                ln1_g, ln1_b, w_router_grp, b_router_grp, w_router_exp, b_router_exp, w_exp_gate, w_exp_up,
                w_exp_down, ln2_g, ln2_b):
    row = lambda v: v.reshape(1, -1).astype(F32)
    wr = jnp.concatenate([w_router_grp[li], w_router_exp[li],
                          jnp.zeros((D_MODEL, 128 - N_GROUPS - N_EXPERTS), F32)], axis=1)
    wr_hi = wr.astype(BF16)
    wb = w_branch[li]
    wb1 = jnp.concatenate([wb[1, 0:64], wb[1, 128:192], wb[1, 64:128], wb[1, 192:256]], axis=0)
    w_t = w_in[li].T
    return dict(
        wm=_mixer_rows(w_t).astype(BF16),
        bm=row(_mixer_rows(b_in[li][:, None])),
        wg=w_t[GATE_COL:].astype(BF16),
        bg=row(b_in[li][GATE_COL:]),
        wa2=jnp.concatenate([gla_w_a2[li], jnp.zeros((128 - GLA_RANK, 128), F32)], axis=0).astype(BF16),
        ba2=row(gla_b_a2[li]),
        gla_g=row(gla_norm_g[li]), ml_g=row(ml_norm_g[li]), ret_g=row(ret_norm_g[li]),
        sinks=row(jnp.concatenate([swa_sinks[li], jnp.zeros((128 - N_HEADS,), F32)])),
        wbr=jnp.stack([wb[0], wb1, wb[2], wb[3]]).astype(BF16),
        wout=w_out[li].astype(BF16),
        ln1_g=row(ln1_g[li]), ln1_b=row(ln1_b[li]), ln2_g=row(ln2_g[li]), ln2_b=row(ln2_b[li]),
        wr_hi=wr_hi, wr_lo=(wr - wr_hi.astype(F32)).astype(BF16),
        br=row(jnp.concatenate([b_router_grp[li], b_router_exp[li], jnp.zeros((128 - N_GROUPS - N_EXPERTS,), F32)])),
        layer=li, we_gate=w_exp_gate, we_up=w_exp_up, we_down=w_exp_down,
    )


TOK_TILE = 256
MOE_PIECES_PROMPT = 32
MOE_PIECES_SAMPLE = 1


MERGE_TILES = 2


def _ffn_half(x2, ys, lw, tm, bp):
    x1, route, cnt, xs = _merge(x2, ys, lw, tm, MERGE_TILES if x2.shape[0] // tm % MERGE_TILES == 0 else 1)
    return _moe(x1, route, cnt, xs, lw, bp, tm)


def _layer_prompt(x2, lw, bsz, seq):
    gla, swa, ml, ret, small = _inproj(x2, lw["wm"], lw["bm"], 2 * TOK_TILE)
    ya, s_gla = _gla_prompt(gla, small, lw, bsz, seq)
    yb = _swa_prompt(swa, lw, bsz, seq)
    yc, c_bd, n_row, m_row = _ml_prompt(ml, small, lw, bsz, seq)
    yd, s_ret = _ret_prompt(ret, lw, bsz, seq)
    kv_tail = swa.reshape(bsz, seq, W_SWA)[:, seq - WINDOW:, 256:512]
    states = (_gla_state_from_blockdiag(s_gla),
              kv_tail[:, :, 0:128].reshape(bsz, WINDOW, SWA_KV, HD),
              kv_tail[:, :, 128:256].reshape(bsz, WINDOW, SWA_KV, HD),
              jnp.swapaxes(_diag_blocks(c_bd), -1, -2), n_row.reshape(bsz, N_HEADS, HD), m_row[:, 0, 0:N_HEADS],
              _ret_state_from_blockdiag(s_ret))
    return _ffn_half(x2, (ya, yb, yc, yd), lw, TOK_TILE, MOE_PIECES_PROMPT), states


def _layer_sample(x2, st, lw):
    s_gla, k_buf, v_buf, c_ml, n_ml, m_ml, s_ret = st
    bsz = x2.shape[0]
    depth = s_gla.shape[0]
    gla, swa, ml, ret, small = _inproj(x2, lw["wm"], lw["bm"], bsz)
    to_lanes = lambda s: jnp.moveaxis(s, 1, -1).reshape(depth, -1, bsz)
    from_lanes = lambda s, like: jnp.moveaxis(s.reshape(like.shape[2:] + (bsz,)), -1, 0)
    cache_t = lambda c: jnp.swapaxes(c.reshape(depth, bsz, WINDOW, SWA_KV * HD), 2, 3)
    ya, s_gla_new = _gla_sample(gla, small, to_lanes(s_gla), lw)
    yb, k_new, v_new = _swa_sample(swa, cache_t(k_buf), cache_t(v_buf), lw)
    k_new, v_new = jnp.swapaxes(k_new, 1, 2), jnp.swapaxes(v_new, 1, 2)
    yc, c_new, n_new, m_new = _ml_sample(ml, _ml_scalars(small, m_ml[lw["layer"]]), to_lanes(n_ml), to_lanes(c_ml), lw)
    yd, s_ret_new = _ret_sample(ret, to_lanes(s_ret), lw)
    states = (from_lanes(s_gla_new, s_gla), k_new.reshape(k_buf.shape[1:]), v_new.reshape(v_buf.shape[1:]),
              from_lanes(c_new, c_ml), from_lanes(n_new, n_ml), m_new[:, 0:N_HEADS], from_lanes(s_ret_new, s_ret))
    return _ffn_half(x2, (ya, yb, yc, yd), lw, bsz, MOE_PIECES_SAMPLE), states


def kernel(x_prompt, x_sample, state_gla, cache_swa_k, cache_swa_v, state_mlstm_c, state_mlstm_n, state_mlstm_m,
           state_ret, w_in, b_in, gla_w_a2, gla_b_a2, gla_norm_g, swa_sinks, ml_norm_g, ret_norm_g, w_branch, w_out,
           ln1_g, ln1_b, w_router_grp, b_router_grp, w_router_exp, b_router_exp, w_exp_gate, w_exp_up, w_exp_down,
           ln2_g, ln2_b):
    bsz, seq, _ = x_prompt.shape
    dec_b, dec_seq, _ = x_sample.shape
    assert dec_seq == 1 and dec_b == LANES and (bsz * seq) % (2 * TOK_TILE) == 0
    assert all(seq % n == 0 for n in (GLA_TILE, SWA_BLOCKS * WINDOW, STEP_CHUNKS * CHUNK))
    weights = (w_in, b_in, gla_w_a2, gla_b_a2, gla_norm_g, swa_sinks, ml_norm_g, ret_norm_g, w_branch, w_out,
               ln1_g, ln1_b, w_router_grp, b_router_grp, w_router_exp, b_router_exp, w_exp_gate, w_exp_up,
               w_exp_down, ln2_g, ln2_b)
    y_p = x_prompt.reshape(bsz * seq, D_MODEL)
    y_s = x_sample.reshape(dec_b, D_MODEL)
    new_p, new_s = [], []
    for li in range(w_in.shape[0]):
        lw = _prep_layer(li, *weights)
        y_p, st_p = _layer_prompt(y_p, lw, bsz, seq)
        st_in = (state_gla, cache_swa_k, cache_swa_v, state_mlstm_c, state_mlstm_n, state_mlstm_m, state_ret)
        y_s, st_s = _layer_sample(y_s, st_in, lw)
        new_p.append(st_p)
        new_s.append(st_s)
    outs = [y_p.reshape(bsz, seq, D_MODEL), y_s.reshape(dec_b, dec_seq, D_MODEL)]
    for i in range(7):
        outs.append(jnp.stack([s[i] for s in new_p], axis=0))
        outs.append(jnp.stack([s[i] for s in new_s], axis=0))
    return tuple(outs)
```

```python
import functools

import jax
import jax.numpy as jnp
from jax import lax
from jax.experimental import pallas as pl
from jax.experimental.pallas import tpu as pltpu

F32 = jnp.float32
BF16 = jnp.bfloat16

D_MODEL = 1024
N_BRANCH = 4
BRANCH_W = D_MODEL // N_BRANCH
N_HEADS = 4
GLA_DK = 32
GLA_RANK = 16
GLA_TAU = 16.0
HD = 64
SWA_KV = 2
WINDOW = 128
ROPE_BASE = 10000.0
PAST_LEN = 16384
N_GROUPS = 4
EPG = 8
N_EXPERTS = N_GROUPS * EPG
D_EXPERT = D_MODEL // 4
DEPTH = 2
ALPHA = (2 * DEPTH) ** 0.25
LN_EPS = 1e-5
VMEM_LIMIT = 56 * 1024 * 1024

W_GLA, W_SWA, W_ML, W_RET, W_SMALL = 768, 512, 1024, 1024, 128
W_MIX = W_GLA + W_SWA + W_ML + W_RET + W_SMALL


def _cparams(sem):
    return pltpu.CompilerParams(dimension_semantics=sem, vmem_limit_bytes=VMEM_LIMIT)


def _dot(a, b):
    return jnp.dot(a, b, preferred_element_type=F32)


def _einsum(spec, a, b):
    return jnp.einsum(spec, a, b, preferred_element_type=F32)


def _split2(x):
    hi = x.astype(BF16)
    lo = (x - hi.astype(F32)).astype(BF16)
    return hi, lo


def _split3(x):
    hi = x.astype(BF16)
    r = x - hi.astype(F32)
    mid = r.astype(BF16)
    lo = (r - mid.astype(F32)).astype(BF16)
    return hi, mid, lo


def _dot_exact_lhs(a_exact, b):
    return sum(_dot(a_exact, p) for p in _split3(b))


def _dot_exact_rhs(a, b_exact):
    return sum(_dot(p, b_exact) for p in _split3(a))


def _layer_norm(x, g, b):
    mu = jnp.mean(x, axis=-1, keepdims=True)
    xc = x - mu
    var = jnp.mean(xc * xc, axis=-1, keepdims=True)
    return xc * lax.rsqrt(var + LN_EPS) * g + b


def _inproj_kernel(x_ref, w_ref, b_ref, gla_ref, swa_ref, ml_ref, ret_ref, sm_ref):
    xb = x_ref[...].astype(BF16)
    off = 0
    for ref, width in ((gla_ref, W_GLA), (swa_ref, W_SWA), (ml_ref, W_ML), (ret_ref, W_RET), (sm_ref, W_SMALL)):
        ref[...] = _dot_nt(xb, w_ref[off:off + width, :]) + b_ref[:, off:off + width]
        off += width


def _inproj(x2, wm, bm, tm):
    t = x2.shape[0]
    widths = (W_GLA, W_SWA, W_ML, W_RET, W_SMALL)
    return pl.pallas_call(
        _inproj_kernel,
        grid=(t // tm,),
        in_specs=[pl.BlockSpec((tm, D_MODEL), lambda i: (i, 0)),
                  pl.BlockSpec((W_MIX, D_MODEL), lambda i: (0, 0)),
                  pl.BlockSpec((1, W_MIX), lambda i: (0, 0))],
        out_specs=[pl.BlockSpec((tm, w), lambda i: (i, 0)) for w in widths],
        out_shape=[jax.ShapeDtypeStruct((t, w), F32) for w in widths],
        compiler_params=_cparams(("parallel",)),
        name="inproj",
    )(x2, wm, bm)


def _merge_kernel(x_ref, ya_ref, yb_ref, yc_ref, yd_ref, wg_ref, bg_ref, wbr_ref, wout_ref, g1_ref, b1_ref,
                  wrh_ref, wrl_ref, br_ref, x1_ref, route_ref, cnt_ref, xs_ref, *, tm, sub):
    rows_t = xs_ref.shape[0] // sub
    x1, picks = [], []
    for t in range(sub):
        rows = slice(t * tm, (t + 1) * tm)
        x = x_ref[rows, :]
        xb = x.astype(BF16)
        merged = jnp.zeros((tm, D_MODEL), F32)
        for n, y_ref in enumerate((ya_ref, yb_ref, yc_ref, yd_ref)):
            cols = slice(n * D_MODEL, (n + 1) * D_MODEL)
            gate = 0.5 * jnp.tanh(0.5 * (_dot_nt(xb, wg_ref[cols, :]) + bg_ref[:, cols])) + 0.5
            merged = merged + gate * _dot(y_ref[rows, :].astype(BF16), wbr_ref[n])
        mix = _dot(merged.astype(BF16), wout_ref[...])
        x1.append(_layer_norm(ALPHA * x + mix, g1_ref[...], b1_ref[...]))
        x1_ref[rows, :] = x1[t]
        picks.append(_route(x1[t], wrh_ref[...], wrl_ref[...], br_ref[...], tm))
    for t in range(sub):
        route, cnt, xs = _sort(x1[t], picks[t], tm, rows_t)
        route_ref[t * tm:(t + 1) * tm, :] = route
        cnt_ref[t] = cnt
        xs_ref[t * rows_t:(t + 1) * rows_t, :] = xs


def _route(x1, wr_hi, wr_lo, br, tm):
    xh, xl = _split2(x1)
    both = _dot(xh, jnp.concatenate([wr_hi, wr_lo], axis=1))
    logits = both[:, 0:128] + both[:, 128:256] + _dot(xl, wr_hi) + br
    lane = lax.broadcasted_iota(jnp.int32, (tm, 128), 1)
    lanef = lane.astype(F32)
    neg = jnp.float32(-jnp.inf)
    big = jnp.float32(1e9)
    isg = lane < N_GROUPS
    gmax = jnp.max(jnp.where(isg, logits, neg), axis=-1, keepdims=True)
    gidx = jnp.min(jnp.where(isg & (logits == gmax), lanef, big), axis=-1, keepdims=True)
    p_group = 1.0 / jnp.sum(jnp.where(isg, jnp.exp(logits - gmax), 0.0), axis=-1, keepdims=True)
    lo = N_GROUPS + EPG * gidx
    em = (lanef >= lo) & (lanef < lo + EPG)
    v1 = jnp.max(jnp.where(em, logits, neg), axis=-1, keepdims=True)
    i1 = jnp.min(jnp.where(em & (logits == v1), lanef, big), axis=-1, keepdims=True)
    em2 = em & (lanef != i1)
    v2 = jnp.max(jnp.where(em2, logits, neg), axis=-1, keepdims=True)
    i2 = jnp.min(jnp.where(em2 & (logits == v2), lanef, big), axis=-1, keepdims=True)
    e21 = jnp.exp(v2 - v1)
    gt1 = 1.0 / (1.0 + e21)
    return i1, i2, p_group * gt1, p_group * (e21 * gt1)


def _sort(x1, picks, tm, rows_t):
    i1, i2, gate1, gate2 = picks
    lane = lax.broadcasted_iota(jnp.int32, (tm, 128), 1)
    lanef = lane.astype(F32)
    sel1 = lanef == i1
    sel2 = lanef == i2
    onehot = jnp.where(sel1 | sel2, 1.0, 0.0)
    cnt = jnp.sum(onehot, axis=0, keepdims=True)
    seg = jnp.floor((cnt + (PIECE - 1)) * (1.0 / PIECE)) * PIECE
    er = lax.broadcasted_iota(jnp.int32, (128, 128), 0)
    ec = lax.broadcasted_iota(jnp.int32, (128, 128), 1)
    seg_off = _dot(jnp.broadcast_to(seg, (8, 128)).astype(BF16), jnp.where(er < ec, 1.0, 0.0).astype(BF16))[0:1]
    row = lax.broadcasted_iota(jnp.int32, (tm, tm), 0)
    col = lax.broadcasted_iota(jnp.int32, (tm, tm), 1)
    earlier = _dot(jnp.where(col < row, 1.0, 0.0).astype(BF16), onehot.astype(BF16))
    where_to = seg_off + earlier
    w1 = jnp.where(sel1, where_to, 0.0)
    w2 = jnp.where(sel2, where_to, 0.0)
    pos1 = jnp.sum(w1, axis=-1, keepdims=True)
    pos2 = jnp.sum(w2, axis=-1, keepdims=True)
    ones = jnp.ones((8, 128), BF16)
    prow1 = sum(_dot_nt(ones, p) for p in _split2(w1))[0:1]
    prow2 = sum(_dot_nt(ones, p) for p in _split2(w2))[0:1]
    r = lax.broadcasted_iota(jnp.int32, (rows_t, tm), 0).astype(F32)
    place = jnp.where((r == prow1) | (r == prow2), 1.0, 0.0).astype(BF16)
    xs = _dot(place, x1.astype(BF16)).astype(BF16)

    vals = (i1 - N_GROUPS, i2 - N_GROUPS, gate1, gate2, pos1, pos2)
    route = jnp.zeros((tm, 128), F32)
    for k, v in enumerate(vals):
        route = jnp.where(lane == k, v, route)
    return route, cnt, xs


PIECE = 16


def _tile_rows(tm):
    return -(-(2 * tm + N_EXPERTS * (PIECE - 1)) // 128) * 128


def _merge(x2, ys, lw, tm, sub):
    t = x2.shape[0]
    nt = t // tm
    rows_t = _tile_rows(tm)
    tok = lambda w: pl.BlockSpec((sub * tm, w), lambda i: (i, 0))
    full = lambda shape: pl.BlockSpec(shape, lambda i: (0,) * len(shape), pipeline_mode=pl.Buffered(1))
    return pl.pallas_call(
        functools.partial(_merge_kernel, tm=tm, sub=sub),
        grid=(nt // sub,),
        in_specs=[tok(D_MODEL), tok(BRANCH_W), tok(BRANCH_W), tok(BRANCH_W), tok(BRANCH_W),
                  full((N_BRANCH * D_MODEL, D_MODEL)), full((1, N_BRANCH * D_MODEL)),
                  full((N_BRANCH, BRANCH_W, D_MODEL)), full((D_MODEL, D_MODEL)),
                  full((1, D_MODEL)), full((1, D_MODEL)),
                  full((D_MODEL, 128)), full((D_MODEL, 128)), full((1, 128))],
        out_specs=[tok(D_MODEL), tok(128), pl.BlockSpec((sub, 1, 128), lambda i: (i, 0, 0)),
                   pl.BlockSpec((sub * rows_t, D_MODEL), lambda i: (i, 0))],
        out_shape=[jax.ShapeDtypeStruct((t, D_MODEL), F32), jax.ShapeDtypeStruct((t, 128), F32),
                   jax.ShapeDtypeStruct((nt, 1, 128), F32), jax.ShapeDtypeStruct((nt * rows_t, D_MODEL), BF16)],
        compiler_params=_cparams(("parallel",)),
        name="merge",
    )(x2, *ys, lw["wg"], lw["bg"], lw["wbr"], lw["wout"], lw["ln1_g"], lw["ln1_b"], lw["wr_hi"], lw["wr_lo"], lw["br"])


def _piece_copy(hbm, hbm_row, vmem, piece, sem, to_vmem):
    h = hbm.at[pl.ds(pl.multiple_of(hbm_row, PIECE), PIECE), :]
    v = vmem.at[pl.ds(pl.multiple_of(piece * PIECE, PIECE), PIECE), :]
    return pltpu.make_async_copy(h, v, sem) if to_vmem else pltpu.make_async_copy(v, h, sem)


def _expert_kernel(be_ref, nu_ref, nv_ref, off_ref, xs_hbm, wg_ref, wu_ref, wd_ref, ys_hbm, xin, yout, sem_in, sem_out,
                   *, bp):
    del xs_hbm
    blk = pl.program_id(0)
    n_used = nu_ref[0]
    slot = blk % 2

    def pieces(b, s, buf, sem, to_vmem, start):
        n = nv_ref[b]

        def one(p, priority):
            cp = _piece_copy(ys_hbm, off_ref[b * bp + p], buf.at[s], p, sem.at[s], to_vmem)
            cp.start(priority=priority) if start else cp.wait()

        def body(i, c):
            one(2 * i, 0)

            @pl.when(2 * i + 1 < n)
            def _():
                one(2 * i + 1, 1)
            return c
        lax.fori_loop(0, (n + 1) // 2, body, 0)

    @pl.when(blk == 0)
    def _():
        xin[...] = jnp.zeros_like(xin)

        @pl.when(n_used > 0)
        def _():
            pieces(0, 0, xin, sem_in, True, True)

    @pl.when(blk + 1 < n_used)
    def _():
        pieces(blk + 1, 1 - slot, xin, sem_in, True, True)

    @pl.when(blk < n_used)
    def _():
        pieces(blk, slot, xin, sem_in, True, False)

        @pl.when(blk >= 2)
        def _():
            pieces(blk - 2, slot, yout, sem_out, False, False)

        xb = xin[slot]
        hid = jax.nn.silu(_dot(xb, wg_ref[...].astype(BF16))) * _dot(xb, wu_ref[...].astype(BF16))
        yout[slot] = _dot(hid.astype(BF16), wd_ref[...].astype(BF16)).astype(BF16)
        pieces(blk, slot, yout, sem_out, False, True)

    @pl.when(blk == n_used - 1)
    def _():
        @pl.when(blk >= 1)
        def _():
            pieces(blk - 1, 1 - slot, yout, sem_out, False, False)
        pieces(blk, slot, yout, sem_out, False, False)


def _experts(xs, block_expert, n_used, n_valid, piece_off, lw, bp):
    n_blocks = block_expert.shape[0]
    rows = bp * PIECE
    li = lw["layer"]
    wspec = lambda shape: pl.BlockSpec((None, None) + shape, lambda b, be, nu, nv, off: (li, be[b], 0, 0))
    grid_spec = pltpu.PrefetchScalarGridSpec(
        num_scalar_prefetch=4,
        grid=(n_blocks,),
        in_specs=[pl.BlockSpec(memory_space=pl.ANY), wspec((D_MODEL, D_EXPERT)), wspec((D_MODEL, D_EXPERT)),
                  wspec((D_EXPERT, D_MODEL))],
        out_specs=pl.BlockSpec(memory_space=pl.ANY),
        scratch_shapes=[pltpu.VMEM((2, rows, D_MODEL), BF16), pltpu.VMEM((2, rows, D_MODEL), BF16),
                        pltpu.SemaphoreType.DMA((2,)), pltpu.SemaphoreType.DMA((2,))],
    )
    return pl.pallas_call(
        functools.partial(_expert_kernel, bp=bp),
        grid_spec=grid_spec,
        out_shape=jax.ShapeDtypeStruct(xs.shape, BF16),
        input_output_aliases={4: 0},
        compiler_params=_cparams(("arbitrary",)),
        name="experts",
    )(block_expert, n_used, n_valid, piece_off, xs, lw["we_gate"], lw["we_up"], lw["we_down"])


COMBINE_TILES = 4


def _combine_kernel(ys_ref, x1_ref, route_ref, g2_ref, b2_ref, out_ref, *, tm, sub):
    rows_t = ys_ref.shape[0] // sub
    r = lax.broadcasted_iota(jnp.int32, (tm, rows_t), 1).astype(F32)
    routes = [route_ref[t * tm:(t + 1) * tm, :] for t in range(sub)]
    picks = [[jnp.where(r == routes[t][:, 4 + j:5 + j], 1.0, 0.0).astype(BF16) for j in range(2)] for t in range(sub)]
    rows = [[_dot(picks[t][j], ys_ref[t * rows_t:(t + 1) * rows_t, :]) for j in range(2)] for t in range(sub)]
    for t in range(sub):
        ffn = routes[t][:, 2:3] * rows[t][0] + routes[t][:, 3:4] * rows[t][1]
        tok = slice(t * tm, (t + 1) * tm)
        out_ref[tok, :] = _layer_norm(ALPHA * x1_ref[tok, :] + ffn, g2_ref[...], b2_ref[...])


def _combine(ys, x1, route, lw, tm):
    t = x1.shape[0]
    rows_t = _tile_rows(tm)
    sub = COMBINE_TILES if (t // tm) % COMBINE_TILES == 0 else 1
    full = lambda shape: pl.BlockSpec(shape, lambda i: (0,) * len(shape))
    return pl.pallas_call(
        functools.partial(_combine_kernel, tm=tm, sub=sub),
        grid=(t // (sub * tm),),
        in_specs=[pl.BlockSpec((sub * rows_t, D_MODEL), lambda i: (i, 0)),
                  pl.BlockSpec((sub * tm, D_MODEL), lambda i: (i, 0)),
                  pl.BlockSpec((sub * tm, 128), lambda i: (i, 0)), full((1, D_MODEL)), full((1, D_MODEL))],
        out_specs=pl.BlockSpec((sub * tm, D_MODEL), lambda i: (i, 0)),
        out_shape=jax.ShapeDtypeStruct((t, D_MODEL), F32),
        compiler_params=_cparams(("parallel",)),
        name="combine",
    )(ys, x1, route, lw["ln2_g"], lw["ln2_b"])


def _piece_tables(cnt, tm, bp):
    nt = cnt.shape[0]
    rows_t = _tile_rows(tm)
    counts = cnt[:, 0, N_GROUPS:N_GROUPS + N_EXPERTS].astype(jnp.int32)
    pcs = ((counts + PIECE - 1) // PIECE).T
    seg_row = ((jnp.cumsum(pcs, axis=0) - pcs) * PIECE + jnp.arange(nt, dtype=jnp.int32)[None, :] * rows_t)
    tile_end = jnp.cumsum(pcs, axis=1)
    total = tile_end[:, -1]
    e_end = jnp.cumsum((total + bp - 1) // bp * bp)
    e_start = jnp.concatenate([jnp.zeros((1,), jnp.int32), e_end[:-1]])
    n_blocks = -(-(nt * rows_t // PIECE + N_EXPERTS * (bp - 1)) // bp)
    g = jnp.arange(n_blocks * bp, dtype=jnp.int32)[:, None]
    in_e = ((e_start[None, :] <= g) & (g < e_end[None, :])).astype(F32)
    pick = lambda table: jnp.dot(in_e, table.astype(F32), precision=lax.Precision.HIGHEST)
    local = g.astype(F32) - pick(e_start[:, None])
    ends, starts, rows = pick(tile_end), pick(tile_end - pcs), pick(seg_row)
    in_t = ((starts <= local) & (local < ends)).astype(F32)
    off = jnp.sum(in_t * (rows + (local - starts) * PIECE), axis=1).astype(jnp.int32)
    n_valid = jnp.sum(jnp.sum(in_t, axis=1).reshape(n_blocks, bp), axis=1).astype(jnp.int32)
    expert = pick(jnp.arange(N_EXPERTS, dtype=jnp.int32)[:, None])[:, 0]
    block_expert = expert.reshape(n_blocks, bp)[:, 0].astype(jnp.int32)
    n_used = (e_end[-1] // bp).astype(jnp.int32).reshape(1)
    return block_expert, n_used, n_valid, off


def _moe(x1, route, cnt, xs, lw, bp, tm):
    ys = _experts(xs, *_piece_tables(cnt, tm, bp), lw, bp)
    return _combine(ys, x1, route, lw, tm)


CHUNK = 128
STEP_CHUNKS = 16


def _dot_nt(a, b):
    return lax.dot_general(a, b, (((1,), (1,)), ((), ())), preferred_element_type=F32)


def _dot_tn(a, b):
    return lax.dot_general(a, b, (((0,), (0,)), ((), ())), preferred_element_type=F32)


def _lane_head(width, group):
    return lax.broadcasted_iota(jnp.int32, (1, width), 1) // group


def _head_norm(o, gain):
    r = lax.broadcasted_iota(jnp.int32, (BRANCH_W, BRANCH_W), 0) // HD
    c = lax.broadcasted_iota(jnp.int32, (BRANCH_W, BRANCH_W), 1) // HD
    avg = jnp.where(r == c, 1.0 / HD, 0.0).astype(BF16)
    mu = _dot_exact_rhs(o, avg)
    xc = o - mu
    var = _dot_exact_rhs(xc * xc, avg)
    return xc * lax.rsqrt(var + LN_EPS) * gain


def _stack_heads(x, head_of_lane):
    return jnp.concatenate([jnp.where(head_of_lane == h, x, 0.0) for h in range(N_HEADS)], axis=0)


def _pick_heads(stacked, rows, head_of_lane):
    out = jnp.where(head_of_lane == 0, stacked[0:rows], 0.0)
    for h in range(1, N_HEADS):
        out = jnp.where(head_of_lane == h, stacked[h * rows:(h + 1) * rows], out)
    return out


def _spread_heads(cols, head_of_lane):
    out = jnp.where(head_of_lane == 0, cols[0], 0.0)
    for h in range(1, N_HEADS):
        out = jnp.where(head_of_lane == h, cols[h], out)
    return out


SWA_BLOCKS = 8


def _swa_prompt_kernel(q_ref, kv_ref, kvp_ref, sink_ref, y_ref):
    n = pl.program_id(1)
    lo = lax.broadcasted_iota(jnp.int32, (1, 128), 1) < HD
    e = lax.broadcasted_iota(jnp.int32, (2 * WINDOW, N_HEADS * WINDOW), 0)
    col = lax.broadcasted_iota(jnp.int32, (2 * WINDOW, N_HEADS * WINDOW), 1)
    a = col % WINDOW
    band = (e > a) & (e <= a + WINDOW)
    head = lax.broadcasted_iota(jnp.int32, (1, N_HEADS * WINDOW), 1) // WINDOW
    sink = _spread_heads([sink_ref[:, h:h + 1] for h in range(N_HEADS)], head)
    blocks = range(SWA_BLOCKS)
    scores, vexts = [], []
    for i in blocks:
        rows = slice(i * WINDOW, (i + 1) * WINDOW)
        q = q_ref[rows, :]
        qa, qb = q[:, 0:128], q[:, 128:256]
        kv = kv_ref[rows, :]
        kvp = kvp_ref[...] if i == 0 else kv_ref[(i - 1) * WINDOW:i * WINDOW, :]
        kext = jnp.concatenate([kvp[:, 0:128], kv[:, 0:128]], axis=0).astype(BF16)
        vexts.append(jnp.concatenate([kvp[:, 128:256], kv[:, 128:256]], axis=0).astype(BF16))
        qs = jnp.concatenate([jnp.where(lo, qa, 0.0), jnp.where(lo, qb, 0.0),
                              jnp.where(lo, 0.0, qa), jnp.where(lo, 0.0, qb)], axis=0).astype(BF16)
        scores.append(_dot_nt(kext, qs))
    probs, inv_den = [], []
    for i in blocks:
        valid = band & ((n > 0) | (e >= WINDOW)) if i == 0 else band
        s = jnp.where(valid, scores[i] * (HD ** -0.5), -jnp.inf)
        mx = jnp.maximum(jnp.max(s, axis=0, keepdims=True), sink)
        p = jnp.exp(s - mx)
        inv_den.append(1.0 / (jnp.sum(p, axis=0, keepdims=True) + jnp.exp(sink - mx)))
        probs.append(p.astype(BF16))
    outs = [_dot_tn(vexts[i], probs[i]) for i in blocks]
    for i in blocks:
        o = outs[i] * inv_den[i]
        ya = jnp.concatenate([o[0:HD, 0:128], o[HD:128, 256:384]], axis=0).T
        yb = jnp.concatenate([o[0:HD, 128:256], o[HD:128, 384:512]], axis=0).T
        y_ref[i * WINDOW:(i + 1) * WINDOW, :] = jnp.concatenate([ya, yb], axis=1)


def _swa_prompt(swa, lw, bsz, seq):
    qt = SWA_BLOCKS * WINDOW
    nt = seq // qt
    return pl.pallas_call(
        _swa_prompt_kernel,
        grid=(bsz, nt),
        in_specs=[pl.BlockSpec((qt, 256), lambda b, n: (b * nt + n, 0)),
                  pl.BlockSpec((qt, 256), lambda b, n: (b * nt + n, 1)),
                  pl.BlockSpec((WINDOW, 256), lambda b, n: (jnp.maximum((b * nt + n) * SWA_BLOCKS - 1, 0), 1)),
                  pl.BlockSpec((1, 128), lambda b, n: (0, 0))],
        out_specs=pl.BlockSpec((qt, BRANCH_W), lambda b, n: (b * nt + n, 0)),
        out_shape=jax.ShapeDtypeStruct((bsz * seq, BRANCH_W), F32),
        compiler_params=_cparams(("parallel", "parallel")),
        name="swa_prompt",
    )(swa, swa, swa, lw["sinks"])


def _rotate(x, cos, sin):
    x1, x2 = x[:, 0:128], x[:, 128:256]
    return jnp.concatenate([x1 * cos - x2 * sin, x1 * sin + x2 * cos], axis=1)


def _ret_prompt_kernel(r_ref, cos_ref, sin_ref, dmat_ref, qdec_ref, kdec_ref, cdec_ref, g_ref, y_ref, s_out_ref, s_ref):
    n = pl.program_id(1)

    @pl.when(n == 0)
    def _():
        s_ref[...] = jnp.zeros_like(s_ref)

    c = CHUNK
    chunks = range(STEP_CHUNKS)
    hq = _lane_head(BRANCH_W, 32) % N_HEADS
    hv = _lane_head(BRANCH_W, HD)
    rows_head = (lax.broadcasted_iota(jnp.int32, (BRANCH_W, 1), 0) // 32) % N_HEADS
    qr, kr, v = [], [], []
    for i in chunks:
        rows = slice(i * c, (i + 1) * c)
        cos, sin = cos_ref[rows, :], sin_ref[rows, :]
        qr.append(_rotate(r_ref[rows, 0:256], cos, sin))
        kr.append(_rotate(r_ref[rows, 256:512], cos, sin) * (HD ** -0.5))
        v.append(r_ref[rows, 512:768].astype(BF16))
    att = [_dot_nt(_stack_heads(qr[i], hq).astype(BF16), kr[i].astype(BF16)) * dmat_ref[...] for i in chunks]
    upd = [_dot_tn((kr[i] * kdec_ref[...]).astype(BF16), v[i]) for i in chunks]
    states = [s_ref[...]]
    for i in chunks:
        states.append(cdec_ref[...] * states[i] + jnp.where(rows_head == hv, upd[i], 0.0))
    o = [_pick_heads(_dot(att[i].astype(BF16), v[i]), c, hv) for i in chunks]
    o = [o[i] + _dot((qr[i] * qdec_ref[...]).astype(BF16), states[i].astype(BF16)) for i in chunks]
    s_ref[...] = states[-1]
    s_out_ref[...] = states[-1]
    y_ref[...] = jax.nn.silu(r_ref[:, 768:1024]) * _head_norm(jnp.concatenate(o, axis=0), g_ref[...])


def _ret_tables(seq):
    c = CHUNK
    half = HD // 2
    freq = ROPE_BASE ** (-jnp.arange(half, dtype=F32) / half)
    ang = jnp.arange(seq).astype(F32)[:, None] * freq[None, :]
    cos = jnp.tile(jnp.cos(ang), (1, N_HEADS))
    sin = jnp.tile(jnp.sin(ang), (1, N_HEADS))
    log_g = jnp.log1p(-jnp.exp2(-5.0 - jnp.arange(N_HEADS, dtype=F32)))
    idx = jnp.arange(c, dtype=F32)
    rel = idx[:, None] - idx[None, :]
    dmat = jnp.exp(jnp.where(rel[None] >= 0, rel[None] * log_g[:, None, None], -jnp.inf)).reshape(N_HEADS * c, c)
    lane_head_q = (jnp.arange(BRANCH_W) // 32) % N_HEADS
    qdec = jnp.exp((idx + 1.0)[:, None] * log_g[lane_head_q][None, :])
    kdec = jnp.exp((c - 1.0 - idx)[:, None] * log_g[lane_head_q][None, :])
    cdec = jnp.exp(c * log_g)[jnp.arange(BRANCH_W) // HD][None, :]
    return cos, sin, dmat, qdec, kdec, cdec


def _ret_prompt(ret, lw, bsz, seq):
    c = CHUNK
    rows = STEP_CHUNKS * c
    nc = seq // rows
    cos, sin, dmat, qdec, kdec, cdec = _ret_tables(seq)
    full = lambda shape: pl.BlockSpec(shape, lambda b, n: (0,) * len(shape))
    return pl.pallas_call(
        _ret_prompt_kernel,
        grid=(bsz, nc),
        in_specs=[pl.BlockSpec((rows, W_RET), lambda b, n: (b * nc + n, 0)),
                  pl.BlockSpec((rows, 128), lambda b, n: (n, 0)), pl.BlockSpec((rows, 128), lambda b, n: (n, 0)),
                  full((N_HEADS * c, c)), full((c, BRANCH_W)), full((c, BRANCH_W)), full((1, BRANCH_W)),
                  full((1, BRANCH_W))],
        out_specs=[pl.BlockSpec((rows, BRANCH_W), lambda b, n: (b * nc + n, 0)),
                   pl.BlockSpec((None, BRANCH_W, BRANCH_W), lambda b, n: (b, 0, 0))],
        out_shape=[jax.ShapeDtypeStruct((bsz * seq, BRANCH_W), F32),
                   jax.ShapeDtypeStruct((bsz, BRANCH_W, BRANCH_W), F32)],
        scratch_shapes=[pltpu.VMEM((BRANCH_W, BRANCH_W), F32)],
        compiler_params=_cparams(("parallel", "arbitrary")),
        name="ret_prompt",
    )(ret, cos, sin, dmat, qdec, kdec, cdec, lw["ret_g"])


def _ret_state_from_blockdiag(s_bd):
    b = s_bd.shape[0]
    s = s_bd.reshape(b, 2, N_HEADS, 32, N_HEADS, HD)
    s = jnp.stack([s[:, :, h, :, h, :] for h in range(N_HEADS)], axis=1)
    return s.reshape(b, N_HEADS, HD, HD)


ML_I_LANE = GLA_RANK
ML_F_LANE = GLA_RANK + N_HEADS


def _ml_prompt_kernel(m_ref, sm_ref, g_ref, y_ref, c_out_ref, n_out_ref, m_out_ref, c_ref, n_ref, mm_ref):
    nchunk = pl.program_id(1)

    @pl.when(nchunk == 0)
    def _():
        c_ref[...] = jnp.zeros_like(c_ref)
        n_ref[...] = jnp.zeros_like(n_ref)
        mm_ref[...] = jnp.zeros_like(mm_ref)

    c = CHUNK
    chunks = range(STEP_CHUNKS)
    row = lax.broadcasted_iota(jnp.int32, (c, c), 0)
    col = lax.broadcasted_iota(jnp.int32, (c, c), 1)
    tri_lo = jnp.where(col <= row, 1.0, 0.0).astype(BF16)
    tri_up = jnp.where(row <= col, 1.0, 0.0).astype(BF16)
    hv = _lane_head(BRANCH_W, HD)
    hq = _lane_head(N_HEADS * c, c)
    key = lax.broadcasted_iota(jnp.int32, (c, N_HEADS * c), 0)
    qry = lax.broadcasted_iota(jnp.int32, (c, N_HEADS * c), 1) % c
    causal_t = key <= qry
    r = lax.broadcasted_iota(jnp.int32, (BRANCH_W, BRANCH_W), 0) // HD
    cc = lax.broadcasted_iota(jnp.int32, (BRANCH_W, BRANCH_W), 1) // HD
    same_head = r == cc
    lane = lax.broadcasted_iota(jnp.int32, (1, 128), 1)
    gate_lane = lax.broadcasted_iota(jnp.int32, (128, 1), 0)
    is_i = (gate_lane >= ML_I_LANE) & (gate_lane < ML_I_LANE + N_HEADS)
    is_f = (gate_lane >= ML_F_LANE) & (gate_lane < ML_F_LANE + N_HEADS)
    gate_head = jnp.where(is_i, gate_lane - ML_I_LANE, gate_lane - ML_F_LANE)
    sel_q = jnp.where((is_i | is_f) & (gate_head == hq), 1.0, 0.0).astype(BF16)
    sel_v = jnp.where((is_i | is_f) & (gate_head == hv), 1.0, 0.0).astype(BF16)
    sel_qv = jnp.concatenate([sel_q, sel_v], axis=1)
    lane128 = lax.broadcasted_iota(jnp.int32, (1, 128), 1)
    in_i = (lane128 >= ML_I_LANE) & (lane128 < ML_I_LANE + N_HEADS)
    in_f = (lane128 >= ML_F_LANE) & (lane128 < ML_F_LANE + N_HEADS)
    head_rows = lax.broadcasted_iota(jnp.int32, (8, 1), 0)

    q, ks, vt, brow, u_q, u_v, sc = [], [], [], [], [], [], []
    for i in chunks:
        rows = slice(i * c, (i + 1) * c)
        q.append(m_ref[rows, 0:256])
        ks.append(m_ref[rows, 256:512] * (HD ** -0.5))
        vt.append(m_ref[rows, 512:768].T.astype(BF16))
        sm = sm_ref[rows, :]
        bcols = _dot_exact_lhs(tri_lo, jax.nn.log_sigmoid(sm))
        brows = _dot_exact_rhs(jax.nn.log_sigmoid(sm.T), tri_up)
        brow.append(jnp.concatenate([brows[ML_F_LANE + h:ML_F_LANE + h + 1, :] for h in range(N_HEADS)], axis=1))
        z = jnp.where(in_i, sm, 0.0) - jnp.where(in_f, bcols, 0.0)
        u = _dot_exact_rhs(z, sel_qv)
        u_q.append(u[:, 0:N_HEADS * c])
        u_v.append(u[:, N_HEADS * c:])
        sc.append(_dot_nt(ks[i].astype(BF16), _stack_heads(q[i], hv).astype(BF16)))

    mm = mm_ref[...]
    qk, w_inter, rowsum, floor, khat, carry_b = [], [], [], [], [], []
    for i in chunks:
        m_prev = _spread_heads([mm[:, h:h + 1] for h in range(N_HEADS)], hq)
        d_log = jnp.where(causal_t, brow[i] + u_q[i], -jnp.inf)
        inter_log = brow[i] + m_prev
        m_t = jnp.maximum(inter_log, jnp.max(d_log, axis=0, keepdims=True))
        qk_i = sc[i] * jnp.exp(d_log - m_t)
        qk.append(qk_i.astype(BF16))
        rowsum.append(jnp.sum(qk_i, axis=0, keepdims=True))
        w_inter.append(jnp.exp(inter_log - m_t))
        floor.append(jnp.exp(-m_t))
        m_new = [m_t[:, (h + 1) * c - 1:(h + 1) * c] for h in range(N_HEADS)]
        b_last = [brow[i][:, (h + 1) * c - 1:(h + 1) * c] for h in range(N_HEADS)]
        carry_b.append(_spread_heads([jnp.exp(b_last[h] + mm[:, h:h + 1] - m_new[h]) for h in range(N_HEADS)], hv))
        khat.append(ks[i] * jnp.exp(u_v[i] + _spread_heads([b_last[h] - m_new[h] for h in range(N_HEADS)], hv)))
        mm = jnp.zeros((1, 128), F32)
        for h in range(N_HEADS):
            mm = jnp.where(lane == h, m_new[h], mm)

    intra = [_dot(vt[i], qk[i]) for i in chunks]
    upd = [_dot(vt[i], khat[i].astype(BF16)) for i in chunks]
    cms, nrows = [c_ref[...]], [n_ref[...]]
    for i in chunks:
        cms.append(carry_b[i] * cms[i] + jnp.where(same_head, upd[i], 0.0))
        nrows.append(carry_b[i] * nrows[i] + jnp.sum(khat[i], axis=0, keepdims=True))
    hout = []
    for i in chunks:
        qb = q[i].astype(BF16)
        inter = _dot_nt(cms[i].astype(BF16), qb)
        n_sel = jnp.where(head_rows == hv, nrows[i], 0.0)
        qn = sum(_dot_nt(p, qb) for p in _split3(n_sel))
        blocks = []
        for h in range(N_HEADS):
            cols = slice(h * c, (h + 1) * c)
            den = w_inter[i][:, cols] * qn[h:h + 1, :] + rowsum[i][:, cols]
            num = w_inter[i][:, cols] * inter[h * HD:(h + 1) * HD, :] + intra[i][h * HD:(h + 1) * HD, cols]
            blocks.append(num / jnp.maximum(jnp.abs(den), floor[i][:, cols]))
        hout.append(jnp.concatenate(blocks, axis=0).T)
    y_ref[...] = _head_norm(jax.nn.sigmoid(m_ref[:, 768:1024]) * jnp.concatenate(hout, axis=0), g_ref[...])

    c_ref[...] = cms[-1]
    n_ref[...] = nrows[-1]
    mm_ref[...] = mm
    c_out_ref[...] = cms[-1]
    n_out_ref[...] = nrows[-1]
    m_out_ref[...] = mm


def _ml_prompt(ml, small, lw, bsz, seq):
    rows = STEP_CHUNKS * CHUNK
    nc = seq // rows
    return pl.pallas_call(
        _ml_prompt_kernel,
        grid=(bsz, nc),
        in_specs=[pl.BlockSpec((rows, W_ML), lambda b, n: (b * nc + n, 0)),
                  pl.BlockSpec((rows, W_SMALL), lambda b, n: (b * nc + n, 0)),
                  pl.BlockSpec((1, BRANCH_W), lambda b, n: (0, 0))],
        out_specs=[pl.BlockSpec((rows, BRANCH_W), lambda b, n: (b * nc + n, 0)),
                   pl.BlockSpec((None, BRANCH_W, BRANCH_W), lambda b, n: (b, 0, 0)),
                   pl.BlockSpec((None, 1, BRANCH_W), lambda b, n: (b, 0, 0)),
                   pl.BlockSpec((None, 1, 128), lambda b, n: (b, 0, 0))],
        out_shape=[jax.ShapeDtypeStruct((bsz * seq, BRANCH_W), F32),
                   jax.ShapeDtypeStruct((bsz, BRANCH_W, BRANCH_W), F32),
                   jax.ShapeDtypeStruct((bsz, 1, BRANCH_W), F32),
                   jax.ShapeDtypeStruct((bsz, 1, 128), F32)],
        scratch_shapes=[pltpu.VMEM((BRANCH_W, BRANCH_W), F32), pltpu.VMEM((1, BRANCH_W), F32),
                        pltpu.VMEM((1, 128), F32)],
        compiler_params=_cparams(("parallel", "arbitrary")),
        name="ml_prompt",
    )(ml, small, lw["ml_g"])


def _diag_blocks(s_bd):
    b = s_bd.shape[0]
    s = s_bd.reshape(b, N_HEADS, HD, N_HEADS, HD)
    return jnp.stack([s[:, h, :, h, :] for h in range(N_HEADS)], axis=1)


GLA_TILE = 256
GLA_STEP_TILES = 2
GLA_SUB = 16


def _gla_prompt_kernel(x_ref, sm_ref, wa2_ref, ba2_ref, g_ref, y_ref, s_out_ref, s_ref):
    n = pl.program_id(1)

    @pl.when(n == 0)
    def _():
        s_ref[...] = jnp.zeros_like(s_ref)

    tq, sc = GLA_TILE, GLA_SUB
    r = lax.broadcasted_iota(jnp.int32, (tq, tq), 0)
    c = lax.broadcasted_iota(jnp.int32, (tq, tq), 1)
    same = (r // sc) == (c // sc)
    cum_in_sub = jnp.where(same & (c <= r), 1.0, 0.0).astype(BF16)
    sum_in_sub = jnp.where(same, 1.0, 0.0).astype(BF16)
    kh = lax.broadcasted_iota(jnp.int32, (128, 1), 0) // GLA_DK
    hv = _lane_head(BRANCH_W, HD)
    expand = jnp.where(kh == hv, 1.0, 0.0).astype(BF16)
    trow = lax.broadcasted_iota(jnp.int32, (sc, 1), 0)
    n_sub = tq // sc
    chunks = range(GLA_STEP_TILES * n_sub)

    q, k, v, b, qd, dec_t, upds = [], [], [], [], [], [], []
    for t in range(GLA_STEP_TILES):
        rows = slice(t * tq, (t + 1) * tq)
        q.append(x_ref[rows, 0:128] * (GLA_DK ** -0.5))
        k.append(x_ref[rows, 128:256])
        v.append(x_ref[rows, 256:512])
        log_a = jax.nn.log_sigmoid(_dot(sm_ref[rows, :].astype(BF16), wa2_ref[...]) + ba2_ref[...]) / GLA_TAU
        b.append(_dot_exact_lhs(cum_in_sub, log_a))
        btot = _dot_exact_lhs(sum_in_sub, log_a)
        qd.append((q[t] * jnp.exp(b[t])).astype(BF16))
        dec_t.append(jnp.exp(btot).T)
        khat_b = (k[t] * jnp.exp(btot - b[t])).astype(BF16)
        vb = v[t].astype(BF16)
        upds += [_dot_tn(khat_b[ci * sc:(ci + 1) * sc], vb[ci * sc:(ci + 1) * sc]) for ci in range(n_sub)]
    states = [s_ref[...]]
    for g in chunks:
        t, lo = g // n_sub, (g % n_sub) * sc
        states.append(dec_t[t][:, lo:lo + 1] * states[g] + jnp.where(kh == hv, upds[g], 0.0))
    outs = []
    for g in chunks:
        t, lo = g // n_sub, (g % n_sub) * sc
        qc, kc, bc, vc = q[t][lo:lo + sc], k[t][lo:lo + sc], b[t][lo:lo + sc], v[t][lo:lo + sc]
        pair = [qc * kc[j:j + 1] * jnp.exp(jnp.where(trow >= j, bc - bc[j:j + 1], -jnp.inf)) for j in range(sc)]
        att = _dot(jnp.concatenate(pair, axis=0).astype(BF16), expand)
        o = att[0:sc] * vc[0:1]
        for j in range(1, sc):
            o = o + att[j * sc:(j + 1) * sc] * vc[j:j + 1]
        outs.append(o)
    outs = [outs[g] + _dot(qd[g // n_sub][(g % n_sub) * sc:(g % n_sub + 1) * sc], states[g].astype(BF16)) for g in chunks]
    s_ref[...] = states[-1]
    s_out_ref[...] = states[-1]
    y_ref[...] = jax.nn.silu(x_ref[:, 512:768]) * _head_norm(jnp.concatenate(outs, axis=0), g_ref[...])


def _gla_prompt(gla, small, lw, bsz, seq):
    tq = GLA_STEP_TILES * GLA_TILE
    nt = seq // tq
    full = lambda shape: pl.BlockSpec(shape, lambda b, n: (0,) * len(shape))
    return pl.pallas_call(
        _gla_prompt_kernel,
        grid=(bsz, nt),
        in_specs=[pl.BlockSpec((tq, W_GLA), lambda b, n: (b * nt + n, 0)),
                  pl.BlockSpec((tq, W_SMALL), lambda b, n: (b * nt + n, 0)),
                  full((128, 128)), full((1, 128)), full((1, BRANCH_W))],
        out_specs=[pl.BlockSpec((tq, BRANCH_W), lambda b, n: (b * nt + n, 0)),
                   pl.BlockSpec((None, 128, BRANCH_W), lambda b, n: (b, 0, 0))],
        out_shape=[jax.ShapeDtypeStruct((bsz * seq, BRANCH_W), F32),
                   jax.ShapeDtypeStruct((bsz, 128, BRANCH_W), F32)],
        scratch_shapes=[pltpu.VMEM((128, BRANCH_W), F32)],
        compiler_params=_cparams(("parallel", "arbitrary")),
        name="gla_prompt",
    )(gla, small, lw["wa2"], lw["ba2"], lw["gla_g"])


def _gla_state_from_blockdiag(s_bd):
    b = s_bd.shape[0]
    s = s_bd.reshape(b, N_HEADS, GLA_DK, N_HEADS, HD)
    return jnp.stack([s[:, h, :, h, :] for h in range(N_HEADS)], axis=1)


LANES = 128


def _rank1_update(st_ref, snt_ref, q_ref, k_ref, decay_row, v, dk):
    def body(i, o):
        rows = pl.ds(pl.multiple_of(i * HD, HD), HD)
        new = decay_row(i) * st_ref[rows, :] + k_ref[pl.ds(i, 1), :] * v
        snt_ref[rows, :] = new
        return o + q_ref[pl.ds(i, 1), :] * new
    return lax.fori_loop(0, dk, body, jnp.zeros((HD, LANES), F32), unroll=4)


def _head_rows(h, n):
    return pl.ds(pl.multiple_of(h * n, n), n)


def _gla_sample_kernel(x_ref, sm_ref, wa2_ref, ba2_ref, g_ref, s_ref, y_ref, s_out_ref,
                       qt_ref, kt_ref, at_ref, vt_ref, ot_ref):
    h = pl.program_id(0)

    @pl.when(h == 0)
    def _():
        x = x_ref[...]
        qt_ref[...] = (x[:, 0:128] * (GLA_DK ** -0.5)).T
        kt_ref[...] = x[:, 128:256].T
        vt_ref[...] = x[:, 256:512].T
        log_a = jax.nn.log_sigmoid(_dot(sm_ref[...].astype(BF16), wa2_ref[...]) + ba2_ref[...]) / GLA_TAU
        at_ref[...] = jnp.exp(log_a).T

    base = h * GLA_DK
    o = _rank1_update(s_ref, s_out_ref, qt_ref.at[_head_rows(h, GLA_DK)], kt_ref.at[_head_rows(h, GLA_DK)],
                      lambda i: at_ref[pl.ds(base + i, 1), :], vt_ref[_head_rows(h, HD), :], GLA_DK)
    ot_ref[_head_rows(h, HD), :] = o

    @pl.when(h == N_HEADS - 1)
    def _():
        y_ref[...] = jax.nn.silu(x_ref[:, 512:768]) * _head_norm(ot_ref[...].T, g_ref[...])


def _gla_sample(gla, small, state, lw):
    per_head = GLA_DK * HD
    li = lw["layer"]
    full = lambda shape: pl.BlockSpec(shape, lambda h: (0,) * len(shape))
    vm = lambda r: pltpu.VMEM((r, LANES), F32)
    return pl.pallas_call(
        _gla_sample_kernel,
        grid=(N_HEADS,),
        in_specs=[full((LANES, W_GLA)), full((LANES, W_SMALL)), full((128, 128)), full((1, 128)), full((1, BRANCH_W)),
                  pl.BlockSpec((None, per_head, LANES), lambda h: (li, h, 0))],
        out_specs=[full((LANES, BRANCH_W)), pl.BlockSpec((per_head, LANES), lambda h: (h, 0))],
        out_shape=[jax.ShapeDtypeStruct((LANES, BRANCH_W), F32), jax.ShapeDtypeStruct(state.shape[1:], F32)],
        scratch_shapes=[vm(128), vm(128), vm(128), vm(256), vm(256)],
        compiler_params=_cparams(("arbitrary",)),
        name="gla_sample",
    )(gla, small, lw["wa2"], lw["ba2"], lw["gla_g"], state)


def _ml_sample_kernel(x_ref, sc_ref, n_ref, g_ref, c_ref, y_ref, c_out_ref, n_out_ref, m_out_ref,
                      qt_ref, kt_ref, vt_ref, ot_ref, w_ref, kp_ref):
    h = pl.program_id(0)

    @pl.when(h == 0)
    def _():
        x = x_ref[...]
        qt_ref[...] = x[:, 0:256].T
        kt_ref[...] = (x[:, 256:512] * (HD ** -0.5)).T
        vt_ref[...] = x[:, 512:768].T
        sct = sc_ref[...].T
        log_f = jax.nn.log_sigmoid(sct[8:16])
        m_old = sct[16:24]
        m_new = jnp.maximum(log_f + m_old, sct[0:8])
        w_ref[0:8, :] = jnp.exp(log_f + m_old - m_new)
        w_ref[8:16, :] = jnp.exp(sct[0:8] - m_new)
        w_ref[16:24, :] = m_new

    w_carry = w_ref[pl.ds(h, 1), :]
    w_tok = w_ref[pl.ds(8 + h, 1), :]
    m_new = w_ref[pl.ds(16 + h, 1), :]
    kp_ref[...] = kt_ref[_head_rows(h, HD), :] * w_tok
    q_h = qt_ref.at[_head_rows(h, HD)]
    n_new = w_carry * n_ref[_head_rows(h, HD), :] + kp_ref[...]
    num = _rank1_update(c_ref, c_out_ref, q_h, kp_ref, lambda i: w_carry, vt_ref[_head_rows(h, HD), :], HD)
    den = jnp.sum(q_h[...] * n_new, axis=0, keepdims=True)
    ot_ref[_head_rows(h, HD), :] = num / jnp.maximum(jnp.abs(den), jnp.exp(-m_new))
    n_out_ref[_head_rows(h, HD), :] = n_new

    @pl.when(h == N_HEADS - 1)
    def _():
        y_ref[...] = _head_norm(jax.nn.sigmoid(x_ref[:, 768:1024]) * ot_ref[...].T, g_ref[...])
        m_out_ref[...] = jnp.concatenate([w_ref[16:24, :], jnp.zeros((LANES - 8, LANES), F32)], axis=0).T


def _ml_sample(ml, scal, n_state, c_state, lw):
    per_head = HD * HD
    li = lw["layer"]
    full = lambda shape: pl.BlockSpec(shape, lambda h: (0,) * len(shape))
    vm = lambda r: pltpu.VMEM((r, LANES), F32)
    return pl.pallas_call(
        _ml_sample_kernel,
        grid=(N_HEADS,),
        in_specs=[full((LANES, W_ML)), full((LANES, 128)),
                  pl.BlockSpec((None, BRANCH_W, LANES), lambda h: (li, 0, 0)), full((1, BRANCH_W)),
                  pl.BlockSpec((None, per_head, LANES), lambda h: (li, h, 0))],
        out_specs=[full((LANES, BRANCH_W)), pl.BlockSpec((per_head, LANES), lambda h: (h, 0)),
                   full((BRANCH_W, LANES)), full((LANES, 128))],
        out_shape=[jax.ShapeDtypeStruct((LANES, BRANCH_W), F32), jax.ShapeDtypeStruct(c_state.shape[1:], F32),
                   jax.ShapeDtypeStruct((BRANCH_W, LANES), F32), jax.ShapeDtypeStruct((LANES, 128), F32)],
        scratch_shapes=[vm(256), vm(256), vm(256), vm(256), vm(24), vm(HD)],
        compiler_params=_cparams(("arbitrary",)),
        name="ml_sample",
    )(ml, scal, n_state, lw["ml_g"], c_state)


def _ret_sample_kernel(x_ref, cos_ref, sin_ref, gam_ref, g_ref, s_ref, y_ref, s_out_ref,
                       qt_ref, kt_ref, vt_ref, ot_ref, qr_ref, kr_ref):
    h = pl.program_id(0)

    @pl.when(h == 0)
    def _():
        x = x_ref[...]
        qt_ref[...] = x[:, 0:256].T
        kt_ref[...] = (x[:, 256:512] * (HD ** -0.5)).T
        vt_ref[...] = x[:, 512:768].T

    cos, sin = cos_ref[...], sin_ref[...]
    for src, dst in ((qt_ref, qr_ref), (kt_ref, kr_ref)):
        x1 = src[_head_rows(h, 32), :]
        x2 = src[pl.ds(pl.multiple_of(128 + h * 32, 32), 32), :]
        dst[0:32, :] = x1 * cos - x2 * sin
        dst[32:64, :] = x1 * sin + x2 * cos
    gamma = gam_ref[pl.ds(h, 1), :]
    ot_ref[_head_rows(h, HD), :] = _rank1_update(s_ref, s_out_ref, qr_ref, kr_ref, lambda i: gamma,
                                                 vt_ref[_head_rows(h, HD), :], HD)

    @pl.when(h == N_HEADS - 1)
    def _():
        y_ref[...] = jax.nn.silu(x_ref[:, 768:1024]) * _head_norm(ot_ref[...].T, g_ref[...])


def _ret_sample(ret, state, lw):
    per_head = HD * HD
    li = lw["layer"]
    half = HD // 2
    freq = ROPE_BASE ** (-jnp.arange(half, dtype=F32) / half)
    ang = jnp.full((1,), PAST_LEN, F32)[:, None] * freq[None, :]
    cos = jnp.broadcast_to(jnp.cos(ang).reshape(half, 1), (half, LANES))
    sin = jnp.broadcast_to(jnp.sin(ang).reshape(half, 1), (half, LANES))
    log_g = jnp.log1p(-jnp.exp2(-5.0 - jnp.arange(N_HEADS, dtype=F32)))
    gam = jnp.broadcast_to(jnp.concatenate([jnp.exp(log_g), jnp.ones((4,), F32)]).reshape(8, 1), (8, LANES))
    full = lambda shape: pl.BlockSpec(shape, lambda h: (0,) * len(shape))
    vm = lambda r: pltpu.VMEM((r, LANES), F32)
    return pl.pallas_call(
        _ret_sample_kernel,
        grid=(N_HEADS,),
        in_specs=[full((LANES, W_RET)), full((half, LANES)), full((half, LANES)), full((8, LANES)), full((1, BRANCH_W)),
                  pl.BlockSpec((None, per_head, LANES), lambda h: (li, h, 0))],
        out_specs=[full((LANES, BRANCH_W)), pl.BlockSpec((per_head, LANES), lambda h: (h, 0))],
        out_shape=[jax.ShapeDtypeStruct((LANES, BRANCH_W), F32), jax.ShapeDtypeStruct(state.shape[1:], F32)],
        scratch_shapes=[vm(256), vm(256), vm(256), vm(256), vm(HD), vm(HD)],
        compiler_params=_cparams(("arbitrary",)),
        name="ret_sample",
    )(ret, cos, sin, gam, lw["ret_g"], state)


def _swa_sample_kernel(q_ref, kn_ref, vn_ref, knc_ref, vnc_ref, sink_ref, kc_ref, vc_ref, o_ref, ko_ref, vo_ref):
    q = q_ref[...]
    kc, vc = kc_ref[...], vc_ref[...]
    kn, vn = kn_ref[...], vn_ref[...]
    scale = HD ** -0.5
    s = _einsum("bhd,bdw->bhw", q.astype(BF16), kc.astype(BF16)) * scale
    s_new = jnp.sum(q * kn, axis=-1, keepdims=True) * scale
    w = lax.broadcasted_iota(jnp.int32, (1, 1, WINDOW), 2)
    s = jnp.where(w > 0, s, -jnp.inf)
    sink = sink_ref[...][:, 0:1]
    mx = jnp.maximum(jnp.maximum(jnp.max(s, axis=-1, keepdims=True), s_new), sink)
    p = jnp.exp(s - mx)
    p_new = jnp.exp(s_new - mx)
    den = jnp.sum(p, axis=-1, keepdims=True) + p_new + jnp.exp(sink - mx)
    o = _einsum("bhw,bdw->bhd", p.astype(BF16), vc.astype(BF16)) + p_new * vn
    o_ref[...] = o / den
    newest = w == WINDOW - 1
    ko_ref[...] = jnp.where(newest, knc_ref[...], pltpu.roll(kc, WINDOW - 1, 2))
    vo_ref[...] = jnp.where(newest, vnc_ref[...], pltpu.roll(vc, WINDOW - 1, 2))


def _swa_sample(swa, cache_k, cache_v, lw, bt=16):
    bsz = swa.shape[0]
    li = lw["layer"]
    cache = pl.BlockSpec((None, bt, WINDOW, 128), lambda i: (li, i, 0, 0))
    lo = (jnp.arange(128) < HD)[None, :]
    qa, qb = swa[:, 0:128], swa[:, 128:256]
    q = jnp.stack([jnp.where(lo, qa, 0.0), jnp.where(lo, qb, 0.0), jnp.where(lo, 0.0, qa), jnp.where(lo, 0.0, qb)], axis=1)
    kn = swa[:, None, 256:384]
    vn = swa[:, None, 384:512]
    sink = jnp.broadcast_to(lw["sinks"][0, 0:N_HEADS].reshape(N_HEADS, 1), (N_HEADS, 128))
    blk = lambda shape: pl.BlockSpec(shape, lambda i: (i,) + (0,) * (len(shape) - 1))
    o, ko, vo = pl.pallas_call(
        _swa_sample_kernel,
        grid=(bsz // bt,),
        in_specs=[blk((bt, N_HEADS, 128)), blk((bt, 1, 128)), blk((bt, 1, 128)), blk((bt, 128, 1)), blk((bt, 128, 1)),
                  pl.BlockSpec((N_HEADS, 128), lambda i: (0, 0)), cache, cache],
        out_specs=[blk((bt, N_HEADS, 128)), blk((bt, WINDOW, 128)), blk((bt, WINDOW, 128))],
        out_shape=[jax.ShapeDtypeStruct((bsz, N_HEADS, 128), F32), jax.ShapeDtypeStruct(cache_k.shape[1:], F32),
                   jax.ShapeDtypeStruct(cache_v.shape[1:], F32)],
        compiler_params=_cparams(("parallel",)),
        name="swa_sample",
    )(q, kn, vn, jnp.swapaxes(kn, 1, 2), jnp.swapaxes(vn, 1, 2), sink, cache_k, cache_v)
    y = jnp.concatenate([o[:, 0, 0:64], o[:, 2, 64:128], o[:, 1, 0:64], o[:, 3, 64:128]], axis=1)
    return y, ko, vo


def _ml_scalars(small, m_state):
    z4 = jnp.zeros((small.shape[0], 4), F32)
    return jnp.concatenate([small[:, ML_I_LANE:ML_I_LANE + 4], z4, small[:, ML_F_LANE:ML_F_LANE + 4], z4, m_state, z4,
                            jnp.zeros((small.shape[0], 128 - 24), F32)], axis=1)


IN_NAMES = ("gq", "gk", "gv", "gr", "ga", "sq", "sk", "sv", "mq", "mk", "mv", "mi", "mf", "mo", "rq", "rk", "rv", "rg",
            "gates")
IN_SIZES = (N_HEADS * GLA_DK, N_HEADS * GLA_DK, BRANCH_W, BRANCH_W, GLA_RANK, BRANCH_W, SWA_KV * HD, SWA_KV * HD,
            BRANCH_W, BRANCH_W, BRANCH_W, N_HEADS, N_HEADS, BRANCH_W, BRANCH_W, BRANCH_W, BRANCH_W, BRANCH_W,
            N_BRANCH * D_MODEL)
IN_START = {name: sum(IN_SIZES[:i]) for i, name in enumerate(IN_NAMES)}
GATE_COL = IN_START["gates"]


def _mixer_rows(w_t):
    x = w_t.shape[1]
    at = lambda name, n: w_t[IN_START[name]:IN_START[name] + n]
    swa_q = at("sq", BRANCH_W).reshape(2, 2, HD, x).transpose(1, 0, 2, 3).reshape(BRANCH_W, x)
    halves = lambda name: at(name, BRANCH_W).reshape(N_HEADS, 2, HD // 2, x).transpose(1, 0, 2, 3).reshape(BRANCH_W, x)
    parts = [at("gq", W_GLA), swa_q, at("sk", 2 * SWA_KV * HD), at("mq", 3 * BRANCH_W), at("mo", BRANCH_W),
             halves("rq"), halves("rk"), at("rv", 2 * BRANCH_W), at("ga", GLA_RANK), at("mi", 2 * N_HEADS)]
    pad = W_MIX - sum(p.shape[0] for p in parts)
    return jnp.concatenate(parts + [jnp.zeros((pad, x), w_t.dtype)], axis=0)


def _prep_layer(li, w_in, b_in, gla_w_a2, gla_b_a2, gla_norm_g, swa_sinks, ml_norm_g, ret_norm_g, w_branch, w_out,
                ln1_g, ln1_b, w_router_grp, b_router_grp, w_router_exp, b_router_exp, w_exp_gate, w_exp_up,
                w_exp_down, ln2_g, ln2_b):
    row = lambda v: v.reshape(1, -1).astype(F32)
    wr = jnp.concatenate([w_router_grp[li], w_router_exp[li],
                          jnp.zeros((D_MODEL, 128 - N_GROUPS - N_EXPERTS), F32)], axis=1)
    wr_hi = wr.astype(BF16)
    wb = w_branch[li]
    wb1 = jnp.concatenate([wb[1, 0:64], wb[1, 128:192], wb[1, 64:128], wb[1, 192:256]], axis=0)
    w_t = w_in[li].T
    return dict(
        wm=_mixer_rows(w_t).astype(BF16),
        bm=row(_mixer_rows(b_in[li][:, None])),
        wg=w_t[GATE_COL:].astype(BF16),
        bg=row(b_in[li][GATE_COL:]),
        wa2=jnp.concatenate([gla_w_a2[li], jnp.zeros((128 - GLA_RANK, 128), F32)], axis=0).astype(BF16),
        ba2=row(gla_b_a2[li]),
        gla_g=row(gla_norm_g[li]), ml_g=row(ml_norm_g[li]), ret_g=row(ret_norm_g[li]),
        sinks=row(jnp.concatenate([swa_sinks[li], jnp.zeros((128 - N_HEADS,), F32)])),
        wbr=jnp.stack([wb[0], wb1, wb[2], wb[3]]).astype(BF16),
        wout=w_out[li].astype(BF16),
        ln1_g=row(ln1_g[li]), ln1_b=row(ln1_b[li]), ln2_g=row(ln2_g[li]), ln2_b=row(ln2_b[li]),
        wr_hi=wr_hi, wr_lo=(wr - wr_hi.astype(F32)).astype(BF16),
        br=row(jnp.concatenate([b_router_grp[li], b_router_exp[li], jnp.zeros((128 - N_GROUPS - N_EXPERTS,), F32)])),
        layer=li, we_gate=w_exp_gate, we_up=w_exp_up, we_down=w_exp_down,
    )


TOK_TILE = 256
MOE_PIECES_PROMPT = 32
MOE_PIECES_SAMPLE = 1


MERGE_TILES = 2


def _ffn_half(x2, ys, lw, tm, bp):
    x1, route, cnt, xs = _merge(x2, ys, lw, tm, MERGE_TILES if x2.shape[0] // tm % MERGE_TILES == 0 else 1)
    return _moe(x1, route, cnt, xs, lw, bp, tm)


def _layer_prompt(x2, lw, bsz, seq):
    gla, swa, ml, ret, small = _inproj(x2, lw["wm"], lw["bm"], 2 * TOK_TILE)
    ya, s_gla = _gla_prompt(gla, small, lw, bsz, seq)
    yb = _swa_prompt(swa, lw, bsz, seq)
    yc, c_bd, n_row, m_row = _ml_prompt(ml, small, lw, bsz, seq)
    yd, s_ret = _ret_prompt(ret, lw, bsz, seq)
    kv_tail = swa.reshape(bsz, seq, W_SWA)[:, seq - WINDOW:, 256:512]
    states = (_gla_state_from_blockdiag(s_gla),
              kv_tail[:, :, 0:128].reshape(bsz, WINDOW, SWA_KV, HD),
              kv_tail[:, :, 128:256].reshape(bsz, WINDOW, SWA_KV, HD),
              jnp.swapaxes(_diag_blocks(c_bd), -1, -2), n_row.reshape(bsz, N_HEADS, HD), m_row[:, 0, 0:N_HEADS],
              _ret_state_from_blockdiag(s_ret))
    return _ffn_half(x2, (ya, yb, yc, yd), lw, TOK_TILE, MOE_PIECES_PROMPT), states


def _layer_sample(x2, st, lw):
    s_gla, k_buf, v_buf, c_ml, n_ml, m_ml, s_ret = st
    bsz = x2.shape[0]
    depth = s_gla.shape[0]
    gla, swa, ml, ret, small = _inproj(x2, lw["wm"], lw["bm"], bsz)
    to_lanes = lambda s: jnp.moveaxis(s, 1, -1).reshape(depth, -1, bsz)
    from_lanes = lambda s, like: jnp.moveaxis(s.reshape(like.shape[2:] + (bsz,)), -1, 0)
    cache_t = lambda c: jnp.swapaxes(c.reshape(depth, bsz, WINDOW, SWA_KV * HD), 2, 3)
    ya, s_gla_new = _gla_sample(gla, small, to_lanes(s_gla), lw)
    yb, k_new, v_new = _swa_sample(swa, cache_t(k_buf), cache_t(v_buf), lw)
    k_new, v_new = jnp.swapaxes(k_new, 1, 2), jnp.swapaxes(v_new, 1, 2)
    yc, c_new, n_new, m_new = _ml_sample(ml, _ml_scalars(small, m_ml[lw["layer"]]), to_lanes(n_ml), to_lanes(c_ml), lw)
    yd, s_ret_new = _ret_sample(ret, to_lanes(s_ret), lw)
    states = (from_lanes(s_gla_new, s_gla), k_new.reshape(k_buf.shape[1:]), v_new.reshape(v_buf.shape[1:]),
              from_lanes(c_new, c_ml), from_lanes(n_new, n_ml), m_new[:, 0:N_HEADS], from_lanes(s_ret_new, s_ret))
    return _ffn_half(x2, (ya, yb, yc, yd), lw, bsz, MOE_PIECES_SAMPLE), states


def kernel(x_prompt, x_sample, state_gla, cache_swa_k, cache_swa_v, state_mlstm_c, state_mlstm_n, state_mlstm_m,
           state_ret, w_in, b_in, gla_w_a2, gla_b_a2, gla_norm_g, swa_sinks, ml_norm_g, ret_norm_g, w_branch, w_out,
           ln1_g, ln1_b, w_router_grp, b_router_grp, w_router_exp, b_router_exp, w_exp_gate, w_exp_up, w_exp_down,
           ln2_g, ln2_b):
    bsz, seq, _ = x_prompt.shape
    dec_b, dec_seq, _ = x_sample.shape
    assert dec_seq == 1 and dec_b == LANES and (bsz * seq) % (2 * TOK_TILE) == 0
    assert all(seq % n == 0 for n in (GLA_STEP_TILES * GLA_TILE, SWA_BLOCKS * WINDOW, STEP_CHUNKS * CHUNK))
    weights = (w_in, b_in, gla_w_a2, gla_b_a2, gla_norm_g, swa_sinks, ml_norm_g, ret_norm_g, w_branch, w_out,
               ln1_g, ln1_b, w_router_grp, b_router_grp, w_router_exp, b_router_exp, w_exp_gate, w_exp_up,
               w_exp_down, ln2_g, ln2_b)
    y_p = x_prompt.reshape(bsz * seq, D_MODEL)
    y_s = x_sample.reshape(dec_b, D_MODEL)
    new_p, new_s = [], []
    for li in range(w_in.shape[0]):
        lw = _prep_layer(li, *weights)
        y_p, st_p = _layer_prompt(y_p, lw, bsz, seq)
        st_in = (state_gla, cache_swa_k, cache_swa_v, state_mlstm_c, state_mlstm_n, state_mlstm_m, state_ret)
        y_s, st_s = _layer_sample(y_s, st_in, lw)
        new_p.append(st_p)
        new_s.append(st_s)
    outs = [y_p.reshape(bsz, seq, D_MODEL), y_s.reshape(dec_b, dec_seq, D_MODEL)]
    for i in range(7):
        outs.append(jnp.stack([s[i] for s in new_p], axis=0))
        outs.append(jnp.stack([s[i] for s in new_s], axis=0))
    return tuple(outs)
```

```python
import functools

import jax
import jax.numpy as jnp
from jax import lax
from jax.experimental import pallas as pl
from jax.experimental.pallas import tpu as pltpu

F32 = jnp.float32
BF16 = jnp.bfloat16

D_MODEL = 1024
N_BRANCH = 4
BRANCH_W = D_MODEL // N_BRANCH
N_HEADS = 4
GLA_DK = 32
GLA_RANK = 16
GLA_TAU = 16.0
HD = 64
SWA_KV = 2
WINDOW = 128
ROPE_BASE = 10000.0
PAST_LEN = 16384
N_GROUPS = 4
EPG = 8
N_EXPERTS = N_GROUPS * EPG
D_EXPERT = D_MODEL // 4
DEPTH = 2
ALPHA = (2 * DEPTH) ** 0.25
LN_EPS = 1e-5
VMEM_LIMIT = 56 * 1024 * 1024

W_GLA, W_SWA, W_ML, W_RET, W_SMALL = 768, 512, 1024, 1024, 128
W_MIX = W_GLA + W_SWA + W_ML + W_RET + W_SMALL


def _cparams(sem):
    return pltpu.CompilerParams(dimension_semantics=sem, vmem_limit_bytes=VMEM_LIMIT)


def _dot(a, b):
    return jnp.dot(a, b, preferred_element_type=F32)


def _einsum(spec, a, b):
    return jnp.einsum(spec, a, b, preferred_element_type=F32)


def _split2(x):
    hi = x.astype(BF16)
    lo = (x - hi.astype(F32)).astype(BF16)
    return hi, lo


def _split3(x):
    hi = x.astype(BF16)
    r = x - hi.astype(F32)
    mid = r.astype(BF16)
    lo = (r - mid.astype(F32)).astype(BF16)
    return hi, mid, lo


def _dot_exact_lhs(a_exact, b):
    return sum(_dot(a_exact, p) for p in _split3(b))


def _dot_exact_rhs(a, b_exact):
    return sum(_dot(p, b_exact) for p in _split3(a))


def _layer_norm(x, g, b):
    mu = jnp.mean(x, axis=-1, keepdims=True)
    xc = x - mu
    var = jnp.mean(xc * xc, axis=-1, keepdims=True)
    return xc * lax.rsqrt(var + LN_EPS) * g + b


def _inproj_kernel(x_ref, w_ref, b_ref, gla_ref, swa_ref, ml_ref, ret_ref, sm_ref):
    xb = x_ref[...].astype(BF16)
    off = 0
    for ref, width in ((gla_ref, W_GLA), (swa_ref, W_SWA), (ml_ref, W_ML), (ret_ref, W_RET), (sm_ref, W_SMALL)):
        ref[...] = _dot_nt(xb, w_ref[off:off + width, :]) + b_ref[:, off:off + width]
        off += width


def _inproj(x2, wm, bm, tm):
    t = x2.shape[0]
    widths = (W_GLA, W_SWA, W_ML, W_RET, W_SMALL)
    return pl.pallas_call(
        _inproj_kernel,
        grid=(t // tm,),
        in_specs=[pl.BlockSpec((tm, D_MODEL), lambda i: (i, 0)),
                  pl.BlockSpec((W_MIX, D_MODEL), lambda i: (0, 0)),
                  pl.BlockSpec((1, W_MIX), lambda i: (0, 0))],
        out_specs=[pl.BlockSpec((tm, w), lambda i: (i, 0)) for w in widths],
        out_shape=[jax.ShapeDtypeStruct((t, w), F32) for w in widths],
        compiler_params=_cparams(("parallel",)),
        name="inproj",
    )(x2, wm, bm)


def _merge_kernel(x_ref, ya_ref, yb_ref, yc_ref, yd_ref, wg_ref, bg_ref, wbr_ref, wout_ref, g1_ref, b1_ref,
                  wrh_ref, wrl_ref, br_ref, x1_ref, route_ref, cnt_ref, xs_ref, *, tm, sub):
    rows_t = xs_ref.shape[0] // sub
    x1, picks = [], []
    for t in range(sub):
        rows = slice(t * tm, (t + 1) * tm)
        x = x_ref[rows, :]
        xb = x.astype(BF16)
        merged = jnp.zeros((tm, D_MODEL), F32)
        for n, y_ref in enumerate((ya_ref, yb_ref, yc_ref, yd_ref)):
            cols = slice(n * D_MODEL, (n + 1) * D_MODEL)
            gate = 0.5 * jnp.tanh(0.5 * (_dot_nt(xb, wg_ref[cols, :]) + bg_ref[:, cols])) + 0.5
            merged = merged + gate * _dot(y_ref[rows, :].astype(BF16), wbr_ref[n])
        mix = _dot(merged.astype(BF16), wout_ref[...])
        x1.append(_layer_norm(ALPHA * x + mix, g1_ref[...], b1_ref[...]))
        x1_ref[rows, :] = x1[t]
        picks.append(_route(x1[t], wrh_ref[...], wrl_ref[...], br_ref[...], tm))
    for t in range(sub):
        route, cnt, xs = _sort(x1[t], picks[t], tm, rows_t)
        route_ref[t * tm:(t + 1) * tm, :] = route
        cnt_ref[t] = cnt
        xs_ref[t * rows_t:(t + 1) * rows_t, :] = xs


def _route(x1, wr_hi, wr_lo, br, tm):
    xh, xl = _split2(x1)
    both = _dot(xh, jnp.concatenate([wr_hi, wr_lo], axis=1))
    logits = both[:, 0:128] + both[:, 128:256] + _dot(xl, wr_hi) + br
    lane = lax.broadcasted_iota(jnp.int32, (tm, 128), 1)
    lanef = lane.astype(F32)
    neg = jnp.float32(-jnp.inf)
    big = jnp.float32(1e9)
    isg = lane < N_GROUPS
    gmax = jnp.max(jnp.where(isg, logits, neg), axis=-1, keepdims=True)
    gidx = jnp.min(jnp.where(isg & (logits == gmax), lanef, big), axis=-1, keepdims=True)
    p_group = 1.0 / jnp.sum(jnp.where(isg, jnp.exp(logits - gmax), 0.0), axis=-1, keepdims=True)
    lo = N_GROUPS + EPG * gidx
    em = (lanef >= lo) & (lanef < lo + EPG)
    v1 = jnp.max(jnp.where(em, logits, neg), axis=-1, keepdims=True)
    i1 = jnp.min(jnp.where(em & (logits == v1), lanef, big), axis=-1, keepdims=True)
    em2 = em & (lanef != i1)
    v2 = jnp.max(jnp.where(em2, logits, neg), axis=-1, keepdims=True)
    i2 = jnp.min(jnp.where(em2 & (logits == v2), lanef, big), axis=-1, keepdims=True)
    e21 = jnp.exp(v2 - v1)
    gt1 = 1.0 / (1.0 + e21)
    return i1, i2, p_group * gt1, p_group * (e21 * gt1)


def _sort(x1, picks, tm, rows_t):
    i1, i2, gate1, gate2 = picks
    lane = lax.broadcasted_iota(jnp.int32, (tm, 128), 1)
    lanef = lane.astype(F32)
    sel1 = lanef == i1
    sel2 = lanef == i2
    onehot = jnp.where(sel1 | sel2, 1.0, 0.0)
    cnt = jnp.sum(onehot, axis=0, keepdims=True)
    seg = jnp.floor((cnt + (PIECE - 1)) * (1.0 / PIECE)) * PIECE
    er = lax.broadcasted_iota(jnp.int32, (128, 128), 0)
    ec = lax.broadcasted_iota(jnp.int32, (128, 128), 1)
    seg_off = _dot(jnp.broadcast_to(seg, (8, 128)).astype(BF16), jnp.where(er < ec, 1.0, 0.0).astype(BF16))[0:1]
    row = lax.broadcasted_iota(jnp.int32, (tm, tm), 0)
    col = lax.broadcasted_iota(jnp.int32, (tm, tm), 1)
    earlier = _dot(jnp.where(col < row, 1.0, 0.0).astype(BF16), onehot.astype(BF16))
    where_to = seg_off + earlier
    w1 = jnp.where(sel1, where_to, 0.0)
    w2 = jnp.where(sel2, where_to, 0.0)
    pos1 = jnp.sum(w1, axis=-1, keepdims=True)
    pos2 = jnp.sum(w2, axis=-1, keepdims=True)
    ones = jnp.ones((8, 128), BF16)
    prow1 = sum(_dot_nt(ones, p) for p in _split2(w1))[0:1]
    prow2 = sum(_dot_nt(ones, p) for p in _split2(w2))[0:1]
    r = lax.broadcasted_iota(jnp.int32, (rows_t, tm), 0).astype(F32)
    place = jnp.where((r == prow1) | (r == prow2), 1.0, 0.0).astype(BF16)
    xs = _dot(place, x1.astype(BF16)).astype(BF16)

    vals = (i1 - N_GROUPS, i2 - N_GROUPS, gate1, gate2, pos1, pos2)
    route = jnp.zeros((tm, 128), F32)
    for k, v in enumerate(vals):
        route = jnp.where(lane == k, v, route)
    return route, cnt, xs


PIECE = 16


def _tile_rows(tm):
    return -(-(2 * tm + N_EXPERTS * (PIECE - 1)) // 128) * 128


def _merge(x2, ys, lw, tm, sub):
    t = x2.shape[0]
    nt = t // tm
    rows_t = _tile_rows(tm)
    tok = lambda w: pl.BlockSpec((sub * tm, w), lambda i: (i, 0))
    full = lambda shape: pl.BlockSpec(shape, lambda i: (0,) * len(shape), pipeline_mode=pl.Buffered(1))
    return pl.pallas_call(
        functools.partial(_merge_kernel, tm=tm, sub=sub),
        grid=(nt // sub,),
        in_specs=[tok(D_MODEL), tok(BRANCH_W), tok(BRANCH_W), tok(BRANCH_W), tok(BRANCH_W),
                  full((N_BRANCH * D_MODEL, D_MODEL)), full((1, N_BRANCH * D_MODEL)),
                  full((N_BRANCH, BRANCH_W, D_MODEL)), full((D_MODEL, D_MODEL)),
                  full((1, D_MODEL)), full((1, D_MODEL)),
                  full((D_MODEL, 128)), full((D_MODEL, 128)), full((1, 128))],
        out_specs=[tok(D_MODEL), tok(128), pl.BlockSpec((sub, 1, 128), lambda i: (i, 0, 0)),
                   pl.BlockSpec((sub * rows_t, D_MODEL), lambda i: (i, 0))],
        out_shape=[jax.ShapeDtypeStruct((t, D_MODEL), F32), jax.ShapeDtypeStruct((t, 128), F32),
                   jax.ShapeDtypeStruct((nt, 1, 128), F32), jax.ShapeDtypeStruct((nt * rows_t, D_MODEL), BF16)],
        compiler_params=_cparams(("parallel",)),
        name="merge",
    )(x2, *ys, lw["wg"], lw["bg"], lw["wbr"], lw["wout"], lw["ln1_g"], lw["ln1_b"], lw["wr_hi"], lw["wr_lo"], lw["br"])


def _piece_copy(hbm, hbm_row, vmem, piece, sem, to_vmem):
    h = hbm.at[pl.ds(pl.multiple_of(hbm_row, PIECE), PIECE), :]
    v = vmem.at[pl.ds(pl.multiple_of(piece * PIECE, PIECE), PIECE), :]
    return pltpu.make_async_copy(h, v, sem) if to_vmem else pltpu.make_async_copy(v, h, sem)


def _expert_kernel(be_ref, nu_ref, nv_ref, off_ref, xs_hbm, wg_ref, wu_ref, wd_ref, ys_hbm, xin, yout, sem_in, sem_out,
                   *, bp):
    del xs_hbm
    blk = pl.program_id(0)
    n_used = nu_ref[0]
    slot = blk % 2

    def pieces(b, s, buf, sem, to_vmem, start):
        n = nv_ref[b]

        def one(p, priority):
            cp = _piece_copy(ys_hbm, off_ref[b * bp + p], buf.at[s], p, sem.at[s], to_vmem)
            cp.start(priority=priority) if start else cp.wait()

        def body(i, c):
            one(2 * i, 0)

            @pl.when(2 * i + 1 < n)
            def _():
                one(2 * i + 1, 1)
            return c
        lax.fori_loop(0, (n + 1) // 2, body, 0)

    @pl.when(blk == 0)
    def _():
        xin[...] = jnp.zeros_like(xin)
        for first in range(2):
            @pl.when(first < n_used)
            def _():
                pieces(first, first, xin, sem_in, True, True)

    @pl.when(blk + 2 < n_used)
    def _():
        pieces(blk + 2, (blk + 2) % 3, xin, sem_in, True, True)

    @pl.when(blk < n_used)
    def _():
        in_slot = blk % 3
        pieces(blk, in_slot, xin, sem_in, True, False)

        @pl.when(blk >= 2)
        def _():
            pieces(blk - 2, slot, yout, sem_out, False, False)

        xb = xin[in_slot]
        hid = jax.nn.silu(_dot(xb, wg_ref[...].astype(BF16))) * _dot(xb, wu_ref[...].astype(BF16))
        yout[slot] = _dot(hid.astype(BF16), wd_ref[...].astype(BF16)).astype(BF16)
        pieces(blk, slot, yout, sem_out, False, True)

    @pl.when(blk == n_used - 1)
    def _():
        @pl.when(blk >= 1)
        def _():
            pieces(blk - 1, 1 - slot, yout, sem_out, False, False)
        pieces(blk, slot, yout, sem_out, False, False)


def _experts(xs, block_expert, n_used, n_valid, piece_off, lw, bp):
    n_blocks = block_expert.shape[0]
    rows = bp * PIECE
    li = lw["layer"]
    wspec = lambda shape: pl.BlockSpec((None, None) + shape, lambda b, be, nu, nv, off: (li, be[b], 0, 0))
    grid_spec = pltpu.PrefetchScalarGridSpec(
        num_scalar_prefetch=4,
        grid=(n_blocks,),
        in_specs=[pl.BlockSpec(memory_space=pl.ANY), wspec((D_MODEL, D_EXPERT)), wspec((D_MODEL, D_EXPERT)),
                  wspec((D_EXPERT, D_MODEL))],
        out_specs=pl.BlockSpec(memory_space=pl.ANY),
        scratch_shapes=[pltpu.VMEM((3, rows, D_MODEL), BF16), pltpu.VMEM((2, rows, D_MODEL), BF16),
                        pltpu.SemaphoreType.DMA((3,)), pltpu.SemaphoreType.DMA((2,))],
    )
    return pl.pallas_call(
        functools.partial(_expert_kernel, bp=bp),
        grid_spec=grid_spec,
        out_shape=jax.ShapeDtypeStruct(xs.shape, BF16),
        input_output_aliases={4: 0},
        compiler_params=_cparams(("arbitrary",)),
        name="experts",
    )(block_expert, n_used, n_valid, piece_off, xs, lw["we_gate"], lw["we_up"], lw["we_down"])


COMBINE_TILES = 4


def _combine_kernel(ys_ref, x1_ref, route_ref, g2_ref, b2_ref, out_ref, *, tm, sub):
    rows_t = ys_ref.shape[0] // sub
    r = lax.broadcasted_iota(jnp.int32, (tm, rows_t), 1).astype(F32)
    routes = [route_ref[t * tm:(t + 1) * tm, :] for t in range(sub)]
    picks = [[jnp.where(r == routes[t][:, 4 + j:5 + j], 1.0, 0.0).astype(BF16) for j in range(2)] for t in range(sub)]
    rows = [[_dot(picks[t][j], ys_ref[t * rows_t:(t + 1) * rows_t, :]) for j in range(2)] for t in range(sub)]
    for t in range(sub):
        ffn = routes[t][:, 2:3] * rows[t][0] + routes[t][:, 3:4] * rows[t][1]
        tok = slice(t * tm, (t + 1) * tm)
        out_ref[tok, :] = _layer_norm(ALPHA * x1_ref[tok, :] + ffn, g2_ref[...], b2_ref[...])


def _combine(ys, x1, route, lw, tm):
    t = x1.shape[0]
    rows_t = _tile_rows(tm)
    sub = COMBINE_TILES if (t // tm) % COMBINE_TILES == 0 else 1
    full = lambda shape: pl.BlockSpec(shape, lambda i: (0,) * len(shape))
    return pl.pallas_call(
        functools.partial(_combine_kernel, tm=tm, sub=sub),
        grid=(t // (sub * tm),),
        in_specs=[pl.BlockSpec((sub * rows_t, D_MODEL), lambda i: (i, 0)),
                  pl.BlockSpec((sub * tm, D_MODEL), lambda i: (i, 0)),
                  pl.BlockSpec((sub * tm, 128), lambda i: (i, 0)), full((1, D_MODEL)), full((1, D_MODEL))],
        out_specs=pl.BlockSpec((sub * tm, D_MODEL), lambda i: (i, 0)),
        out_shape=jax.ShapeDtypeStruct((t, D_MODEL), F32),
        compiler_params=_cparams(("parallel",)),
        name="combine",
    )(ys, x1, route, lw["ln2_g"], lw["ln2_b"])


def _piece_tables(cnt, tm, bp):
    nt = cnt.shape[0]
    rows_t = _tile_rows(tm)
    counts = cnt[:, 0, N_GROUPS:N_GROUPS + N_EXPERTS].astype(jnp.int32)
    pcs = ((counts + PIECE - 1) // PIECE).T
    seg_row = ((jnp.cumsum(pcs, axis=0) - pcs) * PIECE + jnp.arange(nt, dtype=jnp.int32)[None, :] * rows_t)
    tile_end = jnp.cumsum(pcs, axis=1)
    total = tile_end[:, -1]
    e_end = jnp.cumsum((total + bp - 1) // bp * bp)
    e_start = jnp.concatenate([jnp.zeros((1,), jnp.int32), e_end[:-1]])
    n_blocks = -(-(nt * rows_t // PIECE + N_EXPERTS * (bp - 1)) // bp)
    g = jnp.arange(n_blocks * bp, dtype=jnp.int32)[:, None]
    in_e = ((e_start[None, :] <= g) & (g < e_end[None, :])).astype(F32)
    pick = lambda table: jnp.dot(in_e, table.astype(F32), precision=lax.Precision.HIGHEST)
    local = g.astype(F32) - pick(e_start[:, None])
    ends, starts, rows = pick(tile_end), pick(tile_end - pcs), pick(seg_row)
    in_t = ((starts <= local) & (local < ends)).astype(F32)
    off = jnp.sum(in_t * (rows + (local - starts) * PIECE), axis=1).astype(jnp.int32)
    n_valid = jnp.sum(jnp.sum(in_t, axis=1).reshape(n_blocks, bp), axis=1).astype(jnp.int32)
    expert = pick(jnp.arange(N_EXPERTS, dtype=jnp.int32)[:, None])[:, 0]
    block_expert = expert.reshape(n_blocks, bp)[:, 0].astype(jnp.int32)
    n_used = (e_end[-1] // bp).astype(jnp.int32).reshape(1)
    return block_expert, n_used, n_valid, off


def _moe(x1, route, cnt, xs, lw, bp, tm):
    ys = _experts(xs, *_piece_tables(cnt, tm, bp), lw, bp)
    return _combine(ys, x1, route, lw, tm)


CHUNK = 128
STEP_CHUNKS = 16


def _dot_nt(a, b):
    return lax.dot_general(a, b, (((1,), (1,)), ((), ())), preferred_element_type=F32)


def _dot_tn(a, b):
    return lax.dot_general(a, b, (((0,), (0,)), ((), ())), preferred_element_type=F32)


def _lane_head(width, group):
    return lax.broadcasted_iota(jnp.int32, (1, width), 1) // group


def _head_norm(o, gain):
    r = lax.broadcasted_iota(jnp.int32, (BRANCH_W, BRANCH_W), 0) // HD
    c = lax.broadcasted_iota(jnp.int32, (BRANCH_W, BRANCH_W), 1) // HD
    avg = jnp.where(r == c, 1.0 / HD, 0.0).astype(BF16)
    mu = _dot_exact_rhs(o, avg)
    xc = o - mu
    var = _dot_exact_rhs(xc * xc, avg)
    return xc * lax.rsqrt(var + LN_EPS) * gain


def _stack_heads(x, head_of_lane):
    return jnp.concatenate([jnp.where(head_of_lane == h, x, 0.0) for h in range(N_HEADS)], axis=0)


def _pick_heads(stacked, rows, head_of_lane):
    out = jnp.where(head_of_lane == 0, stacked[0:rows], 0.0)
    for h in range(1, N_HEADS):
        out = jnp.where(head_of_lane == h, stacked[h * rows:(h + 1) * rows], out)
    return out


def _spread_heads(cols, head_of_lane):
    out = jnp.where(head_of_lane == 0, cols[0], 0.0)
    for h in range(1, N_HEADS):
        out = jnp.where(head_of_lane == h, cols[h], out)
    return out


SWA_BLOCKS = 8


def _swa_prompt_kernel(q_ref, kv_ref, kvp_ref, sink_ref, y_ref):
    n = pl.program_id(1)
    lo = lax.broadcasted_iota(jnp.int32, (1, 128), 1) < HD
    e = lax.broadcasted_iota(jnp.int32, (2 * WINDOW, N_HEADS * WINDOW), 0)
    col = lax.broadcasted_iota(jnp.int32, (2 * WINDOW, N_HEADS * WINDOW), 1)
    a = col % WINDOW
    band = (e > a) & (e <= a + WINDOW)
    head = lax.broadcasted_iota(jnp.int32, (1, N_HEADS * WINDOW), 1) // WINDOW
    sink = _spread_heads([sink_ref[:, h:h + 1] for h in range(N_HEADS)], head)
    blocks = range(SWA_BLOCKS)
    scores, vexts = [], []
    for i in blocks:
        rows = slice(i * WINDOW, (i + 1) * WINDOW)
        q = q_ref[rows, :]
        qa, qb = q[:, 0:128], q[:, 128:256]
        kv = kv_ref[rows, :]
        kvp = kvp_ref[...] if i == 0 else kv_ref[(i - 1) * WINDOW:i * WINDOW, :]
        kext = jnp.concatenate([kvp[:, 0:128], kv[:, 0:128]], axis=0).astype(BF16)
        vexts.append(jnp.concatenate([kvp[:, 128:256], kv[:, 128:256]], axis=0).astype(BF16))
        qs = jnp.concatenate([jnp.where(lo, qa, 0.0), jnp.where(lo, qb, 0.0),
                              jnp.where(lo, 0.0, qa), jnp.where(lo, 0.0, qb)], axis=0).astype(BF16)
        scores.append(_dot_nt(kext, qs))
    probs, inv_den = [], []
    for i in blocks:
        valid = band & ((n > 0) | (e >= WINDOW)) if i == 0 else band
        s = jnp.where(valid, scores[i] * (HD ** -0.5), -jnp.inf)
        mx = jnp.maximum(jnp.max(s, axis=0, keepdims=True), sink)
        p = jnp.exp(s - mx)
        inv_den.append(1.0 / (jnp.sum(p, axis=0, keepdims=True) + jnp.exp(sink - mx)))
        probs.append(p.astype(BF16))
    outs = [_dot_tn(vexts[i], probs[i]) for i in blocks]
    for i in blocks:
        o = outs[i] * inv_den[i]
        ya = jnp.concatenate([o[0:HD, 0:128], o[HD:128, 256:384]], axis=0).T
        yb = jnp.concatenate([o[0:HD, 128:256], o[HD:128, 384:512]], axis=0).T
        y_ref[i * WINDOW:(i + 1) * WINDOW, :] = jnp.concatenate([ya, yb], axis=1)


def _swa_prompt(swa, lw, bsz, seq):
    qt = SWA_BLOCKS * WINDOW
    nt = seq // qt
    return pl.pallas_call(
        _swa_prompt_kernel,
        grid=(bsz, nt),
        in_specs=[pl.BlockSpec((qt, 256), lambda b, n: (b * nt + n, 0)),
                  pl.BlockSpec((qt, 256), lambda b, n: (b * nt + n, 1)),
                  pl.BlockSpec((WINDOW, 256), lambda b, n: (jnp.maximum((b * nt + n) * SWA_BLOCKS - 1, 0), 1)),
                  pl.BlockSpec((1, 128), lambda b, n: (0, 0))],
        out_specs=pl.BlockSpec((qt, BRANCH_W), lambda b, n: (b * nt + n, 0)),
        out_shape=jax.ShapeDtypeStruct((bsz * seq, BRANCH_W), F32),
        compiler_params=_cparams(("parallel", "parallel")),
        name="swa_prompt",
    )(swa, swa, swa, lw["sinks"])


def _rotate(x, cos, sin):
    x1, x2 = x[:, 0:128], x[:, 128:256]
    return jnp.concatenate([x1 * cos - x2 * sin, x1 * sin + x2 * cos], axis=1)


def _ret_prompt_kernel(r_ref, cos_ref, sin_ref, dmat_ref, qdec_ref, kdec_ref, cdec_ref, g_ref, y_ref, s_out_ref, s_ref):
    n = pl.program_id(1)

    @pl.when(n == 0)
    def _():
        s_ref[...] = jnp.zeros_like(s_ref)

    c = CHUNK
    chunks = range(STEP_CHUNKS)
    hq = _lane_head(BRANCH_W, 32) % N_HEADS
    hv = _lane_head(BRANCH_W, HD)
    rows_head = (lax.broadcasted_iota(jnp.int32, (BRANCH_W, 1), 0) // 32) % N_HEADS
    qr, kr, v = [], [], []
    for i in chunks:
        rows = slice(i * c, (i + 1) * c)
        cos, sin = cos_ref[rows, :], sin_ref[rows, :]
        qr.append(_rotate(r_ref[rows, 0:256], cos, sin))
        kr.append(_rotate(r_ref[rows, 256:512], cos, sin) * (HD ** -0.5))
        v.append(r_ref[rows, 512:768].astype(BF16))
    att = [_dot_nt(_stack_heads(qr[i], hq).astype(BF16), kr[i].astype(BF16)) * dmat_ref[...] for i in chunks]
    upd = [_dot_tn((kr[i] * kdec_ref[...]).astype(BF16), v[i]) for i in chunks]
    states = [s_ref[...]]
    for i in chunks:
        states.append(cdec_ref[...] * states[i] + jnp.where(rows_head == hv, upd[i], 0.0))
    o = [_pick_heads(_dot(att[i].astype(BF16), v[i]), c, hv) for i in chunks]
    o = [o[i] + _dot((qr[i] * qdec_ref[...]).astype(BF16), states[i].astype(BF16)) for i in chunks]
    s_ref[...] = states[-1]
    s_out_ref[...] = states[-1]
    y_ref[...] = jax.nn.silu(r_ref[:, 768:1024]) * _head_norm(jnp.concatenate(o, axis=0), g_ref[...])


def _ret_tables(seq):
    c = CHUNK
    half = HD // 2
    freq = ROPE_BASE ** (-jnp.arange(half, dtype=F32) / half)
    ang = jnp.arange(seq).astype(F32)[:, None] * freq[None, :]
    cos = jnp.tile(jnp.cos(ang), (1, N_HEADS))
    sin = jnp.tile(jnp.sin(ang), (1, N_HEADS))
    log_g = jnp.log1p(-jnp.exp2(-5.0 - jnp.arange(N_HEADS, dtype=F32)))
    idx = jnp.arange(c, dtype=F32)
    rel = idx[:, None] - idx[None, :]
    dmat = jnp.exp(jnp.where(rel[None] >= 0, rel[None] * log_g[:, None, None], -jnp.inf)).reshape(N_HEADS * c, c)
    lane_head_q = (jnp.arange(BRANCH_W) // 32) % N_HEADS
    qdec = jnp.exp((idx + 1.0)[:, None] * log_g[lane_head_q][None, :])
    kdec = jnp.exp((c - 1.0 - idx)[:, None] * log_g[lane_head_q][None, :])
    cdec = jnp.exp(c * log_g)[jnp.arange(BRANCH_W) // HD][None, :]
    return cos, sin, dmat, qdec, kdec, cdec


def _ret_prompt(ret, lw, bsz, seq):
    c = CHUNK
    rows = STEP_CHUNKS * c
    nc = seq // rows
    cos, sin, dmat, qdec, kdec, cdec = _ret_tables(seq)
    full = lambda shape: pl.BlockSpec(shape, lambda b, n: (0,) * len(shape))
    return pl.pallas_call(
        _ret_prompt_kernel,
        grid=(bsz, nc),
        in_specs=[pl.BlockSpec((rows, W_RET), lambda b, n: (b * nc + n, 0)),
                  pl.BlockSpec((rows, 128), lambda b, n: (n, 0)), pl.BlockSpec((rows, 128), lambda b, n: (n, 0)),
                  full((N_HEADS * c, c)), full((c, BRANCH_W)), full((c, BRANCH_W)), full((1, BRANCH_W)),
                  full((1, BRANCH_W))],
        out_specs=[pl.BlockSpec((rows, BRANCH_W), lambda b, n: (b * nc + n, 0)),
                   pl.BlockSpec((None, BRANCH_W, BRANCH_W), lambda b, n: (b, 0, 0))],
        out_shape=[jax.ShapeDtypeStruct((bsz * seq, BRANCH_W), F32),
                   jax.ShapeDtypeStruct((bsz, BRANCH_W, BRANCH_W), F32)],
        scratch_shapes=[pltpu.VMEM((BRANCH_W, BRANCH_W), F32)],
        compiler_params=_cparams(("parallel", "arbitrary")),
        name="ret_prompt",
    )(ret, cos, sin, dmat, qdec, kdec, cdec, lw["ret_g"])


def _ret_state_from_blockdiag(s_bd):
    b = s_bd.shape[0]
    s = s_bd.reshape(b, 2, N_HEADS, 32, N_HEADS, HD)
    s = jnp.stack([s[:, :, h, :, h, :] for h in range(N_HEADS)], axis=1)
    return s.reshape(b, N_HEADS, HD, HD)


ML_I_LANE = GLA_RANK
ML_F_LANE = GLA_RANK + N_HEADS


def _ml_prompt_kernel(m_ref, sm_ref, g_ref, y_ref, c_out_ref, n_out_ref, m_out_ref, c_ref, n_ref, mm_ref):
    nchunk = pl.program_id(1)

    @pl.when(nchunk == 0)
    def _():
        c_ref[...] = jnp.zeros_like(c_ref)
        n_ref[...] = jnp.zeros_like(n_ref)
        mm_ref[...] = jnp.zeros_like(mm_ref)

    c = CHUNK
    chunks = range(STEP_CHUNKS)
    row = lax.broadcasted_iota(jnp.int32, (c, c), 0)
    col = lax.broadcasted_iota(jnp.int32, (c, c), 1)
    tri_lo = jnp.where(col <= row, 1.0, 0.0).astype(BF16)
    tri_up = jnp.where(row <= col, 1.0, 0.0).astype(BF16)
    hv = _lane_head(BRANCH_W, HD)
    hq = _lane_head(N_HEADS * c, c)
    key = lax.broadcasted_iota(jnp.int32, (c, N_HEADS * c), 0)
    qry = lax.broadcasted_iota(jnp.int32, (c, N_HEADS * c), 1) % c
    causal_t = key <= qry
    r = lax.broadcasted_iota(jnp.int32, (BRANCH_W, BRANCH_W), 0) // HD
    cc = lax.broadcasted_iota(jnp.int32, (BRANCH_W, BRANCH_W), 1) // HD
    same_head = r == cc
    lane = lax.broadcasted_iota(jnp.int32, (1, 128), 1)
    gate_lane = lax.broadcasted_iota(jnp.int32, (128, 1), 0)
    is_i = (gate_lane >= ML_I_LANE) & (gate_lane < ML_I_LANE + N_HEADS)
    is_f = (gate_lane >= ML_F_LANE) & (gate_lane < ML_F_LANE + N_HEADS)
    gate_head = jnp.where(is_i, gate_lane - ML_I_LANE, gate_lane - ML_F_LANE)
    sel_q = jnp.where((is_i | is_f) & (gate_head == hq), 1.0, 0.0).astype(BF16)
    sel_v = jnp.where((is_i | is_f) & (gate_head == hv), 1.0, 0.0).astype(BF16)
    sel_qv = jnp.concatenate([sel_q, sel_v], axis=1)
    lane128 = lax.broadcasted_iota(jnp.int32, (1, 128), 1)
    in_i = (lane128 >= ML_I_LANE) & (lane128 < ML_I_LANE + N_HEADS)
    in_f = (lane128 >= ML_F_LANE) & (lane128 < ML_F_LANE + N_HEADS)
    head_rows = lax.broadcasted_iota(jnp.int32, (8, 1), 0)

    q, ks, vt, brow, u_q, u_v, sc = [], [], [], [], [], [], []
    for i in chunks:
        rows = slice(i * c, (i + 1) * c)
        q.append(m_ref[rows, 0:256])
        ks.append(m_ref[rows, 256:512] * (HD ** -0.5))
        vt.append(m_ref[rows, 512:768].T.astype(BF16))
        sm = sm_ref[rows, :]
        bcols = _dot_exact_lhs(tri_lo, jax.nn.log_sigmoid(sm))
        brows = _dot_exact_rhs(jax.nn.log_sigmoid(sm.T), tri_up)
        brow.append(jnp.concatenate([brows[ML_F_LANE + h:ML_F_LANE + h + 1, :] for h in range(N_HEADS)], axis=1))
        z = jnp.where(in_i, sm, 0.0) - jnp.where(in_f, bcols, 0.0)
        u = _dot_exact_rhs(z, sel_qv)
        u_q.append(u[:, 0:N_HEADS * c])
        u_v.append(u[:, N_HEADS * c:])
        sc.append(_dot_nt(ks[i].astype(BF16), _stack_heads(q[i], hv).astype(BF16)))

    mm = mm_ref[...]
    qk, w_inter, rowsum, floor, khat, carry_b = [], [], [], [], [], []
    for i in chunks:
        m_prev = _spread_heads([mm[:, h:h + 1] for h in range(N_HEADS)], hq)
        d_log = jnp.where(causal_t, brow[i] + u_q[i], -jnp.inf)
        inter_log = brow[i] + m_prev
        m_t = jnp.maximum(inter_log, jnp.max(d_log, axis=0, keepdims=True))
        qk_i = sc[i] * jnp.exp(d_log - m_t)
        qk.append(qk_i.astype(BF16))
        rowsum.append(jnp.sum(qk_i, axis=0, keepdims=True))
        w_inter.append(jnp.exp(inter_log - m_t))
        floor.append(jnp.exp(-m_t))
        m_new = [m_t[:, (h + 1) * c - 1:(h + 1) * c] for h in range(N_HEADS)]
        b_last = [brow[i][:, (h + 1) * c - 1:(h + 1) * c] for h in range(N_HEADS)]
        carry_b.append(_spread_heads([jnp.exp(b_last[h] + mm[:, h:h + 1] - m_new[h]) for h in range(N_HEADS)], hv))
        khat.append(ks[i] * jnp.exp(u_v[i] + _spread_heads([b_last[h] - m_new[h] for h in range(N_HEADS)], hv)))
        mm = jnp.zeros((1, 128), F32)
        for h in range(N_HEADS):
            mm = jnp.where(lane == h, m_new[h], mm)

    intra = [_dot(vt[i], qk[i]) for i in chunks]
    upd = [_dot(vt[i], khat[i].astype(BF16)) for i in chunks]
    cms, nrows = [c_ref[...]], [n_ref[...]]
    for i in chunks:
        cms.append(carry_b[i] * cms[i] + jnp.where(same_head, upd[i], 0.0))
        nrows.append(carry_b[i] * nrows[i] + jnp.sum(khat[i], axis=0, keepdims=True))
    hout = []
    for i in chunks:
        qb = q[i].astype(BF16)
        inter = _dot_nt(cms[i].astype(BF16), qb)
        n_sel = jnp.where(head_rows == hv, nrows[i], 0.0)
        qn = sum(_dot_nt(p, qb) for p in _split3(n_sel))
        blocks = []
        for h in range(N_HEADS):
            cols = slice(h * c, (h + 1) * c)
            den = w_inter[i][:, cols] * qn[h:h + 1, :] + rowsum[i][:, cols]
            num = w_inter[i][:, cols] * inter[h * HD:(h + 1) * HD, :] + intra[i][h * HD:(h + 1) * HD, cols]
            blocks.append(num / jnp.maximum(jnp.abs(den), floor[i][:, cols]))
        hout.append(jnp.concatenate(blocks, axis=0).T)
    y_ref[...] = _head_norm(jax.nn.sigmoid(m_ref[:, 768:1024]) * jnp.concatenate(hout, axis=0), g_ref[...])

    c_ref[...] = cms[-1]
    n_ref[...] = nrows[-1]
    mm_ref[...] = mm
    c_out_ref[...] = cms[-1]
    n_out_ref[...] = nrows[-1]
    m_out_ref[...] = mm


def _ml_prompt(ml, small, lw, bsz, seq):
    rows = STEP_CHUNKS * CHUNK
    nc = seq // rows
    return pl.pallas_call(
        _ml_prompt_kernel,
        grid=(bsz, nc),
        in_specs=[pl.BlockSpec((rows, W_ML), lambda b, n: (b * nc + n, 0)),
                  pl.BlockSpec((rows, W_SMALL), lambda b, n: (b * nc + n, 0)),
                  pl.BlockSpec((1, BRANCH_W), lambda b, n: (0, 0))],
        out_specs=[pl.BlockSpec((rows, BRANCH_W), lambda b, n: (b * nc + n, 0)),
                   pl.BlockSpec((None, BRANCH_W, BRANCH_W), lambda b, n: (b, 0, 0)),
                   pl.BlockSpec((None, 1, BRANCH_W), lambda b, n: (b, 0, 0)),
                   pl.BlockSpec((None, 1, 128), lambda b, n: (b, 0, 0))],
        out_shape=[jax.ShapeDtypeStruct((bsz * seq, BRANCH_W), F32),
                   jax.ShapeDtypeStruct((bsz, BRANCH_W, BRANCH_W), F32),
                   jax.ShapeDtypeStruct((bsz, 1, BRANCH_W), F32),
                   jax.ShapeDtypeStruct((bsz, 1, 128), F32)],
        scratch_shapes=[pltpu.VMEM((BRANCH_W, BRANCH_W), F32), pltpu.VMEM((1, BRANCH_W), F32),
                        pltpu.VMEM((1, 128), F32)],
        compiler_params=_cparams(("parallel", "arbitrary")),
        name="ml_prompt",
    )(ml, small, lw["ml_g"])


def _diag_blocks(s_bd):
    b = s_bd.shape[0]
    s = s_bd.reshape(b, N_HEADS, HD, N_HEADS, HD)
    return jnp.stack([s[:, h, :, h, :] for h in range(N_HEADS)], axis=1)


GLA_TILE = 256
GLA_STEP_TILES = 2
GLA_SUB = 16


def _gla_prompt_kernel(x_ref, sm_ref, wa2_ref, ba2_ref, g_ref, y_ref, s_out_ref, s_ref):
    n = pl.program_id(1)

    @pl.when(n == 0)
    def _():
        s_ref[...] = jnp.zeros_like(s_ref)

    tq, sc = GLA_TILE, GLA_SUB
    r = lax.broadcasted_iota(jnp.int32, (tq, tq), 0)
    c = lax.broadcasted_iota(jnp.int32, (tq, tq), 1)
    same = (r // sc) == (c // sc)
    cum_in_sub = jnp.where(same & (c <= r), 1.0, 0.0).astype(BF16)
    sum_in_sub = jnp.where(same, 1.0, 0.0).astype(BF16)
    kh = lax.broadcasted_iota(jnp.int32, (128, 1), 0) // GLA_DK
    hv = _lane_head(BRANCH_W, HD)
    expand = jnp.where(kh == hv, 1.0, 0.0).astype(BF16)
    trow = lax.broadcasted_iota(jnp.int32, (sc, 1), 0)
    n_sub = tq // sc
    chunks = range(GLA_STEP_TILES * n_sub)

    q, k, v, b, qd, dec_t, upds = [], [], [], [], [], [], []
    for t in range(GLA_STEP_TILES):
        rows = slice(t * tq, (t + 1) * tq)
        q.append(x_ref[rows, 0:128] * (GLA_DK ** -0.5))
        k.append(x_ref[rows, 128:256])
        v.append(x_ref[rows, 256:512])
        log_a = jax.nn.log_sigmoid(_dot(sm_ref[rows, :].astype(BF16), wa2_ref[...]) + ba2_ref[...]) / GLA_TAU
        b.append(_dot_exact_lhs(cum_in_sub, log_a))
        btot = _dot_exact_lhs(sum_in_sub, log_a)
        qd.append((q[t] * jnp.exp(b[t])).astype(BF16))
        dec_t.append(jnp.exp(btot).T)
        khat_b = (k[t] * jnp.exp(btot - b[t])).astype(BF16)
        vb = v[t].astype(BF16)
        upds += [_dot_tn(khat_b[ci * sc:(ci + 1) * sc], vb[ci * sc:(ci + 1) * sc]) for ci in range(n_sub)]
    states = [s_ref[...]]
    for g in chunks:
        t, lo = g // n_sub, (g % n_sub) * sc
        states.append(dec_t[t][:, lo:lo + 1] * states[g] + jnp.where(kh == hv, upds[g], 0.0))
    outs = []
    for g in chunks:
        t, lo = g // n_sub, (g % n_sub) * sc
        qc, kc, bc, vc = q[t][lo:lo + sc], k[t][lo:lo + sc], b[t][lo:lo + sc], v[t][lo:lo + sc]
        pair = [qc * kc[j:j + 1] * jnp.exp(jnp.where(trow >= j, bc - bc[j:j + 1], -jnp.inf)) for j in range(sc)]
        att = _dot(jnp.concatenate(pair, axis=0).astype(BF16), expand)
        o = att[0:sc] * vc[0:1]
        for j in range(1, sc):
            o = o + att[j * sc:(j + 1) * sc] * vc[j:j + 1]
        outs.append(o)
    outs = [outs[g] + _dot(qd[g // n_sub][(g % n_sub) * sc:(g % n_sub + 1) * sc], states[g].astype(BF16)) for g in chunks]
    s_ref[...] = states[-1]
    s_out_ref[...] = states[-1]
    y_ref[...] = jax.nn.silu(x_ref[:, 512:768]) * _head_norm(jnp.concatenate(outs, axis=0), g_ref[...])


def _gla_prompt(gla, small, lw, bsz, seq):
    tq = GLA_STEP_TILES * GLA_TILE
    nt = seq // tq
    full = lambda shape: pl.BlockSpec(shape, lambda b, n: (0,) * len(shape))
    return pl.pallas_call(
        _gla_prompt_kernel,
        grid=(bsz, nt),
        in_specs=[pl.BlockSpec((tq, W_GLA), lambda b, n: (b * nt + n, 0)),
                  pl.BlockSpec((tq, W_SMALL), lambda b, n: (b * nt + n, 0)),
                  full((128, 128)), full((1, 128)), full((1, BRANCH_W))],
        out_specs=[pl.BlockSpec((tq, BRANCH_W), lambda b, n: (b * nt + n, 0)),
                   pl.BlockSpec((None, 128, BRANCH_W), lambda b, n: (b, 0, 0))],
        out_shape=[jax.ShapeDtypeStruct((bsz * seq, BRANCH_W), F32),
                   jax.ShapeDtypeStruct((bsz, 128, BRANCH_W), F32)],
        scratch_shapes=[pltpu.VMEM((128, BRANCH_W), F32)],
        compiler_params=_cparams(("parallel", "arbitrary")),
        name="gla_prompt",
    )(gla, small, lw["wa2"], lw["ba2"], lw["gla_g"])


def _gla_state_from_blockdiag(s_bd):
    b = s_bd.shape[0]
    s = s_bd.reshape(b, N_HEADS, GLA_DK, N_HEADS, HD)
    return jnp.stack([s[:, h, :, h, :] for h in range(N_HEADS)], axis=1)


LANES = 128


def _rank1_update(st_ref, snt_ref, q_ref, k_ref, decay_row, v, dk):
    def body(i, o):
        rows = pl.ds(pl.multiple_of(i * HD, HD), HD)
        new = decay_row(i) * st_ref[rows, :] + k_ref[pl.ds(i, 1), :] * v
        snt_ref[rows, :] = new
        return o + q_ref[pl.ds(i, 1), :] * new
    return lax.fori_loop(0, dk, body, jnp.zeros((HD, LANES), F32), unroll=4)


def _head_rows(h, n):
    return pl.ds(pl.multiple_of(h * n, n), n)


def _gla_sample_kernel(x_ref, sm_ref, wa2_ref, ba2_ref, g_ref, s_ref, y_ref, s_out_ref,
                       qt_ref, kt_ref, at_ref, vt_ref, ot_ref):
    h = pl.program_id(0)

    @pl.when(h == 0)
    def _():
        x = x_ref[...]
        qt_ref[...] = (x[:, 0:128] * (GLA_DK ** -0.5)).T
        kt_ref[...] = x[:, 128:256].T
        vt_ref[...] = x[:, 256:512].T
        log_a = jax.nn.log_sigmoid(_dot(sm_ref[...].astype(BF16), wa2_ref[...]) + ba2_ref[...]) / GLA_TAU
        at_ref[...] = jnp.exp(log_a).T

    base = h * GLA_DK
    o = _rank1_update(s_ref, s_out_ref, qt_ref.at[_head_rows(h, GLA_DK)], kt_ref.at[_head_rows(h, GLA_DK)],
                      lambda i: at_ref[pl.ds(base + i, 1), :], vt_ref[_head_rows(h, HD), :], GLA_DK)
    ot_ref[_head_rows(h, HD), :] = o

    @pl.when(h == N_HEADS - 1)
    def _():
        y_ref[...] = jax.nn.silu(x_ref[:, 512:768]) * _head_norm(ot_ref[...].T, g_ref[...])


def _gla_sample(gla, small, state, lw):
    per_head = GLA_DK * HD
    li = lw["layer"]
    full = lambda shape: pl.BlockSpec(shape, lambda h: (0,) * len(shape))
    vm = lambda r: pltpu.VMEM((r, LANES), F32)
    return pl.pallas_call(
        _gla_sample_kernel,
        grid=(N_HEADS,),
        in_specs=[full((LANES, W_GLA)), full((LANES, W_SMALL)), full((128, 128)), full((1, 128)), full((1, BRANCH_W)),
                  pl.BlockSpec((None, per_head, LANES), lambda h: (li, h, 0))],
        out_specs=[full((LANES, BRANCH_W)), pl.BlockSpec((per_head, LANES), lambda h: (h, 0))],
        out_shape=[jax.ShapeDtypeStruct((LANES, BRANCH_W), F32), jax.ShapeDtypeStruct(state.shape[1:], F32)],
        scratch_shapes=[vm(128), vm(128), vm(128), vm(256), vm(256)],
        compiler_params=_cparams(("arbitrary",)),
        name="gla_sample",
    )(gla, small, lw["wa2"], lw["ba2"], lw["gla_g"], state)


def _ml_sample_kernel(x_ref, sc_ref, n_ref, g_ref, c_ref, y_ref, c_out_ref, n_out_ref, m_out_ref,
                      qt_ref, kt_ref, vt_ref, ot_ref, w_ref, kp_ref):
    h = pl.program_id(0)

    @pl.when(h == 0)
    def _():
        x = x_ref[...]
        qt_ref[...] = x[:, 0:256].T
        kt_ref[...] = (x[:, 256:512] * (HD ** -0.5)).T
        vt_ref[...] = x[:, 512:768].T
        sct = sc_ref[...].T
        log_f = jax.nn.log_sigmoid(sct[8:16])
        m_old = sct[16:24]
        m_new = jnp.maximum(log_f + m_old, sct[0:8])
        w_ref[0:8, :] = jnp.exp(log_f + m_old - m_new)
        w_ref[8:16, :] = jnp.exp(sct[0:8] - m_new)
        w_ref[16:24, :] = m_new

    w_carry = w_ref[pl.ds(h, 1), :]
    w_tok = w_ref[pl.ds(8 + h, 1), :]
    m_new = w_ref[pl.ds(16 + h, 1), :]
    kp_ref[...] = kt_ref[_head_rows(h, HD), :] * w_tok
    q_h = qt_ref.at[_head_rows(h, HD)]
    n_new = w_carry * n_ref[_head_rows(h, HD), :] + kp_ref[...]
    num = _rank1_update(c_ref, c_out_ref, q_h, kp_ref, lambda i: w_carry, vt_ref[_head_rows(h, HD), :], HD)
    den = jnp.sum(q_h[...] * n_new, axis=0, keepdims=True)
    ot_ref[_head_rows(h, HD), :] = num / jnp.maximum(jnp.abs(den), jnp.exp(-m_new))
    n_out_ref[_head_rows(h, HD), :] = n_new

    @pl.when(h == N_HEADS - 1)
    def _():
        y_ref[...] = _head_norm(jax.nn.sigmoid(x_ref[:, 768:1024]) * ot_ref[...].T, g_ref[...])
        m_out_ref[...] = jnp.concatenate([w_ref[16:24, :], jnp.zeros((LANES - 8, LANES), F32)], axis=0).T


def _ml_sample(ml, scal, n_state, c_state, lw):
    per_head = HD * HD
    li = lw["layer"]
    full = lambda shape: pl.BlockSpec(shape, lambda h: (0,) * len(shape))
    vm = lambda r: pltpu.VMEM((r, LANES), F32)
    return pl.pallas_call(
        _ml_sample_kernel,
        grid=(N_HEADS,),
        in_specs=[full((LANES, W_ML)), full((LANES, 128)),
                  pl.BlockSpec((None, BRANCH_W, LANES), lambda h: (li, 0, 0)), full((1, BRANCH_W)),
                  pl.BlockSpec((None, per_head, LANES), lambda h: (li, h, 0))],
        out_specs=[full((LANES, BRANCH_W)), pl.BlockSpec((per_head, LANES), lambda h: (h, 0)),
                   full((BRANCH_W, LANES)), full((LANES, 128))],
        out_shape=[jax.ShapeDtypeStruct((LANES, BRANCH_W), F32), jax.ShapeDtypeStruct(c_state.shape[1:], F32),
                   jax.ShapeDtypeStruct((BRANCH_W, LANES), F32), jax.ShapeDtypeStruct((LANES, 128), F32)],
        scratch_shapes=[vm(256), vm(256), vm(256), vm(256), vm(24), vm(HD)],
        compiler_params=_cparams(("arbitrary",)),
        name="ml_sample",
    )(ml, scal, n_state, lw["ml_g"], c_state)


def _ret_sample_kernel(x_ref, cos_ref, sin_ref, gam_ref, g_ref, s_ref, y_ref, s_out_ref,
                       qt_ref, kt_ref, vt_ref, ot_ref, qr_ref, kr_ref):
    h = pl.program_id(0)

    @pl.when(h == 0)
    def _():
        x = x_ref[...]
        qt_ref[...] = x[:, 0:256].T
        kt_ref[...] = (x[:, 256:512] * (HD ** -0.5)).T
        vt_ref[...] = x[:, 512:768].T

    cos, sin = cos_ref[...], sin_ref[...]
    for src, dst in ((qt_ref, qr_ref), (kt_ref, kr_ref)):
        x1 = src[_head_rows(h, 32), :]
        x2 = src[pl.ds(pl.multiple_of(128 + h * 32, 32), 32), :]
        dst[0:32, :] = x1 * cos - x2 * sin
        dst[32:64, :] = x1 * sin + x2 * cos
    gamma = gam_ref[pl.ds(h, 1), :]
    ot_ref[_head_rows(h, HD), :] = _rank1_update(s_ref, s_out_ref, qr_ref, kr_ref, lambda i: gamma,
                                                 vt_ref[_head_rows(h, HD), :], HD)

    @pl.when(h == N_HEADS - 1)
    def _():
        y_ref[...] = jax.nn.silu(x_ref[:, 768:1024]) * _head_norm(ot_ref[...].T, g_ref[...])


def _ret_sample(ret, state, lw):
    per_head = HD * HD
    li = lw["layer"]
    half = HD // 2
    freq = ROPE_BASE ** (-jnp.arange(half, dtype=F32) / half)
    ang = jnp.full((1,), PAST_LEN, F32)[:, None] * freq[None, :]
    cos = jnp.broadcast_to(jnp.cos(ang).reshape(half, 1), (half, LANES))
    sin = jnp.broadcast_to(jnp.sin(ang).reshape(half, 1), (half, LANES))
    log_g = jnp.log1p(-jnp.exp2(-5.0 - jnp.arange(N_HEADS, dtype=F32)))
    gam = jnp.broadcast_to(jnp.concatenate([jnp.exp(log_g), jnp.ones((4,), F32)]).reshape(8, 1), (8, LANES))
    full = lambda shape: pl.BlockSpec(shape, lambda h: (0,) * len(shape))
    vm = lambda r: pltpu.VMEM((r, LANES), F32)
    return pl.pallas_call(
        _ret_sample_kernel,
        grid=(N_HEADS,),
        in_specs=[full((LANES, W_RET)), full((half, LANES)), full((half, LANES)), full((8, LANES)), full((1, BRANCH_W)),
                  pl.BlockSpec((None, per_head, LANES), lambda h: (li, h, 0))],
        out_specs=[full((LANES, BRANCH_W)), pl.BlockSpec((per_head, LANES), lambda h: (h, 0))],
        out_shape=[jax.ShapeDtypeStruct((LANES, BRANCH_W), F32), jax.ShapeDtypeStruct(state.shape[1:], F32)],
        scratch_shapes=[vm(256), vm(256), vm(256), vm(256), vm(HD), vm(HD)],
        compiler_params=_cparams(("arbitrary",)),
        name="ret_sample",
    )(ret, cos, sin, gam, lw["ret_g"], state)


def _swa_sample_kernel(q_ref, kn_ref, vn_ref, knc_ref, vnc_ref, sink_ref, kc_ref, vc_ref, o_ref, ko_ref, vo_ref):
    q = q_ref[...]
    kc, vc = kc_ref[...], vc_ref[...]
    kn, vn = kn_ref[...], vn_ref[...]
    scale = HD ** -0.5
    s = _einsum("bhd,bdw->bhw", q.astype(BF16), kc.astype(BF16)) * scale
    s_new = jnp.sum(q * kn, axis=-1, keepdims=True) * scale
    w = lax.broadcasted_iota(jnp.int32, (1, 1, WINDOW), 2)
    s = jnp.where(w > 0, s, -jnp.inf)
    sink = sink_ref[...][:, 0:1]
    mx = jnp.maximum(jnp.maximum(jnp.max(s, axis=-1, keepdims=True), s_new), sink)
    p = jnp.exp(s - mx)
    p_new = jnp.exp(s_new - mx)
    den = jnp.sum(p, axis=-1, keepdims=True) + p_new + jnp.exp(sink - mx)
    o = _einsum("bhw,bdw->bhd", p.astype(BF16), vc.astype(BF16)) + p_new * vn
    o_ref[...] = o / den
    newest = w == WINDOW - 1
    ko_ref[...] = jnp.where(newest, knc_ref[...], pltpu.roll(kc, WINDOW - 1, 2))
    vo_ref[...] = jnp.where(newest, vnc_ref[...], pltpu.roll(vc, WINDOW - 1, 2))


def _swa_sample(swa, cache_k, cache_v, lw, bt=16):
    bsz = swa.shape[0]
    li = lw["layer"]
    cache = pl.BlockSpec((None, bt, WINDOW, 128), lambda i: (li, i, 0, 0))
    lo = (jnp.arange(128) < HD)[None, :]
    qa, qb = swa[:, 0:128], swa[:, 128:256]
    q = jnp.stack([jnp.where(lo, qa, 0.0), jnp.where(lo, qb, 0.0), jnp.where(lo, 0.0, qa), jnp.where(lo, 0.0, qb)], axis=1)
    kn = swa[:, None, 256:384]
    vn = swa[:, None, 384:512]
    sink = jnp.broadcast_to(lw["sinks"][0, 0:N_HEADS].reshape(N_HEADS, 1), (N_HEADS, 128))
    blk = lambda shape: pl.BlockSpec(shape, lambda i: (i,) + (0,) * (len(shape) - 1))
    o, ko, vo = pl.pallas_call(
        _swa_sample_kernel,
        grid=(bsz // bt,),
        in_specs=[blk((bt, N_HEADS, 128)), blk((bt, 1, 128)), blk((bt, 1, 128)), blk((bt, 128, 1)), blk((bt, 128, 1)),
                  pl.BlockSpec((N_HEADS, 128), lambda i: (0, 0)), cache, cache],
        out_specs=[blk((bt, N_HEADS, 128)), blk((bt, WINDOW, 128)), blk((bt, WINDOW, 128))],
        out_shape=[jax.ShapeDtypeStruct((bsz, N_HEADS, 128), F32), jax.ShapeDtypeStruct(cache_k.shape[1:], F32),
                   jax.ShapeDtypeStruct(cache_v.shape[1:], F32)],
        compiler_params=_cparams(("parallel",)),
        name="swa_sample",
    )(q, kn, vn, jnp.swapaxes(kn, 1, 2), jnp.swapaxes(vn, 1, 2), sink, cache_k, cache_v)
    y = jnp.concatenate([o[:, 0, 0:64], o[:, 2, 64:128], o[:, 1, 0:64], o[:, 3, 64:128]], axis=1)
    return y, ko, vo


def _ml_scalars(small, m_state):
    z4 = jnp.zeros((small.shape[0], 4), F32)
    return jnp.concatenate([small[:, ML_I_LANE:ML_I_LANE + 4], z4, small[:, ML_F_LANE:ML_F_LANE + 4], z4, m_state, z4,
                            jnp.zeros((small.shape[0], 128 - 24), F32)], axis=1)


IN_NAMES = ("gq", "gk", "gv", "gr", "ga", "sq", "sk", "sv", "mq", "mk", "mv", "mi", "mf", "mo", "rq", "rk", "rv", "rg",
            "gates")
IN_SIZES = (N_HEADS * GLA_DK, N_HEADS * GLA_DK, BRANCH_W, BRANCH_W, GLA_RANK, BRANCH_W, SWA_KV * HD, SWA_KV * HD,
            BRANCH_W, BRANCH_W, BRANCH_W, N_HEADS, N_HEADS, BRANCH_W, BRANCH_W, BRANCH_W, BRANCH_W, BRANCH_W,
            N_BRANCH * D_MODEL)
IN_START = {name: sum(IN_SIZES[:i]) for i, name in enumerate(IN_NAMES)}
GATE_COL = IN_START["gates"]


def _mixer_rows(w_t):
    x = w_t.shape[1]
    at = lambda name, n: w_t[IN_START[name]:IN_START[name] + n]
    swa_q = at("sq", BRANCH_W).reshape(2, 2, HD, x).transpose(1, 0, 2, 3).reshape(BRANCH_W, x)
    halves = lambda name: at(name, BRANCH_W).reshape(N_HEADS, 2, HD // 2, x).transpose(1, 0, 2, 3).reshape(BRANCH_W, x)
    parts = [at("gq", W_GLA), swa_q, at("sk", 2 * SWA_KV * HD), at("mq", 3 * BRANCH_W), at("mo", BRANCH_W),
             halves("rq"), halves("rk"), at("rv", 2 * BRANCH_W), at("ga", GLA_RANK), at("mi", 2 * N_HEADS)]
    pad = W_MIX - sum(p.shape[0] for p in parts)
    return jnp.concatenate(parts + [jnp.zeros((pad, x), w_t.dtype)], axis=0)


def _prep_layer(li, w_in, b_in, gla_w_a2, gla_b_a2, gla_norm_g, swa_sinks, ml_norm_g, ret_norm_g, w_branch, w_out,
                ln1_g, ln1_b, w_router_grp, b_router_grp, w_router_exp, b_router_exp, w_exp_gate, w_exp_up,
                w_exp_down, ln2_g, ln2_b):
    row = lambda v: v.reshape(1, -1).astype(F32)
    wr = jnp.concatenate([w_router_grp[li], w_router_exp[li],
                          jnp.zeros((D_MODEL, 128 - N_GROUPS - N_EXPERTS), F32)], axis=1)
    wr_hi = wr.astype(BF16)
    wb = w_branch[li]
    wb1 = jnp.concatenate([wb[1, 0:64], wb[1, 128:192], wb[1, 64:128], wb[1, 192:256]], axis=0)
    w_t = w_in[li].T
    return dict(
        wm=_mixer_rows(w_t).astype(BF16),
        bm=row(_mixer_rows(b_in[li][:, None])),
        wg=w_t[GATE_COL:].astype(BF16),
        bg=row(b_in[li][GATE_COL:]),
        wa2=jnp.concatenate([gla_w_a2[li], jnp.zeros((128 - GLA_RANK, 128), F32)], axis=0).astype(BF16),
        ba2=row(gla_b_a2[li]),
        gla_g=row(gla_norm_g[li]), ml_g=row(ml_norm_g[li]), ret_g=row(ret_norm_g[li]),
        sinks=row(jnp.concatenate([swa_sinks[li], jnp.zeros((128 - N_HEADS,), F32)])),
        wbr=jnp.stack([wb[0], wb1, wb[2], wb[3]]).astype(BF16),
        wout=w_out[li].astype(BF16),
        ln1_g=row(ln1_g[li]), ln1_b=row(ln1_b[li]), ln2_g=row(ln2_g[li]), ln2_b=row(ln2_b[li]),
        wr_hi=wr_hi, wr_lo=(wr - wr_hi.astype(F32)).astype(BF16),
        br=row(jnp.concatenate([b_router_grp[li], b_router_exp[li], jnp.zeros((128 - N_GROUPS - N_EXPERTS,), F32)])),
        layer=li, we_gate=w_exp_gate, we_up=w_exp_up, we_down=w_exp_down,
    )


TOK_TILE = 256
MOE_PIECES_PROMPT = 32
MOE_PIECES_SAMPLE = 1


MERGE_TILES = 2


def _ffn_half(x2, ys, lw, tm, bp):
    x1, route, cnt, xs = _merge(x2, ys, lw, tm, MERGE_TILES if x2.shape[0] // tm % MERGE_TILES == 0 else 1)
    return _moe(x1, route, cnt, xs, lw, bp, tm)


def _layer_prompt(x2, lw, bsz, seq):
    gla, swa, ml, ret, small = _inproj(x2, lw["wm"], lw["bm"], 2 * TOK_TILE)
    ya, s_gla = _gla_prompt(gla, small, lw, bsz, seq)
    yb = _swa_prompt(swa, lw, bsz, seq)
    yc, c_bd, n_row, m_row = _ml_prompt(ml, small, lw, bsz, seq)
    yd, s_ret = _ret_prompt(ret, lw, bsz, seq)
    kv_tail = swa.reshape(bsz, seq, W_SWA)[:, seq - WINDOW:, 256:512]
    states = (_gla_state_from_blockdiag(s_gla),
              kv_tail[:, :, 0:128].reshape(bsz, WINDOW, SWA_KV, HD),
              kv_tail[:, :, 128:256].reshape(bsz, WINDOW, SWA_KV, HD),
              jnp.swapaxes(_diag_blocks(c_bd), -1, -2), n_row.reshape(bsz, N_HEADS, HD), m_row[:, 0, 0:N_HEADS],
              _ret_state_from_blockdiag(s_ret))
    return _ffn_half(x2, (ya, yb, yc, yd), lw, TOK_TILE, MOE_PIECES_PROMPT), states


def _layer_sample(x2, st, lw):
    s_gla, k_buf, v_buf, c_ml, n_ml, m_ml, s_ret = st
    bsz = x2.shape[0]
    depth = s_gla.shape[0]
    gla, swa, ml, ret, small = _inproj(x2, lw["wm"], lw["bm"], bsz)
    to_lanes = lambda s: jnp.moveaxis(s, 1, -1).reshape(depth, -1, bsz)
    from_lanes = lambda s, like: jnp.moveaxis(s.reshape(like.shape[2:] + (bsz,)), -1, 0)
    cache_t = lambda c: jnp.swapaxes(c.reshape(depth, bsz, WINDOW, SWA_KV * HD), 2, 3)
    ya, s_gla_new = _gla_sample(gla, small, to_lanes(s_gla), lw)
    yb, k_new, v_new = _swa_sample(swa, cache_t(k_buf), cache_t(v_buf), lw)
    k_new, v_new = jnp.swapaxes(k_new, 1, 2), jnp.swapaxes(v_new, 1, 2)
    yc, c_new, n_new, m_new = _ml_sample(ml, _ml_scalars(small, m_ml[lw["layer"]]), to_lanes(n_ml), to_lanes(c_ml), lw)
    yd, s_ret_new = _ret_sample(ret, to_lanes(s_ret), lw)
    states = (from_lanes(s_gla_new, s_gla), k_new.reshape(k_buf.shape[1:]), v_new.reshape(v_buf.shape[1:]),
              from_lanes(c_new, c_ml), from_lanes(n_new, n_ml), m_new[:, 0:N_HEADS], from_lanes(s_ret_new, s_ret))
    return _ffn_half(x2, (ya, yb, yc, yd), lw, bsz, MOE_PIECES_SAMPLE), states


def kernel(x_prompt, x_sample, state_gla, cache_swa_k, cache_swa_v, state_mlstm_c, state_mlstm_n, state_mlstm_m,
           state_ret, w_in, b_in, gla_w_a2, gla_b_a2, gla_norm_g, swa_sinks, ml_norm_g, ret_norm_g, w_branch, w_out,
           ln1_g, ln1_b, w_router_grp, b_router_grp, w_router_exp, b_router_exp, w_exp_gate, w_exp_up, w_exp_down,
           ln2_g, ln2_b):
    bsz, seq, _ = x_prompt.shape
    dec_b, dec_seq, _ = x_sample.shape
    assert dec_seq == 1 and dec_b == LANES and (bsz * seq) % (2 * TOK_TILE) == 0
    assert all(seq % n == 0 for n in (GLA_STEP_TILES * GLA_TILE, SWA_BLOCKS * WINDOW, STEP_CHUNKS * CHUNK))
    weights = (w_in, b_in, gla_w_a2, gla_b_a2, gla_norm_g, swa_sinks, ml_norm_g, ret_norm_g, w_branch, w_out,
               ln1_g, ln1_b, w_router_grp, b_router_grp, w_router_exp, b_router_exp, w_exp_gate, w_exp_up,
               w_exp_down, ln2_g, ln2_b)
    y_p = x_prompt.reshape(bsz * seq, D_MODEL)
    y_s = x_sample.reshape(dec_b, D_MODEL)
    new_p, new_s = [], []
    for li in range(w_in.shape[0]):
        lw = _prep_layer(li, *weights)
        y_p, st_p = _layer_prompt(y_p, lw, bsz, seq)
        st_in = (state_gla, cache_swa_k, cache_swa_v, state_mlstm_c, state_mlstm_n, state_mlstm_m, state_ret)
        y_s, st_s = _layer_sample(y_s, st_in, lw)
        new_p.append(st_p)
        new_s.append(st_s)
    outs = [y_p.reshape(bsz, seq, D_MODEL), y_s.reshape(dec_b, dec_seq, D_MODEL)]
    for i in range(7):
        outs.append(jnp.stack([s[i] for s in new_p], axis=0))
        outs.append(jnp.stack([s[i] for s in new_s], axis=0))
    return tuple(outs)
```

```python
import functools

import jax
import jax.numpy as jnp
from jax import lax
from jax.experimental import pallas as pl
from jax.experimental.pallas import tpu as pltpu

F32 = jnp.float32
BF16 = jnp.bfloat16

D_MODEL = 1024
N_BRANCH = 4
BRANCH_W = D_MODEL // N_BRANCH
N_HEADS = 4
GLA_DK = 32
GLA_RANK = 16
GLA_TAU = 16.0
HD = 64
SWA_KV = 2
WINDOW = 128
ROPE_BASE = 10000.0
PAST_LEN = 16384
N_GROUPS = 4
EPG = 8
N_EXPERTS = N_GROUPS * EPG
D_EXPERT = D_MODEL // 4
DEPTH = 2
ALPHA = (2 * DEPTH) ** 0.25
LN_EPS = 1e-5
VMEM_LIMIT = 56 * 1024 * 1024

W_GLA, W_SWA, W_ML, W_RET, W_SMALL = 768, 512, 1024, 1024, 128
W_MIX = W_GLA + W_SWA + W_ML + W_RET + W_SMALL


def _cparams(sem):
    return pltpu.CompilerParams(dimension_semantics=sem, vmem_limit_bytes=VMEM_LIMIT)


def _dot(a, b):
    return jnp.dot(a, b, preferred_element_type=F32)


def _einsum(spec, a, b):
    return jnp.einsum(spec, a, b, preferred_element_type=F32)


def _split2(x):
    hi = x.astype(BF16)
    lo = (x - hi.astype(F32)).astype(BF16)
    return hi, lo


def _split3(x):
    hi = x.astype(BF16)
    r = x - hi.astype(F32)
    mid = r.astype(BF16)
    lo = (r - mid.astype(F32)).astype(BF16)
    return hi, mid, lo


def _dot_exact_lhs(a_exact, b):
    return sum(_dot(a_exact, p) for p in _split3(b))


def _dot_exact_rhs(a, b_exact):
    return sum(_dot(p, b_exact) for p in _split3(a))


def _layer_norm(x, g, b):
    mu = jnp.mean(x, axis=-1, keepdims=True)
    xc = x - mu
    var = jnp.mean(xc * xc, axis=-1, keepdims=True)
    return xc * lax.rsqrt(var + LN_EPS) * g + b


def _inproj_kernel(x_ref, w_ref, b_ref, gla_ref, swa_ref, ml_ref, ret_ref, sm_ref):
    xb = x_ref[...].astype(BF16)
    off = 0
    for ref, width in ((gla_ref, W_GLA), (swa_ref, W_SWA), (ml_ref, W_ML), (ret_ref, W_RET), (sm_ref, W_SMALL)):
        ref[...] = _dot_nt(xb, w_ref[off:off + width, :]) + b_ref[:, off:off + width]
        off += width


def _inproj(x2, wm, bm, tm):
    t = x2.shape[0]
    widths = (W_GLA, W_SWA, W_ML, W_RET, W_SMALL)
    return pl.pallas_call(
        _inproj_kernel,
        grid=(t // tm,),
        in_specs=[pl.BlockSpec((tm, D_MODEL), lambda i: (i, 0)),
                  pl.BlockSpec((W_MIX, D_MODEL), lambda i: (0, 0)),
                  pl.BlockSpec((1, W_MIX), lambda i: (0, 0))],
        out_specs=[pl.BlockSpec((tm, w), lambda i: (i, 0)) for w in widths],
        out_shape=[jax.ShapeDtypeStruct((t, w), F32) for w in widths],
        compiler_params=_cparams(("parallel",)),
        name="inproj",
    )(x2, wm, bm)


def _merge_kernel(x_ref, ya_ref, yb_ref, yc_ref, yd_ref, wg_ref, bg_ref, wbr_ref, wout_ref, g1_ref, b1_ref,
                  wrh_ref, wrl_ref, br_ref, x1_ref, route_ref, cnt_ref, xs_ref, *, tm, sub):
    rows_t = xs_ref.shape[0] // sub
    x1, picks = [], []
    for t in range(sub):
        rows = slice(t * tm, (t + 1) * tm)
        x = x_ref[rows, :]
        xb = x.astype(BF16)
        merged = jnp.zeros((tm, D_MODEL), F32)
        for n, y_ref in enumerate((ya_ref, yb_ref, yc_ref, yd_ref)):
            cols = slice(n * D_MODEL, (n + 1) * D_MODEL)
            gate = 0.5 * jnp.tanh(0.5 * (_dot_nt(xb, wg_ref[cols, :]) + bg_ref[:, cols])) + 0.5
            merged = merged + gate * _dot(y_ref[rows, :].astype(BF16), wbr_ref[n])
        mix = _dot(merged.astype(BF16), wout_ref[...])
        x1.append(_layer_norm(ALPHA * x + mix, g1_ref[...], b1_ref[...]))
        x1_ref[rows, :] = x1[t]
        picks.append(_route(x1[t], wrh_ref[...], wrl_ref[...], br_ref[...], tm))
    for t in range(sub):
        route, cnt, xs = _sort(x1[t], picks[t], tm, rows_t)
        route_ref[t * tm:(t + 1) * tm, :] = route
        cnt_ref[t] = cnt
        xs_ref[t * rows_t:(t + 1) * rows_t, :] = xs


def _route(x1, wr_hi, wr_lo, br, tm):
    xh, xl = _split2(x1)
    both = _dot(xh, jnp.concatenate([wr_hi, wr_lo], axis=1))
    logits = both[:, 0:128] + both[:, 128:256] + _dot(xl, wr_hi) + br
    lane = lax.broadcasted_iota(jnp.int32, (tm, 128), 1)
    lanef = lane.astype(F32)
    neg = jnp.float32(-jnp.inf)
    big = jnp.float32(1e9)
    isg = lane < N_GROUPS
    gmax = jnp.max(jnp.where(isg, logits, neg), axis=-1, keepdims=True)
    gidx = jnp.min(jnp.where(isg & (logits == gmax), lanef, big), axis=-1, keepdims=True)
    p_group = 1.0 / jnp.sum(jnp.where(isg, jnp.exp(logits - gmax), 0.0), axis=-1, keepdims=True)
    lo = N_GROUPS + EPG * gidx
    em = (lanef >= lo) & (lanef < lo + EPG)
    v1 = jnp.max(jnp.where(em, logits, neg), axis=-1, keepdims=True)
    i1 = jnp.min(jnp.where(em & (logits == v1), lanef, big), axis=-1, keepdims=True)
    em2 = em & (lanef != i1)
    v2 = jnp.max(jnp.where(em2, logits, neg), axis=-1, keepdims=True)
    i2 = jnp.min(jnp.where(em2 & (logits == v2), lanef, big), axis=-1, keepdims=True)
    e21 = jnp.exp(v2 - v1)
    gt1 = 1.0 / (1.0 + e21)
    return i1, i2, p_group * gt1, p_group * (e21 * gt1)


def _sort(x1, picks, tm, rows_t):
    i1, i2, gate1, gate2 = picks
    lane = lax.broadcasted_iota(jnp.int32, (tm, 128), 1)
    lanef = lane.astype(F32)
    sel1 = lanef == i1
    sel2 = lanef == i2
    onehot = jnp.where(sel1 | sel2, 1.0, 0.0)
    cnt = jnp.sum(onehot, axis=0, keepdims=True)
    seg = jnp.floor((cnt + (PIECE - 1)) * (1.0 / PIECE)) * PIECE
    er = lax.broadcasted_iota(jnp.int32, (128, 128), 0)
    ec = lax.broadcasted_iota(jnp.int32, (128, 128), 1)
    seg_off = _dot(jnp.broadcast_to(seg, (8, 128)).astype(BF16), jnp.where(er < ec, 1.0, 0.0).astype(BF16))[0:1]
    row = lax.broadcasted_iota(jnp.int32, (tm, tm), 0)
    col = lax.broadcasted_iota(jnp.int32, (tm, tm), 1)
    earlier = _dot(jnp.where(col < row, 1.0, 0.0).astype(BF16), onehot.astype(BF16))
    where_to = seg_off + earlier
    w1 = jnp.where(sel1, where_to, 0.0)
    w2 = jnp.where(sel2, where_to, 0.0)
    pos1 = jnp.sum(w1, axis=-1, keepdims=True)
    pos2 = jnp.sum(w2, axis=-1, keepdims=True)
    ones = jnp.ones((8, 128), BF16)
    prow1 = sum(_dot_nt(ones, p) for p in _split2(w1))[0:1]
    prow2 = sum(_dot_nt(ones, p) for p in _split2(w2))[0:1]
    r = lax.broadcasted_iota(jnp.int32, (rows_t, tm), 0).astype(F32)
    place = jnp.where((r == prow1) | (r == prow2), 1.0, 0.0).astype(BF16)
    xs = _dot(place, x1.astype(BF16)).astype(BF16)

    vals = (i1 - N_GROUPS, i2 - N_GROUPS, gate1, gate2, pos1, pos2)
    route = jnp.zeros((tm, 128), F32)
    for k, v in enumerate(vals):
        route = jnp.where(lane == k, v, route)
    return route, cnt, xs


PIECE = 16


def _tile_rows(tm):
    return -(-(2 * tm + N_EXPERTS * (PIECE - 1)) // 128) * 128


def _merge(x2, ys, lw, tm, sub):
    t = x2.shape[0]
    nt = t // tm
    rows_t = _tile_rows(tm)
    tok = lambda w: pl.BlockSpec((sub * tm, w), lambda i: (i, 0))
    full = lambda shape: pl.BlockSpec(shape, lambda i: (0,) * len(shape), pipeline_mode=pl.Buffered(1))
    return pl.pallas_call(
        functools.partial(_merge_kernel, tm=tm, sub=sub),
        grid=(nt // sub,),
        in_specs=[tok(D_MODEL), tok(BRANCH_W), tok(BRANCH_W), tok(BRANCH_W), tok(BRANCH_W),
                  full((N_BRANCH * D_MODEL, D_MODEL)), full((1, N_BRANCH * D_MODEL)),
                  full((N_BRANCH, BRANCH_W, D_MODEL)), full((D_MODEL, D_MODEL)),
                  full((1, D_MODEL)), full((1, D_MODEL)),
                  full((D_MODEL, 128)), full((D_MODEL, 128)), full((1, 128))],
        out_specs=[tok(D_MODEL), tok(128), pl.BlockSpec((sub, 1, 128), lambda i: (i, 0, 0)),
                   pl.BlockSpec((sub * rows_t, D_MODEL), lambda i: (i, 0))],
        out_shape=[jax.ShapeDtypeStruct((t, D_MODEL), F32), jax.ShapeDtypeStruct((t, 128), F32),
                   jax.ShapeDtypeStruct((nt, 1, 128), F32), jax.ShapeDtypeStruct((nt * rows_t, D_MODEL), BF16)],
        compiler_params=_cparams(("parallel",)),
        name="merge",
    )(x2, *ys, lw["wg"], lw["bg"], lw["wbr"], lw["wout"], lw["ln1_g"], lw["ln1_b"], lw["wr_hi"], lw["wr_lo"], lw["br"])


def _piece_copy(hbm, hbm_row, vmem, piece, sem, to_vmem):
    h = hbm.at[pl.ds(pl.multiple_of(hbm_row, PIECE), PIECE), :]
    v = vmem.at[pl.ds(pl.multiple_of(piece * PIECE, PIECE), PIECE), :]
    return pltpu.make_async_copy(h, v, sem) if to_vmem else pltpu.make_async_copy(v, h, sem)


def _expert_kernel(be_ref, nu_ref, nv_ref, off_ref, xs_hbm, wg_ref, wu_ref, wd_ref, ys_hbm, xin, yout, sem_in, sem_out,
                   *, bp):
    del xs_hbm
    blk = pl.program_id(0)
    n_used = nu_ref[0]
    slot = blk % 2

    def pieces(b, s, buf, sem, to_vmem, start):
        n = nv_ref[b]

        def one(p, priority):
            cp = _piece_copy(ys_hbm, off_ref[b * bp + p], buf.at[s], p, sem.at[s], to_vmem)
            cp.start(priority=priority) if start else cp.wait()

        def body(i, c):
            one(2 * i, 0)

            @pl.when(2 * i + 1 < n)
            def _():
                one(2 * i + 1, 1)
            return c
        lax.fori_loop(0, (n + 1) // 2, body, 0)

    @pl.when(blk == 0)
    def _():
        xin[...] = jnp.zeros_like(xin)
        for first in range(2):
            @pl.when(first < n_used)
            def _():
                pieces(first, first, xin, sem_in, True, True)

    @pl.when(blk + 2 < n_used)
    def _():
        pieces(blk + 2, (blk + 2) % 3, xin, sem_in, True, True)

    @pl.when(blk < n_used)
    def _():
        in_slot = blk % 3
        pieces(blk, in_slot, xin, sem_in, True, False)

        @pl.when(blk >= 2)
        def _():
            pieces(blk - 2, slot, yout, sem_out, False, False)

        xb = xin[in_slot]
        hid = jax.nn.silu(_dot(xb, wg_ref[...].astype(BF16))) * _dot(xb, wu_ref[...].astype(BF16))
        yout[slot] = _dot(hid.astype(BF16), wd_ref[...].astype(BF16)).astype(BF16)
        pieces(blk, slot, yout, sem_out, False, True)

    @pl.when(blk == n_used - 1)
    def _():
        @pl.when(blk >= 1)
        def _():
            pieces(blk - 1, 1 - slot, yout, sem_out, False, False)
        pieces(blk, slot, yout, sem_out, False, False)


def _experts(xs, block_expert, n_used, n_valid, piece_off, lw, bp):
    n_blocks = block_expert.shape[0]
    rows = bp * PIECE
    li = lw["layer"]
    wspec = lambda shape: pl.BlockSpec((None, None) + shape, lambda b, be, nu, nv, off: (li, be[b], 0, 0))
    grid_spec = pltpu.PrefetchScalarGridSpec(
        num_scalar_prefetch=4,
        grid=(n_blocks,),
        in_specs=[pl.BlockSpec(memory_space=pl.ANY), wspec((D_MODEL, D_EXPERT)), wspec((D_MODEL, D_EXPERT)),
                  wspec((D_EXPERT, D_MODEL))],
        out_specs=pl.BlockSpec(memory_space=pl.ANY),
        scratch_shapes=[pltpu.VMEM((3, rows, D_MODEL), BF16), pltpu.VMEM((2, rows, D_MODEL), BF16),
                        pltpu.SemaphoreType.DMA((3,)), pltpu.SemaphoreType.DMA((2,))],
    )
    return pl.pallas_call(
        functools.partial(_expert_kernel, bp=bp),
        grid_spec=grid_spec,
        out_shape=jax.ShapeDtypeStruct(xs.shape, BF16),
        input_output_aliases={4: 0},
        compiler_params=_cparams(("arbitrary",)),
        name="experts",
    )(block_expert, n_used, n_valid, piece_off, xs, lw["we_gate"], lw["we_up"], lw["we_down"])


COMBINE_TILES = 4


def _combine_kernel(ys_ref, x1_ref, route_ref, g2_ref, b2_ref, out_ref, *, tm, sub):
    rows_t = ys_ref.shape[0] // sub
    r = lax.broadcasted_iota(jnp.int32, (tm, rows_t), 1).astype(F32)
    routes = [route_ref[t * tm:(t + 1) * tm, :] for t in range(sub)]
    picks = [[jnp.where(r == routes[t][:, 4 + j:5 + j], 1.0, 0.0).astype(BF16) for j in range(2)] for t in range(sub)]
    rows = [[_dot(picks[t][j], ys_ref[t * rows_t:(t + 1) * rows_t, :]) for j in range(2)] for t in range(sub)]
    for t in range(sub):
        ffn = routes[t][:, 2:3] * rows[t][0] + routes[t][:, 3:4] * rows[t][1]
        tok = slice(t * tm, (t + 1) * tm)
        out_ref[tok, :] = _layer_norm(ALPHA * x1_ref[tok, :] + ffn, g2_ref[...], b2_ref[...])


def _combine(ys, x1, route, lw, tm):
    t = x1.shape[0]
    rows_t = _tile_rows(tm)
    sub = COMBINE_TILES if (t // tm) % COMBINE_TILES == 0 else 1
    full = lambda shape: pl.BlockSpec(shape, lambda i: (0,) * len(shape))
    return pl.pallas_call(
        functools.partial(_combine_kernel, tm=tm, sub=sub),
        grid=(t // (sub * tm),),
        in_specs=[pl.BlockSpec((sub * rows_t, D_MODEL), lambda i: (i, 0)),
                  pl.BlockSpec((sub * tm, D_MODEL), lambda i: (i, 0)),
                  pl.BlockSpec((sub * tm, 128), lambda i: (i, 0)), full((1, D_MODEL)), full((1, D_MODEL))],
        out_specs=pl.BlockSpec((sub * tm, D_MODEL), lambda i: (i, 0)),
        out_shape=jax.ShapeDtypeStruct((t, D_MODEL), F32),
        compiler_params=_cparams(("parallel",)),
        name="combine",
    )(ys, x1, route, lw["ln2_g"], lw["ln2_b"])


def _piece_tables(cnt, tm, bp):
    nt = cnt.shape[0]
    rows_t = _tile_rows(tm)
    counts = cnt[:, 0, N_GROUPS:N_GROUPS + N_EXPERTS].astype(jnp.int32)
    pcs = ((counts + PIECE - 1) // PIECE).T
    seg_row = ((jnp.cumsum(pcs, axis=0) - pcs) * PIECE + jnp.arange(nt, dtype=jnp.int32)[None, :] * rows_t)
    tile_end = jnp.cumsum(pcs, axis=1)
    total = tile_end[:, -1]
    e_end = jnp.cumsum((total + bp - 1) // bp * bp)
    e_start = jnp.concatenate([jnp.zeros((1,), jnp.int32), e_end[:-1]])
    n_blocks = -(-(nt * rows_t // PIECE + N_EXPERTS * (bp - 1)) // bp)
    g = jnp.arange(n_blocks * bp, dtype=jnp.int32)[:, None]
    in_e = ((e_start[None, :] <= g) & (g < e_end[None, :])).astype(F32)
    pick = lambda table: jnp.dot(in_e, table.astype(F32), precision=lax.Precision.HIGHEST)
    local = g.astype(F32) - pick(e_start[:, None])
    ends, starts, rows = pick(tile_end), pick(tile_end - pcs), pick(seg_row)
    in_t = ((starts <= local) & (local < ends)).astype(F32)
    off = jnp.sum(in_t * (rows + (local - starts) * PIECE), axis=1).astype(jnp.int32)
    n_valid = jnp.sum(jnp.sum(in_t, axis=1).reshape(n_blocks, bp), axis=1).astype(jnp.int32)
    expert = pick(jnp.arange(N_EXPERTS, dtype=jnp.int32)[:, None])[:, 0]
    block_expert = expert.reshape(n_blocks, bp)[:, 0].astype(jnp.int32)
    n_used = (e_end[-1] // bp).astype(jnp.int32).reshape(1)
    return block_expert, n_used, n_valid, off


def _moe(x1, route, cnt, xs, lw, bp, tm):
    ys = _experts(xs, *_piece_tables(cnt, tm, bp), lw, bp)
    return _combine(ys, x1, route, lw, tm)


CHUNK = 128
STEP_CHUNKS = 16


def _dot_nt(a, b):
    return lax.dot_general(a, b, (((1,), (1,)), ((), ())), preferred_element_type=F32)


def _dot_tn(a, b):
    return lax.dot_general(a, b, (((0,), (0,)), ((), ())), preferred_element_type=F32)


def _lane_head(width, group):
    return lax.broadcasted_iota(jnp.int32, (1, width), 1) // group


def _head_norm(o, gain):
    r = lax.broadcasted_iota(jnp.int32, (BRANCH_W, BRANCH_W), 0) // HD
    c = lax.broadcasted_iota(jnp.int32, (BRANCH_W, BRANCH_W), 1) // HD
    avg = jnp.where(r == c, 1.0 / HD, 0.0).astype(BF16)
    mu = _dot_exact_rhs(o, avg)
    xc = o - mu
    var = _dot_exact_rhs(xc * xc, avg)
    return xc * lax.rsqrt(var + LN_EPS) * gain


def _stack_heads(x, head_of_lane):
    return jnp.concatenate([jnp.where(head_of_lane == h, x, 0.0) for h in range(N_HEADS)], axis=0)


def _pick_heads(stacked, rows, head_of_lane):
    out = jnp.where(head_of_lane == 0, stacked[0:rows], 0.0)
    for h in range(1, N_HEADS):
        out = jnp.where(head_of_lane == h, stacked[h * rows:(h + 1) * rows], out)
    return out


def _spread_heads(cols, head_of_lane):
    out = jnp.where(head_of_lane == 0, cols[0], 0.0)
    for h in range(1, N_HEADS):
        out = jnp.where(head_of_lane == h, cols[h], out)
    return out


SWA_BLOCKS = 16


def _swa_prompt_kernel(q_ref, kv_ref, kvp_ref, sink_ref, y_ref):
    n = pl.program_id(1)
    lo = lax.broadcasted_iota(jnp.int32, (1, 128), 1) < HD
    e = lax.broadcasted_iota(jnp.int32, (2 * WINDOW, N_HEADS * WINDOW), 0)
    col = lax.broadcasted_iota(jnp.int32, (2 * WINDOW, N_HEADS * WINDOW), 1)
    a = col % WINDOW
    band = (e > a) & (e <= a + WINDOW)
    head = lax.broadcasted_iota(jnp.int32, (1, N_HEADS * WINDOW), 1) // WINDOW
    sink = _spread_heads([sink_ref[:, h:h + 1] for h in range(N_HEADS)], head)
    blocks = range(SWA_BLOCKS)
    scores, vexts = [], []
    for i in blocks:
        rows = slice(i * WINDOW, (i + 1) * WINDOW)
        q = q_ref[rows, :]
        qa, qb = q[:, 0:128], q[:, 128:256]
        kv = kv_ref[rows, :]
        kvp = kvp_ref[...] if i == 0 else kv_ref[(i - 1) * WINDOW:i * WINDOW, :]
        kext = jnp.concatenate([kvp[:, 0:128], kv[:, 0:128]], axis=0).astype(BF16)
        vexts.append(jnp.concatenate([kvp[:, 128:256], kv[:, 128:256]], axis=0).astype(BF16))
        qs = jnp.concatenate([jnp.where(lo, qa, 0.0), jnp.where(lo, qb, 0.0),
                              jnp.where(lo, 0.0, qa), jnp.where(lo, 0.0, qb)], axis=0).astype(BF16)
        scores.append(_dot_nt(kext, qs))
    probs, inv_den = [], []
    for i in blocks:
        valid = band & ((n > 0) | (e >= WINDOW)) if i == 0 else band
        s = jnp.where(valid, scores[i] * (HD ** -0.5), -jnp.inf)
        mx = jnp.maximum(jnp.max(s, axis=0, keepdims=True), sink)
        p = jnp.exp(s - mx)
        inv_den.append(1.0 / (jnp.sum(p, axis=0, keepdims=True) + jnp.exp(sink - mx)))
        probs.append(p.astype(BF16))
    outs = [_dot_tn(vexts[i], probs[i]) for i in blocks]
    for i in blocks:
        o = outs[i] * inv_den[i]
        ya = jnp.concatenate([o[0:HD, 0:128], o[HD:128, 256:384]], axis=0).T
        yb = jnp.concatenate([o[0:HD, 128:256], o[HD:128, 384:512]], axis=0).T
        y_ref[i * WINDOW:(i + 1) * WINDOW, :] = jnp.concatenate([ya, yb], axis=1)


def _swa_prompt(swa, lw, bsz, seq):
    qt = SWA_BLOCKS * WINDOW
    nt = seq // qt
    return pl.pallas_call(
        _swa_prompt_kernel,
        grid=(bsz, nt),
        in_specs=[pl.BlockSpec((qt, 256), lambda b, n: (b * nt + n, 0)),
                  pl.BlockSpec((qt, 256), lambda b, n: (b * nt + n, 1)),
                  pl.BlockSpec((WINDOW, 256), lambda b, n: (jnp.maximum((b * nt + n) * SWA_BLOCKS - 1, 0), 1)),
                  pl.BlockSpec((1, 128), lambda b, n: (0, 0))],
        out_specs=pl.BlockSpec((qt, BRANCH_W), lambda b, n: (b * nt + n, 0)),
        out_shape=jax.ShapeDtypeStruct((bsz * seq, BRANCH_W), F32),
        compiler_params=_cparams(("parallel", "parallel")),
        name="swa_prompt",
    )(swa, swa, swa, lw["sinks"])


def _rotate(x, cos, sin):
    x1, x2 = x[:, 0:128], x[:, 128:256]
    return jnp.concatenate([x1 * cos - x2 * sin, x1 * sin + x2 * cos], axis=1)


def _ret_prompt_kernel(r_ref, cos_ref, sin_ref, dmat_ref, qdec_ref, kdec_ref, cdec_ref, g_ref, y_ref, s_out_ref, s_ref):
    n = pl.program_id(1)

    @pl.when(n == 0)
    def _():
        s_ref[...] = jnp.zeros_like(s_ref)

    c = CHUNK
    chunks = range(STEP_CHUNKS)
    hq = _lane_head(BRANCH_W, 32) % N_HEADS
    hv = _lane_head(BRANCH_W, HD)
    rows_head = (lax.broadcasted_iota(jnp.int32, (BRANCH_W, 1), 0) // 32) % N_HEADS
    qr, kr, v = [], [], []
    for i in chunks:
        rows = slice(i * c, (i + 1) * c)
        cos, sin = cos_ref[rows, :], sin_ref[rows, :]
        qr.append(_rotate(r_ref[rows, 0:256], cos, sin))
        kr.append(_rotate(r_ref[rows, 256:512], cos, sin) * (HD ** -0.5))
        v.append(r_ref[rows, 512:768].astype(BF16))
    att = [_dot_nt(_stack_heads(qr[i], hq).astype(BF16), kr[i].astype(BF16)) * dmat_ref[...] for i in chunks]
    upd = [_dot_tn((kr[i] * kdec_ref[...]).astype(BF16), v[i]) for i in chunks]
    states = [s_ref[...]]
    for i in chunks:
        states.append(cdec_ref[...] * states[i] + jnp.where(rows_head == hv, upd[i], 0.0))
    o = [_pick_heads(_dot(att[i].astype(BF16), v[i]), c, hv) for i in chunks]
    o = [o[i] + _dot((qr[i] * qdec_ref[...]).astype(BF16), states[i].astype(BF16)) for i in chunks]
    s_ref[...] = states[-1]
    s_out_ref[...] = states[-1]
    y_ref[...] = jax.nn.silu(r_ref[:, 768:1024]) * _head_norm(jnp.concatenate(o, axis=0), g_ref[...])


def _ret_tables(seq):
    c = CHUNK
    half = HD // 2
    freq = ROPE_BASE ** (-jnp.arange(half, dtype=F32) / half)
    ang = jnp.arange(seq).astype(F32)[:, None] * freq[None, :]
    cos = jnp.tile(jnp.cos(ang), (1, N_HEADS))
    sin = jnp.tile(jnp.sin(ang), (1, N_HEADS))
    log_g = jnp.log1p(-jnp.exp2(-5.0 - jnp.arange(N_HEADS, dtype=F32)))
    idx = jnp.arange(c, dtype=F32)
    rel = idx[:, None] - idx[None, :]
    dmat = jnp.exp(jnp.where(rel[None] >= 0, rel[None] * log_g[:, None, None], -jnp.inf)).reshape(N_HEADS * c, c)
    lane_head_q = (jnp.arange(BRANCH_W) // 32) % N_HEADS
    qdec = jnp.exp((idx + 1.0)[:, None] * log_g[lane_head_q][None, :])
    kdec = jnp.exp((c - 1.0 - idx)[:, None] * log_g[lane_head_q][None, :])
    cdec = jnp.exp(c * log_g)[jnp.arange(BRANCH_W) // HD][None, :]
    return cos, sin, dmat, qdec, kdec, cdec


def _ret_prompt(ret, lw, bsz, seq):
    c = CHUNK
    rows = STEP_CHUNKS * c
    nc = seq // rows
    cos, sin, dmat, qdec, kdec, cdec = _ret_tables(seq)
    full = lambda shape: pl.BlockSpec(shape, lambda b, n: (0,) * len(shape))
    return pl.pallas_call(
        _ret_prompt_kernel,
        grid=(bsz, nc),
        in_specs=[pl.BlockSpec((rows, W_RET), lambda b, n: (b * nc + n, 0)),
                  pl.BlockSpec((rows, 128), lambda b, n: (n, 0)), pl.BlockSpec((rows, 128), lambda b, n: (n, 0)),
                  full((N_HEADS * c, c)), full((c, BRANCH_W)), full((c, BRANCH_W)), full((1, BRANCH_W)),
                  full((1, BRANCH_W))],
        out_specs=[pl.BlockSpec((rows, BRANCH_W), lambda b, n: (b * nc + n, 0)),
                   pl.BlockSpec((None, BRANCH_W, BRANCH_W), lambda b, n: (b, 0, 0))],
        out_shape=[jax.ShapeDtypeStruct((bsz * seq, BRANCH_W), F32),
                   jax.ShapeDtypeStruct((bsz, BRANCH_W, BRANCH_W), F32)],
        scratch_shapes=[pltpu.VMEM((BRANCH_W, BRANCH_W), F32)],
        compiler_params=_cparams(("parallel", "arbitrary")),
        name="ret_prompt",
    )(ret, cos, sin, dmat, qdec, kdec, cdec, lw["ret_g"])


def _ret_state_from_blockdiag(s_bd):
    b = s_bd.shape[0]
    s = s_bd.reshape(b, 2, N_HEADS, 32, N_HEADS, HD)
    s = jnp.stack([s[:, :, h, :, h, :] for h in range(N_HEADS)], axis=1)
    return s.reshape(b, N_HEADS, HD, HD)


ML_I_LANE = GLA_RANK
ML_F_LANE = GLA_RANK + N_HEADS


def _ml_prompt_kernel(m_ref, sm_ref, g_ref, y_ref, c_out_ref, n_out_ref, m_out_ref, c_ref, n_ref, mm_ref):
    nchunk = pl.program_id(1)

    @pl.when(nchunk == 0)
    def _():
        c_ref[...] = jnp.zeros_like(c_ref)
        n_ref[...] = jnp.zeros_like(n_ref)
        mm_ref[...] = jnp.zeros_like(mm_ref)

    c = CHUNK
    chunks = range(STEP_CHUNKS)
    row = lax.broadcasted_iota(jnp.int32, (c, c), 0)
    col = lax.broadcasted_iota(jnp.int32, (c, c), 1)
    tri_lo = jnp.where(col <= row, 1.0, 0.0).astype(BF16)
    tri_up = jnp.where(row <= col, 1.0, 0.0).astype(BF16)
    hv = _lane_head(BRANCH_W, HD)
    hq = _lane_head(N_HEADS * c, c)
    key = lax.broadcasted_iota(jnp.int32, (c, N_HEADS * c), 0)
    qry = lax.broadcasted_iota(jnp.int32, (c, N_HEADS * c), 1) % c
    causal_t = key <= qry
    r = lax.broadcasted_iota(jnp.int32, (BRANCH_W, BRANCH_W), 0) // HD
    cc = lax.broadcasted_iota(jnp.int32, (BRANCH_W, BRANCH_W), 1) // HD
    same_head = r == cc
    lane = lax.broadcasted_iota(jnp.int32, (1, 128), 1)
    gate_lane = lax.broadcasted_iota(jnp.int32, (128, 1), 0)
    is_i = (gate_lane >= ML_I_LANE) & (gate_lane < ML_I_LANE + N_HEADS)
    is_f = (gate_lane >= ML_F_LANE) & (gate_lane < ML_F_LANE + N_HEADS)
    gate_head = jnp.where(is_i, gate_lane - ML_I_LANE, gate_lane - ML_F_LANE)
    sel_q = jnp.where((is_i | is_f) & (gate_head == hq), 1.0, 0.0).astype(BF16)
    sel_v = jnp.where((is_i | is_f) & (gate_head == hv), 1.0, 0.0).astype(BF16)
    sel_qv = jnp.concatenate([sel_q, sel_v], axis=1)
    lane128 = lax.broadcasted_iota(jnp.int32, (1, 128), 1)
    in_i = (lane128 >= ML_I_LANE) & (lane128 < ML_I_LANE + N_HEADS)
    in_f = (lane128 >= ML_F_LANE) & (lane128 < ML_F_LANE + N_HEADS)
    head_rows = lax.broadcasted_iota(jnp.int32, (8, 1), 0)

    q, ks, vt, brow, u_q, u_v, sc = [], [], [], [], [], [], []
    for i in chunks:
        rows = slice(i * c, (i + 1) * c)
        q.append(m_ref[rows, 0:256])
        ks.append(m_ref[rows, 256:512] * (HD ** -0.5))
        vt.append(m_ref[rows, 512:768].T.astype(BF16))
        sm = sm_ref[rows, :]
        bcols = _dot_exact_lhs(tri_lo, jax.nn.log_sigmoid(sm))
        brows = _dot_exact_rhs(jax.nn.log_sigmoid(sm.T), tri_up)
        brow.append(jnp.concatenate([brows[ML_F_LANE + h:ML_F_LANE + h + 1, :] for h in range(N_HEADS)], axis=1))
        z = jnp.where(in_i, sm, 0.0) - jnp.where(in_f, bcols, 0.0)
        u = _dot_exact_rhs(z, sel_qv)
        u_q.append(u[:, 0:N_HEADS * c])
        u_v.append(u[:, N_HEADS * c:])
        sc.append(_dot_nt(ks[i].astype(BF16), _stack_heads(q[i], hv).astype(BF16)))

    mm = mm_ref[...]
    qk, w_inter, rowsum, floor, khat, carry_b = [], [], [], [], [], []
    for i in chunks:
        m_prev = _spread_heads([mm[:, h:h + 1] for h in range(N_HEADS)], hq)
        d_log = jnp.where(causal_t, brow[i] + u_q[i], -jnp.inf)
        inter_log = brow[i] + m_prev
        m_t = jnp.maximum(inter_log, jnp.max(d_log, axis=0, keepdims=True))
        qk_i = sc[i] * jnp.exp(d_log - m_t)
        qk.append(qk_i.astype(BF16))
        rowsum.append(jnp.sum(qk_i, axis=0, keepdims=True))
        w_inter.append(jnp.exp(inter_log - m_t))
        floor.append(jnp.exp(-m_t))
        m_new = [m_t[:, (h + 1) * c - 1:(h + 1) * c] for h in range(N_HEADS)]
        b_last = [brow[i][:, (h + 1) * c - 1:(h + 1) * c] for h in range(N_HEADS)]
        carry_b.append(_spread_heads([jnp.exp(b_last[h] + mm[:, h:h + 1] - m_new[h]) for h in range(N_HEADS)], hv))
        khat.append(ks[i] * jnp.exp(u_v[i] + _spread_heads([b_last[h] - m_new[h] for h in range(N_HEADS)], hv)))
        mm = jnp.zeros((1, 128), F32)
        for h in range(N_HEADS):
            mm = jnp.where(lane == h, m_new[h], mm)

    intra = [_dot(vt[i], qk[i]) for i in chunks]
    upd = [_dot(vt[i], khat[i].astype(BF16)) for i in chunks]
    cms, nrows = [c_ref[...]], [n_ref[...]]
    for i in chunks:
        cms.append(carry_b[i] * cms[i] + jnp.where(same_head, upd[i], 0.0))
        nrows.append(carry_b[i] * nrows[i] + jnp.sum(khat[i], axis=0, keepdims=True))
    hout = []
    for i in chunks:
        qb = q[i].astype(BF16)
        inter = _dot_nt(cms[i].astype(BF16), qb)
        n_sel = jnp.where(head_rows == hv, nrows[i], 0.0)
        qn = sum(_dot_nt(p, qb) for p in _split3(n_sel))
        blocks = []
        for h in range(N_HEADS):
            cols = slice(h * c, (h + 1) * c)
            den = w_inter[i][:, cols] * qn[h:h + 1, :] + rowsum[i][:, cols]
            num = w_inter[i][:, cols] * inter[h * HD:(h + 1) * HD, :] + intra[i][h * HD:(h + 1) * HD, cols]
            blocks.append(num / jnp.maximum(jnp.abs(den), floor[i][:, cols]))
        hout.append(jnp.concatenate(blocks, axis=0).T)
    y_ref[...] = _head_norm(jax.nn.sigmoid(m_ref[:, 768:1024]) * jnp.concatenate(hout, axis=0), g_ref[...])

    c_ref[...] = cms[-1]
    n_ref[...] = nrows[-1]
    mm_ref[...] = mm
    c_out_ref[...] = cms[-1]
    n_out_ref[...] = nrows[-1]
    m_out_ref[...] = mm


def _ml_prompt(ml, small, lw, bsz, seq):
    rows = STEP_CHUNKS * CHUNK
    nc = seq // rows
    return pl.pallas_call(
        _ml_prompt_kernel,
        grid=(bsz, nc),
        in_specs=[pl.BlockSpec((rows, W_ML), lambda b, n: (b * nc + n, 0)),
                  pl.BlockSpec((rows, W_SMALL), lambda b, n: (b * nc + n, 0)),
                  pl.BlockSpec((1, BRANCH_W), lambda b, n: (0, 0))],
        out_specs=[pl.BlockSpec((rows, BRANCH_W), lambda b, n: (b * nc + n, 0)),
                   pl.BlockSpec((None, BRANCH_W, BRANCH_W), lambda b, n: (b, 0, 0)),
                   pl.BlockSpec((None, 1, BRANCH_W), lambda b, n: (b, 0, 0)),
                   pl.BlockSpec((None, 1, 128), lambda b, n: (b, 0, 0))],
        out_shape=[jax.ShapeDtypeStruct((bsz * seq, BRANCH_W), F32),
                   jax.ShapeDtypeStruct((bsz, BRANCH_W, BRANCH_W), F32),
                   jax.ShapeDtypeStruct((bsz, 1, BRANCH_W), F32),
                   jax.ShapeDtypeStruct((bsz, 1, 128), F32)],
        scratch_shapes=[pltpu.VMEM((BRANCH_W, BRANCH_W), F32), pltpu.VMEM((1, BRANCH_W), F32),
                        pltpu.VMEM((1, 128), F32)],
        compiler_params=_cparams(("parallel", "arbitrary")),
        name="ml_prompt",
    )(ml, small, lw["ml_g"])


def _diag_blocks(s_bd):
    b = s_bd.shape[0]
    s = s_bd.reshape(b, N_HEADS, HD, N_HEADS, HD)
    return jnp.stack([s[:, h, :, h, :] for h in range(N_HEADS)], axis=1)


GLA_TILE = 256
GLA_STEP_TILES = 4
GLA_SUB = 16


def _gla_prompt_kernel(x_ref, sm_ref, wa2_ref, ba2_ref, g_ref, y_ref, s_out_ref, s_ref):
    n = pl.program_id(1)

    @pl.when(n == 0)
    def _():
        s_ref[...] = jnp.zeros_like(s_ref)

    tq, sc = GLA_TILE, GLA_SUB
    r = lax.broadcasted_iota(jnp.int32, (tq, tq), 0)
    c = lax.broadcasted_iota(jnp.int32, (tq, tq), 1)
    same = (r // sc) == (c // sc)
    cum_in_sub = jnp.where(same & (c <= r), 1.0, 0.0).astype(BF16)
    sum_in_sub = jnp.where(same, 1.0, 0.0).astype(BF16)
    kh = lax.broadcasted_iota(jnp.int32, (128, 1), 0) // GLA_DK
    hv = _lane_head(BRANCH_W, HD)
    expand = jnp.where(kh == hv, 1.0, 0.0).astype(BF16)
    trow = lax.broadcasted_iota(jnp.int32, (sc, 1), 0)
    n_sub = tq // sc
    chunks = range(GLA_STEP_TILES * n_sub)

    q, k, v, b, qd, dec_t, upds = [], [], [], [], [], [], []
    for t in range(GLA_STEP_TILES):
        rows = slice(t * tq, (t + 1) * tq)
        q.append(x_ref[rows, 0:128] * (GLA_DK ** -0.5))
        k.append(x_ref[rows, 128:256])
        v.append(x_ref[rows, 256:512])
        log_a = jax.nn.log_sigmoid(_dot(sm_ref[rows, :].astype(BF16), wa2_ref[...]) + ba2_ref[...]) / GLA_TAU
        b.append(_dot_exact_lhs(cum_in_sub, log_a))
        btot = _dot_exact_lhs(sum_in_sub, log_a)
        qd.append((q[t] * jnp.exp(b[t])).astype(BF16))
        dec_t.append(jnp.exp(btot).T)
        khat_b = (k[t] * jnp.exp(btot - b[t])).astype(BF16)
        vb = v[t].astype(BF16)
        upds += [_dot_tn(khat_b[ci * sc:(ci + 1) * sc], vb[ci * sc:(ci + 1) * sc]) for ci in range(n_sub)]
    states = [s_ref[...]]
    for g in chunks:
        t, lo = g // n_sub, (g % n_sub) * sc
        states.append(dec_t[t][:, lo:lo + 1] * states[g] + jnp.where(kh == hv, upds[g], 0.0))
    outs = []
    for g in chunks:
        t, lo = g // n_sub, (g % n_sub) * sc
        qc, kc, bc, vc = q[t][lo:lo + sc], k[t][lo:lo + sc], b[t][lo:lo + sc], v[t][lo:lo + sc]
        pair = [qc * kc[j:j + 1] * jnp.exp(jnp.where(trow >= j, bc - bc[j:j + 1], -jnp.inf)) for j in range(sc)]
        att = _dot(jnp.concatenate(pair, axis=0).astype(BF16), expand)
        o = att[0:sc] * vc[0:1]
        for j in range(1, sc):
            o = o + att[j * sc:(j + 1) * sc] * vc[j:j + 1]
        outs.append(o)
    outs = [outs[g] + _dot(qd[g // n_sub][(g % n_sub) * sc:(g % n_sub + 1) * sc], states[g].astype(BF16)) for g in chunks]
    s_ref[...] = states[-1]
    s_out_ref[...] = states[-1]
    y_ref[...] = jax.nn.silu(x_ref[:, 512:768]) * _head_norm(jnp.concatenate(outs, axis=0), g_ref[...])


def _gla_prompt(gla, small, lw, bsz, seq):
    tq = GLA_STEP_TILES * GLA_TILE
    nt = seq // tq
    full = lambda shape: pl.BlockSpec(shape, lambda b, n: (0,) * len(shape))
    return pl.pallas_call(
        _gla_prompt_kernel,
        grid=(bsz, nt),
        in_specs=[pl.BlockSpec((tq, W_GLA), lambda b, n: (b * nt + n, 0)),
                  pl.BlockSpec((tq, W_SMALL), lambda b, n: (b * nt + n, 0)),
                  full((128, 128)), full((1, 128)), full((1, BRANCH_W))],
        out_specs=[pl.BlockSpec((tq, BRANCH_W), lambda b, n: (b * nt + n, 0)),
                   pl.BlockSpec((None, 128, BRANCH_W), lambda b, n: (b, 0, 0))],
        out_shape=[jax.ShapeDtypeStruct((bsz * seq, BRANCH_W), F32),
                   jax.ShapeDtypeStruct((bsz, 128, BRANCH_W), F32)],
        scratch_shapes=[pltpu.VMEM((128, BRANCH_W), F32)],
        compiler_params=_cparams(("parallel", "arbitrary")),
        name="gla_prompt",
    )(gla, small, lw["wa2"], lw["ba2"], lw["gla_g"])


def _gla_state_from_blockdiag(s_bd):
    b = s_bd.shape[0]
    s = s_bd.reshape(b, N_HEADS, GLA_DK, N_HEADS, HD)
    return jnp.stack([s[:, h, :, h, :] for h in range(N_HEADS)], axis=1)


LANES = 128


def _rank1_update(st_ref, snt_ref, q_ref, k_ref, decay_row, v, dk):
    def body(i, o):
        rows = pl.ds(pl.multiple_of(i * HD, HD), HD)
        new = decay_row(i) * st_ref[rows, :] + k_ref[pl.ds(i, 1), :] * v
        snt_ref[rows, :] = new
        return o + q_ref[pl.ds(i, 1), :] * new
    return lax.fori_loop(0, dk, body, jnp.zeros((HD, LANES), F32), unroll=4)


def _head_rows(h, n):
    return pl.ds(pl.multiple_of(h * n, n), n)


def _gla_sample_kernel(x_ref, sm_ref, wa2_ref, ba2_ref, g_ref, s_ref, y_ref, s_out_ref,
                       qt_ref, kt_ref, at_ref, vt_ref, ot_ref):
    h = pl.program_id(0)

    @pl.when(h == 0)
    def _():
        x = x_ref[...]
        qt_ref[...] = (x[:, 0:128] * (GLA_DK ** -0.5)).T
        kt_ref[...] = x[:, 128:256].T
        vt_ref[...] = x[:, 256:512].T
        log_a = jax.nn.log_sigmoid(_dot(sm_ref[...].astype(BF16), wa2_ref[...]) + ba2_ref[...]) / GLA_TAU
        at_ref[...] = jnp.exp(log_a).T

    base = h * GLA_DK
    o = _rank1_update(s_ref, s_out_ref, qt_ref.at[_head_rows(h, GLA_DK)], kt_ref.at[_head_rows(h, GLA_DK)],
                      lambda i: at_ref[pl.ds(base + i, 1), :], vt_ref[_head_rows(h, HD), :], GLA_DK)
    ot_ref[_head_rows(h, HD), :] = o

    @pl.when(h == N_HEADS - 1)
    def _():
        y_ref[...] = jax.nn.silu(x_ref[:, 512:768]) * _head_norm(ot_ref[...].T, g_ref[...])


def _gla_sample(gla, small, state, lw):
    per_head = GLA_DK * HD
    li = lw["layer"]
    full = lambda shape: pl.BlockSpec(shape, lambda h: (0,) * len(shape))
    vm = lambda r: pltpu.VMEM((r, LANES), F32)
    return pl.pallas_call(
        _gla_sample_kernel,
        grid=(N_HEADS,),
        in_specs=[full((LANES, W_GLA)), full((LANES, W_SMALL)), full((128, 128)), full((1, 128)), full((1, BRANCH_W)),
                  pl.BlockSpec((None, per_head, LANES), lambda h: (li, h, 0))],
        out_specs=[full((LANES, BRANCH_W)), pl.BlockSpec((per_head, LANES), lambda h: (h, 0))],
        out_shape=[jax.ShapeDtypeStruct((LANES, BRANCH_W), F32), jax.ShapeDtypeStruct(state.shape[1:], F32)],
        scratch_shapes=[vm(128), vm(128), vm(128), vm(256), vm(256)],
        compiler_params=_cparams(("arbitrary",)),
        name="gla_sample",
    )(gla, small, lw["wa2"], lw["ba2"], lw["gla_g"], state)


def _ml_sample_kernel(x_ref, sc_ref, n_ref, g_ref, c_ref, y_ref, c_out_ref, n_out_ref, m_out_ref,
                      qt_ref, kt_ref, vt_ref, ot_ref, w_ref, kp_ref):
    h = pl.program_id(0)

    @pl.when(h == 0)
    def _():
        x = x_ref[...]
        qt_ref[...] = x[:, 0:256].T
        kt_ref[...] = (x[:, 256:512] * (HD ** -0.5)).T
        vt_ref[...] = x[:, 512:768].T
        sct = sc_ref[...].T
        log_f = jax.nn.log_sigmoid(sct[8:16])
        m_old = sct[16:24]
        m_new = jnp.maximum(log_f + m_old, sct[0:8])
        w_ref[0:8, :] = jnp.exp(log_f + m_old - m_new)
        w_ref[8:16, :] = jnp.exp(sct[0:8] - m_new)
        w_ref[16:24, :] = m_new

    w_carry = w_ref[pl.ds(h, 1), :]
    w_tok = w_ref[pl.ds(8 + h, 1), :]
    m_new = w_ref[pl.ds(16 + h, 1), :]
    kp_ref[...] = kt_ref[_head_rows(h, HD), :] * w_tok
    q_h = qt_ref.at[_head_rows(h, HD)]
    n_new = w_carry * n_ref[_head_rows(h, HD), :] + kp_ref[...]
    num = _rank1_update(c_ref, c_out_ref, q_h, kp_ref, lambda i: w_carry, vt_ref[_head_rows(h, HD), :], HD)
    den = jnp.sum(q_h[...] * n_new, axis=0, keepdims=True)
    ot_ref[_head_rows(h, HD), :] = num / jnp.maximum(jnp.abs(den), jnp.exp(-m_new))
    n_out_ref[_head_rows(h, HD), :] = n_new

    @pl.when(h == N_HEADS - 1)
    def _():
        y_ref[...] = _head_norm(jax.nn.sigmoid(x_ref[:, 768:1024]) * ot_ref[...].T, g_ref[...])
        m_out_ref[...] = jnp.concatenate([w_ref[16:24, :], jnp.zeros((LANES - 8, LANES), F32)], axis=0).T


def _ml_sample(ml, scal, n_state, c_state, lw):
    per_head = HD * HD
    li = lw["layer"]
    full = lambda shape: pl.BlockSpec(shape, lambda h: (0,) * len(shape))
    vm = lambda r: pltpu.VMEM((r, LANES), F32)
    return pl.pallas_call(
        _ml_sample_kernel,
        grid=(N_HEADS,),
        in_specs=[full((LANES, W_ML)), full((LANES, 128)),
                  pl.BlockSpec((None, BRANCH_W, LANES), lambda h: (li, 0, 0)), full((1, BRANCH_W)),
                  pl.BlockSpec((None, per_head, LANES), lambda h: (li, h, 0))],
        out_specs=[full((LANES, BRANCH_W)), pl.BlockSpec((per_head, LANES), lambda h: (h, 0)),
                   full((BRANCH_W, LANES)), full((LANES, 128))],
        out_shape=[jax.ShapeDtypeStruct((LANES, BRANCH_W), F32), jax.ShapeDtypeStruct(c_state.shape[1:], F32),
                   jax.ShapeDtypeStruct((BRANCH_W, LANES), F32), jax.ShapeDtypeStruct((LANES, 128), F32)],
        scratch_shapes=[vm(256), vm(256), vm(256), vm(256), vm(24), vm(HD)],
        compiler_params=_cparams(("arbitrary",)),
        name="ml_sample",
    )(ml, scal, n_state, lw["ml_g"], c_state)


def _ret_sample_kernel(x_ref, cos_ref, sin_ref, gam_ref, g_ref, s_ref, y_ref, s_out_ref,
                       qt_ref, kt_ref, vt_ref, ot_ref, qr_ref, kr_ref):
    h = pl.program_id(0)

    @pl.when(h == 0)
    def _():
        x = x_ref[...]
        qt_ref[...] = x[:, 0:256].T
        kt_ref[...] = (x[:, 256:512] * (HD ** -0.5)).T
        vt_ref[...] = x[:, 512:768].T

    cos, sin = cos_ref[...], sin_ref[...]
    for src, dst in ((qt_ref, qr_ref), (kt_ref, kr_ref)):
        x1 = src[_head_rows(h, 32), :]
        x2 = src[pl.ds(pl.multiple_of(128 + h * 32, 32), 32), :]
        dst[0:32, :] = x1 * cos - x2 * sin
        dst[32:64, :] = x1 * sin + x2 * cos
    gamma = gam_ref[pl.ds(h, 1), :]
    ot_ref[_head_rows(h, HD), :] = _rank1_update(s_ref, s_out_ref, qr_ref, kr_ref, lambda i: gamma,
                                                 vt_ref[_head_rows(h, HD), :], HD)

    @pl.when(h == N_HEADS - 1)
    def _():
        y_ref[...] = jax.nn.silu(x_ref[:, 768:1024]) * _head_norm(ot_ref[...].T, g_ref[...])


def _ret_sample(ret, state, lw):
    per_head = HD * HD
    li = lw["layer"]
    half = HD // 2
    freq = ROPE_BASE ** (-jnp.arange(half, dtype=F32) / half)
    ang = jnp.full((1,), PAST_LEN, F32)[:, None] * freq[None, :]
    cos = jnp.broadcast_to(jnp.cos(ang).reshape(half, 1), (half, LANES))
    sin = jnp.broadcast_to(jnp.sin(ang).reshape(half, 1), (half, LANES))
    log_g = jnp.log1p(-jnp.exp2(-5.0 - jnp.arange(N_HEADS, dtype=F32)))
    gam = jnp.broadcast_to(jnp.concatenate([jnp.exp(log_g), jnp.ones((4,), F32)]).reshape(8, 1), (8, LANES))
    full = lambda shape: pl.BlockSpec(shape, lambda h: (0,) * len(shape))
    vm = lambda r: pltpu.VMEM((r, LANES), F32)
    return pl.pallas_call(
        _ret_sample_kernel,
        grid=(N_HEADS,),
        in_specs=[full((LANES, W_RET)), full((half, LANES)), full((half, LANES)), full((8, LANES)), full((1, BRANCH_W)),
                  pl.BlockSpec((None, per_head, LANES), lambda h: (li, h, 0))],
        out_specs=[full((LANES, BRANCH_W)), pl.BlockSpec((per_head, LANES), lambda h: (h, 0))],
        out_shape=[jax.ShapeDtypeStruct((LANES, BRANCH_W), F32), jax.ShapeDtypeStruct(state.shape[1:], F32)],
        scratch_shapes=[vm(256), vm(256), vm(256), vm(256), vm(HD), vm(HD)],
        compiler_params=_cparams(("arbitrary",)),
        name="ret_sample",
    )(ret, cos, sin, gam, lw["ret_g"], state)


def _swa_sample_kernel(q_ref, kn_ref, vn_ref, knc_ref, vnc_ref, sink_ref, kc_ref, vc_ref, o_ref, ko_ref, vo_ref):
    q = q_ref[...]
    kc, vc = kc_ref[...], vc_ref[...]
    kn, vn = kn_ref[...], vn_ref[...]
    scale = HD ** -0.5
    s = _einsum("bhd,bdw->bhw", q.astype(BF16), kc.astype(BF16)) * scale
    s_new = jnp.sum(q * kn, axis=-1, keepdims=True) * scale
    w = lax.broadcasted_iota(jnp.int32, (1, 1, WINDOW), 2)
    s = jnp.where(w > 0, s, -jnp.inf)
    sink = sink_ref[...][:, 0:1]
    mx = jnp.maximum(jnp.maximum(jnp.max(s, axis=-1, keepdims=True), s_new), sink)
    p = jnp.exp(s - mx)
    p_new = jnp.exp(s_new - mx)
    den = jnp.sum(p, axis=-1, keepdims=True) + p_new + jnp.exp(sink - mx)
    o = _einsum("bhw,bdw->bhd", p.astype(BF16), vc.astype(BF16)) + p_new * vn
    o_ref[...] = o / den
    newest = w == WINDOW - 1
    ko_ref[...] = jnp.where(newest, knc_ref[...], pltpu.roll(kc, WINDOW - 1, 2))
    vo_ref[...] = jnp.where(newest, vnc_ref[...], pltpu.roll(vc, WINDOW - 1, 2))


def _swa_sample(swa, cache_k, cache_v, lw, bt=16):
    bsz = swa.shape[0]
    li = lw["layer"]
    cache = pl.BlockSpec((None, bt, WINDOW, 128), lambda i: (li, i, 0, 0))
    lo = (jnp.arange(128) < HD)[None, :]
    qa, qb = swa[:, 0:128], swa[:, 128:256]
    q = jnp.stack([jnp.where(lo, qa, 0.0), jnp.where(lo, qb, 0.0), jnp.where(lo, 0.0, qa), jnp.where(lo, 0.0, qb)], axis=1)
    kn = swa[:, None, 256:384]
    vn = swa[:, None, 384:512]
    sink = jnp.broadcast_to(lw["sinks"][0, 0:N_HEADS].reshape(N_HEADS, 1), (N_HEADS, 128))
    blk = lambda shape: pl.BlockSpec(shape, lambda i: (i,) + (0,) * (len(shape) - 1))
    o, ko, vo = pl.pallas_call(
        _swa_sample_kernel,
        grid=(bsz // bt,),
        in_specs=[blk((bt, N_HEADS, 128)), blk((bt, 1, 128)), blk((bt, 1, 128)), blk((bt, 128, 1)), blk((bt, 128, 1)),
                  pl.BlockSpec((N_HEADS, 128), lambda i: (0, 0)), cache, cache],
        out_specs=[blk((bt, N_HEADS, 128)), blk((bt, WINDOW, 128)), blk((bt, WINDOW, 128))],
        out_shape=[jax.ShapeDtypeStruct((bsz, N_HEADS, 128), F32), jax.ShapeDtypeStruct(cache_k.shape[1:], F32),
                   jax.ShapeDtypeStruct(cache_v.shape[1:], F32)],
        compiler_params=_cparams(("parallel",)),
        name="swa_sample",
    )(q, kn, vn, jnp.swapaxes(kn, 1, 2), jnp.swapaxes(vn, 1, 2), sink, cache_k, cache_v)
    y = jnp.concatenate([o[:, 0, 0:64], o[:, 2, 64:128], o[:, 1, 0:64], o[:, 3, 64:128]], axis=1)
    return y, ko, vo


def _ml_scalars(small, m_state):
    z4 = jnp.zeros((small.shape[0], 4), F32)
    return jnp.concatenate([small[:, ML_I_LANE:ML_I_LANE + 4], z4, small[:, ML_F_LANE:ML_F_LANE + 4], z4, m_state, z4,
                            jnp.zeros((small.shape[0], 128 - 24), F32)], axis=1)


IN_NAMES = ("gq", "gk", "gv", "gr", "ga", "sq", "sk", "sv", "mq", "mk", "mv", "mi", "mf", "mo", "rq", "rk", "rv", "rg",
            "gates")
IN_SIZES = (N_HEADS * GLA_DK, N_HEADS * GLA_DK, BRANCH_W, BRANCH_W, GLA_RANK, BRANCH_W, SWA_KV * HD, SWA_KV * HD,
            BRANCH_W, BRANCH_W, BRANCH_W, N_HEADS, N_HEADS, BRANCH_W, BRANCH_W, BRANCH_W, BRANCH_W, BRANCH_W,
            N_BRANCH * D_MODEL)
IN_START = {name: sum(IN_SIZES[:i]) for i, name in enumerate(IN_NAMES)}
GATE_COL = IN_START["gates"]


def _mixer_rows(w_t):
    x = w_t.shape[1]
    at = lambda name, n: w_t[IN_START[name]:IN_START[name] + n]
    swa_q = at("sq", BRANCH_W).reshape(2, 2, HD, x).transpose(1, 0, 2, 3).reshape(BRANCH_W, x)
    halves = lambda name: at(name, BRANCH_W).reshape(N_HEADS, 2, HD // 2, x).transpose(1, 0, 2, 3).reshape(BRANCH_W, x)
    parts = [at("gq", W_GLA), swa_q, at("sk", 2 * SWA_KV * HD), at("mq", 3 * BRANCH_W), at("mo", BRANCH_W),
             halves("rq"), halves("rk"), at("rv", 2 * BRANCH_W), at("ga", GLA_RANK), at("mi", 2 * N_HEADS)]
    pad = W_MIX - sum(p.shape[0] for p in parts)
    return jnp.concatenate(parts + [jnp.zeros((pad, x), w_t.dtype)], axis=0)


def _prep_layer(li, w_in, b_in, gla_w_a2, gla_b_a2, gla_norm_g, swa_sinks, ml_norm_g, ret_norm_g, w_branch, w_out,
                ln1_g, ln1_b, w_router_grp, b_router_grp, w_router_exp, b_router_exp, w_exp_gate, w_exp_up,
                w_exp_down, ln2_g, ln2_b):
    row = lambda v: v.reshape(1, -1).astype(F32)
    wr = jnp.concatenate([w_router_grp[li], w_router_exp[li],
                          jnp.zeros((D_MODEL, 128 - N_GROUPS - N_EXPERTS), F32)], axis=1)
    wr_hi = wr.astype(BF16)
    wb = w_branch[li]
    wb1 = jnp.concatenate([wb[1, 0:64], wb[1, 128:192], wb[1, 64:128], wb[1, 192:256]], axis=0)
    w_t = w_in[li].T
    return dict(
        wm=_mixer_rows(w_t).astype(BF16),
        bm=row(_mixer_rows(b_in[li][:, None])),
        wg=w_t[GATE_COL:].astype(BF16),
        bg=row(b_in[li][GATE_COL:]),
        wa2=jnp.concatenate([gla_w_a2[li], jnp.zeros((128 - GLA_RANK, 128), F32)], axis=0).astype(BF16),
        ba2=row(gla_b_a2[li]),
        gla_g=row(gla_norm_g[li]), ml_g=row(ml_norm_g[li]), ret_g=row(ret_norm_g[li]),
        sinks=row(jnp.concatenate([swa_sinks[li], jnp.zeros((128 - N_HEADS,), F32)])),
        wbr=jnp.stack([wb[0], wb1, wb[2], wb[3]]).astype(BF16),
        wout=w_out[li].astype(BF16),
        ln1_g=row(ln1_g[li]), ln1_b=row(ln1_b[li]), ln2_g=row(ln2_g[li]), ln2_b=row(ln2_b[li]),
        wr_hi=wr_hi, wr_lo=(wr - wr_hi.astype(F32)).astype(BF16),
        br=row(jnp.concatenate([b_router_grp[li], b_router_exp[li], jnp.zeros((128 - N_GROUPS - N_EXPERTS,), F32)])),
        layer=li, we_gate=w_exp_gate, we_up=w_exp_up, we_down=w_exp_down,
    )


TOK_TILE = 256
MOE_PIECES_PROMPT = 32
MOE_PIECES_SAMPLE = 1


MERGE_TILES = 2


def _ffn_half(x2, ys, lw, tm, bp):
    x1, route, cnt, xs = _merge(x2, ys, lw, tm, MERGE_TILES if x2.shape[0] // tm % MERGE_TILES == 0 else 1)
    return _moe(x1, route, cnt, xs, lw, bp, tm)


def _layer_prompt(x2, lw, bsz, seq):
    gla, swa, ml, ret, small = _inproj(x2, lw["wm"], lw["bm"], 2 * TOK_TILE)
    ya, s_gla = _gla_prompt(gla, small, lw, bsz, seq)
    yb = _swa_prompt(swa, lw, bsz, seq)
    yc, c_bd, n_row, m_row = _ml_prompt(ml, small, lw, bsz, seq)
    yd, s_ret = _ret_prompt(ret, lw, bsz, seq)
    kv_tail = swa.reshape(bsz, seq, W_SWA)[:, seq - WINDOW:, 256:512]
    states = (_gla_state_from_blockdiag(s_gla),
              kv_tail[:, :, 0:128].reshape(bsz, WINDOW, SWA_KV, HD),
              kv_tail[:, :, 128:256].reshape(bsz, WINDOW, SWA_KV, HD),
              jnp.swapaxes(_diag_blocks(c_bd), -1, -2), n_row.reshape(bsz, N_HEADS, HD), m_row[:, 0, 0:N_HEADS],
              _ret_state_from_blockdiag(s_ret))
    return _ffn_half(x2, (ya, yb, yc, yd), lw, TOK_TILE, MOE_PIECES_PROMPT), states


def _layer_sample(x2, st, lw):
    s_gla, k_buf, v_buf, c_ml, n_ml, m_ml, s_ret = st
    bsz = x2.shape[0]
    depth = s_gla.shape[0]
    gla, swa, ml, ret, small = _inproj(x2, lw["wm"], lw["bm"], bsz)
    to_lanes = lambda s: jnp.moveaxis(s, 1, -1).reshape(depth, -1, bsz)
    from_lanes = lambda s, like: jnp.moveaxis(s.reshape(like.shape[2:] + (bsz,)), -1, 0)
    cache_t = lambda c: jnp.swapaxes(c.reshape(depth, bsz, WINDOW, SWA_KV * HD), 2, 3)
    ya, s_gla_new = _gla_sample(gla, small, to_lanes(s_gla), lw)
    yb, k_new, v_new = _swa_sample(swa, cache_t(k_buf), cache_t(v_buf), lw)
    k_new, v_new = jnp.swapaxes(k_new, 1, 2), jnp.swapaxes(v_new, 1, 2)
    yc, c_new, n_new, m_new = _ml_sample(ml, _ml_scalars(small, m_ml[lw["layer"]]), to_lanes(n_ml), to_lanes(c_ml), lw)
    yd, s_ret_new = _ret_sample(ret, to_lanes(s_ret), lw)
    states = (from_lanes(s_gla_new, s_gla), k_new.reshape(k_buf.shape[1:]), v_new.reshape(v_buf.shape[1:]),
              from_lanes(c_new, c_ml), from_lanes(n_new, n_ml), m_new[:, 0:N_HEADS], from_lanes(s_ret_new, s_ret))
    return _ffn_half(x2, (ya, yb, yc, yd), lw, bsz, MOE_PIECES_SAMPLE), states


def kernel(x_prompt, x_sample, state_gla, cache_swa_k, cache_swa_v, state_mlstm_c, state_mlstm_n, state_mlstm_m,
           state_ret, w_in, b_in, gla_w_a2, gla_b_a2, gla_norm_g, swa_sinks, ml_norm_g, ret_norm_g, w_branch, w_out,
           ln1_g, ln1_b, w_router_grp, b_router_grp, w_router_exp, b_router_exp, w_exp_gate, w_exp_up, w_exp_down,
           ln2_g, ln2_b):
    bsz, seq, _ = x_prompt.shape
    dec_b, dec_seq, _ = x_sample.shape
    assert dec_seq == 1 and dec_b == LANES and (bsz * seq) % (2 * TOK_TILE) == 0
    assert all(seq % n == 0 for n in (GLA_STEP_TILES * GLA_TILE, SWA_BLOCKS * WINDOW, STEP_CHUNKS * CHUNK))
    weights = (w_in, b_in, gla_w_a2, gla_b_a2, gla_norm_g, swa_sinks, ml_norm_g, ret_norm_g, w_branch, w_out,
               ln1_g, ln1_b, w_router_grp, b_router_grp, w_router_exp, b_router_exp, w_exp_gate, w_exp_up,
               w_exp_down, ln2_g, ln2_b)
    y_p = x_prompt.reshape(bsz * seq, D_MODEL)
    y_s = x_sample.reshape(dec_b, D_MODEL)
    new_p, new_s = [], []
    for li in range(w_in.shape[0]):
        lw = _prep_layer(li, *weights)
        y_p, st_p = _layer_prompt(y_p, lw, bsz, seq)
        st_in = (state_gla, cache_swa_k, cache_swa_v, state_mlstm_c, state_mlstm_n, state_mlstm_m, state_ret)
        y_s, st_s = _layer_sample(y_s, st_in, lw)
        new_p.append(st_p)
        new_s.append(st_s)
    outs = [y_p.reshape(bsz, seq, D_MODEL), y_s.reshape(dec_b, dec_seq, D_MODEL)]
    for i in range(7):
        outs.append(jnp.stack([s[i] for s in new_p], axis=0))
        outs.append(jnp.stack([s[i] for s in new_s], axis=0))
    return tuple(outs)
```

```python
import functools

import jax
import jax.numpy as jnp
from jax import lax
from jax.experimental import pallas as pl
from jax.experimental.pallas import tpu as pltpu

F32 = jnp.float32
BF16 = jnp.bfloat16

D_MODEL = 1024
N_BRANCH = 4
BRANCH_W = D_MODEL // N_BRANCH
N_HEADS = 4
GLA_DK = 32
GLA_RANK = 16
GLA_TAU = 16.0
HD = 64
SWA_KV = 2
WINDOW = 128
ROPE_BASE = 10000.0
PAST_LEN = 16384
N_GROUPS = 4
EPG = 8
N_EXPERTS = N_GROUPS * EPG
D_EXPERT = D_MODEL // 4
DEPTH = 2
ALPHA = (2 * DEPTH) ** 0.25
LN_EPS = 1e-5
VMEM_LIMIT = 56 * 1024 * 1024

W_GLA, W_SWA, W_ML, W_RET, W_SMALL = 768, 512, 1024, 1024, 128
W_MIX = W_GLA + W_SWA + W_ML + W_RET + W_SMALL


def _cparams(sem):
    return pltpu.CompilerParams(dimension_semantics=sem, vmem_limit_bytes=VMEM_LIMIT)


def _dot(a, b):
    return jnp.dot(a, b, preferred_element_type=F32)


def _einsum(spec, a, b):
    return jnp.einsum(spec, a, b, preferred_element_type=F32)


def _split2(x):
    hi = x.astype(BF16)
    lo = (x - hi.astype(F32)).astype(BF16)
    return hi, lo


def _split3(x):
    hi = x.astype(BF16)
    r = x - hi.astype(F32)
    mid = r.astype(BF16)
    lo = (r - mid.astype(F32)).astype(BF16)
    return hi, mid, lo


def _dot_exact_lhs(a_exact, b):
    return sum(_dot(a_exact, p) for p in _split3(b))


def _dot_exact_rhs(a, b_exact):
    return sum(_dot(p, b_exact) for p in _split3(a))


def _layer_norm(x, g, b):
    mu = jnp.mean(x, axis=-1, keepdims=True)
    xc = x - mu
    var = jnp.mean(xc * xc, axis=-1, keepdims=True)
    return xc * lax.rsqrt(var + LN_EPS) * g + b


def _inproj_kernel(x_ref, w_ref, b_ref, gla_ref, swa_ref, ml_ref, ret_ref, sm_ref):
    xb = x_ref[...].astype(BF16)
    off = 0
    for ref, width in ((gla_ref, W_GLA), (swa_ref, W_SWA), (ml_ref, W_ML), (ret_ref, W_RET), (sm_ref, W_SMALL)):
        ref[...] = _dot_nt(xb, w_ref[off:off + width, :]) + b_ref[:, off:off + width]
        off += width


def _inproj(x2, wm, bm, tm):
    t = x2.shape[0]
    widths = (W_GLA, W_SWA, W_ML, W_RET, W_SMALL)
    return pl.pallas_call(
        _inproj_kernel,
        grid=(t // tm,),
        in_specs=[pl.BlockSpec((tm, D_MODEL), lambda i: (i, 0)),
                  pl.BlockSpec((W_MIX, D_MODEL), lambda i: (0, 0)),
                  pl.BlockSpec((1, W_MIX), lambda i: (0, 0))],
        out_specs=[pl.BlockSpec((tm, w), lambda i: (i, 0)) for w in widths],
        out_shape=[jax.ShapeDtypeStruct((t, w), F32) for w in widths],
        compiler_params=_cparams(("parallel",)),
        name="inproj",
    )(x2, wm, bm)


def _merge_kernel(x_ref, ya_ref, yb_ref, yc_ref, yd_ref, wg_ref, bg_ref, wbr_ref, wout_ref, g1_ref, b1_ref,
                  wrh_ref, wrl_ref, br_ref, x1_ref, route_ref, cnt_ref, xs_ref, *, tm, sub):
    rows_t = xs_ref.shape[0] // sub
    x1, picks = [], []
    for t in range(sub):
        rows = slice(t * tm, (t + 1) * tm)
        x = x_ref[rows, :]
        xb = x.astype(BF16)
        merged = jnp.zeros((tm, D_MODEL), F32)
        for n, y_ref in enumerate((ya_ref, yb_ref, yc_ref, yd_ref)):
            cols = slice(n * D_MODEL, (n + 1) * D_MODEL)
            gate = 0.5 * jnp.tanh(0.5 * (_dot_nt(xb, wg_ref[cols, :]) + bg_ref[:, cols])) + 0.5
            merged = merged + gate * _dot(y_ref[rows, :].astype(BF16), wbr_ref[n])
        mix = _dot(merged.astype(BF16), wout_ref[...])
        x1.append(_layer_norm(ALPHA * x + mix, g1_ref[...], b1_ref[...]))
        x1_ref[rows, :] = x1[t]
        picks.append(_route(x1[t], wrh_ref[...], wrl_ref[...], br_ref[...], tm))
    for t in range(sub):
        route, cnt, xs = _sort(x1[t], picks[t], tm, rows_t)
        route_ref[t * tm:(t + 1) * tm, :] = route
        cnt_ref[t] = cnt
        xs_ref[t * rows_t:(t + 1) * rows_t, :] = xs


def _route(x1, wr_hi, wr_lo, br, tm):
    xh, xl = _split2(x1)
    both = _dot(xh, jnp.concatenate([wr_hi, wr_lo], axis=1))
    logits = both[:, 0:128] + both[:, 128:256] + _dot(xl, wr_hi) + br
    lane = lax.broadcasted_iota(jnp.int32, (tm, 128), 1)
    lanef = lane.astype(F32)
    neg = jnp.float32(-jnp.inf)
    big = jnp.float32(1e9)
    isg = lane < N_GROUPS
    gmax = jnp.max(jnp.where(isg, logits, neg), axis=-1, keepdims=True)
    gidx = jnp.min(jnp.where(isg & (logits == gmax), lanef, big), axis=-1, keepdims=True)
    p_group = 1.0 / jnp.sum(jnp.where(isg, jnp.exp(logits - gmax), 0.0), axis=-1, keepdims=True)
    lo = N_GROUPS + EPG * gidx
    em = (lanef >= lo) & (lanef < lo + EPG)
    v1 = jnp.max(jnp.where(em, logits, neg), axis=-1, keepdims=True)
    i1 = jnp.min(jnp.where(em & (logits == v1), lanef, big), axis=-1, keepdims=True)
    em2 = em & (lanef != i1)
    v2 = jnp.max(jnp.where(em2, logits, neg), axis=-1, keepdims=True)
    i2 = jnp.min(jnp.where(em2 & (logits == v2), lanef, big), axis=-1, keepdims=True)
    e21 = jnp.exp(v2 - v1)
    gt1 = 1.0 / (1.0 + e21)
    return i1, i2, p_group * gt1, p_group * (e21 * gt1)


def _sort(x1, picks, tm, rows_t):
    i1, i2, gate1, gate2 = picks
    lane = lax.broadcasted_iota(jnp.int32, (tm, 128), 1)
    lanef = lane.astype(F32)
    sel1 = lanef == i1
    sel2 = lanef == i2
    onehot = jnp.where(sel1 | sel2, 1.0, 0.0)
    cnt = jnp.sum(onehot, axis=0, keepdims=True)
    seg = jnp.floor((cnt + (PIECE - 1)) * (1.0 / PIECE)) * PIECE
    er = lax.broadcasted_iota(jnp.int32, (128, 128), 0)
    ec = lax.broadcasted_iota(jnp.int32, (128, 128), 1)
    seg_off = _dot(jnp.broadcast_to(seg, (8, 128)).astype(BF16), jnp.where(er < ec, 1.0, 0.0).astype(BF16))[0:1]
    row = lax.broadcasted_iota(jnp.int32, (tm, tm), 0)
    col = lax.broadcasted_iota(jnp.int32, (tm, tm), 1)
    earlier = _dot(jnp.where(col < row, 1.0, 0.0).astype(BF16), onehot.astype(BF16))
    where_to = seg_off + earlier
    w1 = jnp.where(sel1, where_to, 0.0)
    w2 = jnp.where(sel2, where_to, 0.0)
    pos1 = jnp.sum(w1, axis=-1, keepdims=True)
    pos2 = jnp.sum(w2, axis=-1, keepdims=True)
    ones = jnp.ones((8, 128), BF16)
    prow1 = sum(_dot_nt(ones, p) for p in _split2(w1))[0:1]
    prow2 = sum(_dot_nt(ones, p) for p in _split2(w2))[0:1]
    r = lax.broadcasted_iota(jnp.int32, (rows_t, tm), 0).astype(F32)
    place = jnp.where((r == prow1) | (r == prow2), 1.0, 0.0).astype(BF16)
    xs = _dot(place, x1.astype(BF16)).astype(BF16)

    vals = (i1 - N_GROUPS, i2 - N_GROUPS, gate1, gate2, pos1, pos2)
    route = jnp.zeros((tm, 128), F32)
    for k, v in enumerate(vals):
        route = jnp.where(lane == k, v, route)
    return route, cnt, xs


PIECE = 16


def _tile_rows(tm):
    return -(-(2 * tm + N_EXPERTS * (PIECE - 1)) // 128) * 128


def _merge(x2, ys, lw, tm, sub):
    t = x2.shape[0]
    nt = t // tm
    rows_t = _tile_rows(tm)
    tok = lambda w: pl.BlockSpec((sub * tm, w), lambda i: (i, 0))
    full = lambda shape: pl.BlockSpec(shape, lambda i: (0,) * len(shape), pipeline_mode=pl.Buffered(1))
    return pl.pallas_call(
        functools.partial(_merge_kernel, tm=tm, sub=sub),
        grid=(nt // sub,),
        in_specs=[tok(D_MODEL), tok(BRANCH_W), tok(BRANCH_W), tok(BRANCH_W), tok(BRANCH_W),
                  full((N_BRANCH * D_MODEL, D_MODEL)), full((1, N_BRANCH * D_MODEL)),
                  full((N_BRANCH, BRANCH_W, D_MODEL)), full((D_MODEL, D_MODEL)),
                  full((1, D_MODEL)), full((1, D_MODEL)),
                  full((D_MODEL, 128)), full((D_MODEL, 128)), full((1, 128))],
        out_specs=[tok(D_MODEL), tok(128), pl.BlockSpec((sub, 1, 128), lambda i: (i, 0, 0)),
                   pl.BlockSpec((sub * rows_t, D_MODEL), lambda i: (i, 0))],
        out_shape=[jax.ShapeDtypeStruct((t, D_MODEL), F32), jax.ShapeDtypeStruct((t, 128), F32),
                   jax.ShapeDtypeStruct((nt, 1, 128), F32), jax.ShapeDtypeStruct((nt * rows_t, D_MODEL), BF16)],
        compiler_params=_cparams(("parallel",)),
        name="merge",
    )(x2, *ys, lw["wg"], lw["bg"], lw["wbr"], lw["wout"], lw["ln1_g"], lw["ln1_b"], lw["wr_hi"], lw["wr_lo"], lw["br"])


def _piece_copy(hbm, hbm_row, vmem, piece, sem, to_vmem):
    h = hbm.at[pl.ds(pl.multiple_of(hbm_row, PIECE), PIECE), :]
    v = vmem.at[pl.ds(pl.multiple_of(piece * PIECE, PIECE), PIECE), :]
    return pltpu.make_async_copy(h, v, sem) if to_vmem else pltpu.make_async_copy(v, h, sem)


EXPERT_SECTION_ROWS = 512


def _expert_kernel(be_ref, nu_ref, nv_ref, off_ref, xs_hbm, wg_ref, wu_ref, wd_ref, ys_hbm, xin, yout, sem_in, sem_out,
                   *, bp):
    del xs_hbm
    blk = pl.program_id(0)
    n_used = nu_ref[0]
    slot = blk % 2
    sec_rows = min(bp * PIECE, EXPERT_SECTION_ROWS)
    sections = bp * PIECE // sec_rows

    def pieces(b, s, buf, sem, to_vmem, start):
        n = nv_ref[b]

        def one(p, priority):
            cp = _piece_copy(ys_hbm, off_ref[b * bp + p], buf.at[s], p, sem.at[s], to_vmem)
            cp.start(priority=priority) if start else cp.wait()

        def body(i, c):
            one(2 * i, 0)

            @pl.when(2 * i + 1 < n)
            def _():
                one(2 * i + 1, 1)
            return c
        lax.fori_loop(0, (n + 1) // 2, body, 0)

    @pl.when(blk == 0)
    def _():
        xin[...] = jnp.zeros_like(xin)
        for first in range(2):
            @pl.when(first < n_used)
            def _():
                pieces(first, first, xin, sem_in, True, True)

    @pl.when(blk + 2 < n_used)
    def _():
        pieces(blk + 2, (blk + 2) % 3, xin, sem_in, True, True)

    @pl.when(blk < n_used)
    def _():
        in_slot = blk % 3
        pieces(blk, in_slot, xin, sem_in, True, False)

        @pl.when(blk >= 2)
        def _():
            pieces(blk - 2, slot, yout, sem_out, False, False)

        def mlp(lo, hi):
            xb = xin[in_slot, lo:hi]
            hid = jax.nn.silu(_dot(xb, wg_ref[...].astype(BF16))) * _dot(xb, wu_ref[...].astype(BF16))
            yout[slot, lo:hi] = _dot(hid.astype(BF16), wd_ref[...].astype(BF16)).astype(BF16)

        for sec in range(sections):
            if sec == 0:
                mlp(0, sec_rows)
            else:
                @pl.when(nv_ref[blk] * PIECE > sec * sec_rows)
                def _():
                    mlp(sec * sec_rows, (sec + 1) * sec_rows)
        pieces(blk, slot, yout, sem_out, False, True)

    @pl.when(blk == n_used - 1)
    def _():
        @pl.when(blk >= 1)
        def _():
            pieces(blk - 1, 1 - slot, yout, sem_out, False, False)
        pieces(blk, slot, yout, sem_out, False, False)


def _experts(xs, block_expert, n_used, n_valid, piece_off, lw, bp):
    n_blocks = block_expert.shape[0]
    rows = bp * PIECE
    li = lw["layer"]
    wspec = lambda shape: pl.BlockSpec((None, None) + shape, lambda b, be, nu, nv, off: (li, be[b], 0, 0))
    grid_spec = pltpu.PrefetchScalarGridSpec(
        num_scalar_prefetch=4,
        grid=(n_blocks,),
        in_specs=[pl.BlockSpec(memory_space=pl.ANY), wspec((D_MODEL, D_EXPERT)), wspec((D_MODEL, D_EXPERT)),
                  wspec((D_EXPERT, D_MODEL))],
        out_specs=pl.BlockSpec(memory_space=pl.ANY),
        scratch_shapes=[pltpu.VMEM((3, rows, D_MODEL), BF16), pltpu.VMEM((2, rows, D_MODEL), BF16),
                        pltpu.SemaphoreType.DMA((3,)), pltpu.SemaphoreType.DMA((2,))],
    )
    return pl.pallas_call(
        functools.partial(_expert_kernel, bp=bp),
        grid_spec=grid_spec,
        out_shape=jax.ShapeDtypeStruct(xs.shape, BF16),
        input_output_aliases={4: 0},
        compiler_params=_cparams(("arbitrary",)),
        name="experts",
    )(block_expert, n_used, n_valid, piece_off, xs, lw["we_gate"], lw["we_up"], lw["we_down"])


COMBINE_TILES = 4


def _combine_kernel(ys_ref, x1_ref, route_ref, g2_ref, b2_ref, out_ref, *, tm, sub):
    rows_t = ys_ref.shape[0] // sub
    r = lax.broadcasted_iota(jnp.int32, (tm, rows_t), 1).astype(F32)
    routes = [route_ref[t * tm:(t + 1) * tm, :] for t in range(sub)]
    picks = [[jnp.where(r == routes[t][:, 4 + j:5 + j], 1.0, 0.0).astype(BF16) for j in range(2)] for t in range(sub)]
    rows = [[_dot(picks[t][j], ys_ref[t * rows_t:(t + 1) * rows_t, :]) for j in range(2)] for t in range(sub)]
    for t in range(sub):
        ffn = routes[t][:, 2:3] * rows[t][0] + routes[t][:, 3:4] * rows[t][1]
        tok = slice(t * tm, (t + 1) * tm)
        out_ref[tok, :] = _layer_norm(ALPHA * x1_ref[tok, :] + ffn, g2_ref[...], b2_ref[...])


def _combine(ys, x1, route, lw, tm):
    t = x1.shape[0]
    rows_t = _tile_rows(tm)
    sub = COMBINE_TILES if (t // tm) % COMBINE_TILES == 0 else 1
    full = lambda shape: pl.BlockSpec(shape, lambda i: (0,) * len(shape))
    return pl.pallas_call(
        functools.partial(_combine_kernel, tm=tm, sub=sub),
        grid=(t // (sub * tm),),
        in_specs=[pl.BlockSpec((sub * rows_t, D_MODEL), lambda i: (i, 0)),
                  pl.BlockSpec((sub * tm, D_MODEL), lambda i: (i, 0)),
                  pl.BlockSpec((sub * tm, 128), lambda i: (i, 0)), full((1, D_MODEL)), full((1, D_MODEL))],
        out_specs=pl.BlockSpec((sub * tm, D_MODEL), lambda i: (i, 0)),
        out_shape=jax.ShapeDtypeStruct((t, D_MODEL), F32),
        compiler_params=_cparams(("parallel",)),
        name="combine",
    )(ys, x1, route, lw["ln2_g"], lw["ln2_b"])


def _piece_tables(cnt, tm, bp):
    nt = cnt.shape[0]
    rows_t = _tile_rows(tm)
    counts = cnt[:, 0, N_GROUPS:N_GROUPS + N_EXPERTS].astype(jnp.int32)
    pcs = ((counts + PIECE - 1) // PIECE).T
    seg_row = ((jnp.cumsum(pcs, axis=0) - pcs) * PIECE + jnp.arange(nt, dtype=jnp.int32)[None, :] * rows_t)
    tile_end = jnp.cumsum(pcs, axis=1)
    total = tile_end[:, -1]
    e_end = jnp.cumsum((total + bp - 1) // bp * bp)
    e_start = jnp.concatenate([jnp.zeros((1,), jnp.int32), e_end[:-1]])
    n_blocks = -(-(nt * rows_t // PIECE + N_EXPERTS * (bp - 1)) // bp)
    g = jnp.arange(n_blocks * bp, dtype=jnp.int32)[:, None]
    in_e = ((e_start[None, :] <= g) & (g < e_end[None, :])).astype(F32)
    pick = lambda table: jnp.dot(in_e, table.astype(F32), precision=lax.Precision.HIGHEST)
    local = g.astype(F32) - pick(e_start[:, None])
    ends, starts, rows = pick(tile_end), pick(tile_end - pcs), pick(seg_row)
    in_t = ((starts <= local) & (local < ends)).astype(F32)
    off = jnp.sum(in_t * (rows + (local - starts) * PIECE), axis=1).astype(jnp.int32)
    n_valid = jnp.sum(jnp.sum(in_t, axis=1).reshape(n_blocks, bp), axis=1).astype(jnp.int32)
    expert = pick(jnp.arange(N_EXPERTS, dtype=jnp.int32)[:, None])[:, 0]
    block_expert = expert.reshape(n_blocks, bp)[:, 0].astype(jnp.int32)
    n_used = (e_end[-1] // bp).astype(jnp.int32).reshape(1)
    return block_expert, n_used, n_valid, off


def _moe(x1, route, cnt, xs, lw, bp, tm):
    ys = _experts(xs, *_piece_tables(cnt, tm, bp), lw, bp)
    return _combine(ys, x1, route, lw, tm)


CHUNK = 128
STEP_CHUNKS = 16


def _dot_nt(a, b):
    return lax.dot_general(a, b, (((1,), (1,)), ((), ())), preferred_element_type=F32)


def _dot_tn(a, b):
    return lax.dot_general(a, b, (((0,), (0,)), ((), ())), preferred_element_type=F32)


def _lane_head(width, group):
    return lax.broadcasted_iota(jnp.int32, (1, width), 1) // group


def _head_norm(o, gain):
    r = lax.broadcasted_iota(jnp.int32, (BRANCH_W, BRANCH_W), 0) // HD
    c = lax.broadcasted_iota(jnp.int32, (BRANCH_W, BRANCH_W), 1) // HD
    avg = jnp.where(r == c, 1.0 / HD, 0.0).astype(BF16)
    mu = _dot_exact_rhs(o, avg)
    xc = o - mu
    var = _dot_exact_rhs(xc * xc, avg)
    return xc * lax.rsqrt(var + LN_EPS) * gain


def _stack_heads(x, head_of_lane):
    return jnp.concatenate([jnp.where(head_of_lane == h, x, 0.0) for h in range(N_HEADS)], axis=0)


def _pick_heads(stacked, rows, head_of_lane):
    out = jnp.where(head_of_lane == 0, stacked[0:rows], 0.0)
    for h in range(1, N_HEADS):
        out = jnp.where(head_of_lane == h, stacked[h * rows:(h + 1) * rows], out)
    return out


def _spread_heads(cols, head_of_lane):
    out = jnp.where(head_of_lane == 0, cols[0], 0.0)
    for h in range(1, N_HEADS):
        out = jnp.where(head_of_lane == h, cols[h], out)
    return out


SWA_BLOCKS = 16


def _swa_prompt_kernel(q_ref, kv_ref, kvp_ref, sink_ref, y_ref):
    n = pl.program_id(1)
    lo = lax.broadcasted_iota(jnp.int32, (1, 128), 1) < HD
    e = lax.broadcasted_iota(jnp.int32, (2 * WINDOW, N_HEADS * WINDOW), 0)
    col = lax.broadcasted_iota(jnp.int32, (2 * WINDOW, N_HEADS * WINDOW), 1)
    a = col % WINDOW
    band = (e > a) & (e <= a + WINDOW)
    head = lax.broadcasted_iota(jnp.int32, (1, N_HEADS * WINDOW), 1) // WINDOW
    sink = _spread_heads([sink_ref[:, h:h + 1] for h in range(N_HEADS)], head)
    blocks = range(SWA_BLOCKS)
    scores, vexts = [], []
    for i in blocks:
        rows = slice(i * WINDOW, (i + 1) * WINDOW)
        q = q_ref[rows, :]
        qa, qb = q[:, 0:128], q[:, 128:256]
        kv = kv_ref[rows, :]
        kvp = kvp_ref[...] if i == 0 else kv_ref[(i - 1) * WINDOW:i * WINDOW, :]
        kext = jnp.concatenate([kvp[:, 0:128], kv[:, 0:128]], axis=0).astype(BF16)
        vexts.append(jnp.concatenate([kvp[:, 128:256], kv[:, 128:256]], axis=0).astype(BF16))
        qs = jnp.concatenate([jnp.where(lo, qa, 0.0), jnp.where(lo, qb, 0.0),
                              jnp.where(lo, 0.0, qa), jnp.where(lo, 0.0, qb)], axis=0).astype(BF16)
        scores.append(_dot_nt(kext, qs))
    probs, inv_den = [], []
    for i in blocks:
        valid = band & ((n > 0) | (e >= WINDOW)) if i == 0 else band
        s = jnp.where(valid, scores[i] * (HD ** -0.5), -jnp.inf)
        mx = jnp.maximum(jnp.max(s, axis=0, keepdims=True), sink)
        p = jnp.exp(s - mx)
        inv_den.append(1.0 / (jnp.sum(p, axis=0, keepdims=True) + jnp.exp(sink - mx)))
        probs.append(p.astype(BF16))
    outs = [_dot_tn(vexts[i], probs[i]) for i in blocks]
    for i in blocks:
        o = outs[i] * inv_den[i]
        ya = jnp.concatenate([o[0:HD, 0:128], o[HD:128, 256:384]], axis=0).T
        yb = jnp.concatenate([o[0:HD, 128:256], o[HD:128, 384:512]], axis=0).T
        y_ref[i * WINDOW:(i + 1) * WINDOW, :] = jnp.concatenate([ya, yb], axis=1)


def _swa_prompt(swa, lw, bsz, seq):
    qt = SWA_BLOCKS * WINDOW
    nt = seq // qt
    return pl.pallas_call(
        _swa_prompt_kernel,
        grid=(bsz, nt),
        in_specs=[pl.BlockSpec((qt, 256), lambda b, n: (b * nt + n, 0)),
                  pl.BlockSpec((qt, 256), lambda b, n: (b * nt + n, 1)),
                  pl.BlockSpec((WINDOW, 256), lambda b, n: (jnp.maximum((b * nt + n) * SWA_BLOCKS - 1, 0), 1)),
                  pl.BlockSpec((1, 128), lambda b, n: (0, 0))],
        out_specs=pl.BlockSpec((qt, BRANCH_W), lambda b, n: (b * nt + n, 0)),
        out_shape=jax.ShapeDtypeStruct((bsz * seq, BRANCH_W), F32),
        compiler_params=_cparams(("parallel", "parallel")),
        name="swa_prompt",
    )(swa, swa, swa, lw["sinks"])


def _rotate(x, cos, sin):
    x1, x2 = x[:, 0:128], x[:, 128:256]
    return jnp.concatenate([x1 * cos - x2 * sin, x1 * sin + x2 * cos], axis=1)


def _ret_prompt_kernel(r_ref, cos_ref, sin_ref, dmat_ref, qdec_ref, kdec_ref, cdec_ref, g_ref, y_ref, s_out_ref, s_ref):
    n = pl.program_id(1)

    @pl.when(n == 0)
    def _():
        s_ref[...] = jnp.zeros_like(s_ref)

    c = CHUNK
    chunks = range(STEP_CHUNKS)
    hq = _lane_head(BRANCH_W, 32) % N_HEADS
    hv = _lane_head(BRANCH_W, HD)
    rows_head = (lax.broadcasted_iota(jnp.int32, (BRANCH_W, 1), 0) // 32) % N_HEADS
    qr, kr, v = [], [], []
    for i in chunks:
        rows = slice(i * c, (i + 1) * c)
        cos, sin = cos_ref[rows, :], sin_ref[rows, :]
        qr.append(_rotate(r_ref[rows, 0:256], cos, sin))
        kr.append(_rotate(r_ref[rows, 256:512], cos, sin) * (HD ** -0.5))
        v.append(r_ref[rows, 512:768].astype(BF16))
    att = [_dot_nt(_stack_heads(qr[i], hq).astype(BF16), kr[i].astype(BF16)) * dmat_ref[...] for i in chunks]
    upd = [_dot_tn((kr[i] * kdec_ref[...]).astype(BF16), v[i]) for i in chunks]
    states = [s_ref[...]]
    for i in chunks:
        states.append(cdec_ref[...] * states[i] + jnp.where(rows_head == hv, upd[i], 0.0))
    o = [_pick_heads(_dot(att[i].astype(BF16), v[i]), c, hv) for i in chunks]
    o = [o[i] + _dot((qr[i] * qdec_ref[...]).astype(BF16), states[i].astype(BF16)) for i in chunks]
    s_ref[...] = states[-1]
    s_out_ref[...] = states[-1]
    y_ref[...] = jax.nn.silu(r_ref[:, 768:1024]) * _head_norm(jnp.concatenate(o, axis=0), g_ref[...])


def _ret_tables(seq):
    c = CHUNK
    half = HD // 2
    freq = ROPE_BASE ** (-jnp.arange(half, dtype=F32) / half)
    ang = jnp.arange(seq).astype(F32)[:, None] * freq[None, :]
    cos = jnp.tile(jnp.cos(ang), (1, N_HEADS))
    sin = jnp.tile(jnp.sin(ang), (1, N_HEADS))
    log_g = jnp.log1p(-jnp.exp2(-5.0 - jnp.arange(N_HEADS, dtype=F32)))
    idx = jnp.arange(c, dtype=F32)
    rel = idx[:, None] - idx[None, :]
    dmat = jnp.exp(jnp.where(rel[None] >= 0, rel[None] * log_g[:, None, None], -jnp.inf)).reshape(N_HEADS * c, c)
    lane_head_q = (jnp.arange(BRANCH_W) // 32) % N_HEADS
    qdec = jnp.exp((idx + 1.0)[:, None] * log_g[lane_head_q][None, :])
    kdec = jnp.exp((c - 1.0 - idx)[:, None] * log_g[lane_head_q][None, :])
    cdec = jnp.exp(c * log_g)[jnp.arange(BRANCH_W) // HD][None, :]
    return cos, sin, dmat, qdec, kdec, cdec


def _ret_prompt(ret, lw, bsz, seq):
    c = CHUNK
    rows = STEP_CHUNKS * c
    nc = seq // rows
    cos, sin, dmat, qdec, kdec, cdec = _ret_tables(seq)
    full = lambda shape: pl.BlockSpec(shape, lambda b, n: (0,) * len(shape))
    return pl.pallas_call(
        _ret_prompt_kernel,
        grid=(bsz, nc),
        in_specs=[pl.BlockSpec((rows, W_RET), lambda b, n: (b * nc + n, 0)),
                  pl.BlockSpec((rows, 128), lambda b, n: (n, 0)), pl.BlockSpec((rows, 128), lambda b, n: (n, 0)),
                  full((N_HEADS * c, c)), full((c, BRANCH_W)), full((c, BRANCH_W)), full((1, BRANCH_W)),
                  full((1, BRANCH_W))],
        out_specs=[pl.BlockSpec((rows, BRANCH_W), lambda b, n: (b * nc + n, 0)),
                   pl.BlockSpec((None, BRANCH_W, BRANCH_W), lambda b, n: (b, 0, 0))],
        out_shape=[jax.ShapeDtypeStruct((bsz * seq, BRANCH_W), F32),
                   jax.ShapeDtypeStruct((bsz, BRANCH_W, BRANCH_W), F32)],
        scratch_shapes=[pltpu.VMEM((BRANCH_W, BRANCH_W), F32)],
        compiler_params=_cparams(("parallel", "arbitrary")),
        name="ret_prompt",
    )(ret, cos, sin, dmat, qdec, kdec, cdec, lw["ret_g"])


def _ret_state_from_blockdiag(s_bd):
    b = s_bd.shape[0]
    s = s_bd.reshape(b, 2, N_HEADS, 32, N_HEADS, HD)
    s = jnp.stack([s[:, :, h, :, h, :] for h in range(N_HEADS)], axis=1)
    return s.reshape(b, N_HEADS, HD, HD)


ML_I_LANE = GLA_RANK
ML_F_LANE = GLA_RANK + N_HEADS


def _ml_prompt_kernel(m_ref, sm_ref, g_ref, y_ref, c_out_ref, n_out_ref, m_out_ref, c_ref, n_ref, mm_ref):
    nchunk = pl.program_id(1)

    @pl.when(nchunk == 0)
    def _():
        c_ref[...] = jnp.zeros_like(c_ref)
        n_ref[...] = jnp.zeros_like(n_ref)
        mm_ref[...] = jnp.zeros_like(mm_ref)

    c = CHUNK
    chunks = range(STEP_CHUNKS)
    row = lax.broadcasted_iota(jnp.int32, (c, c), 0)
    col = lax.broadcasted_iota(jnp.int32, (c, c), 1)
    tri_lo = jnp.where(col <= row, 1.0, 0.0).astype(BF16)
    tri_up = jnp.where(row <= col, 1.0, 0.0).astype(BF16)
    hv = _lane_head(BRANCH_W, HD)
    hq = _lane_head(N_HEADS * c, c)
    key = lax.broadcasted_iota(jnp.int32, (c, N_HEADS * c), 0)
    qry = lax.broadcasted_iota(jnp.int32, (c, N_HEADS * c), 1) % c
    causal_t = key <= qry
    r = lax.broadcasted_iota(jnp.int32, (BRANCH_W, BRANCH_W), 0) // HD
    cc = lax.broadcasted_iota(jnp.int32, (BRANCH_W, BRANCH_W), 1) // HD
    same_head = r == cc
    lane = lax.broadcasted_iota(jnp.int32, (1, 128), 1)
    gate_lane = lax.broadcasted_iota(jnp.int32, (128, 1), 0)
    is_i = (gate_lane >= ML_I_LANE) & (gate_lane < ML_I_LANE + N_HEADS)
    is_f = (gate_lane >= ML_F_LANE) & (gate_lane < ML_F_LANE + N_HEADS)
    gate_head = jnp.where(is_i, gate_lane - ML_I_LANE, gate_lane - ML_F_LANE)
    sel_q = jnp.where((is_i | is_f) & (gate_head == hq), 1.0, 0.0).astype(BF16)
    sel_v = jnp.where((is_i | is_f) & (gate_head == hv), 1.0, 0.0).astype(BF16)
    sel_qv = jnp.concatenate([sel_q, sel_v], axis=1)
    lane128 = lax.broadcasted_iota(jnp.int32, (1, 128), 1)
    in_i = (lane128 >= ML_I_LANE) & (lane128 < ML_I_LANE + N_HEADS)
    in_f = (lane128 >= ML_F_LANE) & (lane128 < ML_F_LANE + N_HEADS)
    head_rows = lax.broadcasted_iota(jnp.int32, (8, 1), 0)

    q, ks, vt, brow, u_q, u_v, sc = [], [], [], [], [], [], []
    for i in chunks:
        rows = slice(i * c, (i + 1) * c)
        q.append(m_ref[rows, 0:256])
        ks.append(m_ref[rows, 256:512] * (HD ** -0.5))
        vt.append(m_ref[rows, 512:768].T.astype(BF16))
        sm = sm_ref[rows, :]
        bcols = _dot_exact_lhs(tri_lo, jax.nn.log_sigmoid(sm))
        brows = _dot_exact_rhs(jax.nn.log_sigmoid(sm.T), tri_up)
        brow.append(jnp.concatenate([brows[ML_F_LANE + h:ML_F_LANE + h + 1, :] for h in range(N_HEADS)], axis=1))
        z = jnp.where(in_i, sm, 0.0) - jnp.where(in_f, bcols, 0.0)
        u = _dot_exact_rhs(z, sel_qv)
        u_q.append(u[:, 0:N_HEADS * c])
        u_v.append(u[:, N_HEADS * c:])
        sc.append(_dot_nt(ks[i].astype(BF16), _stack_heads(q[i], hv).astype(BF16)))

    mm = mm_ref[...]
    qk, w_inter, rowsum, floor, khat, carry_b = [], [], [], [], [], []
    for i in chunks:
        m_prev = _spread_heads([mm[:, h:h + 1] for h in range(N_HEADS)], hq)
        d_log = jnp.where(causal_t, brow[i] + u_q[i], -jnp.inf)
        inter_log = brow[i] + m_prev
        m_t = jnp.maximum(inter_log, jnp.max(d_log, axis=0, keepdims=True))
        qk_i = sc[i] * jnp.exp(d_log - m_t)
        qk.append(qk_i.astype(BF16))
        rowsum.append(jnp.sum(qk_i, axis=0, keepdims=True))
        w_inter.append(jnp.exp(inter_log - m_t))
        floor.append(jnp.exp(-m_t))
        m_new = [m_t[:, (h + 1) * c - 1:(h + 1) * c] for h in range(N_HEADS)]
        b_last = [brow[i][:, (h + 1) * c - 1:(h + 1) * c] for h in range(N_HEADS)]
        carry_b.append(_spread_heads([jnp.exp(b_last[h] + mm[:, h:h + 1] - m_new[h]) for h in range(N_HEADS)], hv))
        khat.append(ks[i] * jnp.exp(u_v[i] + _spread_heads([b_last[h] - m_new[h] for h in range(N_HEADS)], hv)))
        mm = jnp.zeros((1, 128), F32)
        for h in range(N_HEADS):
            mm = jnp.where(lane == h, m_new[h], mm)

    intra = [_dot(vt[i], qk[i]) for i in chunks]
    upd = [_dot(vt[i], khat[i].astype(BF16)) for i in chunks]
    cms, nrows = [c_ref[...]], [n_ref[...]]
    for i in chunks:
        cms.append(carry_b[i] * cms[i] + jnp.where(same_head, upd[i], 0.0))
        nrows.append(carry_b[i] * nrows[i] + jnp.sum(khat[i], axis=0, keepdims=True))
    hout = []
    for i in chunks:
        qb = q[i].astype(BF16)
        inter = _dot_nt(cms[i].astype(BF16), qb)
        n_sel = jnp.where(head_rows == hv, nrows[i], 0.0)
        qn = sum(_dot_nt(p, qb) for p in _split3(n_sel))
        blocks = []
        for h in range(N_HEADS):
            cols = slice(h * c, (h + 1) * c)
            den = w_inter[i][:, cols] * qn[h:h + 1, :] + rowsum[i][:, cols]
            num = w_inter[i][:, cols] * inter[h * HD:(h + 1) * HD, :] + intra[i][h * HD:(h + 1) * HD, cols]
            blocks.append(num / jnp.maximum(jnp.abs(den), floor[i][:, cols]))
        hout.append(jnp.concatenate(blocks, axis=0).T)
    y_ref[...] = _head_norm(jax.nn.sigmoid(m_ref[:, 768:1024]) * jnp.concatenate(hout, axis=0), g_ref[...])

    c_ref[...] = cms[-1]
    n_ref[...] = nrows[-1]
    mm_ref[...] = mm
    c_out_ref[...] = cms[-1]
    n_out_ref[...] = nrows[-1]
    m_out_ref[...] = mm


def _ml_prompt(ml, small, lw, bsz, seq):
    rows = STEP_CHUNKS * CHUNK
    nc = seq // rows
    return pl.pallas_call(
        _ml_prompt_kernel,
        grid=(bsz, nc),
        in_specs=[pl.BlockSpec((rows, W_ML), lambda b, n: (b * nc + n, 0)),
                  pl.BlockSpec((rows, W_SMALL), lambda b, n: (b * nc + n, 0)),
                  pl.BlockSpec((1, BRANCH_W), lambda b, n: (0, 0))],
        out_specs=[pl.BlockSpec((rows, BRANCH_W), lambda b, n: (b * nc + n, 0)),
                   pl.BlockSpec((None, BRANCH_W, BRANCH_W), lambda b, n: (b, 0, 0)),
                   pl.BlockSpec((None, 1, BRANCH_W), lambda b, n: (b, 0, 0)),
                   pl.BlockSpec((None, 1, 128), lambda b, n: (b, 0, 0))],
        out_shape=[jax.ShapeDtypeStruct((bsz * seq, BRANCH_W), F32),
                   jax.ShapeDtypeStruct((bsz, BRANCH_W, BRANCH_W), F32),
                   jax.ShapeDtypeStruct((bsz, 1, BRANCH_W), F32),
                   jax.ShapeDtypeStruct((bsz, 1, 128), F32)],
        scratch_shapes=[pltpu.VMEM((BRANCH_W, BRANCH_W), F32), pltpu.VMEM((1, BRANCH_W), F32),
                        pltpu.VMEM((1, 128), F32)],
        compiler_params=_cparams(("parallel", "arbitrary")),
        name="ml_prompt",
    )(ml, small, lw["ml_g"])


def _diag_blocks(s_bd):
    b = s_bd.shape[0]
    s = s_bd.reshape(b, N_HEADS, HD, N_HEADS, HD)
    return jnp.stack([s[:, h, :, h, :] for h in range(N_HEADS)], axis=1)


GLA_TILE = 256
GLA_STEP_TILES = 4
GLA_SUB = 16


def _gla_prompt_kernel(x_ref, sm_ref, wa2_ref, ba2_ref, g_ref, y_ref, s_out_ref, s_ref):
    n = pl.program_id(1)

    @pl.when(n == 0)
    def _():
        s_ref[...] = jnp.zeros_like(s_ref)

    tq, sc = GLA_TILE, GLA_SUB
    r = lax.broadcasted_iota(jnp.int32, (tq, tq), 0)
    c = lax.broadcasted_iota(jnp.int32, (tq, tq), 1)
    same = (r // sc) == (c // sc)
    cum_in_sub = jnp.where(same & (c <= r), 1.0, 0.0).astype(BF16)
    sum_in_sub = jnp.where(same, 1.0, 0.0).astype(BF16)
    kh = lax.broadcasted_iota(jnp.int32, (128, 1), 0) // GLA_DK
    hv = _lane_head(BRANCH_W, HD)
    expand = jnp.where(kh == hv, 1.0, 0.0).astype(BF16)
    trow = lax.broadcasted_iota(jnp.int32, (sc, 1), 0)
    n_sub = tq // sc
    chunks = range(GLA_STEP_TILES * n_sub)

    q, k, v, b, qd, dec_t, upds = [], [], [], [], [], [], []
    for t in range(GLA_STEP_TILES):
        rows = slice(t * tq, (t + 1) * tq)
        q.append(x_ref[rows, 0:128] * (GLA_DK ** -0.5))
        k.append(x_ref[rows, 128:256])
        v.append(x_ref[rows, 256:512])
        log_a = jax.nn.log_sigmoid(_dot(sm_ref[rows, :].astype(BF16), wa2_ref[...]) + ba2_ref[...]) / GLA_TAU
        b.append(_dot_exact_lhs(cum_in_sub, log_a))
        btot = _dot_exact_lhs(sum_in_sub, log_a)
        qd.append((q[t] * jnp.exp(b[t])).astype(BF16))
        dec_t.append(jnp.exp(btot).T)
        khat_b = (k[t] * jnp.exp(btot - b[t])).astype(BF16)
        vb = v[t].astype(BF16)
        upds += [_dot_tn(khat_b[ci * sc:(ci + 1) * sc], vb[ci * sc:(ci + 1) * sc]) for ci in range(n_sub)]
    states = [s_ref[...]]
    for g in chunks:
        t, lo = g // n_sub, (g % n_sub) * sc
        states.append(dec_t[t][:, lo:lo + 1] * states[g] + jnp.where(kh == hv, upds[g], 0.0))
    outs = []
    for g in chunks:
        t, lo = g // n_sub, (g % n_sub) * sc
        qc, kc, bc, vc = q[t][lo:lo + sc], k[t][lo:lo + sc], b[t][lo:lo + sc], v[t][lo:lo + sc]
        pair = [qc * kc[j:j + 1] * jnp.exp(jnp.where(trow >= j, bc - bc[j:j + 1], -jnp.inf)) for j in range(sc)]
        att = _dot(jnp.concatenate(pair, axis=0).astype(BF16), expand)
        o = att[0:sc] * vc[0:1]
        for j in range(1, sc):
            o = o + att[j * sc:(j + 1) * sc] * vc[j:j + 1]
        outs.append(o)
    outs = [outs[g] + _dot(qd[g // n_sub][(g % n_sub) * sc:(g % n_sub + 1) * sc], states[g].astype(BF16)) for g in chunks]
    s_ref[...] = states[-1]
    s_out_ref[...] = states[-1]
    y_ref[...] = jax.nn.silu(x_ref[:, 512:768]) * _head_norm(jnp.concatenate(outs, axis=0), g_ref[...])


def _gla_prompt(gla, small, lw, bsz, seq):
    tq = GLA_STEP_TILES * GLA_TILE
    nt = seq // tq
    full = lambda shape: pl.BlockSpec(shape, lambda b, n: (0,) * len(shape))
    return pl.pallas_call(
        _gla_prompt_kernel,
        grid=(bsz, nt),
        in_specs=[pl.BlockSpec((tq, W_GLA), lambda b, n: (b * nt + n, 0)),
                  pl.BlockSpec((tq, W_SMALL), lambda b, n: (b * nt + n, 0)),
                  full((128, 128)), full((1, 128)), full((1, BRANCH_W))],
        out_specs=[pl.BlockSpec((tq, BRANCH_W), lambda b, n: (b * nt + n, 0)),
                   pl.BlockSpec((None, 128, BRANCH_W), lambda b, n: (b, 0, 0))],
        out_shape=[jax.ShapeDtypeStruct((bsz * seq, BRANCH_W), F32),
                   jax.ShapeDtypeStruct((bsz, 128, BRANCH_W), F32)],
        scratch_shapes=[pltpu.VMEM((128, BRANCH_W), F32)],
        compiler_params=_cparams(("parallel", "arbitrary")),
        name="gla_prompt",
    )(gla, small, lw["wa2"], lw["ba2"], lw["gla_g"])


def _gla_state_from_blockdiag(s_bd):
    b = s_bd.shape[0]
    s = s_bd.reshape(b, N_HEADS, GLA_DK, N_HEADS, HD)
    return jnp.stack([s[:, h, :, h, :] for h in range(N_HEADS)], axis=1)


LANES = 128


def _rank1_update(st_ref, snt_ref, q_ref, k_ref, decay_row, v, dk):
    def body(i, o):
        rows = pl.ds(pl.multiple_of(i * HD, HD), HD)
        new = decay_row(i) * st_ref[rows, :] + k_ref[pl.ds(i, 1), :] * v
        snt_ref[rows, :] = new
        return o + q_ref[pl.ds(i, 1), :] * new
    return lax.fori_loop(0, dk, body, jnp.zeros((HD, LANES), F32), unroll=4)


def _head_rows(h, n):
    return pl.ds(pl.multiple_of(h * n, n), n)


def _gla_sample_kernel(x_ref, sm_ref, wa2_ref, ba2_ref, g_ref, s_ref, y_ref, s_out_ref,
                       qt_ref, kt_ref, at_ref, vt_ref, ot_ref):
    h = pl.program_id(0)

    @pl.when(h == 0)
    def _():
        x = x_ref[...]
        qt_ref[...] = (x[:, 0:128] * (GLA_DK ** -0.5)).T
        kt_ref[...] = x[:, 128:256].T
        vt_ref[...] = x[:, 256:512].T
        log_a = jax.nn.log_sigmoid(_dot(sm_ref[...].astype(BF16), wa2_ref[...]) + ba2_ref[...]) / GLA_TAU
        at_ref[...] = jnp.exp(log_a).T

    base = h * GLA_DK
    o = _rank1_update(s_ref, s_out_ref, qt_ref.at[_head_rows(h, GLA_DK)], kt_ref.at[_head_rows(h, GLA_DK)],
                      lambda i: at_ref[pl.ds(base + i, 1), :], vt_ref[_head_rows(h, HD), :], GLA_DK)
    ot_ref[_head_rows(h, HD), :] = o

    @pl.when(h == N_HEADS - 1)
    def _():
        y_ref[...] = jax.nn.silu(x_ref[:, 512:768]) * _head_norm(ot_ref[...].T, g_ref[...])


def _gla_sample(gla, small, state, lw):
    per_head = GLA_DK * HD
    li = lw["layer"]
    full = lambda shape: pl.BlockSpec(shape, lambda h: (0,) * len(shape))
    vm = lambda r: pltpu.VMEM((r, LANES), F32)
    return pl.pallas_call(
        _gla_sample_kernel,
        grid=(N_HEADS,),
        in_specs=[full((LANES, W_GLA)), full((LANES, W_SMALL)), full((128, 128)), full((1, 128)), full((1, BRANCH_W)),
                  pl.BlockSpec((None, per_head, LANES), lambda h: (li, h, 0))],
        out_specs=[full((LANES, BRANCH_W)), pl.BlockSpec((per_head, LANES), lambda h: (h, 0))],
        out_shape=[jax.ShapeDtypeStruct((LANES, BRANCH_W), F32), jax.ShapeDtypeStruct(state.shape[1:], F32)],
        scratch_shapes=[vm(128), vm(128), vm(128), vm(256), vm(256)],
        compiler_params=_cparams(("arbitrary",)),
        name="gla_sample",
    )(gla, small, lw["wa2"], lw["ba2"], lw["gla_g"], state)


def _ml_sample_kernel(x_ref, sc_ref, n_ref, g_ref, c_ref, y_ref, c_out_ref, n_out_ref, m_out_ref,
                      qt_ref, kt_ref, vt_ref, ot_ref, w_ref, kp_ref):
    h = pl.program_id(0)

    @pl.when(h == 0)
    def _():
        x = x_ref[...]
        qt_ref[...] = x[:, 0:256].T
        kt_ref[...] = (x[:, 256:512] * (HD ** -0.5)).T
        vt_ref[...] = x[:, 512:768].T
        sct = sc_ref[...].T
        log_f = jax.nn.log_sigmoid(sct[8:16])
        m_old = sct[16:24]
        m_new = jnp.maximum(log_f + m_old, sct[0:8])
        w_ref[0:8, :] = jnp.exp(log_f + m_old - m_new)
        w_ref[8:16, :] = jnp.exp(sct[0:8] - m_new)
        w_ref[16:24, :] = m_new

    w_carry = w_ref[pl.ds(h, 1), :]
    w_tok = w_ref[pl.ds(8 + h, 1), :]
    m_new = w_ref[pl.ds(16 + h, 1), :]
    kp_ref[...] = kt_ref[_head_rows(h, HD), :] * w_tok
    q_h = qt_ref.at[_head_rows(h, HD)]
    n_new = w_carry * n_ref[_head_rows(h, HD), :] + kp_ref[...]
    num = _rank1_update(c_ref, c_out_ref, q_h, kp_ref, lambda i: w_carry, vt_ref[_head_rows(h, HD), :], HD)
    den = jnp.sum(q_h[...] * n_new, axis=0, keepdims=True)
    ot_ref[_head_rows(h, HD), :] = num / jnp.maximum(jnp.abs(den), jnp.exp(-m_new))
    n_out_ref[_head_rows(h, HD), :] = n_new

    @pl.when(h == N_HEADS - 1)
    def _():
        y_ref[...] = _head_norm(jax.nn.sigmoid(x_ref[:, 768:1024]) * ot_ref[...].T, g_ref[...])
        m_out_ref[...] = jnp.concatenate([w_ref[16:24, :], jnp.zeros((LANES - 8, LANES), F32)], axis=0).T


def _ml_sample(ml, scal, n_state, c_state, lw):
    per_head = HD * HD
    li = lw["layer"]
    full = lambda shape: pl.BlockSpec(shape, lambda h: (0,) * len(shape))
    vm = lambda r: pltpu.VMEM((r, LANES), F32)
    return pl.pallas_call(
        _ml_sample_kernel,
        grid=(N_HEADS,),
        in_specs=[full((LANES, W_ML)), full((LANES, 128)),
                  pl.BlockSpec((None, BRANCH_W, LANES), lambda h: (li, 0, 0)), full((1, BRANCH_W)),
                  pl.BlockSpec((None, per_head, LANES), lambda h: (li, h, 0))],
        out_specs=[full((LANES, BRANCH_W)), pl.BlockSpec((per_head, LANES), lambda h: (h, 0)),
                   full((BRANCH_W, LANES)), full((LANES, 128))],
        out_shape=[jax.ShapeDtypeStruct((LANES, BRANCH_W), F32), jax.ShapeDtypeStruct(c_state.shape[1:], F32),
                   jax.ShapeDtypeStruct((BRANCH_W, LANES), F32), jax.ShapeDtypeStruct((LANES, 128), F32)],
        scratch_shapes=[vm(256), vm(256), vm(256), vm(256), vm(24), vm(HD)],
        compiler_params=_cparams(("arbitrary",)),
        name="ml_sample",
    )(ml, scal, n_state, lw["ml_g"], c_state)


def _ret_sample_kernel(x_ref, cos_ref, sin_ref, gam_ref, g_ref, s_ref, y_ref, s_out_ref,
                       qt_ref, kt_ref, vt_ref, ot_ref, qr_ref, kr_ref):
    h = pl.program_id(0)

    @pl.when(h == 0)
    def _():
        x = x_ref[...]
        qt_ref[...] = x[:, 0:256].T
        kt_ref[...] = (x[:, 256:512] * (HD ** -0.5)).T
        vt_ref[...] = x[:, 512:768].T

    cos, sin = cos_ref[...], sin_ref[...]
    for src, dst in ((qt_ref, qr_ref), (kt_ref, kr_ref)):
        x1 = src[_head_rows(h, 32), :]
        x2 = src[pl.ds(pl.multiple_of(128 + h * 32, 32), 32), :]
        dst[0:32, :] = x1 * cos - x2 * sin
        dst[32:64, :] = x1 * sin + x2 * cos
    gamma = gam_ref[pl.ds(h, 1), :]
    ot_ref[_head_rows(h, HD), :] = _rank1_update(s_ref, s_out_ref, qr_ref, kr_ref, lambda i: gamma,
                                                 vt_ref[_head_rows(h, HD), :], HD)

    @pl.when(h == N_HEADS - 1)
    def _():
        y_ref[...] = jax.nn.silu(x_ref[:, 768:1024]) * _head_norm(ot_ref[...].T, g_ref[...])


def _ret_sample(ret, state, lw):
    per_head = HD * HD
    li = lw["layer"]
    half = HD // 2
    freq = ROPE_BASE ** (-jnp.arange(half, dtype=F32) / half)
    ang = jnp.full((1,), PAST_LEN, F32)[:, None] * freq[None, :]
    cos = jnp.broadcast_to(jnp.cos(ang).reshape(half, 1), (half, LANES))
    sin = jnp.broadcast_to(jnp.sin(ang).reshape(half, 1), (half, LANES))
    log_g = jnp.log1p(-jnp.exp2(-5.0 - jnp.arange(N_HEADS, dtype=F32)))
    gam = jnp.broadcast_to(jnp.concatenate([jnp.exp(log_g), jnp.ones((4,), F32)]).reshape(8, 1), (8, LANES))
    full = lambda shape: pl.BlockSpec(shape, lambda h: (0,) * len(shape))
    vm = lambda r: pltpu.VMEM((r, LANES), F32)
    return pl.pallas_call(
        _ret_sample_kernel,
        grid=(N_HEADS,),
        in_specs=[full((LANES, W_RET)), full((half, LANES)), full((half, LANES)), full((8, LANES)), full((1, BRANCH_W)),
                  pl.BlockSpec((None, per_head, LANES), lambda h: (li, h, 0))],
        out_specs=[full((LANES, BRANCH_W)), pl.BlockSpec((per_head, LANES), lambda h: (h, 0))],
        out_shape=[jax.ShapeDtypeStruct((LANES, BRANCH_W), F32), jax.ShapeDtypeStruct(state.shape[1:], F32)],
        scratch_shapes=[vm(256), vm(256), vm(256), vm(256), vm(HD), vm(HD)],
        compiler_params=_cparams(("arbitrary",)),
        name="ret_sample",
    )(ret, cos, sin, gam, lw["ret_g"], state)


def _swa_sample_kernel(q_ref, kn_ref, vn_ref, knc_ref, vnc_ref, sink_ref, kc_ref, vc_ref, o_ref, ko_ref, vo_ref):
    q = q_ref[...]
    kc, vc = kc_ref[...], vc_ref[...]
    kn, vn = kn_ref[...], vn_ref[...]
    scale = HD ** -0.5
    s = _einsum("bhd,bdw->bhw", q.astype(BF16), kc.astype(BF16)) * scale
    s_new = jnp.sum(q * kn, axis=-1, keepdims=True) * scale
    w = lax.broadcasted_iota(jnp.int32, (1, 1, WINDOW), 2)
    s = jnp.where(w > 0, s, -jnp.inf)
    sink = sink_ref[...][:, 0:1]
    mx = jnp.maximum(jnp.maximum(jnp.max(s, axis=-1, keepdims=True), s_new), sink)
    p = jnp.exp(s - mx)
    p_new = jnp.exp(s_new - mx)
    den = jnp.sum(p, axis=-1, keepdims=True) + p_new + jnp.exp(sink - mx)
    o = _einsum("bhw,bdw->bhd", p.astype(BF16), vc.astype(BF16)) + p_new * vn
    o_ref[...] = o / den
    newest = w == WINDOW - 1
    ko_ref[...] = jnp.where(newest, knc_ref[...], pltpu.roll(kc, WINDOW - 1, 2))
    vo_ref[...] = jnp.where(newest, vnc_ref[...], pltpu.roll(vc, WINDOW - 1, 2))


def _swa_sample(swa, cache_k, cache_v, lw, bt=16):
    bsz = swa.shape[0]
    li = lw["layer"]
    cache = pl.BlockSpec((None, bt, WINDOW, 128), lambda i: (li, i, 0, 0))
    lo = (jnp.arange(128) < HD)[None, :]
    qa, qb = swa[:, 0:128], swa[:, 128:256]
    q = jnp.stack([jnp.where(lo, qa, 0.0), jnp.where(lo, qb, 0.0), jnp.where(lo, 0.0, qa), jnp.where(lo, 0.0, qb)], axis=1)
    kn = swa[:, None, 256:384]
    vn = swa[:, None, 384:512]
    sink = jnp.broadcast_to(lw["sinks"][0, 0:N_HEADS].reshape(N_HEADS, 1), (N_HEADS, 128))
    blk = lambda shape: pl.BlockSpec(shape, lambda i: (i,) + (0,) * (len(shape) - 1))
    o, ko, vo = pl.pallas_call(
        _swa_sample_kernel,
        grid=(bsz // bt,),
        in_specs=[blk((bt, N_HEADS, 128)), blk((bt, 1, 128)), blk((bt, 1, 128)), blk((bt, 128, 1)), blk((bt, 128, 1)),
                  pl.BlockSpec((N_HEADS, 128), lambda i: (0, 0)), cache, cache],
        out_specs=[blk((bt, N_HEADS, 128)), blk((bt, WINDOW, 128)), blk((bt, WINDOW, 128))],
        out_shape=[jax.ShapeDtypeStruct((bsz, N_HEADS, 128), F32), jax.ShapeDtypeStruct(cache_k.shape[1:], F32),
                   jax.ShapeDtypeStruct(cache_v.shape[1:], F32)],
        compiler_params=_cparams(("parallel",)),
        name="swa_sample",
    )(q, kn, vn, jnp.swapaxes(kn, 1, 2), jnp.swapaxes(vn, 1, 2), sink, cache_k, cache_v)
    y = jnp.concatenate([o[:, 0, 0:64], o[:, 2, 64:128], o[:, 1, 0:64], o[:, 3, 64:128]], axis=1)
    return y, ko, vo


def _ml_scalars(small, m_state):
    z4 = jnp.zeros((small.shape[0], 4), F32)
    return jnp.concatenate([small[:, ML_I_LANE:ML_I_LANE + 4], z4, small[:, ML_F_LANE:ML_F_LANE + 4], z4, m_state, z4,
                            jnp.zeros((small.shape[0], 128 - 24), F32)], axis=1)


IN_NAMES = ("gq", "gk", "gv", "gr", "ga", "sq", "sk", "sv", "mq", "mk", "mv", "mi", "mf", "mo", "rq", "rk", "rv", "rg",
            "gates")
IN_SIZES = (N_HEADS * GLA_DK, N_HEADS * GLA_DK, BRANCH_W, BRANCH_W, GLA_RANK, BRANCH_W, SWA_KV * HD, SWA_KV * HD,
            BRANCH_W, BRANCH_W, BRANCH_W, N_HEADS, N_HEADS, BRANCH_W, BRANCH_W, BRANCH_W, BRANCH_W, BRANCH_W,
            N_BRANCH * D_MODEL)
IN_START = {name: sum(IN_SIZES[:i]) for i, name in enumerate(IN_NAMES)}
GATE_COL = IN_START["gates"]


def _mixer_rows(w_t):
    x = w_t.shape[1]
    at = lambda name, n: w_t[IN_START[name]:IN_START[name] + n]
    swa_q = at("sq", BRANCH_W).reshape(2, 2, HD, x).transpose(1, 0, 2, 3).reshape(BRANCH_W, x)
    halves = lambda name: at(name, BRANCH_W).reshape(N_HEADS, 2, HD // 2, x).transpose(1, 0, 2, 3).reshape(BRANCH_W, x)
    parts = [at("gq", W_GLA), swa_q, at("sk", 2 * SWA_KV * HD), at("mq", 3 * BRANCH_W), at("mo", BRANCH_W),
             halves("rq"), halves("rk"), at("rv", 2 * BRANCH_W), at("ga", GLA_RANK), at("mi", 2 * N_HEADS)]
    pad = W_MIX - sum(p.shape[0] for p in parts)
    return jnp.concatenate(parts + [jnp.zeros((pad, x), w_t.dtype)], axis=0)


def _prep_layer(li, w_in, b_in, gla_w_a2, gla_b_a2, gla_norm_g, swa_sinks, ml_norm_g, ret_norm_g, w_branch, w_out,
                ln1_g, ln1_b, w_router_grp, b_router_grp, w_router_exp, b_router_exp, w_exp_gate, w_exp_up,
                w_exp_down, ln2_g, ln2_b):
    row = lambda v: v.reshape(1, -1).astype(F32)
    wr = jnp.concatenate([w_router_grp[li], w_router_exp[li],
                          jnp.zeros((D_MODEL, 128 - N_GROUPS - N_EXPERTS), F32)], axis=1)
    wr_hi = wr.astype(BF16)
    wb = w_branch[li]
    wb1 = jnp.concatenate([wb[1, 0:64], wb[1, 128:192], wb[1, 64:128], wb[1, 192:256]], axis=0)
    w_t = w_in[li].T
    return dict(
        wm=_mixer_rows(w_t).astype(BF16),
        bm=row(_mixer_rows(b_in[li][:, None])),
        wg=w_t[GATE_COL:].astype(BF16),
        bg=row(b_in[li][GATE_COL:]),
        wa2=jnp.concatenate([gla_w_a2[li], jnp.zeros((128 - GLA_RANK, 128), F32)], axis=0).astype(BF16),
        ba2=row(gla_b_a2[li]),
        gla_g=row(gla_norm_g[li]), ml_g=row(ml_norm_g[li]), ret_g=row(ret_norm_g[li]),
        sinks=row(jnp.concatenate([swa_sinks[li], jnp.zeros((128 - N_HEADS,), F32)])),
        wbr=jnp.stack([wb[0], wb1, wb[2], wb[3]]).astype(BF16),
        wout=w_out[li].astype(BF16),
        ln1_g=row(ln1_g[li]), ln1_b=row(ln1_b[li]), ln2_g=row(ln2_g[li]), ln2_b=row(ln2_b[li]),
        wr_hi=wr_hi, wr_lo=(wr - wr_hi.astype(F32)).astype(BF16),
        br=row(jnp.concatenate([b_router_grp[li], b_router_exp[li], jnp.zeros((128 - N_GROUPS - N_EXPERTS,), F32)])),
        layer=li, we_gate=w_exp_gate, we_up=w_exp_up, we_down=w_exp_down,
    )


TOK_TILE = 256
MOE_PIECES_PROMPT = 64
MOE_PIECES_SAMPLE = 1


MERGE_TILES = 2


def _ffn_half(x2, ys, lw, tm, bp):
    x1, route, cnt, xs = _merge(x2, ys, lw, tm, MERGE_TILES if x2.shape[0] // tm % MERGE_TILES == 0 else 1)
    return _moe(x1, route, cnt, xs, lw, bp, tm)


def _layer_prompt(x2, lw, bsz, seq):
    gla, swa, ml, ret, small = _inproj(x2, lw["wm"], lw["bm"], 2 * TOK_TILE)
    ya, s_gla = _gla_prompt(gla, small, lw, bsz, seq)
    yb = _swa_prompt(swa, lw, bsz, seq)
    yc, c_bd, n_row, m_row = _ml_prompt(ml, small, lw, bsz, seq)
    yd, s_ret = _ret_prompt(ret, lw, bsz, seq)
    kv_tail = swa.reshape(bsz, seq, W_SWA)[:, seq - WINDOW:, 256:512]
    states = (_gla_state_from_blockdiag(s_gla),
              kv_tail[:, :, 0:128].reshape(bsz, WINDOW, SWA_KV, HD),
              kv_tail[:, :, 128:256].reshape(bsz, WINDOW, SWA_KV, HD),
              jnp.swapaxes(_diag_blocks(c_bd), -1, -2), n_row.reshape(bsz, N_HEADS, HD), m_row[:, 0, 0:N_HEADS],
              _ret_state_from_blockdiag(s_ret))
    return _ffn_half(x2, (ya, yb, yc, yd), lw, TOK_TILE, MOE_PIECES_PROMPT), states


def _layer_sample(x2, st, lw):
    s_gla, k_buf, v_buf, c_ml, n_ml, m_ml, s_ret = st
    bsz = x2.shape[0]
    depth = s_gla.shape[0]
    gla, swa, ml, ret, small = _inproj(x2, lw["wm"], lw["bm"], bsz)
    to_lanes = lambda s: jnp.moveaxis(s, 1, -1).reshape(depth, -1, bsz)
    from_lanes = lambda s, like: jnp.moveaxis(s.reshape(like.shape[2:] + (bsz,)), -1, 0)
    cache_t = lambda c: jnp.swapaxes(c.reshape(depth, bsz, WINDOW, SWA_KV * HD), 2, 3)
    ya, s_gla_new = _gla_sample(gla, small, to_lanes(s_gla), lw)
    yb, k_new, v_new = _swa_sample(swa, cache_t(k_buf), cache_t(v_buf), lw)
    k_new, v_new = jnp.swapaxes(k_new, 1, 2), jnp.swapaxes(v_new, 1, 2)
    yc, c_new, n_new, m_new = _ml_sample(ml, _ml_scalars(small, m_ml[lw["layer"]]), to_lanes(n_ml), to_lanes(c_ml), lw)
    yd, s_ret_new = _ret_sample(ret, to_lanes(s_ret), lw)
    states = (from_lanes(s_gla_new, s_gla), k_new.reshape(k_buf.shape[1:]), v_new.reshape(v_buf.shape[1:]),
              from_lanes(c_new, c_ml), from_lanes(n_new, n_ml), m_new[:, 0:N_HEADS], from_lanes(s_ret_new, s_ret))
    return _ffn_half(x2, (ya, yb, yc, yd), lw, bsz, MOE_PIECES_SAMPLE), states


def kernel(x_prompt, x_sample, state_gla, cache_swa_k, cache_swa_v, state_mlstm_c, state_mlstm_n, state_mlstm_m,
           state_ret, w_in, b_in, gla_w_a2, gla_b_a2, gla_norm_g, swa_sinks, ml_norm_g, ret_norm_g, w_branch, w_out,
           ln1_g, ln1_b, w_router_grp, b_router_grp, w_router_exp, b_router_exp, w_exp_gate, w_exp_up, w_exp_down,
           ln2_g, ln2_b):
    bsz, seq, _ = x_prompt.shape
    dec_b, dec_seq, _ = x_sample.shape
    assert dec_seq == 1 and dec_b == LANES and (bsz * seq) % (2 * TOK_TILE) == 0
    assert all(seq % n == 0 for n in (GLA_STEP_TILES * GLA_TILE, SWA_BLOCKS * WINDOW, STEP_CHUNKS * CHUNK))
    weights = (w_in, b_in, gla_w_a2, gla_b_a2, gla_norm_g, swa_sinks, ml_norm_g, ret_norm_g, w_branch, w_out,
               ln1_g, ln1_b, w_router_grp, b_router_grp, w_router_exp, b_router_exp, w_exp_gate, w_exp_up,
               w_exp_down, ln2_g, ln2_b)
    y_p = x_prompt.reshape(bsz * seq, D_MODEL)
    y_s = x_sample.reshape(dec_b, D_MODEL)
    new_p, new_s = [], []
    for li in range(w_in.shape[0]):
        lw = _prep_layer(li, *weights)
        y_p, st_p = _layer_prompt(y_p, lw, bsz, seq)
        st_in = (state_gla, cache_swa_k, cache_swa_v, state_mlstm_c, state_mlstm_n, state_mlstm_m, state_ret)
        y_s, st_s = _layer_sample(y_s, st_in, lw)
        new_p.append(st_p)
        new_s.append(st_s)
    outs = [y_p.reshape(bsz, seq, D_MODEL), y_s.reshape(dec_b, dec_seq, D_MODEL)]
    for i in range(7):
        outs.append(jnp.stack([s[i] for s in new_p], axis=0))
        outs.append(jnp.stack([s[i] for s in new_s], axis=0))
    return tuple(outs)
```

```python
import functools

import jax
import jax.numpy as jnp
from jax import lax
from jax.experimental import pallas as pl
from jax.experimental.pallas import tpu as pltpu

F32 = jnp.float32
BF16 = jnp.bfloat16

D_MODEL = 1024
N_BRANCH = 4
BRANCH_W = D_MODEL // N_BRANCH
N_HEADS = 4
GLA_DK = 32
GLA_RANK = 16
GLA_TAU = 16.0
HD = 64
SWA_KV = 2
WINDOW = 128
ROPE_BASE = 10000.0
PAST_LEN = 16384
N_GROUPS = 4
EPG = 8
N_EXPERTS = N_GROUPS * EPG
D_EXPERT = D_MODEL // 4
DEPTH = 2
ALPHA = (2 * DEPTH) ** 0.25
LN_EPS = 1e-5
VMEM_LIMIT = 56 * 1024 * 1024

W_GLA, W_SWA, W_ML, W_RET, W_SMALL = 768, 512, 1024, 1024, 128
W_MIX = W_GLA + W_SWA + W_ML + W_RET + W_SMALL


def _cparams(sem):
    return pltpu.CompilerParams(dimension_semantics=sem, vmem_limit_bytes=VMEM_LIMIT)


def _dot(a, b):
    return jnp.dot(a, b, preferred_element_type=F32)


def _einsum(spec, a, b):
    return jnp.einsum(spec, a, b, preferred_element_type=F32)


def _split2(x):
    hi = x.astype(BF16)
    lo = (x - hi.astype(F32)).astype(BF16)
    return hi, lo


def _split3(x):
    hi = x.astype(BF16)
    r = x - hi.astype(F32)
    mid = r.astype(BF16)
    lo = (r - mid.astype(F32)).astype(BF16)
    return hi, mid, lo


def _dot_exact_lhs(a_exact, b):
    return sum(_dot(a_exact, p) for p in _split3(b))


def _dot_exact_rhs(a, b_exact):
    return sum(_dot(p, b_exact) for p in _split3(a))


def _layer_norm(x, g, b):
    mu = jnp.mean(x, axis=-1, keepdims=True)
    xc = x - mu
    var = jnp.mean(xc * xc, axis=-1, keepdims=True)
    return xc * lax.rsqrt(var + LN_EPS) * g + b


def _inproj_kernel(x_ref, w_ref, b_ref, gla_ref, swa_ref, ml_ref, ret_ref, sm_ref):
    xb = x_ref[...].astype(BF16)
    off = 0
    for ref, width in ((gla_ref, W_GLA), (swa_ref, W_SWA), (ml_ref, W_ML), (ret_ref, W_RET), (sm_ref, W_SMALL)):
        ref[...] = _dot_nt(xb, w_ref[off:off + width, :]) + b_ref[:, off:off + width]
        off += width


def _inproj(x2, wm, bm, tm):
    t = x2.shape[0]
    widths = (W_GLA, W_SWA, W_ML, W_RET, W_SMALL)
    return pl.pallas_call(
        _inproj_kernel,
        grid=(t // tm,),
        in_specs=[pl.BlockSpec((tm, D_MODEL), lambda i: (i, 0)),
                  pl.BlockSpec((W_MIX, D_MODEL), lambda i: (0, 0)),
                  pl.BlockSpec((1, W_MIX), lambda i: (0, 0))],
        out_specs=[pl.BlockSpec((tm, w), lambda i: (i, 0)) for w in widths],
        out_shape=[jax.ShapeDtypeStruct((t, w), F32) for w in widths],
        compiler_params=_cparams(("parallel",)),
        name="inproj",
    )(x2, wm, bm)


def _merge_kernel(x_ref, ya_ref, yb_ref, yc_ref, yd_ref, wg_ref, bg_ref, wbr_ref, wout_ref, g1_ref, b1_ref,
                  wrh_ref, wrl_ref, br_ref, x1_ref, route_ref, cnt_ref, xs_ref, *, tm, sub):
    rows_t = xs_ref.shape[0] // sub
    x1, picks = [], []
    for t in range(sub):
        rows = slice(t * tm, (t + 1) * tm)
        x = x_ref[rows, :]
        xb = x.astype(BF16)
        merged = jnp.zeros((tm, D_MODEL), F32)
        for n, y_ref in enumerate((ya_ref, yb_ref, yc_ref, yd_ref)):
            cols = slice(n * D_MODEL, (n + 1) * D_MODEL)
            gate = 0.5 * jnp.tanh(0.5 * (_dot_nt(xb, wg_ref[cols, :]) + bg_ref[:, cols])) + 0.5
            merged = merged + gate * _dot(y_ref[rows, :].astype(BF16), wbr_ref[n])
        mix = _dot(merged.astype(BF16), wout_ref[...])
        x1.append(_layer_norm(ALPHA * x + mix, g1_ref[...], b1_ref[...]))
        x1_ref[rows, :] = x1[t]
        picks.append(_route(x1[t], wrh_ref[...], wrl_ref[...], br_ref[...], tm))
    for t in range(sub):
        route, cnt, xs = _sort(x1[t], picks[t], tm, rows_t)
        route_ref[t * tm:(t + 1) * tm, :] = route
        cnt_ref[t] = cnt
        xs_ref[t * rows_t:(t + 1) * rows_t, :] = xs


def _route(x1, wr_hi, wr_lo, br, tm):
    xh, xl = _split2(x1)
    both = _dot(xh, jnp.concatenate([wr_hi, wr_lo], axis=1))
    logits = both[:, 0:128] + both[:, 128:256] + _dot(xl, wr_hi) + br
    lane = lax.broadcasted_iota(jnp.int32, (tm, 128), 1)
    lanef = lane.astype(F32)
    neg = jnp.float32(-jnp.inf)
    big = jnp.float32(1e9)
    isg = lane < N_GROUPS
    gmax = jnp.max(jnp.where(isg, logits, neg), axis=-1, keepdims=True)
    gidx = jnp.min(jnp.where(isg & (logits == gmax), lanef, big), axis=-1, keepdims=True)
    p_group = 1.0 / jnp.sum(jnp.where(isg, jnp.exp(logits - gmax), 0.0), axis=-1, keepdims=True)
    lo = N_GROUPS + EPG * gidx
    em = (lanef >= lo) & (lanef < lo + EPG)
    v1 = jnp.max(jnp.where(em, logits, neg), axis=-1, keepdims=True)
    i1 = jnp.min(jnp.where(em & (logits == v1), lanef, big), axis=-1, keepdims=True)
    em2 = em & (lanef != i1)
    v2 = jnp.max(jnp.where(em2, logits, neg), axis=-1, keepdims=True)
    i2 = jnp.min(jnp.where(em2 & (logits == v2), lanef, big), axis=-1, keepdims=True)
    e21 = jnp.exp(v2 - v1)
    gt1 = 1.0 / (1.0 + e21)
    return i1, i2, p_group * gt1, p_group * (e21 * gt1)


def _sort(x1, picks, tm, rows_t):
    i1, i2, gate1, gate2 = picks
    lane = lax.broadcasted_iota(jnp.int32, (tm, 128), 1)
    lanef = lane.astype(F32)
    sel1 = lanef == i1
    sel2 = lanef == i2
    onehot = jnp.where(sel1 | sel2, 1.0, 0.0)
    cnt = jnp.sum(onehot, axis=0, keepdims=True)
    seg = jnp.floor((cnt + (PIECE - 1)) * (1.0 / PIECE)) * PIECE
    er = lax.broadcasted_iota(jnp.int32, (128, 128), 0)
    ec = lax.broadcasted_iota(jnp.int32, (128, 128), 1)
    seg_off = _dot(jnp.broadcast_to(seg, (8, 128)).astype(BF16), jnp.where(er < ec, 1.0, 0.0).astype(BF16))[0:1]
    row = lax.broadcasted_iota(jnp.int32, (tm, tm), 0)
    col = lax.broadcasted_iota(jnp.int32, (tm, tm), 1)
    earlier = _dot(jnp.where(col < row, 1.0, 0.0).astype(BF16), onehot.astype(BF16))
    where_to = seg_off + earlier
    w1 = jnp.where(sel1, where_to, 0.0)
    w2 = jnp.where(sel2, where_to, 0.0)
    pos1 = jnp.sum(w1, axis=-1, keepdims=True)
    pos2 = jnp.sum(w2, axis=-1, keepdims=True)
    ones = jnp.ones((8, 128), BF16)
    prow1 = sum(_dot_nt(ones, p) for p in _split2(w1))[0:1]
    prow2 = sum(_dot_nt(ones, p) for p in _split2(w2))[0:1]
    r = lax.broadcasted_iota(jnp.int32, (rows_t, tm), 0).astype(F32)
    place = jnp.where((r == prow1) | (r == prow2), 1.0, 0.0).astype(BF16)
    xs = _dot(place, x1.astype(BF16)).astype(BF16)

    vals = (i1 - N_GROUPS, i2 - N_GROUPS, gate1, gate2, pos1, pos2)
    route = jnp.zeros((tm, 128), F32)
    for k, v in enumerate(vals):
        route = jnp.where(lane == k, v, route)
    return route, cnt, xs


PIECE = 16


def _tile_rows(tm):
    return -(-(2 * tm + N_EXPERTS * (PIECE - 1)) // 128) * 128


def _merge(x2, ys, lw, tm, sub):
    t = x2.shape[0]
    nt = t // tm
    rows_t = _tile_rows(tm)
    tok = lambda w: pl.BlockSpec((sub * tm, w), lambda i: (i, 0))
    full = lambda shape: pl.BlockSpec(shape, lambda i: (0,) * len(shape), pipeline_mode=pl.Buffered(1))
    return pl.pallas_call(
        functools.partial(_merge_kernel, tm=tm, sub=sub),
        grid=(nt // sub,),
        in_specs=[tok(D_MODEL), tok(BRANCH_W), tok(BRANCH_W), tok(BRANCH_W), tok(BRANCH_W),
                  full((N_BRANCH * D_MODEL, D_MODEL)), full((1, N_BRANCH * D_MODEL)),
                  full((N_BRANCH, BRANCH_W, D_MODEL)), full((D_MODEL, D_MODEL)),
                  full((1, D_MODEL)), full((1, D_MODEL)),
                  full((D_MODEL, 128)), full((D_MODEL, 128)), full((1, 128))],
        out_specs=[tok(D_MODEL), tok(128), pl.BlockSpec((sub, 1, 128), lambda i: (i, 0, 0)),
                   pl.BlockSpec((sub * rows_t, D_MODEL), lambda i: (i, 0))],
        out_shape=[jax.ShapeDtypeStruct((t, D_MODEL), F32), jax.ShapeDtypeStruct((t, 128), F32),
                   jax.ShapeDtypeStruct((nt, 1, 128), F32), jax.ShapeDtypeStruct((nt * rows_t, D_MODEL), BF16)],
        compiler_params=_cparams(("parallel",)),
        name="merge",
    )(x2, *ys, lw["wg"], lw["bg"], lw["wbr"], lw["wout"], lw["ln1_g"], lw["ln1_b"], lw["wr_hi"], lw["wr_lo"], lw["br"])


def _piece_copy(hbm, hbm_row, vmem, piece, sem, to_vmem):
    h = hbm.at[pl.ds(pl.multiple_of(hbm_row, PIECE), PIECE), :]
    v = vmem.at[pl.ds(pl.multiple_of(piece * PIECE, PIECE), PIECE), :]
    return pltpu.make_async_copy(h, v, sem) if to_vmem else pltpu.make_async_copy(v, h, sem)


EXPERT_SECTION_ROWS = 512


def _expert_kernel(be_ref, nu_ref, nv_ref, off_ref, xs_hbm, wg_ref, wu_ref, wd_ref, ys_hbm, xin, yout, sem_in, sem_out,
                   *, bp):
    del xs_hbm
    blk = pl.program_id(0)
    n_used = nu_ref[0]
    slot = blk % 2
    sec_rows = min(bp * PIECE, EXPERT_SECTION_ROWS)
    sections = bp * PIECE // sec_rows

    def pieces(b, s, buf, sem, to_vmem, start):
        n = nv_ref[b]

        def one(p, priority):
            cp = _piece_copy(ys_hbm, off_ref[b * bp + p], buf.at[s], p, sem.at[s], to_vmem)
            cp.start(priority=priority) if start else cp.wait()

        def body(i, c):
            one(2 * i, 0)

            @pl.when(2 * i + 1 < n)
            def _():
                one(2 * i + 1, 1)
            return c
        lax.fori_loop(0, (n + 1) // 2, body, 0)

    @pl.when(blk == 0)
    def _():
        xin[...] = jnp.zeros_like(xin)
        for first in range(2):
            @pl.when(first < n_used)
            def _():
                pieces(first, first, xin, sem_in, True, True)

    @pl.when(blk + 2 < n_used)
    def _():
        pieces(blk + 2, (blk + 2) % 3, xin, sem_in, True, True)

    @pl.when(blk < n_used)
    def _():
        in_slot = blk % 3
        pieces(blk, in_slot, xin, sem_in, True, False)

        @pl.when(blk >= 2)
        def _():
            pieces(blk - 2, slot, yout, sem_out, False, False)

        def mlp(lo, hi):
            xb = xin[in_slot, lo:hi]
            hid = jax.nn.silu(_dot(xb, wg_ref[...].astype(BF16))) * _dot(xb, wu_ref[...].astype(BF16))
            yout[slot, lo:hi] = _dot(hid.astype(BF16), wd_ref[...].astype(BF16)).astype(BF16)

        for sec in range(sections):
            if sec == 0:
                mlp(0, sec_rows)
            else:
                @pl.when(nv_ref[blk] * PIECE > sec * sec_rows)
                def _():
                    mlp(sec * sec_rows, (sec + 1) * sec_rows)
        pieces(blk, slot, yout, sem_out, False, True)

    @pl.when(blk == n_used - 1)
    def _():
        @pl.when(blk >= 1)
        def _():
            pieces(blk - 1, 1 - slot, yout, sem_out, False, False)
        pieces(blk, slot, yout, sem_out, False, False)


def _experts(xs, block_expert, n_used, n_valid, piece_off, lw, bp):
    n_blocks = block_expert.shape[0]
    rows = bp * PIECE
    li = lw["layer"]
    wspec = lambda shape: pl.BlockSpec((None, None) + shape, lambda b, be, nu, nv, off: (li, be[b], 0, 0))
    grid_spec = pltpu.PrefetchScalarGridSpec(
        num_scalar_prefetch=4,
        grid=(n_blocks,),
        in_specs=[pl.BlockSpec(memory_space=pl.ANY), wspec((D_MODEL, D_EXPERT)), wspec((D_MODEL, D_EXPERT)),
                  wspec((D_EXPERT, D_MODEL))],
        out_specs=pl.BlockSpec(memory_space=pl.ANY),
        scratch_shapes=[pltpu.VMEM((3, rows, D_MODEL), BF16), pltpu.VMEM((2, rows, D_MODEL), BF16),
                        pltpu.SemaphoreType.DMA((3,)), pltpu.SemaphoreType.DMA((2,))],
    )
    return pl.pallas_call(
        functools.partial(_expert_kernel, bp=bp),
        grid_spec=grid_spec,
        out_shape=jax.ShapeDtypeStruct(xs.shape, BF16),
        input_output_aliases={4: 0},
        compiler_params=_cparams(("arbitrary",)),
        name="experts",
    )(block_expert, n_used, n_valid, piece_off, xs, lw["we_gate"], lw["we_up"], lw["we_down"])


COMBINE_TILES = 4


def _combine_kernel(ys_ref, x1_ref, route_ref, g2_ref, b2_ref, out_ref, *, tm, sub):
    rows_t = ys_ref.shape[0] // sub
    r = lax.broadcasted_iota(jnp.int32, (tm, rows_t), 1).astype(F32)
    routes = [route_ref[t * tm:(t + 1) * tm, :] for t in range(sub)]
    picks = [[jnp.where(r == routes[t][:, 4 + j:5 + j], 1.0, 0.0).astype(BF16) for j in range(2)] for t in range(sub)]
    rows = [[_dot(picks[t][j], ys_ref[t * rows_t:(t + 1) * rows_t, :]) for j in range(2)] for t in range(sub)]
    for t in range(sub):
        ffn = routes[t][:, 2:3] * rows[t][0] + routes[t][:, 3:4] * rows[t][1]
        tok = slice(t * tm, (t + 1) * tm)
        out_ref[tok, :] = _layer_norm(ALPHA * x1_ref[tok, :] + ffn, g2_ref[...], b2_ref[...])


def _combine(ys, x1, route, lw, tm):
    t = x1.shape[0]
    rows_t = _tile_rows(tm)
    sub = COMBINE_TILES if (t // tm) % COMBINE_TILES == 0 else 1
    full = lambda shape: pl.BlockSpec(shape, lambda i: (0,) * len(shape))
    return pl.pallas_call(
        functools.partial(_combine_kernel, tm=tm, sub=sub),
        grid=(t // (sub * tm),),
        in_specs=[pl.BlockSpec((sub * rows_t, D_MODEL), lambda i: (i, 0)),
                  pl.BlockSpec((sub * tm, D_MODEL), lambda i: (i, 0)),
                  pl.BlockSpec((sub * tm, 128), lambda i: (i, 0)), full((1, D_MODEL)), full((1, D_MODEL))],
        out_specs=pl.BlockSpec((sub * tm, D_MODEL), lambda i: (i, 0)),
        out_shape=jax.ShapeDtypeStruct((t, D_MODEL), F32),
        compiler_params=_cparams(("parallel",)),
        name="combine",
    )(ys, x1, route, lw["ln2_g"], lw["ln2_b"])


def _piece_tables(cnt, tm, bp):
    nt = cnt.shape[0]
    rows_t = _tile_rows(tm)
    counts = cnt[:, 0, N_GROUPS:N_GROUPS + N_EXPERTS].astype(jnp.int32)
    pcs = ((counts + PIECE - 1) // PIECE).T
    seg_row = ((jnp.cumsum(pcs, axis=0) - pcs) * PIECE + jnp.arange(nt, dtype=jnp.int32)[None, :] * rows_t)
    tile_end = jnp.cumsum(pcs, axis=1)
    total = tile_end[:, -1]
    e_end = jnp.cumsum((total + bp - 1) // bp * bp)
    e_start = jnp.concatenate([jnp.zeros((1,), jnp.int32), e_end[:-1]])
    n_blocks = -(-(nt * rows_t // PIECE + N_EXPERTS * (bp - 1)) // bp)
    g = jnp.arange(n_blocks * bp, dtype=jnp.int32)[:, None]
    in_e = ((e_start[None, :] <= g) & (g < e_end[None, :])).astype(F32)
    pick = lambda table: jnp.dot(in_e, table.astype(F32), precision=lax.Precision.HIGHEST)
    local = g.astype(F32) - pick(e_start[:, None])
    ends, starts, rows = pick(tile_end), pick(tile_end - pcs), pick(seg_row)
    in_t = ((starts <= local) & (local < ends)).astype(F32)
    off = jnp.sum(in_t * (rows + (local - starts) * PIECE), axis=1).astype(jnp.int32)
    n_valid = jnp.sum(jnp.sum(in_t, axis=1).reshape(n_blocks, bp), axis=1).astype(jnp.int32)
    expert = pick(jnp.arange(N_EXPERTS, dtype=jnp.int32)[:, None])[:, 0]
    block_expert = expert.reshape(n_blocks, bp)[:, 0].astype(jnp.int32)
    n_used = (e_end[-1] // bp).astype(jnp.int32).reshape(1)
    return block_expert, n_used, n_valid, off


def _moe(x1, route, cnt, xs, lw, bp, tm):
    ys = _experts(xs, *_piece_tables(cnt, tm, bp), lw, bp)
    return _combine(ys, x1, route, lw, tm)


CHUNK = 128
STEP_CHUNKS = 16


def _dot_nt(a, b):
    return lax.dot_general(a, b, (((1,), (1,)), ((), ())), preferred_element_type=F32)


def _dot_tn(a, b):
    return lax.dot_general(a, b, (((0,), (0,)), ((), ())), preferred_element_type=F32)


def _lane_head(width, group):
    return lax.broadcasted_iota(jnp.int32, (1, width), 1) // group


def _head_norm(o, gain):
    r = lax.broadcasted_iota(jnp.int32, (BRANCH_W, BRANCH_W), 0) // HD
    c = lax.broadcasted_iota(jnp.int32, (BRANCH_W, BRANCH_W), 1) // HD
    avg = jnp.where(r == c, 1.0 / HD, 0.0).astype(BF16)
    mu = _dot_exact_rhs(o, avg)
    xc = o - mu
    var = _dot_exact_rhs(xc * xc, avg)
    return xc * lax.rsqrt(var + LN_EPS) * gain


def _stack_heads(x, head_of_lane):
    return jnp.concatenate([jnp.where(head_of_lane == h, x, 0.0) for h in range(N_HEADS)], axis=0)


def _pick_heads(stacked, rows, head_of_lane):
    out = jnp.where(head_of_lane == 0, stacked[0:rows], 0.0)
    for h in range(1, N_HEADS):
        out = jnp.where(head_of_lane == h, stacked[h * rows:(h + 1) * rows], out)
    return out


def _spread_heads(cols, head_of_lane):
    out = jnp.where(head_of_lane == 0, cols[0], 0.0)
    for h in range(1, N_HEADS):
        out = jnp.where(head_of_lane == h, cols[h], out)
    return out


SWA_BLOCKS = 32


def _swa_prompt_kernel(q_ref, kv_ref, kvp_ref, sink_ref, y_ref):
    n = pl.program_id(1)
    lo = lax.broadcasted_iota(jnp.int32, (1, 128), 1) < HD
    e = lax.broadcasted_iota(jnp.int32, (2 * WINDOW, N_HEADS * WINDOW), 0)
    col = lax.broadcasted_iota(jnp.int32, (2 * WINDOW, N_HEADS * WINDOW), 1)
    a = col % WINDOW
    band = (e > a) & (e <= a + WINDOW)
    head = lax.broadcasted_iota(jnp.int32, (1, N_HEADS * WINDOW), 1) // WINDOW
    sink = _spread_heads([sink_ref[:, h:h + 1] for h in range(N_HEADS)], head)
    blocks = range(SWA_BLOCKS)
    scores, vexts = [], []
    for i in blocks:
        rows = slice(i * WINDOW, (i + 1) * WINDOW)
        q = q_ref[rows, :]
        qa, qb = q[:, 0:128], q[:, 128:256]
        kv = kv_ref[rows, :]
        kvp = kvp_ref[...] if i == 0 else kv_ref[(i - 1) * WINDOW:i * WINDOW, :]
        kext = jnp.concatenate([kvp[:, 0:128], kv[:, 0:128]], axis=0).astype(BF16)
        vexts.append(jnp.concatenate([kvp[:, 128:256], kv[:, 128:256]], axis=0).astype(BF16))
        qs = jnp.concatenate([jnp.where(lo, qa, 0.0), jnp.where(lo, qb, 0.0),
                              jnp.where(lo, 0.0, qa), jnp.where(lo, 0.0, qb)], axis=0).astype(BF16)
        scores.append(_dot_nt(kext, qs))
    probs, inv_den = [], []
    for i in blocks:
        valid = band & ((n > 0) | (e >= WINDOW)) if i == 0 else band
        s = jnp.where(valid, scores[i] * (HD ** -0.5), -jnp.inf)
        mx = jnp.maximum(jnp.max(s, axis=0, keepdims=True), sink)
        p = jnp.exp(s - mx)
        inv_den.append(1.0 / (jnp.sum(p, axis=0, keepdims=True) + jnp.exp(sink - mx)))
        probs.append(p.astype(BF16))
    outs = [_dot_tn(vexts[i], probs[i]) for i in blocks]
    for i in blocks:
        o = outs[i] * inv_den[i]
        ya = jnp.concatenate([o[0:HD, 0:128], o[HD:128, 256:384]], axis=0).T
        yb = jnp.concatenate([o[0:HD, 128:256], o[HD:128, 384:512]], axis=0).T
        y_ref[i * WINDOW:(i + 1) * WINDOW, :] = jnp.concatenate([ya, yb], axis=1)


def _swa_prompt(swa, lw, bsz, seq):
    qt = SWA_BLOCKS * WINDOW
    nt = seq // qt
    return pl.pallas_call(
        _swa_prompt_kernel,
        grid=(bsz, nt),
        in_specs=[pl.BlockSpec((qt, 256), lambda b, n: (b * nt + n, 0)),
                  pl.BlockSpec((qt, 256), lambda b, n: (b * nt + n, 1)),
                  pl.BlockSpec((WINDOW, 256), lambda b, n: (jnp.maximum((b * nt + n) * SWA_BLOCKS - 1, 0), 1)),
                  pl.BlockSpec((1, 128), lambda b, n: (0, 0))],
        out_specs=pl.BlockSpec((qt, BRANCH_W), lambda b, n: (b * nt + n, 0)),
        out_shape=jax.ShapeDtypeStruct((bsz * seq, BRANCH_W), F32),
        compiler_params=_cparams(("parallel", "parallel")),
        name="swa_prompt",
    )(swa, swa, swa, lw["sinks"])


def _rotate(x, cos, sin):
    x1, x2 = x[:, 0:128], x[:, 128:256]
    return jnp.concatenate([x1 * cos - x2 * sin, x1 * sin + x2 * cos], axis=1)


def _ret_prompt_kernel(r_ref, cos_ref, sin_ref, dmat_ref, qdec_ref, kdec_ref, cdec_ref, g_ref, y_ref, s_out_ref, s_ref):
    n = pl.program_id(1)

    @pl.when(n == 0)
    def _():
        s_ref[...] = jnp.zeros_like(s_ref)

    c = CHUNK
    chunks = range(STEP_CHUNKS)
    hq = _lane_head(BRANCH_W, 32) % N_HEADS
    hv = _lane_head(BRANCH_W, HD)
    rows_head = (lax.broadcasted_iota(jnp.int32, (BRANCH_W, 1), 0) // 32) % N_HEADS
    qr, kr, v = [], [], []
    for i in chunks:
        rows = slice(i * c, (i + 1) * c)
        cos, sin = cos_ref[rows, :], sin_ref[rows, :]
        qr.append(_rotate(r_ref[rows, 0:256], cos, sin))
        kr.append(_rotate(r_ref[rows, 256:512], cos, sin) * (HD ** -0.5))
        v.append(r_ref[rows, 512:768].astype(BF16))
    att = [_dot_nt(_stack_heads(qr[i], hq).astype(BF16), kr[i].astype(BF16)) * dmat_ref[...] for i in chunks]
    upd = [_dot_tn((kr[i] * kdec_ref[...]).astype(BF16), v[i]) for i in chunks]
    states = [s_ref[...]]
    for i in chunks:
        states.append(cdec_ref[...] * states[i] + jnp.where(rows_head == hv, upd[i], 0.0))
    o = [_pick_heads(_dot(att[i].astype(BF16), v[i]), c, hv) for i in chunks]
    o = [o[i] + _dot((qr[i] * qdec_ref[...]).astype(BF16), states[i].astype(BF16)) for i in chunks]
    s_ref[...] = states[-1]
    s_out_ref[...] = states[-1]
    y_ref[...] = jax.nn.silu(r_ref[:, 768:1024]) * _head_norm(jnp.concatenate(o, axis=0), g_ref[...])


def _ret_tables(seq):
    c = CHUNK
    half = HD // 2
    freq = ROPE_BASE ** (-jnp.arange(half, dtype=F32) / half)
    ang = jnp.arange(seq).astype(F32)[:, None] * freq[None, :]
    cos = jnp.tile(jnp.cos(ang), (1, N_HEADS))
    sin = jnp.tile(jnp.sin(ang), (1, N_HEADS))
    log_g = jnp.log1p(-jnp.exp2(-5.0 - jnp.arange(N_HEADS, dtype=F32)))
    idx = jnp.arange(c, dtype=F32)
    rel = idx[:, None] - idx[None, :]
    dmat = jnp.exp(jnp.where(rel[None] >= 0, rel[None] * log_g[:, None, None], -jnp.inf)).reshape(N_HEADS * c, c)
    lane_head_q = (jnp.arange(BRANCH_W) // 32) % N_HEADS
    qdec = jnp.exp((idx + 1.0)[:, None] * log_g[lane_head_q][None, :])
    kdec = jnp.exp((c - 1.0 - idx)[:, None] * log_g[lane_head_q][None, :])
    cdec = jnp.exp(c * log_g)[jnp.arange(BRANCH_W) // HD][None, :]
    return cos, sin, dmat, qdec, kdec, cdec


def _ret_prompt(ret, lw, bsz, seq):
    c = CHUNK
    rows = STEP_CHUNKS * c
    nc = seq // rows
    cos, sin, dmat, qdec, kdec, cdec = _ret_tables(seq)
    full = lambda shape: pl.BlockSpec(shape, lambda b, n: (0,) * len(shape))
    return pl.pallas_call(
        _ret_prompt_kernel,
        grid=(bsz, nc),
        in_specs=[pl.BlockSpec((rows, W_RET), lambda b, n: (b * nc + n, 0)),
                  pl.BlockSpec((rows, 128), lambda b, n: (n, 0)), pl.BlockSpec((rows, 128), lambda b, n: (n, 0)),
                  full((N_HEADS * c, c)), full((c, BRANCH_W)), full((c, BRANCH_W)), full((1, BRANCH_W)),
                  full((1, BRANCH_W))],
        out_specs=[pl.BlockSpec((rows, BRANCH_W), lambda b, n: (b * nc + n, 0)),
                   pl.BlockSpec((None, BRANCH_W, BRANCH_W), lambda b, n: (b, 0, 0))],
        out_shape=[jax.ShapeDtypeStruct((bsz * seq, BRANCH_W), F32),
                   jax.ShapeDtypeStruct((bsz, BRANCH_W, BRANCH_W), F32)],
        scratch_shapes=[pltpu.VMEM((BRANCH_W, BRANCH_W), F32)],
        compiler_params=_cparams(("parallel", "arbitrary")),
        name="ret_prompt",
    )(ret, cos, sin, dmat, qdec, kdec, cdec, lw["ret_g"])


def _ret_state_from_blockdiag(s_bd):
    b = s_bd.shape[0]
    s = s_bd.reshape(b, 2, N_HEADS, 32, N_HEADS, HD)
    s = jnp.stack([s[:, :, h, :, h, :] for h in range(N_HEADS)], axis=1)
    return s.reshape(b, N_HEADS, HD, HD)


ML_I_LANE = GLA_RANK
ML_F_LANE = GLA_RANK + N_HEADS


def _ml_prompt_kernel(m_ref, sm_ref, g_ref, y_ref, c_out_ref, n_out_ref, m_out_ref, c_ref, n_ref, mm_ref):
    nchunk = pl.program_id(1)

    @pl.when(nchunk == 0)
    def _():
        c_ref[...] = jnp.zeros_like(c_ref)
        n_ref[...] = jnp.zeros_like(n_ref)
        mm_ref[...] = jnp.zeros_like(mm_ref)

    c = CHUNK
    chunks = range(STEP_CHUNKS)
    row = lax.broadcasted_iota(jnp.int32, (c, c), 0)
    col = lax.broadcasted_iota(jnp.int32, (c, c), 1)
    tri_lo = jnp.where(col <= row, 1.0, 0.0).astype(BF16)
    tri_up = jnp.where(row <= col, 1.0, 0.0).astype(BF16)
    hv = _lane_head(BRANCH_W, HD)
    hq = _lane_head(N_HEADS * c, c)
    key = lax.broadcasted_iota(jnp.int32, (c, N_HEADS * c), 0)
    qry = lax.broadcasted_iota(jnp.int32, (c, N_HEADS * c), 1) % c
    causal_t = key <= qry
    r = lax.broadcasted_iota(jnp.int32, (BRANCH_W, BRANCH_W), 0) // HD
    cc = lax.broadcasted_iota(jnp.int32, (BRANCH_W, BRANCH_W), 1) // HD
    same_head = r == cc
    lane = lax.broadcasted_iota(jnp.int32, (1, 128), 1)
    gate_lane = lax.broadcasted_iota(jnp.int32, (128, 1), 0)
    is_i = (gate_lane >= ML_I_LANE) & (gate_lane < ML_I_LANE + N_HEADS)
    is_f = (gate_lane >= ML_F_LANE) & (gate_lane < ML_F_LANE + N_HEADS)
    gate_head = jnp.where(is_i, gate_lane - ML_I_LANE, gate_lane - ML_F_LANE)
    sel_q = jnp.where((is_i | is_f) & (gate_head == hq), 1.0, 0.0).astype(BF16)
    sel_v = jnp.where((is_i | is_f) & (gate_head == hv), 1.0, 0.0).astype(BF16)
    sel_qv = jnp.concatenate([sel_q, sel_v], axis=1)
    lane128 = lax.broadcasted_iota(jnp.int32, (1, 128), 1)
    in_i = (lane128 >= ML_I_LANE) & (lane128 < ML_I_LANE + N_HEADS)
    in_f = (lane128 >= ML_F_LANE) & (lane128 < ML_F_LANE + N_HEADS)
    head_rows = lax.broadcasted_iota(jnp.int32, (8, 1), 0)

    q, ks, vt, brow, u_q, u_v, sc = [], [], [], [], [], [], []
    for i in chunks:
        rows = slice(i * c, (i + 1) * c)
        q.append(m_ref[rows, 0:256])
        ks.append(m_ref[rows, 256:512] * (HD ** -0.5))
        vt.append(m_ref[rows, 512:768].T.astype(BF16))
        sm = sm_ref[rows, :]
        bcols = _dot_exact_lhs(tri_lo, jax.nn.log_sigmoid(sm))
        brows = _dot_exact_rhs(jax.nn.log_sigmoid(sm.T), tri_up)
        brow.append(jnp.concatenate([brows[ML_F_LANE + h:ML_F_LANE + h + 1, :] for h in range(N_HEADS)], axis=1))
        z = jnp.where(in_i, sm, 0.0) - jnp.where(in_f, bcols, 0.0)
        u = _dot_exact_rhs(z, sel_qv)
        u_q.append(u[:, 0:N_HEADS * c])
        u_v.append(u[:, N_HEADS * c:])
        sc.append(_dot_nt(ks[i].astype(BF16), _stack_heads(q[i], hv).astype(BF16)))

    mm = mm_ref[...]
    qk, w_inter, rowsum, floor, khat, carry_b = [], [], [], [], [], []
    for i in chunks:
        m_prev = _spread_heads([mm[:, h:h + 1] for h in range(N_HEADS)], hq)
        d_log = jnp.where(causal_t, brow[i] + u_q[i], -jnp.inf)
        inter_log = brow[i] + m_prev
        m_t = jnp.maximum(inter_log, jnp.max(d_log, axis=0, keepdims=True))
        qk_i = sc[i] * jnp.exp(d_log - m_t)
        qk.append(qk_i.astype(BF16))
        rowsum.append(jnp.sum(qk_i, axis=0, keepdims=True))
        w_inter.append(jnp.exp(inter_log - m_t))
        floor.append(jnp.exp(-m_t))
        m_new = [m_t[:, (h + 1) * c - 1:(h + 1) * c] for h in range(N_HEADS)]
        b_last = [brow[i][:, (h + 1) * c - 1:(h + 1) * c] for h in range(N_HEADS)]
        carry_b.append(_spread_heads([jnp.exp(b_last[h] + mm[:, h:h + 1] - m_new[h]) for h in range(N_HEADS)], hv))
        khat.append(ks[i] * jnp.exp(u_v[i] + _spread_heads([b_last[h] - m_new[h] for h in range(N_HEADS)], hv)))
        mm = jnp.zeros((1, 128), F32)
        for h in range(N_HEADS):
            mm = jnp.where(lane == h, m_new[h], mm)

    intra = [_dot(vt[i], qk[i]) for i in chunks]
    upd = [_dot(vt[i], khat[i].astype(BF16)) for i in chunks]
    cms, nrows = [c_ref[...]], [n_ref[...]]
    for i in chunks:
        cms.append(carry_b[i] * cms[i] + jnp.where(same_head, upd[i], 0.0))
        nrows.append(carry_b[i] * nrows[i] + jnp.sum(khat[i], axis=0, keepdims=True))
    hout = []
    for i in chunks:
        qb = q[i].astype(BF16)
        inter = _dot_nt(cms[i].astype(BF16), qb)
        n_sel = jnp.where(head_rows == hv, nrows[i], 0.0)
        qn = sum(_dot_nt(p, qb) for p in _split3(n_sel))
        blocks = []
        for h in range(N_HEADS):
            cols = slice(h * c, (h + 1) * c)
            den = w_inter[i][:, cols] * qn[h:h + 1, :] + rowsum[i][:, cols]
            num = w_inter[i][:, cols] * inter[h * HD:(h + 1) * HD, :] + intra[i][h * HD:(h + 1) * HD, cols]
            blocks.append(num / jnp.maximum(jnp.abs(den), floor[i][:, cols]))
        hout.append(jnp.concatenate(blocks, axis=0).T)
    y_ref[...] = _head_norm(jax.nn.sigmoid(m_ref[:, 768:1024]) * jnp.concatenate(hout, axis=0), g_ref[...])

    c_ref[...] = cms[-1]
    n_ref[...] = nrows[-1]
    mm_ref[...] = mm
    c_out_ref[...] = cms[-1]
    n_out_ref[...] = nrows[-1]
    m_out_ref[...] = mm


def _ml_prompt(ml, small, lw, bsz, seq):
    rows = STEP_CHUNKS * CHUNK
    nc = seq // rows
    return pl.pallas_call(
        _ml_prompt_kernel,
        grid=(bsz, nc),
        in_specs=[pl.BlockSpec((rows, W_ML), lambda b, n: (b * nc + n, 0)),
                  pl.BlockSpec((rows, W_SMALL), lambda b, n: (b * nc + n, 0)),
                  pl.BlockSpec((1, BRANCH_W), lambda b, n: (0, 0))],
        out_specs=[pl.BlockSpec((rows, BRANCH_W), lambda b, n: (b * nc + n, 0)),
                   pl.BlockSpec((None, BRANCH_W, BRANCH_W), lambda b, n: (b, 0, 0)),
                   pl.BlockSpec((None, 1, BRANCH_W), lambda b, n: (b, 0, 0)),
                   pl.BlockSpec((None, 1, 128), lambda b, n: (b, 0, 0))],
        out_shape=[jax.ShapeDtypeStruct((bsz * seq, BRANCH_W), F32),
                   jax.ShapeDtypeStruct((bsz, BRANCH_W, BRANCH_W), F32),
                   jax.ShapeDtypeStruct((bsz, 1, BRANCH_W), F32),
                   jax.ShapeDtypeStruct((bsz, 1, 128), F32)],
        scratch_shapes=[pltpu.VMEM((BRANCH_W, BRANCH_W), F32), pltpu.VMEM((1, BRANCH_W), F32),
                        pltpu.VMEM((1, 128), F32)],
        compiler_params=_cparams(("parallel", "arbitrary")),
        name="ml_prompt",
    )(ml, small, lw["ml_g"])


def _diag_blocks(s_bd):
    b = s_bd.shape[0]
    s = s_bd.reshape(b, N_HEADS, HD, N_HEADS, HD)
    return jnp.stack([s[:, h, :, h, :] for h in range(N_HEADS)], axis=1)


GLA_TILE = 256
GLA_STEP_TILES = 8
GLA_SUB = 16


def _gla_prompt_kernel(x_ref, sm_ref, wa2_ref, ba2_ref, g_ref, y_ref, s_out_ref, s_ref):
    n = pl.program_id(1)

    @pl.when(n == 0)
    def _():
        s_ref[...] = jnp.zeros_like(s_ref)

    tq, sc = GLA_TILE, GLA_SUB
    r = lax.broadcasted_iota(jnp.int32, (tq, tq), 0)
    c = lax.broadcasted_iota(jnp.int32, (tq, tq), 1)
    same = (r // sc) == (c // sc)
    cum_in_sub = jnp.where(same & (c <= r), 1.0, 0.0).astype(BF16)
    sum_in_sub = jnp.where(same, 1.0, 0.0).astype(BF16)
    kh = lax.broadcasted_iota(jnp.int32, (128, 1), 0) // GLA_DK
    hv = _lane_head(BRANCH_W, HD)
    expand = jnp.where(kh == hv, 1.0, 0.0).astype(BF16)
    trow = lax.broadcasted_iota(jnp.int32, (sc, 1), 0)
    n_sub = tq // sc
    chunks = range(GLA_STEP_TILES * n_sub)

    q, k, v, b, qd, dec_t, upds = [], [], [], [], [], [], []
    for t in range(GLA_STEP_TILES):
        rows = slice(t * tq, (t + 1) * tq)
        q.append(x_ref[rows, 0:128] * (GLA_DK ** -0.5))
        k.append(x_ref[rows, 128:256])
        v.append(x_ref[rows, 256:512])
        log_a = jax.nn.log_sigmoid(_dot(sm_ref[rows, :].astype(BF16), wa2_ref[...]) + ba2_ref[...]) / GLA_TAU
        b.append(_dot_exact_lhs(cum_in_sub, log_a))
        btot = _dot_exact_lhs(sum_in_sub, log_a)
        qd.append((q[t] * jnp.exp(b[t])).astype(BF16))
        dec_t.append(jnp.exp(btot).T)
        khat_b = (k[t] * jnp.exp(btot - b[t])).astype(BF16)
        vb = v[t].astype(BF16)
        upds += [_dot_tn(khat_b[ci * sc:(ci + 1) * sc], vb[ci * sc:(ci + 1) * sc]) for ci in range(n_sub)]
    states = [s_ref[...]]
    for g in chunks:
        t, lo = g // n_sub, (g % n_sub) * sc
        states.append(dec_t[t][:, lo:lo + 1] * states[g] + jnp.where(kh == hv, upds[g], 0.0))
    outs = []
    for g in chunks:
        t, lo = g // n_sub, (g % n_sub) * sc
        qc, kc, bc, vc = q[t][lo:lo + sc], k[t][lo:lo + sc], b[t][lo:lo + sc], v[t][lo:lo + sc]
        pair = [qc * kc[j:j + 1] * jnp.exp(jnp.where(trow >= j, bc - bc[j:j + 1], -jnp.inf)) for j in range(sc)]
        att = _dot(jnp.concatenate(pair, axis=0).astype(BF16), expand)
        o = att[0:sc] * vc[0:1]
        for j in range(1, sc):
            o = o + att[j * sc:(j + 1) * sc] * vc[j:j + 1]
        outs.append(o)
    outs = [outs[g] + _dot(qd[g // n_sub][(g % n_sub) * sc:(g % n_sub + 1) * sc], states[g].astype(BF16)) for g in chunks]
    s_ref[...] = states[-1]
    s_out_ref[...] = states[-1]
    y_ref[...] = jax.nn.silu(x_ref[:, 512:768]) * _head_norm(jnp.concatenate(outs, axis=0), g_ref[...])


def _gla_prompt(gla, small, lw, bsz, seq):
    tq = GLA_STEP_TILES * GLA_TILE
    nt = seq // tq
    full = lambda shape: pl.BlockSpec(shape, lambda b, n: (0,) * len(shape))
    return pl.pallas_call(
        _gla_prompt_kernel,
        grid=(bsz, nt),
        in_specs=[pl.BlockSpec((tq, W_GLA), lambda b, n: (b * nt + n, 0)),
                  pl.BlockSpec((tq, W_SMALL), lambda b, n: (b * nt + n, 0)),
                  full((128, 128)), full((1, 128)), full((1, BRANCH_W))],
        out_specs=[pl.BlockSpec((tq, BRANCH_W), lambda b, n: (b * nt + n, 0)),
                   pl.BlockSpec((None, 128, BRANCH_W), lambda b, n: (b, 0, 0))],
        out_shape=[jax.ShapeDtypeStruct((bsz * seq, BRANCH_W), F32),
                   jax.ShapeDtypeStruct((bsz, 128, BRANCH_W), F32)],
        scratch_shapes=[pltpu.VMEM((128, BRANCH_W), F32)],
        compiler_params=_cparams(("parallel", "arbitrary")),
        name="gla_prompt",
    )(gla, small, lw["wa2"], lw["ba2"], lw["gla_g"])


def _gla_state_from_blockdiag(s_bd):
    b = s_bd.shape[0]
    s = s_bd.reshape(b, N_HEADS, GLA_DK, N_HEADS, HD)
    return jnp.stack([s[:, h, :, h, :] for h in range(N_HEADS)], axis=1)


LANES = 128


def _rank1_update(st_ref, snt_ref, q_ref, k_ref, decay_row, v, dk):
    def body(i, o):
        rows = pl.ds(pl.multiple_of(i * HD, HD), HD)
        new = decay_row(i) * st_ref[rows, :] + k_ref[pl.ds(i, 1), :] * v
        snt_ref[rows, :] = new
        return o + q_ref[pl.ds(i, 1), :] * new
    return lax.fori_loop(0, dk, body, jnp.zeros((HD, LANES), F32), unroll=4)


def _head_rows(h, n):
    return pl.ds(pl.multiple_of(h * n, n), n)


def _gla_sample_kernel(x_ref, sm_ref, wa2_ref, ba2_ref, g_ref, s_ref, y_ref, s_out_ref,
                       qt_ref, kt_ref, at_ref, vt_ref, ot_ref):
    h = pl.program_id(0)

    @pl.when(h == 0)
    def _():
        x = x_ref[...]
        qt_ref[...] = (x[:, 0:128] * (GLA_DK ** -0.5)).T
        kt_ref[...] = x[:, 128:256].T
        vt_ref[...] = x[:, 256:512].T
        log_a = jax.nn.log_sigmoid(_dot(sm_ref[...].astype(BF16), wa2_ref[...]) + ba2_ref[...]) / GLA_TAU
        at_ref[...] = jnp.exp(log_a).T

    base = h * GLA_DK
    o = _rank1_update(s_ref, s_out_ref, qt_ref.at[_head_rows(h, GLA_DK)], kt_ref.at[_head_rows(h, GLA_DK)],
                      lambda i: at_ref[pl.ds(base + i, 1), :], vt_ref[_head_rows(h, HD), :], GLA_DK)
    ot_ref[_head_rows(h, HD), :] = o

    @pl.when(h == N_HEADS - 1)
    def _():
        y_ref[...] = jax.nn.silu(x_ref[:, 512:768]) * _head_norm(ot_ref[...].T, g_ref[...])


def _gla_sample(gla, small, state, lw):
    per_head = GLA_DK * HD
    li = lw["layer"]
    full = lambda shape: pl.BlockSpec(shape, lambda h: (0,) * len(shape))
    vm = lambda r: pltpu.VMEM((r, LANES), F32)
    return pl.pallas_call(
        _gla_sample_kernel,
        grid=(N_HEADS,),
        in_specs=[full((LANES, W_GLA)), full((LANES, W_SMALL)), full((128, 128)), full((1, 128)), full((1, BRANCH_W)),
                  pl.BlockSpec((None, per_head, LANES), lambda h: (li, h, 0))],
        out_specs=[full((LANES, BRANCH_W)), pl.BlockSpec((per_head, LANES), lambda h: (h, 0))],
        out_shape=[jax.ShapeDtypeStruct((LANES, BRANCH_W), F32), jax.ShapeDtypeStruct(state.shape[1:], F32)],
        scratch_shapes=[vm(128), vm(128), vm(128), vm(256), vm(256)],
        compiler_params=_cparams(("arbitrary",)),
        name="gla_sample",
    )(gla, small, lw["wa2"], lw["ba2"], lw["gla_g"], state)


def _ml_sample_kernel(x_ref, sc_ref, n_ref, g_ref, c_ref, y_ref, c_out_ref, n_out_ref, m_out_ref,
                      qt_ref, kt_ref, vt_ref, ot_ref, w_ref, kp_ref):
    h = pl.program_id(0)

    @pl.when(h == 0)
    def _():
        x = x_ref[...]
        qt_ref[...] = x[:, 0:256].T
        kt_ref[...] = (x[:, 256:512] * (HD ** -0.5)).T
        vt_ref[...] = x[:, 512:768].T
        sct = sc_ref[...].T
        log_f = jax.nn.log_sigmoid(sct[8:16])
        m_old = sct[16:24]
        m_new = jnp.maximum(log_f + m_old, sct[0:8])
        w_ref[0:8, :] = jnp.exp(log_f + m_old - m_new)
        w_ref[8:16, :] = jnp.exp(sct[0:8] - m_new)
        w_ref[16:24, :] = m_new

    w_carry = w_ref[pl.ds(h, 1), :]
    w_tok = w_ref[pl.ds(8 + h, 1), :]
    m_new = w_ref[pl.ds(16 + h, 1), :]
    kp_ref[...] = kt_ref[_head_rows(h, HD), :] * w_tok
    q_h = qt_ref.at[_head_rows(h, HD)]
    n_new = w_carry * n_ref[_head_rows(h, HD), :] + kp_ref[...]
    num = _rank1_update(c_ref, c_out_ref, q_h, kp_ref, lambda i: w_carry, vt_ref[_head_rows(h, HD), :], HD)
    den = jnp.sum(q_h[...] * n_new, axis=0, keepdims=True)
    ot_ref[_head_rows(h, HD), :] = num / jnp.maximum(jnp.abs(den), jnp.exp(-m_new))
    n_out_ref[_head_rows(h, HD), :] = n_new

    @pl.when(h == N_HEADS - 1)
    def _():
        y_ref[...] = _head_norm(jax.nn.sigmoid(x_ref[:, 768:1024]) * ot_ref[...].T, g_ref[...])
        m_out_ref[...] = jnp.concatenate([w_ref[16:24, :], jnp.zeros((LANES - 8, LANES), F32)], axis=0).T


def _ml_sample(ml, scal, n_state, c_state, lw):
    per_head = HD * HD
    li = lw["layer"]
    full = lambda shape: pl.BlockSpec(shape, lambda h: (0,) * len(shape))
    vm = lambda r: pltpu.VMEM((r, LANES), F32)
    return pl.pallas_call(
        _ml_sample_kernel,
        grid=(N_HEADS,),
        in_specs=[full((LANES, W_ML)), full((LANES, 128)),
                  pl.BlockSpec((None, BRANCH_W, LANES), lambda h: (li, 0, 0)), full((1, BRANCH_W)),
                  pl.BlockSpec((None, per_head, LANES), lambda h: (li, h, 0))],
        out_specs=[full((LANES, BRANCH_W)), pl.BlockSpec((per_head, LANES), lambda h: (h, 0)),
                   full((BRANCH_W, LANES)), full((LANES, 128))],
        out_shape=[jax.ShapeDtypeStruct((LANES, BRANCH_W), F32), jax.ShapeDtypeStruct(c_state.shape[1:], F32),
                   jax.ShapeDtypeStruct((BRANCH_W, LANES), F32), jax.ShapeDtypeStruct((LANES, 128), F32)],
        scratch_shapes=[vm(256), vm(256), vm(256), vm(256), vm(24), vm(HD)],
        compiler_params=_cparams(("arbitrary",)),
        name="ml_sample",
    )(ml, scal, n_state, lw["ml_g"], c_state)


def _ret_sample_kernel(x_ref, cos_ref, sin_ref, gam_ref, g_ref, s_ref, y_ref, s_out_ref,
                       qt_ref, kt_ref, vt_ref, ot_ref, qr_ref, kr_ref):
    h = pl.program_id(0)

    @pl.when(h == 0)
    def _():
        x = x_ref[...]
        qt_ref[...] = x[:, 0:256].T
        kt_ref[...] = (x[:, 256:512] * (HD ** -0.5)).T
        vt_ref[...] = x[:, 512:768].T

    cos, sin = cos_ref[...], sin_ref[...]
    for src, dst in ((qt_ref, qr_ref), (kt_ref, kr_ref)):
        x1 = src[_head_rows(h, 32), :]
        x2 = src[pl.ds(pl.multiple_of(128 + h * 32, 32), 32), :]
        dst[0:32, :] = x1 * cos - x2 * sin
        dst[32:64, :] = x1 * sin + x2 * cos
    gamma = gam_ref[pl.ds(h, 1), :]
    ot_ref[_head_rows(h, HD), :] = _rank1_update(s_ref, s_out_ref, qr_ref, kr_ref, lambda i: gamma,
                                                 vt_ref[_head_rows(h, HD), :], HD)

    @pl.when(h == N_HEADS - 1)
    def _():
        y_ref[...] = jax.nn.silu(x_ref[:, 768:1024]) * _head_norm(ot_ref[...].T, g_ref[...])


def _ret_sample(ret, state, lw):
    per_head = HD * HD
    li = lw["layer"]
    half = HD // 2
    freq = ROPE_BASE ** (-jnp.arange(half, dtype=F32) / half)
    ang = jnp.full((1,), PAST_LEN, F32)[:, None] * freq[None, :]
    cos = jnp.broadcast_to(jnp.cos(ang).reshape(half, 1), (half, LANES))
    sin = jnp.broadcast_to(jnp.sin(ang).reshape(half, 1), (half, LANES))
    log_g = jnp.log1p(-jnp.exp2(-5.0 - jnp.arange(N_HEADS, dtype=F32)))
    gam = jnp.broadcast_to(jnp.concatenate([jnp.exp(log_g), jnp.ones((4,), F32)]).reshape(8, 1), (8, LANES))
    full = lambda shape: pl.BlockSpec(shape, lambda h: (0,) * len(shape))
    vm = lambda r: pltpu.VMEM((r, LANES), F32)
    return pl.pallas_call(
        _ret_sample_kernel,
        grid=(N_HEADS,),
        in_specs=[full((LANES, W_RET)), full((half, LANES)), full((half, LANES)), full((8, LANES)), full((1, BRANCH_W)),
                  pl.BlockSpec((None, per_head, LANES), lambda h: (li, h, 0))],
        out_specs=[full((LANES, BRANCH_W)), pl.BlockSpec((per_head, LANES), lambda h: (h, 0))],
        out_shape=[jax.ShapeDtypeStruct((LANES, BRANCH_W), F32), jax.ShapeDtypeStruct(state.shape[1:], F32)],
        scratch_shapes=[vm(256), vm(256), vm(256), vm(256), vm(HD), vm(HD)],
        compiler_params=_cparams(("arbitrary",)),
        name="ret_sample",
    )(ret, cos, sin, gam, lw["ret_g"], state)


def _swa_sample_kernel(q_ref, kn_ref, vn_ref, knc_ref, vnc_ref, sink_ref, kc_ref, vc_ref, o_ref, ko_ref, vo_ref):
    q = q_ref[...]
    kc, vc = kc_ref[...], vc_ref[...]
    kn, vn = kn_ref[...], vn_ref[...]
    scale = HD ** -0.5
    s = _einsum("bhd,bdw->bhw", q.astype(BF16), kc.astype(BF16)) * scale
    s_new = jnp.sum(q * kn, axis=-1, keepdims=True) * scale
    w = lax.broadcasted_iota(jnp.int32, (1, 1, WINDOW), 2)
    s = jnp.where(w > 0, s, -jnp.inf)
    sink = sink_ref[...][:, 0:1]
    mx = jnp.maximum(jnp.maximum(jnp.max(s, axis=-1, keepdims=True), s_new), sink)
    p = jnp.exp(s - mx)
    p_new = jnp.exp(s_new - mx)
    den = jnp.sum(p, axis=-1, keepdims=True) + p_new + jnp.exp(sink - mx)
    o = _einsum("bhw,bdw->bhd", p.astype(BF16), vc.astype(BF16)) + p_new * vn
    o_ref[...] = o / den
    newest = w == WINDOW - 1
    ko_ref[...] = jnp.where(newest, knc_ref[...], pltpu.roll(kc, WINDOW - 1, 2))
    vo_ref[...] = jnp.where(newest, vnc_ref[...], pltpu.roll(vc, WINDOW - 1, 2))


def _swa_sample(swa, cache_k, cache_v, lw, bt=16):
    bsz = swa.shape[0]
    li = lw["layer"]
    cache = pl.BlockSpec((None, bt, WINDOW, 128), lambda i: (li, i, 0, 0))
    lo = (jnp.arange(128) < HD)[None, :]
    qa, qb = swa[:, 0:128], swa[:, 128:256]
    q = jnp.stack([jnp.where(lo, qa, 0.0), jnp.where(lo, qb, 0.0), jnp.where(lo, 0.0, qa), jnp.where(lo, 0.0, qb)], axis=1)
    kn = swa[:, None, 256:384]
    vn = swa[:, None, 384:512]
    sink = jnp.broadcast_to(lw["sinks"][0, 0:N_HEADS].reshape(N_HEADS, 1), (N_HEADS, 128))
    blk = lambda shape: pl.BlockSpec(shape, lambda i: (i,) + (0,) * (len(shape) - 1))
    o, ko, vo = pl.pallas_call(
        _swa_sample_kernel,
        grid=(bsz // bt,),
        in_specs=[blk((bt, N_HEADS, 128)), blk((bt, 1, 128)), blk((bt, 1, 128)), blk((bt, 128, 1)), blk((bt, 128, 1)),
                  pl.BlockSpec((N_HEADS, 128), lambda i: (0, 0)), cache, cache],
        out_specs=[blk((bt, N_HEADS, 128)), blk((bt, WINDOW, 128)), blk((bt, WINDOW, 128))],
        out_shape=[jax.ShapeDtypeStruct((bsz, N_HEADS, 128), F32), jax.ShapeDtypeStruct(cache_k.shape[1:], F32),
                   jax.ShapeDtypeStruct(cache_v.shape[1:], F32)],
        compiler_params=_cparams(("parallel",)),
        name="swa_sample",
    )(q, kn, vn, jnp.swapaxes(kn, 1, 2), jnp.swapaxes(vn, 1, 2), sink, cache_k, cache_v)
    y = jnp.concatenate([o[:, 0, 0:64], o[:, 2, 64:128], o[:, 1, 0:64], o[:, 3, 64:128]], axis=1)
    return y, ko, vo


def _ml_scalars(small, m_state):
    z4 = jnp.zeros((small.shape[0], 4), F32)
    return jnp.concatenate([small[:, ML_I_LANE:ML_I_LANE + 4], z4, small[:, ML_F_LANE:ML_F_LANE + 4], z4, m_state, z4,
                            jnp.zeros((small.shape[0], 128 - 24), F32)], axis=1)


IN_NAMES = ("gq", "gk", "gv", "gr", "ga", "sq", "sk", "sv", "mq", "mk", "mv", "mi", "mf", "mo", "rq", "rk", "rv", "rg",
            "gates")
IN_SIZES = (N_HEADS * GLA_DK, N_HEADS * GLA_DK, BRANCH_W, BRANCH_W, GLA_RANK, BRANCH_W, SWA_KV * HD, SWA_KV * HD,
            BRANCH_W, BRANCH_W, BRANCH_W, N_HEADS, N_HEADS, BRANCH_W, BRANCH_W, BRANCH_W, BRANCH_W, BRANCH_W,
            N_BRANCH * D_MODEL)
IN_START = {name: sum(IN_SIZES[:i]) for i, name in enumerate(IN_NAMES)}
GATE_COL = IN_START["gates"]


def _mixer_rows(w_t):
    x = w_t.shape[1]
    at = lambda name, n: w_t[IN_START[name]:IN_START[name] + n]
    swa_q = at("sq", BRANCH_W).reshape(2, 2, HD, x).transpose(1, 0, 2, 3).reshape(BRANCH_W, x)
    halves = lambda name: at(name, BRANCH_W).reshape(N_HEADS, 2, HD // 2, x).transpose(1, 0, 2, 3).reshape(BRANCH_W, x)
    parts = [at("gq", W_GLA), swa_q, at("sk", 2 * SWA_KV * HD), at("mq", 3 * BRANCH_W), at("mo", BRANCH_W),
             halves("rq"), halves("rk"), at("rv", 2 * BRANCH_W), at("ga", GLA_RANK), at("mi", 2 * N_HEADS)]
    pad = W_MIX - sum(p.shape[0] for p in parts)
    return jnp.concatenate(parts + [jnp.zeros((pad, x), w_t.dtype)], axis=0)


def _prep_layer(li, w_in, b_in, gla_w_a2, gla_b_a2, gla_norm_g, swa_sinks, ml_norm_g, ret_norm_g, w_branch, w_out,
                ln1_g, ln1_b, w_router_grp, b_router_grp, w_router_exp, b_router_exp, w_exp_gate, w_exp_up,
                w_exp_down, ln2_g, ln2_b):
    row = lambda v: v.reshape(1, -1).astype(F32)
    wr = jnp.concatenate([w_router_grp[li], w_router_exp[li],
                          jnp.zeros((D_MODEL, 128 - N_GROUPS - N_EXPERTS), F32)], axis=1)
    wr_hi = wr.astype(BF16)
    wb = w_branch[li]
    wb1 = jnp.concatenate([wb[1, 0:64], wb[1, 128:192], wb[1, 64:128], wb[1, 192:256]], axis=0)
    w_t = w_in[li].T
    return dict(
        wm=_mixer_rows(w_t).astype(BF16),
        bm=row(_mixer_rows(b_in[li][:, None])),
        wg=w_t[GATE_COL:].astype(BF16),
        bg=row(b_in[li][GATE_COL:]),
        wa2=jnp.concatenate([gla_w_a2[li], jnp.zeros((128 - GLA_RANK, 128), F32)], axis=0).astype(BF16),
        ba2=row(gla_b_a2[li]),
        gla_g=row(gla_norm_g[li]), ml_g=row(ml_norm_g[li]), ret_g=row(ret_norm_g[li]),
        sinks=row(jnp.concatenate([swa_sinks[li], jnp.zeros((128 - N_HEADS,), F32)])),
        wbr=jnp.stack([wb[0], wb1, wb[2], wb[3]]).astype(BF16),
        wout=w_out[li].astype(BF16),
        ln1_g=row(ln1_g[li]), ln1_b=row(ln1_b[li]), ln2_g=row(ln2_g[li]), ln2_b=row(ln2_b[li]),
        wr_hi=wr_hi, wr_lo=(wr - wr_hi.astype(F32)).astype(BF16),
        br=row(jnp.concatenate([b_router_grp[li], b_router_exp[li], jnp.zeros((128 - N_GROUPS - N_EXPERTS,), F32)])),
        layer=li, we_gate=w_exp_gate, we_up=w_exp_up, we_down=w_exp_down,
    )


TOK_TILE = 256
MOE_PIECES_PROMPT = 64
MOE_PIECES_SAMPLE = 1


MERGE_TILES = 2


def _ffn_half(x2, ys, lw, tm, bp):
    x1, route, cnt, xs = _merge(x2, ys, lw, tm, MERGE_TILES if x2.shape[0] // tm % MERGE_TILES == 0 else 1)
    return _moe(x1, route, cnt, xs, lw, bp, tm)


def _layer_prompt(x2, lw, bsz, seq):
    gla, swa, ml, ret, small = _inproj(x2, lw["wm"], lw["bm"], 2 * TOK_TILE)
    ya, s_gla = _gla_prompt(gla, small, lw, bsz, seq)
    yb = _swa_prompt(swa, lw, bsz, seq)
    yc, c_bd, n_row, m_row = _ml_prompt(ml, small, lw, bsz, seq)
    yd, s_ret = _ret_prompt(ret, lw, bsz, seq)
    kv_tail = swa.reshape(bsz, seq, W_SWA)[:, seq - WINDOW:, 256:512]
    states = (_gla_state_from_blockdiag(s_gla),
              kv_tail[:, :, 0:128].reshape(bsz, WINDOW, SWA_KV, HD),
              kv_tail[:, :, 128:256].reshape(bsz, WINDOW, SWA_KV, HD),
              jnp.swapaxes(_diag_blocks(c_bd), -1, -2), n_row.reshape(bsz, N_HEADS, HD), m_row[:, 0, 0:N_HEADS],
              _ret_state_from_blockdiag(s_ret))
    return _ffn_half(x2, (ya, yb, yc, yd), lw, TOK_TILE, MOE_PIECES_PROMPT), states


def _layer_sample(x2, st, lw):
    s_gla, k_buf, v_buf, c_ml, n_ml, m_ml, s_ret = st
    bsz = x2.shape[0]
    depth = s_gla.shape[0]
    gla, swa, ml, ret, small = _inproj(x2, lw["wm"], lw["bm"], bsz)
    to_lanes = lambda s: jnp.moveaxis(s, 1, -1).reshape(depth, -1, bsz)
    from_lanes = lambda s, like: jnp.moveaxis(s.reshape(like.shape[2:] + (bsz,)), -1, 0)
    cache_t = lambda c: jnp.swapaxes(c.reshape(depth, bsz, WINDOW, SWA_KV * HD), 2, 3)
    ya, s_gla_new = _gla_sample(gla, small, to_lanes(s_gla), lw)
    yb, k_new, v_new = _swa_sample(swa, cache_t(k_buf), cache_t(v_buf), lw)
    k_new, v_new = jnp.swapaxes(k_new, 1, 2), jnp.swapaxes(v_new, 1, 2)
    yc, c_new, n_new, m_new = _ml_sample(ml, _ml_scalars(small, m_ml[lw["layer"]]), to_lanes(n_ml), to_lanes(c_ml), lw)
    yd, s_ret_new = _ret_sample(ret, to_lanes(s_ret), lw)
    states = (from_lanes(s_gla_new, s_gla), k_new.reshape(k_buf.shape[1:]), v_new.reshape(v_buf.shape[1:]),
              from_lanes(c_new, c_ml), from_lanes(n_new, n_ml), m_new[:, 0:N_HEADS], from_lanes(s_ret_new, s_ret))
    return _ffn_half(x2, (ya, yb, yc, yd), lw, bsz, MOE_PIECES_SAMPLE), states


def kernel(x_prompt, x_sample, state_gla, cache_swa_k, cache_swa_v, state_mlstm_c, state_mlstm_n, state_mlstm_m,
           state_ret, w_in, b_in, gla_w_a2, gla_b_a2, gla_norm_g, swa_sinks, ml_norm_g, ret_norm_g, w_branch, w_out,
           ln1_g, ln1_b, w_router_grp, b_router_grp, w_router_exp, b_router_exp, w_exp_gate, w_exp_up, w_exp_down,
           ln2_g, ln2_b):
    bsz, seq, _ = x_prompt.shape
    dec_b, dec_seq, _ = x_sample.shape
    assert dec_seq == 1 and dec_b == LANES and (bsz * seq) % (2 * TOK_TILE) == 0
    assert all(seq % n == 0 for n in (GLA_STEP_TILES * GLA_TILE, SWA_BLOCKS * WINDOW, STEP_CHUNKS * CHUNK))
    weights = (w_in, b_in, gla_w_a2, gla_b_a2, gla_norm_g, swa_sinks, ml_norm_g, ret_norm_g, w_branch, w_out,
               ln1_g, ln1_b, w_router_grp, b_router_grp, w_router_exp, b_router_exp, w_exp_gate, w_exp_up,
               w_exp_down, ln2_g, ln2_b)
    y_p = x_prompt.reshape(bsz * seq, D_MODEL)
    y_s = x_sample.reshape(dec_b, D_MODEL)
    new_p, new_s = [], []
    for li in range(w_in.shape[0]):
        lw = _prep_layer(li, *weights)
        y_p, st_p = _layer_prompt(y_p, lw, bsz, seq)
        st_in = (state_gla, cache_swa_k, cache_swa_v, state_mlstm_c, state_mlstm_n, state_mlstm_m, state_ret)
        y_s, st_s = _layer_sample(y_s, st_in, lw)
        new_p.append(st_p)
        new_s.append(st_s)
    outs = [y_p.reshape(bsz, seq, D_MODEL), y_s.reshape(dec_b, dec_seq, D_MODEL)]
    for i in range(7):
        outs.append(jnp.stack([s[i] for s in new_p], axis=0))
        outs.append(jnp.stack([s[i] for s in new_s], axis=0))
    return tuple(outs)
```

```python
import functools

import jax
import jax.numpy as jnp
from jax import lax
from jax.experimental import pallas as pl
from jax.experimental.pallas import tpu as pltpu

F32 = jnp.float32
BF16 = jnp.bfloat16

D_MODEL = 1024
N_BRANCH = 4
BRANCH_W = D_MODEL // N_BRANCH
N_HEADS = 4
GLA_DK = 32
GLA_RANK = 16
GLA_TAU = 16.0
HD = 64
SWA_KV = 2
WINDOW = 128
ROPE_BASE = 10000.0
PAST_LEN = 16384
N_GROUPS = 4
EPG = 8
N_EXPERTS = N_GROUPS * EPG
D_EXPERT = D_MODEL // 4
DEPTH = 2
ALPHA = (2 * DEPTH) ** 0.25
LN_EPS = 1e-5
VMEM_LIMIT = 56 * 1024 * 1024

W_GLA, W_SWA, W_ML, W_RET, W_SMALL = 768, 512, 1024, 1024, 128
W_MIX = W_GLA + W_SWA + W_ML + W_RET + W_SMALL


def _cparams(sem):
    return pltpu.CompilerParams(dimension_semantics=sem, vmem_limit_bytes=VMEM_LIMIT)


def _dot(a, b):
    return jnp.dot(a, b, preferred_element_type=F32)


def _einsum(spec, a, b):
    return jnp.einsum(spec, a, b, preferred_element_type=F32)


def _split2(x):
    hi = x.astype(BF16)
    lo = (x - hi.astype(F32)).astype(BF16)
    return hi, lo


def _split3(x):
    hi = x.astype(BF16)
    r = x - hi.astype(F32)
    mid = r.astype(BF16)
    lo = (r - mid.astype(F32)).astype(BF16)
    return hi, mid, lo


def _dot_exact_lhs(a_exact, b):
    return sum(_dot(a_exact, p) for p in _split3(b))


def _dot_exact_rhs(a, b_exact):
    return sum(_dot(p, b_exact) for p in _split3(a))


def _layer_norm(x, g, b):
    mu = jnp.mean(x, axis=-1, keepdims=True)
    xc = x - mu
    var = jnp.mean(xc * xc, axis=-1, keepdims=True)
    return xc * lax.rsqrt(var + LN_EPS) * g + b


def _inproj_kernel(x_ref, w_ref, b_ref, gla_ref, swa_ref, ml_ref, ret_ref, sm_ref):
    xb = x_ref[...].astype(BF16)
    off = 0
    for ref, width in ((gla_ref, W_GLA), (swa_ref, W_SWA), (ml_ref, W_ML), (ret_ref, W_RET), (sm_ref, W_SMALL)):
        ref[...] = _dot_nt(xb, w_ref[off:off + width, :]) + b_ref[:, off:off + width]
        off += width


def _inproj(x2, wm, bm, tm):
    t = x2.shape[0]
    widths = (W_GLA, W_SWA, W_ML, W_RET, W_SMALL)
    return pl.pallas_call(
        _inproj_kernel,
        grid=(t // tm,),
        in_specs=[pl.BlockSpec((tm, D_MODEL), lambda i: (i, 0)),
                  pl.BlockSpec((W_MIX, D_MODEL), lambda i: (0, 0)),
                  pl.BlockSpec((1, W_MIX), lambda i: (0, 0))],
        out_specs=[pl.BlockSpec((tm, w), lambda i: (i, 0)) for w in widths],
        out_shape=[jax.ShapeDtypeStruct((t, w), F32) for w in widths],
        compiler_params=_cparams(("parallel",)),
        name="inproj",
    )(x2, wm, bm)


MERGE_COLS = 512


def _merge_kernel(x_ref, ya_ref, yb_ref, yc_ref, yd_ref, wg_ref, bg_ref, wbr_ref, wout_ref, g1_ref, b1_ref,
                  wrh_ref, wrl_ref, br_ref, x1_ref, route_ref, cnt_ref, xs_ref, *, tm, sub):
    rows_t = xs_ref.shape[0] // sub
    x1, picks = [], []
    for t in range(sub):
        rows = slice(t * tm, (t + 1) * tm)
        x = x_ref[rows, :]
        xb = x.astype(BF16)
        ybs = [y_ref[rows, :].astype(BF16) for y_ref in (ya_ref, yb_ref, yc_ref, yd_ref)]
        halves = []
        for c0 in range(0, D_MODEL, MERGE_COLS):
            acc = jnp.zeros((tm, MERGE_COLS), F32)
            for n in range(N_BRANCH):
                cols = slice(n * D_MODEL + c0, n * D_MODEL + c0 + MERGE_COLS)
                gate = 0.5 * jnp.tanh(0.5 * (_dot_nt(xb, wg_ref[cols, :]) + bg_ref[:, cols])) + 0.5
                acc = acc + gate * _dot(ybs[n], wbr_ref[n, :, c0:c0 + MERGE_COLS])
            halves.append(acc)
        merged = jnp.concatenate(halves, axis=1)
        mix = _dot(merged.astype(BF16), wout_ref[...])
        x1.append(_layer_norm(ALPHA * x + mix, g1_ref[...], b1_ref[...]))
        x1_ref[rows, :] = x1[t]
        picks.append(_route(x1[t], wrh_ref[...], wrl_ref[...], br_ref[...], tm))
    for t in range(sub):
        route, cnt, xs = _sort(x1[t], picks[t], tm, rows_t)
        route_ref[t * tm:(t + 1) * tm, :] = route
        cnt_ref[t] = cnt
        xs_ref[t * rows_t:(t + 1) * rows_t, :] = xs


def _route(x1, wr_hi, wr_lo, br, tm):
    xh, xl = _split2(x1)
    both = _dot(xh, jnp.concatenate([wr_hi, wr_lo], axis=1))
    logits = both[:, 0:128] + both[:, 128:256] + _dot(xl, wr_hi) + br
    lane = lax.broadcasted_iota(jnp.int32, (tm, 128), 1)
    lanef = lane.astype(F32)
    neg = jnp.float32(-jnp.inf)
    big = jnp.float32(1e9)
    isg = lane < N_GROUPS
    gmax = jnp.max(jnp.where(isg, logits, neg), axis=-1, keepdims=True)
    gidx = jnp.min(jnp.where(isg & (logits == gmax), lanef, big), axis=-1, keepdims=True)
    p_group = 1.0 / jnp.sum(jnp.where(isg, jnp.exp(logits - gmax), 0.0), axis=-1, keepdims=True)
    lo = N_GROUPS + EPG * gidx
    em = (lanef >= lo) & (lanef < lo + EPG)
    v1 = jnp.max(jnp.where(em, logits, neg), axis=-1, keepdims=True)
    i1 = jnp.min(jnp.where(em & (logits == v1), lanef, big), axis=-1, keepdims=True)
    em2 = em & (lanef != i1)
    v2 = jnp.max(jnp.where(em2, logits, neg), axis=-1, keepdims=True)
    i2 = jnp.min(jnp.where(em2 & (logits == v2), lanef, big), axis=-1, keepdims=True)
    e21 = jnp.exp(v2 - v1)
    gt1 = 1.0 / (1.0 + e21)
    return i1, i2, p_group * gt1, p_group * (e21 * gt1)


def _sort(x1, picks, tm, rows_t):
    i1, i2, gate1, gate2 = picks
    lane = lax.broadcasted_iota(jnp.int32, (tm, 128), 1)
    lanef = lane.astype(F32)
    sel1 = lanef == i1
    sel2 = lanef == i2
    onehot = jnp.where(sel1 | sel2, 1.0, 0.0)
    cnt = jnp.sum(onehot, axis=0, keepdims=True)
    seg = jnp.floor((cnt + (PIECE - 1)) * (1.0 / PIECE)) * PIECE
    er = lax.broadcasted_iota(jnp.int32, (128, 128), 0)
    ec = lax.broadcasted_iota(jnp.int32, (128, 128), 1)
    seg_off = _dot(jnp.broadcast_to(seg, (8, 128)).astype(BF16), jnp.where(er < ec, 1.0, 0.0).astype(BF16))[0:1]
    row = lax.broadcasted_iota(jnp.int32, (tm, tm), 0)
    col = lax.broadcasted_iota(jnp.int32, (tm, tm), 1)
    earlier = _dot(jnp.where(col < row, 1.0, 0.0).astype(BF16), onehot.astype(BF16))
    where_to = seg_off + earlier
    w1 = jnp.where(sel1, where_to, 0.0)
    w2 = jnp.where(sel2, where_to, 0.0)
    pos1 = jnp.sum(w1, axis=-1, keepdims=True)
    pos2 = jnp.sum(w2, axis=-1, keepdims=True)
    ones = jnp.ones((8, 128), BF16)
    prow1 = sum(_dot_nt(ones, p) for p in _split2(w1))[0:1]
    prow2 = sum(_dot_nt(ones, p) for p in _split2(w2))[0:1]
    r = lax.broadcasted_iota(jnp.int32, (rows_t, tm), 0).astype(F32)
    place = jnp.where((r == prow1) | (r == prow2), 1.0, 0.0).astype(BF16)
    xs = _dot(place, x1.astype(BF16)).astype(BF16)

    vals = (i1 - N_GROUPS, i2 - N_GROUPS, gate1, gate2, pos1, pos2)
    route = jnp.zeros((tm, 128), F32)
    for k, v in enumerate(vals):
        route = jnp.where(lane == k, v, route)
    return route, cnt, xs


PIECE = 16


def _tile_rows(tm):
    return -(-(2 * tm + N_EXPERTS * (PIECE - 1)) // 128) * 128


def _merge(x2, ys, lw, tm, sub):
    t = x2.shape[0]
    nt = t // tm
    rows_t = _tile_rows(tm)
    tok = lambda w: pl.BlockSpec((sub * tm, w), lambda i: (i, 0))
    full = lambda shape: pl.BlockSpec(shape, lambda i: (0,) * len(shape), pipeline_mode=pl.Buffered(1))
    return pl.pallas_call(
        functools.partial(_merge_kernel, tm=tm, sub=sub),
        grid=(nt // sub,),
        in_specs=[tok(D_MODEL), tok(BRANCH_W), tok(BRANCH_W), tok(BRANCH_W), tok(BRANCH_W),
                  full((N_BRANCH * D_MODEL, D_MODEL)), full((1, N_BRANCH * D_MODEL)),
                  full((N_BRANCH, BRANCH_W, D_MODEL)), full((D_MODEL, D_MODEL)),
                  full((1, D_MODEL)), full((1, D_MODEL)),
                  full((D_MODEL, 128)), full((D_MODEL, 128)), full((1, 128))],
        out_specs=[tok(D_MODEL), tok(128), pl.BlockSpec((sub, 1, 128), lambda i: (i, 0, 0)),
                   pl.BlockSpec((sub * rows_t, D_MODEL), lambda i: (i, 0))],
        out_shape=[jax.ShapeDtypeStruct((t, D_MODEL), F32), jax.ShapeDtypeStruct((t, 128), F32),
                   jax.ShapeDtypeStruct((nt, 1, 128), F32), jax.ShapeDtypeStruct((nt * rows_t, D_MODEL), BF16)],
        compiler_params=_cparams(("parallel",)),
        name="merge",
    )(x2, *ys, lw["wg"], lw["bg"], lw["wbr"], lw["wout"], lw["ln1_g"], lw["ln1_b"], lw["wr_hi"], lw["wr_lo"], lw["br"])


def _piece_copy(hbm, hbm_row, vmem, piece, sem, to_vmem):
    h = hbm.at[pl.ds(pl.multiple_of(hbm_row, PIECE), PIECE), :]
    v = vmem.at[pl.ds(pl.multiple_of(piece * PIECE, PIECE), PIECE), :]
    return pltpu.make_async_copy(h, v, sem) if to_vmem else pltpu.make_async_copy(v, h, sem)


EXPERT_SECTION_ROWS = 512


def _expert_kernel(be_ref, nu_ref, nv_ref, off_ref, xs_hbm, wg_ref, wu_ref, wd_ref, ys_hbm, xin, yout, sem_in, sem_out,
                   *, bp):
    del xs_hbm
    blk = pl.program_id(0)
    n_used = nu_ref[0]
    slot = blk % 2
    sec_rows = min(bp * PIECE, EXPERT_SECTION_ROWS)
    sections = bp * PIECE // sec_rows

    def pieces(b, s, buf, sem, to_vmem, start):
        n = nv_ref[b]

        def one(p, priority):
            cp = _piece_copy(ys_hbm, off_ref[b * bp + p], buf.at[s], p, sem.at[s], to_vmem)
            cp.start(priority=priority) if start else cp.wait()

        def body(i, c):
            one(2 * i, 0)

            @pl.when(2 * i + 1 < n)
            def _():
                one(2 * i + 1, 1)
            return c
        lax.fori_loop(0, (n + 1) // 2, body, 0)

    @pl.when(blk == 0)
    def _():
        xin[...] = jnp.zeros_like(xin)
        for first in range(2):
            @pl.when(first < n_used)
            def _():
                pieces(first, first, xin, sem_in, True, True)

    @pl.when(blk + 2 < n_used)
    def _():
        pieces(blk + 2, (blk + 2) % 3, xin, sem_in, True, True)

    @pl.when(blk < n_used)
    def _():
        in_slot = blk % 3
        pieces(blk, in_slot, xin, sem_in, True, False)

        @pl.when(blk >= 2)
        def _():
            pieces(blk - 2, slot, yout, sem_out, False, False)

        def mlp(lo, hi):
            xb = xin[in_slot, lo:hi]
            hid = jax.nn.silu(_dot(xb, wg_ref[...].astype(BF16))) * _dot(xb, wu_ref[...].astype(BF16))
            yout[slot, lo:hi] = _dot(hid.astype(BF16), wd_ref[...].astype(BF16)).astype(BF16)

        for sec in range(sections):
            if sec == 0:
                mlp(0, sec_rows)
            else:
                @pl.when(nv_ref[blk] * PIECE > sec * sec_rows)
                def _():
                    mlp(sec * sec_rows, (sec + 1) * sec_rows)
        pieces(blk, slot, yout, sem_out, False, True)

    @pl.when(blk == n_used - 1)
    def _():
        @pl.when(blk >= 1)
        def _():
            pieces(blk - 1, 1 - slot, yout, sem_out, False, False)
        pieces(blk, slot, yout, sem_out, False, False)


def _experts(xs, block_expert, n_used, n_valid, piece_off, lw, bp):
    n_blocks = block_expert.shape[0]
    rows = bp * PIECE
    li = lw["layer"]
    wspec = lambda shape: pl.BlockSpec((None, None) + shape, lambda b, be, nu, nv, off: (li, be[b], 0, 0))
    grid_spec = pltpu.PrefetchScalarGridSpec(
        num_scalar_prefetch=4,
        grid=(n_blocks,),
        in_specs=[pl.BlockSpec(memory_space=pl.ANY), wspec((D_MODEL, D_EXPERT)), wspec((D_MODEL, D_EXPERT)),
                  wspec((D_EXPERT, D_MODEL))],
        out_specs=pl.BlockSpec(memory_space=pl.ANY),
        scratch_shapes=[pltpu.VMEM((3, rows, D_MODEL), BF16), pltpu.VMEM((2, rows, D_MODEL), BF16),
                        pltpu.SemaphoreType.DMA((3,)), pltpu.SemaphoreType.DMA((2,))],
    )
    return pl.pallas_call(
        functools.partial(_expert_kernel, bp=bp),
        grid_spec=grid_spec,
        out_shape=jax.ShapeDtypeStruct(xs.shape, BF16),
        input_output_aliases={4: 0},
        compiler_params=_cparams(("arbitrary",)),
        name="experts",
    )(block_expert, n_used, n_valid, piece_off, xs, lw["we_gate"], lw["we_up"], lw["we_down"])


COMBINE_TILES = 4


def _combine_kernel(ys_ref, x1_ref, route_ref, g2_ref, b2_ref, out_ref, *, tm, sub):
    rows_t = ys_ref.shape[0] // sub
    r = lax.broadcasted_iota(jnp.int32, (tm, rows_t), 1).astype(F32)
    routes = [route_ref[t * tm:(t + 1) * tm, :] for t in range(sub)]
    picks = [[jnp.where(r == routes[t][:, 4 + j:5 + j], 1.0, 0.0).astype(BF16) for j in range(2)] for t in range(sub)]
    rows = [[_dot(picks[t][j], ys_ref[t * rows_t:(t + 1) * rows_t, :]) for j in range(2)] for t in range(sub)]
    for t in range(sub):
        ffn = routes[t][:, 2:3] * rows[t][0] + routes[t][:, 3:4] * rows[t][1]
        tok = slice(t * tm, (t + 1) * tm)
        out_ref[tok, :] = _layer_norm(ALPHA * x1_ref[tok, :] + ffn, g2_ref[...], b2_ref[...])


def _combine(ys, x1, route, lw, tm):
    t = x1.shape[0]
    rows_t = _tile_rows(tm)
    sub = COMBINE_TILES if (t // tm) % COMBINE_TILES == 0 else 1
    full = lambda shape: pl.BlockSpec(shape, lambda i: (0,) * len(shape))
    return pl.pallas_call(
        functools.partial(_combine_kernel, tm=tm, sub=sub),
        grid=(t // (sub * tm),),
        in_specs=[pl.BlockSpec((sub * rows_t, D_MODEL), lambda i: (i, 0)),
                  pl.BlockSpec((sub * tm, D_MODEL), lambda i: (i, 0)),
                  pl.BlockSpec((sub * tm, 128), lambda i: (i, 0)), full((1, D_MODEL)), full((1, D_MODEL))],
        out_specs=pl.BlockSpec((sub * tm, D_MODEL), lambda i: (i, 0)),
        out_shape=jax.ShapeDtypeStruct((t, D_MODEL), F32),
        compiler_params=_cparams(("parallel",)),
        name="combine",
    )(ys, x1, route, lw["ln2_g"], lw["ln2_b"])


def _piece_tables(cnt, tm, bp):
    nt = cnt.shape[0]
    rows_t = _tile_rows(tm)
    counts = cnt[:, 0, N_GROUPS:N_GROUPS + N_EXPERTS].astype(jnp.int32)
    pcs = ((counts + PIECE - 1) // PIECE).T
    seg_row = ((jnp.cumsum(pcs, axis=0) - pcs) * PIECE + jnp.arange(nt, dtype=jnp.int32)[None, :] * rows_t)
    tile_end = jnp.cumsum(pcs, axis=1)
    total = tile_end[:, -1]
    e_end = jnp.cumsum((total + bp - 1) // bp * bp)
    e_start = jnp.concatenate([jnp.zeros((1,), jnp.int32), e_end[:-1]])
    n_blocks = -(-(nt * rows_t // PIECE + N_EXPERTS * (bp - 1)) // bp)
    g = jnp.arange(n_blocks * bp, dtype=jnp.int32)[:, None]
    in_e = ((e_start[None, :] <= g) & (g < e_end[None, :])).astype(F32)
    pick = lambda table: jnp.dot(in_e, table.astype(F32), precision=lax.Precision.HIGHEST)
    local = g.astype(F32) - pick(e_start[:, None])
    ends, starts, rows = pick(tile_end), pick(tile_end - pcs), pick(seg_row)
    in_t = ((starts <= local) & (local < ends)).astype(F32)
    off = jnp.sum(in_t * (rows + (local - starts) * PIECE), axis=1).astype(jnp.int32)
    n_valid = jnp.sum(jnp.sum(in_t, axis=1).reshape(n_blocks, bp), axis=1).astype(jnp.int32)
    expert = pick(jnp.arange(N_EXPERTS, dtype=jnp.int32)[:, None])[:, 0]
    block_expert = expert.reshape(n_blocks, bp)[:, 0].astype(jnp.int32)
    n_used = (e_end[-1] // bp).astype(jnp.int32).reshape(1)
    return block_expert, n_used, n_valid, off


def _moe(x1, route, cnt, xs, lw, bp, tm):
    ys = _experts(xs, *_piece_tables(cnt, tm, bp), lw, bp)
    return _combine(ys, x1, route, lw, tm)


CHUNK = 128
STEP_CHUNKS = 16


def _dot_nt(a, b):
    return lax.dot_general(a, b, (((1,), (1,)), ((), ())), preferred_element_type=F32)


def _dot_tn(a, b):
    return lax.dot_general(a, b, (((0,), (0,)), ((), ())), preferred_element_type=F32)


def _lane_head(width, group):
    return lax.broadcasted_iota(jnp.int32, (1, width), 1) // group


def _head_norm(o, gain):
    r = lax.broadcasted_iota(jnp.int32, (BRANCH_W, BRANCH_W), 0) // HD
    c = lax.broadcasted_iota(jnp.int32, (BRANCH_W, BRANCH_W), 1) // HD
    avg = jnp.where(r == c, 1.0 / HD, 0.0).astype(BF16)
    mu = _dot_exact_rhs(o, avg)
    xc = o - mu
    var = _dot_exact_rhs(xc * xc, avg)
    return xc * lax.rsqrt(var + LN_EPS) * gain


def _stack_heads(x, head_of_lane):
    return jnp.concatenate([jnp.where(head_of_lane == h, x, 0.0) for h in range(N_HEADS)], axis=0)


def _pick_heads(stacked, rows, head_of_lane):
    out = jnp.where(head_of_lane == 0, stacked[0:rows], 0.0)
    for h in range(1, N_HEADS):
        out = jnp.where(head_of_lane == h, stacked[h * rows:(h + 1) * rows], out)
    return out


def _spread_heads(cols, head_of_lane):
    out = jnp.where(head_of_lane == 0, cols[0], 0.0)
    for h in range(1, N_HEADS):
        out = jnp.where(head_of_lane == h, cols[h], out)
    return out


SWA_BLOCKS = 32


def _swa_prompt_kernel(q_ref, kv_ref, kvp_ref, sink_ref, y_ref):
    n = pl.program_id(1)
    lo = lax.broadcasted_iota(jnp.int32, (1, 128), 1) < HD
    e = lax.broadcasted_iota(jnp.int32, (2 * WINDOW, N_HEADS * WINDOW), 0)
    col = lax.broadcasted_iota(jnp.int32, (2 * WINDOW, N_HEADS * WINDOW), 1)
    a = col % WINDOW
    band = (e > a) & (e <= a + WINDOW)
    head = lax.broadcasted_iota(jnp.int32, (1, N_HEADS * WINDOW), 1) // WINDOW
    sink = _spread_heads([sink_ref[:, h:h + 1] for h in range(N_HEADS)], head)
    blocks = range(SWA_BLOCKS)
    scores, vexts = [], []
    for i in blocks:
        rows = slice(i * WINDOW, (i + 1) * WINDOW)
        q = q_ref[rows, :]
        qa, qb = q[:, 0:128], q[:, 128:256]
        kv = kv_ref[rows, :]
        kvp = kvp_ref[...] if i == 0 else kv_ref[(i - 1) * WINDOW:i * WINDOW, :]
        kext = jnp.concatenate([kvp[:, 0:128], kv[:, 0:128]], axis=0).astype(BF16)
        vexts.append(jnp.concatenate([kvp[:, 128:256], kv[:, 128:256]], axis=0).astype(BF16))
        qs = jnp.concatenate([jnp.where(lo, qa, 0.0), jnp.where(lo, qb, 0.0),
                              jnp.where(lo, 0.0, qa), jnp.where(lo, 0.0, qb)], axis=0).astype(BF16)
        scores.append(_dot_nt(kext, qs))
    probs, inv_den = [], []
    for i in blocks:
        valid = band & ((n > 0) | (e >= WINDOW)) if i == 0 else band
        s = jnp.where(valid, scores[i] * (HD ** -0.5), -jnp.inf)
        mx = jnp.maximum(jnp.max(s, axis=0, keepdims=True), sink)
        p = jnp.exp(s - mx)
        inv_den.append(1.0 / (jnp.sum(p, axis=0, keepdims=True) + jnp.exp(sink - mx)))
        probs.append(p.astype(BF16))
    outs = [_dot_tn(vexts[i], probs[i]) for i in blocks]
    for i in blocks:
        o = outs[i] * inv_den[i]
        ya = jnp.concatenate([o[0:HD, 0:128], o[HD:128, 256:384]], axis=0).T
        yb = jnp.concatenate([o[0:HD, 128:256], o[HD:128, 384:512]], axis=0).T
        y_ref[i * WINDOW:(i + 1) * WINDOW, :] = jnp.concatenate([ya, yb], axis=1)


def _swa_prompt(swa, lw, bsz, seq):
    qt = SWA_BLOCKS * WINDOW
    nt = seq // qt
    return pl.pallas_call(
        _swa_prompt_kernel,
        grid=(bsz, nt),
        in_specs=[pl.BlockSpec((qt, 256), lambda b, n: (b * nt + n, 0)),
                  pl.BlockSpec((qt, 256), lambda b, n: (b * nt + n, 1)),
                  pl.BlockSpec((WINDOW, 256), lambda b, n: (jnp.maximum((b * nt + n) * SWA_BLOCKS - 1, 0), 1)),
                  pl.BlockSpec((1, 128), lambda b, n: (0, 0))],
        out_specs=pl.BlockSpec((qt, BRANCH_W), lambda b, n: (b * nt + n, 0)),
        out_shape=jax.ShapeDtypeStruct((bsz * seq, BRANCH_W), F32),
        compiler_params=_cparams(("parallel", "parallel")),
        name="swa_prompt",
    )(swa, swa, swa, lw["sinks"])


def _rotate(x, cos, sin):
    x1, x2 = x[:, 0:128], x[:, 128:256]
    return jnp.concatenate([x1 * cos - x2 * sin, x1 * sin + x2 * cos], axis=1)


def _ret_prompt_kernel(r_ref, cos_ref, sin_ref, dmat_ref, qdec_ref, kdec_ref, cdec_ref, g_ref, y_ref, s_out_ref, s_ref):
    n = pl.program_id(1)

    @pl.when(n == 0)
    def _():
        s_ref[...] = jnp.zeros_like(s_ref)

    c = CHUNK
    chunks = range(STEP_CHUNKS)
    hq = _lane_head(BRANCH_W, 32) % N_HEADS
    hv = _lane_head(BRANCH_W, HD)
    rows_head = (lax.broadcasted_iota(jnp.int32, (BRANCH_W, 1), 0) // 32) % N_HEADS
    qr, kr, v = [], [], []
    for i in chunks:
        rows = slice(i * c, (i + 1) * c)
        cos, sin = cos_ref[rows, :], sin_ref[rows, :]
        qr.append(_rotate(r_ref[rows, 0:256], cos, sin))
        kr.append(_rotate(r_ref[rows, 256:512], cos, sin) * (HD ** -0.5))
        v.append(r_ref[rows, 512:768].astype(BF16))
    att = [_dot_nt(_stack_heads(qr[i], hq).astype(BF16), kr[i].astype(BF16)) * dmat_ref[...] for i in chunks]
    upd = [_dot_tn((kr[i] * kdec_ref[...]).astype(BF16), v[i]) for i in chunks]
    states = [s_ref[...]]
    for i in chunks:
        states.append(cdec_ref[...] * states[i] + jnp.where(rows_head == hv, upd[i], 0.0))
    o = [_pick_heads(_dot(att[i].astype(BF16), v[i]), c, hv) for i in chunks]
    o = [o[i] + _dot((qr[i] * qdec_ref[...]).astype(BF16), states[i].astype(BF16)) for i in chunks]
    s_ref[...] = states[-1]
    s_out_ref[...] = states[-1]
    y_ref[...] = jax.nn.silu(r_ref[:, 768:1024]) * _head_norm(jnp.concatenate(o, axis=0), g_ref[...])


def _ret_tables(seq):
    c = CHUNK
    half = HD // 2
    freq = ROPE_BASE ** (-jnp.arange(half, dtype=F32) / half)
    ang = jnp.arange(seq).astype(F32)[:, None] * freq[None, :]
    cos = jnp.tile(jnp.cos(ang), (1, N_HEADS))
    sin = jnp.tile(jnp.sin(ang), (1, N_HEADS))
    log_g = jnp.log1p(-jnp.exp2(-5.0 - jnp.arange(N_HEADS, dtype=F32)))
    idx = jnp.arange(c, dtype=F32)
    rel = idx[:, None] - idx[None, :]
    dmat = jnp.exp(jnp.where(rel[None] >= 0, rel[None] * log_g[:, None, None], -jnp.inf)).reshape(N_HEADS * c, c)
    lane_head_q = (jnp.arange(BRANCH_W) // 32) % N_HEADS
    qdec = jnp.exp((idx + 1.0)[:, None] * log_g[lane_head_q][None, :])
    kdec = jnp.exp((c - 1.0 - idx)[:, None] * log_g[lane_head_q][None, :])
    cdec = jnp.exp(c * log_g)[jnp.arange(BRANCH_W) // HD][None, :]
    return cos, sin, dmat, qdec, kdec, cdec


def _ret_prompt(ret, lw, bsz, seq):
    c = CHUNK
    rows = STEP_CHUNKS * c
    nc = seq // rows
    cos, sin, dmat, qdec, kdec, cdec = _ret_tables(seq)
    full = lambda shape: pl.BlockSpec(shape, lambda b, n: (0,) * len(shape))
    return pl.pallas_call(
        _ret_prompt_kernel,
        grid=(bsz, nc),
        in_specs=[pl.BlockSpec((rows, W_RET), lambda b, n: (b * nc + n, 0)),
                  pl.BlockSpec((rows, 128), lambda b, n: (n, 0)), pl.BlockSpec((rows, 128), lambda b, n: (n, 0)),
                  full((N_HEADS * c, c)), full((c, BRANCH_W)), full((c, BRANCH_W)), full((1, BRANCH_W)),
                  full((1, BRANCH_W))],
        out_specs=[pl.BlockSpec((rows, BRANCH_W), lambda b, n: (b * nc + n, 0)),
                   pl.BlockSpec((None, BRANCH_W, BRANCH_W), lambda b, n: (b, 0, 0))],
        out_shape=[jax.ShapeDtypeStruct((bsz * seq, BRANCH_W), F32),
                   jax.ShapeDtypeStruct((bsz, BRANCH_W, BRANCH_W), F32)],
        scratch_shapes=[pltpu.VMEM((BRANCH_W, BRANCH_W), F32)],
        compiler_params=_cparams(("parallel", "arbitrary")),
        name="ret_prompt",
    )(ret, cos, sin, dmat, qdec, kdec, cdec, lw["ret_g"])


def _ret_state_from_blockdiag(s_bd):
    b = s_bd.shape[0]
    s = s_bd.reshape(b, 2, N_HEADS, 32, N_HEADS, HD)
    s = jnp.stack([s[:, :, h, :, h, :] for h in range(N_HEADS)], axis=1)
    return s.reshape(b, N_HEADS, HD, HD)


ML_I_LANE = GLA_RANK
ML_F_LANE = GLA_RANK + N_HEADS


def _ml_prompt_kernel(m_ref, sm_ref, g_ref, y_ref, c_out_ref, n_out_ref, m_out_ref, c_ref, n_ref, mm_ref):
    nchunk = pl.program_id(1)

    @pl.when(nchunk == 0)
    def _():
        c_ref[...] = jnp.zeros_like(c_ref)
        n_ref[...] = jnp.zeros_like(n_ref)
        mm_ref[...] = jnp.zeros_like(mm_ref)

    c = CHUNK
    chunks = range(STEP_CHUNKS)
    row = lax.broadcasted_iota(jnp.int32, (c, c), 0)
    col = lax.broadcasted_iota(jnp.int32, (c, c), 1)
    tri_lo = jnp.where(col <= row, 1.0, 0.0).astype(BF16)
    tri_up = jnp.where(row <= col, 1.0, 0.0).astype(BF16)
    hv = _lane_head(BRANCH_W, HD)
    hq = _lane_head(N_HEADS * c, c)
    key = lax.broadcasted_iota(jnp.int32, (c, N_HEADS * c), 0)
    qry = lax.broadcasted_iota(jnp.int32, (c, N_HEADS * c), 1) % c
    causal_t = key <= qry
    r = lax.broadcasted_iota(jnp.int32, (BRANCH_W, BRANCH_W), 0) // HD
    cc = lax.broadcasted_iota(jnp.int32, (BRANCH_W, BRANCH_W), 1) // HD
    same_head = r == cc
    lane = lax.broadcasted_iota(jnp.int32, (1, 128), 1)
    gate_lane = lax.broadcasted_iota(jnp.int32, (128, 1), 0)
    is_i = (gate_lane >= ML_I_LANE) & (gate_lane < ML_I_LANE + N_HEADS)
    is_f = (gate_lane >= ML_F_LANE) & (gate_lane < ML_F_LANE + N_HEADS)
    gate_head = jnp.where(is_i, gate_lane - ML_I_LANE, gate_lane - ML_F_LANE)
    sel_q = jnp.where((is_i | is_f) & (gate_head == hq), 1.0, 0.0).astype(BF16)
    sel_v = jnp.where((is_i | is_f) & (gate_head == hv), 1.0, 0.0).astype(BF16)
    sel_qv = jnp.concatenate([sel_q, sel_v], axis=1)
    lane128 = lax.broadcasted_iota(jnp.int32, (1, 128), 1)
    in_i = (lane128 >= ML_I_LANE) & (lane128 < ML_I_LANE + N_HEADS)
    in_f = (lane128 >= ML_F_LANE) & (lane128 < ML_F_LANE + N_HEADS)
    head_rows = lax.broadcasted_iota(jnp.int32, (8, 1), 0)

    q, ks, vt, brow, u_q, u_v, sc = [], [], [], [], [], [], []
    for i in chunks:
        rows = slice(i * c, (i + 1) * c)
        q.append(m_ref[rows, 0:256])
        ks.append(m_ref[rows, 256:512] * (HD ** -0.5))
        vt.append(m_ref[rows, 512:768].T.astype(BF16))
        sm = sm_ref[rows, :]
        bcols = _dot_exact_lhs(tri_lo, jax.nn.log_sigmoid(sm))
        brows = _dot_exact_rhs(jax.nn.log_sigmoid(sm.T), tri_up)
        brow.append(jnp.concatenate([brows[ML_F_LANE + h:ML_F_LANE + h + 1, :] for h in range(N_HEADS)], axis=1))
        z = jnp.where(in_i, sm, 0.0) - jnp.where(in_f, bcols, 0.0)
        u = _dot_exact_rhs(z, sel_qv)
        u_q.append(u[:, 0:N_HEADS * c])
        u_v.append(u[:, N_HEADS * c:])
        sc.append(_dot_nt(ks[i].astype(BF16), _stack_heads(q[i], hv).astype(BF16)))

    mm = mm_ref[...]
    qk, w_inter, rowsum, floor, khat, carry_b = [], [], [], [], [], []
    for i in chunks:
        m_prev = _spread_heads([mm[:, h:h + 1] for h in range(N_HEADS)], hq)
        d_log = jnp.where(causal_t, brow[i] + u_q[i], -jnp.inf)
        inter_log = brow[i] + m_prev
        m_t = jnp.maximum(inter_log, jnp.max(d_log, axis=0, keepdims=True))
        qk_i = sc[i] * jnp.exp(d_log - m_t)
        qk.append(qk_i.astype(BF16))
        rowsum.append(jnp.sum(qk_i, axis=0, keepdims=True))
        w_inter.append(jnp.exp(inter_log - m_t))
        floor.append(jnp.exp(-m_t))
        m_new = [m_t[:, (h + 1) * c - 1:(h + 1) * c] for h in range(N_HEADS)]
        b_last = [brow[i][:, (h + 1) * c - 1:(h + 1) * c] for h in range(N_HEADS)]
        carry_b.append(_spread_heads([jnp.exp(b_last[h] + mm[:, h:h + 1] - m_new[h]) for h in range(N_HEADS)], hv))
        khat.append(ks[i] * jnp.exp(u_v[i] + _spread_heads([b_last[h] - m_new[h] for h in range(N_HEADS)], hv)))
        mm = jnp.zeros((1, 128), F32)
        for h in range(N_HEADS):
            mm = jnp.where(lane == h, m_new[h], mm)

    intra = [_dot(vt[i], qk[i]) for i in chunks]
    upd = [_dot(vt[i], khat[i].astype(BF16)) for i in chunks]
    cms, nrows = [c_ref[...]], [n_ref[...]]
    for i in chunks:
        cms.append(carry_b[i] * cms[i] + jnp.where(same_head, upd[i], 0.0))
        nrows.append(carry_b[i] * nrows[i] + jnp.sum(khat[i], axis=0, keepdims=True))
    hout = []
    for i in chunks:
        qb = q[i].astype(BF16)
        inter = _dot_nt(cms[i].astype(BF16), qb)
        n_sel = jnp.where(head_rows == hv, nrows[i], 0.0)
        qn = sum(_dot_nt(p, qb) for p in _split3(n_sel))
        blocks = []
        for h in range(N_HEADS):
            cols = slice(h * c, (h + 1) * c)
            den = w_inter[i][:, cols] * qn[h:h + 1, :] + rowsum[i][:, cols]
            num = w_inter[i][:, cols] * inter[h * HD:(h + 1) * HD, :] + intra[i][h * HD:(h + 1) * HD, cols]
            blocks.append(num / jnp.maximum(jnp.abs(den), floor[i][:, cols]))
        hout.append(jnp.concatenate(blocks, axis=0).T)
    y_ref[...] = _head_norm(jax.nn.sigmoid(m_ref[:, 768:1024]) * jnp.concatenate(hout, axis=0), g_ref[...])

    c_ref[...] = cms[-1]
    n_ref[...] = nrows[-1]
    mm_ref[...] = mm
    c_out_ref[...] = cms[-1]
    n_out_ref[...] = nrows[-1]
    m_out_ref[...] = mm


def _ml_prompt(ml, small, lw, bsz, seq):
    rows = STEP_CHUNKS * CHUNK
    nc = seq // rows
    return pl.pallas_call(
        _ml_prompt_kernel,
        grid=(bsz, nc),
        in_specs=[pl.BlockSpec((rows, W_ML), lambda b, n: (b * nc + n, 0)),
                  pl.BlockSpec((rows, W_SMALL), lambda b, n: (b * nc + n, 0)),
                  pl.BlockSpec((1, BRANCH_W), lambda b, n: (0, 0))],
        out_specs=[pl.BlockSpec((rows, BRANCH_W), lambda b, n: (b * nc + n, 0)),
                   pl.BlockSpec((None, BRANCH_W, BRANCH_W), lambda b, n: (b, 0, 0)),
                   pl.BlockSpec((None, 1, BRANCH_W), lambda b, n: (b, 0, 0)),
                   pl.BlockSpec((None, 1, 128), lambda b, n: (b, 0, 0))],
        out_shape=[jax.ShapeDtypeStruct((bsz * seq, BRANCH_W), F32),
                   jax.ShapeDtypeStruct((bsz, BRANCH_W, BRANCH_W), F32),
                   jax.ShapeDtypeStruct((bsz, 1, BRANCH_W), F32),
                   jax.ShapeDtypeStruct((bsz, 1, 128), F32)],
        scratch_shapes=[pltpu.VMEM((BRANCH_W, BRANCH_W), F32), pltpu.VMEM((1, BRANCH_W), F32),
                        pltpu.VMEM((1, 128), F32)],
        compiler_params=_cparams(("parallel", "arbitrary")),
        name="ml_prompt",
    )(ml, small, lw["ml_g"])


def _diag_blocks(s_bd):
    b = s_bd.shape[0]
    s = s_bd.reshape(b, N_HEADS, HD, N_HEADS, HD)
    return jnp.stack([s[:, h, :, h, :] for h in range(N_HEADS)], axis=1)


GLA_TILE = 256
GLA_STEP_TILES = 8
GLA_SUB = 16


def _gla_prompt_kernel(x_ref, sm_ref, wa2_ref, ba2_ref, g_ref, y_ref, s_out_ref, s_ref):
    n = pl.program_id(1)

    @pl.when(n == 0)
    def _():
        s_ref[...] = jnp.zeros_like(s_ref)

    tq, sc = GLA_TILE, GLA_SUB
    r = lax.broadcasted_iota(jnp.int32, (tq, tq), 0)
    c = lax.broadcasted_iota(jnp.int32, (tq, tq), 1)
    same = (r // sc) == (c // sc)
    cum_in_sub = jnp.where(same & (c <= r), 1.0, 0.0).astype(BF16)
    sum_in_sub = jnp.where(same, 1.0, 0.0).astype(BF16)
    kh = lax.broadcasted_iota(jnp.int32, (128, 1), 0) // GLA_DK
    hv = _lane_head(BRANCH_W, HD)
    expand = jnp.where(kh == hv, 1.0, 0.0).astype(BF16)
    trow = lax.broadcasted_iota(jnp.int32, (sc, 1), 0)
    n_sub = tq // sc
    chunks = range(GLA_STEP_TILES * n_sub)

    q, k, v, b, qd, dec_t, upds = [], [], [], [], [], [], []
    for t in range(GLA_STEP_TILES):
        rows = slice(t * tq, (t + 1) * tq)
        q.append(x_ref[rows, 0:128] * (GLA_DK ** -0.5))
        k.append(x_ref[rows, 128:256])
        v.append(x_ref[rows, 256:512])
        log_a = jax.nn.log_sigmoid(_dot(sm_ref[rows, :].astype(BF16), wa2_ref[...]) + ba2_ref[...]) / GLA_TAU
        b.append(_dot_exact_lhs(cum_in_sub, log_a))
        btot = _dot_exact_lhs(sum_in_sub, log_a)
        qd.append((q[t] * jnp.exp(b[t])).astype(BF16))
        dec_t.append(jnp.exp(btot).T)
        khat_b = (k[t] * jnp.exp(btot - b[t])).astype(BF16)
        vb = v[t].astype(BF16)
        upds += [_dot_tn(khat_b[ci * sc:(ci + 1) * sc], vb[ci * sc:(ci + 1) * sc]) for ci in range(n_sub)]
    states = [s_ref[...]]
    for g in chunks:
        t, lo = g // n_sub, (g % n_sub) * sc
        states.append(dec_t[t][:, lo:lo + 1] * states[g] + jnp.where(kh == hv, upds[g], 0.0))
    outs = []
    for g in chunks:
        t, lo = g // n_sub, (g % n_sub) * sc
        qc, kc, bc, vc = q[t][lo:lo + sc], k[t][lo:lo + sc], b[t][lo:lo + sc], v[t][lo:lo + sc]
        pair = [qc * kc[j:j + 1] * jnp.exp(jnp.where(trow >= j, bc - bc[j:j + 1], -jnp.inf)) for j in range(sc)]
        att = _dot(jnp.concatenate(pair, axis=0).astype(BF16), expand)
        o = att[0:sc] * vc[0:1]
        for j in range(1, sc):
            o = o + att[j * sc:(j + 1) * sc] * vc[j:j + 1]
        outs.append(o)
    outs = [outs[g] + _dot(qd[g // n_sub][(g % n_sub) * sc:(g % n_sub + 1) * sc], states[g].astype(BF16)) for g in chunks]
    s_ref[...] = states[-1]
    s_out_ref[...] = states[-1]
    y_ref[...] = jax.nn.silu(x_ref[:, 512:768]) * _head_norm(jnp.concatenate(outs, axis=0), g_ref[...])


def _gla_prompt(gla, small, lw, bsz, seq):
    tq = GLA_STEP_TILES * GLA_TILE
    nt = seq // tq
    full = lambda shape: pl.BlockSpec(shape, lambda b, n: (0,) * len(shape))
    return pl.pallas_call(
        _gla_prompt_kernel,
        grid=(bsz, nt),
        in_specs=[pl.BlockSpec((tq, W_GLA), lambda b, n: (b * nt + n, 0)),
                  pl.BlockSpec((tq, W_SMALL), lambda b, n: (b * nt + n, 0)),
                  full((128, 128)), full((1, 128)), full((1, BRANCH_W))],
        out_specs=[pl.BlockSpec((tq, BRANCH_W), lambda b, n: (b * nt + n, 0)),
                   pl.BlockSpec((None, 128, BRANCH_W), lambda b, n: (b, 0, 0))],
        out_shape=[jax.ShapeDtypeStruct((bsz * seq, BRANCH_W), F32),
                   jax.ShapeDtypeStruct((bsz, 128, BRANCH_W), F32)],
        scratch_shapes=[pltpu.VMEM((128, BRANCH_W), F32)],
        compiler_params=_cparams(("parallel", "arbitrary")),
        name="gla_prompt",
    )(gla, small, lw["wa2"], lw["ba2"], lw["gla_g"])


def _gla_state_from_blockdiag(s_bd):
    b = s_bd.shape[0]
    s = s_bd.reshape(b, N_HEADS, GLA_DK, N_HEADS, HD)
    return jnp.stack([s[:, h, :, h, :] for h in range(N_HEADS)], axis=1)


LANES = 128


def _rank1_update(st_ref, snt_ref, q_ref, k_ref, decay_row, v, dk):
    def body(i, o):
        rows = pl.ds(pl.multiple_of(i * HD, HD), HD)
        new = decay_row(i) * st_ref[rows, :] + k_ref[pl.ds(i, 1), :] * v
        snt_ref[rows, :] = new
        return o + q_ref[pl.ds(i, 1), :] * new
    return lax.fori_loop(0, dk, body, jnp.zeros((HD, LANES), F32), unroll=4)


def _head_rows(h, n):
    return pl.ds(pl.multiple_of(h * n, n), n)


def _gla_sample_kernel(x_ref, sm_ref, wa2_ref, ba2_ref, g_ref, s_ref, y_ref, s_out_ref,
                       qt_ref, kt_ref, at_ref, vt_ref, ot_ref):
    h = pl.program_id(0)

    @pl.when(h == 0)
    def _():
        x = x_ref[...]
        qt_ref[...] = (x[:, 0:128] * (GLA_DK ** -0.5)).T
        kt_ref[...] = x[:, 128:256].T
        vt_ref[...] = x[:, 256:512].T
        log_a = jax.nn.log_sigmoid(_dot(sm_ref[...].astype(BF16), wa2_ref[...]) + ba2_ref[...]) / GLA_TAU
        at_ref[...] = jnp.exp(log_a).T

    base = h * GLA_DK
    o = _rank1_update(s_ref, s_out_ref, qt_ref.at[_head_rows(h, GLA_DK)], kt_ref.at[_head_rows(h, GLA_DK)],
                      lambda i: at_ref[pl.ds(base + i, 1), :], vt_ref[_head_rows(h, HD), :], GLA_DK)
    ot_ref[_head_rows(h, HD), :] = o

    @pl.when(h == N_HEADS - 1)
    def _():
        y_ref[...] = jax.nn.silu(x_ref[:, 512:768]) * _head_norm(ot_ref[...].T, g_ref[...])


def _gla_sample(gla, small, state, lw):
    per_head = GLA_DK * HD
    li = lw["layer"]
    full = lambda shape: pl.BlockSpec(shape, lambda h: (0,) * len(shape))
    vm = lambda r: pltpu.VMEM((r, LANES), F32)
    return pl.pallas_call(
        _gla_sample_kernel,
        grid=(N_HEADS,),
        in_specs=[full((LANES, W_GLA)), full((LANES, W_SMALL)), full((128, 128)), full((1, 128)), full((1, BRANCH_W)),
                  pl.BlockSpec((None, per_head, LANES), lambda h: (li, h, 0))],
        out_specs=[full((LANES, BRANCH_W)), pl.BlockSpec((per_head, LANES), lambda h: (h, 0))],
        out_shape=[jax.ShapeDtypeStruct((LANES, BRANCH_W), F32), jax.ShapeDtypeStruct(state.shape[1:], F32)],
        scratch_shapes=[vm(128), vm(128), vm(128), vm(256), vm(256)],
        compiler_params=_cparams(("arbitrary",)),
        name="gla_sample",
    )(gla, small, lw["wa2"], lw["ba2"], lw["gla_g"], state)


def _ml_sample_kernel(x_ref, sc_ref, n_ref, g_ref, c_ref, y_ref, c_out_ref, n_out_ref, m_out_ref,
                      qt_ref, kt_ref, vt_ref, ot_ref, w_ref, kp_ref):
    h = pl.program_id(0)

    @pl.when(h == 0)
    def _():
        x = x_ref[...]
        qt_ref[...] = x[:, 0:256].T
        kt_ref[...] = (x[:, 256:512] * (HD ** -0.5)).T
        vt_ref[...] = x[:, 512:768].T
        sct = sc_ref[...].T
        log_f = jax.nn.log_sigmoid(sct[8:16])
        m_old = sct[16:24]
        m_new = jnp.maximum(log_f + m_old, sct[0:8])
        w_ref[0:8, :] = jnp.exp(log_f + m_old - m_new)
        w_ref[8:16, :] = jnp.exp(sct[0:8] - m_new)
        w_ref[16:24, :] = m_new

    w_carry = w_ref[pl.ds(h, 1), :]
    w_tok = w_ref[pl.ds(8 + h, 1), :]
    m_new = w_ref[pl.ds(16 + h, 1), :]
    kp_ref[...] = kt_ref[_head_rows(h, HD), :] * w_tok
    q_h = qt_ref.at[_head_rows(h, HD)]
    n_new = w_carry * n_ref[_head_rows(h, HD), :] + kp_ref[...]
    num = _rank1_update(c_ref, c_out_ref, q_h, kp_ref, lambda i: w_carry, vt_ref[_head_rows(h, HD), :], HD)
    den = jnp.sum(q_h[...] * n_new, axis=0, keepdims=True)
    ot_ref[_head_rows(h, HD), :] = num / jnp.maximum(jnp.abs(den), jnp.exp(-m_new))
    n_out_ref[_head_rows(h, HD), :] = n_new

    @pl.when(h == N_HEADS - 1)
    def _():
        y_ref[...] = _head_norm(jax.nn.sigmoid(x_ref[:, 768:1024]) * ot_ref[...].T, g_ref[...])
        m_out_ref[...] = jnp.concatenate([w_ref[16:24, :], jnp.zeros((LANES - 8, LANES), F32)], axis=0).T


def _ml_sample(ml, scal, n_state, c_state, lw):
    per_head = HD * HD
    li = lw["layer"]
    full = lambda shape: pl.BlockSpec(shape, lambda h: (0,) * len(shape))
    vm = lambda r: pltpu.VMEM((r, LANES), F32)
    return pl.pallas_call(
        _ml_sample_kernel,
        grid=(N_HEADS,),
        in_specs=[full((LANES, W_ML)), full((LANES, 128)),
                  pl.BlockSpec((None, BRANCH_W, LANES), lambda h: (li, 0, 0)), full((1, BRANCH_W)),
                  pl.BlockSpec((None, per_head, LANES), lambda h: (li, h, 0))],
        out_specs=[full((LANES, BRANCH_W)), pl.BlockSpec((per_head, LANES), lambda h: (h, 0)),
                   full((BRANCH_W, LANES)), full((LANES, 128))],
        out_shape=[jax.ShapeDtypeStruct((LANES, BRANCH_W), F32), jax.ShapeDtypeStruct(c_state.shape[1:], F32),
                   jax.ShapeDtypeStruct((BRANCH_W, LANES), F32), jax.ShapeDtypeStruct((LANES, 128), F32)],
        scratch_shapes=[vm(256), vm(256), vm(256), vm(256), vm(24), vm(HD)],
        compiler_params=_cparams(("arbitrary",)),
        name="ml_sample",
    )(ml, scal, n_state, lw["ml_g"], c_state)


def _ret_sample_kernel(x_ref, cos_ref, sin_ref, gam_ref, g_ref, s_ref, y_ref, s_out_ref,
                       qt_ref, kt_ref, vt_ref, ot_ref, qr_ref, kr_ref):
    h = pl.program_id(0)

    @pl.when(h == 0)
    def _():
        x = x_ref[...]
        qt_ref[...] = x[:, 0:256].T
        kt_ref[...] = (x[:, 256:512] * (HD ** -0.5)).T
        vt_ref[...] = x[:, 512:768].T

    cos, sin = cos_ref[...], sin_ref[...]
    for src, dst in ((qt_ref, qr_ref), (kt_ref, kr_ref)):
        x1 = src[_head_rows(h, 32), :]
        x2 = src[pl.ds(pl.multiple_of(128 + h * 32, 32), 32), :]
        dst[0:32, :] = x1 * cos - x2 * sin
        dst[32:64, :] = x1 * sin + x2 * cos
    gamma = gam_ref[pl.ds(h, 1), :]
    ot_ref[_head_rows(h, HD), :] = _rank1_update(s_ref, s_out_ref, qr_ref, kr_ref, lambda i: gamma,
                                                 vt_ref[_head_rows(h, HD), :], HD)

    @pl.when(h == N_HEADS - 1)
    def _():
        y_ref[...] = jax.nn.silu(x_ref[:, 768:1024]) * _head_norm(ot_ref[...].T, g_ref[...])


def _ret_sample(ret, state, lw):
    per_head = HD * HD
    li = lw["layer"]
    half = HD // 2
    freq = ROPE_BASE ** (-jnp.arange(half, dtype=F32) / half)
    ang = jnp.full((1,), PAST_LEN, F32)[:, None] * freq[None, :]
    cos = jnp.broadcast_to(jnp.cos(ang).reshape(half, 1), (half, LANES))
    sin = jnp.broadcast_to(jnp.sin(ang).reshape(half, 1), (half, LANES))
    log_g = jnp.log1p(-jnp.exp2(-5.0 - jnp.arange(N_HEADS, dtype=F32)))
    gam = jnp.broadcast_to(jnp.concatenate([jnp.exp(log_g), jnp.ones((4,), F32)]).reshape(8, 1), (8, LANES))
    full = lambda shape: pl.BlockSpec(shape, lambda h: (0,) * len(shape))
    vm = lambda r: pltpu.VMEM((r, LANES), F32)
    return pl.pallas_call(
        _ret_sample_kernel,
        grid=(N_HEADS,),
        in_specs=[full((LANES, W_RET)), full((half, LANES)), full((half, LANES)), full((8, LANES)), full((1, BRANCH_W)),
                  pl.BlockSpec((None, per_head, LANES), lambda h: (li, h, 0))],
        out_specs=[full((LANES, BRANCH_W)), pl.BlockSpec((per_head, LANES), lambda h: (h, 0))],
        out_shape=[jax.ShapeDtypeStruct((LANES, BRANCH_W), F32), jax.ShapeDtypeStruct(state.shape[1:], F32)],
        scratch_shapes=[vm(256), vm(256), vm(256), vm(256), vm(HD), vm(HD)],
        compiler_params=_cparams(("arbitrary",)),
        name="ret_sample",
    )(ret, cos, sin, gam, lw["ret_g"], state)


def _swa_sample_kernel(q_ref, kn_ref, vn_ref, knc_ref, vnc_ref, sink_ref, kc_ref, vc_ref, o_ref, ko_ref, vo_ref):
    q = q_ref[...]
    kc, vc = kc_ref[...], vc_ref[...]
    kn, vn = kn_ref[...], vn_ref[...]
    scale = HD ** -0.5
    s = _einsum("bhd,bdw->bhw", q.astype(BF16), kc.astype(BF16)) * scale
    s_new = jnp.sum(q * kn, axis=-1, keepdims=True) * scale
    w = lax.broadcasted_iota(jnp.int32, (1, 1, WINDOW), 2)
    s = jnp.where(w > 0, s, -jnp.inf)
    sink = sink_ref[...][:, 0:1]
    mx = jnp.maximum(jnp.maximum(jnp.max(s, axis=-1, keepdims=True), s_new), sink)
    p = jnp.exp(s - mx)
    p_new = jnp.exp(s_new - mx)
    den = jnp.sum(p, axis=-1, keepdims=True) + p_new + jnp.exp(sink - mx)
    o = _einsum("bhw,bdw->bhd", p.astype(BF16), vc.astype(BF16)) + p_new * vn
    o_ref[...] = o / den
    newest = w == WINDOW - 1
    ko_ref[...] = jnp.where(newest, knc_ref[...], pltpu.roll(kc, WINDOW - 1, 2))
    vo_ref[...] = jnp.where(newest, vnc_ref[...], pltpu.roll(vc, WINDOW - 1, 2))


def _swa_sample(swa, cache_k, cache_v, lw, bt=16):
    bsz = swa.shape[0]
    li = lw["layer"]
    cache = pl.BlockSpec((None, bt, WINDOW, 128), lambda i: (li, i, 0, 0))
    lo = (jnp.arange(128) < HD)[None, :]
    qa, qb = swa[:, 0:128], swa[:, 128:256]
    q = jnp.stack([jnp.where(lo, qa, 0.0), jnp.where(lo, qb, 0.0), jnp.where(lo, 0.0, qa), jnp.where(lo, 0.0, qb)], axis=1)
    kn = swa[:, None, 256:384]
    vn = swa[:, None, 384:512]
    sink = jnp.broadcast_to(lw["sinks"][0, 0:N_HEADS].reshape(N_HEADS, 1), (N_HEADS, 128))
    blk = lambda shape: pl.BlockSpec(shape, lambda i: (i,) + (0,) * (len(shape) - 1))
    o, ko, vo = pl.pallas_call(
        _swa_sample_kernel,
        grid=(bsz // bt,),
        in_specs=[blk((bt, N_HEADS, 128)), blk((bt, 1, 128)), blk((bt, 1, 128)), blk((bt, 128, 1)), blk((bt, 128, 1)),
                  pl.BlockSpec((N_HEADS, 128), lambda i: (0, 0)), cache, cache],
        out_specs=[blk((bt, N_HEADS, 128)), blk((bt, WINDOW, 128)), blk((bt, WINDOW, 128))],
        out_shape=[jax.ShapeDtypeStruct((bsz, N_HEADS, 128), F32), jax.ShapeDtypeStruct(cache_k.shape[1:], F32),
                   jax.ShapeDtypeStruct(cache_v.shape[1:], F32)],
        compiler_params=_cparams(("parallel",)),
        name="swa_sample",
    )(q, kn, vn, jnp.swapaxes(kn, 1, 2), jnp.swapaxes(vn, 1, 2), sink, cache_k, cache_v)
    y = jnp.concatenate([o[:, 0, 0:64], o[:, 2, 64:128], o[:, 1, 0:64], o[:, 3, 64:128]], axis=1)
    return y, ko, vo


def _ml_scalars(small, m_state):
    z4 = jnp.zeros((small.shape[0], 4), F32)
    return jnp.concatenate([small[:, ML_I_LANE:ML_I_LANE + 4], z4, small[:, ML_F_LANE:ML_F_LANE + 4], z4, m_state, z4,
                            jnp.zeros((small.shape[0], 128 - 24), F32)], axis=1)


IN_NAMES = ("gq", "gk", "gv", "gr", "ga", "sq", "sk", "sv", "mq", "mk", "mv", "mi", "mf", "mo", "rq", "rk", "rv", "rg",
            "gates")
IN_SIZES = (N_HEADS * GLA_DK, N_HEADS * GLA_DK, BRANCH_W, BRANCH_W, GLA_RANK, BRANCH_W, SWA_KV * HD, SWA_KV * HD,
            BRANCH_W, BRANCH_W, BRANCH_W, N_HEADS, N_HEADS, BRANCH_W, BRANCH_W, BRANCH_W, BRANCH_W, BRANCH_W,
            N_BRANCH * D_MODEL)
IN_START = {name: sum(IN_SIZES[:i]) for i, name in enumerate(IN_NAMES)}
GATE_COL = IN_START["gates"]


def _mixer_rows(w_t):
    x = w_t.shape[1]
    at = lambda name, n: w_t[IN_START[name]:IN_START[name] + n]
    swa_q = at("sq", BRANCH_W).reshape(2, 2, HD, x).transpose(1, 0, 2, 3).reshape(BRANCH_W, x)
    halves = lambda name: at(name, BRANCH_W).reshape(N_HEADS, 2, HD // 2, x).transpose(1, 0, 2, 3).reshape(BRANCH_W, x)
    parts = [at("gq", W_GLA), swa_q, at("sk", 2 * SWA_KV * HD), at("mq", 3 * BRANCH_W), at("mo", BRANCH_W),
             halves("rq"), halves("rk"), at("rv", 2 * BRANCH_W), at("ga", GLA_RANK), at("mi", 2 * N_HEADS)]
    pad = W_MIX - sum(p.shape[0] for p in parts)
    return jnp.concatenate(parts + [jnp.zeros((pad, x), w_t.dtype)], axis=0)


def _prep_layer(li, w_in, b_in, gla_w_a2, gla_b_a2, gla_norm_g, swa_sinks, ml_norm_g, ret_norm_g, w_branch, w_out,
                ln1_g, ln1_b, w_router_grp, b_router_grp, w_router_exp, b_router_exp, w_exp_gate, w_exp_up,
                w_exp_down, ln2_g, ln2_b):
    row = lambda v: v.reshape(1, -1).astype(F32)
    wr = jnp.concatenate([w_router_grp[li], w_router_exp[li],
                          jnp.zeros((D_MODEL, 128 - N_GROUPS - N_EXPERTS), F32)], axis=1)
    wr_hi = wr.astype(BF16)
    wb = w_branch[li]
    wb1 = jnp.concatenate([wb[1, 0:64], wb[1, 128:192], wb[1, 64:128], wb[1, 192:256]], axis=0)
    w_t = w_in[li].T
    return dict(
        wm=_mixer_rows(w_t).astype(BF16),
        bm=row(_mixer_rows(b_in[li][:, None])),
        wg=w_t[GATE_COL:].astype(BF16),
        bg=row(b_in[li][GATE_COL:]),
        wa2=jnp.concatenate([gla_w_a2[li], jnp.zeros((128 - GLA_RANK, 128), F32)], axis=0).astype(BF16),
        ba2=row(gla_b_a2[li]),
        gla_g=row(gla_norm_g[li]), ml_g=row(ml_norm_g[li]), ret_g=row(ret_norm_g[li]),
        sinks=row(jnp.concatenate([swa_sinks[li], jnp.zeros((128 - N_HEADS,), F32)])),
        wbr=jnp.stack([wb[0], wb1, wb[2], wb[3]]).astype(BF16),
        wout=w_out[li].astype(BF16),
        ln1_g=row(ln1_g[li]), ln1_b=row(ln1_b[li]), ln2_g=row(ln2_g[li]), ln2_b=row(ln2_b[li]),
        wr_hi=wr_hi, wr_lo=(wr - wr_hi.astype(F32)).astype(BF16),
        br=row(jnp.concatenate([b_router_grp[li], b_router_exp[li], jnp.zeros((128 - N_GROUPS - N_EXPERTS,), F32)])),
        layer=li, we_gate=w_exp_gate, we_up=w_exp_up, we_down=w_exp_down,
    )


TOK_TILE = 256
MOE_PIECES_PROMPT = 64
MOE_PIECES_SAMPLE = 1


MERGE_TILES = 2


def _ffn_half(x2, ys, lw, tm, bp):
    x1, route, cnt, xs = _merge(x2, ys, lw, tm, MERGE_TILES if x2.shape[0] // tm % MERGE_TILES == 0 else 1)
    return _moe(x1, route, cnt, xs, lw, bp, tm)


def _layer_prompt(x2, lw, bsz, seq):
    gla, swa, ml, ret, small = _inproj(x2, lw["wm"], lw["bm"], 2 * TOK_TILE)
    ya, s_gla = _gla_prompt(gla, small, lw, bsz, seq)
    yb = _swa_prompt(swa, lw, bsz, seq)
    yc, c_bd, n_row, m_row = _ml_prompt(ml, small, lw, bsz, seq)
    yd, s_ret = _ret_prompt(ret, lw, bsz, seq)
    kv_tail = swa.reshape(bsz, seq, W_SWA)[:, seq - WINDOW:, 256:512]
    states = (_gla_state_from_blockdiag(s_gla),
              kv_tail[:, :, 0:128].reshape(bsz, WINDOW, SWA_KV, HD),
              kv_tail[:, :, 128:256].reshape(bsz, WINDOW, SWA_KV, HD),
              jnp.swapaxes(_diag_blocks(c_bd), -1, -2), n_row.reshape(bsz, N_HEADS, HD), m_row[:, 0, 0:N_HEADS],
              _ret_state_from_blockdiag(s_ret))
    return _ffn_half(x2, (ya, yb, yc, yd), lw, TOK_TILE, MOE_PIECES_PROMPT), states


def _layer_sample(x2, st, lw):
    s_gla, k_buf, v_buf, c_ml, n_ml, m_ml, s_ret = st
    bsz = x2.shape[0]
    depth = s_gla.shape[0]
    gla, swa, ml, ret, small = _inproj(x2, lw["wm"], lw["bm"], bsz)
    to_lanes = lambda s: jnp.moveaxis(s, 1, -1).reshape(depth, -1, bsz)
    from_lanes = lambda s, like: jnp.moveaxis(s.reshape(like.shape[2:] + (bsz,)), -1, 0)
    cache_t = lambda c: jnp.swapaxes(c.reshape(depth, bsz, WINDOW, SWA_KV * HD), 2, 3)
    ya, s_gla_new = _gla_sample(gla, small, to_lanes(s_gla), lw)
    yb, k_new, v_new = _swa_sample(swa, cache_t(k_buf), cache_t(v_buf), lw)
    k_new, v_new = jnp.swapaxes(k_new, 1, 2), jnp.swapaxes(v_new, 1, 2)
    yc, c_new, n_new, m_new = _ml_sample(ml, _ml_scalars(small, m_ml[lw["layer"]]), to_lanes(n_ml), to_lanes(c_ml), lw)
    yd, s_ret_new = _ret_sample(ret, to_lanes(s_ret), lw)
    states = (from_lanes(s_gla_new, s_gla), k_new.reshape(k_buf.shape[1:]), v_new.reshape(v_buf.shape[1:]),
              from_lanes(c_new, c_ml), from_lanes(n_new, n_ml), m_new[:, 0:N_HEADS], from_lanes(s_ret_new, s_ret))
    return _ffn_half(x2, (ya, yb, yc, yd), lw, bsz, MOE_PIECES_SAMPLE), states


def kernel(x_prompt, x_sample, state_gla, cache_swa_k, cache_swa_v, state_mlstm_c, state_mlstm_n, state_mlstm_m,
           state_ret, w_in, b_in, gla_w_a2, gla_b_a2, gla_norm_g, swa_sinks, ml_norm_g, ret_norm_g, w_branch, w_out,
           ln1_g, ln1_b, w_router_grp, b_router_grp, w_router_exp, b_router_exp, w_exp_gate, w_exp_up, w_exp_down,
           ln2_g, ln2_b):
    bsz, seq, _ = x_prompt.shape
    dec_b, dec_seq, _ = x_sample.shape
    assert dec_seq == 1 and dec_b == LANES and (bsz * seq) % (2 * TOK_TILE) == 0
    assert all(seq % n == 0 for n in (GLA_STEP_TILES * GLA_TILE, SWA_BLOCKS * WINDOW, STEP_CHUNKS * CHUNK))
    weights = (w_in, b_in, gla_w_a2, gla_b_a2, gla_norm_g, swa_sinks, ml_norm_g, ret_norm_g, w_branch, w_out,
               ln1_g, ln1_b, w_router_grp, b_router_grp, w_router_exp, b_router_exp, w_exp_gate, w_exp_up,
               w_exp_down, ln2_g, ln2_b)
    y_p = x_prompt.reshape(bsz * seq, D_MODEL)
    y_s = x_sample.reshape(dec_b, D_MODEL)
    new_p, new_s = [], []
    for li in range(w_in.shape[0]):
        lw = _prep_layer(li, *weights)
        y_p, st_p = _layer_prompt(y_p, lw, bsz, seq)
        st_in = (state_gla, cache_swa_k, cache_swa_v, state_mlstm_c, state_mlstm_n, state_mlstm_m, state_ret)
        y_s, st_s = _layer_sample(y_s, st_in, lw)
        new_p.append(st_p)
        new_s.append(st_s)
    outs = [y_p.reshape(bsz, seq, D_MODEL), y_s.reshape(dec_b, dec_seq, D_MODEL)]
    for i in range(7):
        outs.append(jnp.stack([s[i] for s in new_p], axis=0))
        outs.append(jnp.stack([s[i] for s in new_s], axis=0))
    return tuple(outs)
```
